```python
import math
import jax, jax.numpy as jnp
from jax import lax
import numpy as np

D_MODEL = 1024
BATCH = 8
SEQ = 8192
DEPTH = 1

CHUNK = 64
CONV_K = 4
EPS = 1e-6

SSD_HEADS = 16
SSD_HEAD_DIM = 64
SSD_WIDTH = SSD_HEADS * SSD_HEAD_DIM
SSD_GROUPS = 2
SSD_STATE = 128
SSD_CONV_DIM = SSD_WIDTH + 2 * SSD_GROUPS * SSD_STATE

GDN_HEADS = 8
GDN_DK = 128
GDN_DV = 128
GDN_KW = GDN_HEADS * GDN_DK
GDN_VW = GDN_HEADS * GDN_DV
GDN_CONV_DIM = 2 * GDN_KW + GDN_VW

MIX_WIDTH = SSD_WIDTH + GDN_VW
IN_SIZES = (SSD_WIDTH, SSD_CONV_DIM, SSD_HEADS, GDN_VW, GDN_CONV_DIM, GDN_HEADS, GDN_HEADS)
IN_DIM = SSD_WIDTH + SSD_CONV_DIM + SSD_HEADS + GDN_VW + GDN_CONV_DIM + 2 * GDN_HEADS

kernel_name = "hybrid_ssd_gated_deltanet_parallel_heads"


def rmsnorm(x, w):
    xf = x.astype(jnp.float32)
    return xf * lax.rsqrt(jnp.mean(xf * xf, axis=-1, keepdims=True) + EPS) * w.astype(jnp.float32)


def l2norm(x):
    return x * lax.rsqrt(jnp.sum(x * x, axis=-1, keepdims=True) + EPS)


def causal_dwconv(x, w):
    return lax.conv_general_dilated(
        x, w[:, None, :].astype(x.dtype), window_strides=(1,), padding=[(CONV_K - 1, 0)],
        dimension_numbers=("NWC", "WIO", "NWC"), feature_group_count=x.shape[-1])


def ssd_chunked(x, dt, a_neg, bm, cm):
    b, t, h, p = x.shape
    g, n = bm.shape[-2:]
    r = h // g
    c = t // CHUNK
    xs = (x * dt[..., None]).reshape(b, c, CHUNK, g, r, p)
    la = jnp.moveaxis((dt * a_neg).reshape(b, c, CHUNK, g, r), 2, -1)
    la_cum = jnp.cumsum(la, axis=-1)
    bc = bm.reshape(b, c, CHUNK, g, n)
    cc = cm.reshape(b, c, CHUNK, g, n)
    idx = jnp.arange(CHUNK)
    incl = idx[:, None] >= idx[None, :]
    lmat = jnp.exp(jnp.where(incl, la_cum[..., :, None] - la_cum[..., None, :], -jnp.inf))
    cb = jnp.einsum("bclgn,bcsgn->bcgls", cc, bc)
    y_diag = jnp.einsum("bcgrls,bcsgrp->bclgrp", cb[:, :, :, None] * lmat, xs)
    states = jnp.einsum("bclgn,bcgrl,bclgrp->bcgrpn", bc, jnp.exp(la_cum[..., -1:] - la_cum), xs)
    chunk_decay = jnp.exp(la_cum[..., -1])

    def step(hst, inp):
        st, dec = inp
        return hst * dec[..., None, None] + st, hst

    h0 = jnp.zeros((b, g, r, p, n), jnp.float32)
    _, prev = lax.scan(step, h0, (jnp.moveaxis(states, 1, 0), jnp.moveaxis(chunk_decay, 1, 0)))
    prev = jnp.moveaxis(prev, 0, 1)
    y_off = jnp.einsum("bclgn,bcgrpn,bcgrl->bclgrp", cc, prev, jnp.exp(la_cum))
    return (y_diag + y_off).reshape(b, t, h, p)


def gated_delta_chunked(q, k, v, g, beta):
    b, t, h, dk = q.shape
    dv = v.shape[-1]
    c = t // CHUNK

    def blk(u):
        return jnp.moveaxis(u.reshape(b, c, CHUNK, h, *u.shape[3:]), 3, 1)

    q, k, v, g, beta = blk(q), blk(k), blk(v), blk(g), blk(beta)
    g_cum = jnp.cumsum(g, axis=-1)
    idx = jnp.arange(CHUNK)
    incl = idx[:, None] >= idx[None, :]
    strict = idx[:, None] > idx[None, :]
    decay = jnp.exp(jnp.where(incl, g_cum[..., :, None] - g_cum[..., None, :], -jnp.inf))
    k_beta = k * beta[..., None]
    a_mat = jnp.where(strict, jnp.einsum("bhcld,bhcsd->bhcls", k_beta, k) * decay, 0.0)
    lhs = a_mat + jnp.eye(CHUNK, dtype=a_mat.dtype)
    rhs = jnp.concatenate([v * beta[..., None], k_beta * jnp.exp(g_cum)[..., None]], axis=-1)
    sol = lax.linalg.triangular_solve(lhs, rhs, left_side=True, lower=True, unit_diagonal=True)
    u_val, w_dec = sol[..., :dv], sol[..., dv:]
    attn = jnp.einsum("bhcld,bhcsd->bhcls", q, k) * decay
    q_dec = q * jnp.exp(g_cum)[..., None]
    k_dec = k * jnp.exp(g_cum[..., -1:] - g_cum)[..., None]
    g_last = jnp.exp(g_cum[..., -1])

    def step(s, inp):
        u_c, w_c, qd_c, kd_c, at_c, gl_c = inp
        v_new = u_c - jnp.einsum("bhld,bhde->bhle", w_c, s)
        o = jnp.einsum("bhld,bhde->bhle", qd_c, s) + jnp.einsum("bhls,bhse->bhle", at_c, v_new)
        s = s * gl_c[..., None, None] + jnp.einsum("bhld,bhle->bhde", kd_c, v_new)
        return s, o

    xs = tuple(jnp.moveaxis(u, 2, 0) for u in (u_val, w_dec, q_dec, k_dec, attn, g_last))
    s0 = jnp.zeros((b, h, dk, dv), jnp.float32)
    _, o = lax.scan(step, s0, xs)
    o = jnp.moveaxis(jnp.moveaxis(o, 0, 2), 1, 3)
    return o.reshape(b, t, h, dv)


def hybrid_layer(hid, norm_w, w_in, ssd_conv_w, ssd_conv_b, ssd_dt_bias, ssd_a_log, ssd_d,
                 ssd_norm_w, gdn_conv_w, gdn_dt_bias, gdn_a_log, gdn_norm_w, w_out):
    b, t, _ = hid.shape
    u = rmsnorm(hid, norm_w).astype(hid.dtype)
    proj = u @ w_in
    splits = np.cumsum(IN_SIZES)[:-1].tolist()
    z, xbc, dt_raw, gate, qkv, a_raw, b_raw = jnp.split(proj, splits, axis=-1)

    xbc = jax.nn.silu(causal_dwconv(xbc, ssd_conv_w) + ssd_conv_b).astype(jnp.float32)
    xs, bm, cm = jnp.split(xbc, [SSD_WIDTH, SSD_WIDTH + SSD_GROUPS * SSD_STATE], axis=-1)
    xs = xs.reshape(b, t, SSD_HEADS, SSD_HEAD_DIM)
    bm = bm.reshape(b, t, SSD_GROUPS, SSD_STATE)
    cm = cm.reshape(b, t, SSD_GROUPS, SSD_STATE)
    dt = jax.nn.softplus(dt_raw.astype(jnp.float32) + ssd_dt_bias.astype(jnp.float32))
    a_neg = -jnp.exp(ssd_a_log.astype(jnp.float32))
    y = ssd_chunked(xs, dt, a_neg, bm, cm) + ssd_d.astype(jnp.float32)[:, None] * xs
    yg = y.reshape(b, t, SSD_GROUPS, SSD_WIDTH // SSD_GROUPS) * jax.nn.silu(
        z.astype(jnp.float32).reshape(b, t, SSD_GROUPS, SSD_WIDTH // SSD_GROUPS))
    yg = yg * lax.rsqrt(jnp.mean(yg * yg, axis=-1, keepdims=True) + EPS)
    y_ssd = yg.reshape(b, t, SSD_WIDTH) * ssd_norm_w.astype(jnp.float32)

    qkv = jax.nn.silu(causal_dwconv(qkv, gdn_conv_w)).astype(jnp.float32)
    q, k, v = jnp.split(qkv, [GDN_KW, 2 * GDN_KW], axis=-1)
    q = l2norm(q.reshape(b, t, GDN_HEADS, GDN_DK)) * (GDN_DK ** -0.5)
    k = l2norm(k.reshape(b, t, GDN_HEADS, GDN_DK))
    v = v.reshape(b, t, GDN_HEADS, GDN_DV)
    beta = jax.nn.sigmoid(b_raw.astype(jnp.float32))
    g = -jnp.exp(gdn_a_log.astype(jnp.float32)) * jax.nn.softplus(
        a_raw.astype(jnp.float32) + gdn_dt_bias.astype(jnp.float32))
    o = gated_delta_chunked(q, k, v, g, beta)
    o = rmsnorm(o, gdn_norm_w) * jax.nn.silu(gate.astype(jnp.float32).reshape(b, t, GDN_HEADS, GDN_DV))
    y_gdn = o.reshape(b, t, GDN_VW)

    mix = jnp.concatenate([y_ssd, y_gdn], axis=-1).astype(hid.dtype)
    return hid + mix @ w_out


def _inv_softplus_dt(key, shape):
    lo, hi = math.log(1e-3), math.log(1e-1)
    dt = jnp.exp(jax.random.uniform(key, shape, jnp.float32) * (hi - lo) + lo)
    dt = jnp.maximum(dt, 1e-4)
    return dt + jnp.log(-jnp.expm1(-dt))


def _fwd_setup_inputs(seed: int = 0) -> dict:
    key = jax.random.key(seed)
    ks = jax.random.split(key, 16)
    f32 = jnp.float32
    L = DEPTH
    x = jax.random.normal(ks[0], (BATCH, SEQ, D_MODEL), f32)
    norm_w = 1.0 + 0.02 * jax.random.normal(ks[1], (L, D_MODEL), f32)
    w_in = jax.random.normal(ks[2], (L, D_MODEL, IN_DIM), f32) * D_MODEL ** -0.5
    ssd_conv_w = jax.random.normal(ks[3], (L, CONV_K, SSD_CONV_DIM), f32) * CONV_K ** -0.5
    ssd_conv_b = 0.02 * jax.random.normal(ks[4], (L, SSD_CONV_DIM), f32)
    ssd_dt_bias = _inv_softplus_dt(ks[5], (L, SSD_HEADS))
    ssd_a_log = jnp.log(jax.random.uniform(ks[6], (L, SSD_HEADS), f32, 1.0, 16.0))
    ssd_d = 1.0 + 0.02 * jax.random.normal(ks[7], (L, SSD_HEADS), f32)
    ssd_norm_w = 1.0 + 0.02 * jax.random.normal(ks[8], (L, SSD_WIDTH), f32)
    gdn_conv_w = jax.random.normal(ks[9], (L, CONV_K, GDN_CONV_DIM), f32) * CONV_K ** -0.5
    gdn_dt_bias = _inv_softplus_dt(ks[10], (L, GDN_HEADS))
    gdn_a_log = jnp.log(jax.random.uniform(ks[11], (L, GDN_HEADS), f32, 1.0, 16.0))
    gdn_norm_w = 1.0 + 0.02 * jax.random.normal(ks[12], (L, GDN_DV), f32)
    w_out = jax.random.normal(ks[13], (L, MIX_WIDTH, D_MODEL), f32) * MIX_WIDTH ** -0.5
    final_norm_w = 1.0 + 0.02 * jax.random.normal(ks[14], (D_MODEL,), f32)
    return {"x": x, "norm_w": norm_w, "w_in": w_in, "ssd_conv_w": ssd_conv_w,
            "ssd_conv_b": ssd_conv_b, "ssd_dt_bias": ssd_dt_bias, "ssd_a_log": ssd_a_log,
            "ssd_d": ssd_d, "ssd_norm_w": ssd_norm_w, "gdn_conv_w": gdn_conv_w,
            "gdn_dt_bias": gdn_dt_bias, "gdn_a_log": gdn_a_log, "gdn_norm_w": gdn_norm_w,
            "w_out": w_out, "final_norm_w": final_norm_w}


def _fwd_reference(x, norm_w, w_in, ssd_conv_w, ssd_conv_b, ssd_dt_bias, ssd_a_log, ssd_d, ssd_norm_w,
              gdn_conv_w, gdn_dt_bias, gdn_a_log, gdn_norm_w, w_out, final_norm_w):
    hid = x
    for i in range(DEPTH):
        hid = hybrid_layer(hid, norm_w[i], w_in[i], ssd_conv_w[i], ssd_conv_b[i], ssd_dt_bias[i],
                           ssd_a_log[i], ssd_d[i], ssd_norm_w[i], gdn_conv_w[i], gdn_dt_bias[i],
                           gdn_a_log[i], gdn_norm_w[i], w_out[i])
    return rmsnorm(hid, final_norm_w).astype(x.dtype)


import jax as _jax
import jax.numpy as _jnp

TWIN_FORMAT = 'train_step'
FWD_PARAMS = ['x', 'norm_w', 'w_in', 'ssd_conv_w', 'ssd_conv_b', 'ssd_dt_bias', 'ssd_a_log', 'ssd_d', 'ssd_norm_w', 'gdn_conv_w', 'gdn_dt_bias', 'gdn_a_log', 'gdn_norm_w', 'w_out', 'final_norm_w']
TWIN_WEIGHTS = ['norm_w', 'w_in', 'ssd_conv_w', 'ssd_conv_b', 'ssd_dt_bias', 'ssd_a_log', 'ssd_d', 'ssd_norm_w', 'gdn_conv_w', 'gdn_dt_bias', 'gdn_a_log', 'gdn_norm_w', 'w_out', 'final_norm_w']
TWIN_DIFF_INPUT = 'x'
TWIN_INPUTS = ['x', 'norm_w', 'w_in', 'ssd_conv_w', 'ssd_conv_b', 'ssd_dt_bias', 'ssd_a_log', 'ssd_d', 'ssd_norm_w', 'gdn_conv_w', 'gdn_dt_bias', 'gdn_a_log', 'gdn_norm_w', 'w_out', 'final_norm_w', 'loss_target', 'm_norm_w', 'm_w_in', 'm_ssd_conv_w', 'm_ssd_conv_b', 'm_ssd_dt_bias', 'm_ssd_a_log', 'm_ssd_d', 'm_ssd_norm_w', 'm_gdn_conv_w', 'm_gdn_dt_bias', 'm_gdn_a_log', 'm_gdn_norm_w', 'm_w_out', 'm_final_norm_w', 'v_norm_w', 'v_w_in', 'v_ssd_conv_w', 'v_ssd_conv_b', 'v_ssd_dt_bias', 'v_ssd_a_log', 'v_ssd_d', 'v_ssd_norm_w', 'v_gdn_conv_w', 'v_gdn_dt_bias', 'v_gdn_a_log', 'v_gdn_norm_w', 'v_w_out', 'v_final_norm_w']
TWIN_OUTPUTS = ['loss', 'grad_x', 'grad_norm_w', 'grad_w_in', 'grad_ssd_conv_w', 'grad_ssd_conv_b', 'grad_ssd_dt_bias', 'grad_ssd_a_log', 'grad_ssd_d', 'grad_ssd_norm_w', 'grad_gdn_conv_w', 'grad_gdn_dt_bias', 'grad_gdn_a_log', 'grad_gdn_norm_w', 'grad_w_out', 'grad_final_norm_w', 'delta_norm_w', 'delta_w_in', 'delta_ssd_conv_w', 'delta_ssd_conv_b', 'delta_ssd_dt_bias', 'delta_ssd_a_log', 'delta_ssd_d', 'delta_ssd_norm_w', 'delta_gdn_conv_w', 'delta_gdn_dt_bias', 'delta_gdn_a_log', 'delta_gdn_norm_w', 'delta_w_out', 'delta_final_norm_w', 'new_m_norm_w', 'new_m_w_in', 'new_m_ssd_conv_w', 'new_m_ssd_conv_b', 'new_m_ssd_dt_bias', 'new_m_ssd_a_log', 'new_m_ssd_d', 'new_m_ssd_norm_w', 'new_m_gdn_conv_w', 'new_m_gdn_dt_bias', 'new_m_gdn_a_log', 'new_m_gdn_norm_w', 'new_m_w_out', 'new_m_final_norm_w', 'new_v_norm_w', 'new_v_w_in', 'new_v_ssd_conv_w', 'new_v_ssd_conv_b', 'new_v_ssd_dt_bias', 'new_v_ssd_a_log', 'new_v_ssd_d', 'new_v_ssd_norm_w', 'new_v_gdn_conv_w', 'new_v_gdn_dt_bias', 'new_v_gdn_a_log', 'new_v_gdn_norm_w', 'new_v_w_out', 'new_v_final_norm_w']
TWIN_LEAF_KINDS = {'loss': 'loss', 'grad_x': 'grad_x', 'grad_norm_w': 'grad_w', 'grad_w_in': 'grad_w', 'grad_ssd_conv_w': 'grad_w', 'grad_ssd_conv_b': 'grad_w', 'grad_ssd_dt_bias': 'grad_w', 'grad_ssd_a_log': 'grad_w', 'grad_ssd_d': 'grad_w', 'grad_ssd_norm_w': 'grad_w', 'grad_gdn_conv_w': 'grad_w', 'grad_gdn_dt_bias': 'grad_w', 'grad_gdn_a_log': 'grad_w', 'grad_gdn_norm_w': 'grad_w', 'grad_w_out': 'grad_w', 'grad_final_norm_w': 'grad_w', 'delta_norm_w': 'delta_w', 'delta_w_in': 'delta_w', 'delta_ssd_conv_w': 'delta_w', 'delta_ssd_conv_b': 'delta_w', 'delta_ssd_dt_bias': 'delta_w', 'delta_ssd_a_log': 'delta_w', 'delta_ssd_d': 'delta_w', 'delta_ssd_norm_w': 'delta_w', 'delta_gdn_conv_w': 'delta_w', 'delta_gdn_dt_bias': 'delta_w', 'delta_gdn_a_log': 'delta_w', 'delta_gdn_norm_w': 'delta_w', 'delta_w_out': 'delta_w', 'delta_final_norm_w': 'delta_w', 'new_m_norm_w': 'new_m', 'new_m_w_in': 'new_m', 'new_m_ssd_conv_w': 'new_m', 'new_m_ssd_conv_b': 'new_m', 'new_m_ssd_dt_bias': 'new_m', 'new_m_ssd_a_log': 'new_m', 'new_m_ssd_d': 'new_m', 'new_m_ssd_norm_w': 'new_m', 'new_m_gdn_conv_w': 'new_m', 'new_m_gdn_dt_bias': 'new_m', 'new_m_gdn_a_log': 'new_m', 'new_m_gdn_norm_w': 'new_m', 'new_m_w_out': 'new_m', 'new_m_final_norm_w': 'new_m', 'new_v_norm_w': 'new_v', 'new_v_w_in': 'new_v', 'new_v_ssd_conv_w': 'new_v', 'new_v_ssd_conv_b': 'new_v', 'new_v_ssd_dt_bias': 'new_v', 'new_v_ssd_a_log': 'new_v', 'new_v_ssd_d': 'new_v', 'new_v_ssd_norm_w': 'new_v', 'new_v_gdn_conv_w': 'new_v', 'new_v_gdn_dt_bias': 'new_v', 'new_v_gdn_a_log': 'new_v', 'new_v_gdn_norm_w': 'new_v', 'new_v_w_out': 'new_v', 'new_v_final_norm_w': 'new_v'}


def _forward(args):
    return _fwd_reference(*[args[k] for k in FWD_PARAMS])


def _output_shape():
    def fwd():
        inp = _fwd_setup_inputs(0)
        return _fwd_reference(*[inp[k] for k in FWD_PARAMS])
    out = _jax.eval_shape(fwd)
    return out.shape, out.dtype

N_MICROBATCH = 1
ADAM_LR = 0.001
ADAM_B1 = 0.9
ADAM_B2 = 0.999
ADAM_EPS = 1e-08
ADAM_WD = 0.01
ADAM_STEP = 10
PER_EXAMPLE_BATCH_AXIS = {'x': 0, 'loss_target': 0}
SHARED_INPUTS = []
_WEIGHT_DTYPES = {'norm_w': _jnp.float32, 'w_in': _jnp.float32, 'ssd_conv_w': _jnp.float32, 'ssd_conv_b': _jnp.float32, 'ssd_dt_bias': _jnp.float32, 'ssd_a_log': _jnp.float32, 'ssd_d': _jnp.float32, 'ssd_norm_w': _jnp.float32, 'gdn_conv_w': _jnp.float32, 'gdn_dt_bias': _jnp.float32, 'gdn_a_log': _jnp.float32, 'gdn_norm_w': _jnp.float32, 'w_out': _jnp.float32, 'final_norm_w': _jnp.float32}
MOMENT_SCALE = {'norm_w': 2.540162e-01, 'w_in': 9.761329e-02, 'ssd_conv_w': 1.229862e-01, 'ssd_conv_b': 1.534474e-01, 'ssd_dt_bias': 3.521406e-01, 'ssd_a_log': 4.290028e-01, 'ssd_d': 7.356955e-01, 'ssd_norm_w': 1.381076e-01, 'gdn_conv_w': 6.271213e-02, 'gdn_dt_bias': 3.518254e-01, 'gdn_a_log': 3.726198e-01, 'gdn_norm_w': 2.255879e-01, 'w_out': 1.613314e-01, 'final_norm_w': 6.402238e+01}


def _to_microbatches(a, axis):
    t = _jnp.moveaxis(a, axis, 0)
    t = t.reshape((N_MICROBATCH, t.shape[0] // N_MICROBATCH) + t.shape[1:])
    return _jnp.moveaxis(t, 1, axis + 1)


def setup_inputs(seed: int = 0) -> dict:
    inp = _fwd_setup_inputs(seed)
    key = _jax.random.fold_in(_jax.random.key(seed), 7919)
    shape, _ = _output_shape()
    out = dict(inp)
    out["loss_target"] = _jax.random.normal(_jax.random.fold_in(key, 0), shape, _jnp.float32)
    for i, name in enumerate(TWIN_WEIGHTS):
        w = inp[name].astype(_jnp.float32)
        if MOMENT_SCALE is None:
            s = _jnp.sqrt(_jnp.mean(_jnp.square(w)) + 1e-30)
        else:
            s = MOMENT_SCALE[name]
        km, kv = _jax.random.split(_jax.random.fold_in(key, i + 1))
        out[name] = w
        out["m_" + name] = s * _jax.random.normal(km, w.shape, _jnp.float32)
        out["v_" + name] = (s * s) * _jax.random.uniform(kv, w.shape, _jnp.float32, 0.5, 1.5)
    if N_MICROBATCH > 1:
        for name, axis in PER_EXAMPLE_BATCH_AXIS.items():
            out[name] = _to_microbatches(out[name], axis)
    return {'x': out['x'], 'norm_w': out['norm_w'], 'w_in': out['w_in'], 'ssd_conv_w': out['ssd_conv_w'], 'ssd_conv_b': out['ssd_conv_b'], 'ssd_dt_bias': out['ssd_dt_bias'], 'ssd_a_log': out['ssd_a_log'], 'ssd_d': out['ssd_d'], 'ssd_norm_w': out['ssd_norm_w'], 'gdn_conv_w': out['gdn_conv_w'], 'gdn_dt_bias': out['gdn_dt_bias'], 'gdn_a_log': out['gdn_a_log'], 'gdn_norm_w': out['gdn_norm_w'], 'w_out': out['w_out'], 'final_norm_w': out['final_norm_w'], 'loss_target': out['loss_target'], 'm_norm_w': out['m_norm_w'], 'm_w_in': out['m_w_in'], 'm_ssd_conv_w': out['m_ssd_conv_w'], 'm_ssd_conv_b': out['m_ssd_conv_b'], 'm_ssd_dt_bias': out['m_ssd_dt_bias'], 'm_ssd_a_log': out['m_ssd_a_log'], 'm_ssd_d': out['m_ssd_d'], 'm_ssd_norm_w': out['m_ssd_norm_w'], 'm_gdn_conv_w': out['m_gdn_conv_w'], 'm_gdn_dt_bias': out['m_gdn_dt_bias'], 'm_gdn_a_log': out['m_gdn_a_log'], 'm_gdn_norm_w': out['m_gdn_norm_w'], 'm_w_out': out['m_w_out'], 'm_final_norm_w': out['m_final_norm_w'], 'v_norm_w': out['v_norm_w'], 'v_w_in': out['v_w_in'], 'v_ssd_conv_w': out['v_ssd_conv_w'], 'v_ssd_conv_b': out['v_ssd_conv_b'], 'v_ssd_dt_bias': out['v_ssd_dt_bias'], 'v_ssd_a_log': out['v_ssd_a_log'], 'v_ssd_d': out['v_ssd_d'], 'v_ssd_norm_w': out['v_ssd_norm_w'], 'v_gdn_conv_w': out['v_gdn_conv_w'], 'v_gdn_dt_bias': out['v_gdn_dt_bias'], 'v_gdn_a_log': out['v_gdn_a_log'], 'v_gdn_norm_w': out['v_gdn_norm_w'], 'v_w_out': out['v_w_out'], 'v_final_norm_w': out['v_final_norm_w']}


def _loss(weights, diff, rest, loss_target):
    with _jax.named_scope("forward"):
        args = {**rest, TWIN_DIFF_INPUT: diff, **{k: w.astype(_WEIGHT_DTYPES[k]) for k, w in weights.items()}}
        y = _forward(args)
    with _jax.named_scope("loss_head"):
        err = _jnp.square(y.astype(_jnp.float32) - loss_target)
        return 0.5 * _jnp.sum(_jnp.mean(err, axis=-1)) if err.ndim else 0.5 * err


def _adamw(w, g, m, v):
    m = ADAM_B1 * m + (1.0 - ADAM_B1) * g
    v = ADAM_B2 * v + (1.0 - ADAM_B2) * _jnp.square(g)
    m_hat = m / (1.0 - ADAM_B1 ** ADAM_STEP)
    v_hat = v / (1.0 - ADAM_B2 ** ADAM_STEP)
    delta = -ADAM_LR * (m_hat / (_jnp.sqrt(v_hat) + ADAM_EPS) + ADAM_WD * w)
    return delta, m, v


def reference(x, norm_w, w_in, ssd_conv_w, ssd_conv_b, ssd_dt_bias, ssd_a_log, ssd_d, ssd_norm_w, gdn_conv_w, gdn_dt_bias, gdn_a_log, gdn_norm_w, w_out, final_norm_w, loss_target, m_norm_w, m_w_in, m_ssd_conv_w, m_ssd_conv_b, m_ssd_dt_bias, m_ssd_a_log, m_ssd_d, m_ssd_norm_w, m_gdn_conv_w, m_gdn_dt_bias, m_gdn_a_log, m_gdn_norm_w, m_w_out, m_final_norm_w, v_norm_w, v_w_in, v_ssd_conv_w, v_ssd_conv_b, v_ssd_dt_bias, v_ssd_a_log, v_ssd_d, v_ssd_norm_w, v_gdn_conv_w, v_gdn_dt_bias, v_gdn_a_log, v_gdn_norm_w, v_w_out, v_final_norm_w):
    given = dict(x=x, norm_w=norm_w, w_in=w_in, ssd_conv_w=ssd_conv_w, ssd_conv_b=ssd_conv_b, ssd_dt_bias=ssd_dt_bias, ssd_a_log=ssd_a_log, ssd_d=ssd_d, ssd_norm_w=ssd_norm_w, gdn_conv_w=gdn_conv_w, gdn_dt_bias=gdn_dt_bias, gdn_a_log=gdn_a_log, gdn_norm_w=gdn_norm_w, w_out=w_out, final_norm_w=final_norm_w, loss_target=loss_target, m_norm_w=m_norm_w, m_w_in=m_w_in, m_ssd_conv_w=m_ssd_conv_w, m_ssd_conv_b=m_ssd_conv_b, m_ssd_dt_bias=m_ssd_dt_bias, m_ssd_a_log=m_ssd_a_log, m_ssd_d=m_ssd_d, m_ssd_norm_w=m_ssd_norm_w, m_gdn_conv_w=m_gdn_conv_w, m_gdn_dt_bias=m_gdn_dt_bias, m_gdn_a_log=m_gdn_a_log, m_gdn_norm_w=m_gdn_norm_w, m_w_out=m_w_out, m_final_norm_w=m_final_norm_w, v_norm_w=v_norm_w, v_w_in=v_w_in, v_ssd_conv_w=v_ssd_conv_w, v_ssd_conv_b=v_ssd_conv_b, v_ssd_dt_bias=v_ssd_dt_bias, v_ssd_a_log=v_ssd_a_log, v_ssd_d=v_ssd_d, v_ssd_norm_w=v_ssd_norm_w, v_gdn_conv_w=v_gdn_conv_w, v_gdn_dt_bias=v_gdn_dt_bias, v_gdn_a_log=v_gdn_a_log, v_gdn_norm_w=v_gdn_norm_w, v_w_out=v_w_out, v_final_norm_w=v_final_norm_w)
    weights = {n: given[n] for n in TWIN_WEIGHTS}
    shared = {n: given[n] for n in SHARED_INPUTS}
    per_example = {n: given[n] for n in ['x']}
    grad_fn = _jax.value_and_grad(_loss, argnums=(0, 1))

    def one_microbatch(ex, loss_target):
        ex = dict(ex)
        diff = ex.pop(TWIN_DIFF_INPUT)
        return grad_fn(weights, diff, {**shared, **ex}, loss_target)

    if N_MICROBATCH == 1:
        loss, (grad_w, grad_x) = one_microbatch(per_example, given["loss_target"])
    else:
        def body(carry, xs):
            loss_sum, grad_sum = carry
            l_k, (gw_k, gx_k) = one_microbatch(xs[0], xs[1])
            with _jax.named_scope("update"):
                return (loss_sum + l_k, _jax.tree.map(_jnp.add, grad_sum, gw_k)), gx_k

        init = (_jnp.zeros((), _jnp.float32), _jax.tree.map(_jnp.zeros_like, weights))
        (loss, grad_w), grad_x = _jax.lax.scan(body, init, (per_example, given["loss_target"]))
    with _jax.named_scope("update"):
        delta_w, new_m, new_v = {}, {}, {}
        for n in TWIN_WEIGHTS:
            delta_w[n], new_m[n], new_v[n] = _adamw(weights[n], grad_w[n], given["m_" + n], given["v_" + n])
    return (loss, grad_x, *[grad_w[n] for n in TWIN_WEIGHTS], *[delta_w[n] for n in TWIN_WEIGHTS],
            *[new_m[n] for n in TWIN_WEIGHTS], *[new_v[n] for n in TWIN_WEIGHTS])
```

```python
import jax
import jax.numpy as jnp
import numpy as np
from jax import lax
from jax.experimental import pallas as pl
from jax.experimental.pallas import tpu as pltpu

_MM = jnp.bfloat16

D_MODEL = 1024
CHUNK = 64
CONV_K = 4
EPS = 1e-6
SSD_CONV_DIM = 1536
GDN_HEADS = 8
GDN_DK = 128
GDN_CONV_DIM = 3072
MIX_WIDTH = 2048
IN_DIM = 6688
N_DEV = 8
W_IN_SHARD = IN_DIM // N_DEV
PERM_DIM = 6784
HI = lax.Precision.HIGHEST
VMEM_LIMIT = 56 * 1024 * 1024

ADAM_LR = 0.001
ADAM_B1 = 0.9
ADAM_B2 = 0.999
ADAM_EPS = 1e-08
ADAM_WD = 0.01
ADAM_STEP = 10


def _pc(body, **kw):
    return pl.pallas_call(body, **kw)


def _cparams(sem):
    return pltpu.CompilerParams(dimension_semantics=sem, vmem_limit_bytes=VMEM_LIMIT)


def _silu(x):
    return x * (1.0 / (1.0 + jnp.exp(-x)))


def _sigmoid(x):
    return 1.0 / (1.0 + jnp.exp(-x))


def _softplus(x):
    return jnp.maximum(x, 0.0) + jnp.log(1.0 + jnp.exp(-jnp.abs(x)))


def _mm(a, b):
    return jnp.dot(a.astype(_MM), b.astype(_MM), preferred_element_type=jnp.float32)


def _mm_nt(a, b):
    return lax.dot_general(a.astype(_MM), b.astype(_MM), (((1,), (1,)), ((), ())),
                           preferred_element_type=jnp.float32)


def _mm_tn(a, b):
    return lax.dot_general(a.astype(_MM), b.astype(_MM), (((0,), (0,)), ((), ())),
                           preferred_element_type=jnp.float32)


def _dot_hi(a, b):
    return jnp.dot(a, b, precision=HI, preferred_element_type=jnp.float32)


def _bmm(a, b):
    return lax.dot_general(a.astype(_MM), b.astype(_MM), (((2,), (1,)), ((0,), (0,))),
                           preferred_element_type=jnp.float32)


def _bmm_nt(a, b):
    return lax.dot_general(a.astype(_MM), b.astype(_MM), (((2,), (2,)), ((0,), (0,))),
                           preferred_element_type=jnp.float32)


def _bmm_tn(a, b):
    return lax.dot_general(a.astype(_MM), b.astype(_MM), (((1,), (1,)), ((0,), (0,))),
                           preferred_element_type=jnp.float32)


def _bmm_hi(a, b):
    return lax.dot_general(a, b, (((2,), (1,)), ((0,), (0,))), precision=HI, preferred_element_type=jnp.float32)


def _bmm_nt_hi(a, b):
    return lax.dot_general(a, b, (((2,), (2,)), ((0,), (0,))), precision=HI, preferred_element_type=jnp.float32)


def _bmm_tn_hi(a, b):
    return lax.dot_general(a, b, (((1,), (1,)), ((0,), (0,))), precision=HI, preferred_element_type=jnp.float32)


def _consts():
    l = np.arange(CHUNK)
    tri = (l[:, None] >= l[None, :]).astype(np.float32)
    lane = np.arange(128)
    e_ssd = np.zeros((8, 128, 128), np.float32)
    for j in range(8):
        e_ssd[j, 2 * j, :64] = 1.0
        e_ssd[j, 2 * j + 1, 64:] = 1.0
    i2 = (l[:, None] == (lane[None, :] % 64)).astype(np.float32)
    mask2 = (l[:, None] >= (lane[None, :] % 64)).astype(np.float32)
    lo = (lane < 64).astype(np.float32)[None, :]
    e_a = np.zeros((8, 128, 128), np.float32)
    e_b = np.zeros((8, 128, 128), np.float32)
    for h in range(8):
        e_a[h, 16 + h, :] = 1.0
        e_b[h, 24 + h, :] = 1.0
    i64 = np.eye(CHUNK, dtype=np.float32)
    strict = (l[:, None] > l[None, :]).astype(np.float32)
    return dict(tri=jnp.asarray(tri), e_ssd=jnp.asarray(e_ssd), i2=jnp.asarray(i2), mask2=jnp.asarray(mask2),
                lo=jnp.asarray(lo), e_a=jnp.asarray(e_a), e_b=jnp.asarray(e_b), i64=jnp.asarray(i64),
                strict=jnp.asarray(strict))


def _ssd_chunk(xs_pre, b_pre, c_pre, z, sm, ht, dtb, alog, dpar, nw, tri, e_ssd, i2, mask2, lo):
    lane = lax.broadcasted_iota(jnp.int32, (1, 128), 1)
    m16 = lane < 16
    dt = jnp.where(m16, _softplus(sm + dtb), 0.0)
    a_neg = -jnp.exp(alog)
    cum = _dot_hi(tri, dt * a_neg)
    row = lax.broadcasted_iota(jnp.int32, (CHUNK, 1), 0)
    is_last = row == CHUNK - 1
    hi = 1.0 - lo
    bm = [_silu(b) for b in b_pre]
    cm = [_silu(c) for c in c_pre]
    cb2 = [_mm_nt(cm[g], jnp.concatenate([bm[g], bm[g]], axis=0)) for g in range(2)]
    yg, ht_next = [], []
    for j in range(8):
        g = j // 4
        ej = e_ssd[j]
        xs = _silu(xs_pre[j])
        dte = _dot_hi(dt, ej)
        cume = _dot_hi(cum, ej)
        cum_last = jnp.sum(jnp.where(is_last, cume, 0.0), axis=0, keepdims=True)
        xdt = xs * dte
        rowv = jnp.sum(cume * i2, axis=0, keepdims=True)
        lm = jnp.exp(jnp.where(mask2 > 0.5, cume - rowv, -jnp.inf))
        m = cb2[g] * lm
        xblk = jnp.concatenate([xdt * lo, xdt * hi], axis=0)
        y = _mm(m, xblk)
        y = y + _mm(cm[g], ht[j]) * jnp.exp(cume)
        y = y + _dot_hi(dpar, ej) * xs
        yg.append(y * _silu(z[j]))
        st = _mm_tn(bm[g], xdt * jnp.exp(cum_last - cume))
        ht_next.append(ht[j] * jnp.exp(cum_last) + st)
    outs = []
    for g in range(2):
        ss = sum(jnp.sum(yg[j] * yg[j], axis=-1, keepdims=True) for j in range(4 * g, 4 * g + 4))
        rs = lax.rsqrt(ss * (1.0 / 512.0) + EPS)
        for j in range(4 * g, 4 * g + 4):
            outs.append(yg[j] * rs * nw[j])
    return outs, ht_next


def _tri_inverse(a):
    eye = jnp.eye(CHUNK, dtype=jnp.float32)[None]
    p = eye - a
    ap = a
    for _ in range(5):
        ap = _bmm_hi(ap, ap)
        p = p + _bmm_hi(p, ap)
    return p


@jax.custom_vjp
def _solve(a, r1, r2):
    t = _tri_inverse(a)
    return _bmm_hi(t, r1), _bmm_hi(t, r2)


def _solve_fwd(a, r1, r2):
    t = _tri_inverse(a)
    u, w = _bmm_hi(t, r1), _bmm_hi(t, r2)
    return (u, w), (t, u, w)


def _solve_bwd(res, cts):
    t, u, w = res
    du, dw = cts
    dr1 = _bmm_tn_hi(t, du)
    dr2 = _bmm_tn_hi(t, dw)
    da = -(_bmm_nt_hi(dr1, u) + _bmm_nt_hi(dr2, w))
    return da, dr1, dr2


_solve.defvjp(_solve_fwd, _solve_bwd)


def _gdn_chunk(q_pre, k_pre, v_pre, gate, sm, s, dtb, alog, nw, tri, e_a, e_b, i64, strict):
    lane = lax.broadcasted_iota(jnp.int32, (1, 128), 1)
    m_a = (lane >= 16) & (lane < 24)
    g_full = jnp.where(m_a, -jnp.exp(alog) * _softplus(sm + dtb), 0.0)
    gc = _dot_hi(tri, g_full)
    sig = _sigmoid(sm)
    gc3 = jnp.stack([_dot_hi(gc, e_a[h]) for h in range(GDN_HEADS)])
    beta3 = jnp.stack([_dot_hi(sig, e_b[h]) for h in range(GDN_HEADS)])
    q = _silu(q_pre)
    q = q * lax.rsqrt(jnp.sum(q * q, axis=-1, keepdims=True) + EPS) * (GDN_DK ** -0.5)
    k = _silu(k_pre)
    k = k * lax.rsqrt(jnp.sum(k * k, axis=-1, keepdims=True) + EPS)
    v = _silu(v_pre)
    gcl = gc3[:, :, :CHUNK]
    gc_row = jnp.sum(gcl * i64[None], axis=1, keepdims=True)
    incl = (strict + i64)[None] > 0.5
    decay = jnp.exp(jnp.where(incl, gcl - gc_row, -jnp.inf))
    kb = k * beta3
    a = jnp.where(strict[None] > 0.5, _bmm_nt(kb, k) * decay, 0.0)
    egc = jnp.exp(gc3)
    u, w = _solve(a, v * beta3, kb * egc)
    attn = _bmm_nt(q, k) * decay
    row = lax.broadcasted_iota(jnp.int32, (1, CHUNK, 1), 1)
    gl = jnp.sum(jnp.where(row == CHUNK - 1, gc3, 0.0), axis=1, keepdims=True)
    q_dec = q * egc
    k_dec = k * jnp.exp(gl - gc3)
    v_new = u - _bmm(w, s)
    o = _bmm(q_dec, s) + _bmm(attn, v_new)
    s_next = s * jnp.exp(gl) + _bmm_tn(k_dec, v_new)
    on = o * lax.rsqrt(jnp.mean(o * o, axis=-1, keepdims=True) + EPS) * nw
    return on * _silu(gate), s_next


def _conv_fwd(pbuf, w_ref, c0, c1):
    acc = None
    for j in range(CONV_K):
        term = w_ref[j:j + 1, c0:c1] * pbuf[5 + j:69 + j, c0:c1]
        acc = term if acc is None else acc + term
    return acc


GROUPS = (("z", 0, 1024), ("xbc", 1024, 2560), ("gate", 2560, 3584), ("qkv", 3584, 6656), ("sm", 6656, 6784))


def inproj_fwd(x, norm_w, w_perm):
    t = x.shape[0]
    tm = min(256, t)

    def body(x_ref, nw_ref, w_ref, u_ref, z_ref, xbc_ref, gate_ref, qkv_ref, sm_ref):
        xf = x_ref[...]
        rstd = lax.rsqrt(jnp.mean(xf * xf, axis=-1, keepdims=True) + EPS)
        u = (xf * rstd * nw_ref[...]).astype(_MM)
        u_ref[...] = u
        for (name, c0, c1), o_ref in zip(GROUPS, (z_ref, xbc_ref, gate_ref, qkv_ref, sm_ref)):
            o_ref[...] = jnp.dot(u, w_ref[:, c0:c1], preferred_element_type=jnp.float32)

    outs = [jax.ShapeDtypeStruct((t, D_MODEL), _MM)] + [jax.ShapeDtypeStruct((t, c1 - c0), jnp.float32)
                                                        for _, c0, c1 in GROUPS]
    return _pc(
        body, name="inproj_fwd", grid=(t // tm,),
        in_specs=[pl.BlockSpec((tm, D_MODEL), lambda i: (i, 0)),
                  pl.BlockSpec((1, D_MODEL), lambda i: (0, 0)),
                  pl.BlockSpec((D_MODEL, PERM_DIM), lambda i: (0, 0))],
        out_specs=[pl.BlockSpec((tm, D_MODEL), lambda i: (i, 0))] +
                  [pl.BlockSpec((tm, c1 - c0), lambda i: (i, 0)) for _, c0, c1 in GROUPS],
        out_shape=outs, compiler_params=_cparams(("arbitrary",)),
    )(x, norm_w, w_perm)


def _halo_spec(width, idx_fn):
    return pl.BlockSpec((8, width), lambda i: (jnp.maximum(idx_fn(i) * 8 - 1, 0), 0))


def _full(shape):
    nd = len(shape)
    return pl.BlockSpec(shape, lambda i: (0,) * nd)


def _ssd_split(pre_fn, z_ref, sm_ref):
    xs_pre = [pre_fn(128 * j, 128 * j + 128) for j in range(8)]
    b_pre = [pre_fn(1024 + 128 * g, 1152 + 128 * g) for g in range(2)]
    c_pre = [pre_fn(1280 + 128 * g, 1408 + 128 * g) for g in range(2)]
    z = [z_ref[:, 128 * j:128 * j + 128] for j in range(8)]
    return xs_pre, b_pre, c_pre, z, sm_ref[...]


def ssd_fwd(z, xbc, sm, conv_w, conv_b, dtb, alog, dpar, nw, cs):
    t = z.shape[0]
    nc = t // CHUNK

    def body(z_ref, xbc_ref, halo_ref, sm_ref, cw_ref, cb_ref, dtb_ref, alog_ref, dpar_ref, nw_ref,
             tri_ref, e_ref, i2_ref, mask2_ref, lo_ref, y_ref, hs_ref, pbuf, ht_scr):
        i = pl.program_id(0)

        @pl.when(i == 0)
        def _():
            ht_scr[...] = jnp.zeros_like(ht_scr)

        pbuf[0:8, :] = jnp.where(i == 0, 0.0, halo_ref[...])
        pbuf[8:72, :] = xbc_ref[...]
        pre_fn = lambda c0, c1: _conv_fwd(pbuf, cw_ref, c0, c1) + cb_ref[:, c0:c1]
        xs_pre, b_pre, c_pre, zz, smv = _ssd_split(pre_fn, z_ref, sm_ref)
        ht = [ht_scr[:, 128 * j:128 * j + 128] for j in range(8)]
        hs_ref[0] = ht_scr[...]
        nwl = [nw_ref[:, 128 * j:128 * j + 128] for j in range(8)]
        outs, ht_next = _ssd_chunk(xs_pre, b_pre, c_pre, zz, smv, ht, dtb_ref[...], alog_ref[...], dpar_ref[...],
                                   nwl, tri_ref[...], e_ref, i2_ref[...], mask2_ref[...], lo_ref[...])
        for j in range(8):
            y_ref[:, 128 * j:128 * j + 128] = outs[j].astype(y_ref.dtype)
            ht_scr[:, 128 * j:128 * j + 128] = ht_next[j]

    blk = lambda w: pl.BlockSpec((CHUNK, w), lambda i: (i, 0))
    return _pc(
        body, name="ssd_fwd", grid=(nc,),
        in_specs=[blk(1024), blk(1536), _halo_spec(1536, lambda i: i), blk(128),
                  _full((CONV_K, 1536)), _full((1, 1536)), _full((1, 128)), _full((1, 128)), _full((1, 128)),
                  _full((1, 1024)), _full((64, 64)), _full((8, 128, 128)), _full((64, 128)), _full((64, 128)),
                  _full((1, 128))],
        out_specs=[blk(1024), pl.BlockSpec((1, 128, 1024), lambda i: (i, 0, 0))],
        out_shape=[jax.ShapeDtypeStruct((t, 1024), _MM), jax.ShapeDtypeStruct((nc, 128, 1024), jnp.float32)],
        scratch_shapes=[pltpu.VMEM((72, 1536), jnp.float32), pltpu.VMEM((128, 1024), jnp.float32)],
        compiler_params=_cparams(("arbitrary",)),
    )(z, xbc, xbc, sm, conv_w, conv_b, dtb, alog, dpar, nw, cs["tri"], cs["e_ssd"], cs["i2"], cs["mask2"], cs["lo"])


def _conv_bwd(dpre_list, col_ranges, dbuf, carry, pbuf, cw_ref, dx_ref, dcw_ref, dcb_ref, first):
    for dpre, (c0, c1) in zip(dpre_list, col_ranges):
        dbuf[0:64, c0:c1] = dpre
    dbuf[64:72, :] = jnp.where(first, 0.0, carry[...])
    carry[...] = dbuf[0:8, :]
    for (c0, c1) in col_ranges:
        acc = None
        for j in range(CONV_K):
            term = cw_ref[j:j + 1, c0:c1] * dbuf[3 - j:67 - j, c0:c1]
            acc = term if acc is None else acc + term
        dx_ref[:, c0:c1] = acc
        dpre = dbuf[0:64, c0:c1]
        for j in range(CONV_K):
            dcw_ref[j:j + 1, c0:c1] += jnp.sum(dpre * pbuf[5 + j:69 + j, c0:c1], axis=0, keepdims=True)
        if dcb_ref is not None:
            dcb_ref[0:1, c0:c1] += jnp.sum(dpre, axis=0, keepdims=True)


def ssd_bwd(z, xbc, sm, hs, dy, conv_w, conv_b, dtb, alog, dpar, nw, cs):
    t = z.shape[0]
    nc = t // CHUNK

    def body(z_ref, xbc_ref, halo_ref, sm_ref, hs_ref, dy_ref, cw_ref, cb_ref, dtb_ref, alog_ref, dpar_ref, nw_ref,
             tri_ref, e_ref, i2_ref, mask2_ref, lo_ref,
             dz_ref, dxbc_ref, dsm_ref, dcw_ref, dcb_ref, ddtb_ref, dalog_ref, ddpar_ref, dnw_ref,
             pbuf, dbuf, carry, dht_scr):
        i = pl.program_id(0)
        c = nc - 1 - i

        @pl.when(i == 0)
        def _():
            dht_scr[...] = jnp.zeros_like(dht_scr)
            dcw_ref[...] = jnp.zeros_like(dcw_ref)
            dcb_ref[...] = jnp.zeros_like(dcb_ref)
            ddtb_ref[...] = jnp.zeros_like(ddtb_ref)
            dalog_ref[...] = jnp.zeros_like(dalog_ref)
            ddpar_ref[...] = jnp.zeros_like(ddpar_ref)
            dnw_ref[...] = jnp.zeros_like(dnw_ref)

        pbuf[0:8, :] = jnp.where(c == 0, 0.0, halo_ref[...])
        pbuf[8:72, :] = xbc_ref[...]
        pre_fn = lambda c0, c1: _conv_fwd(pbuf, cw_ref, c0, c1) + cb_ref[:, c0:c1]
        xs_pre, b_pre, c_pre, zz, smv = _ssd_split(pre_fn, z_ref, sm_ref)
        ht = [hs_ref[0, :, 128 * j:128 * j + 128] for j in range(8)]
        nwl = [nw_ref[:, 128 * j:128 * j + 128] for j in range(8)]
        consts = (tri_ref[...], e_ref[...], i2_ref[...], mask2_ref[...], lo_ref[...])

        def f(xs_pre, b_pre, c_pre, zz, smv, ht, dtb, alog, dpar, nwl):
            return _ssd_chunk(xs_pre, b_pre, c_pre, zz, smv, ht, dtb, alog, dpar, nwl, *consts)

        _, vjp = jax.vjp(f, xs_pre, b_pre, c_pre, zz, smv, ht, dtb_ref[...], alog_ref[...], dpar_ref[...], nwl)
        dys = [dy_ref[:, 128 * j:128 * j + 128] for j in range(8)]
        dhts = [dht_scr[:, 128 * j:128 * j + 128] for j in range(8)]
        dxs, db, dc, dzz, dsm, dht, ddtb, dalog, ddpar, dnwl = vjp((dys, dhts))
        for j in range(8):
            dz_ref[:, 128 * j:128 * j + 128] = dzz[j]
            dht_scr[:, 128 * j:128 * j + 128] = dht[j]
            dnw_ref[0:1, 128 * j:128 * j + 128] += dnwl[j]
        dsm_ref[...] = dsm
        ddtb_ref[0:1, :] += ddtb
        dalog_ref[0:1, :] += dalog
        ddpar_ref[0:1, :] += ddpar
        ranges = ([(128 * j, 128 * j + 128) for j in range(8)] + [(1024 + 128 * g, 1152 + 128 * g) for g in range(2)]
                  + [(1280 + 128 * g, 1408 + 128 * g) for g in range(2)])
        _conv_bwd(dxs + db + dc, ranges, dbuf, carry, pbuf, cw_ref, dxbc_ref, dcw_ref, dcb_ref, i == 0)

    rblk = lambda w: pl.BlockSpec((CHUNK, w), lambda i: (nc - 1 - i, 0))
    acc = lambda w: pl.BlockSpec((8, w), lambda i: (0, 0))
    f32 = jnp.float32
    return _pc(
        body, name="ssd_bwd", grid=(nc,),
        in_specs=[rblk(1024), rblk(1536), _halo_spec(1536, lambda i: nc - 1 - i), rblk(128),
                  pl.BlockSpec((1, 128, 1024), lambda i: (nc - 1 - i, 0, 0)), rblk(1024),
                  _full((CONV_K, 1536)), _full((1, 1536)), _full((1, 128)), _full((1, 128)), _full((1, 128)),
                  _full((1, 1024)), _full((64, 64)), _full((8, 128, 128)), _full((64, 128)), _full((64, 128)),
                  _full((1, 128))],
        out_specs=[rblk(1024), rblk(1536), rblk(128), acc(1536), acc(1536), acc(128), acc(128), acc(128), acc(1024)],
        out_shape=[jax.ShapeDtypeStruct((t, 1024), f32), jax.ShapeDtypeStruct((t, 1536), f32),
                   jax.ShapeDtypeStruct((t, 128), f32), jax.ShapeDtypeStruct((8, 1536), f32),
                   jax.ShapeDtypeStruct((8, 1536), f32), jax.ShapeDtypeStruct((8, 128), f32),
                   jax.ShapeDtypeStruct((8, 128), f32), jax.ShapeDtypeStruct((8, 128), f32),
                   jax.ShapeDtypeStruct((8, 1024), f32)],
        scratch_shapes=[pltpu.VMEM((72, 1536), f32), pltpu.VMEM((72, 1536), f32), pltpu.VMEM((8, 1536), f32),
                        pltpu.VMEM((128, 1024), f32)],
        compiler_params=_cparams(("arbitrary",)),
    )(z, xbc, xbc, sm, hs, dy, conv_w, conv_b, dtb, alog, dpar, nw,
      cs["tri"], cs["e_ssd"], cs["i2"], cs["mask2"], cs["lo"])


def _gdn_split(pbuf, cw_ref, gate_ref):
    def heads(base):
        return jnp.stack([_conv_fwd(pbuf, cw_ref, base + 128 * h, base + 128 * h + 128) for h in range(GDN_HEADS)])
    gate = jnp.stack([gate_ref[:, 128 * h:128 * h + 128] for h in range(GDN_HEADS)])
    return heads(0), heads(1024), heads(2048), gate


def gdn_fwd(gate, qkv, sm, conv_w, dtb, alog, nw, cs):
    t = gate.shape[0]
    nc = t // CHUNK

    def body(gate_ref, qkv_ref, halo_ref, sm_ref, cw_ref, dtb_ref, alog_ref, nw_ref,
             tri_ref, ea_ref, eb_ref, i64_ref, strict_ref, o_ref, ss_ref, pbuf, s_scr):
        i = pl.program_id(0)

        @pl.when(i == 0)
        def _():
            s_scr[...] = jnp.zeros_like(s_scr)

        pbuf[0:8, :] = jnp.where(i == 0, 0.0, halo_ref[...])
        pbuf[8:72, :] = qkv_ref[...]
        q_pre, k_pre, v_pre, g3 = _gdn_split(pbuf, cw_ref, gate_ref)
        s = s_scr[...]
        ss_ref[0] = s
        out, s_next = _gdn_chunk(q_pre, k_pre, v_pre, g3, sm_ref[...], s, dtb_ref[...], alog_ref[...], nw_ref[...],
                                 tri_ref[...], ea_ref[...], eb_ref[...], i64_ref[...], strict_ref[...])
        s_scr[...] = s_next
        for h in range(GDN_HEADS):
            o_ref[:, 128 * h:128 * h + 128] = out[h].astype(o_ref.dtype)

    blk = lambda w: pl.BlockSpec((CHUNK, w), lambda i: (i, 0))
    return _pc(
        body, name="gdn_fwd", grid=(nc,),
        in_specs=[blk(1024), blk(3072), _halo_spec(3072, lambda i: i), blk(128),
                  _full((CONV_K, 3072)), _full((1, 128)), _full((1, 128)), _full((1, 128)),
                  _full((64, 64)), _full((8, 128, 128)), _full((8, 128, 128)), _full((64, 64)), _full((64, 64))],
        out_specs=[blk(1024), pl.BlockSpec((1, 8, 128, 128), lambda i: (i, 0, 0, 0))],
        out_shape=[jax.ShapeDtypeStruct((t, 1024), _MM), jax.ShapeDtypeStruct((nc, 8, 128, 128), jnp.float32)],
        scratch_shapes=[pltpu.VMEM((72, 3072), jnp.float32), pltpu.VMEM((8, 128, 128), jnp.float32)],
        compiler_params=_cparams(("arbitrary",)),
    )(gate, qkv, qkv, sm, conv_w, dtb, alog, nw, cs["tri"], cs["e_a"], cs["e_b"], cs["i64"], cs["strict"])


def gdn_bwd(gate, qkv, sm, ss, do, dsm_ssd, conv_w, dtb, alog, nw, cs):
    t = gate.shape[0]
    nc = t // CHUNK

    def body(gate_ref, qkv_ref, halo_ref, sm_ref, ss_ref, do_ref, dsm_in_ref, cw_ref, dtb_ref, alog_ref, nw_ref,
             tri_ref, ea_ref, eb_ref, i64_ref, strict_ref,
             dgate_ref, dqkv_ref, dsm_ref, dcw_ref, ddtb_ref, dalog_ref, dnw_ref,
             pbuf, dbuf, carry, ds_scr):
        i = pl.program_id(0)
        c = nc - 1 - i

        @pl.when(i == 0)
        def _():
            ds_scr[...] = jnp.zeros_like(ds_scr)
            dcw_ref[...] = jnp.zeros_like(dcw_ref)
            ddtb_ref[...] = jnp.zeros_like(ddtb_ref)
            dalog_ref[...] = jnp.zeros_like(dalog_ref)
            dnw_ref[...] = jnp.zeros_like(dnw_ref)

        pbuf[0:8, :] = jnp.where(c == 0, 0.0, halo_ref[...])
        pbuf[8:72, :] = qkv_ref[...]
        q_pre, k_pre, v_pre, g3 = _gdn_split(pbuf, cw_ref, gate_ref)
        consts = (tri_ref[...], ea_ref[...], eb_ref[...], i64_ref[...], strict_ref[...])

        def f(q_pre, k_pre, v_pre, g3, smv, s, dtb, alog, nwv):
            return _gdn_chunk(q_pre, k_pre, v_pre, g3, smv, s, dtb, alog, nwv, *consts)

        _, vjp = jax.vjp(f, q_pre, k_pre, v_pre, g3, sm_ref[...], ss_ref[0], dtb_ref[...], alog_ref[...], nw_ref[...])
        do3 = jnp.stack([do_ref[:, 128 * h:128 * h + 128] for h in range(GDN_HEADS)])
        dq, dk, dv, dg3, dsm, ds, ddtb, dalog, dnw = vjp((do3, ds_scr[...]))
        ds_scr[...] = ds
        for h in range(GDN_HEADS):
            dgate_ref[:, 128 * h:128 * h + 128] = dg3[h]
        dsm_ref[...] = dsm + dsm_in_ref[...]
        ddtb_ref[0:1, :] += ddtb
        dalog_ref[0:1, :] += dalog
        dnw_ref[0:1, :] += dnw
        ranges = [(base + 128 * h, base + 128 * h + 128) for base in (0, 1024, 2048) for h in range(GDN_HEADS)]
        dlist = [d[h] for d in (dq, dk, dv) for h in range(GDN_HEADS)]
        _conv_bwd(dlist, ranges, dbuf, carry, pbuf, cw_ref, dqkv_ref, dcw_ref, None, i == 0)

    rblk = lambda w: pl.BlockSpec((CHUNK, w), lambda i: (nc - 1 - i, 0))
    acc = lambda w: pl.BlockSpec((8, w), lambda i: (0, 0))
    f32 = jnp.float32
    return _pc(
        body, name="gdn_bwd", grid=(nc,),
        in_specs=[rblk(1024), rblk(3072), _halo_spec(3072, lambda i: nc - 1 - i), rblk(128),
                  pl.BlockSpec((1, 8, 128, 128), lambda i: (nc - 1 - i, 0, 0, 0)), rblk(1024), rblk(128),
                  _full((CONV_K, 3072)), _full((1, 128)), _full((1, 128)), _full((1, 128)),
                  _full((64, 64)), _full((8, 128, 128)), _full((8, 128, 128)), _full((64, 64)), _full((64, 64))],
        out_specs=[rblk(1024), rblk(3072), rblk(128), acc(3072), acc(128), acc(128), acc(128)],
        out_shape=[jax.ShapeDtypeStruct((t, 1024), f32), jax.ShapeDtypeStruct((t, 3072), f32),
                   jax.ShapeDtypeStruct((t, 128), f32), jax.ShapeDtypeStruct((8, 3072), f32),
                   jax.ShapeDtypeStruct((8, 128), f32), jax.ShapeDtypeStruct((8, 128), f32),
                   jax.ShapeDtypeStruct((8, 128), f32)],
        scratch_shapes=[pltpu.VMEM((72, 3072), f32), pltpu.VMEM((72, 3072), f32), pltpu.VMEM((8, 3072), f32),
                        pltpu.VMEM((8, 128, 128), f32)],
        compiler_params=_cparams(("arbitrary",)),
    )(gate, qkv, qkv, sm, ss, do, dsm_ssd, conv_w, dtb, alog, nw,
      cs["tri"], cs["e_a"], cs["e_b"], cs["i64"], cs["strict"])


def out_fwd_bwd(x, tgt, y_ssd, y_gdn, w_out, fnw):
    t = x.shape[0]
    tm = min(512, t)
    f32 = jnp.float32

    def body(x_ref, tgt_ref, ys_ref, yg_ref, w_ref, fnw_ref,
             dout_ref, dys_ref, dyg_ref, gw_ref, gfnw_ref, loss_ref):
        i = pl.program_id(0)

        @pl.when(i == 0)
        def _():
            gw_ref[...] = jnp.zeros_like(gw_ref)
            gfnw_ref[...] = jnp.zeros_like(gfnw_ref)
            loss_ref[...] = jnp.zeros_like(loss_ref)

        ys = ys_ref[...]
        yg = yg_ref[...]
        out = x_ref[...] + jnp.dot(ys, w_ref[0:1024, :], preferred_element_type=f32) \
            + jnp.dot(yg, w_ref[1024:2048, :], preferred_element_type=f32)
        rstd = lax.rsqrt(jnp.mean(out * out, axis=-1, keepdims=True) + EPS)
        yhat = out * rstd
        fw = fnw_ref[...]
        e = yhat * fw - tgt_ref[...]
        loss_ref[...] += 0.5 * jnp.sum(jnp.sum(e * e, axis=-1, keepdims=True) * (1.0 / D_MODEL), axis=0, keepdims=True)
        dyf = e * (1.0 / D_MODEL)
        gfnw_ref[0:1, :] += jnp.sum(dyf * yhat, axis=0, keepdims=True)
        dyhat = dyf * fw
        dout = rstd * (dyhat - yhat * jnp.mean(dyhat * yhat, axis=-1, keepdims=True))
        dout_ref[...] = dout
        db = dout.astype(_MM)
        dys_ref[...] = lax.dot_general(db, w_ref[0:1024, :], (((1,), (1,)), ((), ())), preferred_element_type=f32)
        dyg_ref[...] = lax.dot_general(db, w_ref[1024:2048, :], (((1,), (1,)), ((), ())), preferred_element_type=f32)
        gw_ref[0:1024, :] += lax.dot_general(ys, db, (((0,), (0,)), ((), ())), preferred_element_type=f32)
        gw_ref[1024:2048, :] += lax.dot_general(yg, db, (((0,), (0,)), ((), ())), preferred_element_type=f32)

    blk = pl.BlockSpec((tm, D_MODEL), lambda i: (i, 0))
    return _pc(
        body, name="out_fwd_bwd", grid=(t // tm,),
        in_specs=[blk, blk, blk, blk, _full((MIX_WIDTH, D_MODEL)), _full((1, D_MODEL))],
        out_specs=[blk, blk, blk, _full((MIX_WIDTH, D_MODEL)), _full((8, D_MODEL)), _full((1, 128))],
        out_shape=[jax.ShapeDtypeStruct((t, D_MODEL), f32)] * 3 +
                  [jax.ShapeDtypeStruct((MIX_WIDTH, D_MODEL), f32), jax.ShapeDtypeStruct((8, D_MODEL), f32),
                   jax.ShapeDtypeStruct((1, 128), f32)],
        compiler_params=_cparams(("arbitrary",)),
    )(x, tgt, y_ssd, y_gdn, w_out, fnw)


def inproj_bwd_dx(x, dout, norm_w, w_perm, dgroups):
    t = x.shape[0]
    tm = min(256, t)
    f32 = jnp.float32

    def body(x_ref, dout_ref, nw_ref, w_ref, dz_ref, dxbc_ref, dgate_ref, dqkv_ref, dsm_ref, dx_ref, gnw_ref):
        i = pl.program_id(0)

        @pl.when(i == 0)
        def _():
            gnw_ref[...] = jnp.zeros_like(gnw_ref)

        du = None
        for (name, c0, c1), d_ref in zip(GROUPS, (dz_ref, dxbc_ref, dgate_ref, dqkv_ref, dsm_ref)):
            term = lax.dot_general(d_ref[...].astype(_MM), w_ref[:, c0:c1], (((1,), (1,)), ((), ())),
                                   preferred_element_type=f32)
            du = term if du is None else du + term
        xf = x_ref[...]
        rstd = lax.rsqrt(jnp.mean(xf * xf, axis=-1, keepdims=True) + EPS)
        xhat = xf * rstd
        gnw_ref[0:1, :] += jnp.sum(du * xhat, axis=0, keepdims=True)
        dxh = du * nw_ref[...]
        dx_ref[...] = dout_ref[...] + rstd * (dxh - xhat * jnp.mean(dxh * xhat, axis=-1, keepdims=True))

    blk = lambda w: pl.BlockSpec((tm, w), lambda i: (i, 0))
    return _pc(
        body, name="inproj_bwd_dx", grid=(t // tm,),
        in_specs=[blk(D_MODEL), blk(D_MODEL), _full((1, D_MODEL)), _full((D_MODEL, PERM_DIM))] +
                 [blk(c1 - c0) for _, c0, c1 in GROUPS],
        out_specs=[blk(D_MODEL), _full((8, D_MODEL))],
        out_shape=[jax.ShapeDtypeStruct((t, D_MODEL), f32), jax.ShapeDtypeStruct((8, D_MODEL), f32)],
        compiler_params=_cparams(("arbitrary",)),
    )(x, dout, norm_w, w_perm, *dgroups)


def grad_w_group(u, dg, name):
    t, n = dg.shape
    tn = 512 if n % 512 == 0 else n
    tm = 1024 if t % 1024 == 0 else t
    f32 = jnp.float32

    def body(u_ref, d_ref, o_ref):
        @pl.when(pl.program_id(1) == 0)
        def _():
            o_ref[...] = jnp.zeros_like(o_ref)

        o_ref[...] += lax.dot_general(u_ref[...], d_ref[...].astype(_MM), (((0,), (0,)), ((), ())),
                                      preferred_element_type=f32)

    return _pc(
        body, name=name, grid=(n // tn, t // tm),
        in_specs=[pl.BlockSpec((tm, D_MODEL), lambda j, k: (k, 0)), pl.BlockSpec((tm, tn), lambda j, k: (k, j))],
        out_specs=pl.BlockSpec((D_MODEL, tn), lambda j, k: (0, j)),
        out_shape=jax.ShapeDtypeStruct((D_MODEL, n), f32),
        compiler_params=_cparams(("arbitrary", "arbitrary")),
    )(u, dg)


def _pad_lanes(v, off):
    n = v.shape[-1]
    return jnp.pad(v.reshape(1, n).astype(jnp.float32), ((0, 0), (off, 128 - off - n)))


def perm_w_in(w_full):
    z = w_full[:, 0:1024]
    xbc = w_full[:, 1024:2560]
    dt = w_full[:, 2560:2576]
    gate = w_full[:, 2576:3600]
    qkv = w_full[:, 3600:6672]
    ab = w_full[:, 6672:6688]
    pad = jnp.zeros((w_full.shape[0], PERM_DIM - IN_DIM), w_full.dtype)
    return jnp.concatenate([z, xbc, gate, qkv, dt, ab, pad], axis=1)


def unperm_w_in(gz, gxbc, ggate, gqkv, gsm):
    return jnp.concatenate([gz, gxbc, gsm[:, 0:16], ggate, gqkv, gsm[:, 16:32]], axis=1)


def local_step(x, tgt, w_perm, w_out, norm_w, ssd_conv_w, ssd_conv_b, ssd_dt_bias, ssd_a_log, ssd_d, ssd_norm_w,
               gdn_conv_w, gdn_dt_bias, gdn_a_log, gdn_norm_w, final_norm_w):
    cs = _consts()
    dtb_s = _pad_lanes(ssd_dt_bias, 0)
    alog_s = _pad_lanes(ssd_a_log, 0)
    dpar = _pad_lanes(ssd_d, 0)
    dtb_g = _pad_lanes(gdn_dt_bias, 16)
    alog_g = _pad_lanes(gdn_a_log, 16)
    nw_g = gdn_norm_w.reshape(1, 128)
    nw_s = ssd_norm_w.reshape(1, 1024)
    cb_s = ssd_conv_b.reshape(1, 1536)

    u, z, xbc, gate, qkv, sm = inproj_fwd(x, norm_w.reshape(1, D_MODEL), w_perm)
    y_ssd, hs = ssd_fwd(z, xbc, sm, ssd_conv_w, cb_s, dtb_s, alog_s, dpar, nw_s, cs)
    y_gdn, ss = gdn_fwd(gate, qkv, sm, gdn_conv_w, dtb_g, alog_g, nw_g, cs)
    dout, dys, dyg, g_wout, g_fnw, loss = out_fwd_bwd(x, tgt, y_ssd, y_gdn, w_out, final_norm_w.reshape(1, D_MODEL))
    dz, dxbc, dsm_s, g_cw_s, g_cb_s, g_dtb_s, g_alog_s, g_d, g_nw_s = ssd_bwd(
        z, xbc, sm, hs, dys, ssd_conv_w, cb_s, dtb_s, alog_s, dpar, nw_s, cs)
    dgate, dqkv, dsm, g_cw_g, g_dtb_g, g_alog_g, g_nw_g = gdn_bwd(
        gate, qkv, sm, ss, dyg, dsm_s, gdn_conv_w, dtb_g, alog_g, nw_g, cs)
    dgroups = (dz, dxbc, dgate, dqkv, dsm)
    dx, g_nw = inproj_bwd_dx(x, dout, norm_w.reshape(1, D_MODEL), w_perm, dgroups)
    gws = [grad_w_group(u, dg, "grad_w_in_" + name) for dg, (name, _, _) in zip(dgroups, GROUPS)]
    g_w_in = unperm_w_in(*gws)
    grads = dict(
        norm_w=g_nw[0:1, :], w_in=g_w_in, ssd_conv_w=g_cw_s[0:4, :], ssd_conv_b=g_cb_s[0:1, :],
        ssd_dt_bias=g_dtb_s[0:1, 0:16], ssd_a_log=g_alog_s[0:1, 0:16], ssd_d=g_d[0:1, 0:16],
        ssd_norm_w=g_nw_s[0:1, :], gdn_conv_w=g_cw_g[0:4, :], gdn_dt_bias=g_dtb_g[0:1, 16:24],
        gdn_a_log=g_alog_g[0:1, 16:24], gdn_norm_w=g_nw_g[0:1, :], w_out=g_wout, final_norm_w=g_fnw[0:1, :])
    return loss, dx, grads


MESH = pl.DeviceIdType.MESH
ANY = pl.BlockSpec(memory_space=pl.ANY)


def _pc_comm(body, **kw):
    return pl.pallas_call(body, **kw)


def _me():
    x, y, c = lax.axis_index("x"), lax.axis_index("y"), lax.axis_index("c")
    return x, y, c, 4 * x + 2 * y + c


def _peer(r):
    x, y, c, _ = _me()
    px = 1 - x if r & 4 else x
    py = 1 - y if r & 2 else y
    pc = 1 - c if r & 1 else c
    return (px, py, pc), 4 * px + 2 * py + pc


def all_to_all(arrs, name):
    n = len(arrs)

    def body(*refs):
        ins, outs = refs[:n], refs[n:2 * n]
        send_sems, recv_sems, local_sems = refs[2 * n:]
        me = _me()[3]
        local = [pltpu.make_async_copy(ins[a].at[me], outs[a].at[me], local_sems.at[a]) for a in range(n)]
        for cp in local:
            cp.start()
        sends = []
        for a in range(n):
            for r in range(1, N_DEV):
                peer, pidx = _peer(r)
                cp = pltpu.make_async_remote_copy(src_ref=ins[a].at[pidx], dst_ref=outs[a].at[me],
                                                  send_sem=send_sems.at[a, r - 1], recv_sem=recv_sems.at[a, r - 1],
                                                  device_id=peer, device_id_type=MESH)
                cp.start()
                sends.append(cp)
        for a in range(n):
            for r in range(1, N_DEV):
                peer, pidx = _peer(r)
                pltpu.make_async_remote_copy(src_ref=ins[a].at[pidx], dst_ref=outs[a].at[pidx],
                                             send_sem=send_sems.at[a, r - 1], recv_sem=recv_sems.at[a, r - 1],
                                             device_id=peer, device_id_type=MESH).wait_recv()
        for cp in sends:
            cp.wait_send()
        for cp in local:
            cp.wait()

    return _pc_comm(
        body, name=name, in_specs=[ANY] * n, out_specs=[ANY] * n,
        out_shape=[jax.ShapeDtypeStruct(a.shape, a.dtype) for a in arrs],
        scratch_shapes=[pltpu.SemaphoreType.DMA((n, N_DEV - 1)), pltpu.SemaphoreType.DMA((n, N_DEV - 1)),
                        pltpu.SemaphoreType.DMA((n,))],
    )(*arrs)


def all_gather(arrs, name):
    n = len(arrs)

    def body(*refs):
        ins, outs = refs[:n], refs[n:2 * n]
        send_sems, recv_sems, local_sems = refs[2 * n:]
        x, y, c, me = _me()
        sibling = (x, y, 1 - c)
        chips = [(1 - x, y), (x, 1 - y), (1 - x, 1 - y)]

        def idx(px, py, pc):
            return 4 * px + 2 * py + pc

        def copy(a, k, block, to, src=None):
            slot = outs[a].at[idx(*block)]
            return pltpu.make_async_remote_copy(src_ref=slot if src is None else src, dst_ref=slot,
                                                send_sem=send_sems.at[a, k], recv_sem=recv_sems.at[a, k],
                                                device_id=to, device_id_type=MESH)

        local = [pltpu.make_async_copy(ins[a], outs[a].at[me], local_sems.at[a]) for a in range(n)]
        for cp in local:
            cp.start()
        started = []
        for a in range(n):
            first = [copy(a, 0, (x, y, c), sibling, src=ins[a])]
            first += [copy(a, 1 + j, (x, y, c), (*chip, c), src=ins[a]) for j, chip in enumerate(chips)]
            for cp in first:
                cp.start()
            started += first
        for a in range(n):
            for j, chip in enumerate(chips):
                copy(a, 1 + j, (*chip, c), (x, y, c)).wait_recv()
                fwd = copy(a, 4 + j, (*chip, c), sibling)
                fwd.start()
                started.append(fwd)
        for a in range(n):
            copy(a, 0, sibling, (x, y, c)).wait_recv()
            for j, chip in enumerate(chips):
                copy(a, 4 + j, (*chip, 1 - c), (x, y, c)).wait_recv()
        for cp in started:
            cp.wait_send()
        for cp in local:
            cp.wait()

    return _pc_comm(
        body, name=name, in_specs=[ANY] * n, out_specs=[ANY] * n,
        out_shape=[jax.ShapeDtypeStruct((N_DEV,) + a.shape, a.dtype) for a in arrs],
        scratch_shapes=[pltpu.SemaphoreType.DMA((n, 7)), pltpu.SemaphoreType.DMA((n, 7)),
                        pltpu.SemaphoreType.DMA((n,))],
    )(*arrs)


def adamw_sum(recv, w, m, v, rows, name):
    r, ccols = w.shape
    f32 = jnp.float32
    c1 = 1.0 / (1.0 - ADAM_B1 ** ADAM_STEP)
    c2 = 1.0 / (1.0 - ADAM_B2 ** ADAM_STEP)

    def body(recv_ref, w_ref, m_ref, v_ref, g_ref, d_ref, mo_ref, vo_ref):
        g = recv_ref[0]
        for k in range(1, N_DEV):
            g = g + recv_ref[k]
        mn = ADAM_B1 * m_ref[...] + (1.0 - ADAM_B1) * g
        vn = ADAM_B2 * v_ref[...] + (1.0 - ADAM_B2) * (g * g)
        g_ref[...] = g
        mo_ref[...] = mn
        vo_ref[...] = vn
        d_ref[...] = -ADAM_LR * ((mn * c1) / (jnp.sqrt(vn * c2) + ADAM_EPS) + ADAM_WD * w_ref[...])

    blk = pl.BlockSpec((rows, ccols), lambda i: (i, 0))
    return _pc(
        body, name=name, grid=(r // rows,),
        in_specs=[pl.BlockSpec((N_DEV, rows, ccols), lambda i: (0, i, 0)), blk, blk, blk],
        out_specs=[blk] * 4, out_shape=[jax.ShapeDtypeStruct((r, ccols), f32)] * 4,
        compiler_params=_cparams(("arbitrary",)),
    )(recv, w, m, v)


REP = (("norm_w", 1024), ("ssd_conv_b", 1536), ("ssd_dt_bias", 16), ("ssd_a_log", 16), ("ssd_d", 16),
       ("ssd_norm_w", 1024), ("gdn_dt_bias", 8), ("gdn_a_log", 8), ("gdn_norm_w", 128), ("final_norm_w", 1024))
REP_ROWS = 48
SHARD = (("ssd_conv_w", CONV_K * SSD_CONV_DIM // N_DEV), ("gdn_conv_w", CONV_K * GDN_CONV_DIM // N_DEV))
SHARD_ROWS = 24


def _rows_of(size):
    return -(-size // 128)


def _pack(vals, layout, total_rows):
    parts = []
    for (name, size), val in zip(layout, vals):
        flat = val.reshape(-1).astype(jnp.float32)
        parts.append(jnp.pad(flat, (0, _rows_of(size) * 128 - size)).reshape(-1, 128))
    used = sum(_rows_of(s) for _, s in layout)
    parts.append(jnp.zeros((total_rows - used, 128), jnp.float32))
    return jnp.concatenate(parts, axis=0)


def _unpack(packed, layout, row0=0):
    out, r = {}, row0
    for name, size in layout:
        n = _rows_of(size)
        out[name] = packed[r:r + n].reshape(-1)[:size]
        r += n
    return out


def _conv_slabs(g_full):
    k, ccols = g_full.shape
    return g_full.reshape(k, N_DEV, ccols // N_DEV).transpose(1, 0, 2).reshape(N_DEV, -1)


def _conv_full(gathered_flat, ccols):
    return gathered_flat.reshape(N_DEV, CONV_K, ccols // N_DEV).transpose(1, 0, 2).reshape(CONV_K, ccols)


def kernel(x, norm_w, w_in, ssd_conv_w, ssd_conv_b, ssd_dt_bias, ssd_a_log, ssd_d, ssd_norm_w, gdn_conv_w, gdn_dt_bias, gdn_a_log, gdn_norm_w, w_out, final_norm_w, loss_target, m_norm_w, m_w_in, m_ssd_conv_w, m_ssd_conv_b, m_ssd_dt_bias, m_ssd_a_log, m_ssd_d, m_ssd_norm_w, m_gdn_conv_w, m_gdn_dt_bias, m_gdn_a_log, m_gdn_norm_w, m_w_out, m_final_norm_w, v_norm_w, v_w_in, v_ssd_conv_w, v_ssd_conv_b, v_ssd_dt_bias, v_ssd_a_log, v_ssd_d, v_ssd_norm_w, v_gdn_conv_w, v_gdn_dt_bias, v_gdn_a_log, v_gdn_norm_w, v_w_out, v_final_norm_w):
    f32 = jnp.float32
    w = dict(norm_w=norm_w, w_in=w_in, ssd_conv_w=ssd_conv_w, ssd_conv_b=ssd_conv_b, ssd_dt_bias=ssd_dt_bias,
             ssd_a_log=ssd_a_log, ssd_d=ssd_d, ssd_norm_w=ssd_norm_w, gdn_conv_w=gdn_conv_w, gdn_dt_bias=gdn_dt_bias,
             gdn_a_log=gdn_a_log, gdn_norm_w=gdn_norm_w, w_out=w_out, final_norm_w=final_norm_w)
    m = dict(norm_w=m_norm_w, w_in=m_w_in, ssd_conv_w=m_ssd_conv_w, ssd_conv_b=m_ssd_conv_b, ssd_dt_bias=m_ssd_dt_bias,
             ssd_a_log=m_ssd_a_log, ssd_d=m_ssd_d, ssd_norm_w=m_ssd_norm_w, gdn_conv_w=m_gdn_conv_w,
             gdn_dt_bias=m_gdn_dt_bias, gdn_a_log=m_gdn_a_log, gdn_norm_w=m_gdn_norm_w, w_out=m_w_out,
             final_norm_w=m_final_norm_w)
    v = dict(norm_w=v_norm_w, w_in=v_w_in, ssd_conv_w=v_ssd_conv_w, ssd_conv_b=v_ssd_conv_b, ssd_dt_bias=v_ssd_dt_bias,
             ssd_a_log=v_ssd_a_log, ssd_d=v_ssd_d, ssd_norm_w=v_ssd_norm_w, gdn_conv_w=v_gdn_conv_w,
             gdn_dt_bias=v_gdn_dt_bias, gdn_a_log=v_gdn_a_log, gdn_norm_w=v_gdn_norm_w, w_out=v_w_out,
             final_norm_w=v_final_norm_w)
    names = list(w)
    shapes = {n: w[n].shape for n in names}

    conv_pack = _pack([w["ssd_conv_w"], w["gdn_conv_w"]], SHARD, SHARD_ROWS)
    g_w_in, g_w_out, g_conv = all_gather([w_in[0].astype(_MM), w_out[0].astype(_MM), conv_pack], "gather_weights")
    w_in_full = g_w_in.transpose(1, 0, 2).reshape(D_MODEL, IN_DIM)
    w_perm = perm_w_in(w_in_full)
    w_out_full = g_w_out.reshape(MIX_WIDTH, D_MODEL)
    ssd_cw_full = _conv_full(g_conv[:, 0:6].reshape(N_DEV, -1), SSD_CONV_DIM)
    gdn_cw_full = _conv_full(g_conv[:, 6:18].reshape(N_DEV, -1), GDN_CONV_DIM)

    loss_l, dx, g = local_step(x[0], loss_target[0], w_perm, w_out_full, norm_w, ssd_cw_full, ssd_conv_b,
                               ssd_dt_bias, ssd_a_log, ssd_d, ssd_norm_w, gdn_cw_full, gdn_dt_bias, gdn_a_log,
                               gdn_norm_w, final_norm_w)

    t_w_in = g["w_in"].reshape(D_MODEL, N_DEV, W_IN_SHARD).transpose(1, 0, 2)
    t_w_out = g["w_out"].reshape(N_DEV, MIX_WIDTH // N_DEV, D_MODEL)
    rep = _pack([g[n] for n, _ in REP], REP, REP_ROWS)
    cs = _conv_slabs(g["ssd_conv_w"])
    cg = _conv_slabs(g["gdn_conv_w"])
    shard_rows = jnp.concatenate([cs.reshape(N_DEV, 6, 128), cg.reshape(N_DEV, 12, 128),
                                  jnp.zeros((N_DEV, SHARD_ROWS - 18, 128), f32)], axis=1)
    t_small = jnp.concatenate([jnp.broadcast_to(rep[None], (N_DEV, REP_ROWS, 128)), shard_rows], axis=1)
    r_w_in, r_w_out, r_small = all_to_all([t_w_in, t_w_out, t_small], "scatter_grads")

    o_w_in = adamw_sum(r_w_in, w_in[0], m_w_in[0], v_w_in[0], 128, "adamw_w_in")
    o_w_out = adamw_sum(r_w_out, w_out[0], m_w_out[0], v_w_out[0], 64, "adamw_w_out")
    small = [jnp.concatenate([_pack([d[n] for n, _ in REP], REP, REP_ROWS),
                              _pack([d[n] for n, _ in SHARD], SHARD, SHARD_ROWS)], axis=0) for d in (w, m, v)]
    o_small = adamw_sum(r_small, small[0], small[1], small[2], REP_ROWS + SHARD_ROWS, "adamw_small")

    loss = lax.psum(loss_l[0, 0], ("x", "y", "c"))
    outs = [loss, dx[None]]
    for k in range(4):
        parts = {**_unpack(o_small[k], REP), **_unpack(o_small[k], SHARD, REP_ROWS),
                 "w_in": o_w_in[k], "w_out": o_w_out[k]}
        outs += [parts[n].reshape(shapes[n]) for n in names]
    return tuple(outs)
```

```python
import functools

import jax
import jax.numpy as jnp
import numpy as np
from jax import lax
from jax.experimental import pallas as pl
from jax.experimental.pallas import tpu as pltpu

_MM = jnp.bfloat16

D_MODEL = 1024
CHUNK = 64
CONV_K = 4
EPS = 1e-6
SSD_CONV_DIM = 1536
GDN_HEADS = 8
GDN_DK = 128
GDN_CONV_DIM = 3072
MIX_WIDTH = 2048
IN_DIM = 6688
N_DEV = 8
W_IN_SHARD = IN_DIM // N_DEV
PERM_DIM = 6784
HI = lax.Precision.HIGHEST
HIGH = lax.Precision.HIGH
VMEM_LIMIT = 56 * 1024 * 1024

ADAM_LR = 0.001
ADAM_B1 = 0.9
ADAM_B2 = 0.999
ADAM_EPS = 1e-08
ADAM_WD = 0.01
ADAM_STEP = 10


def _pc(body, **kw):
    return pl.pallas_call(body, **kw)


def _cparams(sem):
    return pltpu.CompilerParams(dimension_semantics=sem, vmem_limit_bytes=VMEM_LIMIT)


def _sig(x):
    return 0.5 * jnp.tanh(0.5 * x) + 0.5


@jax.custom_vjp
def _sigmoid(x):
    return _sig(x)


def _sigmoid_fwd(x):
    s = _sig(x)
    return s, s


def _sigmoid_bwd(s, g):
    return (g * s * (1.0 - s),)


_sigmoid.defvjp(_sigmoid_fwd, _sigmoid_bwd)


@jax.custom_vjp
def _silu(x):
    return x * _sig(x)


def _silu_fwd(x):
    s = _sig(x)
    return x * s, (x, s)


def _silu_bwd(res, g):
    x, s = res
    return (g * (s * (1.0 + x * (1.0 - s))),)


_silu.defvjp(_silu_fwd, _silu_bwd)


def _softplus_impl(x):
    return jnp.maximum(x, 0.0) + jnp.log(1.0 + jnp.exp(-jnp.abs(x)))


@jax.custom_vjp
def _softplus(x):
    return _softplus_impl(x)


def _softplus_fwd(x):
    return _softplus_impl(x), x


def _softplus_bwd(x, g):
    return (g * _sig(x),)


_softplus.defvjp(_softplus_fwd, _softplus_bwd)


def _lane_bcast_impl(x, k):
    return jnp.broadcast_to(x[..., k:k + 1], x.shape)


@functools.partial(jax.custom_vjp, nondiff_argnums=(1,))
def _lane_bcast(x, k):
    return _lane_bcast_impl(x, k)


def _lane_bcast_fwd(x, k):
    return _lane_bcast_impl(x, k), None


def _lane_bcast_bwd(k, _, g):
    lane = lax.broadcasted_iota(jnp.int32, g.shape, g.ndim - 1)
    return (jnp.where(lane == k, jnp.sum(g, axis=-1, keepdims=True), 0.0),)


_lane_bcast.defvjp(_lane_bcast_fwd, _lane_bcast_bwd)


def _mm(a, b):
    return jnp.dot(a.astype(_MM), b.astype(_MM), preferred_element_type=jnp.float32)


def _mm_nt(a, b):
    return lax.dot_general(a.astype(_MM), b.astype(_MM), (((1,), (1,)), ((), ())),
                           preferred_element_type=jnp.float32)


def _mm_tn(a, b):
    return lax.dot_general(a.astype(_MM), b.astype(_MM), (((0,), (0,)), ((), ())),
                           preferred_element_type=jnp.float32)


def _dot_hi(a, b):
    return jnp.dot(a, b, precision=HI, preferred_element_type=jnp.float32)


def _bmm(a, b):
    return lax.dot_general(a.astype(_MM), b.astype(_MM), (((2,), (1,)), ((0,), (0,))),
                           preferred_element_type=jnp.float32)


def _bmm_nt(a, b):
    return lax.dot_general(a.astype(_MM), b.astype(_MM), (((2,), (2,)), ((0,), (0,))),
                           preferred_element_type=jnp.float32)


def _bmm_tn(a, b):
    return lax.dot_general(a.astype(_MM), b.astype(_MM), (((1,), (1,)), ((0,), (0,))),
                           preferred_element_type=jnp.float32)


def _bmm_hi(a, b):
    return lax.dot_general(a, b, (((2,), (1,)), ((0,), (0,))), precision=HIGH, preferred_element_type=jnp.float32)


def _bmm_nt_hi(a, b):
    return lax.dot_general(a, b, (((2,), (2,)), ((0,), (0,))), precision=HIGH, preferred_element_type=jnp.float32)


def _bmm_tn_hi(a, b):
    return lax.dot_general(a, b, (((1,), (1,)), ((0,), (0,))), precision=HIGH, preferred_element_type=jnp.float32)


def _consts():
    l = np.arange(CHUNK)
    tri = (l[:, None] >= l[None, :]).astype(np.float32)
    lane = np.arange(128)
    i2 =(l[:, None] == (lane[None, :] % 64)).astype(np.float32)
    mask2 = (l[:, None] >= (lane[None, :] % 64)).astype(np.float32)
    lo = (lane < 64).astype(np.float32)[None, :]
    i64 = np.eye(CHUNK, dtype=np.float32)
    strict = (l[:, None] > l[None, :]).astype(np.float32)
    return dict(tri=jnp.asarray(tri), i2=jnp.asarray(i2), mask2=jnp.asarray(mask2), lo=jnp.asarray(lo),
                i64=jnp.asarray(i64), strict=jnp.asarray(strict))


def _ssd_chunk(xs_pre, b_pre, c_pre, z, sm, ht, dtb, alog, dpar, nw, tri, i2, mask2, lo):
    lane = lax.broadcasted_iota(jnp.int32, (1, 128), 1)
    m16 = lane < 16
    dt = jnp.where(m16, _softplus(sm + dtb), 0.0)
    a_neg = -jnp.exp(alog)
    cum = _dot_hi(tri, dt * a_neg)
    row = lax.broadcasted_iota(jnp.int32, (CHUNK, 1), 0)
    is_last = row == CHUNK - 1
    hi = 1.0 - lo
    bm = [_silu(b) for b in b_pre]
    cm = [_silu(c) for c in c_pre]
    cb2 = [_mm_nt(cm[g], jnp.concatenate([bm[g], bm[g]], axis=0)) for g in range(2)]
    yg, ht_next = [], []
    for j in range(8):
        g = j // 4
        pair = lambda v, j=j: jnp.where(lo > 0.5, _lane_bcast(v, 2 * j), _lane_bcast(v, 2 * j + 1))
        xs = _silu(xs_pre[j])
        dte = pair(dt)
        cume = pair(cum)
        cum_last = jnp.sum(jnp.where(is_last, cume, 0.0), axis=0, keepdims=True)
        xdt = xs * dte
        rowv = jnp.sum(cume * i2, axis=0, keepdims=True)
        lm = jnp.exp(jnp.where(mask2 > 0.5, cume - rowv, -jnp.inf))
        m = cb2[g] * lm
        xblk = jnp.concatenate([xdt * lo, xdt * hi], axis=0)
        y = _mm(m, xblk)
        y = y + _mm(cm[g], ht[j]) * jnp.exp(cume)
        y = y + pair(dpar) * xs
        yg.append(y * _silu(z[j]))
        st = _mm_tn(bm[g], xdt * jnp.exp(cum_last - cume))
        ht_next.append(ht[j] * jnp.exp(cum_last) + st)
    outs = []
    for g in range(2):
        ss = sum(jnp.sum(yg[j] * yg[j], axis=-1, keepdims=True) for j in range(4 * g, 4 * g + 4))
        rs = lax.rsqrt(ss * (1.0 / 512.0) + EPS)
        for j in range(4 * g, 4 * g + 4):
            outs.append(yg[j] * rs * nw[j])
    return outs, ht_next


def _tri_inverse(a):
    eye = jnp.eye(CHUNK, dtype=jnp.float32)[None]
    p = eye - a
    ap = a
    for _ in range(5):
        ap = _bmm_hi(ap, ap)
        p = p + _bmm_hi(p, ap)
    return p


@jax.custom_vjp
def _solve(a, r1, r2, t):
    return _bmm_hi(t, r1), _bmm_hi(t, r2)


def _solve_fwd(a, r1, r2, t):
    u, w = _bmm_hi(t, r1), _bmm_hi(t, r2)
    return (u, w), (t, u, w)


def _solve_bwd(res, cts):
    t, u, w = res
    du, dw = cts
    dr1 = _bmm_tn_hi(t, du)
    dr2 = _bmm_tn_hi(t, dw)
    da = -(_bmm_nt_hi(dr1, u) + _bmm_nt_hi(dr2, w))
    return da, dr1, dr2, jnp.zeros_like(t)


_solve.defvjp(_solve_fwd, _solve_bwd)


def _gdn_chunk(q_pre, k_pre, v_pre, gate, sm, s, dtb, alog, nw, tri, i64, strict, t_in=None):
    lane = lax.broadcasted_iota(jnp.int32, (1, 128), 1)
    m_a = (lane >= 16) & (lane < 24)
    g_full = jnp.where(m_a, -jnp.exp(alog) * _softplus(sm + dtb), 0.0)
    gc = _dot_hi(tri, g_full)
    sig = _sigmoid(sm)
    gc3 = jnp.stack([_lane_bcast(gc, 16 + h) for h in range(GDN_HEADS)])
    beta3 = jnp.stack([_lane_bcast(sig, 24 + h) for h in range(GDN_HEADS)])
    q = _silu(q_pre)
    q = q * lax.rsqrt(jnp.sum(q * q, axis=-1, keepdims=True) + EPS) * (GDN_DK ** -0.5)
    k = _silu(k_pre)
    k = k * lax.rsqrt(jnp.sum(k * k, axis=-1, keepdims=True) + EPS)
    v = _silu(v_pre)
    gcl = gc3[:, :, :CHUNK]
    gc_row = jnp.sum(gcl * i64[None], axis=1, keepdims=True)
    incl = (strict + i64)[None] > 0.5
    decay = jnp.exp(jnp.where(incl, gcl - gc_row, -jnp.inf))
    kb = k * beta3
    a = jnp.where(strict[None] > 0.5, _bmm_nt(kb, k) * decay, 0.0)
    egc = jnp.exp(gc3)
    t = _tri_inverse(a) if t_in is None else t_in
    u, w = _solve(a, v * beta3, kb * egc, t)
    attn = _bmm_nt(q, k) * decay
    row = lax.broadcasted_iota(jnp.int32, (1, CHUNK, 1), 1)
    gl = jnp.sum(jnp.where(row == CHUNK - 1, gc3, 0.0), axis=1, keepdims=True)
    q_dec = q * egc
    k_dec = k * jnp.exp(gl - gc3)
    v_new = u - _bmm(w, s)
    o = _bmm(q_dec, s) + _bmm(attn, v_new)
    s_next = s * jnp.exp(gl) + _bmm_tn(k_dec, v_new)
    on = o * lax.rsqrt(jnp.mean(o * o, axis=-1, keepdims=True) + EPS) * nw
    return on * _silu(gate), s_next, t


def _conv_fwd(pbuf, w_ref, c0, c1):
    acc = None
    for j in range(CONV_K):
        term = w_ref[j:j + 1, c0:c1] * pbuf[5 + j:69 + j, c0:c1]
        acc = term if acc is None else acc + term
    return acc


GROUPS = (("z", 0, 1024), ("xbc", 1024, 2560), ("gate", 2560, 3584), ("qkv", 3584, 6656), ("sm", 6656, 6784))


def inproj_fwd(x, norm_w, w_perm):
    t = x.shape[0]
    tm = min(256, t)

    def body(x_ref, nw_ref, w_ref, u_ref, z_ref, xbc_ref, gate_ref, qkv_ref, sm_ref):
        xf = x_ref[...]
        rstd = lax.rsqrt(jnp.mean(xf * xf, axis=-1, keepdims=True) + EPS)
        u = (xf * rstd * nw_ref[...]).astype(_MM)
        u_ref[...] = u
        for (name, c0, c1), o_ref in zip(GROUPS, (z_ref, xbc_ref, gate_ref, qkv_ref, sm_ref)):
            o_ref[...] = jnp.dot(u, w_ref[:, c0:c1], preferred_element_type=jnp.float32)

    outs = [jax.ShapeDtypeStruct((t, D_MODEL), _MM)] + [jax.ShapeDtypeStruct((t, c1 - c0), jnp.float32)
                                                        for _, c0, c1 in GROUPS]
    return _pc(
        body, name="inproj_fwd", grid=(t // tm,),
        in_specs=[pl.BlockSpec((tm, D_MODEL), lambda i: (i, 0)),
                  pl.BlockSpec((1, D_MODEL), lambda i: (0, 0)),
                  pl.BlockSpec((D_MODEL, PERM_DIM), lambda i: (0, 0))],
        out_specs=[pl.BlockSpec((tm, D_MODEL), lambda i: (i, 0))] +
                  [pl.BlockSpec((tm, c1 - c0), lambda i: (i, 0)) for _, c0, c1 in GROUPS],
        out_shape=outs, compiler_params=_cparams(("arbitrary",)),
    )(x, norm_w, w_perm)


def _halo_spec(width, idx_fn):
    return pl.BlockSpec((8, width), lambda i: (jnp.maximum(idx_fn(i) * 8 - 1, 0), 0))


def _full(shape):
    nd = len(shape)
    return pl.BlockSpec(shape, lambda i: (0,) * nd)


def _ssd_split(pre_fn, z_ref, sm_ref):
    xs_pre = [pre_fn(128 * j, 128 * j + 128) for j in range(8)]
    b_pre = [pre_fn(1024 + 128 * g, 1152 + 128 * g) for g in range(2)]
    c_pre = [pre_fn(1280 + 128 * g, 1408 + 128 * g) for g in range(2)]
    z = [z_ref[:, 128 * j:128 * j + 128] for j in range(8)]
    return xs_pre, b_pre, c_pre, z, sm_ref[...]


def ssd_fwd(z, xbc, sm, conv_w, conv_b, dtb, alog, dpar, nw, cs):
    t = z.shape[0]
    nc = t // CHUNK

    def body(z_ref, xbc_ref, halo_ref, sm_ref, cw_ref, cb_ref, dtb_ref, alog_ref, dpar_ref, nw_ref,
             tri_ref, i2_ref, mask2_ref, lo_ref, y_ref, hs_ref, pbuf, ht_scr):
        i = pl.program_id(0)

        @pl.when(i == 0)
        def _():
            ht_scr[...] = jnp.zeros_like(ht_scr)

        pbuf[0:8, :] = jnp.where(i == 0, 0.0, halo_ref[...])
        pbuf[8:72, :] = xbc_ref[...]
        pre_fn = lambda c0, c1: _conv_fwd(pbuf, cw_ref, c0, c1) + cb_ref[:, c0:c1]
        xs_pre, b_pre, c_pre, zz, smv = _ssd_split(pre_fn, z_ref, sm_ref)
        ht = [ht_scr[:, 128 * j:128 * j + 128] for j in range(8)]
        hs_ref[0] = ht_scr[...]
        nwl = [nw_ref[:, 128 * j:128 * j + 128] for j in range(8)]
        outs, ht_next = _ssd_chunk(xs_pre, b_pre, c_pre, zz, smv, ht, dtb_ref[...], alog_ref[...], dpar_ref[...],
                                   nwl, tri_ref[...], i2_ref[...], mask2_ref[...], lo_ref[...])
        for j in range(8):
            y_ref[:, 128 * j:128 * j + 128] = outs[j].astype(y_ref.dtype)
            ht_scr[:, 128 * j:128 * j + 128] = ht_next[j]

    blk = lambda w: pl.BlockSpec((CHUNK, w), lambda i: (i, 0))
    return _pc(
        body, name="ssd_fwd", grid=(nc,),
        in_specs=[blk(1024), blk(1536), _halo_spec(1536, lambda i: i), blk(128),
                  _full((CONV_K, 1536)), _full((1, 1536)), _full((1, 128)), _full((1, 128)), _full((1, 128)),
                  _full((1, 1024)), _full((64, 64)), _full((64, 128)), _full((64, 128)),
                  _full((1, 128))],
        out_specs=[blk(1024), pl.BlockSpec((1, 128, 1024), lambda i: (i, 0, 0))],
        out_shape=[jax.ShapeDtypeStruct((t, 1024), _MM), jax.ShapeDtypeStruct((nc, 128, 1024), jnp.float32)],
        scratch_shapes=[pltpu.VMEM((72, 1536), jnp.float32), pltpu.VMEM((128, 1024), jnp.float32)],
        compiler_params=_cparams(("arbitrary",)),
    )(z, xbc, xbc, sm, conv_w, conv_b, dtb, alog, dpar, nw, cs["tri"], cs["i2"], cs["mask2"], cs["lo"])


def _conv_bwd(dpre_list, col_ranges, dbuf, carry, pbuf, cw_ref, dx_ref, dcw_ref, dcb_ref, first):
    for dpre, (c0, c1) in zip(dpre_list, col_ranges):
        dbuf[0:64, c0:c1] = dpre
    dbuf[64:72, :] = jnp.where(first, 0.0, carry[...])
    carry[...] = dbuf[0:8, :]
    for (c0, c1) in col_ranges:
        acc = None
        for j in range(CONV_K):
            term = cw_ref[j:j + 1, c0:c1] * dbuf[3 - j:67 - j, c0:c1]
            acc = term if acc is None else acc + term
        dx_ref[:, c0:c1] = acc
        dpre = dbuf[0:64, c0:c1]
        for j in range(CONV_K):
            dcw_ref[j:j + 1, c0:c1] += jnp.sum(dpre * pbuf[5 + j:69 + j, c0:c1], axis=0, keepdims=True)
        if dcb_ref is not None:
            dcb_ref[0:1, c0:c1] += jnp.sum(dpre, axis=0, keepdims=True)


def ssd_bwd(z, xbc, sm, hs, dy, conv_w, conv_b, dtb, alog, dpar, nw, cs):
    t = z.shape[0]
    nc = t // CHUNK

    def body(z_ref, xbc_ref, halo_ref, sm_ref, hs_ref, dy_ref, cw_ref, cb_ref, dtb_ref, alog_ref, dpar_ref, nw_ref,
             tri_ref, i2_ref, mask2_ref, lo_ref,
             dz_ref, dxbc_ref, dsm_ref, dcw_ref, dcb_ref, ddtb_ref, dalog_ref, ddpar_ref, dnw_ref,
             pbuf, dbuf, carry, dht_scr):
        i = pl.program_id(0)
        c = nc - 1 - i

        @pl.when(i == 0)
        def _():
            dht_scr[...] = jnp.zeros_like(dht_scr)
            dcw_ref[...] = jnp.zeros_like(dcw_ref)
            dcb_ref[...] = jnp.zeros_like(dcb_ref)
            ddtb_ref[...] = jnp.zeros_like(ddtb_ref)
            dalog_ref[...] = jnp.zeros_like(dalog_ref)
            ddpar_ref[...] = jnp.zeros_like(ddpar_ref)
            dnw_ref[...] = jnp.zeros_like(dnw_ref)

        pbuf[0:8, :] = jnp.where(c == 0, 0.0, halo_ref[...])
        pbuf[8:72, :] = xbc_ref[...]
        pre_fn = lambda c0, c1: _conv_fwd(pbuf, cw_ref, c0, c1) + cb_ref[:, c0:c1]
        xs_pre, b_pre, c_pre, zz, smv = _ssd_split(pre_fn, z_ref, sm_ref)
        ht = [hs_ref[0, :, 128 * j:128 * j + 128] for j in range(8)]
        nwl = [nw_ref[:, 128 * j:128 * j + 128] for j in range(8)]
        consts = (tri_ref[...], i2_ref[...], mask2_ref[...], lo_ref[...])

        def f(xs_pre, b_pre, c_pre, zz, smv, ht, dtb, alog, dpar, nwl):
            return _ssd_chunk(xs_pre, b_pre, c_pre, zz, smv, ht, dtb, alog, dpar, nwl, *consts)

        _, vjp = jax.vjp(f, xs_pre, b_pre, c_pre, zz, smv, ht, dtb_ref[...], alog_ref[...], dpar_ref[...], nwl)
        dys = [dy_ref[:, 128 * j:128 * j + 128] for j in range(8)]
        dhts = [dht_scr[:, 128 * j:128 * j + 128] for j in range(8)]
        dxs, db, dc, dzz, dsm, dht, ddtb, dalog, ddpar, dnwl = vjp((dys, dhts))
        for j in range(8):
            dz_ref[:, 128 * j:128 * j + 128] = dzz[j]
            dht_scr[:, 128 * j:128 * j + 128] = dht[j]
            dnw_ref[0:1, 128 * j:128 * j + 128] += dnwl[j]
        dsm_ref[...] = dsm
        ddtb_ref[0:1, :] += ddtb
        dalog_ref[0:1, :] += dalog
        ddpar_ref[0:1, :] += ddpar
        ranges = ([(128 * j, 128 * j + 128) for j in range(8)] + [(1024 + 128 * g, 1152 + 128 * g) for g in range(2)]
                  + [(1280 + 128 * g, 1408 + 128 * g) for g in range(2)])
        _conv_bwd(dxs + db + dc, ranges, dbuf, carry, pbuf, cw_ref, dxbc_ref, dcw_ref, dcb_ref, i == 0)

    rblk = lambda w: pl.BlockSpec((CHUNK, w), lambda i: (nc - 1 - i, 0))
    acc = lambda w: pl.BlockSpec((8, w), lambda i: (0, 0))
    f32 = jnp.float32
    return _pc(
        body, name="ssd_bwd", grid=(nc,),
        in_specs=[rblk(1024), rblk(1536), _halo_spec(1536, lambda i: nc - 1 - i), rblk(128),
                  pl.BlockSpec((1, 128, 1024), lambda i: (nc - 1 - i, 0, 0)), rblk(1024),
                  _full((CONV_K, 1536)), _full((1, 1536)), _full((1, 128)), _full((1, 128)), _full((1, 128)),
                  _full((1, 1024)), _full((64, 64)), _full((64, 128)), _full((64, 128)),
                  _full((1, 128))],
        out_specs=[rblk(1024), rblk(1536), rblk(128), acc(1536), acc(1536), acc(128), acc(128), acc(128), acc(1024)],
        out_shape=[jax.ShapeDtypeStruct((t, 1024), f32), jax.ShapeDtypeStruct((t, 1536), f32),
                   jax.ShapeDtypeStruct((t, 128), f32), jax.ShapeDtypeStruct((8, 1536), f32),
                   jax.ShapeDtypeStruct((8, 1536), f32), jax.ShapeDtypeStruct((8, 128), f32),
                   jax.ShapeDtypeStruct((8, 128), f32), jax.ShapeDtypeStruct((8, 128), f32),
                   jax.ShapeDtypeStruct((8, 1024), f32)],
        scratch_shapes=[pltpu.VMEM((72, 1536), f32), pltpu.VMEM((72, 1536), f32), pltpu.VMEM((8, 1536), f32),
                        pltpu.VMEM((128, 1024), f32)],
        compiler_params=_cparams(("arbitrary",)),
    )(z, xbc, xbc, sm, hs, dy, conv_w, conv_b, dtb, alog, dpar, nw,
      cs["tri"], cs["i2"], cs["mask2"], cs["lo"])


def _gdn_split(pbuf, cw_ref, gate_ref):
    def heads(base):
        return jnp.stack([_conv_fwd(pbuf, cw_ref, base + 128 * h, base + 128 * h + 128) for h in range(GDN_HEADS)])
    gate = jnp.stack([gate_ref[:, 128 * h:128 * h + 128] for h in range(GDN_HEADS)])
    return heads(0), heads(1024), heads(2048), gate


def gdn_fwd(gate, qkv, sm, conv_w, dtb, alog, nw, cs):
    t = gate.shape[0]
    nc = t // CHUNK

    def body(gate_ref, qkv_ref, halo_ref, sm_ref, cw_ref, dtb_ref, alog_ref, nw_ref,
             tri_ref, i64_ref, strict_ref, o_ref, ss_ref, ts_ref, pbuf, s_scr):
        i = pl.program_id(0)

        @pl.when(i == 0)
        def _():
            s_scr[...] = jnp.zeros_like(s_scr)

        pbuf[0:8, :] = jnp.where(i == 0, 0.0, halo_ref[...])
        pbuf[8:72, :] = qkv_ref[...]
        q_pre, k_pre, v_pre, g3 = _gdn_split(pbuf, cw_ref, gate_ref)
        s = s_scr[...]
        ss_ref[0] = s
        out, s_next, tinv = _gdn_chunk(q_pre, k_pre, v_pre, g3, sm_ref[...], s, dtb_ref[...], alog_ref[...],
                                       nw_ref[...], tri_ref[...], i64_ref[...], strict_ref[...])
        ts_ref[0] = tinv
        s_scr[...] = s_next
        for h in range(GDN_HEADS):
            o_ref[:, 128 * h:128 * h + 128] = out[h].astype(o_ref.dtype)

    blk = lambda w: pl.BlockSpec((CHUNK, w), lambda i: (i, 0))
    return _pc(
        body, name="gdn_fwd", grid=(nc,),
        in_specs=[blk(1024), blk(3072), _halo_spec(3072, lambda i: i), blk(128),
                  _full((CONV_K, 3072)), _full((1, 128)), _full((1, 128)), _full((1, 128)),
                  _full((64, 64)), _full((64, 64)), _full((64, 64))],
        out_specs=[blk(1024), pl.BlockSpec((1, 8, 128, 128), lambda i: (i, 0, 0, 0)),
                   pl.BlockSpec((1, 8, CHUNK, CHUNK), lambda i: (i, 0, 0, 0))],
        out_shape=[jax.ShapeDtypeStruct((t, 1024), _MM), jax.ShapeDtypeStruct((nc, 8, 128, 128), jnp.float32),
                   jax.ShapeDtypeStruct((nc, 8, CHUNK, CHUNK), jnp.float32)],
        scratch_shapes=[pltpu.VMEM((72, 3072), jnp.float32), pltpu.VMEM((8, 128, 128), jnp.float32)],
        compiler_params=_cparams(("arbitrary",)),
    )(gate, qkv, qkv, sm, conv_w, dtb, alog, nw, cs["tri"], cs["i64"], cs["strict"])


def gdn_bwd(gate, qkv, sm, ss, ts, do, dsm_ssd, conv_w, dtb, alog, nw, cs):
    t = gate.shape[0]
    nc = t // CHUNK

    def body(gate_ref, qkv_ref, halo_ref, sm_ref, ss_ref, ts_ref, do_ref, dsm_in_ref, cw_ref, dtb_ref, alog_ref,
             nw_ref, tri_ref, i64_ref, strict_ref,
             dgate_ref, dqkv_ref, dsm_ref, dcw_ref, ddtb_ref, dalog_ref, dnw_ref,
             pbuf, dbuf, carry, ds_scr):
        i = pl.program_id(0)
        c = nc - 1 - i

        @pl.when(i == 0)
        def _():
            ds_scr[...] = jnp.zeros_like(ds_scr)
            dcw_ref[...] = jnp.zeros_like(dcw_ref)
            ddtb_ref[...] = jnp.zeros_like(ddtb_ref)
            dalog_ref[...] = jnp.zeros_like(dalog_ref)
            dnw_ref[...] = jnp.zeros_like(dnw_ref)

        pbuf[0:8, :] = jnp.where(c == 0, 0.0, halo_ref[...])
        pbuf[8:72, :] = qkv_ref[...]
        q_pre, k_pre, v_pre, g3 = _gdn_split(pbuf, cw_ref, gate_ref)
        consts = (tri_ref[...], i64_ref[...], strict_ref[...], ts_ref[0])

        def f(q_pre, k_pre, v_pre, g3, smv, s, dtb, alog, nwv):
            return _gdn_chunk(q_pre, k_pre, v_pre, g3, smv, s, dtb, alog, nwv, *consts)[:2]

        _, vjp = jax.vjp(f, q_pre, k_pre, v_pre, g3, sm_ref[...], ss_ref[0], dtb_ref[...], alog_ref[...], nw_ref[...])
        do3 = jnp.stack([do_ref[:, 128 * h:128 * h + 128] for h in range(GDN_HEADS)])
        dq, dk, dv, dg3, dsm, ds, ddtb, dalog, dnw = vjp((do3, ds_scr[...]))
        ds_scr[...] = ds
        for h in range(GDN_HEADS):
            dgate_ref[:, 128 * h:128 * h + 128] = dg3[h]
        dsm_ref[...] = dsm + dsm_in_ref[...]
        ddtb_ref[0:1, :] += ddtb
        dalog_ref[0:1, :] += dalog
        dnw_ref[0:1, :] += dnw
        ranges = [(base + 128 * h, base + 128 * h + 128) for base in (0, 1024, 2048) for h in range(GDN_HEADS)]
        dlist = [d[h] for d in (dq, dk, dv) for h in range(GDN_HEADS)]
        _conv_bwd(dlist, ranges, dbuf, carry, pbuf, cw_ref, dqkv_ref, dcw_ref, None, i == 0)

    rblk = lambda w: pl.BlockSpec((CHUNK, w), lambda i: (nc - 1 - i, 0))
    acc = lambda w: pl.BlockSpec((8, w), lambda i: (0, 0))
    f32 = jnp.float32
    return _pc(
        body, name="gdn_bwd", grid=(nc,),
        in_specs=[rblk(1024), rblk(3072), _halo_spec(3072, lambda i: nc - 1 - i), rblk(128),
                  pl.BlockSpec((1, 8, 128, 128), lambda i: (nc - 1 - i, 0, 0, 0)),
                  pl.BlockSpec((1, 8, CHUNK, CHUNK), lambda i: (nc - 1 - i, 0, 0, 0)), rblk(1024), rblk(128),
                  _full((CONV_K, 3072)), _full((1, 128)), _full((1, 128)), _full((1, 128)),
                  _full((64, 64)), _full((64, 64)), _full((64, 64))],
        out_specs=[rblk(1024), rblk(3072), rblk(128), acc(3072), acc(128), acc(128), acc(128)],
        out_shape=[jax.ShapeDtypeStruct((t, 1024), f32), jax.ShapeDtypeStruct((t, 3072), f32),
                   jax.ShapeDtypeStruct((t, 128), f32), jax.ShapeDtypeStruct((8, 3072), f32),
                   jax.ShapeDtypeStruct((8, 128), f32), jax.ShapeDtypeStruct((8, 128), f32),
                   jax.ShapeDtypeStruct((8, 128), f32)],
        scratch_shapes=[pltpu.VMEM((72, 3072), f32), pltpu.VMEM((72, 3072), f32), pltpu.VMEM((8, 3072), f32),
                        pltpu.VMEM((8, 128, 128), f32)],
        compiler_params=_cparams(("arbitrary",)),
    )(gate, qkv, qkv, sm, ss, ts, do, dsm_ssd, conv_w, dtb, alog, nw, cs["tri"], cs["i64"], cs["strict"])


def out_fwd_bwd(x, tgt, y_ssd, y_gdn, w_out, fnw):
    t = x.shape[0]
    tm = min(512, t)
    f32 = jnp.float32

    def body(x_ref, tgt_ref, ys_ref, yg_ref, w_ref, fnw_ref,
             dout_ref, dys_ref, dyg_ref, gw_ref, gfnw_ref, loss_ref):
        i = pl.program_id(0)

        @pl.when(i == 0)
        def _():
            gw_ref[...] = jnp.zeros_like(gw_ref)
            gfnw_ref[...] = jnp.zeros_like(gfnw_ref)
            loss_ref[...] = jnp.zeros_like(loss_ref)

        ys = ys_ref[...]
        yg = yg_ref[...]
        out = x_ref[...] + jnp.dot(ys, w_ref[0:1024, :], preferred_element_type=f32) \
            + jnp.dot(yg, w_ref[1024:2048, :], preferred_element_type=f32)
        rstd = lax.rsqrt(jnp.mean(out * out, axis=-1, keepdims=True) + EPS)
        yhat = out * rstd
        fw = fnw_ref[...]
        e = yhat * fw - tgt_ref[...]
        loss_ref[...] += 0.5 * jnp.sum(jnp.sum(e * e, axis=-1, keepdims=True) * (1.0 / D_MODEL), axis=0, keepdims=True)
        dyf = e * (1.0 / D_MODEL)
        gfnw_ref[0:1, :] += jnp.sum(dyf * yhat, axis=0, keepdims=True)
        dyhat = dyf * fw
        dout = rstd * (dyhat - yhat * jnp.mean(dyhat * yhat, axis=-1, keepdims=True))
        dout_ref[...] = dout
        db = dout.astype(_MM)
        dys_ref[...] = lax.dot_general(db, w_ref[0:1024, :], (((1,), (1,)), ((), ())), preferred_element_type=f32)
        dyg_ref[...] = lax.dot_general(db, w_ref[1024:2048, :], (((1,), (1,)), ((), ())), preferred_element_type=f32)
        gw_ref[0:1024, :] += lax.dot_general(ys, db, (((0,), (0,)), ((), ())), preferred_element_type=f32)
        gw_ref[1024:2048, :] += lax.dot_general(yg, db, (((0,), (0,)), ((), ())), preferred_element_type=f32)

    blk = pl.BlockSpec((tm, D_MODEL), lambda i: (i, 0))
    return _pc(
        body, name="out_fwd_bwd", grid=(t // tm,),
        in_specs=[blk, blk, blk, blk, _full((MIX_WIDTH, D_MODEL)), _full((1, D_MODEL))],
        out_specs=[blk, blk, blk, _full((MIX_WIDTH, D_MODEL)), _full((8, D_MODEL)), _full((1, 128))],
        out_shape=[jax.ShapeDtypeStruct((t, D_MODEL), f32)] * 3 +
                  [jax.ShapeDtypeStruct((MIX_WIDTH, D_MODEL), f32), jax.ShapeDtypeStruct((8, D_MODEL), f32),
                   jax.ShapeDtypeStruct((1, 128), f32)],
        compiler_params=_cparams(("arbitrary",)),
    )(x, tgt, y_ssd, y_gdn, w_out, fnw)


def inproj_bwd_dx(x, dout, norm_w, w_perm, dgroups):
    t = x.shape[0]
    tm = min(256, t)
    f32 = jnp.float32

    def body(x_ref, dout_ref, nw_ref, w_ref, dz_ref, dxbc_ref, dgate_ref, dqkv_ref, dsm_ref, dx_ref, gnw_ref):
        i = pl.program_id(0)

        @pl.when(i == 0)
        def _():
            gnw_ref[...] = jnp.zeros_like(gnw_ref)

        du = None
        for (name, c0, c1), d_ref in zip(GROUPS, (dz_ref, dxbc_ref, dgate_ref, dqkv_ref, dsm_ref)):
            term = lax.dot_general(d_ref[...].astype(_MM), w_ref[:, c0:c1], (((1,), (1,)), ((), ())),
                                   preferred_element_type=f32)
            du = term if du is None else du + term
        xf = x_ref[...]
        rstd = lax.rsqrt(jnp.mean(xf * xf, axis=-1, keepdims=True) + EPS)
        xhat = xf * rstd
        gnw_ref[0:1, :] += jnp.sum(du * xhat, axis=0, keepdims=True)
        dxh = du * nw_ref[...]
        dx_ref[...] = dout_ref[...] + rstd * (dxh - xhat * jnp.mean(dxh * xhat, axis=-1, keepdims=True))

    blk = lambda w: pl.BlockSpec((tm, w), lambda i: (i, 0))
    return _pc(
        body, name="inproj_bwd_dx", grid=(t // tm,),
        in_specs=[blk(D_MODEL), blk(D_MODEL), _full((1, D_MODEL)), _full((D_MODEL, PERM_DIM))] +
                 [blk(c1 - c0) for _, c0, c1 in GROUPS],
        out_specs=[blk(D_MODEL), _full((8, D_MODEL))],
        out_shape=[jax.ShapeDtypeStruct((t, D_MODEL), f32), jax.ShapeDtypeStruct((8, D_MODEL), f32)],
        compiler_params=_cparams(("arbitrary",)),
    )(x, dout, norm_w, w_perm, *dgroups)


def grad_w_group(u, dg, name):
    t, n = dg.shape
    tn = 512 if n % 512 == 0 else n
    tm = 1024 if t % 1024 == 0 else t
    f32 = jnp.float32

    def body(u_ref, d_ref, o_ref):
        @pl.when(pl.program_id(1) == 0)
        def _():
            o_ref[...] = jnp.zeros_like(o_ref)

        o_ref[...] += lax.dot_general(u_ref[...], d_ref[...].astype(_MM), (((0,), (0,)), ((), ())),
                                      preferred_element_type=f32)

    return _pc(
        body, name=name, grid=(n // tn, t // tm),
        in_specs=[pl.BlockSpec((tm, D_MODEL), lambda j, k: (k, 0)), pl.BlockSpec((tm, tn), lambda j, k: (k, j))],
        out_specs=pl.BlockSpec((D_MODEL, tn), lambda j, k: (0, j)),
        out_shape=jax.ShapeDtypeStruct((D_MODEL, n), f32),
        compiler_params=_cparams(("arbitrary", "arbitrary")),
    )(u, dg)


def _pad_lanes(v, off):
    n = v.shape[-1]
    return jnp.pad(v.reshape(1, n).astype(jnp.float32), ((0, 0), (off, 128 - off - n)))


def perm_w_in(w_full):
    z = w_full[:, 0:1024]
    xbc = w_full[:, 1024:2560]
    dt = w_full[:, 2560:2576]
    gate = w_full[:, 2576:3600]
    qkv = w_full[:, 3600:6672]
    ab = w_full[:, 6672:6688]
    pad = jnp.zeros((w_full.shape[0], PERM_DIM - IN_DIM), w_full.dtype)
    return jnp.concatenate([z, xbc, gate, qkv, dt, ab, pad], axis=1)


def unperm_w_in(gz, gxbc, ggate, gqkv, gsm):
    return jnp.concatenate([gz, gxbc, gsm[:, 0:16], ggate, gqkv, gsm[:, 16:32]], axis=1)


def local_step(x, tgt, w_perm, w_out, norm_w, ssd_conv_w, ssd_conv_b, ssd_dt_bias, ssd_a_log, ssd_d, ssd_norm_w,
               gdn_conv_w, gdn_dt_bias, gdn_a_log, gdn_norm_w, final_norm_w):
    cs = _consts()
    dtb_s = _pad_lanes(ssd_dt_bias, 0)
    alog_s = _pad_lanes(ssd_a_log, 0)
    dpar = _pad_lanes(ssd_d, 0)
    dtb_g = _pad_lanes(gdn_dt_bias, 16)
    alog_g = _pad_lanes(gdn_a_log, 16)
    nw_g = gdn_norm_w.reshape(1, 128)
    nw_s = ssd_norm_w.reshape(1, 1024)
    cb_s = ssd_conv_b.reshape(1, 1536)

    u, z, xbc, gate, qkv, sm = inproj_fwd(x, norm_w.reshape(1, D_MODEL), w_perm)
    y_ssd, hs = ssd_fwd(z, xbc, sm, ssd_conv_w, cb_s, dtb_s, alog_s, dpar, nw_s, cs)
    y_gdn, ss, ts = gdn_fwd(gate, qkv, sm, gdn_conv_w, dtb_g, alog_g, nw_g, cs)
    dout, dys, dyg, g_wout, g_fnw, loss = out_fwd_bwd(x, tgt, y_ssd, y_gdn, w_out, final_norm_w.reshape(1, D_MODEL))
    dz, dxbc, dsm_s, g_cw_s, g_cb_s, g_dtb_s, g_alog_s, g_d, g_nw_s = ssd_bwd(
        z, xbc, sm, hs, dys, ssd_conv_w, cb_s, dtb_s, alog_s, dpar, nw_s, cs)
    dgate, dqkv, dsm, g_cw_g, g_dtb_g, g_alog_g, g_nw_g = gdn_bwd(
        gate, qkv, sm, ss, ts, dyg, dsm_s, gdn_conv_w, dtb_g, alog_g, nw_g, cs)
    dgroups = (dz, dxbc, dgate, dqkv, dsm)
    dx, g_nw = inproj_bwd_dx(x, dout, norm_w.reshape(1, D_MODEL), w_perm, dgroups)
    gws = [grad_w_group(u, dg, "grad_w_in_" + name) for dg, (name, _, _) in zip(dgroups, GROUPS)]
    g_w_in = unperm_w_in(*gws)
    grads = dict(
        norm_w=g_nw[0:1, :], w_in=g_w_in, ssd_conv_w=g_cw_s[0:4, :], ssd_conv_b=g_cb_s[0:1, :],
        ssd_dt_bias=g_dtb_s[0:1, 0:16], ssd_a_log=g_alog_s[0:1, 0:16], ssd_d=g_d[0:1, 0:16],
        ssd_norm_w=g_nw_s[0:1, :], gdn_conv_w=g_cw_g[0:4, :], gdn_dt_bias=g_dtb_g[0:1, 16:24],
        gdn_a_log=g_alog_g[0:1, 16:24], gdn_norm_w=g_nw_g[0:1, :], w_out=g_wout, final_norm_w=g_fnw[0:1, :])
    return loss, dx, grads


MESH = pl.DeviceIdType.MESH
ANY = pl.BlockSpec(memory_space=pl.ANY)


def _pc_comm(body, **kw):
    return pl.pallas_call(body, **kw)


def _me():
    x, y, c = lax.axis_index("x"), lax.axis_index("y"), lax.axis_index("c")
    return x, y, c, 4 * x + 2 * y + c


def _peer(r):
    x, y, c, _ = _me()
    px = 1 - x if r & 4 else x
    py = 1 - y if r & 2 else y
    pc = 1 - c if r & 1 else c
    return (px, py, pc), 4 * px + 2 * py + pc


def all_to_all(arrs, name):
    n = len(arrs)

    def body(*refs):
        ins, outs = refs[:n], refs[n:2 * n]
        send_sems, recv_sems, local_sems = refs[2 * n:]
        me = _me()[3]
        local = [pltpu.make_async_copy(ins[a].at[me], outs[a].at[me], local_sems.at[a]) for a in range(n)]
        for cp in local:
            cp.start()
        sends = []
        for a in range(n):
            for r in range(1, N_DEV):
                peer, pidx = _peer(r)
                cp = pltpu.make_async_remote_copy(src_ref=ins[a].at[pidx], dst_ref=outs[a].at[me],
                                                  send_sem=send_sems.at[a, r - 1], recv_sem=recv_sems.at[a, r - 1],
                                                  device_id=peer, device_id_type=MESH)
                cp.start()
                sends.append(cp)
        for a in range(n):
            for r in range(1, N_DEV):
                peer, pidx = _peer(r)
                pltpu.make_async_remote_copy(src_ref=ins[a].at[pidx], dst_ref=outs[a].at[pidx],
                                             send_sem=send_sems.at[a, r - 1], recv_sem=recv_sems.at[a, r - 1],
                                             device_id=peer, device_id_type=MESH).wait_recv()
        for cp in sends:
            cp.wait_send()
        for cp in local:
            cp.wait()

    return _pc_comm(
        body, name=name, in_specs=[ANY] * n, out_specs=[ANY] * n,
        out_shape=[jax.ShapeDtypeStruct(a.shape, a.dtype) for a in arrs],
        scratch_shapes=[pltpu.SemaphoreType.DMA((n, N_DEV - 1)), pltpu.SemaphoreType.DMA((n, N_DEV - 1)),
                        pltpu.SemaphoreType.DMA((n,))],
    )(*arrs)


def all_gather(arrs, name):
    n = len(arrs)

    def body(*refs):
        ins, outs = refs[:n], refs[n:2 * n]
        send_sems, recv_sems, local_sems = refs[2 * n:]
        x, y, c, me = _me()
        sibling = (x, y, 1 - c)
        chips = [(1 - x, y), (x, 1 - y), (1 - x, 1 - y)]

        def idx(px, py, pc):
            return 4 * px + 2 * py + pc

        def copy(a, k, block, to, src=None):
            slot = outs[a].at[idx(*block)]
            return pltpu.make_async_remote_copy(src_ref=slot if src is None else src, dst_ref=slot,
                                                send_sem=send_sems.at[a, k], recv_sem=recv_sems.at[a, k],
                                                device_id=to, device_id_type=MESH)

        local = [pltpu.make_async_copy(ins[a], outs[a].at[me], local_sems.at[a]) for a in range(n)]
        for cp in local:
            cp.start()
        started = []
        for a in range(n):
            first = [copy(a, 0, (x, y, c), sibling, src=ins[a])]
            first += [copy(a, 1 + j, (x, y, c), (*chip, c), src=ins[a]) for j, chip in enumerate(chips)]
            for cp in first:
                cp.start()
            started += first
        for a in range(n):
            for j, chip in enumerate(chips):
                copy(a, 1 + j, (*chip, c), (x, y, c)).wait_recv()
                fwd = copy(a, 4 + j, (*chip, c), sibling)
                fwd.start()
                started.append(fwd)
        for a in range(n):
            copy(a, 0, sibling, (x, y, c)).wait_recv()
            for j, chip in enumerate(chips):
                copy(a, 4 + j, (*chip, 1 - c), (x, y, c)).wait_recv()
        for cp in started:
            cp.wait_send()
        for cp in local:
            cp.wait()

    return _pc_comm(
        body, name=name, in_specs=[ANY] * n, out_specs=[ANY] * n,
        out_shape=[jax.ShapeDtypeStruct((N_DEV,) + a.shape, a.dtype) for a in arrs],
        scratch_shapes=[pltpu.SemaphoreType.DMA((n, 7)), pltpu.SemaphoreType.DMA((n, 7)),
                        pltpu.SemaphoreType.DMA((n,))],
    )(*arrs)


def adamw_sum(recv, w, m, v, rows, name):
    r, ccols = w.shape
    f32 = jnp.float32
    c1 = 1.0 / (1.0 - ADAM_B1 ** ADAM_STEP)
    c2 = 1.0 / (1.0 - ADAM_B2 ** ADAM_STEP)

    def body(recv_ref, w_ref, m_ref, v_ref, g_ref, d_ref, mo_ref, vo_ref):
        g = recv_ref[0].astype(f32)
        for k in range(1, N_DEV):
            g = g + recv_ref[k].astype(f32)
        mn = ADAM_B1 * m_ref[...] + (1.0 - ADAM_B1) * g
        vn = ADAM_B2 * v_ref[...] + (1.0 - ADAM_B2) * (g * g)
        g_ref[...] = g
        mo_ref[...] = mn
        vo_ref[...] = vn
        d_ref[...] = -ADAM_LR * ((mn * c1) / (jnp.sqrt(vn * c2) + ADAM_EPS) + ADAM_WD * w_ref[...])

    blk = pl.BlockSpec((rows, ccols), lambda i: (i, 0))
    return _pc(
        body, name=name, grid=(r // rows,),
        in_specs=[pl.BlockSpec((N_DEV, rows, ccols), lambda i: (0, i, 0)), blk, blk, blk],
        out_specs=[blk] * 4, out_shape=[jax.ShapeDtypeStruct((r, ccols), f32)] * 4,
        compiler_params=_cparams(("arbitrary",)),
    )(recv, w, m, v)


REP = (("norm_w", 1024), ("ssd_conv_b", 1536), ("ssd_dt_bias", 16), ("ssd_a_log", 16), ("ssd_d", 16),
       ("ssd_norm_w", 1024), ("gdn_dt_bias", 8), ("gdn_a_log", 8), ("gdn_norm_w", 128), ("final_norm_w", 1024))
REP_ROWS = 48
SHARD = (("ssd_conv_w", CONV_K * SSD_CONV_DIM // N_DEV), ("gdn_conv_w", CONV_K * GDN_CONV_DIM // N_DEV))
SHARD_ROWS = 24


def _rows_of(size):
    return -(-size // 128)


def _pack(vals, layout, total_rows):
    parts = []
    for (name, size), val in zip(layout, vals):
        flat = val.reshape(-1).astype(jnp.float32)
        parts.append(jnp.pad(flat, (0, _rows_of(size) * 128 - size)).reshape(-1, 128))
    used = sum(_rows_of(s) for _, s in layout)
    parts.append(jnp.zeros((total_rows - used, 128), jnp.float32))
    return jnp.concatenate(parts, axis=0)


def _unpack(packed, layout, row0=0):
    out, r = {}, row0
    for name, size in layout:
        n = _rows_of(size)
        out[name] = packed[r:r + n].reshape(-1)[:size]
        r += n
    return out


def _conv_slabs(g_full):
    k, ccols = g_full.shape
    return g_full.reshape(k, N_DEV, ccols // N_DEV).transpose(1, 0, 2).reshape(N_DEV, -1)


def _conv_full(gathered_flat, ccols):
    return gathered_flat.reshape(N_DEV, CONV_K, ccols // N_DEV).transpose(1, 0, 2).reshape(CONV_K, ccols)


def kernel(x, norm_w, w_in, ssd_conv_w, ssd_conv_b, ssd_dt_bias, ssd_a_log, ssd_d, ssd_norm_w, gdn_conv_w, gdn_dt_bias, gdn_a_log, gdn_norm_w, w_out, final_norm_w, loss_target, m_norm_w, m_w_in, m_ssd_conv_w, m_ssd_conv_b, m_ssd_dt_bias, m_ssd_a_log, m_ssd_d, m_ssd_norm_w, m_gdn_conv_w, m_gdn_dt_bias, m_gdn_a_log, m_gdn_norm_w, m_w_out, m_final_norm_w, v_norm_w, v_w_in, v_ssd_conv_w, v_ssd_conv_b, v_ssd_dt_bias, v_ssd_a_log, v_ssd_d, v_ssd_norm_w, v_gdn_conv_w, v_gdn_dt_bias, v_gdn_a_log, v_gdn_norm_w, v_w_out, v_final_norm_w):
    f32 = jnp.float32
    w = dict(norm_w=norm_w, w_in=w_in, ssd_conv_w=ssd_conv_w, ssd_conv_b=ssd_conv_b, ssd_dt_bias=ssd_dt_bias,
             ssd_a_log=ssd_a_log, ssd_d=ssd_d, ssd_norm_w=ssd_norm_w, gdn_conv_w=gdn_conv_w, gdn_dt_bias=gdn_dt_bias,
             gdn_a_log=gdn_a_log, gdn_norm_w=gdn_norm_w, w_out=w_out, final_norm_w=final_norm_w)
    m = dict(norm_w=m_norm_w, w_in=m_w_in, ssd_conv_w=m_ssd_conv_w, ssd_conv_b=m_ssd_conv_b, ssd_dt_bias=m_ssd_dt_bias,
             ssd_a_log=m_ssd_a_log, ssd_d=m_ssd_d, ssd_norm_w=m_ssd_norm_w, gdn_conv_w=m_gdn_conv_w,
             gdn_dt_bias=m_gdn_dt_bias, gdn_a_log=m_gdn_a_log, gdn_norm_w=m_gdn_norm_w, w_out=m_w_out,
             final_norm_w=m_final_norm_w)
    v = dict(norm_w=v_norm_w, w_in=v_w_in, ssd_conv_w=v_ssd_conv_w, ssd_conv_b=v_ssd_conv_b, ssd_dt_bias=v_ssd_dt_bias,
             ssd_a_log=v_ssd_a_log, ssd_d=v_ssd_d, ssd_norm_w=v_ssd_norm_w, gdn_conv_w=v_gdn_conv_w,
             gdn_dt_bias=v_gdn_dt_bias, gdn_a_log=v_gdn_a_log, gdn_norm_w=v_gdn_norm_w, w_out=v_w_out,
             final_norm_w=v_final_norm_w)
    names = list(w)
    shapes = {n: w[n].shape for n in names}

    conv_pack = _pack([w["ssd_conv_w"], w["gdn_conv_w"]], SHARD, SHARD_ROWS)
    g_w_in, g_w_out, g_conv = all_gather([w_in[0].astype(_MM), w_out[0].astype(_MM), conv_pack], "gather_weights")
    w_in_full = g_w_in.transpose(1, 0, 2).reshape(D_MODEL, IN_DIM)
    w_perm = perm_w_in(w_in_full)
    w_out_full = g_w_out.reshape(MIX_WIDTH, D_MODEL)
    ssd_cw_full = _conv_full(g_conv[:, 0:6].reshape(N_DEV, -1), SSD_CONV_DIM)
    gdn_cw_full = _conv_full(g_conv[:, 6:18].reshape(N_DEV, -1), GDN_CONV_DIM)

    loss_l, dx, g = local_step(x[0], loss_target[0], w_perm, w_out_full, norm_w, ssd_cw_full, ssd_conv_b,
                               ssd_dt_bias, ssd_a_log, ssd_d, ssd_norm_w, gdn_cw_full, gdn_dt_bias, gdn_a_log,
                               gdn_norm_w, final_norm_w)

    t_w_in = g["w_in"].reshape(D_MODEL, N_DEV, W_IN_SHARD).transpose(1, 0, 2).astype(_MM)
    t_w_out = g["w_out"].reshape(N_DEV, MIX_WIDTH // N_DEV, D_MODEL).astype(_MM)
    rep = _pack([g[n] for n, _ in REP], REP, REP_ROWS)
    cs = _conv_slabs(g["ssd_conv_w"])
    cg = _conv_slabs(g["gdn_conv_w"])
    shard_rows = jnp.concatenate([cs.reshape(N_DEV, 6, 128), cg.reshape(N_DEV, 12, 128),
                                  jnp.zeros((N_DEV, SHARD_ROWS - 18, 128), f32)], axis=1)
    t_small = jnp.concatenate([jnp.broadcast_to(rep[None], (N_DEV, REP_ROWS, 128)), shard_rows], axis=1)
    r_w_in, r_w_out, r_small = all_to_all([t_w_in, t_w_out, t_small], "scatter_grads")

    o_w_in = adamw_sum(r_w_in, w_in[0], m_w_in[0], v_w_in[0], 128, "adamw_w_in")
    o_w_out = adamw_sum(r_w_out, w_out[0], m_w_out[0], v_w_out[0], 64, "adamw_w_out")
    small = [jnp.concatenate([_pack([d[n] for n, _ in REP], REP, REP_ROWS),
                              _pack([d[n] for n, _ in SHARD], SHARD, SHARD_ROWS)], axis=0) for d in (w, m, v)]
    o_small = adamw_sum(r_small, small[0], small[1], small[2], REP_ROWS + SHARD_ROWS, "adamw_small")

    loss = lax.psum(loss_l[0, 0], ("x", "y", "c"))
    outs = [loss, dx[None]]
    for k in range(4):
        parts = {**_unpack(o_small[k], REP), **_unpack(o_small[k], SHARD, REP_ROWS),
                 "w_in": o_w_in[k], "w_out": o_w_out[k]}
        outs += [parts[n].reshape(shapes[n]) for n in names]
    return tuple(outs)
```

```python
import functools

import jax
import jax.numpy as jnp
import numpy as np
from jax import lax
from jax.experimental import pallas as pl
from jax.experimental.pallas import tpu as pltpu

_MM = jnp.bfloat16

D_MODEL = 1024
CHUNK = 64
CONV_K = 4
EPS = 1e-6
SSD_CONV_DIM = 1536
GDN_HEADS = 8
GDN_DK = 128
GDN_CONV_DIM = 3072
MIX_WIDTH = 2048
IN_DIM = 6688
N_DEV = 8
W_IN_SHARD = IN_DIM // N_DEV
PERM_DIM = 6784
HI = lax.Precision.HIGHEST
HIGH = lax.Precision.HIGH
VMEM_LIMIT = 56 * 1024 * 1024

ADAM_LR = 0.001
ADAM_B1 = 0.9
ADAM_B2 = 0.999
ADAM_EPS = 1e-08
ADAM_WD = 0.01
ADAM_STEP = 10


def _pc(body, **kw):
    return pl.pallas_call(body, **kw)


def _cparams(sem):
    return pltpu.CompilerParams(dimension_semantics=sem, vmem_limit_bytes=VMEM_LIMIT)


def _sig(x):
    return 0.5 * jnp.tanh(0.5 * x) + 0.5


@jax.custom_vjp
def _sigmoid(x):
    return _sig(x)


def _sigmoid_fwd(x):
    s = _sig(x)
    return s, s


def _sigmoid_bwd(s, g):
    return (g * s * (1.0 - s),)


_sigmoid.defvjp(_sigmoid_fwd, _sigmoid_bwd)


@jax.custom_vjp
def _silu(x):
    return x * _sig(x)


def _silu_fwd(x):
    s = _sig(x)
    return x * s, (x, s)


def _silu_bwd(res, g):
    x, s = res
    return (g * (s * (1.0 + x * (1.0 - s))),)


_silu.defvjp(_silu_fwd, _silu_bwd)


def _softplus_impl(x):
    return jnp.maximum(x, 0.0) + jnp.log(1.0 + jnp.exp(-jnp.abs(x)))


@jax.custom_vjp
def _softplus(x):
    return _softplus_impl(x)


def _softplus_fwd(x):
    return _softplus_impl(x), x


def _softplus_bwd(x, g):
    return (g * _sig(x),)


_softplus.defvjp(_softplus_fwd, _softplus_bwd)


def _lane_bcast_impl(x, k):
    return jnp.broadcast_to(x[..., k:k + 1], x.shape)


@functools.partial(jax.custom_vjp, nondiff_argnums=(1,))
def _lane_bcast(x, k):
    return _lane_bcast_impl(x, k)


def _lane_bcast_fwd(x, k):
    return _lane_bcast_impl(x, k), None


def _lane_bcast_bwd(k, _, g):
    lane = lax.broadcasted_iota(jnp.int32, g.shape, g.ndim - 1)
    return (jnp.where(lane == k, jnp.sum(g, axis=-1, keepdims=True), 0.0),)


_lane_bcast.defvjp(_lane_bcast_fwd, _lane_bcast_bwd)


def _mm(a, b):
    return jnp.dot(a.astype(_MM), b.astype(_MM), preferred_element_type=jnp.float32)


def _mm_nt(a, b):
    return lax.dot_general(a.astype(_MM), b.astype(_MM), (((1,), (1,)), ((), ())),
                           preferred_element_type=jnp.float32)


def _mm_tn(a, b):
    return lax.dot_general(a.astype(_MM), b.astype(_MM), (((0,), (0,)), ((), ())),
                           preferred_element_type=jnp.float32)


def _dot_hi(a, b):
    return jnp.dot(a, b, precision=HI, preferred_element_type=jnp.float32)


def _bmm(a, b):
    return lax.dot_general(a.astype(_MM), b.astype(_MM), (((2,), (1,)), ((0,), (0,))),
                           preferred_element_type=jnp.float32)


def _bmm_nt(a, b):
    return lax.dot_general(a.astype(_MM), b.astype(_MM), (((2,), (2,)), ((0,), (0,))),
                           preferred_element_type=jnp.float32)


def _bmm_tn(a, b):
    return lax.dot_general(a.astype(_MM), b.astype(_MM), (((1,), (1,)), ((0,), (0,))),
                           preferred_element_type=jnp.float32)


def _bmm_hi(a, b):
    return lax.dot_general(a, b, (((2,), (1,)), ((0,), (0,))), precision=HIGH, preferred_element_type=jnp.float32)


def _bmm_nt_hi(a, b):
    return lax.dot_general(a, b, (((2,), (2,)), ((0,), (0,))), precision=HIGH, preferred_element_type=jnp.float32)


def _bmm_tn_hi(a, b):
    return lax.dot_general(a, b, (((1,), (1,)), ((0,), (0,))), precision=HIGH, preferred_element_type=jnp.float32)


def _consts():
    l = np.arange(CHUNK)
    tri = (l[:, None] >= l[None, :]).astype(np.float32)
    lane = np.arange(128)
    i2 =(l[:, None] == (lane[None, :] % 64)).astype(np.float32)
    mask2 = (l[:, None] >= (lane[None, :] % 64)).astype(np.float32)
    lo = (lane < 64).astype(np.float32)[None, :]
    i64 = np.eye(CHUNK, dtype=np.float32)
    strict = (l[:, None] > l[None, :]).astype(np.float32)
    return dict(tri=jnp.asarray(tri), i2=jnp.asarray(i2), mask2=jnp.asarray(mask2), lo=jnp.asarray(lo),
                i64=jnp.asarray(i64), strict=jnp.asarray(strict))


def _ssd_chunk(xs_pre, b_pre, c_pre, z, sm, ht, dtb, alog, dpar, nw, tri, i2, mask2, lo):
    lane = lax.broadcasted_iota(jnp.int32, (1, 128), 1)
    m16 = lane < 16
    dt = jnp.where(m16, _softplus(sm + dtb), 0.0)
    a_neg = -jnp.exp(alog)
    cum = _dot_hi(tri, dt * a_neg)
    row = lax.broadcasted_iota(jnp.int32, (CHUNK, 1), 0)
    is_last = row == CHUNK - 1
    hi = 1.0 - lo
    bm = [_silu(b) for b in b_pre]
    cm = [_silu(c) for c in c_pre]
    cb2 = [_mm_nt(cm[g], jnp.concatenate([bm[g], bm[g]], axis=0)) for g in range(2)]
    yg, ht_next = [], []
    for j in range(8):
        g = j // 4
        pair = lambda v, j=j: jnp.where(lo > 0.5, _lane_bcast(v, 2 * j), _lane_bcast(v, 2 * j + 1))
        xs = _silu(xs_pre[j])
        dte = pair(dt)
        cume = pair(cum)
        cum_last = jnp.sum(jnp.where(is_last, cume, 0.0), axis=0, keepdims=True)
        xdt = xs * dte
        rowv = jnp.sum(cume * i2, axis=0, keepdims=True)
        lm = jnp.exp(jnp.where(mask2 > 0.5, cume - rowv, -jnp.inf))
        m = cb2[g] * lm
        xblk = jnp.concatenate([xdt * lo, xdt * hi], axis=0)
        y = _mm(m, xblk)
        y = y + _mm(cm[g], ht[j]) * jnp.exp(cume)
        y = y + pair(dpar) * xs
        yg.append(y * _silu(z[j]))
        st = _mm_tn(bm[g], xdt * jnp.exp(cum_last - cume))
        ht_next.append(ht[j] * jnp.exp(cum_last) + st)
    outs = []
    for g in range(2):
        ss = sum(jnp.sum(yg[j] * yg[j], axis=-1, keepdims=True) for j in range(4 * g, 4 * g + 4))
        rs = lax.rsqrt(ss * (1.0 / 512.0) + EPS)
        for j in range(4 * g, 4 * g + 4):
            outs.append(yg[j] * rs * nw[j])
    return outs, ht_next


def _tri_inverse(a):
    eye = jnp.eye(CHUNK, dtype=jnp.float32)[None]
    p = eye - a
    ap = a
    for _ in range(5):
        ap = _bmm_hi(ap, ap)
        p = p + _bmm_hi(p, ap)
    return p


@jax.custom_vjp
def _solve(a, r1, r2, t):
    return _bmm_hi(t, r1), _bmm_hi(t, r2)


def _solve_fwd(a, r1, r2, t):
    u, w = _bmm_hi(t, r1), _bmm_hi(t, r2)
    return (u, w), (t, u, w)


def _solve_bwd(res, cts):
    t, u, w = res
    du, dw = cts
    dr1 = _bmm_tn_hi(t, du)
    dr2 = _bmm_tn_hi(t, dw)
    da = -(_bmm_nt_hi(dr1, u) + _bmm_nt_hi(dr2, w))
    return da, dr1, dr2, jnp.zeros_like(t)


_solve.defvjp(_solve_fwd, _solve_bwd)


def _gdn_chunk(q_pre, k_pre, v_pre, gate, sm, s, dtb, alog, nw, tri, i64, strict, t_in=None):
    lane = lax.broadcasted_iota(jnp.int32, (1, 128), 1)
    m_a = (lane >= 16) & (lane < 24)
    g_full = jnp.where(m_a, -jnp.exp(alog) * _softplus(sm + dtb), 0.0)
    gc = _dot_hi(tri, g_full)
    sig = _sigmoid(sm)
    gc3 = jnp.stack([_lane_bcast(gc, 16 + h) for h in range(GDN_HEADS)])
    beta3 = jnp.stack([_lane_bcast(sig, 24 + h) for h in range(GDN_HEADS)])
    q = _silu(q_pre)
    q = q * lax.rsqrt(jnp.sum(q * q, axis=-1, keepdims=True) + EPS) * (GDN_DK ** -0.5)
    k = _silu(k_pre)
    k = k * lax.rsqrt(jnp.sum(k * k, axis=-1, keepdims=True) + EPS)
    v = _silu(v_pre)
    gcl = gc3[:, :, :CHUNK]
    gc_row = jnp.sum(gcl * i64[None], axis=1, keepdims=True)
    incl = (strict + i64)[None] > 0.5
    decay = jnp.exp(jnp.where(incl, gcl - gc_row, -jnp.inf))
    kb = k * beta3
    a = jnp.where(strict[None] > 0.5, _bmm_nt(kb, k) * decay, 0.0)
    egc = jnp.exp(gc3)
    t = _tri_inverse(a) if t_in is None else t_in
    u, w = _solve(a, v * beta3, kb * egc, t)
    attn = _bmm_nt(q, k) * decay
    row = lax.broadcasted_iota(jnp.int32, (1, CHUNK, 1), 1)
    gl = jnp.sum(jnp.where(row == CHUNK - 1, gc3, 0.0), axis=1, keepdims=True)
    q_dec = q * egc
    k_dec = k * jnp.exp(gl - gc3)
    v_new = u - _bmm(w, s)
    o = _bmm(q_dec, s) + _bmm(attn, v_new)
    s_next = s * jnp.exp(gl) + _bmm_tn(k_dec, v_new)
    on = o * lax.rsqrt(jnp.mean(o * o, axis=-1, keepdims=True) + EPS) * nw
    return on * _silu(gate), s_next, t


def _conv_fwd(pbuf, w_ref, c0, c1):
    acc = None
    for j in range(CONV_K):
        term = w_ref[j:j + 1, c0:c1] * pbuf[5 + j:69 + j, c0:c1]
        acc = term if acc is None else acc + term
    return acc


GROUPS = (("z", 0, 1024), ("xbc", 1024, 2560), ("gate", 2560, 3584), ("qkv", 3584, 6656), ("sm", 6656, 6784))


def inproj_fwd(x, norm_w, w_perm):
    t = x.shape[0]
    tm = min(256, t)

    def body(x_ref, nw_ref, w_ref, u_ref, z_ref, xbc_ref, gate_ref, qkv_ref, sm_ref):
        xf = x_ref[...]
        rstd = lax.rsqrt(jnp.mean(xf * xf, axis=-1, keepdims=True) + EPS)
        u = (xf * rstd * nw_ref[...]).astype(_MM)
        u_ref[...] = u
        for (name, c0, c1), o_ref in zip(GROUPS, (z_ref, xbc_ref, gate_ref, qkv_ref, sm_ref)):
            o_ref[...] = jnp.dot(u, w_ref[:, c0:c1], preferred_element_type=jnp.float32)

    outs = [jax.ShapeDtypeStruct((t, D_MODEL), _MM)] + [jax.ShapeDtypeStruct((t, c1 - c0), jnp.float32)
                                                        for _, c0, c1 in GROUPS]
    return _pc(
        body, name="inproj_fwd", grid=(t // tm,),
        in_specs=[pl.BlockSpec((tm, D_MODEL), lambda i: (i, 0)),
                  pl.BlockSpec((1, D_MODEL), lambda i: (0, 0)),
                  pl.BlockSpec((D_MODEL, PERM_DIM), lambda i: (0, 0))],
        out_specs=[pl.BlockSpec((tm, D_MODEL), lambda i: (i, 0))] +
                  [pl.BlockSpec((tm, c1 - c0), lambda i: (i, 0)) for _, c0, c1 in GROUPS],
        out_shape=outs, compiler_params=_cparams(("arbitrary",)),
    )(x, norm_w, w_perm)


def _halo_spec(width, idx_fn):
    return pl.BlockSpec((8, width), lambda i: (jnp.maximum(idx_fn(i) * 8 - 1, 0), 0))


def _full(shape):
    nd = len(shape)
    return pl.BlockSpec(shape, lambda i: (0,) * nd)


def _ssd_split(pre_fn, z_ref, sm_ref):
    xs_pre = [pre_fn(128 * j, 128 * j + 128) for j in range(8)]
    b_pre = [pre_fn(1024 + 128 * g, 1152 + 128 * g) for g in range(2)]
    c_pre = [pre_fn(1280 + 128 * g, 1408 + 128 * g) for g in range(2)]
    z = [z_ref[:, 128 * j:128 * j + 128] for j in range(8)]
    return xs_pre, b_pre, c_pre, z, sm_ref[...]


def ssd_fwd(z, xbc, sm, conv_w, conv_b, dtb, alog, dpar, nw, cs):
    t = z.shape[0]
    nc = t // CHUNK

    def body(z_ref, xbc_ref, halo_ref, sm_ref, cw_ref, cb_ref, dtb_ref, alog_ref, dpar_ref, nw_ref,
             tri_ref, i2_ref, mask2_ref, lo_ref, y_ref, hs_ref, pre_ref, pbuf, ht_scr):
        i = pl.program_id(0)

        @pl.when(i == 0)
        def _():
            ht_scr[...] = jnp.zeros_like(ht_scr)

        pbuf[0:8, :] = jnp.where(i == 0, 0.0, halo_ref[...])
        pbuf[8:72, :] = xbc_ref[...]

        def pre_fn(c0, c1):
            pre = _conv_fwd(pbuf, cw_ref, c0, c1) + cb_ref[:, c0:c1]
            pre_ref[:, c0:c1] = pre
            return pre

        xs_pre, b_pre, c_pre, zz, smv = _ssd_split(pre_fn, z_ref, sm_ref)
        ht = [ht_scr[:, 128 * j:128 * j + 128] for j in range(8)]
        hs_ref[0] = ht_scr[...]
        nwl = [nw_ref[:, 128 * j:128 * j + 128] for j in range(8)]
        outs, ht_next = _ssd_chunk(xs_pre, b_pre, c_pre, zz, smv, ht, dtb_ref[...], alog_ref[...], dpar_ref[...],
                                   nwl, tri_ref[...], i2_ref[...], mask2_ref[...], lo_ref[...])
        for j in range(8):
            y_ref[:, 128 * j:128 * j + 128] = outs[j].astype(y_ref.dtype)
            ht_scr[:, 128 * j:128 * j + 128] = ht_next[j]

    blk = lambda w: pl.BlockSpec((CHUNK, w), lambda i: (i, 0))
    return _pc(
        body, name="ssd_fwd", grid=(nc,),
        in_specs=[blk(1024), blk(1536), _halo_spec(1536, lambda i: i), blk(128),
                  _full((CONV_K, 1536)), _full((1, 1536)), _full((1, 128)), _full((1, 128)), _full((1, 128)),
                  _full((1, 1024)), _full((64, 64)), _full((64, 128)), _full((64, 128)),
                  _full((1, 128))],
        out_specs=[blk(1024), pl.BlockSpec((1, 128, 1024), lambda i: (i, 0, 0)), blk(1536)],
        out_shape=[jax.ShapeDtypeStruct((t, 1024), _MM), jax.ShapeDtypeStruct((nc, 128, 1024), jnp.float32),
                   jax.ShapeDtypeStruct((t, 1536), jnp.float32)],
        scratch_shapes=[pltpu.VMEM((72, 1536), jnp.float32), pltpu.VMEM((128, 1024), jnp.float32)],
        compiler_params=_cparams(("arbitrary",)),
    )(z, xbc, xbc, sm, conv_w, conv_b, dtb, alog, dpar, nw, cs["tri"], cs["i2"], cs["mask2"], cs["lo"])


def _conv_bwd(dpre_list, col_ranges, dbuf, carry, x_ref, cw_ref, dx_ref, dcw_ref, dcb_ref, first):
    for dpre, (c0, c1) in zip(dpre_list, col_ranges):
        dbuf[0:64, c0:c1] = dpre
    dbuf[64:72, :] = jnp.where(first, 0.0, carry[...])
    carry[...] = dbuf[0:8, :]
    for (c0, c1) in col_ranges:
        xin = x_ref[:, c0:c1]
        acc = None
        for j in range(CONV_K):
            sh = dbuf[3 - j:67 - j, c0:c1]
            term = cw_ref[j:j + 1, c0:c1] * sh
            acc = term if acc is None else acc + term
            dcw_ref[j:j + 1, c0:c1] += jnp.sum(xin * sh, axis=0, keepdims=True)
        dx_ref[:, c0:c1] = acc
        if dcb_ref is not None:
            dcb_ref[0:1, c0:c1] += jnp.sum(dbuf[0:64, c0:c1], axis=0, keepdims=True)


def ssd_bwd(z, xbc, pre, sm, hs, dy, conv_w, dtb, alog, dpar, nw, cs):
    t = z.shape[0]
    nc = t // CHUNK

    def body(z_ref, xbc_ref, pre_ref, sm_ref, hs_ref, dy_ref, cw_ref, dtb_ref, alog_ref, dpar_ref, nw_ref,
             tri_ref, i2_ref, mask2_ref, lo_ref,
             dz_ref, dxbc_ref, dsm_ref, dcw_ref, dcb_ref, ddtb_ref, dalog_ref, ddpar_ref, dnw_ref,
             dbuf, carry, dht_scr):
        i = pl.program_id(0)

        @pl.when(i == 0)
        def _():
            dht_scr[...] = jnp.zeros_like(dht_scr)
            dcw_ref[...] = jnp.zeros_like(dcw_ref)
            dcb_ref[...] = jnp.zeros_like(dcb_ref)
            ddtb_ref[...] = jnp.zeros_like(ddtb_ref)
            dalog_ref[...] = jnp.zeros_like(dalog_ref)
            ddpar_ref[...] = jnp.zeros_like(ddpar_ref)
            dnw_ref[...] = jnp.zeros_like(dnw_ref)

        pre_fn = lambda c0, c1: pre_ref[:, c0:c1]
        xs_pre, b_pre, c_pre, zz, smv = _ssd_split(pre_fn, z_ref, sm_ref)
        ht = [hs_ref[0, :, 128 * j:128 * j + 128] for j in range(8)]
        nwl = [nw_ref[:, 128 * j:128 * j + 128] for j in range(8)]
        consts = (tri_ref[...], i2_ref[...], mask2_ref[...], lo_ref[...])

        def f(xs_pre, b_pre, c_pre, zz, smv, ht, dtb, alog, dpar, nwl):
            return _ssd_chunk(xs_pre, b_pre, c_pre, zz, smv, ht, dtb, alog, dpar, nwl, *consts)

        _, vjp = jax.vjp(f, xs_pre, b_pre, c_pre, zz, smv, ht, dtb_ref[...], alog_ref[...], dpar_ref[...], nwl)
        dys = [dy_ref[:, 128 * j:128 * j + 128] for j in range(8)]
        dhts = [dht_scr[:, 128 * j:128 * j + 128] for j in range(8)]
        dxs, db, dc, dzz, dsm, dht, ddtb, dalog, ddpar, dnwl = vjp((dys, dhts))
        for j in range(8):
            dz_ref[:, 128 * j:128 * j + 128] = dzz[j]
            dht_scr[:, 128 * j:128 * j + 128] = dht[j]
            dnw_ref[0:1, 128 * j:128 * j + 128] += dnwl[j]
        dsm_ref[...] = dsm
        ddtb_ref[0:1, :] += ddtb
        dalog_ref[0:1, :] += dalog
        ddpar_ref[0:1, :] += ddpar
        ranges = ([(128 * j, 128 * j + 128) for j in range(8)] + [(1024 + 128 * g, 1152 + 128 * g) for g in range(2)]
                  + [(1280 + 128 * g, 1408 + 128 * g) for g in range(2)])
        _conv_bwd(dxs + db + dc, ranges, dbuf, carry, xbc_ref, cw_ref, dxbc_ref, dcw_ref, dcb_ref, i == 0)

    rblk = lambda w: pl.BlockSpec((CHUNK, w), lambda i: (nc - 1 - i, 0))
    acc = lambda w: pl.BlockSpec((8, w), lambda i: (0, 0))
    f32 = jnp.float32
    return _pc(
        body, name="ssd_bwd", grid=(nc,),
        in_specs=[rblk(1024), rblk(1536), rblk(1536), rblk(128),
                  pl.BlockSpec((1, 128, 1024), lambda i: (nc - 1 - i, 0, 0)), rblk(1024),
                  _full((CONV_K, 1536)), _full((1, 128)), _full((1, 128)), _full((1, 128)),
                  _full((1, 1024)), _full((64, 64)), _full((64, 128)), _full((64, 128)),
                  _full((1, 128))],
        out_specs=[rblk(1024), rblk(1536), rblk(128), acc(1536), acc(1536), acc(128), acc(128), acc(128), acc(1024)],
        out_shape=[jax.ShapeDtypeStruct((t, 1024), f32), jax.ShapeDtypeStruct((t, 1536), f32),
                   jax.ShapeDtypeStruct((t, 128), f32), jax.ShapeDtypeStruct((8, 1536), f32),
                   jax.ShapeDtypeStruct((8, 1536), f32), jax.ShapeDtypeStruct((8, 128), f32),
                   jax.ShapeDtypeStruct((8, 128), f32), jax.ShapeDtypeStruct((8, 128), f32),
                   jax.ShapeDtypeStruct((8, 1024), f32)],
        scratch_shapes=[pltpu.VMEM((72, 1536), f32), pltpu.VMEM((8, 1536), f32), pltpu.VMEM((128, 1024), f32)],
        compiler_params=_cparams(("arbitrary",)),
    )(z, xbc, pre, sm, hs, dy, conv_w, dtb, alog, dpar, nw, cs["tri"], cs["i2"], cs["mask2"], cs["lo"])


def _gdn_split(pre_fn, gate_ref):
    def heads(base):
        return jnp.stack([pre_fn(base + 128 * h, base + 128 * h + 128) for h in range(GDN_HEADS)])
    gate = jnp.stack([gate_ref[:, 128 * h:128 * h + 128] for h in range(GDN_HEADS)])
    return heads(0), heads(1024), heads(2048), gate


def gdn_fwd(gate, qkv, sm, conv_w, dtb, alog, nw, cs):
    t = gate.shape[0]
    nc = t // CHUNK

    def body(gate_ref, qkv_ref, halo_ref, sm_ref, cw_ref, dtb_ref, alog_ref, nw_ref,
             tri_ref, i64_ref, strict_ref, o_ref, ss_ref, ts_ref, pre_ref, pbuf, s_scr):
        i = pl.program_id(0)

        @pl.when(i == 0)
        def _():
            s_scr[...] = jnp.zeros_like(s_scr)

        pbuf[0:8, :] = jnp.where(i == 0, 0.0, halo_ref[...])
        pbuf[8:72, :] = qkv_ref[...]

        def pre_fn(c0, c1):
            pre = _conv_fwd(pbuf, cw_ref, c0, c1)
            pre_ref[:, c0:c1] = pre
            return pre

        q_pre, k_pre, v_pre, g3 = _gdn_split(pre_fn, gate_ref)
        s = s_scr[...]
        ss_ref[0] = s
        out, s_next, tinv = _gdn_chunk(q_pre, k_pre, v_pre, g3, sm_ref[...], s, dtb_ref[...], alog_ref[...],
                                       nw_ref[...], tri_ref[...], i64_ref[...], strict_ref[...])
        ts_ref[0] = tinv
        s_scr[...] = s_next
        for h in range(GDN_HEADS):
            o_ref[:, 128 * h:128 * h + 128] = out[h].astype(o_ref.dtype)

    blk = lambda w: pl.BlockSpec((CHUNK, w), lambda i: (i, 0))
    return _pc(
        body, name="gdn_fwd", grid=(nc,),
        in_specs=[blk(1024), blk(3072), _halo_spec(3072, lambda i: i), blk(128),
                  _full((CONV_K, 3072)), _full((1, 128)), _full((1, 128)), _full((1, 128)),
                  _full((64, 64)), _full((64, 64)), _full((64, 64))],
        out_specs=[blk(1024), pl.BlockSpec((1, 8, 128, 128), lambda i: (i, 0, 0, 0)),
                   pl.BlockSpec((1, 8, CHUNK, CHUNK), lambda i: (i, 0, 0, 0)), blk(3072)],
        out_shape=[jax.ShapeDtypeStruct((t, 1024), _MM), jax.ShapeDtypeStruct((nc, 8, 128, 128), jnp.float32),
                   jax.ShapeDtypeStruct((nc, 8, CHUNK, CHUNK), jnp.float32),
                   jax.ShapeDtypeStruct((t, 3072), jnp.float32)],
        scratch_shapes=[pltpu.VMEM((72, 3072), jnp.float32), pltpu.VMEM((8, 128, 128), jnp.float32)],
        compiler_params=_cparams(("arbitrary",)),
    )(gate, qkv, qkv, sm, conv_w, dtb, alog, nw, cs["tri"], cs["i64"], cs["strict"])


def gdn_bwd(gate, qkv, pre, sm, ss, ts, do, dsm_ssd, conv_w, dtb, alog, nw, cs):
    t = gate.shape[0]
    nc = t // CHUNK

    def body(gate_ref, qkv_ref, pre_ref, sm_ref, ss_ref, ts_ref, do_ref, dsm_in_ref, cw_ref, dtb_ref, alog_ref,
             nw_ref, tri_ref, i64_ref, strict_ref,
             dgate_ref, dqkv_ref, dsm_ref, dcw_ref, ddtb_ref, dalog_ref, dnw_ref,
             dbuf, carry, ds_scr):
        i = pl.program_id(0)

        @pl.when(i == 0)
        def _():
            ds_scr[...] = jnp.zeros_like(ds_scr)
            dcw_ref[...] = jnp.zeros_like(dcw_ref)
            ddtb_ref[...] = jnp.zeros_like(ddtb_ref)
            dalog_ref[...] = jnp.zeros_like(dalog_ref)
            dnw_ref[...] = jnp.zeros_like(dnw_ref)

        q_pre, k_pre, v_pre, g3 = _gdn_split(lambda c0, c1: pre_ref[:, c0:c1], gate_ref)
        consts = (tri_ref[...], i64_ref[...], strict_ref[...], ts_ref[0])

        def f(q_pre, k_pre, v_pre, g3, smv, s, dtb, alog, nwv):
            return _gdn_chunk(q_pre, k_pre, v_pre, g3, smv, s, dtb, alog, nwv, *consts)[:2]

        _, vjp = jax.vjp(f, q_pre, k_pre, v_pre, g3, sm_ref[...], ss_ref[0], dtb_ref[...], alog_ref[...], nw_ref[...])
        do3 = jnp.stack([do_ref[:, 128 * h:128 * h + 128] for h in range(GDN_HEADS)])
        dq, dk, dv, dg3, dsm, ds, ddtb, dalog, dnw = vjp((do3, ds_scr[...]))
        ds_scr[...] = ds
        for h in range(GDN_HEADS):
            dgate_ref[:, 128 * h:128 * h + 128] = dg3[h]
        dsm_ref[...] = dsm + dsm_in_ref[...]
        ddtb_ref[0:1, :] += ddtb
        dalog_ref[0:1, :] += dalog
        dnw_ref[0:1, :] += dnw
        ranges = [(base + 128 * h, base + 128 * h + 128) for base in (0, 1024, 2048) for h in range(GDN_HEADS)]
        dlist = [d[h] for d in (dq, dk, dv) for h in range(GDN_HEADS)]
        _conv_bwd(dlist, ranges, dbuf, carry, qkv_ref, cw_ref, dqkv_ref, dcw_ref, None, i == 0)

    rblk = lambda w: pl.BlockSpec((CHUNK, w), lambda i: (nc - 1 - i, 0))
    acc = lambda w: pl.BlockSpec((8, w), lambda i: (0, 0))
    f32 = jnp.float32
    return _pc(
        body, name="gdn_bwd", grid=(nc,),
        in_specs=[rblk(1024), rblk(3072), rblk(3072), rblk(128),
                  pl.BlockSpec((1, 8, 128, 128), lambda i: (nc - 1 - i, 0, 0, 0)),
                  pl.BlockSpec((1, 8, CHUNK, CHUNK), lambda i: (nc - 1 - i, 0, 0, 0)), rblk(1024), rblk(128),
                  _full((CONV_K, 3072)), _full((1, 128)), _full((1, 128)), _full((1, 128)),
                  _full((64, 64)), _full((64, 64)), _full((64, 64))],
        out_specs=[rblk(1024), rblk(3072), rblk(128), acc(3072), acc(128), acc(128), acc(128)],
        out_shape=[jax.ShapeDtypeStruct((t, 1024), f32), jax.ShapeDtypeStruct((t, 3072), f32),
                   jax.ShapeDtypeStruct((t, 128), f32), jax.ShapeDtypeStruct((8, 3072), f32),
                   jax.ShapeDtypeStruct((8, 128), f32), jax.ShapeDtypeStruct((8, 128), f32),
                   jax.ShapeDtypeStruct((8, 128), f32)],
        scratch_shapes=[pltpu.VMEM((72, 3072), f32), pltpu.VMEM((8, 3072), f32), pltpu.VMEM((8, 128, 128), f32)],
        compiler_params=_cparams(("arbitrary",)),
    )(gate, qkv, pre, sm, ss, ts, do, dsm_ssd, conv_w, dtb, alog, nw, cs["tri"], cs["i64"], cs["strict"])


def out_fwd_bwd(x, tgt, y_ssd, y_gdn, w_out, fnw):
    t = x.shape[0]
    tm = min(512, t)
    f32 = jnp.float32

    def body(x_ref, tgt_ref, ys_ref, yg_ref, w_ref, fnw_ref,
             dout_ref, dys_ref, dyg_ref, gw_ref, gfnw_ref, loss_ref):
        i = pl.program_id(0)

        @pl.when(i == 0)
        def _():
            gw_ref[...] = jnp.zeros_like(gw_ref)
            gfnw_ref[...] = jnp.zeros_like(gfnw_ref)
            loss_ref[...] = jnp.zeros_like(loss_ref)

        ys = ys_ref[...]
        yg = yg_ref[...]
        out = x_ref[...] + jnp.dot(ys, w_ref[0:1024, :], preferred_element_type=f32) \
            + jnp.dot(yg, w_ref[1024:2048, :], preferred_element_type=f32)
        rstd = lax.rsqrt(jnp.mean(out * out, axis=-1, keepdims=True) + EPS)
        yhat = out * rstd
        fw = fnw_ref[...]
        e = yhat * fw - tgt_ref[...]
        loss_ref[...] += 0.5 * jnp.sum(jnp.sum(e * e, axis=-1, keepdims=True) * (1.0 / D_MODEL), axis=0, keepdims=True)
        dyf = e * (1.0 / D_MODEL)
        gfnw_ref[0:1, :] += jnp.sum(dyf * yhat, axis=0, keepdims=True)
        dyhat = dyf * fw
        dout = rstd * (dyhat - yhat * jnp.mean(dyhat * yhat, axis=-1, keepdims=True))
        dout_ref[...] = dout
        db = dout.astype(_MM)
        dys_ref[...] = lax.dot_general(db, w_ref[0:1024, :], (((1,), (1,)), ((), ())), preferred_element_type=f32)
        dyg_ref[...] = lax.dot_general(db, w_ref[1024:2048, :], (((1,), (1,)), ((), ())), preferred_element_type=f32)
        gw_ref[0:1024, :] += lax.dot_general(ys, db, (((0,), (0,)), ((), ())), preferred_element_type=f32)
        gw_ref[1024:2048, :] += lax.dot_general(yg, db, (((0,), (0,)), ((), ())), preferred_element_type=f32)

    blk = pl.BlockSpec((tm, D_MODEL), lambda i: (i, 0))
    return _pc(
        body, name="out_fwd_bwd", grid=(t // tm,),
        in_specs=[blk, blk, blk, blk, _full((MIX_WIDTH, D_MODEL)), _full((1, D_MODEL))],
        out_specs=[blk, blk, blk, _full((MIX_WIDTH, D_MODEL)), _full((8, D_MODEL)), _full((1, 128))],
        out_shape=[jax.ShapeDtypeStruct((t, D_MODEL), f32)] * 3 +
                  [jax.ShapeDtypeStruct((MIX_WIDTH, D_MODEL), f32), jax.ShapeDtypeStruct((8, D_MODEL), f32),
                   jax.ShapeDtypeStruct((1, 128), f32)],
        compiler_params=_cparams(("arbitrary",)),
    )(x, tgt, y_ssd, y_gdn, w_out, fnw)


def inproj_bwd_dx(x, dout, norm_w, w_perm, dgroups):
    t = x.shape[0]
    tm = min(256, t)
    f32 = jnp.float32

    def body(x_ref, dout_ref, nw_ref, w_ref, dz_ref, dxbc_ref, dgate_ref, dqkv_ref, dsm_ref, dx_ref, gnw_ref):
        i = pl.program_id(0)

        @pl.when(i == 0)
        def _():
            gnw_ref[...] = jnp.zeros_like(gnw_ref)

        du = None
        for (name, c0, c1), d_ref in zip(GROUPS, (dz_ref, dxbc_ref, dgate_ref, dqkv_ref, dsm_ref)):
            term = lax.dot_general(d_ref[...].astype(_MM), w_ref[:, c0:c1], (((1,), (1,)), ((), ())),
                                   preferred_element_type=f32)
            du = term if du is None else du + term
        xf = x_ref[...]
        rstd = lax.rsqrt(jnp.mean(xf * xf, axis=-1, keepdims=True) + EPS)
        xhat = xf * rstd
        gnw_ref[0:1, :] += jnp.sum(du * xhat, axis=0, keepdims=True)
        dxh = du * nw_ref[...]
        dx_ref[...] = dout_ref[...] + rstd * (dxh - xhat * jnp.mean(dxh * xhat, axis=-1, keepdims=True))

    blk = lambda w: pl.BlockSpec((tm, w), lambda i: (i, 0))
    return _pc(
        body, name="inproj_bwd_dx", grid=(t // tm,),
        in_specs=[blk(D_MODEL), blk(D_MODEL), _full((1, D_MODEL)), _full((D_MODEL, PERM_DIM))] +
                 [blk(c1 - c0) for _, c0, c1 in GROUPS],
        out_specs=[blk(D_MODEL), _full((8, D_MODEL))],
        out_shape=[jax.ShapeDtypeStruct((t, D_MODEL), f32), jax.ShapeDtypeStruct((8, D_MODEL), f32)],
        compiler_params=_cparams(("arbitrary",)),
    )(x, dout, norm_w, w_perm, *dgroups)


def grad_w_group(u, dg, name):
    t, n = dg.shape
    tn = 512 if n % 512 == 0 else n
    tm = 1024 if t % 1024 == 0 else t
    f32 = jnp.float32

    def body(u_ref, d_ref, o_ref):
        @pl.when(pl.program_id(1) == 0)
        def _():
            o_ref[...] = jnp.zeros_like(o_ref)

        o_ref[...] += lax.dot_general(u_ref[...], d_ref[...].astype(_MM), (((0,), (0,)), ((), ())),
                                      preferred_element_type=f32)

    return _pc(
        body, name=name, grid=(n // tn, t // tm),
        in_specs=[pl.BlockSpec((tm, D_MODEL), lambda j, k: (k, 0)), pl.BlockSpec((tm, tn), lambda j, k: (k, j))],
        out_specs=pl.BlockSpec((D_MODEL, tn), lambda j, k: (0, j)),
        out_shape=jax.ShapeDtypeStruct((D_MODEL, n), f32),
        compiler_params=_cparams(("arbitrary", "arbitrary")),
    )(u, dg)


def _pad_lanes(v, off):
    n = v.shape[-1]
    return jnp.pad(v.reshape(1, n).astype(jnp.float32), ((0, 0), (off, 128 - off - n)))


def perm_w_in(w_full):
    z = w_full[:, 0:1024]
    xbc = w_full[:, 1024:2560]
    dt = w_full[:, 2560:2576]
    gate = w_full[:, 2576:3600]
    qkv = w_full[:, 3600:6672]
    ab = w_full[:, 6672:6688]
    pad = jnp.zeros((w_full.shape[0], PERM_DIM - IN_DIM), w_full.dtype)
    return jnp.concatenate([z, xbc, gate, qkv, dt, ab, pad], axis=1)


def unperm_w_in(gz, gxbc, ggate, gqkv, gsm):
    return jnp.concatenate([gz, gxbc, gsm[:, 0:16], ggate, gqkv, gsm[:, 16:32]], axis=1)


def local_step(x, tgt, w_perm, w_out, norm_w, ssd_conv_w, ssd_conv_b, ssd_dt_bias, ssd_a_log, ssd_d, ssd_norm_w,
               gdn_conv_w, gdn_dt_bias, gdn_a_log, gdn_norm_w, final_norm_w):
    cs = _consts()
    dtb_s = _pad_lanes(ssd_dt_bias, 0)
    alog_s = _pad_lanes(ssd_a_log, 0)
    dpar = _pad_lanes(ssd_d, 0)
    dtb_g = _pad_lanes(gdn_dt_bias, 16)
    alog_g = _pad_lanes(gdn_a_log, 16)
    nw_g = gdn_norm_w.reshape(1, 128)
    nw_s = ssd_norm_w.reshape(1, 1024)
    cb_s = ssd_conv_b.reshape(1, 1536)

    u, z, xbc, gate, qkv, sm = inproj_fwd(x, norm_w.reshape(1, D_MODEL), w_perm)
    y_ssd, hs, pre_s = ssd_fwd(z, xbc, sm, ssd_conv_w, cb_s, dtb_s, alog_s, dpar, nw_s, cs)
    y_gdn, ss, ts, pre_g = gdn_fwd(gate, qkv, sm, gdn_conv_w, dtb_g, alog_g, nw_g, cs)
    dout, dys, dyg, g_wout, g_fnw, loss = out_fwd_bwd(x, tgt, y_ssd, y_gdn, w_out, final_norm_w.reshape(1, D_MODEL))
    dz, dxbc, dsm_s, g_cw_s, g_cb_s, g_dtb_s, g_alog_s, g_d, g_nw_s = ssd_bwd(
        z, xbc, pre_s, sm, hs, dys, ssd_conv_w, dtb_s, alog_s, dpar, nw_s, cs)
    dgate, dqkv, dsm, g_cw_g, g_dtb_g, g_alog_g, g_nw_g = gdn_bwd(
        gate, qkv, pre_g, sm, ss, ts, dyg, dsm_s, gdn_conv_w, dtb_g, alog_g, nw_g, cs)
    dgroups = (dz, dxbc, dgate, dqkv, dsm)
    dx, g_nw = inproj_bwd_dx(x, dout, norm_w.reshape(1, D_MODEL), w_perm, dgroups)
    gws = [grad_w_group(u, dg, "grad_w_in_" + name) for dg, (name, _, _) in zip(dgroups, GROUPS)]
    g_w_in = unperm_w_in(*gws)
    grads = dict(
        norm_w=g_nw[0:1, :], w_in=g_w_in, ssd_conv_w=g_cw_s[0:4, :], ssd_conv_b=g_cb_s[0:1, :],
        ssd_dt_bias=g_dtb_s[0:1, 0:16], ssd_a_log=g_alog_s[0:1, 0:16], ssd_d=g_d[0:1, 0:16],
        ssd_norm_w=g_nw_s[0:1, :], gdn_conv_w=g_cw_g[0:4, :], gdn_dt_bias=g_dtb_g[0:1, 16:24],
        gdn_a_log=g_alog_g[0:1, 16:24], gdn_norm_w=g_nw_g[0:1, :], w_out=g_wout, final_norm_w=g_fnw[0:1, :])
    return loss, dx, grads


MESH = pl.DeviceIdType.MESH
ANY = pl.BlockSpec(memory_space=pl.ANY)


def _pc_comm(body, **kw):
    return pl.pallas_call(body, **kw)


def _me():
    x, y, c = lax.axis_index("x"), lax.axis_index("y"), lax.axis_index("c")
    return x, y, c, 4 * x + 2 * y + c


def _peer(r):
    x, y, c, _ = _me()
    px = 1 - x if r & 4 else x
    py = 1 - y if r & 2 else y
    pc = 1 - c if r & 1 else c
    return (px, py, pc), 4 * px + 2 * py + pc


def all_to_all(arrs, name):
    n = len(arrs)

    def body(*refs):
        ins, outs = refs[:n], refs[n:2 * n]
        send_sems, recv_sems, local_sems = refs[2 * n:]
        me = _me()[3]
        local = [pltpu.make_async_copy(ins[a].at[me], outs[a].at[me], local_sems.at[a]) for a in range(n)]
        for cp in local:
            cp.start()
        sends = []
        for a in range(n):
            for r in range(1, N_DEV):
                peer, pidx = _peer(r)
                cp = pltpu.make_async_remote_copy(src_ref=ins[a].at[pidx], dst_ref=outs[a].at[me],
                                                  send_sem=send_sems.at[a, r - 1], recv_sem=recv_sems.at[a, r - 1],
                                                  device_id=peer, device_id_type=MESH)
                cp.start()
                sends.append(cp)
        for a in range(n):
            for r in range(1, N_DEV):
                peer, pidx = _peer(r)
                pltpu.make_async_remote_copy(src_ref=ins[a].at[pidx], dst_ref=outs[a].at[pidx],
                                             send_sem=send_sems.at[a, r - 1], recv_sem=recv_sems.at[a, r - 1],
                                             device_id=peer, device_id_type=MESH).wait_recv()
        for cp in sends:
            cp.wait_send()
        for cp in local:
            cp.wait()

    return _pc_comm(
        body, name=name, in_specs=[ANY] * n, out_specs=[ANY] * n,
        out_shape=[jax.ShapeDtypeStruct(a.shape, a.dtype) for a in arrs],
        scratch_shapes=[pltpu.SemaphoreType.DMA((n, N_DEV - 1)), pltpu.SemaphoreType.DMA((n, N_DEV - 1)),
                        pltpu.SemaphoreType.DMA((n,))],
    )(*arrs)


def all_gather(arrs, name):
    n = len(arrs)

    def body(*refs):
        ins, outs = refs[:n], refs[n:2 * n]
        send_sems, recv_sems, local_sems = refs[2 * n:]
        x, y, c, me = _me()
        sibling = (x, y, 1 - c)
        chips = [(1 - x, y), (x, 1 - y), (1 - x, 1 - y)]

        def idx(px, py, pc):
            return 4 * px + 2 * py + pc

        def copy(a, k, block, to, src=None):
            slot = outs[a].at[idx(*block)]
            return pltpu.make_async_remote_copy(src_ref=slot if src is None else src, dst_ref=slot,
                                                send_sem=send_sems.at[a, k], recv_sem=recv_sems.at[a, k],
                                                device_id=to, device_id_type=MESH)

        local = [pltpu.make_async_copy(ins[a], outs[a].at[me], local_sems.at[a]) for a in range(n)]
        for cp in local:
            cp.start()
        started = []
        for a in range(n):
            first = [copy(a, 0, (x, y, c), sibling, src=ins[a])]
            first += [copy(a, 1 + j, (x, y, c), (*chip, c), src=ins[a]) for j, chip in enumerate(chips)]
            for cp in first:
                cp.start()
            started += first
        for a in range(n):
            for j, chip in enumerate(chips):
                copy(a, 1 + j, (*chip, c), (x, y, c)).wait_recv()
                fwd = copy(a, 4 + j, (*chip, c), sibling)
                fwd.start()
                started.append(fwd)
        for a in range(n):
            copy(a, 0, sibling, (x, y, c)).wait_recv()
            for j, chip in enumerate(chips):
                copy(a, 4 + j, (*chip, 1 - c), (x, y, c)).wait_recv()
        for cp in started:
            cp.wait_send()
        for cp in local:
            cp.wait()

    return _pc_comm(
        body, name=name, in_specs=[ANY] * n, out_specs=[ANY] * n,
        out_shape=[jax.ShapeDtypeStruct((N_DEV,) + a.shape, a.dtype) for a in arrs],
        scratch_shapes=[pltpu.SemaphoreType.DMA((n, 7)), pltpu.SemaphoreType.DMA((n, 7)),
                        pltpu.SemaphoreType.DMA((n,))],
    )(*arrs)


def adamw_sum(recv, w, m, v, rows, name):
    r, ccols = w.shape
    f32 = jnp.float32
    c1 = 1.0 / (1.0 - ADAM_B1 ** ADAM_STEP)
    c2 = 1.0 / (1.0 - ADAM_B2 ** ADAM_STEP)

    def body(recv_ref, w_ref, m_ref, v_ref, g_ref, d_ref, mo_ref, vo_ref):
        g = recv_ref[0].astype(f32)
        for k in range(1, N_DEV):
            g = g + recv_ref[k].astype(f32)
        mn = ADAM_B1 * m_ref[...] + (1.0 - ADAM_B1) * g
        vn = ADAM_B2 * v_ref[...] + (1.0 - ADAM_B2) * (g * g)
        g_ref[...] = g
        mo_ref[...] = mn
        vo_ref[...] = vn
        d_ref[...] = -ADAM_LR * ((mn * c1) / (jnp.sqrt(vn * c2) + ADAM_EPS) + ADAM_WD * w_ref[...])

    blk = pl.BlockSpec((rows, ccols), lambda i: (i, 0))
    return _pc(
        body, name=name, grid=(r // rows,),
        in_specs=[pl.BlockSpec((N_DEV, rows, ccols), lambda i: (0, i, 0)), blk, blk, blk],
        out_specs=[blk] * 4, out_shape=[jax.ShapeDtypeStruct((r, ccols), f32)] * 4,
        compiler_params=_cparams(("arbitrary",)),
    )(recv, w, m, v)


REP = (("norm_w", 1024), ("ssd_conv_b", 1536), ("ssd_dt_bias", 16), ("ssd_a_log", 16), ("ssd_d", 16),
       ("ssd_norm_w", 1024), ("gdn_dt_bias", 8), ("gdn_a_log", 8), ("gdn_norm_w", 128), ("final_norm_w", 1024))
REP_ROWS = 48
SHARD = (("ssd_conv_w", CONV_K * SSD_CONV_DIM // N_DEV), ("gdn_conv_w", CONV_K * GDN_CONV_DIM // N_DEV))
SHARD_ROWS = 24


def _rows_of(size):
    return -(-size // 128)


def _pack(vals, layout, total_rows):
    parts = []
    for (name, size), val in zip(layout, vals):
        flat = val.reshape(-1).astype(jnp.float32)
        parts.append(jnp.pad(flat, (0, _rows_of(size) * 128 - size)).reshape(-1, 128))
    used = sum(_rows_of(s) for _, s in layout)
    parts.append(jnp.zeros((total_rows - used, 128), jnp.float32))
    return jnp.concatenate(parts, axis=0)


def _unpack(packed, layout, row0=0):
    out, r = {}, row0
    for name, size in layout:
        n = _rows_of(size)
        out[name] = packed[r:r + n].reshape(-1)[:size]
        r += n
    return out


def _conv_slabs(g_full):
    k, ccols = g_full.shape
    return g_full.reshape(k, N_DEV, ccols // N_DEV).transpose(1, 0, 2).reshape(N_DEV, -1)


def _conv_full(gathered_flat, ccols):
    return gathered_flat.reshape(N_DEV, CONV_K, ccols // N_DEV).transpose(1, 0, 2).reshape(CONV_K, ccols)


def kernel(x, norm_w, w_in, ssd_conv_w, ssd_conv_b, ssd_dt_bias, ssd_a_log, ssd_d, ssd_norm_w, gdn_conv_w, gdn_dt_bias, gdn_a_log, gdn_norm_w, w_out, final_norm_w, loss_target, m_norm_w, m_w_in, m_ssd_conv_w, m_ssd_conv_b, m_ssd_dt_bias, m_ssd_a_log, m_ssd_d, m_ssd_norm_w, m_gdn_conv_w, m_gdn_dt_bias, m_gdn_a_log, m_gdn_norm_w, m_w_out, m_final_norm_w, v_norm_w, v_w_in, v_ssd_conv_w, v_ssd_conv_b, v_ssd_dt_bias, v_ssd_a_log, v_ssd_d, v_ssd_norm_w, v_gdn_conv_w, v_gdn_dt_bias, v_gdn_a_log, v_gdn_norm_w, v_w_out, v_final_norm_w):
    f32 = jnp.float32
    w = dict(norm_w=norm_w, w_in=w_in, ssd_conv_w=ssd_conv_w, ssd_conv_b=ssd_conv_b, ssd_dt_bias=ssd_dt_bias,
             ssd_a_log=ssd_a_log, ssd_d=ssd_d, ssd_norm_w=ssd_norm_w, gdn_conv_w=gdn_conv_w, gdn_dt_bias=gdn_dt_bias,
             gdn_a_log=gdn_a_log, gdn_norm_w=gdn_norm_w, w_out=w_out, final_norm_w=final_norm_w)
    m = dict(norm_w=m_norm_w, w_in=m_w_in, ssd_conv_w=m_ssd_conv_w, ssd_conv_b=m_ssd_conv_b, ssd_dt_bias=m_ssd_dt_bias,
             ssd_a_log=m_ssd_a_log, ssd_d=m_ssd_d, ssd_norm_w=m_ssd_norm_w, gdn_conv_w=m_gdn_conv_w,
             gdn_dt_bias=m_gdn_dt_bias, gdn_a_log=m_gdn_a_log, gdn_norm_w=m_gdn_norm_w, w_out=m_w_out,
             final_norm_w=m_final_norm_w)
    v = dict(norm_w=v_norm_w, w_in=v_w_in, ssd_conv_w=v_ssd_conv_w, ssd_conv_b=v_ssd_conv_b, ssd_dt_bias=v_ssd_dt_bias,
             ssd_a_log=v_ssd_a_log, ssd_d=v_ssd_d, ssd_norm_w=v_ssd_norm_w, gdn_conv_w=v_gdn_conv_w,
             gdn_dt_bias=v_gdn_dt_bias, gdn_a_log=v_gdn_a_log, gdn_norm_w=v_gdn_norm_w, w_out=v_w_out,
             final_norm_w=v_final_norm_w)
    names = list(w)
    shapes = {n: w[n].shape for n in names}

    conv_pack = _pack([w["ssd_conv_w"], w["gdn_conv_w"]], SHARD, SHARD_ROWS)
    g_w_in, g_w_out, g_conv = all_gather([w_in[0].astype(_MM), w_out[0].astype(_MM), conv_pack], "gather_weights")
    w_in_full = g_w_in.transpose(1, 0, 2).reshape(D_MODEL, IN_DIM)
    w_perm = perm_w_in(w_in_full)
    w_out_full = g_w_out.reshape(MIX_WIDTH, D_MODEL)
    ssd_cw_full = _conv_full(g_conv[:, 0:6].reshape(N_DEV, -1), SSD_CONV_DIM)
    gdn_cw_full = _conv_full(g_conv[:, 6:18].reshape(N_DEV, -1), GDN_CONV_DIM)

    loss_l, dx, g = local_step(x[0], loss_target[0], w_perm, w_out_full, norm_w, ssd_cw_full, ssd_conv_b,
                               ssd_dt_bias, ssd_a_log, ssd_d, ssd_norm_w, gdn_cw_full, gdn_dt_bias, gdn_a_log,
                               gdn_norm_w, final_norm_w)

    t_w_in = g["w_in"].reshape(D_MODEL, N_DEV, W_IN_SHARD).transpose(1, 0, 2).astype(_MM)
    t_w_out = g["w_out"].reshape(N_DEV, MIX_WIDTH // N_DEV, D_MODEL).astype(_MM)
    rep = _pack([g[n] for n, _ in REP], REP, REP_ROWS)
    cs = _conv_slabs(g["ssd_conv_w"])
    cg = _conv_slabs(g["gdn_conv_w"])
    shard_rows = jnp.concatenate([cs.reshape(N_DEV, 6, 128), cg.reshape(N_DEV, 12, 128),
                                  jnp.zeros((N_DEV, SHARD_ROWS - 18, 128), f32)], axis=1)
    t_small = jnp.concatenate([jnp.broadcast_to(rep[None], (N_DEV, REP_ROWS, 128)), shard_rows], axis=1)
    r_w_in, r_w_out, r_small = all_to_all([t_w_in, t_w_out, t_small], "scatter_grads")

    o_w_in = adamw_sum(r_w_in, w_in[0], m_w_in[0], v_w_in[0], 128, "adamw_w_in")
    o_w_out = adamw_sum(r_w_out, w_out[0], m_w_out[0], v_w_out[0], 64, "adamw_w_out")
    small = [jnp.concatenate([_pack([d[n] for n, _ in REP], REP, REP_ROWS),
                              _pack([d[n] for n, _ in SHARD], SHARD, SHARD_ROWS)], axis=0) for d in (w, m, v)]
    o_small = adamw_sum(r_small, small[0], small[1], small[2], REP_ROWS + SHARD_ROWS, "adamw_small")

    loss = lax.psum(loss_l[0, 0], ("x", "y", "c"))
    outs = [loss, dx[None]]
    for k in range(4):
        parts = {**_unpack(o_small[k], REP), **_unpack(o_small[k], SHARD, REP_ROWS),
                 "w_in": o_w_in[k], "w_out": o_w_out[k]}
        outs += [parts[n].reshape(shapes[n]) for n in names]
    return tuple(outs)
```

```python
import functools

import jax
import jax.numpy as jnp
import numpy as np
from jax import lax
from jax.experimental import pallas as pl
from jax.experimental.pallas import tpu as pltpu

_MM = jnp.bfloat16

D_MODEL = 1024
CHUNK = 64
CONV_K = 4
EPS = 1e-6
SSD_CONV_DIM = 1536
GDN_HEADS = 8
GDN_DK = 128
GDN_CONV_DIM = 3072
MIX_WIDTH = 2048
IN_DIM = 6688
N_DEV = 8
W_IN_SHARD = IN_DIM // N_DEV
PERM_DIM = 6784
HI = lax.Precision.HIGHEST
HIGH = lax.Precision.HIGH
VMEM_LIMIT = 56 * 1024 * 1024

ADAM_LR = 0.001
ADAM_B1 = 0.9
ADAM_B2 = 0.999
ADAM_EPS = 1e-08
ADAM_WD = 0.01
ADAM_STEP = 10


def _pc(body, **kw):
    return pl.pallas_call(body, **kw)


def _pc_comm(body, **kw):
    return pl.pallas_call(body, **kw)


def _cparams(sem):
    return pltpu.CompilerParams(dimension_semantics=sem, vmem_limit_bytes=VMEM_LIMIT)


def _sig(x):
    return 0.5 * jnp.tanh(0.5 * x) + 0.5


@jax.custom_vjp
def _sigmoid(x):
    return _sig(x)


def _sigmoid_fwd(x):
    s = _sig(x)
    return s, s


def _sigmoid_bwd(s, g):
    return (g * s * (1.0 - s),)


_sigmoid.defvjp(_sigmoid_fwd, _sigmoid_bwd)


@jax.custom_vjp
def _silu(x):
    return x * _sig(x)


def _silu_fwd(x):
    s = _sig(x)
    return x * s, (x, s)


def _silu_bwd(res, g):
    x, s = res
    return (g * (s * (1.0 + x * (1.0 - s))),)


_silu.defvjp(_silu_fwd, _silu_bwd)


def _softplus_impl(x):
    return jnp.maximum(x, 0.0) + jnp.log(1.0 + jnp.exp(-jnp.abs(x)))


@jax.custom_vjp
def _softplus(x):
    return _softplus_impl(x)


def _softplus_fwd(x):
    return _softplus_impl(x), x


def _softplus_bwd(x, g):
    return (g * _sig(x),)


_softplus.defvjp(_softplus_fwd, _softplus_bwd)


def _lane_bcast_impl(x, k):
    return jnp.broadcast_to(x[..., k:k + 1], x.shape)


@functools.partial(jax.custom_vjp, nondiff_argnums=(1,))
def _lane_bcast(x, k):
    return _lane_bcast_impl(x, k)


def _lane_bcast_fwd(x, k):
    return _lane_bcast_impl(x, k), None


def _lane_bcast_bwd(k, _, g):
    lane = lax.broadcasted_iota(jnp.int32, g.shape, g.ndim - 1)
    return (jnp.where(lane == k, jnp.sum(g, axis=-1, keepdims=True), 0.0),)


_lane_bcast.defvjp(_lane_bcast_fwd, _lane_bcast_bwd)


def _mm(a, b):
    return jnp.dot(a.astype(_MM), b.astype(_MM), preferred_element_type=jnp.float32)


def _mm_nt(a, b):
    return lax.dot_general(a.astype(_MM), b.astype(_MM), (((1,), (1,)), ((), ())),
                           preferred_element_type=jnp.float32)


def _mm_tn(a, b):
    return lax.dot_general(a.astype(_MM), b.astype(_MM), (((0,), (0,)), ((), ())),
                           preferred_element_type=jnp.float32)


def _dot_hi(a, b):
    return jnp.dot(a, b, precision=HI, preferred_element_type=jnp.float32)


def _bmm(a, b):
    return lax.dot_general(a.astype(_MM), b.astype(_MM), (((2,), (1,)), ((0,), (0,))),
                           preferred_element_type=jnp.float32)


def _bmm_nt(a, b):
    return lax.dot_general(a.astype(_MM), b.astype(_MM), (((2,), (2,)), ((0,), (0,))),
                           preferred_element_type=jnp.float32)


def _bmm_tn(a, b):
    return lax.dot_general(a.astype(_MM), b.astype(_MM), (((1,), (1,)), ((0,), (0,))),
                           preferred_element_type=jnp.float32)


def _bmm_hi(a, b):
    return lax.dot_general(a, b, (((2,), (1,)), ((0,), (0,))), precision=HIGH, preferred_element_type=jnp.float32)


def _bmm_nt_hi(a, b):
    return lax.dot_general(a, b, (((2,), (2,)), ((0,), (0,))), precision=HIGH, preferred_element_type=jnp.float32)


def _bmm_tn_hi(a, b):
    return lax.dot_general(a, b, (((1,), (1,)), ((0,), (0,))), precision=HIGH, preferred_element_type=jnp.float32)


def _consts():
    l = np.arange(CHUNK)
    tri = (l[:, None] >= l[None, :]).astype(np.float32)
    lane = np.arange(128)
    i2 =(l[:, None] == (lane[None, :] % 64)).astype(np.float32)
    mask2 = (l[:, None] >= (lane[None, :] % 64)).astype(np.float32)
    lo = (lane < 64).astype(np.float32)[None, :]
    i64 = np.eye(CHUNK, dtype=np.float32)
    strict = (l[:, None] > l[None, :]).astype(np.float32)
    return dict(tri=jnp.asarray(tri), i2=jnp.asarray(i2), mask2=jnp.asarray(mask2), lo=jnp.asarray(lo),
                i64=jnp.asarray(i64), strict=jnp.asarray(strict))


def _ssd_chunk(xs_pre, b_pre, c_pre, z, sm, ht, dtb, alog, dpar, nw, tri, i2, mask2, lo):
    lane = lax.broadcasted_iota(jnp.int32, (1, 128), 1)
    m16 = lane < 16
    dt = jnp.where(m16, _softplus(sm + dtb), 0.0)
    a_neg = -jnp.exp(alog)
    cum = _dot_hi(tri, dt * a_neg)
    row = lax.broadcasted_iota(jnp.int32, (CHUNK, 1), 0)
    is_last = row == CHUNK - 1
    hi = 1.0 - lo
    bm = [_silu(b) for b in b_pre]
    cm = [_silu(c) for c in c_pre]
    cb2 = [_mm_nt(cm[g], jnp.concatenate([bm[g], bm[g]], axis=0)) for g in range(2)]
    yg, ht_next = [], []
    for j in range(8):
        g = j // 4
        pair = lambda v, j=j: jnp.where(lo > 0.5, _lane_bcast(v, 2 * j), _lane_bcast(v, 2 * j + 1))
        xs = _silu(xs_pre[j])
        dte = pair(dt)
        cume = pair(cum)
        cum_last = jnp.sum(jnp.where(is_last, cume, 0.0), axis=0, keepdims=True)
        xdt = xs * dte
        rowv = jnp.sum(cume * i2, axis=0, keepdims=True)
        lm = jnp.exp(jnp.where(mask2 > 0.5, cume - rowv, -jnp.inf))
        m = cb2[g] * lm
        xblk = jnp.concatenate([xdt * lo, xdt * hi], axis=0)
        y = _mm(m, xblk)
        y = y + _mm(cm[g], ht[j]) * jnp.exp(cume)
        y = y + pair(dpar) * xs
        yg.append(y * _silu(z[j]))
        st = _mm_tn(bm[g], xdt * jnp.exp(cum_last - cume))
        ht_next.append(ht[j] * jnp.exp(cum_last) + st)
    outs = []
    for g in range(2):
        ss = sum(jnp.sum(yg[j] * yg[j], axis=-1, keepdims=True) for j in range(4 * g, 4 * g + 4))
        rs = lax.rsqrt(ss * (1.0 / 512.0) + EPS)
        for j in range(4 * g, 4 * g + 4):
            outs.append(yg[j] * rs * nw[j])
    return outs, ht_next


def _tri_inverse(a):
    eye = jnp.eye(CHUNK, dtype=jnp.float32)[None]
    p = eye - a
    ap = a
    for _ in range(5):
        ap = _bmm_hi(ap, ap)
        p = p + _bmm_hi(p, ap)
    return p


@jax.custom_vjp
def _solve(a, r1, r2, t):
    return _bmm_hi(t, r1), _bmm_hi(t, r2)


def _solve_fwd(a, r1, r2, t):
    u, w = _bmm_hi(t, r1), _bmm_hi(t, r2)
    return (u, w), (t, u, w)


def _solve_bwd(res, cts):
    t, u, w = res
    du, dw = cts
    dr1 = _bmm_tn_hi(t, du)
    dr2 = _bmm_tn_hi(t, dw)
    da = -(_bmm_nt_hi(dr1, u) + _bmm_nt_hi(dr2, w))
    return da, dr1, dr2, jnp.zeros_like(t)


_solve.defvjp(_solve_fwd, _solve_bwd)


def _gdn_chunk(q_pre, k_pre, v_pre, gate, sm, s, dtb, alog, nw, tri, i64, strict, t_in=None):
    lane = lax.broadcasted_iota(jnp.int32, (1, 128), 1)
    m_a = (lane >= 16) & (lane < 24)
    g_full = jnp.where(m_a, -jnp.exp(alog) * _softplus(sm + dtb), 0.0)
    gc = _dot_hi(tri, g_full)
    sig = _sigmoid(sm)
    gc3 = jnp.stack([_lane_bcast(gc, 16 + h) for h in range(GDN_HEADS)])
    beta3 = jnp.stack([_lane_bcast(sig, 24 + h) for h in range(GDN_HEADS)])
    q = _silu(q_pre)
    q = q * lax.rsqrt(jnp.sum(q * q, axis=-1, keepdims=True) + EPS) * (GDN_DK ** -0.5)
    k = _silu(k_pre)
    k = k * lax.rsqrt(jnp.sum(k * k, axis=-1, keepdims=True) + EPS)
    v = _silu(v_pre)
    gcl = gc3[:, :, :CHUNK]
    gc_row = jnp.sum(gcl * i64[None], axis=1, keepdims=True)
    incl = (strict + i64)[None] > 0.5
    decay = jnp.exp(jnp.where(incl, gcl - gc_row, -jnp.inf))
    kb = k * beta3
    a = jnp.where(strict[None] > 0.5, _bmm_nt(kb, k) * decay, 0.0)
    egc = jnp.exp(gc3)
    t = _tri_inverse(a) if t_in is None else t_in
    u, w = _solve(a, v * beta3, kb * egc, t)
    attn = _bmm_nt(q, k) * decay
    row = lax.broadcasted_iota(jnp.int32, (1, CHUNK, 1), 1)
    gl = jnp.sum(jnp.where(row == CHUNK - 1, gc3, 0.0), axis=1, keepdims=True)
    q_dec = q * egc
    k_dec = k * jnp.exp(gl - gc3)
    v_new = u - _bmm(w, s)
    o = _bmm(q_dec, s) + _bmm(attn, v_new)
    s_next = s * jnp.exp(gl) + _bmm_tn(k_dec, v_new)
    on = o * lax.rsqrt(jnp.mean(o * o, axis=-1, keepdims=True) + EPS) * nw
    return on * _silu(gate), s_next, t


def _conv_fwd(pbuf, w_ref, c0, c1):
    acc = None
    for j in range(CONV_K):
        term = w_ref[j:j + 1, c0:c1] * pbuf[5 + j:69 + j, c0:c1]
        acc = term if acc is None else acc + term
    return acc


MESH = pl.DeviceIdType.MESH
ANY = pl.BlockSpec(memory_space=pl.ANY)


def _me():
    x, y, c = lax.axis_index("x"), lax.axis_index("y"), lax.axis_index("c")
    return x, y, c, 4 * x + 2 * y + c


def _peer(r):
    x, y, c, _ = _me()
    px = 1 - x if r & 4 else x
    py = 1 - y if r & 2 else y
    pc = 1 - c if r & 1 else c
    return (px, py, pc), 4 * px + 2 * py + pc


def _exchange_ops(kind, in_ref, out_ref, send_sems, recv_sems, local_sem):
    me = _me()[3]
    local = pltpu.make_async_copy(in_ref.at[me] if kind == "scatter" else in_ref, out_ref.at[me], local_sem)
    sends, recvs = [], []
    for r in range(1, N_DEV):
        peer, pidx = _peer(r)
        src = in_ref.at[pidx] if kind == "scatter" else in_ref
        sems = dict(send_sem=send_sems.at[r - 1], recv_sem=recv_sems.at[r - 1], device_id=peer, device_id_type=MESH)
        sends.append(pltpu.make_async_remote_copy(src_ref=src, dst_ref=out_ref.at[me], **sems))
        recvs.append(pltpu.make_async_remote_copy(src_ref=src, dst_ref=out_ref.at[pidx], **sems))

    def start():
        local.start()
        for cp in sends:
            cp.start()

    def wait():
        for cp in recvs:
            cp.wait_recv()
        for cp in sends:
            cp.wait_send()
        local.wait()

    return start, wait


def _exchange_sems(n):
    return [pltpu.SemaphoreType.DMA((N_DEV - 1,)), pltpu.SemaphoreType.DMA((N_DEV - 1,)),
            pltpu.SemaphoreType.DMA(())] * n


def _exchange_out_shape(kind, a):
    return jax.ShapeDtypeStruct(a.shape if kind == "scatter" else (N_DEV,) + a.shape, a.dtype)


def _hosting(body, n_in, n_out, n_scratch, kinds, first, last):
    ne = len(kinds)

    def wrapped(*refs):
        ins, ex_in = refs[:n_in], refs[n_in:n_in + ne]
        o0 = n_in + ne
        outs, ex_out = refs[o0:o0 + n_out], refs[o0 + n_out:o0 + n_out + ne]
        s0 = o0 + n_out + ne
        scr, sems = refs[s0:s0 + n_scratch], refs[s0 + n_scratch:]
        ops = [_exchange_ops(kinds[e], ex_in[e], ex_out[e], *sems[3 * e:3 * e + 3]) for e in range(ne)]

        @pl.when(first())
        def _():
            for start, _ in ops:
                start()

        body(*ins, *outs, *scr)

        @pl.when(last())
        def _():
            for _, wait in ops:
                wait()

    return wrapped


def exchange(arrs, kinds, name):
    n = len(arrs)

    def body(*refs):
        ins, outs, sems = refs[:n], refs[n:2 * n], refs[2 * n:]
        ops = [_exchange_ops(kinds[e], ins[e], outs[e], *sems[3 * e:3 * e + 3]) for e in range(n)]
        for start, _ in ops:
            start()
        for _, wait in ops:
            wait()

    return _pc_comm(
        body, name=name, in_specs=[ANY] * n, out_specs=[ANY] * n,
        out_shape=[_exchange_out_shape(k, a) for k, a in zip(kinds, arrs)], scratch_shapes=_exchange_sems(n),
    )(*arrs)


GROUPS = (("z", 0, 1024), ("xbc", 1024, 2560), ("gate", 2560, 3584), ("qkv", 3584, 6656), ("sm", 6656, 6784))


def inproj_fwd(x, norm_w, w_perm, gathered):
    t = x.shape[0]
    tm = min(256, t)
    steps = t // tm
    kinds = ["gather"] * len(gathered)

    def body(x_ref, nw_ref, w_ref, u_ref, z_ref, xbc_ref, gate_ref, qkv_ref, sm_ref):
        xf = x_ref[...]
        rstd = lax.rsqrt(jnp.mean(xf * xf, axis=-1, keepdims=True) + EPS)
        u = (xf * rstd * nw_ref[...]).astype(_MM)
        u_ref[...] = u
        for (name, c0, c1), o_ref in zip(GROUPS, (z_ref, xbc_ref, gate_ref, qkv_ref, sm_ref)):
            o_ref[...] = jnp.dot(u, w_ref[:, c0:c1], preferred_element_type=jnp.float32)

    outs = [jax.ShapeDtypeStruct((t, D_MODEL), _MM)] + [jax.ShapeDtypeStruct((t, c1 - c0), jnp.float32)
                                                        for _, c0, c1 in GROUPS]
    hosted = _hosting(body, 3, 6, 0, kinds, lambda: pl.program_id(0) == 0, lambda: pl.program_id(0) == steps - 1)
    return _pc_comm(
        hosted, name="inproj_fwd", grid=(steps,),
        in_specs=[pl.BlockSpec((tm, D_MODEL), lambda i: (i, 0)),
                  pl.BlockSpec((1, D_MODEL), lambda i: (0, 0)),
                  pl.BlockSpec((D_MODEL, PERM_DIM), lambda i: (0, 0))] + [ANY] * len(gathered),
        out_specs=[pl.BlockSpec((tm, D_MODEL), lambda i: (i, 0))] +
                  [pl.BlockSpec((tm, c1 - c0), lambda i: (i, 0)) for _, c0, c1 in GROUPS] + [ANY] * len(gathered),
        out_shape=outs + [_exchange_out_shape("gather", a) for a in gathered],
        scratch_shapes=_exchange_sems(len(gathered)), compiler_params=_cparams(("arbitrary",)),
    )(x, norm_w, w_perm, *gathered)


def _halo_spec(width, idx_fn):
    return pl.BlockSpec((8, width), lambda i: (jnp.maximum(idx_fn(i) * 8 - 1, 0), 0))


def _full(shape):
    nd = len(shape)
    return pl.BlockSpec(shape, lambda i: (0,) * nd)


def _ssd_split(pre_fn, z_ref, sm_ref):
    xs_pre = [pre_fn(128 * j, 128 * j + 128) for j in range(8)]
    b_pre = [pre_fn(1024 + 128 * g, 1152 + 128 * g) for g in range(2)]
    c_pre = [pre_fn(1280 + 128 * g, 1408 + 128 * g) for g in range(2)]
    z = [z_ref[:, 128 * j:128 * j + 128] for j in range(8)]
    return xs_pre, b_pre, c_pre, z, sm_ref[...]


def ssd_fwd(z, xbc, sm, conv_w, conv_b, dtb, alog, dpar, nw, cs):
    t = z.shape[0]
    nc = t // CHUNK

    def body(z_ref, xbc_ref, halo_ref, sm_ref, cw_ref, cb_ref, dtb_ref, alog_ref, dpar_ref, nw_ref,
             tri_ref, i2_ref, mask2_ref, lo_ref, y_ref, hs_ref, pre_ref, pbuf, ht_scr):
        i = pl.program_id(0)

        @pl.when(i == 0)
        def _():
            ht_scr[...] = jnp.zeros_like(ht_scr)

        pbuf[0:8, :] = jnp.where(i == 0, 0.0, halo_ref[...])
        pbuf[8:72, :] = xbc_ref[...]

        def pre_fn(c0, c1):
            pre = _conv_fwd(pbuf, cw_ref, c0, c1) + cb_ref[:, c0:c1]
            pre_ref[:, c0:c1] = pre
            return pre

        xs_pre, b_pre, c_pre, zz, smv = _ssd_split(pre_fn, z_ref, sm_ref)
        ht = [ht_scr[:, 128 * j:128 * j + 128] for j in range(8)]
        hs_ref[0] = ht_scr[...]
        nwl = [nw_ref[:, 128 * j:128 * j + 128] for j in range(8)]
        outs, ht_next = _ssd_chunk(xs_pre, b_pre, c_pre, zz, smv, ht, dtb_ref[...], alog_ref[...], dpar_ref[...],
                                   nwl, tri_ref[...], i2_ref[...], mask2_ref[...], lo_ref[...])
        for j in range(8):
            y_ref[:, 128 * j:128 * j + 128] = outs[j].astype(y_ref.dtype)
            ht_scr[:, 128 * j:128 * j + 128] = ht_next[j]

    blk = lambda w: pl.BlockSpec((CHUNK, w), lambda i: (i, 0))
    return _pc(
        body, name="ssd_fwd", grid=(nc,),
        in_specs=[blk(1024), blk(1536), _halo_spec(1536, lambda i: i), blk(128),
                  _full((CONV_K, 1536)), _full((1, 1536)), _full((1, 128)), _full((1, 128)), _full((1, 128)),
                  _full((1, 1024)), _full((64, 64)), _full((64, 128)), _full((64, 128)),
                  _full((1, 128))],
        out_specs=[blk(1024), pl.BlockSpec((1, 128, 1024), lambda i: (i, 0, 0)), blk(1536)],
        out_shape=[jax.ShapeDtypeStruct((t, 1024), _MM), jax.ShapeDtypeStruct((nc, 128, 1024), jnp.float32),
                   jax.ShapeDtypeStruct((t, 1536), jnp.float32)],
        scratch_shapes=[pltpu.VMEM((72, 1536), jnp.float32), pltpu.VMEM((128, 1024), jnp.float32)],
        compiler_params=_cparams(("arbitrary",)),
    )(z, xbc, xbc, sm, conv_w, conv_b, dtb, alog, dpar, nw, cs["tri"], cs["i2"], cs["mask2"], cs["lo"])


def _conv_bwd(dpre_list, col_ranges, dbuf, carry, x_ref, cw_ref, dx_ref, dcw_ref, dcb_ref, first):
    for dpre, (c0, c1) in zip(dpre_list, col_ranges):
        dbuf[0:64, c0:c1] = dpre
    dbuf[64:72, :] = jnp.where(first, 0.0, carry[...])
    carry[...] = dbuf[0:8, :]
    for (c0, c1) in col_ranges:
        xin = x_ref[:, c0:c1]
        acc = None
        for j in range(CONV_K):
            sh = dbuf[3 - j:67 - j, c0:c1]
            term = cw_ref[j:j + 1, c0:c1] * sh
            acc = term if acc is None else acc + term
            dcw_ref[j:j + 1, c0:c1] += jnp.sum(xin * sh, axis=0, keepdims=True)
        dx_ref[:, c0:c1] = acc
        if dcb_ref is not None:
            dcb_ref[0:1, c0:c1] += jnp.sum(dbuf[0:64, c0:c1], axis=0, keepdims=True)


def ssd_bwd(z, xbc, pre, sm, hs, dy, conv_w, dtb, alog, dpar, nw, cs):
    t = z.shape[0]
    nc = t // CHUNK

    def body(z_ref, xbc_ref, pre_ref, sm_ref, hs_ref, dy_ref, cw_ref, dtb_ref, alog_ref, dpar_ref, nw_ref,
             tri_ref, i2_ref, mask2_ref, lo_ref,
             dz_ref, dxbc_ref, dsm_ref, dcw_ref, dcb_ref, ddtb_ref, dalog_ref, ddpar_ref, dnw_ref,
             dbuf, carry, dht_scr):
        i = pl.program_id(0)

        @pl.when(i == 0)
        def _():
            dht_scr[...] = jnp.zeros_like(dht_scr)
            dcw_ref[...] = jnp.zeros_like(dcw_ref)
            dcb_ref[...] = jnp.zeros_like(dcb_ref)
            ddtb_ref[...] = jnp.zeros_like(ddtb_ref)
            dalog_ref[...] = jnp.zeros_like(dalog_ref)
            ddpar_ref[...] = jnp.zeros_like(ddpar_ref)
            dnw_ref[...] = jnp.zeros_like(dnw_ref)

        pre_fn = lambda c0, c1: pre_ref[:, c0:c1]
        xs_pre, b_pre, c_pre, zz, smv = _ssd_split(pre_fn, z_ref, sm_ref)
        ht = [hs_ref[0, :, 128 * j:128 * j + 128] for j in range(8)]
        nwl = [nw_ref[:, 128 * j:128 * j + 128] for j in range(8)]
        consts = (tri_ref[...], i2_ref[...], mask2_ref[...], lo_ref[...])

        def f(xs_pre, b_pre, c_pre, zz, smv, ht, dtb, alog, dpar, nwl):
            return _ssd_chunk(xs_pre, b_pre, c_pre, zz, smv, ht, dtb, alog, dpar, nwl, *consts)

        _, vjp = jax.vjp(f, xs_pre, b_pre, c_pre, zz, smv, ht, dtb_ref[...], alog_ref[...], dpar_ref[...], nwl)
        dys = [dy_ref[:, 128 * j:128 * j + 128] for j in range(8)]
        dhts = [dht_scr[:, 128 * j:128 * j + 128] for j in range(8)]
        dxs, db, dc, dzz, dsm, dht, ddtb, dalog, ddpar, dnwl = vjp((dys, dhts))
        for j in range(8):
            dz_ref[:, 128 * j:128 * j + 128] = dzz[j]
            dht_scr[:, 128 * j:128 * j + 128] = dht[j]
            dnw_ref[0:1, 128 * j:128 * j + 128] += dnwl[j]
        dsm_ref[...] = dsm
        ddtb_ref[0:1, :] += ddtb
        dalog_ref[0:1, :] += dalog
        ddpar_ref[0:1, :] += ddpar
        ranges = ([(128 * j, 128 * j + 128) for j in range(8)] + [(1024 + 128 * g, 1152 + 128 * g) for g in range(2)]
                  + [(1280 + 128 * g, 1408 + 128 * g) for g in range(2)])
        _conv_bwd(dxs + db + dc, ranges, dbuf, carry, xbc_ref, cw_ref, dxbc_ref, dcw_ref, dcb_ref, i == 0)

    rblk = lambda w: pl.BlockSpec((CHUNK, w), lambda i: (nc - 1 - i, 0))
    acc = lambda w: pl.BlockSpec((8, w), lambda i: (0, 0))
    f32 = jnp.float32
    return _pc(
        body, name="ssd_bwd", grid=(nc,),
        in_specs=[rblk(1024), rblk(1536), rblk(1536), rblk(128),
                  pl.BlockSpec((1, 128, 1024), lambda i: (nc - 1 - i, 0, 0)), rblk(1024),
                  _full((CONV_K, 1536)), _full((1, 128)), _full((1, 128)), _full((1, 128)),
                  _full((1, 1024)), _full((64, 64)), _full((64, 128)), _full((64, 128)),
                  _full((1, 128))],
        out_specs=[rblk(1024), rblk(1536), rblk(128), acc(1536), acc(1536), acc(128), acc(128), acc(128), acc(1024)],
        out_shape=[jax.ShapeDtypeStruct((t, 1024), f32), jax.ShapeDtypeStruct((t, 1536), f32),
                   jax.ShapeDtypeStruct((t, 128), f32), jax.ShapeDtypeStruct((8, 1536), f32),
                   jax.ShapeDtypeStruct((8, 1536), f32), jax.ShapeDtypeStruct((8, 128), f32),
                   jax.ShapeDtypeStruct((8, 128), f32), jax.ShapeDtypeStruct((8, 128), f32),
                   jax.ShapeDtypeStruct((8, 1024), f32)],
        scratch_shapes=[pltpu.VMEM((72, 1536), f32), pltpu.VMEM((8, 1536), f32), pltpu.VMEM((128, 1024), f32)],
        compiler_params=_cparams(("arbitrary",)),
    )(z, xbc, pre, sm, hs, dy, conv_w, dtb, alog, dpar, nw, cs["tri"], cs["i2"], cs["mask2"], cs["lo"])


def _gdn_split(pre_fn, gate_ref):
    def heads(base):
        return jnp.stack([pre_fn(base + 128 * h, base + 128 * h + 128) for h in range(GDN_HEADS)])
    gate = jnp.stack([gate_ref[:, 128 * h:128 * h + 128] for h in range(GDN_HEADS)])
    return heads(0), heads(1024), heads(2048), gate


def gdn_fwd(gate, qkv, sm, conv_w, dtb, alog, nw, cs):
    t = gate.shape[0]
    nc = t // CHUNK

    def body(gate_ref, qkv_ref, halo_ref, sm_ref, cw_ref, dtb_ref, alog_ref, nw_ref,
             tri_ref, i64_ref, strict_ref, o_ref, ss_ref, ts_ref, pre_ref, pbuf, s_scr):
        i = pl.program_id(0)

        @pl.when(i == 0)
        def _():
            s_scr[...] = jnp.zeros_like(s_scr)

        pbuf[0:8, :] = jnp.where(i == 0, 0.0, halo_ref[...])
        pbuf[8:72, :] = qkv_ref[...]

        def pre_fn(c0, c1):
            pre = _conv_fwd(pbuf, cw_ref, c0, c1)
            pre_ref[:, c0:c1] = pre
            return pre

        q_pre, k_pre, v_pre, g3 = _gdn_split(pre_fn, gate_ref)
        s = s_scr[...]
        ss_ref[0] = s
        out, s_next, tinv = _gdn_chunk(q_pre, k_pre, v_pre, g3, sm_ref[...], s, dtb_ref[...], alog_ref[...],
                                       nw_ref[...], tri_ref[...], i64_ref[...], strict_ref[...])
        ts_ref[0] = tinv
        s_scr[...] = s_next
        for h in range(GDN_HEADS):
            o_ref[:, 128 * h:128 * h + 128] = out[h].astype(o_ref.dtype)

    blk = lambda w: pl.BlockSpec((CHUNK, w), lambda i: (i, 0))
    return _pc(
        body, name="gdn_fwd", grid=(nc,),
        in_specs=[blk(1024), blk(3072), _halo_spec(3072, lambda i: i), blk(128),
                  _full((CONV_K, 3072)), _full((1, 128)), _full((1, 128)), _full((1, 128)),
                  _full((64, 64)), _full((64, 64)), _full((64, 64))],
        out_specs=[blk(1024), pl.BlockSpec((1, 8, 128, 128), lambda i: (i, 0, 0, 0)),
                   pl.BlockSpec((1, 8, CHUNK, CHUNK), lambda i: (i, 0, 0, 0)), blk(3072)],
        out_shape=[jax.ShapeDtypeStruct((t, 1024), _MM), jax.ShapeDtypeStruct((nc, 8, 128, 128), jnp.float32),
                   jax.ShapeDtypeStruct((nc, 8, CHUNK, CHUNK), jnp.float32),
                   jax.ShapeDtypeStruct((t, 3072), jnp.float32)],
        scratch_shapes=[pltpu.VMEM((72, 3072), jnp.float32), pltpu.VMEM((8, 128, 128), jnp.float32)],
        compiler_params=_cparams(("arbitrary",)),
    )(gate, qkv, qkv, sm, conv_w, dtb, alog, nw, cs["tri"], cs["i64"], cs["strict"])


def gdn_bwd(gate, qkv, pre, sm, ss, ts, do, dsm_ssd, conv_w, dtb, alog, nw, cs):
    t = gate.shape[0]
    nc = t // CHUNK

    def body(gate_ref, qkv_ref, pre_ref, sm_ref, ss_ref, ts_ref, do_ref, dsm_in_ref, cw_ref, dtb_ref, alog_ref,
             nw_ref, tri_ref, i64_ref, strict_ref,
             dgate_ref, dqkv_ref, dsm_ref, dcw_ref, ddtb_ref, dalog_ref, dnw_ref,
             dbuf, carry, ds_scr):
        i = pl.program_id(0)

        @pl.when(i == 0)
        def _():
            ds_scr[...] = jnp.zeros_like(ds_scr)
            dcw_ref[...] = jnp.zeros_like(dcw_ref)
            ddtb_ref[...] = jnp.zeros_like(ddtb_ref)
            dalog_ref[...] = jnp.zeros_like(dalog_ref)
            dnw_ref[...] = jnp.zeros_like(dnw_ref)

        q_pre, k_pre, v_pre, g3 = _gdn_split(lambda c0, c1: pre_ref[:, c0:c1], gate_ref)
        consts = (tri_ref[...], i64_ref[...], strict_ref[...], ts_ref[0])

        def f(q_pre, k_pre, v_pre, g3, smv, s, dtb, alog, nwv):
            return _gdn_chunk(q_pre, k_pre, v_pre, g3, smv, s, dtb, alog, nwv, *consts)[:2]

        _, vjp = jax.vjp(f, q_pre, k_pre, v_pre, g3, sm_ref[...], ss_ref[0], dtb_ref[...], alog_ref[...], nw_ref[...])
        do3 = jnp.stack([do_ref[:, 128 * h:128 * h + 128] for h in range(GDN_HEADS)])
        dq, dk, dv, dg3, dsm, ds, ddtb, dalog, dnw = vjp((do3, ds_scr[...]))
        ds_scr[...] = ds
        for h in range(GDN_HEADS):
            dgate_ref[:, 128 * h:128 * h + 128] = dg3[h]
        dsm_ref[...] = dsm + dsm_in_ref[...]
        ddtb_ref[0:1, :] += ddtb
        dalog_ref[0:1, :] += dalog
        dnw_ref[0:1, :] += dnw
        ranges = [(base + 128 * h, base + 128 * h + 128) for base in (0, 1024, 2048) for h in range(GDN_HEADS)]
        dlist = [d[h] for d in (dq, dk, dv) for h in range(GDN_HEADS)]
        _conv_bwd(dlist, ranges, dbuf, carry, qkv_ref, cw_ref, dqkv_ref, dcw_ref, None, i == 0)

    rblk = lambda w: pl.BlockSpec((CHUNK, w), lambda i: (nc - 1 - i, 0))
    acc = lambda w: pl.BlockSpec((8, w), lambda i: (0, 0))
    f32 = jnp.float32
    return _pc(
        body, name="gdn_bwd", grid=(nc,),
        in_specs=[rblk(1024), rblk(3072), rblk(3072), rblk(128),
                  pl.BlockSpec((1, 8, 128, 128), lambda i: (nc - 1 - i, 0, 0, 0)),
                  pl.BlockSpec((1, 8, CHUNK, CHUNK), lambda i: (nc - 1 - i, 0, 0, 0)), rblk(1024), rblk(128),
                  _full((CONV_K, 3072)), _full((1, 128)), _full((1, 128)), _full((1, 128)),
                  _full((64, 64)), _full((64, 64)), _full((64, 64))],
        out_specs=[rblk(1024), rblk(3072), rblk(128), acc(3072), acc(128), acc(128), acc(128)],
        out_shape=[jax.ShapeDtypeStruct((t, 1024), f32), jax.ShapeDtypeStruct((t, 3072), f32),
                   jax.ShapeDtypeStruct((t, 128), f32), jax.ShapeDtypeStruct((8, 3072), f32),
                   jax.ShapeDtypeStruct((8, 128), f32), jax.ShapeDtypeStruct((8, 128), f32),
                   jax.ShapeDtypeStruct((8, 128), f32)],
        scratch_shapes=[pltpu.VMEM((72, 3072), f32), pltpu.VMEM((8, 3072), f32), pltpu.VMEM((8, 128, 128), f32)],
        compiler_params=_cparams(("arbitrary",)),
    )(gate, qkv, pre, sm, ss, ts, do, dsm_ssd, conv_w, dtb, alog, nw, cs["tri"], cs["i64"], cs["strict"])


def out_fwd_bwd(x, tgt, y_ssd, y_gdn, w_out, fnw):
    t = x.shape[0]
    tm = min(512, t)
    f32 = jnp.float32

    def body(x_ref, tgt_ref, ys_ref, yg_ref, w_ref, fnw_ref,
             dout_ref, dys_ref, dyg_ref, gw_ref, gfnw_ref, loss_ref):
        i = pl.program_id(0)

        @pl.when(i == 0)
        def _():
            gw_ref[...] = jnp.zeros_like(gw_ref)
            gfnw_ref[...] = jnp.zeros_like(gfnw_ref)
            loss_ref[...] = jnp.zeros_like(loss_ref)

        ys = ys_ref[...]
        yg = yg_ref[...]
        out = x_ref[...] + jnp.dot(ys, w_ref[0:1024, :], preferred_element_type=f32) \
            + jnp.dot(yg, w_ref[1024:2048, :], preferred_element_type=f32)
        rstd = lax.rsqrt(jnp.mean(out * out, axis=-1, keepdims=True) + EPS)
        yhat = out * rstd
        fw = fnw_ref[...]
        e = yhat * fw - tgt_ref[...]
        loss_ref[...] += 0.5 * jnp.sum(jnp.sum(e * e, axis=-1, keepdims=True) * (1.0 / D_MODEL), axis=0, keepdims=True)
        dyf = e * (1.0 / D_MODEL)
        gfnw_ref[0:1, :] += jnp.sum(dyf * yhat, axis=0, keepdims=True)
        dyhat = dyf * fw
        dout = rstd * (dyhat - yhat * jnp.mean(dyhat * yhat, axis=-1, keepdims=True))
        dout_ref[...] = dout
        db = dout.astype(_MM)
        dys_ref[...] = lax.dot_general(db, w_ref[0:1024, :], (((1,), (1,)), ((), ())), preferred_element_type=f32)
        dyg_ref[...] = lax.dot_general(db, w_ref[1024:2048, :], (((1,), (1,)), ((), ())), preferred_element_type=f32)
        gw_ref[0:1024, :] += lax.dot_general(ys, db, (((0,), (0,)), ((), ())), preferred_element_type=f32)
        gw_ref[1024:2048, :] += lax.dot_general(yg, db, (((0,), (0,)), ((), ())), preferred_element_type=f32)

    blk = pl.BlockSpec((tm, D_MODEL), lambda i: (i, 0))
    return _pc(
        body, name="out_fwd_bwd", grid=(t // tm,),
        in_specs=[blk, blk, blk, blk, _full((MIX_WIDTH, D_MODEL)), _full((1, D_MODEL))],
        out_specs=[blk, blk, blk, _full((MIX_WIDTH, D_MODEL)), _full((8, D_MODEL)), _full((1, 128))],
        out_shape=[jax.ShapeDtypeStruct((t, D_MODEL), f32)] * 3 +
                  [jax.ShapeDtypeStruct((MIX_WIDTH, D_MODEL), f32), jax.ShapeDtypeStruct((8, D_MODEL), f32),
                   jax.ShapeDtypeStruct((1, 128), f32)],
        compiler_params=_cparams(("arbitrary",)),
    )(x, tgt, y_ssd, y_gdn, w_out, fnw)


def inproj_bwd_dx(x, dout, norm_w, w_perm, dgroups, scattered):
    t = x.shape[0]
    tm = min(256, t)
    f32 = jnp.float32

    def body(x_ref, dout_ref, nw_ref, w_ref, dz_ref, dxbc_ref, dgate_ref, dqkv_ref, dsm_ref, dx_ref, gnw_ref):
        i = pl.program_id(0)

        @pl.when(i == 0)
        def _():
            gnw_ref[...] = jnp.zeros_like(gnw_ref)

        du = None
        for (name, c0, c1), d_ref in zip(GROUPS, (dz_ref, dxbc_ref, dgate_ref, dqkv_ref, dsm_ref)):
            term = lax.dot_general(d_ref[...].astype(_MM), w_ref[:, c0:c1], (((1,), (1,)), ((), ())),
                                   preferred_element_type=f32)
            du = term if du is None else du + term
        xf = x_ref[...]
        rstd = lax.rsqrt(jnp.mean(xf * xf, axis=-1, keepdims=True) + EPS)
        xhat = xf * rstd
        gnw_ref[0:1, :] += jnp.sum(du * xhat, axis=0, keepdims=True)
        dxh = du * nw_ref[...]
        dx_ref[...] = dout_ref[...] + rstd * (dxh - xhat * jnp.mean(dxh * xhat, axis=-1, keepdims=True))

    blk = lambda w: pl.BlockSpec((tm, w), lambda i: (i, 0))
    steps = t // tm
    kinds = ["scatter"] * len(scattered)
    hosted = _hosting(body, 9, 2, 0, kinds, lambda: pl.program_id(0) == 0, lambda: pl.program_id(0) == steps - 1)
    return _pc_comm(
        hosted, name="inproj_bwd_dx", grid=(steps,),
        in_specs=[blk(D_MODEL), blk(D_MODEL), _full((1, D_MODEL)), _full((D_MODEL, PERM_DIM))] +
                 [blk(c1 - c0) for _, c0, c1 in GROUPS] + [ANY] * len(scattered),
        out_specs=[blk(D_MODEL), _full((8, D_MODEL))] + [ANY] * len(scattered),
        out_shape=[jax.ShapeDtypeStruct((t, D_MODEL), f32), jax.ShapeDtypeStruct((8, D_MODEL), f32)] +
                  [_exchange_out_shape("scatter", a) for a in scattered],
        scratch_shapes=_exchange_sems(len(scattered)), compiler_params=_cparams(("arbitrary",)),
    )(x, dout, norm_w, w_perm, *dgroups, *scattered)


def grad_w_group(u, dg, name, scattered=()):
    t, n = dg.shape
    tn = 512 if n % 512 == 0 else n
    tm = 1024 if t % 1024 == 0 else t
    nj, nk = n // tn, t // tm
    f32 = jnp.float32

    def body(u_ref, d_ref, o_ref):
        @pl.when(pl.program_id(1) == 0)
        def _():
            o_ref[...] = jnp.zeros_like(o_ref)

        o_ref[...] += lax.dot_general(u_ref[...], d_ref[...].astype(_MM), (((0,), (0,)), ((), ())),
                                      preferred_element_type=f32)

    ne = len(scattered)
    hosted = _hosting(body, 2, 1, 0, ["scatter"] * ne,
                      lambda: (pl.program_id(0) == 0) & (pl.program_id(1) == 0),
                      lambda: (pl.program_id(0) == nj - 1) & (pl.program_id(1) == nk - 1))
    res = (_pc_comm if ne else _pc)(
        hosted, name=name, grid=(nj, nk),
        in_specs=[pl.BlockSpec((tm, D_MODEL), lambda j, k: (k, 0)),
                  pl.BlockSpec((tm, tn), lambda j, k: (k, j))] + [ANY] * ne,
        out_specs=[pl.BlockSpec((D_MODEL, tn), lambda j, k: (0, j))] + [ANY] * ne,
        out_shape=[jax.ShapeDtypeStruct((D_MODEL, n), f32)] + [_exchange_out_shape("scatter", a) for a in scattered],
        scratch_shapes=_exchange_sems(ne), compiler_params=_cparams(("arbitrary", "arbitrary")),
    )(u, dg, *scattered)
    return res if ne else res[0]


def _pad_lanes(v, off):
    n = v.shape[-1]
    return jnp.pad(v.reshape(1, n).astype(jnp.float32), ((0, 0), (off, 128 - off - n)))


def perm_w_in(w_full):
    z = w_full[:, 0:1024]
    xbc = w_full[:, 1024:2560]
    dt = w_full[:, 2560:2576]
    gate = w_full[:, 2576:3600]
    qkv = w_full[:, 3600:6672]
    ab = w_full[:, 6672:6688]
    pad = jnp.zeros((w_full.shape[0], PERM_DIM - IN_DIM), w_full.dtype)
    return jnp.concatenate([z, xbc, gate, qkv, dt, ab, pad], axis=1)


def unperm_w_in(gz, gxbc, ggate, gqkv, gsm):
    return jnp.concatenate([gz, gxbc, gsm[:, 0:16], ggate, gqkv, gsm[:, 16:32]], axis=1)


def all_gather(arrs, name):
    n = len(arrs)

    def body(*refs):
        ins, outs = refs[:n], refs[n:2 * n]
        send_sems, recv_sems, local_sems = refs[2 * n:]
        x, y, c, me = _me()
        sibling = (x, y, 1 - c)
        chips = [(1 - x, y), (x, 1 - y), (1 - x, 1 - y)]

        def idx(px, py, pc):
            return 4 * px + 2 * py + pc

        def copy(a, k, block, to, src=None):
            slot = outs[a].at[idx(*block)]
            return pltpu.make_async_remote_copy(src_ref=slot if src is None else src, dst_ref=slot,
                                                send_sem=send_sems.at[a, k], recv_sem=recv_sems.at[a, k],
                                                device_id=to, device_id_type=MESH)

        local = [pltpu.make_async_copy(ins[a], outs[a].at[me], local_sems.at[a]) for a in range(n)]
        for cp in local:
            cp.start()
        started = []
        for a in range(n):
            first = [copy(a, 0, (x, y, c), sibling, src=ins[a])]
            first += [copy(a, 1 + j, (x, y, c), (*chip, c), src=ins[a]) for j, chip in enumerate(chips)]
            for cp in first:
                cp.start()
            started += first
        for a in range(n):
            for j, chip in enumerate(chips):
                copy(a, 1 + j, (*chip, c), (x, y, c)).wait_recv()
                fwd = copy(a, 4 + j, (*chip, c), sibling)
                fwd.start()
                started.append(fwd)
        for a in range(n):
            copy(a, 0, sibling, (x, y, c)).wait_recv()
            for j, chip in enumerate(chips):
                copy(a, 4 + j, (*chip, 1 - c), (x, y, c)).wait_recv()
        for cp in started:
            cp.wait_send()
        for cp in local:
            cp.wait()

    return _pc_comm(
        body, name=name, in_specs=[ANY] * n, out_specs=[ANY] * n,
        out_shape=[jax.ShapeDtypeStruct((N_DEV,) + a.shape, a.dtype) for a in arrs],
        scratch_shapes=[pltpu.SemaphoreType.DMA((n, 7)), pltpu.SemaphoreType.DMA((n, 7)),
                        pltpu.SemaphoreType.DMA((n,))],
    )(*arrs)


def adamw_sum(recv, w, m, v, rows, name):
    r, ccols = w.shape
    f32 = jnp.float32
    c1 = 1.0 / (1.0 - ADAM_B1 ** ADAM_STEP)
    c2 = 1.0 / (1.0 - ADAM_B2 ** ADAM_STEP)

    def body(recv_ref, w_ref, m_ref, v_ref, g_ref, d_ref, mo_ref, vo_ref):
        g = recv_ref[0].astype(f32)
        for k in range(1, N_DEV):
            g = g + recv_ref[k].astype(f32)
        mn = ADAM_B1 * m_ref[...] + (1.0 - ADAM_B1) * g
        vn = ADAM_B2 * v_ref[...] + (1.0 - ADAM_B2) * (g * g)
        g_ref[...] = g
        mo_ref[...] = mn
        vo_ref[...] = vn
        d_ref[...] = -ADAM_LR * ((mn * c1) / (jnp.sqrt(vn * c2) + ADAM_EPS) + ADAM_WD * w_ref[...])

    blk = pl.BlockSpec((rows, ccols), lambda i: (i, 0))
    return _pc(
        body, name=name, grid=(r // rows,),
        in_specs=[pl.BlockSpec((N_DEV, rows, ccols), lambda i: (0, i, 0)), blk, blk, blk],
        out_specs=[blk] * 4, out_shape=[jax.ShapeDtypeStruct((r, ccols), f32)] * 4,
        compiler_params=_cparams(("arbitrary",)),
    )(recv, w, m, v)


REP = (("norm_w", 1024), ("ssd_conv_b", 1536), ("ssd_dt_bias", 16), ("ssd_a_log", 16), ("ssd_d", 16),
       ("ssd_norm_w", 1024), ("gdn_dt_bias", 8), ("gdn_a_log", 8), ("gdn_norm_w", 128), ("final_norm_w", 1024))
REP_ROWS = 48
SHARD = (("ssd_conv_w", CONV_K * SSD_CONV_DIM // N_DEV), ("gdn_conv_w", CONV_K * GDN_CONV_DIM // N_DEV))
SHARD_ROWS = 24


def _rows_of(size):
    return -(-size // 128)


def _pack(vals, layout, total_rows):
    parts = []
    for (name, size), val in zip(layout, vals):
        flat = val.reshape(-1).astype(jnp.float32)
        parts.append(jnp.pad(flat, (0, _rows_of(size) * 128 - size)).reshape(-1, 128))
    used = sum(_rows_of(s) for _, s in layout)
    parts.append(jnp.zeros((total_rows - used, 128), jnp.float32))
    return jnp.concatenate(parts, axis=0)


def _unpack(packed, layout, row0=0):
    out, r = {}, row0
    for name, size in layout:
        n = _rows_of(size)
        out[name] = packed[r:r + n].reshape(-1)[:size]
        r += n
    return out


def _conv_slabs(g_full):
    k, ccols = g_full.shape
    return g_full.reshape(k, N_DEV, ccols // N_DEV).transpose(1, 0, 2).reshape(N_DEV, -1)


def _conv_full(gathered_flat, ccols):
    return gathered_flat.reshape(N_DEV, CONV_K, ccols // N_DEV).transpose(1, 0, 2).reshape(CONV_K, ccols)


def kernel(x, norm_w, w_in, ssd_conv_w, ssd_conv_b, ssd_dt_bias, ssd_a_log, ssd_d, ssd_norm_w, gdn_conv_w, gdn_dt_bias, gdn_a_log, gdn_norm_w, w_out, final_norm_w, loss_target, m_norm_w, m_w_in, m_ssd_conv_w, m_ssd_conv_b, m_ssd_dt_bias, m_ssd_a_log, m_ssd_d, m_ssd_norm_w, m_gdn_conv_w, m_gdn_dt_bias, m_gdn_a_log, m_gdn_norm_w, m_w_out, m_final_norm_w, v_norm_w, v_w_in, v_ssd_conv_w, v_ssd_conv_b, v_ssd_dt_bias, v_ssd_a_log, v_ssd_d, v_ssd_norm_w, v_gdn_conv_w, v_gdn_dt_bias, v_gdn_a_log, v_gdn_norm_w, v_w_out, v_final_norm_w):
    f32 = jnp.float32
    w = dict(norm_w=norm_w, w_in=w_in, ssd_conv_w=ssd_conv_w, ssd_conv_b=ssd_conv_b, ssd_dt_bias=ssd_dt_bias,
             ssd_a_log=ssd_a_log, ssd_d=ssd_d, ssd_norm_w=ssd_norm_w, gdn_conv_w=gdn_conv_w, gdn_dt_bias=gdn_dt_bias,
             gdn_a_log=gdn_a_log, gdn_norm_w=gdn_norm_w, w_out=w_out, final_norm_w=final_norm_w)
    m = dict(norm_w=m_norm_w, w_in=m_w_in, ssd_conv_w=m_ssd_conv_w, ssd_conv_b=m_ssd_conv_b, ssd_dt_bias=m_ssd_dt_bias,
             ssd_a_log=m_ssd_a_log, ssd_d=m_ssd_d, ssd_norm_w=m_ssd_norm_w, gdn_conv_w=m_gdn_conv_w,
             gdn_dt_bias=m_gdn_dt_bias, gdn_a_log=m_gdn_a_log, gdn_norm_w=m_gdn_norm_w, w_out=m_w_out,
             final_norm_w=m_final_norm_w)
    v = dict(norm_w=v_norm_w, w_in=v_w_in, ssd_conv_w=v_ssd_conv_w, ssd_conv_b=v_ssd_conv_b, ssd_dt_bias=v_ssd_dt_bias,
             ssd_a_log=v_ssd_a_log, ssd_d=v_ssd_d, ssd_norm_w=v_ssd_norm_w, gdn_conv_w=v_gdn_conv_w,
             gdn_dt_bias=v_gdn_dt_bias, gdn_a_log=v_gdn_a_log, gdn_norm_w=v_gdn_norm_w, w_out=v_w_out,
             final_norm_w=v_final_norm_w)
    names = list(w)
    shapes = {n: w[n].shape for n in names}

    xl, tgt = x[0], loss_target[0]
    cs = _consts()
    dtb_s = _pad_lanes(ssd_dt_bias, 0)
    alog_s = _pad_lanes(ssd_a_log, 0)
    dpar = _pad_lanes(ssd_d, 0)
    dtb_g = _pad_lanes(gdn_dt_bias, 16)
    alog_g = _pad_lanes(gdn_a_log, 16)
    nw_g = gdn_norm_w.reshape(1, 128)
    nw_s = ssd_norm_w.reshape(1, 1024)
    cb_s = ssd_conv_b.reshape(1, 1536)
    nw1 = norm_w.reshape(1, D_MODEL)

    (g_w_in,) = all_gather([w_in[0].astype(_MM)], "gather_w_in")
    w_perm = perm_w_in(g_w_in.transpose(1, 0, 2).reshape(D_MODEL, IN_DIM))
    conv_pack = _pack([w["ssd_conv_w"], w["gdn_conv_w"]], SHARD, SHARD_ROWS)
    u, z, xbc, gate, qkv, sm, g_w_out, g_conv = inproj_fwd(xl, nw1, w_perm, [w_out[0].astype(_MM), conv_pack])
    w_out_full = g_w_out.reshape(MIX_WIDTH, D_MODEL)
    ssd_cw = _conv_full(g_conv[:, 0:6].reshape(N_DEV, -1), SSD_CONV_DIM)
    gdn_cw = _conv_full(g_conv[:, 6:18].reshape(N_DEV, -1), GDN_CONV_DIM)

    y_ssd, hs, pre_s = ssd_fwd(z, xbc, sm, ssd_cw, cb_s, dtb_s, alog_s, dpar, nw_s, cs)
    y_gdn, ss, ts, pre_g = gdn_fwd(gate, qkv, sm, gdn_cw, dtb_g, alog_g, nw_g, cs)
    dout, dys, dyg, g_wout, g_fnw, loss_l = out_fwd_bwd(xl, tgt, y_ssd, y_gdn, w_out_full,
                                                        final_norm_w.reshape(1, D_MODEL))
    dz, dxbc, dsm_s, g_cw_s, g_cb_s, g_dtb_s, g_alog_s, g_d, g_nw_s = ssd_bwd(
        z, xbc, pre_s, sm, hs, dys, ssd_cw, dtb_s, alog_s, dpar, nw_s, cs)
    dgate, dqkv, dsm, g_cw_g, g_dtb_g, g_alog_g, g_nw_g = gdn_bwd(
        gate, qkv, pre_g, sm, ss, ts, dyg, dsm_s, gdn_cw, dtb_g, alog_g, nw_g, cs)

    t_w_out = g_wout.reshape(N_DEV, MIX_WIDTH // N_DEV, D_MODEL).astype(_MM)
    gws = {}
    for dg, (name, _, _) in zip((dz, dxbc, dgate, dsm), (GROUPS[0], GROUPS[1], GROUPS[2], GROUPS[4])):
        gws[name] = grad_w_group(u, dg, "grad_w_in_" + name)
    gws["qkv"], r_w_out = grad_w_group(u, dqkv, "grad_w_in_qkv", [t_w_out])
    g_w_in_full = unperm_w_in(gws["z"], gws["xbc"], gws["gate"], gws["qkv"], gws["sm"])
    t_w_in = g_w_in_full.reshape(D_MODEL, N_DEV, W_IN_SHARD).transpose(1, 0, 2).astype(_MM)
    dx, g_nw, r_w_in = inproj_bwd_dx(xl, dout, nw1, w_perm, (dz, dxbc, dgate, dqkv, dsm), [t_w_in])

    g = dict(norm_w=g_nw[0:1, :], ssd_conv_b=g_cb_s[0:1, :], ssd_dt_bias=g_dtb_s[0:1, 0:16],
             ssd_a_log=g_alog_s[0:1, 0:16], ssd_d=g_d[0:1, 0:16], ssd_norm_w=g_nw_s[0:1, :],
             gdn_dt_bias=g_dtb_g[0:1, 16:24], gdn_a_log=g_alog_g[0:1, 16:24], gdn_norm_w=g_nw_g[0:1, :],
             final_norm_w=g_fnw[0:1, :])
    rep = _pack([g[n] for n, _ in REP], REP, REP_ROWS)
    shard_rows = jnp.concatenate([_conv_slabs(g_cw_s[0:4, :]).reshape(N_DEV, 6, 128),
                                  _conv_slabs(g_cw_g[0:4, :]).reshape(N_DEV, 12, 128),
                                  jnp.zeros((N_DEV, SHARD_ROWS - 18, 128), f32)], axis=1)
    t_small = jnp.concatenate([jnp.broadcast_to(rep[None], (N_DEV, REP_ROWS, 128)), shard_rows], axis=1)
    (r_small,) = exchange([t_small], ["scatter"], "scatter_small_grads")

    o_w_in = adamw_sum(r_w_in, w_in[0], m_w_in[0], v_w_in[0], 128, "adamw_w_in")
    o_w_out = adamw_sum(r_w_out, w_out[0], m_w_out[0], v_w_out[0], 64, "adamw_w_out")
    small = [jnp.concatenate([_pack([d[n] for n, _ in REP], REP, REP_ROWS),
                              _pack([d[n] for n, _ in SHARD], SHARD, SHARD_ROWS)], axis=0) for d in (w, m, v)]
    o_small = adamw_sum(r_small, small[0], small[1], small[2], REP_ROWS + SHARD_ROWS, "adamw_small")

    loss = lax.psum(loss_l[0, 0], ("x", "y", "c"))
    outs = [loss, dx[None]]
    for k in range(4):
        parts = {**_unpack(o_small[k], REP), **_unpack(o_small[k], SHARD, REP_ROWS),
                 "w_in": o_w_in[k], "w_out": o_w_out[k]}
        outs += [parts[n].reshape(shapes[n]) for n in names]
    return tuple(outs)
```

```python
import functools

import jax
import jax.numpy as jnp
import numpy as np
from jax import lax
from jax.experimental import pallas as pl
from jax.experimental.pallas import tpu as pltpu

_MM = jnp.bfloat16

D_MODEL = 1024
CHUNK = 64
CONV_K = 4
EPS = 1e-6
SSD_CONV_DIM = 1536
GDN_HEADS = 8
GDN_DK = 128
GDN_CONV_DIM = 3072
MIX_WIDTH = 2048
IN_DIM = 6688
N_DEV = 8
W_IN_SHARD = IN_DIM // N_DEV
PERM_DIM = 6784
HI = lax.Precision.HIGHEST
HIGH = lax.Precision.HIGH
VMEM_LIMIT = 56 * 1024 * 1024

ADAM_LR = 0.001
ADAM_B1 = 0.9
ADAM_B2 = 0.999
ADAM_EPS = 1e-08
ADAM_WD = 0.01
ADAM_STEP = 10


def _pc(body, **kw):
    return pl.pallas_call(body, **kw)


def _pc_comm(body, **kw):
    return pl.pallas_call(body, **kw)


def _cparams(sem):
    return pltpu.CompilerParams(dimension_semantics=sem, vmem_limit_bytes=VMEM_LIMIT)


def _sig(x):
    return 0.5 * jnp.tanh(0.5 * x) + 0.5


@jax.custom_vjp
def _sigmoid(x):
    return _sig(x)


def _sigmoid_fwd(x):
    s = _sig(x)
    return s, s


def _sigmoid_bwd(s, g):
    return (g * s * (1.0 - s),)


_sigmoid.defvjp(_sigmoid_fwd, _sigmoid_bwd)


@jax.custom_vjp
def _silu(x):
    return x * _sig(x)


def _silu_fwd(x):
    s = _sig(x)
    return x * s, (x, s)


def _silu_bwd(res, g):
    x, s = res
    return (g * (s * (1.0 + x * (1.0 - s))),)


_silu.defvjp(_silu_fwd, _silu_bwd)


def _softplus_impl(x):
    return jnp.maximum(x, 0.0) + jnp.log(1.0 + jnp.exp(-jnp.abs(x)))


@jax.custom_vjp
def _softplus(x):
    return _softplus_impl(x)


def _softplus_fwd(x):
    return _softplus_impl(x), x


def _softplus_bwd(x, g):
    return (g * _sig(x),)


_softplus.defvjp(_softplus_fwd, _softplus_bwd)


def _lane_bcast_impl(x, k):
    return jnp.broadcast_to(x[..., k:k + 1], x.shape)


@functools.partial(jax.custom_vjp, nondiff_argnums=(1,))
def _lane_bcast(x, k):
    return _lane_bcast_impl(x, k)


def _lane_bcast_fwd(x, k):
    return _lane_bcast_impl(x, k), None


def _lane_bcast_bwd(k, _, g):
    lane = lax.broadcasted_iota(jnp.int32, g.shape, g.ndim - 1)
    return (jnp.where(lane == k, jnp.sum(g, axis=-1, keepdims=True), 0.0),)


_lane_bcast.defvjp(_lane_bcast_fwd, _lane_bcast_bwd)


def _mm(a, b):
    return jnp.dot(a.astype(_MM), b.astype(_MM), preferred_element_type=jnp.float32)


def _mm_nt(a, b):
    return lax.dot_general(a.astype(_MM), b.astype(_MM), (((1,), (1,)), ((), ())),
                           preferred_element_type=jnp.float32)


def _mm_tn(a, b):
    return lax.dot_general(a.astype(_MM), b.astype(_MM), (((0,), (0,)), ((), ())),
                           preferred_element_type=jnp.float32)


def _dot_hi(a, b):
    return jnp.dot(a, b, precision=HI, preferred_element_type=jnp.float32)


def _bmm(a, b):
    return lax.dot_general(a.astype(_MM), b.astype(_MM), (((2,), (1,)), ((0,), (0,))),
                           preferred_element_type=jnp.float32)


def _bmm_nt(a, b):
    return lax.dot_general(a.astype(_MM), b.astype(_MM), (((2,), (2,)), ((0,), (0,))),
                           preferred_element_type=jnp.float32)


def _bmm_tn(a, b):
    return lax.dot_general(a.astype(_MM), b.astype(_MM), (((1,), (1,)), ((0,), (0,))),
                           preferred_element_type=jnp.float32)


def _bmm_hi(a, b):
    return lax.dot_general(a, b, (((2,), (1,)), ((0,), (0,))), precision=HIGH, preferred_element_type=jnp.float32)


def _bmm_nt_hi(a, b):
    return lax.dot_general(a, b, (((2,), (2,)), ((0,), (0,))), precision=HIGH, preferred_element_type=jnp.float32)


def _bmm_tn_hi(a, b):
    return lax.dot_general(a, b, (((1,), (1,)), ((0,), (0,))), precision=HIGH, preferred_element_type=jnp.float32)


def _consts():
    l = np.arange(CHUNK)
    tri = (l[:, None] >= l[None, :]).astype(np.float32)
    lane = np.arange(128)
    i2 =(l[:, None] == (lane[None, :] % 64)).astype(np.float32)
    mask2 = (l[:, None] >= (lane[None, :] % 64)).astype(np.float32)
    lo = (lane < 64).astype(np.float32)[None, :]
    i64 = np.eye(CHUNK, dtype=np.float32)
    strict = (l[:, None] > l[None, :]).astype(np.float32)
    return dict(tri=jnp.asarray(tri), i2=jnp.asarray(i2), mask2=jnp.asarray(mask2), lo=jnp.asarray(lo),
                i64=jnp.asarray(i64), strict=jnp.asarray(strict))


def _ssd_chunk(xs_pre, b_pre, c_pre, z, sm, ht, dtb, alog, dpar, nw, tri, i2, mask2, lo):
    lane = lax.broadcasted_iota(jnp.int32, (1, 128), 1)
    m16 = lane < 16
    dt = jnp.where(m16, _softplus(sm + dtb), 0.0)
    a_neg = -jnp.exp(alog)
    cum = _dot_hi(tri, dt * a_neg)
    row = lax.broadcasted_iota(jnp.int32, (CHUNK, 1), 0)
    is_last = row == CHUNK - 1
    hi = 1.0 - lo
    bm = [_silu(b) for b in b_pre]
    cm = [_silu(c) for c in c_pre]
    cb2 = [_mm_nt(cm[g], jnp.concatenate([bm[g], bm[g]], axis=0)) for g in range(2)]
    yg, ht_next = [], []
    for j in range(8):
        g = j // 4
        pair = lambda v, j=j: jnp.where(lo > 0.5, _lane_bcast(v, 2 * j), _lane_bcast(v, 2 * j + 1))
        xs = _silu(xs_pre[j])
        dte = pair(dt)
        cume = pair(cum)
        cum_last = jnp.sum(jnp.where(is_last, cume, 0.0), axis=0, keepdims=True)
        xdt = xs * dte
        rowv = jnp.sum(cume * i2, axis=0, keepdims=True)
        lm = jnp.exp(jnp.where(mask2 > 0.5, cume - rowv, -jnp.inf))
        m = cb2[g] * lm
        xblk = jnp.concatenate([xdt * lo, xdt * hi], axis=0)
        y = _mm(m, xblk)
        y = y + _mm(cm[g], ht[j]) * jnp.exp(cume)
        y = y + pair(dpar) * xs
        yg.append(y * _silu(z[j]))
        st = _mm_tn(bm[g], xdt * jnp.exp(cum_last - cume))
        ht_next.append(ht[j] * jnp.exp(cum_last) + st)
    outs = []
    for g in range(2):
        ss = sum(jnp.sum(yg[j] * yg[j], axis=-1, keepdims=True) for j in range(4 * g, 4 * g + 4))
        rs = lax.rsqrt(ss * (1.0 / 512.0) + EPS)
        for j in range(4 * g, 4 * g + 4):
            outs.append(yg[j] * rs * nw[j])
    return outs, ht_next


def _tri_inverse(a):
    eye = jnp.eye(CHUNK, dtype=jnp.float32)[None]
    p = eye - a
    ap = a
    for _ in range(5):
        ap = _bmm_hi(ap, ap)
        p = p + _bmm_hi(p, ap)
    return p


@jax.custom_vjp
def _solve(a, r1, r2, t):
    return _bmm_hi(t, r1), _bmm_hi(t, r2)


def _solve_fwd(a, r1, r2, t):
    u, w = _bmm_hi(t, r1), _bmm_hi(t, r2)
    return (u, w), (t, u, w)


def _solve_bwd(res, cts):
    t, u, w = res
    du, dw = cts
    dr1 = _bmm_tn_hi(t, du)
    dr2 = _bmm_tn_hi(t, dw)
    da = -(_bmm_nt_hi(dr1, u) + _bmm_nt_hi(dr2, w))
    return da, dr1, dr2, jnp.zeros_like(t)


_solve.defvjp(_solve_fwd, _solve_bwd)


def _gdn_chunk(q_pre, k_pre, v_pre, gate, sm, s, dtb, alog, nw, tri, i64, strict, t_in=None):
    lane = lax.broadcasted_iota(jnp.int32, (1, 128), 1)
    m_a = (lane >= 16) & (lane < 24)
    g_full = jnp.where(m_a, -jnp.exp(alog) * _softplus(sm + dtb), 0.0)
    gc = _dot_hi(tri, g_full)
    sig = _sigmoid(sm)
    gc3 = jnp.stack([_lane_bcast(gc, 16 + h) for h in range(GDN_HEADS)])
    beta3 = jnp.stack([_lane_bcast(sig, 24 + h) for h in range(GDN_HEADS)])
    q = _silu(q_pre)
    q = q * lax.rsqrt(jnp.sum(q * q, axis=-1, keepdims=True) + EPS) * (GDN_DK ** -0.5)
    k = _silu(k_pre)
    k = k * lax.rsqrt(jnp.sum(k * k, axis=-1, keepdims=True) + EPS)
    v = _silu(v_pre)
    gcl = gc3[:, :, :CHUNK]
    gc_row = jnp.sum(gcl * i64[None], axis=1, keepdims=True)
    incl = (strict + i64)[None] > 0.5
    decay = jnp.exp(jnp.where(incl, gcl - gc_row, -jnp.inf))
    kb = k * beta3
    a = jnp.where(strict[None] > 0.5, _bmm_nt(kb, k) * decay, 0.0)
    egc = jnp.exp(gc3)
    t = _tri_inverse(a) if t_in is None else t_in
    u, w = _solve(a, v * beta3, kb * egc, t)
    attn = _bmm_nt(q, k) * decay
    row = lax.broadcasted_iota(jnp.int32, (1, CHUNK, 1), 1)
    gl = jnp.sum(jnp.where(row == CHUNK - 1, gc3, 0.0), axis=1, keepdims=True)
    q_dec = q * egc
    k_dec = k * jnp.exp(gl - gc3)
    v_new = u - _bmm(w, s)
    o = _bmm(q_dec, s) + _bmm(attn, v_new)
    s_next = s * jnp.exp(gl) + _bmm_tn(k_dec, v_new)
    on = o * lax.rsqrt(jnp.mean(o * o, axis=-1, keepdims=True) + EPS) * nw
    return on * _silu(gate), s_next, t


def _conv_fwd(pbuf, w_ref, c0, c1):
    acc = None
    for j in range(CONV_K):
        term = w_ref[j:j + 1, c0:c1] * pbuf[5 + j:69 + j, c0:c1]
        acc = term if acc is None else acc + term
    return acc


MESH = pl.DeviceIdType.MESH
ANY = pl.BlockSpec(memory_space=pl.ANY)


def _me():
    x, y, c = lax.axis_index("x"), lax.axis_index("y"), lax.axis_index("c")
    return x, y, c, 4 * x + 2 * y + c


def _peer(r):
    x, y, c, _ = _me()
    px = 1 - x if r & 4 else x
    py = 1 - y if r & 2 else y
    pc = 1 - c if r & 1 else c
    return (px, py, pc), 4 * px + 2 * py + pc


def _exchange_ops(kind, in_ref, out_ref, send_sems, recv_sems, local_sem):
    me = _me()[3]
    local = pltpu.make_async_copy(in_ref.at[me] if kind == "scatter" else in_ref, out_ref.at[me], local_sem)
    sends, recvs = [], []
    for r in range(1, N_DEV):
        peer, pidx = _peer(r)
        src = in_ref.at[pidx] if kind == "scatter" else in_ref
        sems = dict(send_sem=send_sems.at[r - 1], recv_sem=recv_sems.at[r - 1], device_id=peer, device_id_type=MESH)
        sends.append(pltpu.make_async_remote_copy(src_ref=src, dst_ref=out_ref.at[me], **sems))
        recvs.append(pltpu.make_async_remote_copy(src_ref=src, dst_ref=out_ref.at[pidx], **sems))

    def start():
        local.start()
        for cp in sends:
            cp.start()

    def wait():
        for cp in recvs:
            cp.wait_recv()
        for cp in sends:
            cp.wait_send()
        local.wait()

    return start, wait


def _exchange_sems(n):
    return [pltpu.SemaphoreType.DMA((N_DEV - 1,)), pltpu.SemaphoreType.DMA((N_DEV - 1,)),
            pltpu.SemaphoreType.DMA(())] * n


def _exchange_out_shape(kind, a):
    return jax.ShapeDtypeStruct(a.shape if kind == "scatter" else (N_DEV,) + a.shape, a.dtype)


def _hosting(body, n_in, n_out, n_scratch, kinds, first, last):
    ne = len(kinds)

    def wrapped(*refs):
        ins, ex_in = refs[:n_in], refs[n_in:n_in + ne]
        o0 = n_in + ne
        outs, ex_out = refs[o0:o0 + n_out], refs[o0 + n_out:o0 + n_out + ne]
        s0 = o0 + n_out + ne
        scr, sems = refs[s0:s0 + n_scratch], refs[s0 + n_scratch:]
        ops = [_exchange_ops(kinds[e], ex_in[e], ex_out[e], *sems[3 * e:3 * e + 3]) for e in range(ne)]

        @pl.when(first())
        def _():
            for start, _ in ops:
                start()

        body(*ins, *outs, *scr)

        @pl.when(last())
        def _():
            for _, wait in ops:
                wait()

    return wrapped


def exchange(arrs, kinds, name):
    n = len(arrs)

    def body(*refs):
        ins, outs, sems = refs[:n], refs[n:2 * n], refs[2 * n:]
        ops = [_exchange_ops(kinds[e], ins[e], outs[e], *sems[3 * e:3 * e + 3]) for e in range(n)]
        for start, _ in ops:
            start()
        for _, wait in ops:
            wait()

    return _pc_comm(
        body, name=name, in_specs=[ANY] * n, out_specs=[ANY] * n,
        out_shape=[_exchange_out_shape(k, a) for k, a in zip(kinds, arrs)], scratch_shapes=_exchange_sems(n),
    )(*arrs)


GROUPS = (("z", 0, 1024), ("xbc", 1024, 2560), ("gate", 2560, 3584), ("qkv", 3584, 6656), ("sm", 6656, 6784))


def inproj_fwd(x, norm_w, w_perm, gathered):
    t = x.shape[0]
    tm = min(256, t)
    steps = t // tm
    kinds = ["gather"] * len(gathered)

    def body(x_ref, nw_ref, w_ref, u_ref, z_ref, xbc_ref, gate_ref, qkv_ref, sm_ref):
        xf = x_ref[...]
        rstd = lax.rsqrt(jnp.mean(xf * xf, axis=-1, keepdims=True) + EPS)
        u = (xf * rstd * nw_ref[...]).astype(_MM)
        u_ref[...] = u
        for (name, c0, c1), o_ref in zip(GROUPS, (z_ref, xbc_ref, gate_ref, qkv_ref, sm_ref)):
            o_ref[...] = jnp.dot(u, w_ref[:, c0:c1], preferred_element_type=jnp.float32)

    outs = [jax.ShapeDtypeStruct((t, D_MODEL), _MM)] + [jax.ShapeDtypeStruct((t, c1 - c0), jnp.float32)
                                                        for _, c0, c1 in GROUPS]
    hosted = _hosting(body, 3, 6, 0, kinds, lambda: pl.program_id(0) == 0, lambda: pl.program_id(0) == steps - 1)
    return _pc_comm(
        hosted, name="inproj_fwd", grid=(steps,),
        in_specs=[pl.BlockSpec((tm, D_MODEL), lambda i: (i, 0)),
                  pl.BlockSpec((1, D_MODEL), lambda i: (0, 0)),
                  pl.BlockSpec((D_MODEL, PERM_DIM), lambda i: (0, 0))] + [ANY] * len(gathered),
        out_specs=[pl.BlockSpec((tm, D_MODEL), lambda i: (i, 0))] +
                  [pl.BlockSpec((tm, c1 - c0), lambda i: (i, 0)) for _, c0, c1 in GROUPS] + [ANY] * len(gathered),
        out_shape=outs + [_exchange_out_shape("gather", a) for a in gathered],
        scratch_shapes=_exchange_sems(len(gathered)), compiler_params=_cparams(("arbitrary",)),
    )(x, norm_w, w_perm, *gathered)


def _halo_spec(width, idx_fn):
    return pl.BlockSpec((8, width), lambda i: (jnp.maximum(idx_fn(i) * 8 - 1, 0), 0))


def _full(shape):
    nd = len(shape)
    return pl.BlockSpec(shape, lambda i: (0,) * nd)


def _ssd_split(pre_fn, z_ref, sm_ref):
    xs_pre = [pre_fn(128 * j, 128 * j + 128) for j in range(8)]
    b_pre = [pre_fn(1024 + 128 * g, 1152 + 128 * g) for g in range(2)]
    c_pre = [pre_fn(1280 + 128 * g, 1408 + 128 * g) for g in range(2)]
    z = [z_ref[:, 128 * j:128 * j + 128] for j in range(8)]
    return xs_pre, b_pre, c_pre, z, sm_ref[...]


def ssd_fwd(z, xbc, sm, conv_w, conv_b, dtb, alog, dpar, nw, cs):
    t = z.shape[0]
    nc = t // CHUNK

    def body(shared, z_ref, xbc_ref, halo_ref, sm_ref, cw_ref, cb_ref, dtb_ref, alog_ref, dpar_ref, nw_ref,
             tri_ref, i2_ref, mask2_ref, lo_ref, y_ref, hs_ref, pre_ref, pbuf, ht_scr):
        i = pl.program_id(0)

        @pl.when(i == 0)
        def _():
            ht_scr[...] = jnp.zeros_like(ht_scr)

        pbuf[0:8, :] = jnp.where(i == 0, 0.0, halo_ref[...])
        pbuf[8:72, :] = xbc_ref[...]

        def pre_fn(c0, c1):
            pre = _conv_fwd(pbuf, cw_ref, c0, c1) + cb_ref[:, c0:c1]
            pre_ref[:, c0:c1] = pre
            return pre

        xs_pre, b_pre, c_pre, zz, smv = _ssd_split(pre_fn, z_ref, sm_ref)
        ht = [ht_scr[:, 128 * j:128 * j + 128] for j in range(8)]
        hs_ref[0] = ht_scr[...]
        nwl = [nw_ref[:, 128 * j:128 * j + 128] for j in range(8)]
        outs, ht_next = _ssd_chunk(xs_pre, b_pre, c_pre, zz, smv, ht, dtb_ref[...], alog_ref[...], dpar_ref[...],
                                   nwl, tri_ref[...], i2_ref[...], mask2_ref[...], lo_ref[...])
        for j in range(8):
            y_ref[:, 128 * j:128 * j + 128] = outs[j].astype(y_ref.dtype)
            ht_scr[:, 128 * j:128 * j + 128] = ht_next[j]

    blk = lambda w: pl.BlockSpec((CHUNK, w), lambda i: (i, 0))
    return dict(
        body=body,
        in_specs=[blk(1024), blk(1536), _halo_spec(1536, lambda i: i), blk(128),
                  _full((CONV_K, 1536)), _full((1, 1536)), _full((1, 128)), _full((1, 128)), _full((1, 128)),
                  _full((1, 1024)), _full((64, 64)), _full((64, 128)), _full((64, 128)),
                  _full((1, 128))],
        out_specs=[blk(1024), pl.BlockSpec((1, 128, 1024), lambda i: (i, 0, 0)), blk(1536)],
        out_shape=[jax.ShapeDtypeStruct((t, 1024), _MM), jax.ShapeDtypeStruct((nc, 128, 1024), jnp.float32),
                   jax.ShapeDtypeStruct((t, 1536), jnp.float32)],
        scratch=[pltpu.VMEM((72, 1536), jnp.float32), pltpu.VMEM((128, 1024), jnp.float32)],
        args=[z, xbc, xbc, sm, conv_w, conv_b, dtb, alog, dpar, nw, cs["tri"], cs["i2"], cs["mask2"], cs["lo"]])


def _conv_bwd(dpre_list, col_ranges, dbuf, carry, x_ref, cw_ref, dx_ref, dcw_ref, dcb_ref, first):
    for dpre, (c0, c1) in zip(dpre_list, col_ranges):
        dbuf[0:64, c0:c1] = dpre
    dbuf[64:72, :] = jnp.where(first, 0.0, carry[...])
    carry[...] = dbuf[0:8, :]
    for (c0, c1) in col_ranges:
        xin = x_ref[:, c0:c1]
        acc = None
        for j in range(CONV_K):
            sh = dbuf[3 - j:67 - j, c0:c1]
            term = cw_ref[j:j + 1, c0:c1] * sh
            acc = term if acc is None else acc + term
            dcw_ref[j:j + 1, c0:c1] += jnp.sum(xin * sh, axis=0, keepdims=True)
        dx_ref[:, c0:c1] = acc
        if dcb_ref is not None:
            dcb_ref[0:1, c0:c1] += jnp.sum(dbuf[0:64, c0:c1], axis=0, keepdims=True)


def ssd_bwd(z, xbc, pre, sm, hs, dy, conv_w, dtb, alog, dpar, nw, cs):
    t = z.shape[0]
    nc = t // CHUNK

    def body(shared, z_ref, xbc_ref, pre_ref, sm_ref, hs_ref, dy_ref, cw_ref, dtb_ref, alog_ref, dpar_ref, nw_ref,
             tri_ref, i2_ref, mask2_ref, lo_ref,
             dz_ref, dxbc_ref, dcw_ref, dcb_ref, ddtb_ref, dalog_ref, ddpar_ref, dnw_ref,
             dbuf, carry, dht_scr):
        i = pl.program_id(0)

        @pl.when(i == 0)
        def _():
            dht_scr[...] = jnp.zeros_like(dht_scr)
            dcw_ref[...] = jnp.zeros_like(dcw_ref)
            dcb_ref[...] = jnp.zeros_like(dcb_ref)
            ddtb_ref[...] = jnp.zeros_like(ddtb_ref)
            dalog_ref[...] = jnp.zeros_like(dalog_ref)
            ddpar_ref[...] = jnp.zeros_like(ddpar_ref)
            dnw_ref[...] = jnp.zeros_like(dnw_ref)

        pre_fn = lambda c0, c1: pre_ref[:, c0:c1]
        xs_pre, b_pre, c_pre, zz, smv = _ssd_split(pre_fn, z_ref, sm_ref)
        ht = [hs_ref[0, :, 128 * j:128 * j + 128] for j in range(8)]
        nwl = [nw_ref[:, 128 * j:128 * j + 128] for j in range(8)]
        consts = (tri_ref[...], i2_ref[...], mask2_ref[...], lo_ref[...])

        def f(xs_pre, b_pre, c_pre, zz, smv, ht, dtb, alog, dpar, nwl):
            return _ssd_chunk(xs_pre, b_pre, c_pre, zz, smv, ht, dtb, alog, dpar, nwl, *consts)

        _, vjp = jax.vjp(f, xs_pre, b_pre, c_pre, zz, smv, ht, dtb_ref[...], alog_ref[...], dpar_ref[...], nwl)
        dys = [dy_ref[:, 128 * j:128 * j + 128] for j in range(8)]
        dhts = [dht_scr[:, 128 * j:128 * j + 128] for j in range(8)]
        dxs, db, dc, dzz, dsm, dht, ddtb, dalog, ddpar, dnwl = vjp((dys, dhts))
        for j in range(8):
            dz_ref[:, 128 * j:128 * j + 128] = dzz[j]
            dht_scr[:, 128 * j:128 * j + 128] = dht[j]
            dnw_ref[0:1, 128 * j:128 * j + 128] += dnwl[j]
        shared["dsm_ssd"] = dsm
        ddtb_ref[0:1, :] += ddtb
        dalog_ref[0:1, :] += dalog
        ddpar_ref[0:1, :] += ddpar
        ranges = ([(128 * j, 128 * j + 128) for j in range(8)] + [(1024 + 128 * g, 1152 + 128 * g) for g in range(2)]
                  + [(1280 + 128 * g, 1408 + 128 * g) for g in range(2)])
        _conv_bwd(dxs + db + dc, ranges, dbuf, carry, xbc_ref, cw_ref, dxbc_ref, dcw_ref, dcb_ref, i == 0)

    rblk = lambda w: pl.BlockSpec((CHUNK, w), lambda i: (nc - 1 - i, 0))
    acc = lambda w: pl.BlockSpec((8, w), lambda i: (0, 0))
    f32 = jnp.float32
    return dict(
        body=body,
        in_specs=[rblk(1024), rblk(1536), rblk(1536), rblk(128),
                  pl.BlockSpec((1, 128, 1024), lambda i: (nc - 1 - i, 0, 0)), rblk(1024),
                  _full((CONV_K, 1536)), _full((1, 128)), _full((1, 128)), _full((1, 128)),
                  _full((1, 1024)), _full((64, 64)), _full((64, 128)), _full((64, 128)),
                  _full((1, 128))],
        out_specs=[rblk(1024), rblk(1536), acc(1536), acc(1536), acc(128), acc(128), acc(128), acc(1024)],
        out_shape=[jax.ShapeDtypeStruct((t, 1024), f32), jax.ShapeDtypeStruct((t, 1536), f32),
                   jax.ShapeDtypeStruct((8, 1536), f32),
                   jax.ShapeDtypeStruct((8, 1536), f32), jax.ShapeDtypeStruct((8, 128), f32),
                   jax.ShapeDtypeStruct((8, 128), f32), jax.ShapeDtypeStruct((8, 128), f32),
                   jax.ShapeDtypeStruct((8, 1024), f32)],
        scratch=[pltpu.VMEM((72, 1536), f32), pltpu.VMEM((8, 1536), f32), pltpu.VMEM((128, 1024), f32)],
        args=[z, xbc, pre, sm, hs, dy, conv_w, dtb, alog, dpar, nw, cs["tri"], cs["i2"], cs["mask2"], cs["lo"]])


def _gdn_split(pre_fn, gate_ref):
    def heads(base):
        return jnp.stack([pre_fn(base + 128 * h, base + 128 * h + 128) for h in range(GDN_HEADS)])
    gate = jnp.stack([gate_ref[:, 128 * h:128 * h + 128] for h in range(GDN_HEADS)])
    return heads(0), heads(1024), heads(2048), gate


def gdn_fwd(gate, qkv, sm, conv_w, dtb, alog, nw, cs):
    t = gate.shape[0]
    nc = t // CHUNK

    def body(shared, gate_ref, qkv_ref, halo_ref, sm_ref, cw_ref, dtb_ref, alog_ref, nw_ref,
             tri_ref, i64_ref, strict_ref, o_ref, ss_ref, ts_ref, pre_ref, pbuf, s_scr):
        i = pl.program_id(0)

        @pl.when(i == 0)
        def _():
            s_scr[...] = jnp.zeros_like(s_scr)

        pbuf[0:8, :] = jnp.where(i == 0, 0.0, halo_ref[...])
        pbuf[8:72, :] = qkv_ref[...]

        def pre_fn(c0, c1):
            pre = _conv_fwd(pbuf, cw_ref, c0, c1)
            pre_ref[:, c0:c1] = pre
            return pre

        q_pre, k_pre, v_pre, g3 = _gdn_split(pre_fn, gate_ref)
        s = s_scr[...]
        ss_ref[0] = s
        out, s_next, tinv = _gdn_chunk(q_pre, k_pre, v_pre, g3, sm_ref[...], s, dtb_ref[...], alog_ref[...],
                                       nw_ref[...], tri_ref[...], i64_ref[...], strict_ref[...])
        ts_ref[0] = tinv
        s_scr[...] = s_next
        for h in range(GDN_HEADS):
            o_ref[:, 128 * h:128 * h + 128] = out[h].astype(o_ref.dtype)

    blk = lambda w: pl.BlockSpec((CHUNK, w), lambda i: (i, 0))
    return dict(
        body=body,
        in_specs=[blk(1024), blk(3072), _halo_spec(3072, lambda i: i), blk(128),
                  _full((CONV_K, 3072)), _full((1, 128)), _full((1, 128)), _full((1, 128)),
                  _full((64, 64)), _full((64, 64)), _full((64, 64))],
        out_specs=[blk(1024), pl.BlockSpec((1, 8, 128, 128), lambda i: (i, 0, 0, 0)),
                   pl.BlockSpec((1, 8, CHUNK, CHUNK), lambda i: (i, 0, 0, 0)), blk(3072)],
        out_shape=[jax.ShapeDtypeStruct((t, 1024), _MM), jax.ShapeDtypeStruct((nc, 8, 128, 128), jnp.float32),
                   jax.ShapeDtypeStruct((nc, 8, CHUNK, CHUNK), jnp.float32),
                   jax.ShapeDtypeStruct((t, 3072), jnp.float32)],
        scratch=[pltpu.VMEM((72, 3072), jnp.float32), pltpu.VMEM((8, 128, 128), jnp.float32)],
        args=[gate, qkv, qkv, sm, conv_w, dtb, alog, nw, cs["tri"], cs["i64"], cs["strict"]])


def gdn_bwd(gate, qkv, pre, sm, ss, ts, do, conv_w, dtb, alog, nw, cs):
    t = gate.shape[0]
    nc = t // CHUNK

    def body(shared, gate_ref, qkv_ref, pre_ref, sm_ref, ss_ref, ts_ref, do_ref, cw_ref, dtb_ref, alog_ref,
             nw_ref, tri_ref, i64_ref, strict_ref,
             dgate_ref, dqkv_ref, dsm_ref, dcw_ref, ddtb_ref, dalog_ref, dnw_ref,
             dbuf, carry, ds_scr):
        i = pl.program_id(0)

        @pl.when(i == 0)
        def _():
            ds_scr[...] = jnp.zeros_like(ds_scr)
            dcw_ref[...] = jnp.zeros_like(dcw_ref)
            ddtb_ref[...] = jnp.zeros_like(ddtb_ref)
            dalog_ref[...] = jnp.zeros_like(dalog_ref)
            dnw_ref[...] = jnp.zeros_like(dnw_ref)

        q_pre, k_pre, v_pre, g3 = _gdn_split(lambda c0, c1: pre_ref[:, c0:c1], gate_ref)
        consts = (tri_ref[...], i64_ref[...], strict_ref[...], ts_ref[0])

        def f(q_pre, k_pre, v_pre, g3, smv, s, dtb, alog, nwv):
            return _gdn_chunk(q_pre, k_pre, v_pre, g3, smv, s, dtb, alog, nwv, *consts)[:2]

        _, vjp = jax.vjp(f, q_pre, k_pre, v_pre, g3, sm_ref[...], ss_ref[0], dtb_ref[...], alog_ref[...], nw_ref[...])
        do3 = jnp.stack([do_ref[:, 128 * h:128 * h + 128] for h in range(GDN_HEADS)])
        dq, dk, dv, dg3, dsm, ds, ddtb, dalog, dnw = vjp((do3, ds_scr[...]))
        ds_scr[...] = ds
        for h in range(GDN_HEADS):
            dgate_ref[:, 128 * h:128 * h + 128] = dg3[h]
        dsm_ref[...] = dsm + shared["dsm_ssd"]
        ddtb_ref[0:1, :] += ddtb
        dalog_ref[0:1, :] += dalog
        dnw_ref[0:1, :] += dnw
        ranges = [(base + 128 * h, base + 128 * h + 128) for base in (0, 1024, 2048) for h in range(GDN_HEADS)]
        dlist = [d[h] for d in (dq, dk, dv) for h in range(GDN_HEADS)]
        _conv_bwd(dlist, ranges, dbuf, carry, qkv_ref, cw_ref, dqkv_ref, dcw_ref, None, i == 0)

    rblk = lambda w: pl.BlockSpec((CHUNK, w), lambda i: (nc - 1 - i, 0))
    acc = lambda w: pl.BlockSpec((8, w), lambda i: (0, 0))
    f32 = jnp.float32
    return dict(
        body=body,
        in_specs=[rblk(1024), rblk(3072), rblk(3072), rblk(128),
                  pl.BlockSpec((1, 8, 128, 128), lambda i: (nc - 1 - i, 0, 0, 0)),
                  pl.BlockSpec((1, 8, CHUNK, CHUNK), lambda i: (nc - 1 - i, 0, 0, 0)), rblk(1024),
                  _full((CONV_K, 3072)), _full((1, 128)), _full((1, 128)), _full((1, 128)),
                  _full((64, 64)), _full((64, 64)), _full((64, 64))],
        out_specs=[rblk(1024), rblk(3072), rblk(128), acc(3072), acc(128), acc(128), acc(128)],
        out_shape=[jax.ShapeDtypeStruct((t, 1024), f32), jax.ShapeDtypeStruct((t, 3072), f32),
                   jax.ShapeDtypeStruct((t, 128), f32), jax.ShapeDtypeStruct((8, 3072), f32),
                   jax.ShapeDtypeStruct((8, 128), f32), jax.ShapeDtypeStruct((8, 128), f32),
                   jax.ShapeDtypeStruct((8, 128), f32)],
        scratch=[pltpu.VMEM((72, 3072), f32), pltpu.VMEM((8, 3072), f32), pltpu.VMEM((8, 128, 128), f32)],
        args=[gate, qkv, pre, sm, ss, ts, do, conv_w, dtb, alog, nw, cs["tri"], cs["i64"], cs["strict"]])


def _chunk_call(parts, name, nc):
    n_in = [len(p["args"]) for p in parts]
    n_out = [len(p["out_shape"]) for p in parts]
    n_scr = [len(p["scratch"]) for p in parts]

    def body(*refs):
        ins, outs, scr = refs[:sum(n_in)], refs[sum(n_in):sum(n_in) + sum(n_out)], refs[sum(n_in) + sum(n_out):]
        shared = {}
        for k, p in enumerate(parts):
            i0, o0, s0 = sum(n_in[:k]), sum(n_out[:k]), sum(n_scr[:k])
            p["body"](shared, *ins[i0:i0 + n_in[k]], *outs[o0:o0 + n_out[k]], *scr[s0:s0 + n_scr[k]])

    cat = lambda key: [v for p in parts for v in p[key]]
    return _pc(body, name=name, grid=(nc,), in_specs=cat("in_specs"), out_specs=cat("out_specs"),
               out_shape=cat("out_shape"), scratch_shapes=cat("scratch"),
               compiler_params=_cparams(("arbitrary",)))(*cat("args"))


def out_fwd_bwd(x, tgt, y_ssd, y_gdn, w_out, fnw):
    t = x.shape[0]
    tm = min(512, t)
    f32 = jnp.float32

    def body(x_ref, tgt_ref, ys_ref, yg_ref, w_ref, fnw_ref,
             dout_ref, dys_ref, dyg_ref, gw_ref, gfnw_ref, loss_ref):
        i = pl.program_id(0)

        @pl.when(i == 0)
        def _():
            gw_ref[...] = jnp.zeros_like(gw_ref)
            gfnw_ref[...] = jnp.zeros_like(gfnw_ref)
            loss_ref[...] = jnp.zeros_like(loss_ref)

        ys = ys_ref[...]
        yg = yg_ref[...]
        out = x_ref[...] + jnp.dot(ys, w_ref[0:1024, :], preferred_element_type=f32) \
            + jnp.dot(yg, w_ref[1024:2048, :], preferred_element_type=f32)
        rstd = lax.rsqrt(jnp.mean(out * out, axis=-1, keepdims=True) + EPS)
        yhat = out * rstd
        fw = fnw_ref[...]
        e = yhat * fw - tgt_ref[...]
        loss_ref[...] += 0.5 * jnp.sum(jnp.sum(e * e, axis=-1, keepdims=True) * (1.0 / D_MODEL), axis=0, keepdims=True)
        dyf = e * (1.0 / D_MODEL)
        gfnw_ref[0:1, :] += jnp.sum(dyf * yhat, axis=0, keepdims=True)
        dyhat = dyf * fw
        dout = rstd * (dyhat - yhat * jnp.mean(dyhat * yhat, axis=-1, keepdims=True))
        dout_ref[...] = dout
        db = dout.astype(_MM)
        dys_ref[...] = lax.dot_general(db, w_ref[0:1024, :], (((1,), (1,)), ((), ())), preferred_element_type=f32)
        dyg_ref[...] = lax.dot_general(db, w_ref[1024:2048, :], (((1,), (1,)), ((), ())), preferred_element_type=f32)
        gw_ref[0:1024, :] += lax.dot_general(ys, db, (((0,), (0,)), ((), ())), preferred_element_type=f32)
        gw_ref[1024:2048, :] += lax.dot_general(yg, db, (((0,), (0,)), ((), ())), preferred_element_type=f32)

    blk = pl.BlockSpec((tm, D_MODEL), lambda i: (i, 0))
    return _pc(
        body, name="out_fwd_bwd", grid=(t // tm,),
        in_specs=[blk, blk, blk, blk, _full((MIX_WIDTH, D_MODEL)), _full((1, D_MODEL))],
        out_specs=[blk, blk, blk, _full((MIX_WIDTH, D_MODEL)), _full((8, D_MODEL)), _full((1, 128))],
        out_shape=[jax.ShapeDtypeStruct((t, D_MODEL), f32)] * 3 +
                  [jax.ShapeDtypeStruct((MIX_WIDTH, D_MODEL), f32), jax.ShapeDtypeStruct((8, D_MODEL), f32),
                   jax.ShapeDtypeStruct((1, 128), f32)],
        compiler_params=_cparams(("arbitrary",)),
    )(x, tgt, y_ssd, y_gdn, w_out, fnw)


def inproj_bwd_dx(x, dout, norm_w, w_perm, dgroups, scattered):
    t = x.shape[0]
    tm = min(256, t)
    f32 = jnp.float32

    def body(x_ref, dout_ref, nw_ref, w_ref, dz_ref, dxbc_ref, dgate_ref, dqkv_ref, dsm_ref, dx_ref, gnw_ref):
        i = pl.program_id(0)

        @pl.when(i == 0)
        def _():
            gnw_ref[...] = jnp.zeros_like(gnw_ref)

        du = None
        for (name, c0, c1), d_ref in zip(GROUPS, (dz_ref, dxbc_ref, dgate_ref, dqkv_ref, dsm_ref)):
            term = lax.dot_general(d_ref[...].astype(_MM), w_ref[:, c0:c1], (((1,), (1,)), ((), ())),
                                   preferred_element_type=f32)
            du = term if du is None else du + term
        xf = x_ref[...]
        rstd = lax.rsqrt(jnp.mean(xf * xf, axis=-1, keepdims=True) + EPS)
        xhat = xf * rstd
        gnw_ref[0:1, :] += jnp.sum(du * xhat, axis=0, keepdims=True)
        dxh = du * nw_ref[...]
        dx_ref[...] = dout_ref[...] + rstd * (dxh - xhat * jnp.mean(dxh * xhat, axis=-1, keepdims=True))

    blk = lambda w: pl.BlockSpec((tm, w), lambda i: (i, 0))
    steps = t // tm
    kinds = ["scatter"] * len(scattered)
    hosted = _hosting(body, 9, 2, 0, kinds, lambda: pl.program_id(0) == 0, lambda: pl.program_id(0) == steps - 1)
    return _pc_comm(
        hosted, name="inproj_bwd_dx", grid=(steps,),
        in_specs=[blk(D_MODEL), blk(D_MODEL), _full((1, D_MODEL)), _full((D_MODEL, PERM_DIM))] +
                 [blk(c1 - c0) for _, c0, c1 in GROUPS] + [ANY] * len(scattered),
        out_specs=[blk(D_MODEL), _full((8, D_MODEL))] + [ANY] * len(scattered),
        out_shape=[jax.ShapeDtypeStruct((t, D_MODEL), f32), jax.ShapeDtypeStruct((8, D_MODEL), f32)] +
                  [_exchange_out_shape("scatter", a) for a in scattered],
        scratch_shapes=_exchange_sems(len(scattered)), compiler_params=_cparams(("arbitrary",)),
    )(x, dout, norm_w, w_perm, *dgroups, *scattered)


def grad_w_group(u, dg, name, scattered=()):
    t, n = dg.shape
    tn = 512 if n % 512 == 0 else n
    tm = 1024 if t % 1024 == 0 else t
    nj, nk = n // tn, t // tm
    f32 = jnp.float32

    def body(u_ref, d_ref, o_ref):
        @pl.when(pl.program_id(1) == 0)
        def _():
            o_ref[...] = jnp.zeros_like(o_ref)

        o_ref[...] += lax.dot_general(u_ref[...], d_ref[...].astype(_MM), (((0,), (0,)), ((), ())),
                                      preferred_element_type=f32)

    ne = len(scattered)
    hosted = _hosting(body, 2, 1, 0, ["scatter"] * ne,
                      lambda: (pl.program_id(0) == 0) & (pl.program_id(1) == 0),
                      lambda: (pl.program_id(0) == nj - 1) & (pl.program_id(1) == nk - 1))
    res = (_pc_comm if ne else _pc)(
        hosted, name=name, grid=(nj, nk),
        in_specs=[pl.BlockSpec((tm, D_MODEL), lambda j, k: (k, 0)),
                  pl.BlockSpec((tm, tn), lambda j, k: (k, j))] + [ANY] * ne,
        out_specs=[pl.BlockSpec((D_MODEL, tn), lambda j, k: (0, j))] + [ANY] * ne,
        out_shape=[jax.ShapeDtypeStruct((D_MODEL, n), f32)] + [_exchange_out_shape("scatter", a) for a in scattered],
        scratch_shapes=_exchange_sems(ne), compiler_params=_cparams(("arbitrary", "arbitrary")),
    )(u, dg, *scattered)
    return res if ne else res[0]


def _pad_lanes(v, off):
    n = v.shape[-1]
    return jnp.pad(v.reshape(1, n).astype(jnp.float32), ((0, 0), (off, 128 - off - n)))


def perm_w_in(w_full):
    z = w_full[:, 0:1024]
    xbc = w_full[:, 1024:2560]
    dt = w_full[:, 2560:2576]
    gate = w_full[:, 2576:3600]
    qkv = w_full[:, 3600:6672]
    ab = w_full[:, 6672:6688]
    pad = jnp.zeros((w_full.shape[0], PERM_DIM - IN_DIM), w_full.dtype)
    return jnp.concatenate([z, xbc, gate, qkv, dt, ab, pad], axis=1)


def unperm_w_in(gz, gxbc, ggate, gqkv, gsm):
    return jnp.concatenate([gz, gxbc, gsm[:, 0:16], ggate, gqkv, gsm[:, 16:32]], axis=1)


def all_gather(arrs, name):
    n = len(arrs)

    def body(*refs):
        ins, outs = refs[:n], refs[n:2 * n]
        send_sems, recv_sems, local_sems = refs[2 * n:]
        x, y, c, me = _me()
        sibling = (x, y, 1 - c)
        chips = [(1 - x, y), (x, 1 - y), (1 - x, 1 - y)]

        def idx(px, py, pc):
            return 4 * px + 2 * py + pc

        def copy(a, k, block, to, src=None):
            slot = outs[a].at[idx(*block)]
            return pltpu.make_async_remote_copy(src_ref=slot if src is None else src, dst_ref=slot,
                                                send_sem=send_sems.at[a, k], recv_sem=recv_sems.at[a, k],
                                                device_id=to, device_id_type=MESH)

        local = [pltpu.make_async_copy(ins[a], outs[a].at[me], local_sems.at[a]) for a in range(n)]
        for cp in local:
            cp.start()
        started = []
        for a in range(n):
            first = [copy(a, 0, (x, y, c), sibling, src=ins[a])]
            first += [copy(a, 1 + j, (x, y, c), (*chip, c), src=ins[a]) for j, chip in enumerate(chips)]
            for cp in first:
                cp.start()
            started += first
        for a in range(n):
            for j, chip in enumerate(chips):
                copy(a, 1 + j, (*chip, c), (x, y, c)).wait_recv()
                fwd = copy(a, 4 + j, (*chip, c), sibling)
                fwd.start()
                started.append(fwd)
        for a in range(n):
            copy(a, 0, sibling, (x, y, c)).wait_recv()
            for j, chip in enumerate(chips):
                copy(a, 4 + j, (*chip, 1 - c), (x, y, c)).wait_recv()
        for cp in started:
            cp.wait_send()
        for cp in local:
            cp.wait()

    return _pc_comm(
        body, name=name, in_specs=[ANY] * n, out_specs=[ANY] * n,
        out_shape=[jax.ShapeDtypeStruct((N_DEV,) + a.shape, a.dtype) for a in arrs],
        scratch_shapes=[pltpu.SemaphoreType.DMA((n, 7)), pltpu.SemaphoreType.DMA((n, 7)),
                        pltpu.SemaphoreType.DMA((n,))],
    )(*arrs)


def adamw_sum(recv, w, m, v, rows, name):
    r, ccols = w.shape
    f32 = jnp.float32
    c1 = 1.0 / (1.0 - ADAM_B1 ** ADAM_STEP)
    c2 = 1.0 / (1.0 - ADAM_B2 ** ADAM_STEP)

    def body(recv_ref, w_ref, m_ref, v_ref, g_ref, d_ref, mo_ref, vo_ref):
        g = recv_ref[0].astype(f32)
        for k in range(1, N_DEV):
            g = g + recv_ref[k].astype(f32)
        mn = ADAM_B1 * m_ref[...] + (1.0 - ADAM_B1) * g
        vn = ADAM_B2 * v_ref[...] + (1.0 - ADAM_B2) * (g * g)
        g_ref[...] = g
        mo_ref[...] = mn
        vo_ref[...] = vn
        d_ref[...] = -ADAM_LR * ((mn * c1) / (jnp.sqrt(vn * c2) + ADAM_EPS) + ADAM_WD * w_ref[...])

    blk = pl.BlockSpec((rows, ccols), lambda i: (i, 0))
    return _pc(
        body, name=name, grid=(r // rows,),
        in_specs=[pl.BlockSpec((N_DEV, rows, ccols), lambda i: (0, i, 0)), blk, blk, blk],
        out_specs=[blk] * 4, out_shape=[jax.ShapeDtypeStruct((r, ccols), f32)] * 4,
        compiler_params=_cparams(("arbitrary",)),
    )(recv, w, m, v)


REP = (("norm_w", 1024), ("ssd_conv_b", 1536), ("ssd_dt_bias", 16), ("ssd_a_log", 16), ("ssd_d", 16),
       ("ssd_norm_w", 1024), ("gdn_dt_bias", 8), ("gdn_a_log", 8), ("gdn_norm_w", 128), ("final_norm_w", 1024))
REP_ROWS = 48
SHARD = (("ssd_conv_w", CONV_K * SSD_CONV_DIM // N_DEV), ("gdn_conv_w", CONV_K * GDN_CONV_DIM // N_DEV))
SHARD_ROWS = 24


def _rows_of(size):
    return -(-size // 128)


def _pack(vals, layout, total_rows):
    parts = []
    for (name, size), val in zip(layout, vals):
        flat = val.reshape(-1).astype(jnp.float32)
        parts.append(jnp.pad(flat, (0, _rows_of(size) * 128 - size)).reshape(-1, 128))
    used = sum(_rows_of(s) for _, s in layout)
    parts.append(jnp.zeros((total_rows - used, 128), jnp.float32))
    return jnp.concatenate(parts, axis=0)


def _unpack(packed, layout, row0=0):
    out, r = {}, row0
    for name, size in layout:
        n = _rows_of(size)
        out[name] = packed[r:r + n].reshape(-1)[:size]
        r += n
    return out


def _conv_slabs(g_full):
    k, ccols = g_full.shape
    return g_full.reshape(k, N_DEV, ccols // N_DEV).transpose(1, 0, 2).reshape(N_DEV, -1)


def _conv_full(gathered_flat, ccols):
    return gathered_flat.reshape(N_DEV, CONV_K, ccols // N_DEV).transpose(1, 0, 2).reshape(CONV_K, ccols)


def kernel(x, norm_w, w_in, ssd_conv_w, ssd_conv_b, ssd_dt_bias, ssd_a_log, ssd_d, ssd_norm_w, gdn_conv_w, gdn_dt_bias, gdn_a_log, gdn_norm_w, w_out, final_norm_w, loss_target, m_norm_w, m_w_in, m_ssd_conv_w, m_ssd_conv_b, m_ssd_dt_bias, m_ssd_a_log, m_ssd_d, m_ssd_norm_w, m_gdn_conv_w, m_gdn_dt_bias, m_gdn_a_log, m_gdn_norm_w, m_w_out, m_final_norm_w, v_norm_w, v_w_in, v_ssd_conv_w, v_ssd_conv_b, v_ssd_dt_bias, v_ssd_a_log, v_ssd_d, v_ssd_norm_w, v_gdn_conv_w, v_gdn_dt_bias, v_gdn_a_log, v_gdn_norm_w, v_w_out, v_final_norm_w):
    f32 = jnp.float32
    w = dict(norm_w=norm_w, w_in=w_in, ssd_conv_w=ssd_conv_w, ssd_conv_b=ssd_conv_b, ssd_dt_bias=ssd_dt_bias,
             ssd_a_log=ssd_a_log, ssd_d=ssd_d, ssd_norm_w=ssd_norm_w, gdn_conv_w=gdn_conv_w, gdn_dt_bias=gdn_dt_bias,
             gdn_a_log=gdn_a_log, gdn_norm_w=gdn_norm_w, w_out=w_out, final_norm_w=final_norm_w)
    m = dict(norm_w=m_norm_w, w_in=m_w_in, ssd_conv_w=m_ssd_conv_w, ssd_conv_b=m_ssd_conv_b, ssd_dt_bias=m_ssd_dt_bias,
             ssd_a_log=m_ssd_a_log, ssd_d=m_ssd_d, ssd_norm_w=m_ssd_norm_w, gdn_conv_w=m_gdn_conv_w,
             gdn_dt_bias=m_gdn_dt_bias, gdn_a_log=m_gdn_a_log, gdn_norm_w=m_gdn_norm_w, w_out=m_w_out,
             final_norm_w=m_final_norm_w)
    v = dict(norm_w=v_norm_w, w_in=v_w_in, ssd_conv_w=v_ssd_conv_w, ssd_conv_b=v_ssd_conv_b, ssd_dt_bias=v_ssd_dt_bias,
             ssd_a_log=v_ssd_a_log, ssd_d=v_ssd_d, ssd_norm_w=v_ssd_norm_w, gdn_conv_w=v_gdn_conv_w,
             gdn_dt_bias=v_gdn_dt_bias, gdn_a_log=v_gdn_a_log, gdn_norm_w=v_gdn_norm_w, w_out=v_w_out,
             final_norm_w=v_final_norm_w)
    names = list(w)
    shapes = {n: w[n].shape for n in names}

    xl, tgt = x[0], loss_target[0]
    cs = _consts()
    dtb_s = _pad_lanes(ssd_dt_bias, 0)
    alog_s = _pad_lanes(ssd_a_log, 0)
    dpar = _pad_lanes(ssd_d, 0)
    dtb_g = _pad_lanes(gdn_dt_bias, 16)
    alog_g = _pad_lanes(gdn_a_log, 16)
    nw_g = gdn_norm_w.reshape(1, 128)
    nw_s = ssd_norm_w.reshape(1, 1024)
    cb_s = ssd_conv_b.reshape(1, 1536)
    nw1 = norm_w.reshape(1, D_MODEL)

    (g_w_in,) = all_gather([w_in[0].astype(_MM)], "gather_w_in")
    w_perm = perm_w_in(g_w_in.transpose(1, 0, 2).reshape(D_MODEL, IN_DIM))
    conv_pack = _pack([w["ssd_conv_w"], w["gdn_conv_w"]], SHARD, SHARD_ROWS)
    u, z, xbc, gate, qkv, sm, g_w_out, g_conv = inproj_fwd(xl, nw1, w_perm, [w_out[0].astype(_MM), conv_pack])
    w_out_full = g_w_out.reshape(MIX_WIDTH, D_MODEL)
    ssd_cw = _conv_full(g_conv[:, 0:6].reshape(N_DEV, -1), SSD_CONV_DIM)
    gdn_cw = _conv_full(g_conv[:, 6:18].reshape(N_DEV, -1), GDN_CONV_DIM)

    nc = xl.shape[0] // CHUNK
    y_ssd, hs, pre_s, y_gdn, ss, ts, pre_g = _chunk_call(
        [ssd_fwd(z, xbc, sm, ssd_cw, cb_s, dtb_s, alog_s, dpar, nw_s, cs),
         gdn_fwd(gate, qkv, sm, gdn_cw, dtb_g, alog_g, nw_g, cs)], "scan_fwd", nc)
    dout, dys, dyg, g_wout, g_fnw, loss_l = out_fwd_bwd(xl, tgt, y_ssd, y_gdn, w_out_full,
                                                        final_norm_w.reshape(1, D_MODEL))
    (dz, dxbc, g_cw_s, g_cb_s, g_dtb_s, g_alog_s, g_d, g_nw_s,
     dgate, dqkv, dsm, g_cw_g, g_dtb_g, g_alog_g, g_nw_g) = _chunk_call(
        [ssd_bwd(z, xbc, pre_s, sm, hs, dys, ssd_cw, dtb_s, alog_s, dpar, nw_s, cs),
         gdn_bwd(gate, qkv, pre_g, sm, ss, ts, dyg, gdn_cw, dtb_g, alog_g, nw_g, cs)], "scan_bwd", nc)

    t_w_out = g_wout.reshape(N_DEV, MIX_WIDTH // N_DEV, D_MODEL).astype(_MM)
    gws = {}
    for dg, (name, _, _) in zip((dz, dxbc, dgate, dsm), (GROUPS[0], GROUPS[1], GROUPS[2], GROUPS[4])):
        gws[name] = grad_w_group(u, dg, "grad_w_in_" + name)
    gws["qkv"], r_w_out = grad_w_group(u, dqkv, "grad_w_in_qkv", [t_w_out])
    g_w_in_full = unperm_w_in(gws["z"], gws["xbc"], gws["gate"], gws["qkv"], gws["sm"])
    t_w_in = g_w_in_full.reshape(D_MODEL, N_DEV, W_IN_SHARD).transpose(1, 0, 2).astype(_MM)
    dx, g_nw, r_w_in = inproj_bwd_dx(xl, dout, nw1, w_perm, (dz, dxbc, dgate, dqkv, dsm), [t_w_in])

    g = dict(norm_w=g_nw[0:1, :], ssd_conv_b=g_cb_s[0:1, :], ssd_dt_bias=g_dtb_s[0:1, 0:16],
             ssd_a_log=g_alog_s[0:1, 0:16], ssd_d=g_d[0:1, 0:16], ssd_norm_w=g_nw_s[0:1, :],
             gdn_dt_bias=g_dtb_g[0:1, 16:24], gdn_a_log=g_alog_g[0:1, 16:24], gdn_norm_w=g_nw_g[0:1, :],
             final_norm_w=g_fnw[0:1, :])
    rep = _pack([g[n] for n, _ in REP], REP, REP_ROWS)
    shard_rows = jnp.concatenate([_conv_slabs(g_cw_s[0:4, :]).reshape(N_DEV, 6, 128),
                                  _conv_slabs(g_cw_g[0:4, :]).reshape(N_DEV, 12, 128),
                                  jnp.zeros((N_DEV, SHARD_ROWS - 18, 128), f32)], axis=1)
    t_small = jnp.concatenate([jnp.broadcast_to(rep[None], (N_DEV, REP_ROWS, 128)), shard_rows], axis=1)
    (r_small,) = exchange([t_small], ["scatter"], "scatter_small_grads")

    o_w_in = adamw_sum(r_w_in, w_in[0], m_w_in[0], v_w_in[0], 128, "adamw_w_in")
    o_w_out = adamw_sum(r_w_out, w_out[0], m_w_out[0], v_w_out[0], 64, "adamw_w_out")
    small = [jnp.concatenate([_pack([d[n] for n, _ in REP], REP, REP_ROWS),
                              _pack([d[n] for n, _ in SHARD], SHARD, SHARD_ROWS)], axis=0) for d in (w, m, v)]
    o_small = adamw_sum(r_small, small[0], small[1], small[2], REP_ROWS + SHARD_ROWS, "adamw_small")

    loss = lax.psum(loss_l[0, 0], ("x", "y", "c"))
    outs = [loss, dx[None]]
    for k in range(4):
        parts = {**_unpack(o_small[k], REP), **_unpack(o_small[k], SHARD, REP_ROWS),
                 "w_in": o_w_in[k], "w_out": o_w_out[k]}
        outs += [parts[n].reshape(shapes[n]) for n in names]
    return tuple(outs)
```

```python
import functools

import jax
import jax.numpy as jnp
import numpy as np
from jax import lax
from jax.experimental import pallas as pl
from jax.experimental.pallas import tpu as pltpu

_MM = jnp.bfloat16

D_MODEL = 1024
CHUNK = 64
CONV_K = 4
EPS = 1e-6
SSD_CONV_DIM = 1536
GDN_HEADS = 8
GDN_DK = 128
GDN_CONV_DIM = 3072
MIX_WIDTH = 2048
IN_DIM = 6688
N_DEV = 8
W_IN_SHARD = IN_DIM // N_DEV
PERM_DIM = 6784
HI = lax.Precision.HIGHEST
HIGH = lax.Precision.HIGH
VMEM_LIMIT = 56 * 1024 * 1024

ADAM_LR = 0.001
ADAM_B1 = 0.9
ADAM_B2 = 0.999
ADAM_EPS = 1e-08
ADAM_WD = 0.01
ADAM_STEP = 10


def _pc(body, **kw):
    return pl.pallas_call(body, **kw)


def _pc_comm(body, **kw):
    return pl.pallas_call(body, **kw)


def _cparams(sem):
    return pltpu.CompilerParams(dimension_semantics=sem, vmem_limit_bytes=VMEM_LIMIT)


def _sig(x):
    return 0.5 * jnp.tanh(0.5 * x) + 0.5


@jax.custom_vjp
def _sigmoid(x):
    return _sig(x)


def _sigmoid_fwd(x):
    s = _sig(x)
    return s, s


def _sigmoid_bwd(s, g):
    return (g * s * (1.0 - s),)


_sigmoid.defvjp(_sigmoid_fwd, _sigmoid_bwd)


@jax.custom_vjp
def _silu(x):
    return x * _sig(x)


def _silu_fwd(x):
    s = _sig(x)
    return x * s, (x, s)


def _silu_bwd(res, g):
    x, s = res
    return (g * (s * (1.0 + x * (1.0 - s))),)


_silu.defvjp(_silu_fwd, _silu_bwd)


def _softplus_impl(x):
    return jnp.maximum(x, 0.0) + jnp.log(1.0 + jnp.exp(-jnp.abs(x)))


@jax.custom_vjp
def _softplus(x):
    return _softplus_impl(x)


def _softplus_fwd(x):
    return _softplus_impl(x), x


def _softplus_bwd(x, g):
    return (g * _sig(x),)


_softplus.defvjp(_softplus_fwd, _softplus_bwd)


def _lane_bcast_impl(x, k):
    return jnp.broadcast_to(x[..., k:k + 1], x.shape)


@functools.partial(jax.custom_vjp, nondiff_argnums=(1,))
def _lane_bcast(x, k):
    return _lane_bcast_impl(x, k)


def _lane_bcast_fwd(x, k):
    return _lane_bcast_impl(x, k), None


def _lane_bcast_bwd(k, _, g):
    lane = lax.broadcasted_iota(jnp.int32, g.shape, g.ndim - 1)
    return (jnp.where(lane == k, jnp.sum(g, axis=-1, keepdims=True), 0.0),)


_lane_bcast.defvjp(_lane_bcast_fwd, _lane_bcast_bwd)


def _mm(a, b):
    return jnp.dot(a.astype(_MM), b.astype(_MM), preferred_element_type=jnp.float32)


def _mm_nt(a, b):
    return lax.dot_general(a.astype(_MM), b.astype(_MM), (((1,), (1,)), ((), ())),
                           preferred_element_type=jnp.float32)


def _mm_tn(a, b):
    return lax.dot_general(a.astype(_MM), b.astype(_MM), (((0,), (0,)), ((), ())),
                           preferred_element_type=jnp.float32)


def _dot_hi(a, b):
    return jnp.dot(a, b, precision=HI, preferred_element_type=jnp.float32)


def _bmm(a, b):
    return lax.dot_general(a.astype(_MM), b.astype(_MM), (((2,), (1,)), ((0,), (0,))),
                           preferred_element_type=jnp.float32)


def _bmm_nt(a, b):
    return lax.dot_general(a.astype(_MM), b.astype(_MM), (((2,), (2,)), ((0,), (0,))),
                           preferred_element_type=jnp.float32)


def _bmm_tn(a, b):
    return lax.dot_general(a.astype(_MM), b.astype(_MM), (((1,), (1,)), ((0,), (0,))),
                           preferred_element_type=jnp.float32)


def _bmm_hi(a, b):
    return lax.dot_general(a, b, (((2,), (1,)), ((0,), (0,))), precision=HIGH, preferred_element_type=jnp.float32)


def _bmm_nt_hi(a, b):
    return lax.dot_general(a, b, (((2,), (2,)), ((0,), (0,))), precision=HIGH, preferred_element_type=jnp.float32)


def _bmm_tn_hi(a, b):
    return lax.dot_general(a, b, (((1,), (1,)), ((0,), (0,))), precision=HIGH, preferred_element_type=jnp.float32)


def _consts():
    l = np.arange(CHUNK)
    tri = (l[:, None] >= l[None, :]).astype(np.float32)
    lane = np.arange(128)
    i2 =(l[:, None] == (lane[None, :] % 64)).astype(np.float32)
    mask2 = (l[:, None] >= (lane[None, :] % 64)).astype(np.float32)
    lo = (lane < 64).astype(np.float32)[None, :]
    i64 = np.eye(CHUNK, dtype=np.float32)
    strict = (l[:, None] > l[None, :]).astype(np.float32)
    return dict(tri=jnp.asarray(tri), i2=jnp.asarray(i2), mask2=jnp.asarray(mask2), lo=jnp.asarray(lo),
                i64=jnp.asarray(i64), strict=jnp.asarray(strict))


def _ssd_chunk(xs_pre, b_pre, c_pre, z, sm, ht, dtb, alog, dpar, nw, tri, i2, mask2, lo):
    lane = lax.broadcasted_iota(jnp.int32, (1, 128), 1)
    m16 = lane < 16
    dt = jnp.where(m16, _softplus(sm + dtb), 0.0)
    a_neg = -jnp.exp(alog)
    cum = _dot_hi(tri, dt * a_neg)
    row = lax.broadcasted_iota(jnp.int32, (CHUNK, 1), 0)
    is_last = row == CHUNK - 1
    hi = 1.0 - lo
    bm = [_silu(b) for b in b_pre]
    cm = [_silu(c) for c in c_pre]
    cb2 = [_mm_nt(cm[g], jnp.concatenate([bm[g], bm[g]], axis=0)) for g in range(2)]
    yg, ht_next = [], []
    for j in range(8):
        g = j // 4
        pair = lambda v, j=j: jnp.where(lo > 0.5, _lane_bcast(v, 2 * j), _lane_bcast(v, 2 * j + 1))
        xs = _silu(xs_pre[j])
        dte = pair(dt)
        cume = pair(cum)
        cum_last = jnp.sum(jnp.where(is_last, cume, 0.0), axis=0, keepdims=True)
        xdt = xs * dte
        rowv = jnp.sum(cume * i2, axis=0, keepdims=True)
        lm = jnp.exp(jnp.where(mask2 > 0.5, cume - rowv, -jnp.inf))
        m = cb2[g] * lm
        xblk = jnp.concatenate([xdt * lo, xdt * hi], axis=0)
        y = _mm(m, xblk)
        y = y + _mm(cm[g], ht[j]) * jnp.exp(cume)
        y = y + pair(dpar) * xs
        yg.append(y * _silu(z[j]))
        st = _mm_tn(bm[g], xdt * jnp.exp(cum_last - cume))
        ht_next.append(ht[j] * jnp.exp(cum_last) + st)
    outs = []
    for g in range(2):
        ss = sum(jnp.sum(yg[j] * yg[j], axis=-1, keepdims=True) for j in range(4 * g, 4 * g + 4))
        rs = lax.rsqrt(ss * (1.0 / 512.0) + EPS)
        for j in range(4 * g, 4 * g + 4):
            outs.append(yg[j] * rs * nw[j])
    return outs, ht_next


def _tri_inverse(a):
    eye = jnp.eye(CHUNK, dtype=jnp.float32)[None]
    p = eye - a
    ap = a
    for _ in range(5):
        ap = _bmm_hi(ap, ap)
        p = p + _bmm_hi(p, ap)
    return p


@jax.custom_vjp
def _solve(a, r1, r2, t):
    return _bmm_hi(t, r1), _bmm_hi(t, r2)


def _solve_fwd(a, r1, r2, t):
    u, w = _bmm_hi(t, r1), _bmm_hi(t, r2)
    return (u, w), (t, u, w)


def _solve_bwd(res, cts):
    t, u, w = res
    du, dw = cts
    dr1 = _bmm_tn_hi(t, du)
    dr2 = _bmm_tn_hi(t, dw)
    da = -(_bmm_nt_hi(dr1, u) + _bmm_nt_hi(dr2, w))
    return da, dr1, dr2, jnp.zeros_like(t)


_solve.defvjp(_solve_fwd, _solve_bwd)


def _gdn_chunk(q_pre, k_pre, v_pre, gate, sm, s, dtb, alog, nw, tri, i64, strict, t_in=None):
    lane = lax.broadcasted_iota(jnp.int32, (1, 128), 1)
    m_a = (lane >= 16) & (lane < 24)
    g_full = jnp.where(m_a, -jnp.exp(alog) * _softplus(sm + dtb), 0.0)
    gc = _dot_hi(tri, g_full)
    sig = _sigmoid(sm)
    gc3 = jnp.stack([_lane_bcast(gc, 16 + h) for h in range(GDN_HEADS)])
    beta3 = jnp.stack([_lane_bcast(sig, 24 + h) for h in range(GDN_HEADS)])
    q = _silu(q_pre)
    q = q * lax.rsqrt(jnp.sum(q * q, axis=-1, keepdims=True) + EPS) * (GDN_DK ** -0.5)
    k = _silu(k_pre)
    k = k * lax.rsqrt(jnp.sum(k * k, axis=-1, keepdims=True) + EPS)
    v = _silu(v_pre)
    gcl = gc3[:, :, :CHUNK]
    gc_row = jnp.sum(gcl * i64[None], axis=1, keepdims=True)
    incl = (strict + i64)[None] > 0.5
    decay = jnp.exp(jnp.where(incl, gcl - gc_row, -jnp.inf))
    kb = k * beta3
    a = jnp.where(strict[None] > 0.5, _bmm_nt(kb, k) * decay, 0.0)
    egc = jnp.exp(gc3)
    t = _tri_inverse(a) if t_in is None else t_in
    u, w = _solve(a, v * beta3, kb * egc, t)
    attn = _bmm_nt(q, k) * decay
    row = lax.broadcasted_iota(jnp.int32, (1, CHUNK, 1), 1)
    gl = jnp.sum(jnp.where(row == CHUNK - 1, gc3, 0.0), axis=1, keepdims=True)
    q_dec = q * egc
    k_dec = k * jnp.exp(gl - gc3)
    v_new = u - _bmm(w, s)
    o = _bmm(q_dec, s) + _bmm(attn, v_new)
    s_next = s * jnp.exp(gl) + _bmm_tn(k_dec, v_new)
    on = o * lax.rsqrt(jnp.mean(o * o, axis=-1, keepdims=True) + EPS) * nw
    return on * _silu(gate), s_next, t


def _conv_fwd(pbuf, w_ref, c0, c1):
    acc = None
    for j in range(CONV_K):
        term = w_ref[j:j + 1, c0:c1] * pbuf[5 + j:69 + j, c0:c1]
        acc = term if acc is None else acc + term
    return acc


MESH = pl.DeviceIdType.MESH
ANY = pl.BlockSpec(memory_space=pl.ANY)


def _me():
    x, y, c = lax.axis_index("x"), lax.axis_index("y"), lax.axis_index("c")
    return x, y, c, 4 * x + 2 * y + c


def _peer(r):
    x, y, c, _ = _me()
    px = 1 - x if r & 4 else x
    py = 1 - y if r & 2 else y
    pc = 1 - c if r & 1 else c
    return (px, py, pc), 4 * px + 2 * py + pc


def _exchange_ops(kind, in_ref, out_ref, send_sems, recv_sems, local_sem):
    me = _me()[3]
    local = pltpu.make_async_copy(in_ref.at[me] if kind == "scatter" else in_ref, out_ref.at[me], local_sem)
    sends, recvs = [], []
    for r in range(1, N_DEV):
        peer, pidx = _peer(r)
        src = in_ref.at[pidx] if kind == "scatter" else in_ref
        sems = dict(send_sem=send_sems.at[r - 1], recv_sem=recv_sems.at[r - 1], device_id=peer, device_id_type=MESH)
        sends.append(pltpu.make_async_remote_copy(src_ref=src, dst_ref=out_ref.at[me], **sems))
        recvs.append(pltpu.make_async_remote_copy(src_ref=src, dst_ref=out_ref.at[pidx], **sems))

    def start():
        local.start()
        for cp in sends:
            cp.start()

    def wait():
        for cp in recvs:
            cp.wait_recv()
        for cp in sends:
            cp.wait_send()
        local.wait()

    return start, wait


def _exchange_sems(n):
    return [pltpu.SemaphoreType.DMA((N_DEV - 1,)), pltpu.SemaphoreType.DMA((N_DEV - 1,)),
            pltpu.SemaphoreType.DMA(())] * n


def _exchange_out_shape(kind, a):
    return jax.ShapeDtypeStruct(a.shape if kind == "scatter" else (N_DEV,) + a.shape, a.dtype)


def _hosting(body, n_in, n_out, n_scratch, kinds, first, last):
    ne = len(kinds)

    def wrapped(*refs):
        ins, ex_in = refs[:n_in], refs[n_in:n_in + ne]
        o0 = n_in + ne
        outs, ex_out = refs[o0:o0 + n_out], refs[o0 + n_out:o0 + n_out + ne]
        s0 = o0 + n_out + ne
        scr, sems = refs[s0:s0 + n_scratch], refs[s0 + n_scratch:]
        ops = [_exchange_ops(kinds[e], ex_in[e], ex_out[e], *sems[3 * e:3 * e + 3]) for e in range(ne)]

        @pl.when(first())
        def _():
            for start, _ in ops:
                start()

        body(*ins, *outs, *scr)

        @pl.when(last())
        def _():
            for _, wait in ops:
                wait()

    return wrapped


def exchange(arrs, kinds, name):
    n = len(arrs)

    def body(*refs):
        ins, outs, sems = refs[:n], refs[n:2 * n], refs[2 * n:]
        ops = [_exchange_ops(kinds[e], ins[e], outs[e], *sems[3 * e:3 * e + 3]) for e in range(n)]
        for start, _ in ops:
            start()
        for _, wait in ops:
            wait()

    return _pc_comm(
        body, name=name, in_specs=[ANY] * n, out_specs=[ANY] * n,
        out_shape=[_exchange_out_shape(k, a) for k, a in zip(kinds, arrs)], scratch_shapes=_exchange_sems(n),
    )(*arrs)


GROUPS = (("z", 0, 1024), ("xbc", 1024, 2560), ("gate", 2560, 3584), ("qkv", 3584, 6656), ("sm", 6656, 6784))


def inproj_fwd(x, norm_w, w_perm, gathered):
    t = x.shape[0]
    tm = min(256, t)
    steps = t // tm
    kinds = ["gather"] * len(gathered)

    def body(x_ref, nw_ref, w_ref, u_ref, z_ref, xbc_ref, gate_ref, qkv_ref, sm_ref):
        xf = x_ref[...]
        rstd = lax.rsqrt(jnp.mean(xf * xf, axis=-1, keepdims=True) + EPS)
        u = (xf * rstd * nw_ref[...]).astype(_MM)
        u_ref[...] = u
        for (name, c0, c1), o_ref in zip(GROUPS, (z_ref, xbc_ref, gate_ref, qkv_ref, sm_ref)):
            o_ref[...] = lax.dot_general(u, w_ref[c0:c1, :], (((1,), (1,)), ((), ())),
                                         preferred_element_type=jnp.float32)

    outs = [jax.ShapeDtypeStruct((t, D_MODEL), _MM)] + [jax.ShapeDtypeStruct((t, c1 - c0), jnp.float32)
                                                        for _, c0, c1 in GROUPS]
    hosted = _hosting(body, 3, 6, 0, kinds, lambda: pl.program_id(0) == 0, lambda: pl.program_id(0) == steps - 1)
    return _pc_comm(
        hosted, name="inproj_fwd", grid=(steps,),
        in_specs=[pl.BlockSpec((tm, D_MODEL), lambda i: (i, 0)),
                  pl.BlockSpec((1, D_MODEL), lambda i: (0, 0)),
                  pl.BlockSpec((PERM_DIM, D_MODEL), lambda i: (0, 0))] + [ANY] * len(gathered),
        out_specs=[pl.BlockSpec((tm, D_MODEL), lambda i: (i, 0))] +
                  [pl.BlockSpec((tm, c1 - c0), lambda i: (i, 0)) for _, c0, c1 in GROUPS] + [ANY] * len(gathered),
        out_shape=outs + [_exchange_out_shape("gather", a) for a in gathered],
        scratch_shapes=_exchange_sems(len(gathered)), compiler_params=_cparams(("arbitrary",)),
    )(x, norm_w, w_perm, *gathered)


def _halo_spec(width, idx_fn):
    return pl.BlockSpec((8, width), lambda i: (jnp.maximum(idx_fn(i) * 8 - 1, 0), 0))


def _full(shape):
    nd = len(shape)
    return pl.BlockSpec(shape, lambda i: (0,) * nd)


def _ssd_split(pre_fn, z_ref, sm_ref):
    xs_pre = [pre_fn(128 * j, 128 * j + 128) for j in range(8)]
    b_pre = [pre_fn(1024 + 128 * g, 1152 + 128 * g) for g in range(2)]
    c_pre = [pre_fn(1280 + 128 * g, 1408 + 128 * g) for g in range(2)]
    z = [z_ref[:, 128 * j:128 * j + 128] for j in range(8)]
    return xs_pre, b_pre, c_pre, z, sm_ref[...]


def ssd_fwd(z, xbc, sm, conv_w, conv_b, dtb, alog, dpar, nw, cs):
    t = z.shape[0]
    nc = t // CHUNK

    def body(shared, z_ref, xbc_ref, halo_ref, sm_ref, cw_ref, cb_ref, dtb_ref, alog_ref, dpar_ref, nw_ref,
             tri_ref, i2_ref, mask2_ref, lo_ref, y_ref, hs_ref, pre_ref, pbuf, ht_scr):
        i = pl.program_id(0)

        @pl.when(i == 0)
        def _():
            ht_scr[...] = jnp.zeros_like(ht_scr)

        pbuf[0:8, :] = jnp.where(i == 0, 0.0, halo_ref[...])
        pbuf[8:72, :] = xbc_ref[...]

        def pre_fn(c0, c1):
            pre = _conv_fwd(pbuf, cw_ref, c0, c1) + cb_ref[:, c0:c1]
            pre_ref[:, c0:c1] = pre
            return pre

        xs_pre, b_pre, c_pre, zz, smv = _ssd_split(pre_fn, z_ref, sm_ref)
        ht = [ht_scr[:, 128 * j:128 * j + 128] for j in range(8)]
        hs_ref[0] = ht_scr[...]
        nwl = [nw_ref[:, 128 * j:128 * j + 128] for j in range(8)]
        outs, ht_next = _ssd_chunk(xs_pre, b_pre, c_pre, zz, smv, ht, dtb_ref[...], alog_ref[...], dpar_ref[...],
                                   nwl, tri_ref[...], i2_ref[...], mask2_ref[...], lo_ref[...])
        for j in range(8):
            y_ref[:, 128 * j:128 * j + 128] = outs[j].astype(y_ref.dtype)
            ht_scr[:, 128 * j:128 * j + 128] = ht_next[j]

    blk = lambda w: pl.BlockSpec((CHUNK, w), lambda i: (i, 0))
    return dict(
        body=body,
        in_specs=[blk(1024), blk(1536), _halo_spec(1536, lambda i: i), blk(128),
                  _full((CONV_K, 1536)), _full((1, 1536)), _full((1, 128)), _full((1, 128)), _full((1, 128)),
                  _full((1, 1024)), _full((64, 64)), _full((64, 128)), _full((64, 128)),
                  _full((1, 128))],
        out_specs=[blk(1024), pl.BlockSpec((1, 128, 1024), lambda i: (i, 0, 0)), blk(1536)],
        out_shape=[jax.ShapeDtypeStruct((t, 1024), _MM), jax.ShapeDtypeStruct((nc, 128, 1024), jnp.float32),
                   jax.ShapeDtypeStruct((t, 1536), jnp.float32)],
        scratch=[pltpu.VMEM((72, 1536), jnp.float32), pltpu.VMEM((128, 1024), jnp.float32)],
        args=[z, xbc, xbc, sm, conv_w, conv_b, dtb, alog, dpar, nw, cs["tri"], cs["i2"], cs["mask2"], cs["lo"]])


def _conv_bwd(dpre_list, col_ranges, dbuf, carry, x_ref, cw_ref, dx_ref, dcw_ref, dcb_ref, first):
    for dpre, (c0, c1) in zip(dpre_list, col_ranges):
        dbuf[0:64, c0:c1] = dpre
    dbuf[64:72, :] = jnp.where(first, 0.0, carry[...])
    carry[...] = dbuf[0:8, :]
    for (c0, c1) in col_ranges:
        xin = x_ref[:, c0:c1]
        acc = None
        for j in range(CONV_K):
            sh = dbuf[3 - j:67 - j, c0:c1]
            term = cw_ref[j:j + 1, c0:c1] * sh
            acc = term if acc is None else acc + term
            dcw_ref[j:j + 1, c0:c1] += jnp.sum(xin * sh, axis=0, keepdims=True)
        dx_ref[:, c0:c1] = acc
        if dcb_ref is not None:
            dcb_ref[0:1, c0:c1] += jnp.sum(dbuf[0:64, c0:c1], axis=0, keepdims=True)


def ssd_bwd(z, xbc, pre, sm, hs, dy, conv_w, dtb, alog, dpar, nw, cs):
    t = z.shape[0]
    nc = t // CHUNK

    def body(shared, z_ref, xbc_ref, pre_ref, sm_ref, hs_ref, dy_ref, cw_ref, dtb_ref, alog_ref, dpar_ref, nw_ref,
             tri_ref, i2_ref, mask2_ref, lo_ref,
             dz_ref, dxbc_ref, dcw_ref, dcb_ref, ddtb_ref, dalog_ref, ddpar_ref, dnw_ref,
             dbuf, carry, dht_scr):
        i = pl.program_id(0)

        @pl.when(i == 0)
        def _():
            dht_scr[...] = jnp.zeros_like(dht_scr)
            dcw_ref[...] = jnp.zeros_like(dcw_ref)
            dcb_ref[...] = jnp.zeros_like(dcb_ref)
            ddtb_ref[...] = jnp.zeros_like(ddtb_ref)
            dalog_ref[...] = jnp.zeros_like(dalog_ref)
            ddpar_ref[...] = jnp.zeros_like(ddpar_ref)
            dnw_ref[...] = jnp.zeros_like(dnw_ref)

        pre_fn = lambda c0, c1: pre_ref[:, c0:c1]
        xs_pre, b_pre, c_pre, zz, smv = _ssd_split(pre_fn, z_ref, sm_ref)
        ht = [hs_ref[0, :, 128 * j:128 * j + 128] for j in range(8)]
        nwl = [nw_ref[:, 128 * j:128 * j + 128] for j in range(8)]
        consts = (tri_ref[...], i2_ref[...], mask2_ref[...], lo_ref[...])

        def f(xs_pre, b_pre, c_pre, zz, smv, ht, dtb, alog, dpar, nwl):
            return _ssd_chunk(xs_pre, b_pre, c_pre, zz, smv, ht, dtb, alog, dpar, nwl, *consts)

        _, vjp = jax.vjp(f, xs_pre, b_pre, c_pre, zz, smv, ht, dtb_ref[...], alog_ref[...], dpar_ref[...], nwl)
        dys = [dy_ref[:, 128 * j:128 * j + 128] for j in range(8)]
        dhts = [dht_scr[:, 128 * j:128 * j + 128] for j in range(8)]
        dxs, db, dc, dzz, dsm, dht, ddtb, dalog, ddpar, dnwl = vjp((dys, dhts))
        for j in range(8):
            dz_ref[:, 128 * j:128 * j + 128] = dzz[j]
            dht_scr[:, 128 * j:128 * j + 128] = dht[j]
            dnw_ref[0:1, 128 * j:128 * j + 128] += dnwl[j]
        shared["dsm_ssd"] = dsm
        ddtb_ref[0:1, :] += ddtb
        dalog_ref[0:1, :] += dalog
        ddpar_ref[0:1, :] += ddpar
        ranges = ([(128 * j, 128 * j + 128) for j in range(8)] + [(1024 + 128 * g, 1152 + 128 * g) for g in range(2)]
                  + [(1280 + 128 * g, 1408 + 128 * g) for g in range(2)])
        _conv_bwd(dxs + db + dc, ranges, dbuf, carry, xbc_ref, cw_ref, dxbc_ref, dcw_ref, dcb_ref, i == 0)

    rblk = lambda w: pl.BlockSpec((CHUNK, w), lambda i: (nc - 1 - i, 0))
    acc = lambda w: pl.BlockSpec((8, w), lambda i: (0, 0))
    f32 = jnp.float32
    return dict(
        body=body,
        in_specs=[rblk(1024), rblk(1536), rblk(1536), rblk(128),
                  pl.BlockSpec((1, 128, 1024), lambda i: (nc - 1 - i, 0, 0)), rblk(1024),
                  _full((CONV_K, 1536)), _full((1, 128)), _full((1, 128)), _full((1, 128)),
                  _full((1, 1024)), _full((64, 64)), _full((64, 128)), _full((64, 128)),
                  _full((1, 128))],
        out_specs=[rblk(1024), rblk(1536), acc(1536), acc(1536), acc(128), acc(128), acc(128), acc(1024)],
        out_shape=[jax.ShapeDtypeStruct((t, 1024), f32), jax.ShapeDtypeStruct((t, 1536), f32),
                   jax.ShapeDtypeStruct((8, 1536), f32),
                   jax.ShapeDtypeStruct((8, 1536), f32), jax.ShapeDtypeStruct((8, 128), f32),
                   jax.ShapeDtypeStruct((8, 128), f32), jax.ShapeDtypeStruct((8, 128), f32),
                   jax.ShapeDtypeStruct((8, 1024), f32)],
        scratch=[pltpu.VMEM((72, 1536), f32), pltpu.VMEM((8, 1536), f32), pltpu.VMEM((128, 1024), f32)],
        args=[z, xbc, pre, sm, hs, dy, conv_w, dtb, alog, dpar, nw, cs["tri"], cs["i2"], cs["mask2"], cs["lo"]])


def _gdn_split(pre_fn, gate_ref):
    def heads(base):
        return jnp.stack([pre_fn(base + 128 * h, base + 128 * h + 128) for h in range(GDN_HEADS)])
    gate = jnp.stack([gate_ref[:, 128 * h:128 * h + 128] for h in range(GDN_HEADS)])
    return heads(0), heads(1024), heads(2048), gate


def gdn_fwd(gate, qkv, sm, conv_w, dtb, alog, nw, cs):
    t = gate.shape[0]
    nc = t // CHUNK

    def body(shared, gate_ref, qkv_ref, halo_ref, sm_ref, cw_ref, dtb_ref, alog_ref, nw_ref,
             tri_ref, i64_ref, strict_ref, o_ref, ss_ref, ts_ref, pre_ref, pbuf, s_scr):
        i = pl.program_id(0)

        @pl.when(i == 0)
        def _():
            s_scr[...] = jnp.zeros_like(s_scr)

        pbuf[0:8, :] = jnp.where(i == 0, 0.0, halo_ref[...])
        pbuf[8:72, :] = qkv_ref[...]

        def pre_fn(c0, c1):
            pre = _conv_fwd(pbuf, cw_ref, c0, c1)
            pre_ref[:, c0:c1] = pre
            return pre

        q_pre, k_pre, v_pre, g3 = _gdn_split(pre_fn, gate_ref)
        s = s_scr[...]
        ss_ref[0] = s
        out, s_next, tinv = _gdn_chunk(q_pre, k_pre, v_pre, g3, sm_ref[...], s, dtb_ref[...], alog_ref[...],
                                       nw_ref[...], tri_ref[...], i64_ref[...], strict_ref[...])
        ts_ref[0] = tinv
        s_scr[...] = s_next
        for h in range(GDN_HEADS):
            o_ref[:, 128 * h:128 * h + 128] = out[h].astype(o_ref.dtype)

    blk = lambda w: pl.BlockSpec((CHUNK, w), lambda i: (i, 0))
    return dict(
        body=body,
        in_specs=[blk(1024), blk(3072), _halo_spec(3072, lambda i: i), blk(128),
                  _full((CONV_K, 3072)), _full((1, 128)), _full((1, 128)), _full((1, 128)),
                  _full((64, 64)), _full((64, 64)), _full((64, 64))],
        out_specs=[blk(1024), pl.BlockSpec((1, 8, 128, 128), lambda i: (i, 0, 0, 0)),
                   pl.BlockSpec((1, 8, CHUNK, CHUNK), lambda i: (i, 0, 0, 0)), blk(3072)],
        out_shape=[jax.ShapeDtypeStruct((t, 1024), _MM), jax.ShapeDtypeStruct((nc, 8, 128, 128), jnp.float32),
                   jax.ShapeDtypeStruct((nc, 8, CHUNK, CHUNK), jnp.float32),
                   jax.ShapeDtypeStruct((t, 3072), jnp.float32)],
        scratch=[pltpu.VMEM((72, 3072), jnp.float32), pltpu.VMEM((8, 128, 128), jnp.float32)],
        args=[gate, qkv, qkv, sm, conv_w, dtb, alog, nw, cs["tri"], cs["i64"], cs["strict"]])


def gdn_bwd(gate, qkv, pre, sm, ss, ts, do, conv_w, dtb, alog, nw, cs):
    t = gate.shape[0]
    nc = t // CHUNK

    def body(shared, gate_ref, qkv_ref, pre_ref, sm_ref, ss_ref, ts_ref, do_ref, cw_ref, dtb_ref, alog_ref,
             nw_ref, tri_ref, i64_ref, strict_ref,
             dgate_ref, dqkv_ref, dsm_ref, dcw_ref, ddtb_ref, dalog_ref, dnw_ref,
             dbuf, carry, ds_scr):
        i = pl.program_id(0)

        @pl.when(i == 0)
        def _():
            ds_scr[...] = jnp.zeros_like(ds_scr)
            dcw_ref[...] = jnp.zeros_like(dcw_ref)
            ddtb_ref[...] = jnp.zeros_like(ddtb_ref)
            dalog_ref[...] = jnp.zeros_like(dalog_ref)
            dnw_ref[...] = jnp.zeros_like(dnw_ref)

        q_pre, k_pre, v_pre, g3 = _gdn_split(lambda c0, c1: pre_ref[:, c0:c1], gate_ref)
        consts = (tri_ref[...], i64_ref[...], strict_ref[...], ts_ref[0])

        def f(q_pre, k_pre, v_pre, g3, smv, s, dtb, alog, nwv):
            return _gdn_chunk(q_pre, k_pre, v_pre, g3, smv, s, dtb, alog, nwv, *consts)[:2]

        _, vjp = jax.vjp(f, q_pre, k_pre, v_pre, g3, sm_ref[...], ss_ref[0], dtb_ref[...], alog_ref[...], nw_ref[...])
        do3 = jnp.stack([do_ref[:, 128 * h:128 * h + 128] for h in range(GDN_HEADS)])
        dq, dk, dv, dg3, dsm, ds, ddtb, dalog, dnw = vjp((do3, ds_scr[...]))
        ds_scr[...] = ds
        for h in range(GDN_HEADS):
            dgate_ref[:, 128 * h:128 * h + 128] = dg3[h]
        dsm_ref[...] = dsm + shared["dsm_ssd"]
        ddtb_ref[0:1, :] += ddtb
        dalog_ref[0:1, :] += dalog
        dnw_ref[0:1, :] += dnw
        ranges = [(base + 128 * h, base + 128 * h + 128) for base in (0, 1024, 2048) for h in range(GDN_HEADS)]
        dlist = [d[h] for d in (dq, dk, dv) for h in range(GDN_HEADS)]
        _conv_bwd(dlist, ranges, dbuf, carry, qkv_ref, cw_ref, dqkv_ref, dcw_ref, None, i == 0)

    rblk = lambda w: pl.BlockSpec((CHUNK, w), lambda i: (nc - 1 - i, 0))
    acc = lambda w: pl.BlockSpec((8, w), lambda i: (0, 0))
    f32 = jnp.float32
    return dict(
        body=body,
        in_specs=[rblk(1024), rblk(3072), rblk(3072), rblk(128),
                  pl.BlockSpec((1, 8, 128, 128), lambda i: (nc - 1 - i, 0, 0, 0)),
                  pl.BlockSpec((1, 8, CHUNK, CHUNK), lambda i: (nc - 1 - i, 0, 0, 0)), rblk(1024),
                  _full((CONV_K, 3072)), _full((1, 128)), _full((1, 128)), _full((1, 128)),
                  _full((64, 64)), _full((64, 64)), _full((64, 64))],
        out_specs=[rblk(1024), rblk(3072), rblk(128), acc(3072), acc(128), acc(128), acc(128)],
        out_shape=[jax.ShapeDtypeStruct((t, 1024), f32), jax.ShapeDtypeStruct((t, 3072), f32),
                   jax.ShapeDtypeStruct((t, 128), f32), jax.ShapeDtypeStruct((8, 3072), f32),
                   jax.ShapeDtypeStruct((8, 128), f32), jax.ShapeDtypeStruct((8, 128), f32),
                   jax.ShapeDtypeStruct((8, 128), f32)],
        scratch=[pltpu.VMEM((72, 3072), f32), pltpu.VMEM((8, 3072), f32), pltpu.VMEM((8, 128, 128), f32)],
        args=[gate, qkv, pre, sm, ss, ts, do, conv_w, dtb, alog, nw, cs["tri"], cs["i64"], cs["strict"]])


def _chunk_call(parts, name, nc):
    n_in = [len(p["args"]) for p in parts]
    n_out = [len(p["out_shape"]) for p in parts]
    n_scr = [len(p["scratch"]) for p in parts]

    def body(*refs):
        ins, outs, scr = refs[:sum(n_in)], refs[sum(n_in):sum(n_in) + sum(n_out)], refs[sum(n_in) + sum(n_out):]
        shared = {}
        for k, p in enumerate(parts):
            i0, o0, s0 = sum(n_in[:k]), sum(n_out[:k]), sum(n_scr[:k])
            p["body"](shared, *ins[i0:i0 + n_in[k]], *outs[o0:o0 + n_out[k]], *scr[s0:s0 + n_scr[k]])

    cat = lambda key: [v for p in parts for v in p[key]]
    return _pc(body, name=name, grid=(nc,), in_specs=cat("in_specs"), out_specs=cat("out_specs"),
               out_shape=cat("out_shape"), scratch_shapes=cat("scratch"),
               compiler_params=_cparams(("arbitrary",)))(*cat("args"))


def out_fwd_bwd(x, tgt, y_ssd, y_gdn, w_out, fnw):
    t = x.shape[0]
    tm = min(512, t)
    f32 = jnp.float32

    def body(x_ref, tgt_ref, ys_ref, yg_ref, w_ref, fnw_ref,
             dout_ref, dys_ref, dyg_ref, gw_ref, gfnw_ref, loss_ref, gw_acc):
        i = pl.program_id(0)

        @pl.when(i == 0)
        def _():
            gw_acc[...] = jnp.zeros_like(gw_acc)
            gfnw_ref[...] = jnp.zeros_like(gfnw_ref)
            loss_ref[...] = jnp.zeros_like(loss_ref)

        ys = ys_ref[...]
        yg = yg_ref[...]
        out = x_ref[...] + jnp.dot(ys, w_ref[0:1024, :], preferred_element_type=f32) \
            + jnp.dot(yg, w_ref[1024:2048, :], preferred_element_type=f32)
        rstd = lax.rsqrt(jnp.mean(out * out, axis=-1, keepdims=True) + EPS)
        yhat = out * rstd
        fw = fnw_ref[...]
        e = yhat * fw - tgt_ref[...]
        loss_ref[...] += 0.5 * jnp.sum(jnp.sum(e * e, axis=-1, keepdims=True) * (1.0 / D_MODEL), axis=0, keepdims=True)
        dyf = e * (1.0 / D_MODEL)
        gfnw_ref[0:1, :] += jnp.sum(dyf * yhat, axis=0, keepdims=True)
        dyhat = dyf * fw
        dout = rstd * (dyhat - yhat * jnp.mean(dyhat * yhat, axis=-1, keepdims=True))
        dout_ref[...] = dout
        db = dout.astype(_MM)
        dys_ref[...] = lax.dot_general(db, w_ref[0:1024, :], (((1,), (1,)), ((), ())), preferred_element_type=f32)
        dyg_ref[...] = lax.dot_general(db, w_ref[1024:2048, :], (((1,), (1,)), ((), ())), preferred_element_type=f32)
        gw_acc[0:1024, :] += lax.dot_general(ys, db, (((0,), (0,)), ((), ())), preferred_element_type=f32)
        gw_acc[1024:2048, :] += lax.dot_general(yg, db, (((0,), (0,)), ((), ())), preferred_element_type=f32)

        @pl.when(i == steps - 1)
        def _():
            gw_ref[...] = gw_acc[...].astype(gw_ref.dtype)

    steps = t // tm
    blk = pl.BlockSpec((tm, D_MODEL), lambda i: (i, 0))
    return _pc(
        body, name="out_fwd_bwd", grid=(steps,),
        in_specs=[blk, blk, blk, blk, _full((MIX_WIDTH, D_MODEL)), _full((1, D_MODEL))],
        out_specs=[blk, blk, blk, _full((MIX_WIDTH, D_MODEL)), _full((8, D_MODEL)), _full((1, 128))],
        out_shape=[jax.ShapeDtypeStruct((t, D_MODEL), f32)] * 3 +
                  [jax.ShapeDtypeStruct((MIX_WIDTH, D_MODEL), _MM), jax.ShapeDtypeStruct((8, D_MODEL), f32),
                   jax.ShapeDtypeStruct((1, 128), f32)],
        scratch_shapes=[pltpu.VMEM((MIX_WIDTH, D_MODEL), f32)],
        compiler_params=_cparams(("arbitrary",)),
    )(x, tgt, y_ssd, y_gdn, w_out, fnw)


def inproj_bwd_dx(x, dout, norm_w, w_perm, dgroups, scattered):
    t = x.shape[0]
    tm = min(256, t)
    f32 = jnp.float32

    def body(x_ref, dout_ref, nw_ref, w_ref, dz_ref, dxbc_ref, dgate_ref, dqkv_ref, dsm_ref, dx_ref, gnw_ref):
        i = pl.program_id(0)

        @pl.when(i == 0)
        def _():
            gnw_ref[...] = jnp.zeros_like(gnw_ref)

        du = None
        for (name, c0, c1), d_ref in zip(GROUPS, (dz_ref, dxbc_ref, dgate_ref, dqkv_ref, dsm_ref)):
            term = jnp.dot(d_ref[...].astype(_MM), w_ref[c0:c1, :], preferred_element_type=f32)
            du = term if du is None else du + term
        xf = x_ref[...]
        rstd = lax.rsqrt(jnp.mean(xf * xf, axis=-1, keepdims=True) + EPS)
        xhat = xf * rstd
        gnw_ref[0:1, :] += jnp.sum(du * xhat, axis=0, keepdims=True)
        dxh = du * nw_ref[...]
        dx_ref[...] = dout_ref[...] + rstd * (dxh - xhat * jnp.mean(dxh * xhat, axis=-1, keepdims=True))

    blk = lambda w: pl.BlockSpec((tm, w), lambda i: (i, 0))
    steps = t // tm
    kinds = ["scatter"] * len(scattered)
    hosted = _hosting(body, 9, 2, 0, kinds, lambda: pl.program_id(0) == 0, lambda: pl.program_id(0) == steps - 1)
    return _pc_comm(
        hosted, name="inproj_bwd_dx", grid=(steps,),
        in_specs=[blk(D_MODEL), blk(D_MODEL), _full((1, D_MODEL)), _full((PERM_DIM, D_MODEL))] +
                 [blk(c1 - c0) for _, c0, c1 in GROUPS] + [ANY] * len(scattered),
        out_specs=[blk(D_MODEL), _full((8, D_MODEL))] + [ANY] * len(scattered),
        out_shape=[jax.ShapeDtypeStruct((t, D_MODEL), f32), jax.ShapeDtypeStruct((8, D_MODEL), f32)] +
                  [_exchange_out_shape("scatter", a) for a in scattered],
        scratch_shapes=_exchange_sems(len(scattered)), compiler_params=_cparams(("arbitrary",)),
    )(x, dout, norm_w, w_perm, *dgroups, *scattered)


def grad_w_group(u, dg, name, scattered=()):
    t, n = dg.shape
    tn = 512 if n % 512 == 0 else n
    tm = 1024 if t % 1024 == 0 else t
    nj, nk = n // tn, t // tm
    f32 = jnp.float32

    def body(u_ref, d_ref, o_ref, acc):
        k = pl.program_id(1)

        @pl.when(k == 0)
        def _():
            acc[...] = jnp.zeros_like(acc)

        acc[...] += lax.dot_general(d_ref[...].astype(_MM), u_ref[...], (((0,), (0,)), ((), ())),
                                    preferred_element_type=f32)

        @pl.when(k == nk - 1)
        def _():
            o_ref[...] = acc[...].astype(o_ref.dtype)

    ne = len(scattered)
    hosted = _hosting(body, 2, 1, 1, ["scatter"] * ne,
                      lambda: (pl.program_id(0) == 0) & (pl.program_id(1) == 0),
                      lambda: (pl.program_id(0) == nj - 1) & (pl.program_id(1) == nk - 1))
    res = (_pc_comm if ne else _pc)(
        hosted, name=name, grid=(nj, nk),
        in_specs=[pl.BlockSpec((tm, D_MODEL), lambda j, k: (k, 0)),
                  pl.BlockSpec((tm, tn), lambda j, k: (k, j))] + [ANY] * ne,
        out_specs=[pl.BlockSpec((tn, D_MODEL), lambda j, k: (j, 0))] + [ANY] * ne,
        out_shape=[jax.ShapeDtypeStruct((n, D_MODEL), _MM)] + [_exchange_out_shape("scatter", a) for a in scattered],
        scratch_shapes=[pltpu.VMEM((tn, D_MODEL), f32)] + _exchange_sems(ne),
        compiler_params=_cparams(("arbitrary", "arbitrary")),
    )(u, dg, *scattered)
    return res if ne else res[0]


def _pad_lanes(v, off):
    n = v.shape[-1]
    return jnp.pad(v.reshape(1, n).astype(jnp.float32), ((0, 0), (off, 128 - off - n)))


def perm_w_in(wt_full):
    z = wt_full[0:1024]
    xbc = wt_full[1024:2560]
    dt = wt_full[2560:2576]
    gate = wt_full[2576:3600]
    qkv = wt_full[3600:6672]
    ab = wt_full[6672:6688]
    pad = jnp.zeros((PERM_DIM - IN_DIM, wt_full.shape[1]), wt_full.dtype)
    return jnp.concatenate([z, xbc, gate, qkv, dt, ab, pad], axis=0)


def unperm_w_in(gz, gxbc, ggate, gqkv, gsm):
    return jnp.concatenate([gz, gxbc, gsm[0:16], ggate, gqkv, gsm[16:32]], axis=0)


def all_gather(arrs, name):
    n = len(arrs)

    def body(*refs):
        ins, outs = refs[:n], refs[n:2 * n]
        send_sems, recv_sems, local_sems = refs[2 * n:]
        x, y, c, me = _me()
        sibling = (x, y, 1 - c)
        chips = [(1 - x, y), (x, 1 - y), (1 - x, 1 - y)]

        def idx(px, py, pc):
            return 4 * px + 2 * py + pc

        def copy(a, k, block, to, src=None):
            slot = outs[a].at[idx(*block)]
            return pltpu.make_async_remote_copy(src_ref=slot if src is None else src, dst_ref=slot,
                                                send_sem=send_sems.at[a, k], recv_sem=recv_sems.at[a, k],
                                                device_id=to, device_id_type=MESH)

        local = [pltpu.make_async_copy(ins[a], outs[a].at[me], local_sems.at[a]) for a in range(n)]
        for cp in local:
            cp.start()
        started = []
        for a in range(n):
            first = [copy(a, 0, (x, y, c), sibling, src=ins[a])]
            first += [copy(a, 1 + j, (x, y, c), (*chip, c), src=ins[a]) for j, chip in enumerate(chips)]
            for cp in first:
                cp.start()
            started += first
        for a in range(n):
            for j, chip in enumerate(chips):
                copy(a, 1 + j, (*chip, c), (x, y, c)).wait_recv()
                fwd = copy(a, 4 + j, (*chip, c), sibling)
                fwd.start()
                started.append(fwd)
        for a in range(n):
            copy(a, 0, sibling, (x, y, c)).wait_recv()
            for j, chip in enumerate(chips):
                copy(a, 4 + j, (*chip, 1 - c), (x, y, c)).wait_recv()
        for cp in started:
            cp.wait_send()
        for cp in local:
            cp.wait()

    return _pc_comm(
        body, name=name, in_specs=[ANY] * n, out_specs=[ANY] * n,
        out_shape=[jax.ShapeDtypeStruct((N_DEV,) + a.shape, a.dtype) for a in arrs],
        scratch_shapes=[pltpu.SemaphoreType.DMA((n, 7)), pltpu.SemaphoreType.DMA((n, 7)),
                        pltpu.SemaphoreType.DMA((n,))],
    )(*arrs)


def adamw_sum(recv, w, m, v, rows, name, cols=None):
    r, ccols = w.shape
    f32 = jnp.float32
    c1 = 1.0 / (1.0 - ADAM_B1 ** ADAM_STEP)
    c2 = 1.0 / (1.0 - ADAM_B2 ** ADAM_STEP)

    def body(recv_ref, w_ref, m_ref, v_ref, g_ref, d_ref, mo_ref, vo_ref):
        g = recv_ref[0].astype(f32)
        for k in range(1, N_DEV):
            g = g + recv_ref[k].astype(f32)
        mn = ADAM_B1 * m_ref[...] + (1.0 - ADAM_B1) * g
        vn = ADAM_B2 * v_ref[...] + (1.0 - ADAM_B2) * (g * g)
        g_ref[...] = g
        mo_ref[...] = mn
        vo_ref[...] = vn
        d_ref[...] = -ADAM_LR * ((mn * c1) / (jnp.sqrt(vn * c2) + ADAM_EPS) + ADAM_WD * w_ref[...])

    if cols is None:
        blk = pl.BlockSpec((rows, ccols), lambda i: (i, 0))
        rblk, steps = pl.BlockSpec((N_DEV, rows, ccols), lambda i: (0, i, 0)), r // rows
    else:
        blk = pl.BlockSpec((r, cols), lambda i: (0, i))
        rblk, steps = pl.BlockSpec((N_DEV, r, cols), lambda i: (0, 0, i)), ccols // cols
    return _pc(
        body, name=name, grid=(steps,),
        in_specs=[rblk, blk, blk, blk],
        out_specs=[blk] * 4, out_shape=[jax.ShapeDtypeStruct((r, ccols), f32)] * 4,
        compiler_params=_cparams(("arbitrary",)),
    )(recv, w, m, v)


REP = (("norm_w", 1024), ("ssd_conv_b", 1536), ("ssd_dt_bias", 16), ("ssd_a_log", 16), ("ssd_d", 16),
       ("ssd_norm_w", 1024), ("gdn_dt_bias", 8), ("gdn_a_log", 8), ("gdn_norm_w", 128), ("final_norm_w", 1024))
REP_ROWS = 48
SHARD = (("ssd_conv_w", CONV_K * SSD_CONV_DIM // N_DEV), ("gdn_conv_w", CONV_K * GDN_CONV_DIM // N_DEV))
SHARD_ROWS = 24


def _rows_of(size):
    return -(-size // 128)


def _pack(vals, layout, total_rows):
    parts = []
    for (name, size), val in zip(layout, vals):
        flat = val.reshape(-1).astype(jnp.float32)
        parts.append(jnp.pad(flat, (0, _rows_of(size) * 128 - size)).reshape(-1, 128))
    used = sum(_rows_of(s) for _, s in layout)
    parts.append(jnp.zeros((total_rows - used, 128), jnp.float32))
    return jnp.concatenate(parts, axis=0)


def _unpack(packed, layout, row0=0):
    out, r = {}, row0
    for name, size in layout:
        n = _rows_of(size)
        out[name] = packed[r:r + n].reshape(-1)[:size]
        r += n
    return out


def _conv_slabs(g_full):
    k, ccols = g_full.shape
    return g_full.reshape(k, N_DEV, ccols // N_DEV).transpose(1, 0, 2).reshape(N_DEV, -1)


def _conv_full(gathered_flat, ccols):
    return gathered_flat.reshape(N_DEV, CONV_K, ccols // N_DEV).transpose(1, 0, 2).reshape(CONV_K, ccols)


def kernel(x, norm_w, w_in, ssd_conv_w, ssd_conv_b, ssd_dt_bias, ssd_a_log, ssd_d, ssd_norm_w, gdn_conv_w, gdn_dt_bias, gdn_a_log, gdn_norm_w, w_out, final_norm_w, loss_target, m_norm_w, m_w_in, m_ssd_conv_w, m_ssd_conv_b, m_ssd_dt_bias, m_ssd_a_log, m_ssd_d, m_ssd_norm_w, m_gdn_conv_w, m_gdn_dt_bias, m_gdn_a_log, m_gdn_norm_w, m_w_out, m_final_norm_w, v_norm_w, v_w_in, v_ssd_conv_w, v_ssd_conv_b, v_ssd_dt_bias, v_ssd_a_log, v_ssd_d, v_ssd_norm_w, v_gdn_conv_w, v_gdn_dt_bias, v_gdn_a_log, v_gdn_norm_w, v_w_out, v_final_norm_w):
    f32 = jnp.float32
    w = dict(norm_w=norm_w, w_in=w_in, ssd_conv_w=ssd_conv_w, ssd_conv_b=ssd_conv_b, ssd_dt_bias=ssd_dt_bias,
             ssd_a_log=ssd_a_log, ssd_d=ssd_d, ssd_norm_w=ssd_norm_w, gdn_conv_w=gdn_conv_w, gdn_dt_bias=gdn_dt_bias,
             gdn_a_log=gdn_a_log, gdn_norm_w=gdn_norm_w, w_out=w_out, final_norm_w=final_norm_w)
    m = dict(norm_w=m_norm_w, w_in=m_w_in, ssd_conv_w=m_ssd_conv_w, ssd_conv_b=m_ssd_conv_b, ssd_dt_bias=m_ssd_dt_bias,
             ssd_a_log=m_ssd_a_log, ssd_d=m_ssd_d, ssd_norm_w=m_ssd_norm_w, gdn_conv_w=m_gdn_conv_w,
             gdn_dt_bias=m_gdn_dt_bias, gdn_a_log=m_gdn_a_log, gdn_norm_w=m_gdn_norm_w, w_out=m_w_out,
             final_norm_w=m_final_norm_w)
    v = dict(norm_w=v_norm_w, w_in=v_w_in, ssd_conv_w=v_ssd_conv_w, ssd_conv_b=v_ssd_conv_b, ssd_dt_bias=v_ssd_dt_bias,
             ssd_a_log=v_ssd_a_log, ssd_d=v_ssd_d, ssd_norm_w=v_ssd_norm_w, gdn_conv_w=v_gdn_conv_w,
             gdn_dt_bias=v_gdn_dt_bias, gdn_a_log=v_gdn_a_log, gdn_norm_w=v_gdn_norm_w, w_out=v_w_out,
             final_norm_w=v_final_norm_w)
    names = list(w)
    shapes = {n: w[n].shape for n in names}

    xl, tgt = x[0], loss_target[0]
    cs = _consts()
    dtb_s = _pad_lanes(ssd_dt_bias, 0)
    alog_s = _pad_lanes(ssd_a_log, 0)
    dpar = _pad_lanes(ssd_d, 0)
    dtb_g = _pad_lanes(gdn_dt_bias, 16)
    alog_g = _pad_lanes(gdn_a_log, 16)
    nw_g = gdn_norm_w.reshape(1, 128)
    nw_s = ssd_norm_w.reshape(1, 1024)
    cb_s = ssd_conv_b.reshape(1, 1536)
    nw1 = norm_w.reshape(1, D_MODEL)

    (g_w_in,) = all_gather([w_in[0].T.astype(_MM)], "gather_w_in")
    w_perm = perm_w_in(g_w_in.reshape(IN_DIM, D_MODEL))
    conv_pack = _pack([w["ssd_conv_w"], w["gdn_conv_w"]], SHARD, SHARD_ROWS)
    u, z, xbc, gate, qkv, sm, g_w_out, g_conv = inproj_fwd(xl, nw1, w_perm, [w_out[0].astype(_MM), conv_pack])
    w_out_full = g_w_out.reshape(MIX_WIDTH, D_MODEL)
    ssd_cw = _conv_full(g_conv[:, 0:6].reshape(N_DEV, -1), SSD_CONV_DIM)
    gdn_cw = _conv_full(g_conv[:, 6:18].reshape(N_DEV, -1), GDN_CONV_DIM)

    nc = xl.shape[0] // CHUNK
    y_ssd, hs, pre_s, y_gdn, ss, ts, pre_g = _chunk_call(
        [ssd_fwd(z, xbc, sm, ssd_cw, cb_s, dtb_s, alog_s, dpar, nw_s, cs),
         gdn_fwd(gate, qkv, sm, gdn_cw, dtb_g, alog_g, nw_g, cs)], "scan_fwd", nc)
    dout, dys, dyg, g_wout, g_fnw, loss_l = out_fwd_bwd(xl, tgt, y_ssd, y_gdn, w_out_full,
                                                        final_norm_w.reshape(1, D_MODEL))
    (dz, dxbc, g_cw_s, g_cb_s, g_dtb_s, g_alog_s, g_d, g_nw_s,
     dgate, dqkv, dsm, g_cw_g, g_dtb_g, g_alog_g, g_nw_g) = _chunk_call(
        [ssd_bwd(z, xbc, pre_s, sm, hs, dys, ssd_cw, dtb_s, alog_s, dpar, nw_s, cs),
         gdn_bwd(gate, qkv, pre_g, sm, ss, ts, dyg, gdn_cw, dtb_g, alog_g, nw_g, cs)], "scan_bwd", nc)

    t_w_out = g_wout.reshape(N_DEV, MIX_WIDTH // N_DEV, D_MODEL)
    gws = {}
    for dg, (name, _, _) in zip((dz, dxbc, dgate, dsm), (GROUPS[0], GROUPS[1], GROUPS[2], GROUPS[4])):
        gws[name] = grad_w_group(u, dg, "grad_w_in_" + name)
    gws["qkv"], r_w_out = grad_w_group(u, dqkv, "grad_w_in_qkv", [t_w_out])
    g_w_in_full = unperm_w_in(gws["z"], gws["xbc"], gws["gate"], gws["qkv"], gws["sm"])
    t_w_in = g_w_in_full.reshape(N_DEV, W_IN_SHARD, D_MODEL)
    dx, g_nw, r_w_in = inproj_bwd_dx(xl, dout, nw1, w_perm, (dz, dxbc, dgate, dqkv, dsm), [t_w_in])

    g = dict(norm_w=g_nw[0:1, :], ssd_conv_b=g_cb_s[0:1, :], ssd_dt_bias=g_dtb_s[0:1, 0:16],
             ssd_a_log=g_alog_s[0:1, 0:16], ssd_d=g_d[0:1, 0:16], ssd_norm_w=g_nw_s[0:1, :],
             gdn_dt_bias=g_dtb_g[0:1, 16:24], gdn_a_log=g_alog_g[0:1, 16:24], gdn_norm_w=g_nw_g[0:1, :],
             final_norm_w=g_fnw[0:1, :])
    rep = _pack([g[n] for n, _ in REP], REP, REP_ROWS)
    shard_rows = jnp.concatenate([_conv_slabs(g_cw_s[0:4, :]).reshape(N_DEV, 6, 128),
                                  _conv_slabs(g_cw_g[0:4, :]).reshape(N_DEV, 12, 128),
                                  jnp.zeros((N_DEV, SHARD_ROWS - 18, 128), f32)], axis=1)
    t_small = jnp.concatenate([jnp.broadcast_to(rep[None], (N_DEV, REP_ROWS, 128)), shard_rows], axis=1)
    (r_small,) = exchange([t_small], ["scatter"], "scatter_small_grads")

    o_w_in = adamw_sum(r_w_in, w_in[0].T, m_w_in[0].T, v_w_in[0].T, None, "adamw_w_in", cols=256)
    o_w_in = [o.T for o in o_w_in]
    o_w_out = adamw_sum(r_w_out, w_out[0], m_w_out[0], v_w_out[0], 64, "adamw_w_out")
    small = [jnp.concatenate([_pack([d[n] for n, _ in REP], REP, REP_ROWS),
                              _pack([d[n] for n, _ in SHARD], SHARD, SHARD_ROWS)], axis=0) for d in (w, m, v)]
    o_small = adamw_sum(r_small, small[0], small[1], small[2], REP_ROWS + SHARD_ROWS, "adamw_small")

    loss = lax.psum(loss_l[0, 0], ("x", "y", "c"))
    outs = [loss, dx[None]]
    for k in range(4):
        parts = {**_unpack(o_small[k], REP), **_unpack(o_small[k], SHARD, REP_ROWS),
                 "w_in": o_w_in[k], "w_out": o_w_out[k]}
        outs += [parts[n].reshape(shapes[n]) for n in names]
    return tuple(outs)
```

```python
import functools

import jax
import jax.numpy as jnp
import numpy as np
from jax import lax
from jax.experimental import pallas as pl
from jax.experimental.pallas import tpu as pltpu

_MM = jnp.bfloat16

D_MODEL = 1024
CHUNK = 64
CONV_K = 4
EPS = 1e-6
SSD_CONV_DIM = 1536
GDN_HEADS = 8
GDN_DK = 128
GDN_CONV_DIM = 3072
MIX_WIDTH = 2048
IN_DIM = 6688
N_DEV = 8
W_IN_SHARD = IN_DIM // N_DEV
PERM_DIM = 6784
HI = lax.Precision.HIGHEST
HIGH = lax.Precision.HIGH
VMEM_LIMIT = 56 * 1024 * 1024

ADAM_LR = 0.001
ADAM_B1 = 0.9
ADAM_B2 = 0.999
ADAM_EPS = 1e-08
ADAM_WD = 0.01
ADAM_STEP = 10


def _pc(body, **kw):
    return pl.pallas_call(body, **kw)


def _pc_comm(body, **kw):
    return pl.pallas_call(body, **kw)


def _cparams(sem):
    return pltpu.CompilerParams(dimension_semantics=sem, vmem_limit_bytes=VMEM_LIMIT)


def _sig(x):
    return 0.5 * jnp.tanh(0.5 * x) + 0.5


@jax.custom_vjp
def _sigmoid(x):
    return _sig(x)


def _sigmoid_fwd(x):
    s = _sig(x)
    return s, s


def _sigmoid_bwd(s, g):
    return (g * s * (1.0 - s),)


_sigmoid.defvjp(_sigmoid_fwd, _sigmoid_bwd)


@jax.custom_vjp
def _silu(x):
    return x * _sig(x)


def _silu_fwd(x):
    s = _sig(x)
    return x * s, (x, s)


def _silu_bwd(res, g):
    x, s = res
    return (g * (s * (1.0 + x * (1.0 - s))),)


_silu.defvjp(_silu_fwd, _silu_bwd)


def _softplus_impl(x):
    return jnp.maximum(x, 0.0) + jnp.log(1.0 + jnp.exp(-jnp.abs(x)))


@jax.custom_vjp
def _softplus(x):
    return _softplus_impl(x)


def _softplus_fwd(x):
    return _softplus_impl(x), x


def _softplus_bwd(x, g):
    return (g * _sig(x),)


_softplus.defvjp(_softplus_fwd, _softplus_bwd)


def _lane_bcast_impl(x, k):
    return jnp.broadcast_to(x[..., k:k + 1], x.shape)


@functools.partial(jax.custom_vjp, nondiff_argnums=(1,))
def _lane_bcast(x, k):
    return _lane_bcast_impl(x, k)


def _lane_bcast_fwd(x, k):
    return _lane_bcast_impl(x, k), None


def _lane_bcast_bwd(k, _, g):
    lane = lax.broadcasted_iota(jnp.int32, g.shape, g.ndim - 1)
    return (jnp.where(lane == k, jnp.sum(g, axis=-1, keepdims=True), 0.0),)


_lane_bcast.defvjp(_lane_bcast_fwd, _lane_bcast_bwd)


def _mm(a, b):
    return jnp.dot(a.astype(_MM), b.astype(_MM), preferred_element_type=jnp.float32)


def _mm_nt(a, b):
    return lax.dot_general(a.astype(_MM), b.astype(_MM), (((1,), (1,)), ((), ())),
                           preferred_element_type=jnp.float32)


def _mm_tn(a, b):
    return lax.dot_general(a.astype(_MM), b.astype(_MM), (((0,), (0,)), ((), ())),
                           preferred_element_type=jnp.float32)


def _dot_hi(a, b):
    return jnp.dot(a, b, precision=HI, preferred_element_type=jnp.float32)


def _bmm(a, b):
    return lax.dot_general(a.astype(_MM), b.astype(_MM), (((2,), (1,)), ((0,), (0,))),
                           preferred_element_type=jnp.float32)


def _bmm_nt(a, b):
    return lax.dot_general(a.astype(_MM), b.astype(_MM), (((2,), (2,)), ((0,), (0,))),
                           preferred_element_type=jnp.float32)


def _bmm_tn(a, b):
    return lax.dot_general(a.astype(_MM), b.astype(_MM), (((1,), (1,)), ((0,), (0,))),
                           preferred_element_type=jnp.float32)


def _bmm_hi(a, b):
    return lax.dot_general(a, b, (((2,), (1,)), ((0,), (0,))), precision=HIGH, preferred_element_type=jnp.float32)


def _bmm_nt_hi(a, b):
    return lax.dot_general(a, b, (((2,), (2,)), ((0,), (0,))), precision=HIGH, preferred_element_type=jnp.float32)


def _bmm_tn_hi(a, b):
    return lax.dot_general(a, b, (((1,), (1,)), ((0,), (0,))), precision=HIGH, preferred_element_type=jnp.float32)


def _consts():
    l = np.arange(CHUNK)
    tri = (l[:, None] >= l[None, :]).astype(np.float32)
    lane = np.arange(128)
    i2 =(l[:, None] == (lane[None, :] % 64)).astype(np.float32)
    mask2 = (l[:, None] >= (lane[None, :] % 64)).astype(np.float32)
    lo = (lane < 64).astype(np.float32)[None, :]
    i64 = np.eye(CHUNK, dtype=np.float32)
    strict = (l[:, None] > l[None, :]).astype(np.float32)
    return dict(tri=jnp.asarray(tri), i2=jnp.asarray(i2), mask2=jnp.asarray(mask2), lo=jnp.asarray(lo),
                i64=jnp.asarray(i64), strict=jnp.asarray(strict))


def _ssd_chunk(xs_pre, b_pre, c_pre, z, sm, ht, dtb, alog, dpar, nw, tri, i2, mask2, lo):
    lane = lax.broadcasted_iota(jnp.int32, (1, 128), 1)
    m16 = lane < 16
    dt = jnp.where(m16, _softplus(sm + dtb), 0.0)
    a_neg = -jnp.exp(alog)
    cum = _dot_hi(tri, dt * a_neg)
    row = lax.broadcasted_iota(jnp.int32, (CHUNK, 1), 0)
    is_last = row == CHUNK - 1
    hi = 1.0 - lo
    bm = [_silu(b) for b in b_pre]
    cm = [_silu(c) for c in c_pre]
    cb2 = [_mm_nt(cm[g], jnp.concatenate([bm[g], bm[g]], axis=0)) for g in range(2)]
    yg, ht_next = [], []
    for j in range(8):
        g = j // 4
        pair = lambda v, j=j: jnp.where(lo > 0.5, _lane_bcast(v, 2 * j), _lane_bcast(v, 2 * j + 1))
        xs = _silu(xs_pre[j])
        dte = pair(dt)
        cume = pair(cum)
        cum_last = jnp.sum(jnp.where(is_last, cume, 0.0), axis=0, keepdims=True)
        xdt = xs * dte
        rowv = jnp.sum(cume * i2, axis=0, keepdims=True)
        lm = jnp.exp(jnp.where(mask2 > 0.5, cume - rowv, -jnp.inf))
        m = cb2[g] * lm
        xblk = jnp.concatenate([xdt * lo, xdt * hi], axis=0)
        y = _mm(m, xblk)
        y = y + _mm(cm[g], ht[j]) * jnp.exp(cume)
        y = y + pair(dpar) * xs
        yg.append(y * _silu(z[j]))
        st = _mm_tn(bm[g], xdt * jnp.exp(cum_last - cume))
        ht_next.append(ht[j] * jnp.exp(cum_last) + st)
    outs = []
    for g in range(2):
        ss = sum(jnp.sum(yg[j] * yg[j], axis=-1, keepdims=True) for j in range(4 * g, 4 * g + 4))
        rs = lax.rsqrt(ss * (1.0 / 512.0) + EPS)
        for j in range(4 * g, 4 * g + 4):
            outs.append(yg[j] * rs * nw[j])
    return outs, ht_next


def _tri_inverse(a):
    eye = jnp.eye(CHUNK, dtype=jnp.float32)[None]
    p = eye - a
    ap = a
    for _ in range(5):
        ap = _bmm_hi(ap, ap)
        p = p + _bmm_hi(p, ap)
    return p


@jax.custom_vjp
def _solve(a, r1, r2, t):
    return _bmm_hi(t, r1), _bmm_hi(t, r2)


def _solve_fwd(a, r1, r2, t):
    u, w = _bmm_hi(t, r1), _bmm_hi(t, r2)
    return (u, w), (t, u, w)


def _solve_bwd(res, cts):
    t, u, w = res
    du, dw = cts
    dr1 = _bmm_tn_hi(t, du)
    dr2 = _bmm_tn_hi(t, dw)
    da = -(_bmm_nt_hi(dr1, u) + _bmm_nt_hi(dr2, w))
    return da, dr1, dr2, jnp.zeros_like(t)


_solve.defvjp(_solve_fwd, _solve_bwd)


def _gdn_chunk(q_pre, k_pre, v_pre, gate, sm, s, dtb, alog, nw, tri, i64, strict, t_in=None):
    lane = lax.broadcasted_iota(jnp.int32, (1, 128), 1)
    m_a = (lane >= 16) & (lane < 24)
    g_full = jnp.where(m_a, -jnp.exp(alog) * _softplus(sm + dtb), 0.0)
    gc = _dot_hi(tri, g_full)
    sig = _sigmoid(sm)
    gc3 = jnp.stack([_lane_bcast(gc, 16 + h) for h in range(GDN_HEADS)])
    beta3 = jnp.stack([_lane_bcast(sig, 24 + h) for h in range(GDN_HEADS)])
    q = _silu(q_pre)
    q = q * lax.rsqrt(jnp.sum(q * q, axis=-1, keepdims=True) + EPS) * (GDN_DK ** -0.5)
    k = _silu(k_pre)
    k = k * lax.rsqrt(jnp.sum(k * k, axis=-1, keepdims=True) + EPS)
    v = _silu(v_pre)
    gcl = gc3[:, :, :CHUNK]
    gc_row = jnp.sum(gcl * i64[None], axis=1, keepdims=True)
    incl = (strict + i64)[None] > 0.5
    decay = jnp.exp(jnp.where(incl, gcl - gc_row, -jnp.inf))
    kb = k * beta3
    a = jnp.where(strict[None] > 0.5, _bmm_nt(kb, k) * decay, 0.0)
    egc = jnp.exp(gc3)
    t = _tri_inverse(a) if t_in is None else t_in
    u, w = _solve(a, v * beta3, kb * egc, t)
    attn = _bmm_nt(q, k) * decay
    row = lax.broadcasted_iota(jnp.int32, (1, CHUNK, 1), 1)
    gl = jnp.sum(jnp.where(row == CHUNK - 1, gc3, 0.0), axis=1, keepdims=True)
    q_dec = q * egc
    k_dec = k * jnp.exp(gl - gc3)
    v_new = u - _bmm(w, s)
    o = _bmm(q_dec, s) + _bmm(attn, v_new)
    s_next = s * jnp.exp(gl) + _bmm_tn(k_dec, v_new)
    on = o * lax.rsqrt(jnp.mean(o * o, axis=-1, keepdims=True) + EPS) * nw
    return on * _silu(gate), s_next, t


def _conv_fwd(pbuf, w_ref, c0, c1):
    acc = None
    for j in range(CONV_K):
        term = w_ref[j:j + 1, c0:c1] * pbuf[5 + j:69 + j, c0:c1]
        acc = term if acc is None else acc + term
    return acc


MESH = pl.DeviceIdType.MESH
ANY = pl.BlockSpec(memory_space=pl.ANY)


def _me():
    x, y, c = lax.axis_index("x"), lax.axis_index("y"), lax.axis_index("c")
    return x, y, c, 4 * x + 2 * y + c


def _peer(r):
    x, y, c, _ = _me()
    px = 1 - x if r & 4 else x
    py = 1 - y if r & 2 else y
    pc = 1 - c if r & 1 else c
    return (px, py, pc), 4 * px + 2 * py + pc


def _exchange_ops(kind, in_ref, out_ref, send_sems, recv_sems, local_sem):
    me = _me()[3]
    local = pltpu.make_async_copy(in_ref.at[me] if kind == "scatter" else in_ref, out_ref.at[me], local_sem)
    sends, recvs = [], []
    for r in range(1, N_DEV):
        peer, pidx = _peer(r)
        src = in_ref.at[pidx] if kind == "scatter" else in_ref
        sems = dict(send_sem=send_sems.at[r - 1], recv_sem=recv_sems.at[r - 1], device_id=peer, device_id_type=MESH)
        sends.append(pltpu.make_async_remote_copy(src_ref=src, dst_ref=out_ref.at[me], **sems))
        recvs.append(pltpu.make_async_remote_copy(src_ref=src, dst_ref=out_ref.at[pidx], **sems))

    def start():
        local.start()
        for cp in sends:
            cp.start()

    def wait():
        for cp in recvs:
            cp.wait_recv()
        for cp in sends:
            cp.wait_send()
        local.wait()

    return start, wait


def _exchange_sems(n):
    return [pltpu.SemaphoreType.DMA((N_DEV - 1,)), pltpu.SemaphoreType.DMA((N_DEV - 1,)),
            pltpu.SemaphoreType.DMA(())] * n


def _exchange_out_shape(kind, a):
    return jax.ShapeDtypeStruct(a.shape if kind == "scatter" else (N_DEV,) + a.shape, a.dtype)


def _hosting(body, n_in, n_out, n_scratch, kinds, first, last):
    ne = len(kinds)

    def wrapped(*refs):
        ins, ex_in = refs[:n_in], refs[n_in:n_in + ne]
        o0 = n_in + ne
        outs, ex_out = refs[o0:o0 + n_out], refs[o0 + n_out:o0 + n_out + ne]
        s0 = o0 + n_out + ne
        scr, sems = refs[s0:s0 + n_scratch], refs[s0 + n_scratch:]
        ops = [_exchange_ops(kinds[e], ex_in[e], ex_out[e], *sems[3 * e:3 * e + 3]) for e in range(ne)]

        @pl.when(first())
        def _():
            for start, _ in ops:
                start()

        body(*ins, *outs, *scr)

        @pl.when(last())
        def _():
            for _, wait in ops:
                wait()

    return wrapped


GROUPS = (("z", 0, 1024), ("xbc", 1024, 2560), ("gate", 2560, 3584), ("qkv", 3584, 6656), ("sm", 6656, 6784))


def inproj_fwd(x, norm_w, w_perm, gathered):
    t = x.shape[0]
    tm = min(256, t)
    steps = t // tm
    kinds = ["gather"] * len(gathered)

    def body(x_ref, nw_ref, w_ref, u_ref, z_ref, xbc_ref, gate_ref, qkv_ref, sm_ref):
        xf = x_ref[...]
        rstd = lax.rsqrt(jnp.mean(xf * xf, axis=-1, keepdims=True) + EPS)
        u = (xf * rstd * nw_ref[...]).astype(_MM)
        u_ref[...] = u
        for (name, c0, c1), o_ref in zip(GROUPS, (z_ref, xbc_ref, gate_ref, qkv_ref, sm_ref)):
            o_ref[...] = lax.dot_general(u, w_ref[c0:c1, :], (((1,), (1,)), ((), ())),
                                         preferred_element_type=jnp.float32)

    outs = [jax.ShapeDtypeStruct((t, D_MODEL), _MM)] + [jax.ShapeDtypeStruct((t, c1 - c0), jnp.float32)
                                                        for _, c0, c1 in GROUPS]
    hosted = _hosting(body, 3, 6, 0, kinds, lambda: pl.program_id(0) == 0, lambda: pl.program_id(0) == steps - 1)
    return _pc_comm(
        hosted, name="inproj_fwd", grid=(steps,),
        in_specs=[pl.BlockSpec((tm, D_MODEL), lambda i: (i, 0)),
                  pl.BlockSpec((1, D_MODEL), lambda i: (0, 0)),
                  pl.BlockSpec((PERM_DIM, D_MODEL), lambda i: (0, 0))] + [ANY] * len(gathered),
        out_specs=[pl.BlockSpec((tm, D_MODEL), lambda i: (i, 0))] +
                  [pl.BlockSpec((tm, c1 - c0), lambda i: (i, 0)) for _, c0, c1 in GROUPS] + [ANY] * len(gathered),
        out_shape=outs + [_exchange_out_shape("gather", a) for a in gathered],
        scratch_shapes=_exchange_sems(len(gathered)), compiler_params=_cparams(("arbitrary",)),
    )(x, norm_w, w_perm, *gathered)


def _halo_spec(width, idx_fn):
    return pl.BlockSpec((8, width), lambda i: (jnp.maximum(idx_fn(i) * 8 - 1, 0), 0))


def _full(shape):
    nd = len(shape)
    return pl.BlockSpec(shape, lambda i: (0,) * nd)


def _ssd_split(pre_fn, z_ref, sm_ref):
    xs_pre = [pre_fn(128 * j, 128 * j + 128) for j in range(8)]
    b_pre = [pre_fn(1024 + 128 * g, 1152 + 128 * g) for g in range(2)]
    c_pre = [pre_fn(1280 + 128 * g, 1408 + 128 * g) for g in range(2)]
    z = [z_ref[:, 128 * j:128 * j + 128] for j in range(8)]
    return xs_pre, b_pre, c_pre, z, sm_ref[...]


def ssd_fwd(z, xbc, sm, conv_w, conv_b, dtb, alog, dpar, nw, cs):
    t = z.shape[0]
    nc = t // CHUNK

    def body(shared, z_ref, xbc_ref, halo_ref, sm_ref, cw_ref, cb_ref, dtb_ref, alog_ref, dpar_ref, nw_ref,
             tri_ref, i2_ref, mask2_ref, lo_ref, y_ref, hs_ref, pre_ref, pbuf, ht_scr):
        i = pl.program_id(0)

        @pl.when(i == 0)
        def _():
            ht_scr[...] = jnp.zeros_like(ht_scr)

        pbuf[0:8, :] = jnp.where(i == 0, 0.0, halo_ref[...])
        pbuf[8:72, :] = xbc_ref[...]

        def pre_fn(c0, c1):
            pre = _conv_fwd(pbuf, cw_ref, c0, c1) + cb_ref[:, c0:c1]
            pre_ref[:, c0:c1] = pre
            return pre

        xs_pre, b_pre, c_pre, zz, smv = _ssd_split(pre_fn, z_ref, sm_ref)
        ht = [ht_scr[:, 128 * j:128 * j + 128] for j in range(8)]
        hs_ref[0] = ht_scr[...]
        nwl = [nw_ref[:, 128 * j:128 * j + 128] for j in range(8)]
        outs, ht_next = _ssd_chunk(xs_pre, b_pre, c_pre, zz, smv, ht, dtb_ref[...], alog_ref[...], dpar_ref[...],
                                   nwl, tri_ref[...], i2_ref[...], mask2_ref[...], lo_ref[...])
        for j in range(8):
            y_ref[:, 128 * j:128 * j + 128] = outs[j].astype(y_ref.dtype)
            ht_scr[:, 128 * j:128 * j + 128] = ht_next[j]

    blk = lambda w: pl.BlockSpec((CHUNK, w), lambda i: (i, 0))
    return dict(
        body=body,
        in_specs=[blk(1024), blk(1536), _halo_spec(1536, lambda i: i), blk(128),
                  _full((CONV_K, 1536)), _full((1, 1536)), _full((1, 128)), _full((1, 128)), _full((1, 128)),
                  _full((1, 1024)), _full((64, 64)), _full((64, 128)), _full((64, 128)),
                  _full((1, 128))],
        out_specs=[blk(1024), pl.BlockSpec((1, 128, 1024), lambda i: (i, 0, 0)), blk(1536)],
        out_shape=[jax.ShapeDtypeStruct((t, 1024), _MM), jax.ShapeDtypeStruct((nc, 128, 1024), jnp.float32),
                   jax.ShapeDtypeStruct((t, 1536), jnp.float32)],
        scratch=[pltpu.VMEM((72, 1536), jnp.float32), pltpu.VMEM((128, 1024), jnp.float32)],
        args=[z, xbc, xbc, sm, conv_w, conv_b, dtb, alog, dpar, nw, cs["tri"], cs["i2"], cs["mask2"], cs["lo"]])


def _conv_bwd(dpre_list, col_ranges, dbuf, carry, x_ref, cw_ref, dx_ref, dcw_ref, dcb_ref, first):
    for dpre, (c0, c1) in zip(dpre_list, col_ranges):
        dbuf[0:64, c0:c1] = dpre
    dbuf[64:72, :] = jnp.where(first, 0.0, carry[...])
    carry[...] = dbuf[0:8, :]
    for (c0, c1) in col_ranges:
        xin = x_ref[:, c0:c1]
        acc = None
        for j in range(CONV_K):
            sh = dbuf[3 - j:67 - j, c0:c1]
            term = cw_ref[j:j + 1, c0:c1] * sh
            acc = term if acc is None else acc + term
            dcw_ref[j:j + 1, c0:c1] += jnp.sum(xin * sh, axis=0, keepdims=True)
        dx_ref[:, c0:c1] = acc
        if dcb_ref is not None:
            dcb_ref[0:1, c0:c1] += jnp.sum(dbuf[0:64, c0:c1], axis=0, keepdims=True)


def ssd_bwd(z, xbc, pre, sm, hs, dy, conv_w, dtb, alog, dpar, nw, cs):
    t = z.shape[0]
    nc = t // CHUNK

    def body(shared, z_ref, xbc_ref, pre_ref, sm_ref, hs_ref, dy_ref, cw_ref, dtb_ref, alog_ref, dpar_ref, nw_ref,
             tri_ref, i2_ref, mask2_ref, lo_ref,
             dz_ref, dxbc_ref, dcw_ref, dcb_ref, ddtb_ref, dalog_ref, ddpar_ref, dnw_ref,
             dbuf, carry, dht_scr):
        i = pl.program_id(0)

        @pl.when(i == 0)
        def _():
            dht_scr[...] = jnp.zeros_like(dht_scr)
            dcw_ref[...] = jnp.zeros_like(dcw_ref)
            dcb_ref[...] = jnp.zeros_like(dcb_ref)
            ddtb_ref[...] = jnp.zeros_like(ddtb_ref)
            dalog_ref[...] = jnp.zeros_like(dalog_ref)
            ddpar_ref[...] = jnp.zeros_like(ddpar_ref)
            dnw_ref[...] = jnp.zeros_like(dnw_ref)

        pre_fn = lambda c0, c1: pre_ref[:, c0:c1]
        xs_pre, b_pre, c_pre, zz, smv = _ssd_split(pre_fn, z_ref, sm_ref)
        ht = [hs_ref[0, :, 128 * j:128 * j + 128] for j in range(8)]
        nwl = [nw_ref[:, 128 * j:128 * j + 128] for j in range(8)]
        consts = (tri_ref[...], i2_ref[...], mask2_ref[...], lo_ref[...])

        def f(xs_pre, b_pre, c_pre, zz, smv, ht, dtb, alog, dpar, nwl):
            return _ssd_chunk(xs_pre, b_pre, c_pre, zz, smv, ht, dtb, alog, dpar, nwl, *consts)

        _, vjp = jax.vjp(f, xs_pre, b_pre, c_pre, zz, smv, ht, dtb_ref[...], alog_ref[...], dpar_ref[...], nwl)
        dys = [dy_ref[:, 128 * j:128 * j + 128] for j in range(8)]
        dhts = [dht_scr[:, 128 * j:128 * j + 128] for j in range(8)]
        dxs, db, dc, dzz, dsm, dht, ddtb, dalog, ddpar, dnwl = vjp((dys, dhts))
        for j in range(8):
            dz_ref[:, 128 * j:128 * j + 128] = dzz[j]
            dht_scr[:, 128 * j:128 * j + 128] = dht[j]
            dnw_ref[0:1, 128 * j:128 * j + 128] += dnwl[j]
        shared["dsm_ssd"] = dsm
        ddtb_ref[0:1, :] += ddtb
        dalog_ref[0:1, :] += dalog
        ddpar_ref[0:1, :] += ddpar
        ranges = ([(128 * j, 128 * j + 128) for j in range(8)] + [(1024 + 128 * g, 1152 + 128 * g) for g in range(2)]
                  + [(1280 + 128 * g, 1408 + 128 * g) for g in range(2)])
        _conv_bwd(dxs + db + dc, ranges, dbuf, carry, xbc_ref, cw_ref, dxbc_ref, dcw_ref, dcb_ref, i == 0)

    rblk = lambda w: pl.BlockSpec((CHUNK, w), lambda i: (nc - 1 - i, 0))
    acc = lambda w: pl.BlockSpec((8, w), lambda i: (0, 0))
    f32 = jnp.float32
    return dict(
        body=body,
        in_specs=[rblk(1024), rblk(1536), rblk(1536), rblk(128),
                  pl.BlockSpec((1, 128, 1024), lambda i: (nc - 1 - i, 0, 0)), rblk(1024),
                  _full((CONV_K, 1536)), _full((1, 128)), _full((1, 128)), _full((1, 128)),
                  _full((1, 1024)), _full((64, 64)), _full((64, 128)), _full((64, 128)),
                  _full((1, 128))],
        out_specs=[rblk(1024), rblk(1536), acc(1536), acc(1536), acc(128), acc(128), acc(128), acc(1024)],
        out_shape=[jax.ShapeDtypeStruct((t, 1024), f32), jax.ShapeDtypeStruct((t, 1536), f32),
                   jax.ShapeDtypeStruct((8, 1536), f32),
                   jax.ShapeDtypeStruct((8, 1536), f32), jax.ShapeDtypeStruct((8, 128), f32),
                   jax.ShapeDtypeStruct((8, 128), f32), jax.ShapeDtypeStruct((8, 128), f32),
                   jax.ShapeDtypeStruct((8, 1024), f32)],
        scratch=[pltpu.VMEM((72, 1536), f32), pltpu.VMEM((8, 1536), f32), pltpu.VMEM((128, 1024), f32)],
        args=[z, xbc, pre, sm, hs, dy, conv_w, dtb, alog, dpar, nw, cs["tri"], cs["i2"], cs["mask2"], cs["lo"]])


def _gdn_split(pre_fn, gate_ref):
    def heads(base):
        return jnp.stack([pre_fn(base + 128 * h, base + 128 * h + 128) for h in range(GDN_HEADS)])
    gate = jnp.stack([gate_ref[:, 128 * h:128 * h + 128] for h in range(GDN_HEADS)])
    return heads(0), heads(1024), heads(2048), gate


def gdn_fwd(gate, qkv, sm, conv_w, dtb, alog, nw, cs):
    t = gate.shape[0]
    nc = t // CHUNK

    def body(shared, gate_ref, qkv_ref, halo_ref, sm_ref, cw_ref, dtb_ref, alog_ref, nw_ref,
             tri_ref, i64_ref, strict_ref, o_ref, ss_ref, ts_ref, pre_ref, pbuf, s_scr):
        i = pl.program_id(0)

        @pl.when(i == 0)
        def _():
            s_scr[...] = jnp.zeros_like(s_scr)

        pbuf[0:8, :] = jnp.where(i == 0, 0.0, halo_ref[...])
        pbuf[8:72, :] = qkv_ref[...]

        def pre_fn(c0, c1):
            pre = _conv_fwd(pbuf, cw_ref, c0, c1)
            pre_ref[:, c0:c1] = pre
            return pre

        q_pre, k_pre, v_pre, g3 = _gdn_split(pre_fn, gate_ref)
        s = s_scr[...]
        ss_ref[0] = s
        out, s_next, tinv = _gdn_chunk(q_pre, k_pre, v_pre, g3, sm_ref[...], s, dtb_ref[...], alog_ref[...],
                                       nw_ref[...], tri_ref[...], i64_ref[...], strict_ref[...])
        ts_ref[0] = tinv
        s_scr[...] = s_next
        for h in range(GDN_HEADS):
            o_ref[:, 128 * h:128 * h + 128] = out[h].astype(o_ref.dtype)

    blk = lambda w: pl.BlockSpec((CHUNK, w), lambda i: (i, 0))
    return dict(
        body=body,
        in_specs=[blk(1024), blk(3072), _halo_spec(3072, lambda i: i), blk(128),
                  _full((CONV_K, 3072)), _full((1, 128)), _full((1, 128)), _full((1, 128)),
                  _full((64, 64)), _full((64, 64)), _full((64, 64))],
        out_specs=[blk(1024), pl.BlockSpec((1, 8, 128, 128), lambda i: (i, 0, 0, 0)),
                   pl.BlockSpec((1, 8, CHUNK, CHUNK), lambda i: (i, 0, 0, 0)), blk(3072)],
        out_shape=[jax.ShapeDtypeStruct((t, 1024), _MM), jax.ShapeDtypeStruct((nc, 8, 128, 128), jnp.float32),
                   jax.ShapeDtypeStruct((nc, 8, CHUNK, CHUNK), jnp.float32),
                   jax.ShapeDtypeStruct((t, 3072), jnp.float32)],
        scratch=[pltpu.VMEM((72, 3072), jnp.float32), pltpu.VMEM((8, 128, 128), jnp.float32)],
        args=[gate, qkv, qkv, sm, conv_w, dtb, alog, nw, cs["tri"], cs["i64"], cs["strict"]])


def gdn_bwd(gate, qkv, pre, sm, ss, ts, do, conv_w, dtb, alog, nw, cs):
    t = gate.shape[0]
    nc = t // CHUNK

    def body(shared, gate_ref, qkv_ref, pre_ref, sm_ref, ss_ref, ts_ref, do_ref, cw_ref, dtb_ref, alog_ref,
             nw_ref, tri_ref, i64_ref, strict_ref,
             dgate_ref, dqkv_ref, dsm_ref, dcw_ref, ddtb_ref, dalog_ref, dnw_ref,
             dbuf, carry, ds_scr):
        i = pl.program_id(0)

        @pl.when(i == 0)
        def _():
            ds_scr[...] = jnp.zeros_like(ds_scr)
            dcw_ref[...] = jnp.zeros_like(dcw_ref)
            ddtb_ref[...] = jnp.zeros_like(ddtb_ref)
            dalog_ref[...] = jnp.zeros_like(dalog_ref)
            dnw_ref[...] = jnp.zeros_like(dnw_ref)

        q_pre, k_pre, v_pre, g3 = _gdn_split(lambda c0, c1: pre_ref[:, c0:c1], gate_ref)
        consts = (tri_ref[...], i64_ref[...], strict_ref[...], ts_ref[0])

        def f(q_pre, k_pre, v_pre, g3, smv, s, dtb, alog, nwv):
            return _gdn_chunk(q_pre, k_pre, v_pre, g3, smv, s, dtb, alog, nwv, *consts)[:2]

        _, vjp = jax.vjp(f, q_pre, k_pre, v_pre, g3, sm_ref[...], ss_ref[0], dtb_ref[...], alog_ref[...], nw_ref[...])
        do3 = jnp.stack([do_ref[:, 128 * h:128 * h + 128] for h in range(GDN_HEADS)])
        dq, dk, dv, dg3, dsm, ds, ddtb, dalog, dnw = vjp((do3, ds_scr[...]))
        ds_scr[...] = ds
        for h in range(GDN_HEADS):
            dgate_ref[:, 128 * h:128 * h + 128] = dg3[h]
        dsm_ref[...] = dsm + shared["dsm_ssd"]
        ddtb_ref[0:1, :] += ddtb
        dalog_ref[0:1, :] += dalog
        dnw_ref[0:1, :] += dnw
        ranges = [(base + 128 * h, base + 128 * h + 128) for base in (0, 1024, 2048) for h in range(GDN_HEADS)]
        dlist = [d[h] for d in (dq, dk, dv) for h in range(GDN_HEADS)]
        _conv_bwd(dlist, ranges, dbuf, carry, qkv_ref, cw_ref, dqkv_ref, dcw_ref, None, i == 0)

    rblk = lambda w: pl.BlockSpec((CHUNK, w), lambda i: (nc - 1 - i, 0))
    acc = lambda w: pl.BlockSpec((8, w), lambda i: (0, 0))
    f32 = jnp.float32
    return dict(
        body=body,
        in_specs=[rblk(1024), rblk(3072), rblk(3072), rblk(128),
                  pl.BlockSpec((1, 8, 128, 128), lambda i: (nc - 1 - i, 0, 0, 0)),
                  pl.BlockSpec((1, 8, CHUNK, CHUNK), lambda i: (nc - 1 - i, 0, 0, 0)), rblk(1024),
                  _full((CONV_K, 3072)), _full((1, 128)), _full((1, 128)), _full((1, 128)),
                  _full((64, 64)), _full((64, 64)), _full((64, 64))],
        out_specs=[rblk(1024), rblk(3072), rblk(128), acc(3072), acc(128), acc(128), acc(128)],
        out_shape=[jax.ShapeDtypeStruct((t, 1024), f32), jax.ShapeDtypeStruct((t, 3072), f32),
                   jax.ShapeDtypeStruct((t, 128), f32), jax.ShapeDtypeStruct((8, 3072), f32),
                   jax.ShapeDtypeStruct((8, 128), f32), jax.ShapeDtypeStruct((8, 128), f32),
                   jax.ShapeDtypeStruct((8, 128), f32)],
        scratch=[pltpu.VMEM((72, 3072), f32), pltpu.VMEM((8, 3072), f32), pltpu.VMEM((8, 128, 128), f32)],
        args=[gate, qkv, pre, sm, ss, ts, do, conv_w, dtb, alog, nw, cs["tri"], cs["i64"], cs["strict"]])


def _chunk_call(parts, name, nc):
    n_in = [len(p["args"]) for p in parts]
    n_out = [len(p["out_shape"]) for p in parts]
    n_scr = [len(p["scratch"]) for p in parts]

    def body(*refs):
        ins, outs, scr = refs[:sum(n_in)], refs[sum(n_in):sum(n_in) + sum(n_out)], refs[sum(n_in) + sum(n_out):]
        shared = {}
        for k, p in enumerate(parts):
            i0, o0, s0 = sum(n_in[:k]), sum(n_out[:k]), sum(n_scr[:k])
            p["body"](shared, *ins[i0:i0 + n_in[k]], *outs[o0:o0 + n_out[k]], *scr[s0:s0 + n_scr[k]])

    cat = lambda key: [v for p in parts for v in p[key]]
    return _pc(body, name=name, grid=(nc,), in_specs=cat("in_specs"), out_specs=cat("out_specs"),
               out_shape=cat("out_shape"), scratch_shapes=cat("scratch"),
               compiler_params=_cparams(("arbitrary",)))(*cat("args"))


def out_fwd_bwd(x, tgt, y_ssd, y_gdn, w_out, fnw):
    t = x.shape[0]
    tm = min(512, t)
    f32 = jnp.float32

    def body(x_ref, tgt_ref, ys_ref, yg_ref, w_ref, fnw_ref,
             dout_ref, dys_ref, dyg_ref, gw_ref, gfnw_ref, loss_ref, gw_acc):
        i = pl.program_id(0)

        @pl.when(i == 0)
        def _():
            gw_acc[...] = jnp.zeros_like(gw_acc)
            gfnw_ref[...] = jnp.zeros_like(gfnw_ref)
            loss_ref[...] = jnp.zeros_like(loss_ref)

        ys = ys_ref[...]
        yg = yg_ref[...]
        out = x_ref[...] + jnp.dot(ys, w_ref[0:1024, :], preferred_element_type=f32) \
            + jnp.dot(yg, w_ref[1024:2048, :], preferred_element_type=f32)
        rstd = lax.rsqrt(jnp.mean(out * out, axis=-1, keepdims=True) + EPS)
        yhat = out * rstd
        fw = fnw_ref[...]
        e = yhat * fw - tgt_ref[...]
        loss_ref[...] += 0.5 * jnp.sum(jnp.sum(e * e, axis=-1, keepdims=True) * (1.0 / D_MODEL), axis=0, keepdims=True)
        dyf = e * (1.0 / D_MODEL)
        gfnw_ref[0:1, :] += jnp.sum(dyf * yhat, axis=0, keepdims=True)
        dyhat = dyf * fw
        dout = rstd * (dyhat - yhat * jnp.mean(dyhat * yhat, axis=-1, keepdims=True))
        dout_ref[...] = dout
        db = dout.astype(_MM)
        dys_ref[...] = lax.dot_general(db, w_ref[0:1024, :], (((1,), (1,)), ((), ())), preferred_element_type=f32)
        dyg_ref[...] = lax.dot_general(db, w_ref[1024:2048, :], (((1,), (1,)), ((), ())), preferred_element_type=f32)
        gw_acc[0:1024, :] += lax.dot_general(ys, db, (((0,), (0,)), ((), ())), preferred_element_type=f32)
        gw_acc[1024:2048, :] += lax.dot_general(yg, db, (((0,), (0,)), ((), ())), preferred_element_type=f32)

        @pl.when(i == steps - 1)
        def _():
            gw_ref[...] = gw_acc[...].astype(gw_ref.dtype)

    steps = t // tm
    blk = pl.BlockSpec((tm, D_MODEL), lambda i: (i, 0))
    return _pc(
        body, name="out_fwd_bwd", grid=(steps,),
        in_specs=[blk, blk, blk, blk, _full((MIX_WIDTH, D_MODEL)), _full((1, D_MODEL))],
        out_specs=[blk, blk, blk, _full((MIX_WIDTH, D_MODEL)), _full((8, D_MODEL)), _full((1, 128))],
        out_shape=[jax.ShapeDtypeStruct((t, D_MODEL), f32)] * 3 +
                  [jax.ShapeDtypeStruct((MIX_WIDTH, D_MODEL), _MM), jax.ShapeDtypeStruct((8, D_MODEL), f32),
                   jax.ShapeDtypeStruct((1, 128), f32)],
        scratch_shapes=[pltpu.VMEM((MIX_WIDTH, D_MODEL), f32)],
        compiler_params=_cparams(("arbitrary",)),
    )(x, tgt, y_ssd, y_gdn, w_out, fnw)


def inproj_bwd_dx(x, dout, norm_w, w_perm, dgroups, scattered):
    t = x.shape[0]
    tm = min(256, t)
    f32 = jnp.float32

    def body(x_ref, dout_ref, nw_ref, w_ref, dz_ref, dxbc_ref, dgate_ref, dqkv_ref, dsm_ref, dx_ref, gnw_ref):
        i = pl.program_id(0)

        @pl.when(i == 0)
        def _():
            gnw_ref[...] = jnp.zeros_like(gnw_ref)

        du = None
        for (name, c0, c1), d_ref in zip(GROUPS, (dz_ref, dxbc_ref, dgate_ref, dqkv_ref, dsm_ref)):
            term = jnp.dot(d_ref[...].astype(_MM), w_ref[c0:c1, :], preferred_element_type=f32)
            du = term if du is None else du + term
        xf = x_ref[...]
        rstd = lax.rsqrt(jnp.mean(xf * xf, axis=-1, keepdims=True) + EPS)
        xhat = xf * rstd
        gnw_ref[0:1, :] += jnp.sum(du * xhat, axis=0, keepdims=True)
        dxh = du * nw_ref[...]
        dx_ref[...] = dout_ref[...] + rstd * (dxh - xhat * jnp.mean(dxh * xhat, axis=-1, keepdims=True))

    blk = lambda w: pl.BlockSpec((tm, w), lambda i: (i, 0))
    steps = t // tm
    kinds = ["scatter"] * len(scattered)
    hosted = _hosting(body, 9, 2, 0, kinds, lambda: pl.program_id(0) == 0, lambda: pl.program_id(0) == steps - 1)
    return _pc_comm(
        hosted, name="inproj_bwd_dx", grid=(steps,),
        in_specs=[blk(D_MODEL), blk(D_MODEL), _full((1, D_MODEL)), _full((PERM_DIM, D_MODEL))] +
                 [blk(c1 - c0) for _, c0, c1 in GROUPS] + [ANY] * len(scattered),
        out_specs=[blk(D_MODEL), _full((8, D_MODEL))] + [ANY] * len(scattered),
        out_shape=[jax.ShapeDtypeStruct((t, D_MODEL), f32), jax.ShapeDtypeStruct((8, D_MODEL), f32)] +
                  [_exchange_out_shape("scatter", a) for a in scattered],
        scratch_shapes=_exchange_sems(len(scattered)), compiler_params=_cparams(("arbitrary",)),
    )(x, dout, norm_w, w_perm, *dgroups, *scattered)


def grad_w_group(u, dg, name, scattered=()):
    t, n = dg.shape
    tn = 512 if n % 512 == 0 else n
    tm = 1024 if t % 1024 == 0 else t
    nj, nk = n // tn, t // tm
    f32 = jnp.float32

    def body(u_ref, d_ref, o_ref, acc):
        k = pl.program_id(1)

        @pl.when(k == 0)
        def _():
            acc[...] = jnp.zeros_like(acc)

        acc[...] += lax.dot_general(d_ref[...].astype(_MM), u_ref[...], (((0,), (0,)), ((), ())),
                                    preferred_element_type=f32)

        @pl.when(k == nk - 1)
        def _():
            o_ref[...] = acc[...].astype(o_ref.dtype)

    ne = len(scattered)
    hosted = _hosting(body, 2, 1, 1, ["scatter"] * ne,
                      lambda: (pl.program_id(0) == 0) & (pl.program_id(1) == 0),
                      lambda: (pl.program_id(0) == nj - 1) & (pl.program_id(1) == nk - 1))
    res = (_pc_comm if ne else _pc)(
        hosted, name=name, grid=(nj, nk),
        in_specs=[pl.BlockSpec((tm, D_MODEL), lambda j, k: (k, 0)),
                  pl.BlockSpec((tm, tn), lambda j, k: (k, j))] + [ANY] * ne,
        out_specs=[pl.BlockSpec((tn, D_MODEL), lambda j, k: (j, 0))] + [ANY] * ne,
        out_shape=[jax.ShapeDtypeStruct((n, D_MODEL), _MM)] + [_exchange_out_shape("scatter", a) for a in scattered],
        scratch_shapes=[pltpu.VMEM((tn, D_MODEL), f32)] + _exchange_sems(ne),
        compiler_params=_cparams(("arbitrary", "arbitrary")),
    )(u, dg, *scattered)
    return res if ne else res[0]


def _pad_lanes(v, off):
    n = v.shape[-1]
    return jnp.pad(v.reshape(1, n).astype(jnp.float32), ((0, 0), (off, 128 - off - n)))


REF_ROWS = dict(z=(0, 1024), xbc=(1024, 2560), dt=(2560, 2576), gate=(2576, 3600), qkv=(3600, 6672), ab=(6672, 6688))


def perm_w_in(gathered):
    def rows(a, b):
        out = []
        while a < b:
            k, r = divmod(a, W_IN_SHARD)
            n = min(b - a, W_IN_SHARD - r)
            out.append(gathered[k, r:r + n])
            a += n
        return out

    pieces = [p for name in ("z", "xbc", "gate", "qkv", "dt", "ab") for p in rows(*REF_ROWS[name])]
    pieces.append(jnp.zeros((PERM_DIM - IN_DIM, gathered.shape[2]), gathered.dtype))
    return jnp.concatenate(pieces, axis=0)


def unperm_w_in(gz, gxbc, ggate, gqkv, gsm):
    src = dict(z=gz, xbc=gxbc, dt=gsm[0:16], gate=ggate, qkv=gqkv, ab=gsm[16:32])
    slabs = []
    for k in range(N_DEV):
        a, b = k * W_IN_SHARD, (k + 1) * W_IN_SHARD
        parts = []
        for name, (s, e) in REF_ROWS.items():
            lo, hi = max(a, s), min(b, e)
            if lo < hi:
                parts.append(src[name][lo - s:hi - s])
        slabs.append(jnp.concatenate(parts, axis=0))
    return jnp.stack(slabs)


def all_gather(arrs, name):
    n = len(arrs)

    def body(*refs):
        ins, outs = refs[:n], refs[n:2 * n]
        send_sems, recv_sems, local_sems = refs[2 * n:]
        x, y, c, me = _me()
        sibling = (x, y, 1 - c)
        chips = [(1 - x, y), (x, 1 - y), (1 - x, 1 - y)]

        def idx(px, py, pc):
            return 4 * px + 2 * py + pc

        def copy(a, k, block, to, src=None):
            slot = outs[a].at[idx(*block)]
            return pltpu.make_async_remote_copy(src_ref=slot if src is None else src, dst_ref=slot,
                                                send_sem=send_sems.at[a, k], recv_sem=recv_sems.at[a, k],
                                                device_id=to, device_id_type=MESH)

        local = [pltpu.make_async_copy(ins[a], outs[a].at[me], local_sems.at[a]) for a in range(n)]
        for cp in local:
            cp.start()
        started = []
        for a in range(n):
            first = [copy(a, 0, (x, y, c), sibling, src=ins[a])]
            first += [copy(a, 1 + j, (x, y, c), (*chip, c), src=ins[a]) for j, chip in enumerate(chips)]
            for cp in first:
                cp.start()
            started += first
        for a in range(n):
            for j, chip in enumerate(chips):
                copy(a, 1 + j, (*chip, c), (x, y, c)).wait_recv()
                fwd = copy(a, 4 + j, (*chip, c), sibling)
                fwd.start()
                started.append(fwd)
        for a in range(n):
            copy(a, 0, sibling, (x, y, c)).wait_recv()
            for j, chip in enumerate(chips):
                copy(a, 4 + j, (*chip, 1 - c), (x, y, c)).wait_recv()
        for cp in started:
            cp.wait_send()
        for cp in local:
            cp.wait()

    return _pc_comm(
        body, name=name, in_specs=[ANY] * n, out_specs=[ANY] * n,
        out_shape=[jax.ShapeDtypeStruct((N_DEV,) + a.shape, a.dtype) for a in arrs],
        scratch_shapes=[pltpu.SemaphoreType.DMA((n, 7)), pltpu.SemaphoreType.DMA((n, 7)),
                        pltpu.SemaphoreType.DMA((n,))],
    )(*arrs)


def adamw_sum(recv, w, m, v, rows, name, cols=None):
    r, ccols = w.shape
    f32 = jnp.float32
    c1 = 1.0 / (1.0 - ADAM_B1 ** ADAM_STEP)
    c2 = 1.0 / (1.0 - ADAM_B2 ** ADAM_STEP)

    def body(recv_ref, w_ref, m_ref, v_ref, g_ref, d_ref, mo_ref, vo_ref):
        g = recv_ref[0].astype(f32)
        for k in range(1, N_DEV):
            g = g + recv_ref[k].astype(f32)
        mn = ADAM_B1 * m_ref[...] + (1.0 - ADAM_B1) * g
        vn = ADAM_B2 * v_ref[...] + (1.0 - ADAM_B2) * (g * g)
        g_ref[...] = g
        mo_ref[...] = mn
        vo_ref[...] = vn
        d_ref[...] = -ADAM_LR * ((mn * c1) / (jnp.sqrt(vn * c2) + ADAM_EPS) + ADAM_WD * w_ref[...])

    if cols is None:
        blk = pl.BlockSpec((rows, ccols), lambda i: (i, 0))
        rblk, steps = pl.BlockSpec((N_DEV, rows, ccols), lambda i: (0, i, 0)), r // rows
    else:
        blk = pl.BlockSpec((r, cols), lambda i: (0, i))
        rblk, steps = pl.BlockSpec((N_DEV, r, cols), lambda i: (0, 0, i)), ccols // cols
    return _pc(
        body, name=name, grid=(steps,),
        in_specs=[rblk, blk, blk, blk],
        out_specs=[blk] * 4, out_shape=[jax.ShapeDtypeStruct((r, ccols), f32)] * 4,
        compiler_params=_cparams(("arbitrary",)),
    )(recv, w, m, v)


SMALL = (("norm_w", 1, 1024, 0), ("ssd_conv_b", 1, 1536, 0), ("ssd_dt_bias", 1, 16, 0), ("ssd_a_log", 1, 16, 0),
         ("ssd_d", 1, 16, 0), ("ssd_norm_w", 1, 1024, 0), ("gdn_dt_bias", 1, 8, 16), ("gdn_a_log", 1, 8, 16),
         ("gdn_norm_w", 1, 128, 0), ("final_norm_w", 1, 1024, 0),
         ("ssd_conv_w", CONV_K, SSD_CONV_DIM // N_DEV, 0), ("gdn_conv_w", CONV_K, GDN_CONV_DIM // N_DEV, 0))


def _small_layout():
    out, off = [], 0
    for name, rows, n, lane0 in SMALL:
        stride = -(-(lane0 + n) // 128) * 128
        out.append((name, rows, n, lane0, stride, off))
        off += rows * stride
    return out, off


def scatter_small(accs):
    layout, total = _small_layout()
    f32 = jnp.float32

    def body(*refs):
        acc_refs, out_ref, slabs = refs[:len(layout)], refs[len(layout)], refs[len(layout) + 1]
        sems = refs[len(layout) + 2:]
        slabs[...] = jnp.zeros_like(slabs)
        for (name, rows, n, lane0, stride, off), acc in zip(layout, acc_refs):
            for k in range(N_DEV):
                if rows == 1:
                    slabs[k, :, off:off + stride] = acc[0:1, 0:stride]
                else:
                    for j in range(rows):
                        slabs[k, :, off + stride * j:off + stride * j + n] = acc[j:j + 1, n * k:n * k + n]
        start, wait = _exchange_ops("scatter", slabs, out_ref, *sems)
        start()
        wait()

    return _pc_comm(
        body, name="scatter_small_grads", out_specs=ANY, out_shape=jax.ShapeDtypeStruct((N_DEV, 1, total), f32),
        scratch_shapes=[pltpu.VMEM((N_DEV, 1, total), f32)] + _exchange_sems(1),
    )(*accs)


def adamw_small(recv, w, m, v):
    layout, total = _small_layout()
    f32 = jnp.float32
    c1 = 1.0 / (1.0 - ADAM_B1 ** ADAM_STEP)
    c2 = 1.0 / (1.0 - ADAM_B2 ** ADAM_STEP)
    np_ = len(layout)

    def body(*refs):
        recv_ref = refs[0]
        w_refs, m_refs, v_refs = refs[1:1 + np_], refs[1 + np_:1 + 2 * np_], refs[1 + 2 * np_:1 + 3 * np_]
        o_refs = refs[1 + 3 * np_:]
        g_all = recv_ref[0]
        for k in range(1, N_DEV):
            g_all = g_all + recv_ref[k]

        def update(g, wv, mv, vv):
            mn = ADAM_B1 * mv + (1.0 - ADAM_B1) * g
            vn = ADAM_B2 * vv + (1.0 - ADAM_B2) * (g * g)
            return g, -ADAM_LR * ((mn * c1) / (jnp.sqrt(vn * c2) + ADAM_EPS) + ADAM_WD * wv), mn, vn

        for p, (name, rows, n, lane0, stride, off) in enumerate(layout):
            outs = o_refs[4 * p:4 * p + 4]
            if rows == 1:
                res = update(g_all[:, off + lane0:off + lane0 + n], w_refs[p][...], m_refs[p][...], v_refs[p][...])
                for o, r in zip(outs, res):
                    o[...] = r
            else:
                for j in range(rows):
                    res = update(g_all[:, off + stride * j:off + stride * j + n], w_refs[p][0, j:j + 1, :],
                                 m_refs[p][0, j:j + 1, :], v_refs[p][0, j:j + 1, :])
                    for o, r in zip(outs, res):
                        o[0, j:j + 1, :] = r

    names = [e[0] for e in layout]
    ins = [recv] + [d[nm] for d in (w, m, v) for nm in names]
    out_shape = [jax.ShapeDtypeStruct(w[nm].shape, f32) for nm in names for _ in range(4)]
    res = _pc(body, name="adamw_small", out_shape=out_shape)(*ins)
    return {nm: tuple(res[4 * p:4 * p + 4]) for p, nm in enumerate(names)}


SHARD = (("ssd_conv_w", CONV_K * SSD_CONV_DIM // N_DEV), ("gdn_conv_w", CONV_K * GDN_CONV_DIM // N_DEV))
SHARD_ROWS = 24


def _rows_of(size):
    return -(-size // 128)


def _pack(vals, layout, total_rows):
    parts = []
    for (name, size), val in zip(layout, vals):
        flat = val.reshape(-1).astype(jnp.float32)
        parts.append(jnp.pad(flat, (0, _rows_of(size) * 128 - size)).reshape(-1, 128))
    used = sum(_rows_of(s) for _, s in layout)
    parts.append(jnp.zeros((total_rows - used, 128), jnp.float32))
    return jnp.concatenate(parts, axis=0)


def _conv_full(gathered_flat, ccols):
    return gathered_flat.reshape(N_DEV, CONV_K, ccols // N_DEV).transpose(1, 0, 2).reshape(CONV_K, ccols)


def kernel(x, norm_w, w_in, ssd_conv_w, ssd_conv_b, ssd_dt_bias, ssd_a_log, ssd_d, ssd_norm_w, gdn_conv_w, gdn_dt_bias, gdn_a_log, gdn_norm_w, w_out, final_norm_w, loss_target, m_norm_w, m_w_in, m_ssd_conv_w, m_ssd_conv_b, m_ssd_dt_bias, m_ssd_a_log, m_ssd_d, m_ssd_norm_w, m_gdn_conv_w, m_gdn_dt_bias, m_gdn_a_log, m_gdn_norm_w, m_w_out, m_final_norm_w, v_norm_w, v_w_in, v_ssd_conv_w, v_ssd_conv_b, v_ssd_dt_bias, v_ssd_a_log, v_ssd_d, v_ssd_norm_w, v_gdn_conv_w, v_gdn_dt_bias, v_gdn_a_log, v_gdn_norm_w, v_w_out, v_final_norm_w):
    f32 = jnp.float32
    w = dict(norm_w=norm_w, w_in=w_in, ssd_conv_w=ssd_conv_w, ssd_conv_b=ssd_conv_b, ssd_dt_bias=ssd_dt_bias,
             ssd_a_log=ssd_a_log, ssd_d=ssd_d, ssd_norm_w=ssd_norm_w, gdn_conv_w=gdn_conv_w, gdn_dt_bias=gdn_dt_bias,
             gdn_a_log=gdn_a_log, gdn_norm_w=gdn_norm_w, w_out=w_out, final_norm_w=final_norm_w)
    m = dict(norm_w=m_norm_w, w_in=m_w_in, ssd_conv_w=m_ssd_conv_w, ssd_conv_b=m_ssd_conv_b, ssd_dt_bias=m_ssd_dt_bias,
             ssd_a_log=m_ssd_a_log, ssd_d=m_ssd_d, ssd_norm_w=m_ssd_norm_w, gdn_conv_w=m_gdn_conv_w,
             gdn_dt_bias=m_gdn_dt_bias, gdn_a_log=m_gdn_a_log, gdn_norm_w=m_gdn_norm_w, w_out=m_w_out,
             final_norm_w=m_final_norm_w)
    v = dict(norm_w=v_norm_w, w_in=v_w_in, ssd_conv_w=v_ssd_conv_w, ssd_conv_b=v_ssd_conv_b, ssd_dt_bias=v_ssd_dt_bias,
             ssd_a_log=v_ssd_a_log, ssd_d=v_ssd_d, ssd_norm_w=v_ssd_norm_w, gdn_conv_w=v_gdn_conv_w,
             gdn_dt_bias=v_gdn_dt_bias, gdn_a_log=v_gdn_a_log, gdn_norm_w=v_gdn_norm_w, w_out=v_w_out,
             final_norm_w=v_final_norm_w)
    names = list(w)
    shapes = {n: w[n].shape for n in names}

    xl, tgt = x[0], loss_target[0]
    cs = _consts()
    dtb_s = _pad_lanes(ssd_dt_bias, 0)
    alog_s = _pad_lanes(ssd_a_log, 0)
    dpar = _pad_lanes(ssd_d, 0)
    dtb_g = _pad_lanes(gdn_dt_bias, 16)
    alog_g = _pad_lanes(gdn_a_log, 16)
    nw_g = gdn_norm_w.reshape(1, 128)
    nw_s = ssd_norm_w.reshape(1, 1024)
    cb_s = ssd_conv_b.reshape(1, 1536)
    nw1 = norm_w.reshape(1, D_MODEL)

    (g_w_in,) = all_gather([w_in[0].T.astype(_MM)], "gather_w_in")
    w_perm = perm_w_in(g_w_in)
    conv_pack = _pack([w["ssd_conv_w"], w["gdn_conv_w"]], SHARD, SHARD_ROWS)
    u, z, xbc, gate, qkv, sm, g_w_out, g_conv = inproj_fwd(xl, nw1, w_perm, [w_out[0].astype(_MM), conv_pack])
    w_out_full = g_w_out.reshape(MIX_WIDTH, D_MODEL)
    ssd_cw = _conv_full(g_conv[:, 0:6].reshape(N_DEV, -1), SSD_CONV_DIM)
    gdn_cw = _conv_full(g_conv[:, 6:18].reshape(N_DEV, -1), GDN_CONV_DIM)

    nc = xl.shape[0] // CHUNK
    y_ssd, hs, pre_s, y_gdn, ss, ts, pre_g = _chunk_call(
        [ssd_fwd(z, xbc, sm, ssd_cw, cb_s, dtb_s, alog_s, dpar, nw_s, cs),
         gdn_fwd(gate, qkv, sm, gdn_cw, dtb_g, alog_g, nw_g, cs)], "scan_fwd", nc)
    dout, dys, dyg, g_wout, g_fnw, loss_l = out_fwd_bwd(xl, tgt, y_ssd, y_gdn, w_out_full,
                                                        final_norm_w.reshape(1, D_MODEL))
    (dz, dxbc, g_cw_s, g_cb_s, g_dtb_s, g_alog_s, g_d, g_nw_s,
     dgate, dqkv, dsm, g_cw_g, g_dtb_g, g_alog_g, g_nw_g) = _chunk_call(
        [ssd_bwd(z, xbc, pre_s, sm, hs, dys, ssd_cw, dtb_s, alog_s, dpar, nw_s, cs),
         gdn_bwd(gate, qkv, pre_g, sm, ss, ts, dyg, gdn_cw, dtb_g, alog_g, nw_g, cs)], "scan_bwd", nc)

    t_w_out = g_wout.reshape(N_DEV, MIX_WIDTH // N_DEV, D_MODEL)
    gws = {}
    for dg, (name, _, _) in zip((dz, dxbc, dgate, dsm), (GROUPS[0], GROUPS[1], GROUPS[2], GROUPS[4])):
        gws[name] = grad_w_group(u, dg, "grad_w_in_" + name)
    gws["qkv"], r_w_out = grad_w_group(u, dqkv, "grad_w_in_qkv", [t_w_out])
    t_w_in = unperm_w_in(gws["z"], gws["xbc"], gws["gate"], gws["qkv"], gws["sm"])
    dx, g_nw, r_w_in = inproj_bwd_dx(xl, dout, nw1, w_perm, (dz, dxbc, dgate, dqkv, dsm), [t_w_in])

    accs = dict(norm_w=g_nw, ssd_conv_b=g_cb_s, ssd_dt_bias=g_dtb_s, ssd_a_log=g_alog_s, ssd_d=g_d,
                ssd_norm_w=g_nw_s, gdn_dt_bias=g_dtb_g, gdn_a_log=g_alog_g, gdn_norm_w=g_nw_g, final_norm_w=g_fnw,
                ssd_conv_w=g_cw_s, gdn_conv_w=g_cw_g)
    r_small = scatter_small([accs[e[0]] for e in SMALL])

    o_w_in = adamw_sum(r_w_in, w_in[0].T, m_w_in[0].T, v_w_in[0].T, None, "adamw_w_in", cols=256)
    o_w_out = adamw_sum(r_w_out, w_out[0], m_w_out[0], v_w_out[0], 64, "adamw_w_out")
    row = lambda d: {n: (a.reshape(1, -1) if a.ndim == 1 else a) for n, a in d.items()}
    o_small = adamw_small(r_small, row(w), row(m), row(v))

    loss = lax.psum(loss_l[0, 0], ("x", "y", "c"))
    outs = [loss, dx[None]]
    for k in range(4):
        parts = {n: o_small[n][k] for n in o_small}
        parts["w_in"] = o_w_in[k].T
        parts["w_out"] = o_w_out[k]
        outs += [parts[n].reshape(shapes[n]) for n in names]
    return tuple(outs)
```

```python
import functools

import jax
import jax.numpy as jnp
import numpy as np
from jax import lax
from jax.experimental import pallas as pl
from jax.experimental.pallas import tpu as pltpu

_MM = jnp.bfloat16

D_MODEL = 1024
CHUNK = 64
CONV_K = 4
EPS = 1e-6
SSD_CONV_DIM = 1536
GDN_HEADS = 8
GDN_DK = 128
GDN_CONV_DIM = 3072
MIX_WIDTH = 2048
IN_DIM = 6688
N_DEV = 8
W_IN_SHARD = IN_DIM // N_DEV
PERM_DIM = 6784
HI = lax.Precision.HIGHEST
HIGH = lax.Precision.HIGH
VMEM_LIMIT = 56 * 1024 * 1024

ADAM_LR = 0.001
ADAM_B1 = 0.9
ADAM_B2 = 0.999
ADAM_EPS = 1e-08
ADAM_WD = 0.01
ADAM_STEP = 10


def _pc(body, **kw):
    return pl.pallas_call(body, **kw)


def _pc_comm(body, **kw):
    return pl.pallas_call(body, **kw)


def _cparams(sem):
    return pltpu.CompilerParams(dimension_semantics=sem, vmem_limit_bytes=VMEM_LIMIT)


def _sig(x):
    return 0.5 * jnp.tanh(0.5 * x) + 0.5


@jax.custom_vjp
def _sigmoid(x):
    return _sig(x)


def _sigmoid_fwd(x):
    s = _sig(x)
    return s, s


def _sigmoid_bwd(s, g):
    return (g * s * (1.0 - s),)


_sigmoid.defvjp(_sigmoid_fwd, _sigmoid_bwd)


@jax.custom_vjp
def _silu(x):
    return x * _sig(x)


def _silu_fwd(x):
    s = _sig(x)
    return x * s, (x, s)


def _silu_bwd(res, g):
    x, s = res
    return (g * (s * (1.0 + x * (1.0 - s))),)


_silu.defvjp(_silu_fwd, _silu_bwd)


def _softplus_impl(x):
    return jnp.maximum(x, 0.0) + jnp.log(1.0 + jnp.exp(-jnp.abs(x)))


@jax.custom_vjp
def _softplus(x):
    return _softplus_impl(x)


def _softplus_fwd(x):
    return _softplus_impl(x), x


def _softplus_bwd(x, g):
    return (g * _sig(x),)


_softplus.defvjp(_softplus_fwd, _softplus_bwd)


def _lane_bcast_impl(x, k):
    return jnp.broadcast_to(x[..., k:k + 1], x.shape)


@functools.partial(jax.custom_vjp, nondiff_argnums=(1,))
def _lane_bcast(x, k):
    return _lane_bcast_impl(x, k)


def _lane_bcast_fwd(x, k):
    return _lane_bcast_impl(x, k), None


def _lane_bcast_bwd(k, _, g):
    lane = lax.broadcasted_iota(jnp.int32, g.shape, g.ndim - 1)
    return (jnp.where(lane == k, jnp.sum(g, axis=-1, keepdims=True), 0.0),)


_lane_bcast.defvjp(_lane_bcast_fwd, _lane_bcast_bwd)


def _mm(a, b):
    return jnp.dot(a.astype(_MM), b.astype(_MM), preferred_element_type=jnp.float32)


def _mm_nt(a, b):
    return lax.dot_general(a.astype(_MM), b.astype(_MM), (((1,), (1,)), ((), ())),
                           preferred_element_type=jnp.float32)


def _mm_tn(a, b):
    return lax.dot_general(a.astype(_MM), b.astype(_MM), (((0,), (0,)), ((), ())),
                           preferred_element_type=jnp.float32)


def _dot_hi(a, b):
    return jnp.dot(a, b, precision=HI, preferred_element_type=jnp.float32)


def _bmm(a, b):
    return lax.dot_general(a.astype(_MM), b.astype(_MM), (((2,), (1,)), ((0,), (0,))),
                           preferred_element_type=jnp.float32)


def _bmm_nt(a, b):
    return lax.dot_general(a.astype(_MM), b.astype(_MM), (((2,), (2,)), ((0,), (0,))),
                           preferred_element_type=jnp.float32)


def _bmm_tn(a, b):
    return lax.dot_general(a.astype(_MM), b.astype(_MM), (((1,), (1,)), ((0,), (0,))),
                           preferred_element_type=jnp.float32)


def _bmm_hi(a, b):
    return lax.dot_general(a, b, (((2,), (1,)), ((0,), (0,))), precision=HIGH, preferred_element_type=jnp.float32)


def _bmm_nt_hi(a, b):
    return lax.dot_general(a, b, (((2,), (2,)), ((0,), (0,))), precision=HIGH, preferred_element_type=jnp.float32)


def _bmm_tn_hi(a, b):
    return lax.dot_general(a, b, (((1,), (1,)), ((0,), (0,))), precision=HIGH, preferred_element_type=jnp.float32)


def _consts():
    l = np.arange(CHUNK)
    tri = (l[:, None] >= l[None, :]).astype(np.float32)
    lane = np.arange(128)
    i2 =(l[:, None] == (lane[None, :] % 64)).astype(np.float32)
    mask2 = (l[:, None] >= (lane[None, :] % 64)).astype(np.float32)
    lo = (lane < 64).astype(np.float32)[None, :]
    i64 = np.eye(CHUNK, dtype=np.float32)
    strict = (l[:, None] > l[None, :]).astype(np.float32)
    return dict(tri=jnp.asarray(tri), i2=jnp.asarray(i2), mask2=jnp.asarray(mask2), lo=jnp.asarray(lo),
                i64=jnp.asarray(i64), strict=jnp.asarray(strict))


def _ssd_chunk(xs_pre, b_pre, c_pre, z, sm, ht, dtb, alog, dpar, nw, tri, i2, mask2, lo):
    lane = lax.broadcasted_iota(jnp.int32, (1, 128), 1)
    m16 = lane < 16
    dt = jnp.where(m16, _softplus(sm + dtb), 0.0)
    a_neg = -jnp.exp(alog)
    cum = _dot_hi(tri, dt * a_neg)
    row = lax.broadcasted_iota(jnp.int32, (CHUNK, 1), 0)
    is_last = row == CHUNK - 1
    hi = 1.0 - lo
    bm = [_silu(b) for b in b_pre]
    cm = [_silu(c) for c in c_pre]
    cb2 = [_mm_nt(cm[g], jnp.concatenate([bm[g], bm[g]], axis=0)) for g in range(2)]
    yg, ht_next = [], []
    for j in range(8):
        g = j // 4
        pair = lambda v, j=j: jnp.where(lo > 0.5, _lane_bcast(v, 2 * j), _lane_bcast(v, 2 * j + 1))
        xs = _silu(xs_pre[j])
        dte = pair(dt)
        cume = pair(cum)
        cum_last = jnp.sum(jnp.where(is_last, cume, 0.0), axis=0, keepdims=True)
        xdt = xs * dte
        rowv = jnp.sum(cume * i2, axis=0, keepdims=True)
        lm = jnp.exp(jnp.where(mask2 > 0.5, cume - rowv, -jnp.inf))
        m = cb2[g] * lm
        xblk = jnp.concatenate([xdt * lo, xdt * hi], axis=0)
        y = _mm(m, xblk)
        y = y + _mm(cm[g], ht[j]) * jnp.exp(cume)
        y = y + pair(dpar) * xs
        yg.append(y * _silu(z[j]))
        st = _mm_tn(bm[g], xdt * jnp.exp(cum_last - cume))
        ht_next.append(ht[j] * jnp.exp(cum_last) + st)
    outs = []
    for g in range(2):
        ss = sum(jnp.sum(yg[j] * yg[j], axis=-1, keepdims=True) for j in range(4 * g, 4 * g + 4))
        rs = lax.rsqrt(ss * (1.0 / 512.0) + EPS)
        for j in range(4 * g, 4 * g + 4):
            outs.append(yg[j] * rs * nw[j])
    return outs, ht_next


def _tri_inverse(a):
    eye = jnp.eye(CHUNK, dtype=jnp.float32)[None]
    p = eye - a
    ap = a
    for _ in range(5):
        ap = _bmm_hi(ap, ap)
        p = p + _bmm_hi(p, ap)
    return p


@jax.custom_vjp
def _solve(a, r1, r2, t):
    return _bmm_hi(t, r1), _bmm_hi(t, r2)


def _solve_fwd(a, r1, r2, t):
    u, w = _bmm_hi(t, r1), _bmm_hi(t, r2)
    return (u, w), (t, u, w)


def _solve_bwd(res, cts):
    t, u, w = res
    du, dw = cts
    dr1 = _bmm_tn_hi(t, du)
    dr2 = _bmm_tn_hi(t, dw)
    da = -(_bmm_nt_hi(dr1, u) + _bmm_nt_hi(dr2, w))
    return da, dr1, dr2, jnp.zeros_like(t)


_solve.defvjp(_solve_fwd, _solve_bwd)


def _gdn_chunk(q_pre, k_pre, v_pre, gate, sm, s, dtb, alog, nw, tri, i64, strict, t_in=None):
    lane = lax.broadcasted_iota(jnp.int32, (1, 128), 1)
    m_a = (lane >= 16) & (lane < 24)
    g_full = jnp.where(m_a, -jnp.exp(alog) * _softplus(sm + dtb), 0.0)
    gc = _dot_hi(tri, g_full)
    sig = _sigmoid(sm)
    gc3 = jnp.stack([_lane_bcast(gc, 16 + h) for h in range(GDN_HEADS)])
    beta3 = jnp.stack([_lane_bcast(sig, 24 + h) for h in range(GDN_HEADS)])
    q = _silu(q_pre)
    q = q * lax.rsqrt(jnp.sum(q * q, axis=-1, keepdims=True) + EPS) * (GDN_DK ** -0.5)
    k = _silu(k_pre)
    k = k * lax.rsqrt(jnp.sum(k * k, axis=-1, keepdims=True) + EPS)
    v = _silu(v_pre)
    gcl = gc3[:, :, :CHUNK]
    gc_row = jnp.sum(gcl * i64[None], axis=1, keepdims=True)
    incl = (strict + i64)[None] > 0.5
    decay = jnp.exp(jnp.where(incl, gcl - gc_row, -jnp.inf))
    kb = k * beta3
    a = jnp.where(strict[None] > 0.5, _bmm_nt(kb, k) * decay, 0.0)
    egc = jnp.exp(gc3)
    t = _tri_inverse(a) if t_in is None else t_in
    u, w = _solve(a, v * beta3, kb * egc, t)
    attn = _bmm_nt(q, k) * decay
    row = lax.broadcasted_iota(jnp.int32, (1, CHUNK, 1), 1)
    gl = jnp.sum(jnp.where(row == CHUNK - 1, gc3, 0.0), axis=1, keepdims=True)
    q_dec = q * egc
    k_dec = k * jnp.exp(gl - gc3)
    v_new = u - _bmm(w, s)
    o = _bmm(q_dec, s) + _bmm(attn, v_new)
    s_next = s * jnp.exp(gl) + _bmm_tn(k_dec, v_new)
    on = o * lax.rsqrt(jnp.mean(o * o, axis=-1, keepdims=True) + EPS) * nw
    return on * _silu(gate), s_next, t


def _conv_fwd(pbuf, w_ref, c0, c1):
    acc = None
    for j in range(CONV_K):
        term = w_ref[j:j + 1, c0:c1] * pbuf[5 + j:69 + j, c0:c1]
        acc = term if acc is None else acc + term
    return acc


MESH = pl.DeviceIdType.MESH
ANY = pl.BlockSpec(memory_space=pl.ANY)


def _me():
    x, y, c = lax.axis_index("x"), lax.axis_index("y"), lax.axis_index("c")
    return x, y, c, 4 * x + 2 * y + c


def _peer(r):
    x, y, c, _ = _me()
    px = 1 - x if r & 4 else x
    py = 1 - y if r & 2 else y
    pc = 1 - c if r & 1 else c
    return (px, py, pc), 4 * px + 2 * py + pc


def _exchange_ops(kind, in_ref, out_ref, send_sems, recv_sems, local_sem):
    me = _me()[3]
    local = pltpu.make_async_copy(in_ref.at[me] if kind == "scatter" else in_ref, out_ref.at[me], local_sem)
    sends, recvs = [], []
    for r in range(1, N_DEV):
        peer, pidx = _peer(r)
        src = in_ref.at[pidx] if kind == "scatter" else in_ref
        sems = dict(send_sem=send_sems.at[r - 1], recv_sem=recv_sems.at[r - 1], device_id=peer, device_id_type=MESH)
        sends.append(pltpu.make_async_remote_copy(src_ref=src, dst_ref=out_ref.at[me], **sems))
        recvs.append(pltpu.make_async_remote_copy(src_ref=src, dst_ref=out_ref.at[pidx], **sems))

    def start():
        local.start()
        for cp in sends:
            cp.start()

    def wait():
        for cp in recvs:
            cp.wait_recv()
        for cp in sends:
            cp.wait_send()
        local.wait()

    return start, wait


def _exchange_sems(n):
    return [pltpu.SemaphoreType.DMA((N_DEV - 1,)), pltpu.SemaphoreType.DMA((N_DEV - 1,)),
            pltpu.SemaphoreType.DMA(())] * n


def _exchange_out_shape(kind, a):
    return jax.ShapeDtypeStruct(a.shape if kind == "scatter" else (N_DEV,) + a.shape, a.dtype)


def _hosting(body, n_in, n_out, n_scratch, kinds, first, last):
    ne = len(kinds)

    def wrapped(*refs):
        ins, ex_in = refs[:n_in], refs[n_in:n_in + ne]
        o0 = n_in + ne
        outs, ex_out = refs[o0:o0 + n_out], refs[o0 + n_out:o0 + n_out + ne]
        s0 = o0 + n_out + ne
        scr, sems = refs[s0:s0 + n_scratch], refs[s0 + n_scratch:]
        ops = [_exchange_ops(kinds[e], ex_in[e], ex_out[e], *sems[3 * e:3 * e + 3]) for e in range(ne)]

        @pl.when(first())
        def _():
            for start, _ in ops:
                start()

        body(*ins, *outs, *scr)

        @pl.when(last())
        def _():
            for _, wait in ops:
                wait()

    return wrapped


GROUPS = (("z", 0, 1024), ("xbc", 1024, 2560), ("gate", 2560, 3584), ("qkv", 3584, 6656), ("sm", 6656, 6784))


def inproj_fwd(x, norm_w, w_perm, gathered):
    t = x.shape[0]
    tm = min(256, t)
    steps = t // tm
    kinds = ["gather"] * len(gathered)

    def body(x_ref, nw_ref, w_ref, u_ref, z_ref, xbc_ref, gate_ref, qkv_ref, sm_ref):
        xf = x_ref[...]
        rstd = lax.rsqrt(jnp.mean(xf * xf, axis=-1, keepdims=True) + EPS)
        u = (xf * rstd * nw_ref[...]).astype(_MM)
        u_ref[...] = u
        for (name, c0, c1), o_ref in zip(GROUPS, (z_ref, xbc_ref, gate_ref, qkv_ref, sm_ref)):
            o_ref[...] = lax.dot_general(u, w_ref[c0:c1, :], (((1,), (1,)), ((), ())),
                                         preferred_element_type=jnp.float32)

    outs = [jax.ShapeDtypeStruct((t, D_MODEL), _MM)] + [jax.ShapeDtypeStruct((t, c1 - c0), jnp.float32)
                                                        for _, c0, c1 in GROUPS]
    hosted = _hosting(body, 3, 6, 0, kinds, lambda: pl.program_id(0) == 0, lambda: pl.program_id(0) == steps - 1)
    return _pc_comm(
        hosted, name="inproj_fwd", grid=(steps,),
        in_specs=[pl.BlockSpec((tm, D_MODEL), lambda i: (i, 0)),
                  pl.BlockSpec((1, D_MODEL), lambda i: (0, 0)),
                  pl.BlockSpec((PERM_DIM, D_MODEL), lambda i: (0, 0))] + [ANY] * len(gathered),
        out_specs=[pl.BlockSpec((tm, D_MODEL), lambda i: (i, 0))] +
                  [pl.BlockSpec((tm, c1 - c0), lambda i: (i, 0)) for _, c0, c1 in GROUPS] + [ANY] * len(gathered),
        out_shape=outs + [_exchange_out_shape("gather", a) for a in gathered],
        scratch_shapes=_exchange_sems(len(gathered)), compiler_params=_cparams(("arbitrary",)),
    )(x, norm_w, w_perm, *gathered)


def _halo_spec(width, idx_fn):
    return pl.BlockSpec((8, width), lambda i: (jnp.maximum(idx_fn(i) * 8 - 1, 0), 0))


def _full(shape):
    nd = len(shape)
    return pl.BlockSpec(shape, lambda i: (0,) * nd)


def _ssd_split(pre_fn, z_ref, sm_ref):
    xs_pre = [pre_fn(128 * j, 128 * j + 128) for j in range(8)]
    b_pre = [pre_fn(1024 + 128 * g, 1152 + 128 * g) for g in range(2)]
    c_pre = [pre_fn(1280 + 128 * g, 1408 + 128 * g) for g in range(2)]
    z = [z_ref[:, 128 * j:128 * j + 128] for j in range(8)]
    return xs_pre, b_pre, c_pre, z, sm_ref[...]


def ssd_fwd(z, xbc, sm, conv_w, conv_b, dtb, alog, dpar, nw, cs):
    t = z.shape[0]
    nc = t // CHUNK

    def body(shared, z_ref, xbc_ref, halo_ref, sm_ref, cw_ref, cb_ref, dtb_ref, alog_ref, dpar_ref, nw_ref,
             tri_ref, i2_ref, mask2_ref, lo_ref, y_ref, hs_ref, pre_ref, pbuf, ht_scr):
        i = pl.program_id(0)

        @pl.when(i == 0)
        def _():
            ht_scr[...] = jnp.zeros_like(ht_scr)

        pbuf[0:8, :] = jnp.where(i == 0, 0.0, halo_ref[...])
        pbuf[8:72, :] = xbc_ref[...]

        def pre_fn(c0, c1):
            pre = _conv_fwd(pbuf, cw_ref, c0, c1) + cb_ref[:, c0:c1]
            pre_ref[:, c0:c1] = pre
            return pre

        xs_pre, b_pre, c_pre, zz, smv = _ssd_split(pre_fn, z_ref, sm_ref)
        ht = [ht_scr[:, 128 * j:128 * j + 128] for j in range(8)]
        hs_ref[0] = ht_scr[...]
        nwl = [nw_ref[:, 128 * j:128 * j + 128] for j in range(8)]
        outs, ht_next = _ssd_chunk(xs_pre, b_pre, c_pre, zz, smv, ht, dtb_ref[...], alog_ref[...], dpar_ref[...],
                                   nwl, tri_ref[...], i2_ref[...], mask2_ref[...], lo_ref[...])
        for j in range(8):
            y_ref[:, 128 * j:128 * j + 128] = outs[j].astype(y_ref.dtype)
            ht_scr[:, 128 * j:128 * j + 128] = ht_next[j]

    blk = lambda w: pl.BlockSpec((CHUNK, w), lambda i: (i, 0))
    return dict(
        body=body,
        in_specs=[blk(1024), blk(1536), _halo_spec(1536, lambda i: i), blk(128),
                  _full((CONV_K, 1536)), _full((1, 1536)), _full((1, 128)), _full((1, 128)), _full((1, 128)),
                  _full((1, 1024)), _full((64, 64)), _full((64, 128)), _full((64, 128)),
                  _full((1, 128))],
        out_specs=[blk(1024), pl.BlockSpec((1, 128, 1024), lambda i: (i, 0, 0)), blk(1536)],
        out_shape=[jax.ShapeDtypeStruct((t, 1024), _MM), jax.ShapeDtypeStruct((nc, 128, 1024), jnp.float32),
                   jax.ShapeDtypeStruct((t, 1536), jnp.float32)],
        scratch=[pltpu.VMEM((72, 1536), jnp.float32), pltpu.VMEM((128, 1024), jnp.float32)],
        args=[z, xbc, xbc, sm, conv_w, conv_b, dtb, alog, dpar, nw, cs["tri"], cs["i2"], cs["mask2"], cs["lo"]])


def _conv_bwd(dpre_list, col_ranges, dbuf, carry, x_ref, cw_ref, dx_ref, dcw_ref, dcb_ref, first):
    for dpre, (c0, c1) in zip(dpre_list, col_ranges):
        dbuf[0:64, c0:c1] = dpre
    dbuf[64:72, :] = jnp.where(first, 0.0, carry[...])
    carry[...] = dbuf[0:8, :]
    for (c0, c1) in col_ranges:
        xin = x_ref[:, c0:c1]
        acc = None
        for j in range(CONV_K):
            sh = dbuf[3 - j:67 - j, c0:c1]
            term = cw_ref[j:j + 1, c0:c1] * sh
            acc = term if acc is None else acc + term
            dcw_ref[j:j + 1, c0:c1] += jnp.sum(xin * sh, axis=0, keepdims=True)
        dx_ref[:, c0:c1] = acc.astype(dx_ref.dtype)
        if dcb_ref is not None:
            dcb_ref[0:1, c0:c1] += jnp.sum(dbuf[0:64, c0:c1], axis=0, keepdims=True)


def ssd_bwd(z, xbc, pre, sm, hs, dy, conv_w, dtb, alog, dpar, nw, cs):
    t = z.shape[0]
    nc = t // CHUNK

    def body(shared, z_ref, xbc_ref, pre_ref, sm_ref, hs_ref, dy_ref, cw_ref, dtb_ref, alog_ref, dpar_ref, nw_ref,
             tri_ref, i2_ref, mask2_ref, lo_ref,
             dz_ref, dxbc_ref, dcw_ref, dcb_ref, ddtb_ref, dalog_ref, ddpar_ref, dnw_ref,
             dbuf, carry, dht_scr):
        i = pl.program_id(0)

        @pl.when(i == 0)
        def _():
            dht_scr[...] = jnp.zeros_like(dht_scr)
            dcw_ref[...] = jnp.zeros_like(dcw_ref)
            dcb_ref[...] = jnp.zeros_like(dcb_ref)
            ddtb_ref[...] = jnp.zeros_like(ddtb_ref)
            dalog_ref[...] = jnp.zeros_like(dalog_ref)
            ddpar_ref[...] = jnp.zeros_like(ddpar_ref)
            dnw_ref[...] = jnp.zeros_like(dnw_ref)

        pre_fn = lambda c0, c1: pre_ref[:, c0:c1]
        xs_pre, b_pre, c_pre, zz, smv = _ssd_split(pre_fn, z_ref, sm_ref)
        ht = [hs_ref[0, :, 128 * j:128 * j + 128] for j in range(8)]
        nwl = [nw_ref[:, 128 * j:128 * j + 128] for j in range(8)]
        consts = (tri_ref[...], i2_ref[...], mask2_ref[...], lo_ref[...])

        def f(xs_pre, b_pre, c_pre, zz, smv, ht, dtb, alog, dpar, nwl):
            return _ssd_chunk(xs_pre, b_pre, c_pre, zz, smv, ht, dtb, alog, dpar, nwl, *consts)

        _, vjp = jax.vjp(f, xs_pre, b_pre, c_pre, zz, smv, ht, dtb_ref[...], alog_ref[...], dpar_ref[...], nwl)
        dys = [dy_ref[:, 128 * j:128 * j + 128] for j in range(8)]
        dhts = [dht_scr[:, 128 * j:128 * j + 128] for j in range(8)]
        dxs, db, dc, dzz, dsm, dht, ddtb, dalog, ddpar, dnwl = vjp((dys, dhts))
        for j in range(8):
            dz_ref[:, 128 * j:128 * j + 128] = dzz[j].astype(dz_ref.dtype)
            dht_scr[:, 128 * j:128 * j + 128] = dht[j]
            dnw_ref[0:1, 128 * j:128 * j + 128] += dnwl[j]
        shared["dsm_ssd"] = dsm
        ddtb_ref[0:1, :] += ddtb
        dalog_ref[0:1, :] += dalog
        ddpar_ref[0:1, :] += ddpar
        ranges = ([(128 * j, 128 * j + 128) for j in range(8)] + [(1024 + 128 * g, 1152 + 128 * g) for g in range(2)]
                  + [(1280 + 128 * g, 1408 + 128 * g) for g in range(2)])
        _conv_bwd(dxs + db + dc, ranges, dbuf, carry, xbc_ref, cw_ref, dxbc_ref, dcw_ref, dcb_ref, i == 0)

    rblk = lambda w: pl.BlockSpec((CHUNK, w), lambda i: (nc - 1 - i, 0))
    acc = lambda w: pl.BlockSpec((8, w), lambda i: (0, 0))
    f32 = jnp.float32
    return dict(
        body=body,
        in_specs=[rblk(1024), rblk(1536), rblk(1536), rblk(128),
                  pl.BlockSpec((1, 128, 1024), lambda i: (nc - 1 - i, 0, 0)), rblk(1024),
                  _full((CONV_K, 1536)), _full((1, 128)), _full((1, 128)), _full((1, 128)),
                  _full((1, 1024)), _full((64, 64)), _full((64, 128)), _full((64, 128)),
                  _full((1, 128))],
        out_specs=[rblk(1024), rblk(1536), acc(1536), acc(1536), acc(128), acc(128), acc(128), acc(1024)],
        out_shape=[jax.ShapeDtypeStruct((t, 1024), _MM), jax.ShapeDtypeStruct((t, 1536), _MM),
                   jax.ShapeDtypeStruct((8, 1536), f32),
                   jax.ShapeDtypeStruct((8, 1536), f32), jax.ShapeDtypeStruct((8, 128), f32),
                   jax.ShapeDtypeStruct((8, 128), f32), jax.ShapeDtypeStruct((8, 128), f32),
                   jax.ShapeDtypeStruct((8, 1024), f32)],
        scratch=[pltpu.VMEM((72, 1536), f32), pltpu.VMEM((8, 1536), f32), pltpu.VMEM((128, 1024), f32)],
        args=[z, xbc, pre, sm, hs, dy, conv_w, dtb, alog, dpar, nw, cs["tri"], cs["i2"], cs["mask2"], cs["lo"]])


def _gdn_split(pre_fn, gate_ref):
    def heads(base):
        return jnp.stack([pre_fn(base + 128 * h, base + 128 * h + 128) for h in range(GDN_HEADS)])
    gate = jnp.stack([gate_ref[:, 128 * h:128 * h + 128] for h in range(GDN_HEADS)])
    return heads(0), heads(1024), heads(2048), gate


def gdn_fwd(gate, qkv, sm, conv_w, dtb, alog, nw, cs):
    t = gate.shape[0]
    nc = t // CHUNK

    def body(shared, gate_ref, qkv_ref, halo_ref, sm_ref, cw_ref, dtb_ref, alog_ref, nw_ref,
             tri_ref, i64_ref, strict_ref, o_ref, ss_ref, ts_ref, pre_ref, pbuf, s_scr):
        i = pl.program_id(0)

        @pl.when(i == 0)
        def _():
            s_scr[...] = jnp.zeros_like(s_scr)

        pbuf[0:8, :] = jnp.where(i == 0, 0.0, halo_ref[...])
        pbuf[8:72, :] = qkv_ref[...]

        def pre_fn(c0, c1):
            pre = _conv_fwd(pbuf, cw_ref, c0, c1)
            pre_ref[:, c0:c1] = pre
            return pre

        q_pre, k_pre, v_pre, g3 = _gdn_split(pre_fn, gate_ref)
        s = s_scr[...]
        ss_ref[0] = s
        out, s_next, tinv = _gdn_chunk(q_pre, k_pre, v_pre, g3, sm_ref[...], s, dtb_ref[...], alog_ref[...],
                                       nw_ref[...], tri_ref[...], i64_ref[...], strict_ref[...])
        ts_ref[0] = tinv
        s_scr[...] = s_next
        for h in range(GDN_HEADS):
            o_ref[:, 128 * h:128 * h + 128] = out[h].astype(o_ref.dtype)

    blk = lambda w: pl.BlockSpec((CHUNK, w), lambda i: (i, 0))
    return dict(
        body=body,
        in_specs=[blk(1024), blk(3072), _halo_spec(3072, lambda i: i), blk(128),
                  _full((CONV_K, 3072)), _full((1, 128)), _full((1, 128)), _full((1, 128)),
                  _full((64, 64)), _full((64, 64)), _full((64, 64))],
        out_specs=[blk(1024), pl.BlockSpec((1, 8, 128, 128), lambda i: (i, 0, 0, 0)),
                   pl.BlockSpec((1, 8, CHUNK, CHUNK), lambda i: (i, 0, 0, 0)), blk(3072)],
        out_shape=[jax.ShapeDtypeStruct((t, 1024), _MM), jax.ShapeDtypeStruct((nc, 8, 128, 128), jnp.float32),
                   jax.ShapeDtypeStruct((nc, 8, CHUNK, CHUNK), jnp.float32),
                   jax.ShapeDtypeStruct((t, 3072), jnp.float32)],
        scratch=[pltpu.VMEM((72, 3072), jnp.float32), pltpu.VMEM((8, 128, 128), jnp.float32)],
        args=[gate, qkv, qkv, sm, conv_w, dtb, alog, nw, cs["tri"], cs["i64"], cs["strict"]])


def gdn_bwd(gate, qkv, pre, sm, ss, ts, do, conv_w, dtb, alog, nw, cs):
    t = gate.shape[0]
    nc = t // CHUNK

    def body(shared, gate_ref, qkv_ref, pre_ref, sm_ref, ss_ref, ts_ref, do_ref, cw_ref, dtb_ref, alog_ref,
             nw_ref, tri_ref, i64_ref, strict_ref,
             dgate_ref, dqkv_ref, dsm_ref, dcw_ref, ddtb_ref, dalog_ref, dnw_ref,
             dbuf, carry, ds_scr):
        i = pl.program_id(0)

        @pl.when(i == 0)
        def _():
            ds_scr[...] = jnp.zeros_like(ds_scr)
            dcw_ref[...] = jnp.zeros_like(dcw_ref)
            ddtb_ref[...] = jnp.zeros_like(ddtb_ref)
            dalog_ref[...] = jnp.zeros_like(dalog_ref)
            dnw_ref[...] = jnp.zeros_like(dnw_ref)

        q_pre, k_pre, v_pre, g3 = _gdn_split(lambda c0, c1: pre_ref[:, c0:c1], gate_ref)
        consts = (tri_ref[...], i64_ref[...], strict_ref[...], ts_ref[0])

        def f(q_pre, k_pre, v_pre, g3, smv, s, dtb, alog, nwv):
            return _gdn_chunk(q_pre, k_pre, v_pre, g3, smv, s, dtb, alog, nwv, *consts)[:2]

        _, vjp = jax.vjp(f, q_pre, k_pre, v_pre, g3, sm_ref[...], ss_ref[0], dtb_ref[...], alog_ref[...], nw_ref[...])
        do3 = jnp.stack([do_ref[:, 128 * h:128 * h + 128] for h in range(GDN_HEADS)])
        dq, dk, dv, dg3, dsm, ds, ddtb, dalog, dnw = vjp((do3, ds_scr[...]))
        ds_scr[...] = ds
        for h in range(GDN_HEADS):
            dgate_ref[:, 128 * h:128 * h + 128] = dg3[h].astype(dgate_ref.dtype)
        dsm_ref[...] = (dsm + shared["dsm_ssd"]).astype(dsm_ref.dtype)
        ddtb_ref[0:1, :] += ddtb
        dalog_ref[0:1, :] += dalog
        dnw_ref[0:1, :] += dnw
        ranges = [(base + 128 * h, base + 128 * h + 128) for base in (0, 1024, 2048) for h in range(GDN_HEADS)]
        dlist = [d[h] for d in (dq, dk, dv) for h in range(GDN_HEADS)]
        _conv_bwd(dlist, ranges, dbuf, carry, qkv_ref, cw_ref, dqkv_ref, dcw_ref, None, i == 0)

    rblk = lambda w: pl.BlockSpec((CHUNK, w), lambda i: (nc - 1 - i, 0))
    acc = lambda w: pl.BlockSpec((8, w), lambda i: (0, 0))
    f32 = jnp.float32
    return dict(
        body=body,
        in_specs=[rblk(1024), rblk(3072), rblk(3072), rblk(128),
                  pl.BlockSpec((1, 8, 128, 128), lambda i: (nc - 1 - i, 0, 0, 0)),
                  pl.BlockSpec((1, 8, CHUNK, CHUNK), lambda i: (nc - 1 - i, 0, 0, 0)), rblk(1024),
                  _full((CONV_K, 3072)), _full((1, 128)), _full((1, 128)), _full((1, 128)),
                  _full((64, 64)), _full((64, 64)), _full((64, 64))],
        out_specs=[rblk(1024), rblk(3072), rblk(128), acc(3072), acc(128), acc(128), acc(128)],
        out_shape=[jax.ShapeDtypeStruct((t, 1024), _MM), jax.ShapeDtypeStruct((t, 3072), _MM),
                   jax.ShapeDtypeStruct((t, 128), _MM), jax.ShapeDtypeStruct((8, 3072), f32),
                   jax.ShapeDtypeStruct((8, 128), f32), jax.ShapeDtypeStruct((8, 128), f32),
                   jax.ShapeDtypeStruct((8, 128), f32)],
        scratch=[pltpu.VMEM((72, 3072), f32), pltpu.VMEM((8, 3072), f32), pltpu.VMEM((8, 128, 128), f32)],
        args=[gate, qkv, pre, sm, ss, ts, do, conv_w, dtb, alog, nw, cs["tri"], cs["i64"], cs["strict"]])


def _chunk_call(parts, name, nc):
    n_in = [len(p["args"]) for p in parts]
    n_out = [len(p["out_shape"]) for p in parts]
    n_scr = [len(p["scratch"]) for p in parts]

    def body(*refs):
        ins, outs, scr = refs[:sum(n_in)], refs[sum(n_in):sum(n_in) + sum(n_out)], refs[sum(n_in) + sum(n_out):]
        shared = {}
        for k, p in enumerate(parts):
            i0, o0, s0 = sum(n_in[:k]), sum(n_out[:k]), sum(n_scr[:k])
            p["body"](shared, *ins[i0:i0 + n_in[k]], *outs[o0:o0 + n_out[k]], *scr[s0:s0 + n_scr[k]])

    cat = lambda key: [v for p in parts for v in p[key]]
    return _pc(body, name=name, grid=(nc,), in_specs=cat("in_specs"), out_specs=cat("out_specs"),
               out_shape=cat("out_shape"), scratch_shapes=cat("scratch"),
               compiler_params=_cparams(("arbitrary",)))(*cat("args"))


def out_fwd_bwd(x, tgt, y_ssd, y_gdn, w_out, fnw):
    t = x.shape[0]
    tm = min(512, t)
    f32 = jnp.float32

    def body(x_ref, tgt_ref, ys_ref, yg_ref, w_ref, fnw_ref,
             dout_ref, dys_ref, dyg_ref, gw_ref, gfnw_ref, loss_ref, gw_acc):
        i = pl.program_id(0)

        @pl.when(i == 0)
        def _():
            gw_acc[...] = jnp.zeros_like(gw_acc)
            gfnw_ref[...] = jnp.zeros_like(gfnw_ref)
            loss_ref[...] = jnp.zeros_like(loss_ref)

        ys = ys_ref[...]
        yg = yg_ref[...]
        out = x_ref[...] + jnp.dot(ys, w_ref[0:1024, :], preferred_element_type=f32) \
            + jnp.dot(yg, w_ref[1024:2048, :], preferred_element_type=f32)
        rstd = lax.rsqrt(jnp.mean(out * out, axis=-1, keepdims=True) + EPS)
        yhat = out * rstd
        fw = fnw_ref[...]
        e = yhat * fw - tgt_ref[...]
        loss_ref[...] += 0.5 * jnp.sum(jnp.sum(e * e, axis=-1, keepdims=True) * (1.0 / D_MODEL), axis=0, keepdims=True)
        dyf = e * (1.0 / D_MODEL)
        gfnw_ref[0:1, :] += jnp.sum(dyf * yhat, axis=0, keepdims=True)
        dyhat = dyf * fw
        dout = rstd * (dyhat - yhat * jnp.mean(dyhat * yhat, axis=-1, keepdims=True))
        dout_ref[...] = dout
        db = dout.astype(_MM)
        dys_ref[...] = lax.dot_general(db, w_ref[0:1024, :], (((1,), (1,)), ((), ())), preferred_element_type=f32)
        dyg_ref[...] = lax.dot_general(db, w_ref[1024:2048, :], (((1,), (1,)), ((), ())), preferred_element_type=f32)
        gw_acc[0:1024, :] += lax.dot_general(ys, db, (((0,), (0,)), ((), ())), preferred_element_type=f32)
        gw_acc[1024:2048, :] += lax.dot_general(yg, db, (((0,), (0,)), ((), ())), preferred_element_type=f32)

        @pl.when(i == steps - 1)
        def _():
            gw_ref[...] = gw_acc[...].astype(gw_ref.dtype)

    steps = t // tm
    blk = pl.BlockSpec((tm, D_MODEL), lambda i: (i, 0))
    return _pc(
        body, name="out_fwd_bwd", grid=(steps,),
        in_specs=[blk, blk, blk, blk, _full((MIX_WIDTH, D_MODEL)), _full((1, D_MODEL))],
        out_specs=[blk, blk, blk, _full((MIX_WIDTH, D_MODEL)), _full((8, D_MODEL)), _full((1, 128))],
        out_shape=[jax.ShapeDtypeStruct((t, D_MODEL), f32)] * 3 +
                  [jax.ShapeDtypeStruct((MIX_WIDTH, D_MODEL), _MM), jax.ShapeDtypeStruct((8, D_MODEL), f32),
                   jax.ShapeDtypeStruct((1, 128), f32)],
        scratch_shapes=[pltpu.VMEM((MIX_WIDTH, D_MODEL), f32)],
        compiler_params=_cparams(("arbitrary",)),
    )(x, tgt, y_ssd, y_gdn, w_out, fnw)


def inproj_bwd_dx(x, dout, norm_w, w_perm, dgroups, scattered):
    t = x.shape[0]
    tm = min(256, t)
    f32 = jnp.float32

    def body(x_ref, dout_ref, nw_ref, w_ref, dz_ref, dxbc_ref, dgate_ref, dqkv_ref, dsm_ref, dx_ref, gnw_ref):
        i = pl.program_id(0)

        @pl.when(i == 0)
        def _():
            gnw_ref[...] = jnp.zeros_like(gnw_ref)

        du = None
        for (name, c0, c1), d_ref in zip(GROUPS, (dz_ref, dxbc_ref, dgate_ref, dqkv_ref, dsm_ref)):
            term = jnp.dot(d_ref[...].astype(_MM), w_ref[c0:c1, :], preferred_element_type=f32)
            du = term if du is None else du + term
        xf = x_ref[...]
        rstd = lax.rsqrt(jnp.mean(xf * xf, axis=-1, keepdims=True) + EPS)
        xhat = xf * rstd
        gnw_ref[0:1, :] += jnp.sum(du * xhat, axis=0, keepdims=True)
        dxh = du * nw_ref[...]
        dx_ref[...] = dout_ref[...] + rstd * (dxh - xhat * jnp.mean(dxh * xhat, axis=-1, keepdims=True))

    blk = lambda w: pl.BlockSpec((tm, w), lambda i: (i, 0))
    steps = t // tm
    kinds = ["scatter"] * len(scattered)
    hosted = _hosting(body, 9, 2, 0, kinds, lambda: pl.program_id(0) == 0, lambda: pl.program_id(0) == steps - 1)
    return _pc_comm(
        hosted, name="inproj_bwd_dx", grid=(steps,),
        in_specs=[blk(D_MODEL), blk(D_MODEL), _full((1, D_MODEL)), _full((PERM_DIM, D_MODEL))] +
                 [blk(c1 - c0) for _, c0, c1 in GROUPS] + [ANY] * len(scattered),
        out_specs=[blk(D_MODEL), _full((8, D_MODEL))] + [ANY] * len(scattered),
        out_shape=[jax.ShapeDtypeStruct((t, D_MODEL), f32), jax.ShapeDtypeStruct((8, D_MODEL), f32)] +
                  [_exchange_out_shape("scatter", a) for a in scattered],
        scratch_shapes=_exchange_sems(len(scattered)), compiler_params=_cparams(("arbitrary",)),
    )(x, dout, norm_w, w_perm, *dgroups, *scattered)


def grad_w_group(u, dg, name, scattered=()):
    t, n = dg.shape
    tn = 512 if n % 512 == 0 else n
    tm = 1024 if t % 1024 == 0 else t
    nj, nk = n // tn, t // tm
    f32 = jnp.float32

    def body(u_ref, d_ref, o_ref, acc):
        k = pl.program_id(1)

        @pl.when(k == 0)
        def _():
            acc[...] = jnp.zeros_like(acc)

        acc[...] += lax.dot_general(d_ref[...].astype(_MM), u_ref[...], (((0,), (0,)), ((), ())),
                                    preferred_element_type=f32)

        @pl.when(k == nk - 1)
        def _():
            o_ref[...] = acc[...].astype(o_ref.dtype)

    ne = len(scattered)
    hosted = _hosting(body, 2, 1, 1, ["scatter"] * ne,
                      lambda: (pl.program_id(0) == 0) & (pl.program_id(1) == 0),
                      lambda: (pl.program_id(0) == nj - 1) & (pl.program_id(1) == nk - 1))
    res = (_pc_comm if ne else _pc)(
        hosted, name=name, grid=(nj, nk),
        in_specs=[pl.BlockSpec((tm, D_MODEL), lambda j, k: (k, 0)),
                  pl.BlockSpec((tm, tn), lambda j, k: (k, j))] + [ANY] * ne,
        out_specs=[pl.BlockSpec((tn, D_MODEL), lambda j, k: (j, 0))] + [ANY] * ne,
        out_shape=[jax.ShapeDtypeStruct((n, D_MODEL), _MM)] + [_exchange_out_shape("scatter", a) for a in scattered],
        scratch_shapes=[pltpu.VMEM((tn, D_MODEL), f32)] + _exchange_sems(ne),
        compiler_params=_cparams(("arbitrary", "arbitrary")),
    )(u, dg, *scattered)
    return res if ne else res[0]


def _pad_lanes(v, off):
    n = v.shape[-1]
    return jnp.pad(v.reshape(1, n).astype(jnp.float32), ((0, 0), (off, 128 - off - n)))


REF_ROWS = dict(z=(0, 1024), xbc=(1024, 2560), dt=(2560, 2576), gate=(2576, 3600), qkv=(3600, 6672), ab=(6672, 6688))


def perm_w_in(gathered):
    wt = gathered.reshape(IN_DIM, gathered.shape[2])
    pieces = [wt[s:e] for s, e in (REF_ROWS[name] for name in ("z", "xbc", "gate", "qkv", "dt", "ab"))]
    pieces.append(jnp.zeros((PERM_DIM - IN_DIM, wt.shape[1]), wt.dtype))
    return jnp.concatenate(pieces, axis=0)


def unperm_w_in(gz, gxbc, ggate, gqkv, gsm):
    src = dict(z=gz, xbc=gxbc, dt=gsm[0:16], gate=ggate, qkv=gqkv, ab=gsm[16:32])
    slabs = []
    for k in range(N_DEV):
        a, b = k * W_IN_SHARD, (k + 1) * W_IN_SHARD
        parts = []
        for name, (s, e) in REF_ROWS.items():
            lo, hi = max(a, s), min(b, e)
            if lo < hi:
                parts.append(src[name][lo - s:hi - s])
        slabs.append(jnp.concatenate(parts, axis=0))
    return jnp.stack(slabs)


def all_gather(arrs, name):
    n = len(arrs)

    def body(*refs):
        ins, outs = refs[:n], refs[n:2 * n]
        send_sems, recv_sems, local_sems = refs[2 * n:]
        x, y, c, me = _me()
        sibling = (x, y, 1 - c)
        chips = [(1 - x, y), (x, 1 - y), (1 - x, 1 - y)]

        def idx(px, py, pc):
            return 4 * px + 2 * py + pc

        def copy(a, k, block, to, src=None):
            slot = outs[a].at[idx(*block)]
            return pltpu.make_async_remote_copy(src_ref=slot if src is None else src, dst_ref=slot,
                                                send_sem=send_sems.at[a, k], recv_sem=recv_sems.at[a, k],
                                                device_id=to, device_id_type=MESH)

        local = [pltpu.make_async_copy(ins[a], outs[a].at[me], local_sems.at[a]) for a in range(n)]
        for cp in local:
            cp.start()
        started = []
        for a in range(n):
            first = [copy(a, 0, (x, y, c), sibling, src=ins[a])]
            first += [copy(a, 1 + j, (x, y, c), (*chip, c), src=ins[a]) for j, chip in enumerate(chips)]
            for cp in first:
                cp.start()
            started += first
        for a in range(n):
            for j, chip in enumerate(chips):
                copy(a, 1 + j, (*chip, c), (x, y, c)).wait_recv()
                fwd = copy(a, 4 + j, (*chip, c), sibling)
                fwd.start()
                started.append(fwd)
        for a in range(n):
            copy(a, 0, sibling, (x, y, c)).wait_recv()
            for j, chip in enumerate(chips):
                copy(a, 4 + j, (*chip, 1 - c), (x, y, c)).wait_recv()
        for cp in started:
            cp.wait_send()
        for cp in local:
            cp.wait()

    return _pc_comm(
        body, name=name, in_specs=[ANY] * n, out_specs=[ANY] * n,
        out_shape=[jax.ShapeDtypeStruct((N_DEV,) + a.shape, a.dtype) for a in arrs],
        scratch_shapes=[pltpu.SemaphoreType.DMA((n, 7)), pltpu.SemaphoreType.DMA((n, 7)),
                        pltpu.SemaphoreType.DMA((n,))],
    )(*arrs)


def adamw_sum(recv, w, m, v, rows, name, cols=None):
    r, ccols = w.shape
    f32 = jnp.float32
    c1 = 1.0 / (1.0 - ADAM_B1 ** ADAM_STEP)
    c2 = 1.0 / (1.0 - ADAM_B2 ** ADAM_STEP)

    def body(recv_ref, w_ref, m_ref, v_ref, g_ref, d_ref, mo_ref, vo_ref):
        g = recv_ref[0].astype(f32)
        for k in range(1, N_DEV):
            g = g + recv_ref[k].astype(f32)
        mn = ADAM_B1 * m_ref[...] + (1.0 - ADAM_B1) * g
        vn = ADAM_B2 * v_ref[...] + (1.0 - ADAM_B2) * (g * g)
        g_ref[...] = g
        mo_ref[...] = mn
        vo_ref[...] = vn
        d_ref[...] = -ADAM_LR * ((mn * c1) / (jnp.sqrt(vn * c2) + ADAM_EPS) + ADAM_WD * w_ref[...])

    if cols is None:
        blk = pl.BlockSpec((rows, ccols), lambda i: (i, 0))
        rblk, steps = pl.BlockSpec((N_DEV, rows, ccols), lambda i: (0, i, 0)), r // rows
    else:
        blk = pl.BlockSpec((r, cols), lambda i: (0, i))
        rblk, steps = pl.BlockSpec((N_DEV, r, cols), lambda i: (0, 0, i)), ccols // cols
    return _pc(
        body, name=name, grid=(steps,),
        in_specs=[rblk, blk, blk, blk],
        out_specs=[blk] * 4, out_shape=[jax.ShapeDtypeStruct((r, ccols), f32)] * 4,
        compiler_params=_cparams(("arbitrary",)),
    )(recv, w, m, v)


SMALL = (("norm_w", 1, 1024, 0), ("ssd_conv_b", 1, 1536, 0), ("ssd_dt_bias", 1, 16, 0), ("ssd_a_log", 1, 16, 0),
         ("ssd_d", 1, 16, 0), ("ssd_norm_w", 1, 1024, 0), ("gdn_dt_bias", 1, 8, 16), ("gdn_a_log", 1, 8, 16),
         ("gdn_norm_w", 1, 128, 0), ("final_norm_w", 1, 1024, 0),
         ("ssd_conv_w", CONV_K, SSD_CONV_DIM // N_DEV, 0), ("gdn_conv_w", CONV_K, GDN_CONV_DIM // N_DEV, 0))


def _small_layout():
    out, off = [], 0
    for name, rows, n, lane0 in SMALL:
        stride = -(-(lane0 + n) // 128) * 128
        out.append((name, rows, n, lane0, stride, off))
        off += rows * stride
    return out, off


def scatter_small(accs):
    layout, total = _small_layout()
    f32 = jnp.float32

    def body(*refs):
        acc_refs, out_ref, slabs = refs[:len(layout)], refs[len(layout)], refs[len(layout) + 1]
        sems = refs[len(layout) + 2:]
        slabs[...] = jnp.zeros_like(slabs)
        for (name, rows, n, lane0, stride, off), acc in zip(layout, acc_refs):
            for k in range(N_DEV):
                if rows == 1:
                    slabs[k, :, off:off + stride] = acc[0:1, 0:stride]
                else:
                    for j in range(rows):
                        slabs[k, :, off + stride * j:off + stride * j + n] = acc[j:j + 1, n * k:n * k + n]
        start, wait = _exchange_ops("scatter", slabs, out_ref, *sems)
        start()
        wait()

    return _pc_comm(
        body, name="scatter_small_grads", out_specs=ANY, out_shape=jax.ShapeDtypeStruct((N_DEV, 1, total), f32),
        scratch_shapes=[pltpu.VMEM((N_DEV, 1, total), f32)] + _exchange_sems(1),
    )(*accs)


def adamw_small(recv, w, m, v):
    layout, total = _small_layout()
    f32 = jnp.float32
    c1 = 1.0 / (1.0 - ADAM_B1 ** ADAM_STEP)
    c2 = 1.0 / (1.0 - ADAM_B2 ** ADAM_STEP)
    np_ = len(layout)

    def body(*refs):
        recv_ref = refs[0]
        w_refs, m_refs, v_refs = refs[1:1 + np_], refs[1 + np_:1 + 2 * np_], refs[1 + 2 * np_:1 + 3 * np_]
        o_refs = refs[1 + 3 * np_:]
        g_all = recv_ref[0]
        for k in range(1, N_DEV):
            g_all = g_all + recv_ref[k]

        def update(g, wv, mv, vv):
            mn = ADAM_B1 * mv + (1.0 - ADAM_B1) * g
            vn = ADAM_B2 * vv + (1.0 - ADAM_B2) * (g * g)
            return g, -ADAM_LR * ((mn * c1) / (jnp.sqrt(vn * c2) + ADAM_EPS) + ADAM_WD * wv), mn, vn

        for p, (name, rows, n, lane0, stride, off) in enumerate(layout):
            outs = o_refs[4 * p:4 * p + 4]
            if rows == 1:
                res = update(g_all[:, off + lane0:off + lane0 + n], w_refs[p][...], m_refs[p][...], v_refs[p][...])
                for o, r in zip(outs, res):
                    o[...] = r
            else:
                for j in range(rows):
                    res = update(g_all[:, off + stride * j:off + stride * j + n], w_refs[p][0, j:j + 1, :],
                                 m_refs[p][0, j:j + 1, :], v_refs[p][0, j:j + 1, :])
                    for o, r in zip(outs, res):
                        o[0, j:j + 1, :] = r

    names = [e[0] for e in layout]
    ins = [recv] + [d[nm] for d in (w, m, v) for nm in names]
    out_shape = [jax.ShapeDtypeStruct(w[nm].shape, f32) for nm in names for _ in range(4)]
    res = _pc(body, name="adamw_small", out_shape=out_shape)(*ins)
    return {nm: tuple(res[4 * p:4 * p + 4]) for p, nm in enumerate(names)}


SHARD = (("ssd_conv_w", CONV_K * SSD_CONV_DIM // N_DEV), ("gdn_conv_w", CONV_K * GDN_CONV_DIM // N_DEV))
SHARD_ROWS = 24


def _rows_of(size):
    return -(-size // 128)


def _pack(vals, layout, total_rows):
    parts = []
    for (name, size), val in zip(layout, vals):
        flat = val.reshape(-1).astype(jnp.float32)
        parts.append(jnp.pad(flat, (0, _rows_of(size) * 128 - size)).reshape(-1, 128))
    used = sum(_rows_of(s) for _, s in layout)
    parts.append(jnp.zeros((total_rows - used, 128), jnp.float32))
    return jnp.concatenate(parts, axis=0)


def _conv_full(gathered_flat, ccols):
    return gathered_flat.reshape(N_DEV, CONV_K, ccols // N_DEV).transpose(1, 0, 2).reshape(CONV_K, ccols)


def kernel(x, norm_w, w_in, ssd_conv_w, ssd_conv_b, ssd_dt_bias, ssd_a_log, ssd_d, ssd_norm_w, gdn_conv_w, gdn_dt_bias, gdn_a_log, gdn_norm_w, w_out, final_norm_w, loss_target, m_norm_w, m_w_in, m_ssd_conv_w, m_ssd_conv_b, m_ssd_dt_bias, m_ssd_a_log, m_ssd_d, m_ssd_norm_w, m_gdn_conv_w, m_gdn_dt_bias, m_gdn_a_log, m_gdn_norm_w, m_w_out, m_final_norm_w, v_norm_w, v_w_in, v_ssd_conv_w, v_ssd_conv_b, v_ssd_dt_bias, v_ssd_a_log, v_ssd_d, v_ssd_norm_w, v_gdn_conv_w, v_gdn_dt_bias, v_gdn_a_log, v_gdn_norm_w, v_w_out, v_final_norm_w):
    f32 = jnp.float32
    w = dict(norm_w=norm_w, w_in=w_in, ssd_conv_w=ssd_conv_w, ssd_conv_b=ssd_conv_b, ssd_dt_bias=ssd_dt_bias,
             ssd_a_log=ssd_a_log, ssd_d=ssd_d, ssd_norm_w=ssd_norm_w, gdn_conv_w=gdn_conv_w, gdn_dt_bias=gdn_dt_bias,
             gdn_a_log=gdn_a_log, gdn_norm_w=gdn_norm_w, w_out=w_out, final_norm_w=final_norm_w)
    m = dict(norm_w=m_norm_w, w_in=m_w_in, ssd_conv_w=m_ssd_conv_w, ssd_conv_b=m_ssd_conv_b, ssd_dt_bias=m_ssd_dt_bias,
             ssd_a_log=m_ssd_a_log, ssd_d=m_ssd_d, ssd_norm_w=m_ssd_norm_w, gdn_conv_w=m_gdn_conv_w,
             gdn_dt_bias=m_gdn_dt_bias, gdn_a_log=m_gdn_a_log, gdn_norm_w=m_gdn_norm_w, w_out=m_w_out,
             final_norm_w=m_final_norm_w)
    v = dict(norm_w=v_norm_w, w_in=v_w_in, ssd_conv_w=v_ssd_conv_w, ssd_conv_b=v_ssd_conv_b, ssd_dt_bias=v_ssd_dt_bias,
             ssd_a_log=v_ssd_a_log, ssd_d=v_ssd_d, ssd_norm_w=v_ssd_norm_w, gdn_conv_w=v_gdn_conv_w,
             gdn_dt_bias=v_gdn_dt_bias, gdn_a_log=v_gdn_a_log, gdn_norm_w=v_gdn_norm_w, w_out=v_w_out,
             final_norm_w=v_final_norm_w)
    names = list(w)
    shapes = {n: w[n].shape for n in names}

    xl, tgt = x[0], loss_target[0]
    cs = _consts()
    dtb_s = _pad_lanes(ssd_dt_bias, 0)
    alog_s = _pad_lanes(ssd_a_log, 0)
    dpar = _pad_lanes(ssd_d, 0)
    dtb_g = _pad_lanes(gdn_dt_bias, 16)
    alog_g = _pad_lanes(gdn_a_log, 16)
    nw_g = gdn_norm_w.reshape(1, 128)
    nw_s = ssd_norm_w.reshape(1, 1024)
    cb_s = ssd_conv_b.reshape(1, 1536)
    nw1 = norm_w.reshape(1, D_MODEL)

    (g_w_in,) = all_gather([w_in[0].T.astype(_MM)], "gather_w_in")
    w_perm = perm_w_in(g_w_in)
    conv_pack = _pack([w["ssd_conv_w"], w["gdn_conv_w"]], SHARD, SHARD_ROWS)
    u, z, xbc, gate, qkv, sm, g_w_out, g_conv = inproj_fwd(xl, nw1, w_perm, [w_out[0].astype(_MM), conv_pack])
    w_out_full = g_w_out.reshape(MIX_WIDTH, D_MODEL)
    ssd_cw = _conv_full(g_conv[:, 0:6].reshape(N_DEV, -1), SSD_CONV_DIM)
    gdn_cw = _conv_full(g_conv[:, 6:18].reshape(N_DEV, -1), GDN_CONV_DIM)

    nc = xl.shape[0] // CHUNK
    y_ssd, hs, pre_s, y_gdn, ss, ts, pre_g = _chunk_call(
        [ssd_fwd(z, xbc, sm, ssd_cw, cb_s, dtb_s, alog_s, dpar, nw_s, cs),
         gdn_fwd(gate, qkv, sm, gdn_cw, dtb_g, alog_g, nw_g, cs)], "scan_fwd", nc)
    dout, dys, dyg, g_wout, g_fnw, loss_l = out_fwd_bwd(xl, tgt, y_ssd, y_gdn, w_out_full,
                                                        final_norm_w.reshape(1, D_MODEL))
    (dz, dxbc, g_cw_s, g_cb_s, g_dtb_s, g_alog_s, g_d, g_nw_s,
     dgate, dqkv, dsm, g_cw_g, g_dtb_g, g_alog_g, g_nw_g) = _chunk_call(
        [ssd_bwd(z, xbc, pre_s, sm, hs, dys, ssd_cw, dtb_s, alog_s, dpar, nw_s, cs),
         gdn_bwd(gate, qkv, pre_g, sm, ss, ts, dyg, gdn_cw, dtb_g, alog_g, nw_g, cs)], "scan_bwd", nc)

    t_w_out = g_wout.reshape(N_DEV, MIX_WIDTH // N_DEV, D_MODEL)
    gws = {}
    for dg, (name, _, _) in zip((dz, dxbc, dgate, dsm), (GROUPS[0], GROUPS[1], GROUPS[2], GROUPS[4])):
        gws[name] = grad_w_group(u, dg, "grad_w_in_" + name)
    gws["qkv"], r_w_out = grad_w_group(u, dqkv, "grad_w_in_qkv", [t_w_out])
    t_w_in = unperm_w_in(gws["z"], gws["xbc"], gws["gate"], gws["qkv"], gws["sm"])
    dx, g_nw, r_w_in = inproj_bwd_dx(xl, dout, nw1, w_perm, (dz, dxbc, dgate, dqkv, dsm), [t_w_in])

    accs = dict(norm_w=g_nw, ssd_conv_b=g_cb_s, ssd_dt_bias=g_dtb_s, ssd_a_log=g_alog_s, ssd_d=g_d,
                ssd_norm_w=g_nw_s, gdn_dt_bias=g_dtb_g, gdn_a_log=g_alog_g, gdn_norm_w=g_nw_g, final_norm_w=g_fnw,
                ssd_conv_w=g_cw_s, gdn_conv_w=g_cw_g)
    r_small = scatter_small([accs[e[0]] for e in SMALL])

    o_w_in = adamw_sum(r_w_in, w_in[0].T, m_w_in[0].T, v_w_in[0].T, None, "adamw_w_in", cols=256)
    o_w_out = adamw_sum(r_w_out, w_out[0], m_w_out[0], v_w_out[0], 64, "adamw_w_out")
    row = lambda d: {n: (a.reshape(1, -1) if a.ndim == 1 else a) for n, a in d.items()}
    o_small = adamw_small(r_small, row(w), row(m), row(v))

    loss = lax.psum(loss_l[0, 0], ("x", "y", "c"))
    outs = [loss, dx[None]]
    for k in range(4):
        parts = {n: o_small[n][k] for n in o_small}
        parts["w_in"] = o_w_in[k].T
        parts["w_out"] = o_w_out[k]
        outs += [parts[n].reshape(shapes[n]) for n in names]
    return tuple(outs)
```

```python
import functools

import jax
import jax.numpy as jnp
import numpy as np
from jax import lax
from jax.experimental import pallas as pl
from jax.experimental.pallas import tpu as pltpu

_MM = jnp.bfloat16

D_MODEL = 1024
CHUNK = 64
CONV_K = 4
EPS = 1e-6
SSD_CONV_DIM = 1536
GDN_HEADS = 8
GDN_DK = 128
GDN_CONV_DIM = 3072
MIX_WIDTH = 2048
IN_DIM = 6688
N_DEV = 8
W_IN_SHARD = IN_DIM // N_DEV
PERM_DIM = 6784
HI = lax.Precision.HIGHEST
HIGH = lax.Precision.HIGH
VMEM_LIMIT = 56 * 1024 * 1024

ADAM_LR = 0.001
ADAM_B1 = 0.9
ADAM_B2 = 0.999
ADAM_EPS = 1e-08
ADAM_WD = 0.01
ADAM_STEP = 10


def _pc(body, **kw):
    return pl.pallas_call(body, **kw)


def _pc_comm(body, **kw):
    return pl.pallas_call(body, **kw)


def _cparams(sem):
    return pltpu.CompilerParams(dimension_semantics=sem, vmem_limit_bytes=VMEM_LIMIT)


def _sig(x):
    return 0.5 * jnp.tanh(0.5 * x) + 0.5


@jax.custom_vjp
def _sigmoid(x):
    return _sig(x)


def _sigmoid_fwd(x):
    s = _sig(x)
    return s, s


def _sigmoid_bwd(s, g):
    return (g * s * (1.0 - s),)


_sigmoid.defvjp(_sigmoid_fwd, _sigmoid_bwd)


@jax.custom_vjp
def _silu(x):
    return x * _sig(x)


def _silu_fwd(x):
    s = _sig(x)
    return x * s, (x, s)


def _silu_bwd(res, g):
    x, s = res
    return (g * (s * (1.0 + x * (1.0 - s))),)


_silu.defvjp(_silu_fwd, _silu_bwd)


def _softplus_impl(x):
    return jnp.maximum(x, 0.0) + jnp.log(1.0 + jnp.exp(-jnp.abs(x)))


@jax.custom_vjp
def _softplus(x):
    return _softplus_impl(x)


def _softplus_fwd(x):
    return _softplus_impl(x), x


def _softplus_bwd(x, g):
    return (g * _sig(x),)


_softplus.defvjp(_softplus_fwd, _softplus_bwd)


def _lane_bcast_impl(x, k):
    return jnp.broadcast_to(x[..., k:k + 1], x.shape)


@functools.partial(jax.custom_vjp, nondiff_argnums=(1,))
def _lane_bcast(x, k):
    return _lane_bcast_impl(x, k)


def _lane_bcast_fwd(x, k):
    return _lane_bcast_impl(x, k), None


def _lane_bcast_bwd(k, _, g):
    lane = lax.broadcasted_iota(jnp.int32, g.shape, g.ndim - 1)
    return (jnp.where(lane == k, jnp.sum(g, axis=-1, keepdims=True), 0.0),)


_lane_bcast.defvjp(_lane_bcast_fwd, _lane_bcast_bwd)


def _mm(a, b):
    return jnp.dot(a.astype(_MM), b.astype(_MM), preferred_element_type=jnp.float32)


def _mm_nt(a, b):
    return lax.dot_general(a.astype(_MM), b.astype(_MM), (((1,), (1,)), ((), ())),
                           preferred_element_type=jnp.float32)


def _mm_tn(a, b):
    return lax.dot_general(a.astype(_MM), b.astype(_MM), (((0,), (0,)), ((), ())),
                           preferred_element_type=jnp.float32)


def _dot_hi(a, b):
    return jnp.dot(a, b, precision=HI, preferred_element_type=jnp.float32)


def _bmm(a, b):
    return lax.dot_general(a.astype(_MM), b.astype(_MM), (((2,), (1,)), ((0,), (0,))),
                           preferred_element_type=jnp.float32)


def _bmm_nt(a, b):
    return lax.dot_general(a.astype(_MM), b.astype(_MM), (((2,), (2,)), ((0,), (0,))),
                           preferred_element_type=jnp.float32)


def _bmm_tn(a, b):
    return lax.dot_general(a.astype(_MM), b.astype(_MM), (((1,), (1,)), ((0,), (0,))),
                           preferred_element_type=jnp.float32)


def _bmm_hi(a, b):
    return lax.dot_general(a, b, (((2,), (1,)), ((0,), (0,))), precision=HIGH, preferred_element_type=jnp.float32)


def _bmm_nt_hi(a, b):
    return lax.dot_general(a, b, (((2,), (2,)), ((0,), (0,))), precision=HIGH, preferred_element_type=jnp.float32)


def _bmm_tn_hi(a, b):
    return lax.dot_general(a, b, (((1,), (1,)), ((0,), (0,))), precision=HIGH, preferred_element_type=jnp.float32)


def _consts():
    l = np.arange(CHUNK)
    tri = (l[:, None] >= l[None, :]).astype(np.float32)
    lane = np.arange(128)
    i2 =(l[:, None] == (lane[None, :] % 64)).astype(np.float32)
    mask2 = (l[:, None] >= (lane[None, :] % 64)).astype(np.float32)
    lo = (lane < 64).astype(np.float32)[None, :]
    i64 = np.eye(CHUNK, dtype=np.float32)
    strict = (l[:, None] > l[None, :]).astype(np.float32)
    return dict(tri=jnp.asarray(tri), i2=jnp.asarray(i2), mask2=jnp.asarray(mask2), lo=jnp.asarray(lo),
                i64=jnp.asarray(i64), strict=jnp.asarray(strict))


def _ssd_chunk(xs_pre, b_pre, c_pre, z, sm, ht, dtb, alog, dpar, nw, tri, i2, mask2, lo):
    lane = lax.broadcasted_iota(jnp.int32, (1, 128), 1)
    m16 = lane < 16
    dt = jnp.where(m16, _softplus(sm + dtb), 0.0)
    a_neg = -jnp.exp(alog)
    cum = _dot_hi(tri, dt * a_neg)
    row = lax.broadcasted_iota(jnp.int32, (CHUNK, 1), 0)
    is_last = row == CHUNK - 1
    hi = 1.0 - lo
    bm = [_silu(b) for b in b_pre]
    cm = [_silu(c) for c in c_pre]
    cb2 = [_mm_nt(cm[g], jnp.concatenate([bm[g], bm[g]], axis=0)) for g in range(2)]
    yg, ht_next = [], []
    for j in range(8):
        g = j // 4
        pair = lambda v, j=j: jnp.where(lo > 0.5, _lane_bcast(v, 2 * j), _lane_bcast(v, 2 * j + 1))
        xs = _silu(xs_pre[j])
        dte = pair(dt)
        cume = pair(cum)
        cum_last = jnp.sum(jnp.where(is_last, cume, 0.0), axis=0, keepdims=True)
        xdt = xs * dte
        rowv = jnp.sum(cume * i2, axis=0, keepdims=True)
        lm = jnp.exp(jnp.where(mask2 > 0.5, cume - rowv, -jnp.inf))
        m = cb2[g] * lm
        xblk = jnp.concatenate([xdt * lo, xdt * hi], axis=0)
        y = _mm(m, xblk)
        y = y + _mm(cm[g], ht[j]) * jnp.exp(cume)
        y = y + pair(dpar) * xs
        yg.append(y * _silu(z[j]))
        st = _mm_tn(bm[g], xdt * jnp.exp(cum_last - cume))
        ht_next.append(ht[j] * jnp.exp(cum_last) + st)
    outs = []
    for g in range(2):
        ss = sum(jnp.sum(yg[j] * yg[j], axis=-1, keepdims=True) for j in range(4 * g, 4 * g + 4))
        rs = lax.rsqrt(ss * (1.0 / 512.0) + EPS)
        for j in range(4 * g, 4 * g + 4):
            outs.append(yg[j] * rs * nw[j])
    return outs, ht_next


def _tri_inverse(a):
    eye = jnp.eye(CHUNK, dtype=jnp.float32)[None]
    p = eye - a
    ap = a
    for _ in range(5):
        ap = _bmm_hi(ap, ap)
        p = p + _bmm_hi(p, ap)
    return p


@jax.custom_vjp
def _solve(a, r1, r2, t):
    return _bmm_hi(t, r1), _bmm_hi(t, r2)


def _solve_fwd(a, r1, r2, t):
    u, w = _bmm_hi(t, r1), _bmm_hi(t, r2)
    return (u, w), (t, u, w)


def _solve_bwd(res, cts):
    t, u, w = res
    du, dw = cts
    dr1 = _bmm_tn_hi(t, du)
    dr2 = _bmm_tn_hi(t, dw)
    da = -(_bmm_nt_hi(dr1, u) + _bmm_nt_hi(dr2, w))
    return da, dr1, dr2, jnp.zeros_like(t)


_solve.defvjp(_solve_fwd, _solve_bwd)


def _gdn_chunk(q_pre, k_pre, v_pre, gate, sm, s, dtb, alog, nw, tri, i64, strict, t_in=None):
    lane = lax.broadcasted_iota(jnp.int32, (1, 128), 1)
    m_a = (lane >= 16) & (lane < 24)
    g_full = jnp.where(m_a, -jnp.exp(alog) * _softplus(sm + dtb), 0.0)
    gc = _dot_hi(tri, g_full)
    sig = _sigmoid(sm)
    gc3 = jnp.stack([_lane_bcast(gc, 16 + h) for h in range(GDN_HEADS)])
    beta3 = jnp.stack([_lane_bcast(sig, 24 + h) for h in range(GDN_HEADS)])
    q = _silu(q_pre)
    q = q * lax.rsqrt(jnp.sum(q * q, axis=-1, keepdims=True) + EPS) * (GDN_DK ** -0.5)
    k = _silu(k_pre)
    k = k * lax.rsqrt(jnp.sum(k * k, axis=-1, keepdims=True) + EPS)
    v = _silu(v_pre)
    gcl = gc3[:, :, :CHUNK]
    gc_row = jnp.sum(gcl * i64[None], axis=1, keepdims=True)
    incl = (strict + i64)[None] > 0.5
    decay = jnp.exp(jnp.where(incl, gcl - gc_row, -jnp.inf))
    kb = k * beta3
    a = jnp.where(strict[None] > 0.5, _bmm_nt(kb, k) * decay, 0.0)
    egc = jnp.exp(gc3)
    t = _tri_inverse(a) if t_in is None else t_in
    u, w = _solve(a, v * beta3, kb * egc, t)
    attn = _bmm_nt(q, k) * decay
    row = lax.broadcasted_iota(jnp.int32, (1, CHUNK, 1), 1)
    gl = jnp.sum(jnp.where(row == CHUNK - 1, gc3, 0.0), axis=1, keepdims=True)
    q_dec = q * egc
    k_dec = k * jnp.exp(gl - gc3)
    v_new = u - _bmm(w, s)
    o = _bmm(q_dec, s) + _bmm(attn, v_new)
    s_next = s * jnp.exp(gl) + _bmm_tn(k_dec, v_new)
    on = o * lax.rsqrt(jnp.mean(o * o, axis=-1, keepdims=True) + EPS) * nw
    return on * _silu(gate), s_next, t


def _conv_fwd(pbuf, w_ref, c0, c1):
    acc = None
    for j in range(CONV_K):
        term = w_ref[j:j + 1, c0:c1] * pbuf[5 + j:69 + j, c0:c1]
        acc = term if acc is None else acc + term
    return acc


MESH = pl.DeviceIdType.MESH
ANY = pl.BlockSpec(memory_space=pl.ANY)


def _me():
    x, y, c = lax.axis_index("x"), lax.axis_index("y"), lax.axis_index("c")
    return x, y, c, 4 * x + 2 * y + c


def _peer(r):
    x, y, c, _ = _me()
    px = 1 - x if r & 4 else x
    py = 1 - y if r & 2 else y
    pc = 1 - c if r & 1 else c
    return (px, py, pc), 4 * px + 2 * py + pc


def _exchange_ops(kind, in_ref, out_ref, send_sems, recv_sems, local_sem):
    me = _me()[3]
    local = pltpu.make_async_copy(in_ref.at[me] if kind == "scatter" else in_ref, out_ref.at[me], local_sem)
    sends, recvs = [], []
    for r in range(1, N_DEV):
        peer, pidx = _peer(r)
        src = in_ref.at[pidx] if kind == "scatter" else in_ref
        sems = dict(send_sem=send_sems.at[r - 1], recv_sem=recv_sems.at[r - 1], device_id=peer, device_id_type=MESH)
        sends.append(pltpu.make_async_remote_copy(src_ref=src, dst_ref=out_ref.at[me], **sems))
        recvs.append(pltpu.make_async_remote_copy(src_ref=src, dst_ref=out_ref.at[pidx], **sems))

    def start():
        local.start()
        for cp in sends:
            cp.start()

    def wait():
        for cp in recvs:
            cp.wait_recv()
        for cp in sends:
            cp.wait_send()
        local.wait()

    return start, wait


def _exchange_sems(n):
    return [pltpu.SemaphoreType.DMA((N_DEV - 1,)), pltpu.SemaphoreType.DMA((N_DEV - 1,)),
            pltpu.SemaphoreType.DMA(())] * n


def _exchange_out_shape(kind, a):
    return jax.ShapeDtypeStruct(a.shape if kind == "scatter" else (N_DEV,) + a.shape, a.dtype)


def _hosting(body, n_in, n_out, n_scratch, kinds, first, last):
    ne = len(kinds)

    def wrapped(*refs):
        ins, ex_in = refs[:n_in], refs[n_in:n_in + ne]
        o0 = n_in + ne
        outs, ex_out = refs[o0:o0 + n_out], refs[o0 + n_out:o0 + n_out + ne]
        s0 = o0 + n_out + ne
        scr, sems = refs[s0:s0 + n_scratch], refs[s0 + n_scratch:]
        ops = [_exchange_ops(kinds[e], ex_in[e], ex_out[e], *sems[3 * e:3 * e + 3]) for e in range(ne)]

        @pl.when(first())
        def _():
            for start, _ in ops:
                start()

        body(*ins, *outs, *scr)

        @pl.when(last())
        def _():
            for _, wait in ops:
                wait()

    return wrapped


GROUPS = (("z", 0, 1024), ("xbc", 1024, 2560), ("gate", 2560, 3584), ("qkv", 3584, 6656), ("sm", 6656, 6784))


def inproj_fwd(x, norm_w, w_perm, gathered):
    t = x.shape[0]
    tm = min(512, t)
    steps = t // tm
    kinds = ["gather"] * len(gathered)

    def body(x_ref, nw_ref, w_ref, u_ref, z_ref, xbc_ref, gate_ref, qkv_ref, sm_ref):
        xf = x_ref[...]
        rstd = lax.rsqrt(jnp.mean(xf * xf, axis=-1, keepdims=True) + EPS)
        u = (xf * rstd * nw_ref[...]).astype(_MM)
        u_ref[...] = u
        for (name, c0, c1), o_ref in zip(GROUPS, (z_ref, xbc_ref, gate_ref, qkv_ref, sm_ref)):
            o_ref[...] = lax.dot_general(u, w_ref[c0:c1, :], (((1,), (1,)), ((), ())),
                                         preferred_element_type=jnp.float32)

    outs = [jax.ShapeDtypeStruct((t, D_MODEL), _MM)] + [jax.ShapeDtypeStruct((t, c1 - c0), jnp.float32)
                                                        for _, c0, c1 in GROUPS]
    hosted = _hosting(body, 3, 6, 0, kinds, lambda: pl.program_id(0) == 0, lambda: pl.program_id(0) == steps - 1)
    return _pc_comm(
        hosted, name="inproj_fwd", grid=(steps,),
        in_specs=[pl.BlockSpec((tm, D_MODEL), lambda i: (i, 0)),
                  pl.BlockSpec((1, D_MODEL), lambda i: (0, 0)),
                  pl.BlockSpec((PERM_DIM, D_MODEL), lambda i: (0, 0), pipeline_mode=pl.Buffered(1))] +
                 [ANY] * len(gathered),
        out_specs=[pl.BlockSpec((tm, D_MODEL), lambda i: (i, 0))] +
                  [pl.BlockSpec((tm, c1 - c0), lambda i: (i, 0)) for _, c0, c1 in GROUPS] + [ANY] * len(gathered),
        out_shape=outs + [_exchange_out_shape("gather", a) for a in gathered],
        scratch_shapes=_exchange_sems(len(gathered)), compiler_params=_cparams(("arbitrary",)),
    )(x, norm_w, w_perm, *gathered)


def _halo_spec(width, idx_fn):
    return pl.BlockSpec((8, width), lambda i: (jnp.maximum(idx_fn(i) * 8 - 1, 0), 0))


def _full(shape):
    nd = len(shape)
    return pl.BlockSpec(shape, lambda i: (0,) * nd)


def _ssd_split(pre_fn, z_ref, sm_ref):
    xs_pre = [pre_fn(128 * j, 128 * j + 128) for j in range(8)]
    b_pre = [pre_fn(1024 + 128 * g, 1152 + 128 * g) for g in range(2)]
    c_pre = [pre_fn(1280 + 128 * g, 1408 + 128 * g) for g in range(2)]
    z = [z_ref[:, 128 * j:128 * j + 128] for j in range(8)]
    return xs_pre, b_pre, c_pre, z, sm_ref[...]


def ssd_fwd(z, xbc, sm, conv_w, conv_b, dtb, alog, dpar, nw, cs):
    t = z.shape[0]
    nc = t // CHUNK

    def body(shared, z_ref, xbc_ref, halo_ref, sm_ref, cw_ref, cb_ref, dtb_ref, alog_ref, dpar_ref, nw_ref,
             tri_ref, i2_ref, mask2_ref, lo_ref, y_ref, hs_ref, pre_ref, pbuf, ht_scr):
        i = pl.program_id(0)

        @pl.when(i == 0)
        def _():
            ht_scr[...] = jnp.zeros_like(ht_scr)

        pbuf[0:8, :] = jnp.where(i == 0, 0.0, halo_ref[...])
        pbuf[8:72, :] = xbc_ref[...]

        def pre_fn(c0, c1):
            pre = _conv_fwd(pbuf, cw_ref, c0, c1) + cb_ref[:, c0:c1]
            pre_ref[:, c0:c1] = pre
            return pre

        xs_pre, b_pre, c_pre, zz, smv = _ssd_split(pre_fn, z_ref, sm_ref)
        ht = [ht_scr[:, 128 * j:128 * j + 128] for j in range(8)]
        hs_ref[0] = ht_scr[...]
        nwl = [nw_ref[:, 128 * j:128 * j + 128] for j in range(8)]
        outs, ht_next = _ssd_chunk(xs_pre, b_pre, c_pre, zz, smv, ht, dtb_ref[...], alog_ref[...], dpar_ref[...],
                                   nwl, tri_ref[...], i2_ref[...], mask2_ref[...], lo_ref[...])
        for j in range(8):
            y_ref[:, 128 * j:128 * j + 128] = outs[j].astype(y_ref.dtype)
            ht_scr[:, 128 * j:128 * j + 128] = ht_next[j]

    blk = lambda w: pl.BlockSpec((CHUNK, w), lambda i: (i, 0))
    return dict(
        body=body,
        in_specs=[blk(1024), blk(1536), _halo_spec(1536, lambda i: i), blk(128),
                  _full((CONV_K, 1536)), _full((1, 1536)), _full((1, 128)), _full((1, 128)), _full((1, 128)),
                  _full((1, 1024)), _full((64, 64)), _full((64, 128)), _full((64, 128)),
                  _full((1, 128))],
        out_specs=[blk(1024), pl.BlockSpec((1, 128, 1024), lambda i: (i, 0, 0)), blk(1536)],
        out_shape=[jax.ShapeDtypeStruct((t, 1024), _MM), jax.ShapeDtypeStruct((nc, 128, 1024), jnp.float32),
                   jax.ShapeDtypeStruct((t, 1536), jnp.float32)],
        scratch=[pltpu.VMEM((72, 1536), jnp.float32), pltpu.VMEM((128, 1024), jnp.float32)],
        args=[z, xbc, xbc, sm, conv_w, conv_b, dtb, alog, dpar, nw, cs["tri"], cs["i2"], cs["mask2"], cs["lo"]])


def _conv_bwd(dpre_list, col_ranges, dbuf, carry, x_ref, cw_ref, dx_ref, dcw_ref, dcb_ref, first):
    for dpre, (c0, c1) in zip(dpre_list, col_ranges):
        dbuf[0:64, c0:c1] = dpre
    dbuf[64:72, :] = jnp.where(first, 0.0, carry[...])
    carry[...] = dbuf[0:8, :]
    for (c0, c1) in col_ranges:
        xin = x_ref[:, c0:c1]
        acc = None
        for j in range(CONV_K):
            sh = dbuf[3 - j:67 - j, c0:c1]
            term = cw_ref[j:j + 1, c0:c1] * sh
            acc = term if acc is None else acc + term
            dcw_ref[j:j + 1, c0:c1] += jnp.sum(xin * sh, axis=0, keepdims=True)
        dx_ref[:, c0:c1] = acc.astype(dx_ref.dtype)
        if dcb_ref is not None:
            dcb_ref[0:1, c0:c1] += jnp.sum(dbuf[0:64, c0:c1], axis=0, keepdims=True)


def ssd_bwd(z, xbc, pre, sm, hs, dy, conv_w, dtb, alog, dpar, nw, cs):
    t = z.shape[0]
    nc = t // CHUNK

    def body(shared, z_ref, xbc_ref, pre_ref, sm_ref, hs_ref, dy_ref, cw_ref, dtb_ref, alog_ref, dpar_ref, nw_ref,
             tri_ref, i2_ref, mask2_ref, lo_ref,
             dz_ref, dxbc_ref, dcw_ref, dcb_ref, ddtb_ref, dalog_ref, ddpar_ref, dnw_ref,
             dbuf, carry, dht_scr):
        i = pl.program_id(0)

        @pl.when(i == 0)
        def _():
            dht_scr[...] = jnp.zeros_like(dht_scr)
            dcw_ref[...] = jnp.zeros_like(dcw_ref)
            dcb_ref[...] = jnp.zeros_like(dcb_ref)
            ddtb_ref[...] = jnp.zeros_like(ddtb_ref)
            dalog_ref[...] = jnp.zeros_like(dalog_ref)
            ddpar_ref[...] = jnp.zeros_like(ddpar_ref)
            dnw_ref[...] = jnp.zeros_like(dnw_ref)

        pre_fn = lambda c0, c1: pre_ref[:, c0:c1]
        xs_pre, b_pre, c_pre, zz, smv = _ssd_split(pre_fn, z_ref, sm_ref)
        ht = [hs_ref[0, :, 128 * j:128 * j + 128] for j in range(8)]
        nwl = [nw_ref[:, 128 * j:128 * j + 128] for j in range(8)]
        consts = (tri_ref[...], i2_ref[...], mask2_ref[...], lo_ref[...])

        def f(xs_pre, b_pre, c_pre, zz, smv, ht, dtb, alog, dpar, nwl):
            return _ssd_chunk(xs_pre, b_pre, c_pre, zz, smv, ht, dtb, alog, dpar, nwl, *consts)

        _, vjp = jax.vjp(f, xs_pre, b_pre, c_pre, zz, smv, ht, dtb_ref[...], alog_ref[...], dpar_ref[...], nwl)
        dys = [dy_ref[:, 128 * j:128 * j + 128] for j in range(8)]
        dhts = [dht_scr[:, 128 * j:128 * j + 128] for j in range(8)]
        dxs, db, dc, dzz, dsm, dht, ddtb, dalog, ddpar, dnwl = vjp((dys, dhts))
        for j in range(8):
            dz_ref[:, 128 * j:128 * j + 128] = dzz[j].astype(dz_ref.dtype)
            dht_scr[:, 128 * j:128 * j + 128] = dht[j]
            dnw_ref[0:1, 128 * j:128 * j + 128] += dnwl[j]
        shared["dsm_ssd"] = dsm
        ddtb_ref[0:1, :] += ddtb
        dalog_ref[0:1, :] += dalog
        ddpar_ref[0:1, :] += ddpar
        ranges = ([(128 * j, 128 * j + 128) for j in range(8)] + [(1024 + 128 * g, 1152 + 128 * g) for g in range(2)]
                  + [(1280 + 128 * g, 1408 + 128 * g) for g in range(2)])
        _conv_bwd(dxs + db + dc, ranges, dbuf, carry, xbc_ref, cw_ref, dxbc_ref, dcw_ref, dcb_ref, i == 0)

    rblk = lambda w: pl.BlockSpec((CHUNK, w), lambda i: (nc - 1 - i, 0))
    acc = lambda w: pl.BlockSpec((8, w), lambda i: (0, 0))
    f32 = jnp.float32
    return dict(
        body=body,
        in_specs=[rblk(1024), rblk(1536), rblk(1536), rblk(128),
                  pl.BlockSpec((1, 128, 1024), lambda i: (nc - 1 - i, 0, 0)), rblk(1024),
                  _full((CONV_K, 1536)), _full((1, 128)), _full((1, 128)), _full((1, 128)),
                  _full((1, 1024)), _full((64, 64)), _full((64, 128)), _full((64, 128)),
                  _full((1, 128))],
        out_specs=[rblk(1024), rblk(1536), acc(1536), acc(1536), acc(128), acc(128), acc(128), acc(1024)],
        out_shape=[jax.ShapeDtypeStruct((t, 1024), f32), jax.ShapeDtypeStruct((t, 1536), f32),
                   jax.ShapeDtypeStruct((8, 1536), f32),
                   jax.ShapeDtypeStruct((8, 1536), f32), jax.ShapeDtypeStruct((8, 128), f32),
                   jax.ShapeDtypeStruct((8, 128), f32), jax.ShapeDtypeStruct((8, 128), f32),
                   jax.ShapeDtypeStruct((8, 1024), f32)],
        scratch=[pltpu.VMEM((72, 1536), f32), pltpu.VMEM((8, 1536), f32), pltpu.VMEM((128, 1024), f32)],
        args=[z, xbc, pre, sm, hs, dy, conv_w, dtb, alog, dpar, nw, cs["tri"], cs["i2"], cs["mask2"], cs["lo"]])


def _gdn_split(pre_fn, gate_ref):
    def heads(base):
        return jnp.stack([pre_fn(base + 128 * h, base + 128 * h + 128) for h in range(GDN_HEADS)])
    gate = jnp.stack([gate_ref[:, 128 * h:128 * h + 128] for h in range(GDN_HEADS)])
    return heads(0), heads(1024), heads(2048), gate


def gdn_fwd(gate, qkv, sm, conv_w, dtb, alog, nw, cs):
    t = gate.shape[0]
    nc = t // CHUNK

    def body(shared, gate_ref, qkv_ref, halo_ref, sm_ref, cw_ref, dtb_ref, alog_ref, nw_ref,
             tri_ref, i64_ref, strict_ref, o_ref, ss_ref, ts_ref, pre_ref, pbuf, s_scr):
        i = pl.program_id(0)

        @pl.when(i == 0)
        def _():
            s_scr[...] = jnp.zeros_like(s_scr)

        pbuf[0:8, :] = jnp.where(i == 0, 0.0, halo_ref[...])
        pbuf[8:72, :] = qkv_ref[...]

        def pre_fn(c0, c1):
            pre = _conv_fwd(pbuf, cw_ref, c0, c1)
            pre_ref[:, c0:c1] = pre
            return pre

        q_pre, k_pre, v_pre, g3 = _gdn_split(pre_fn, gate_ref)
        s = s_scr[...]
        ss_ref[0] = s
        out, s_next, tinv = _gdn_chunk(q_pre, k_pre, v_pre, g3, sm_ref[...], s, dtb_ref[...], alog_ref[...],
                                       nw_ref[...], tri_ref[...], i64_ref[...], strict_ref[...])
        ts_ref[0] = tinv
        s_scr[...] = s_next
        for h in range(GDN_HEADS):
            o_ref[:, 128 * h:128 * h + 128] = out[h].astype(o_ref.dtype)

    blk = lambda w: pl.BlockSpec((CHUNK, w), lambda i: (i, 0))
    return dict(
        body=body,
        in_specs=[blk(1024), blk(3072), _halo_spec(3072, lambda i: i), blk(128),
                  _full((CONV_K, 3072)), _full((1, 128)), _full((1, 128)), _full((1, 128)),
                  _full((64, 64)), _full((64, 64)), _full((64, 64))],
        out_specs=[blk(1024), pl.BlockSpec((1, 8, 128, 128), lambda i: (i, 0, 0, 0)),
                   pl.BlockSpec((1, 8, CHUNK, CHUNK), lambda i: (i, 0, 0, 0)), blk(3072)],
        out_shape=[jax.ShapeDtypeStruct((t, 1024), _MM), jax.ShapeDtypeStruct((nc, 8, 128, 128), jnp.float32),
                   jax.ShapeDtypeStruct((nc, 8, CHUNK, CHUNK), jnp.float32),
                   jax.ShapeDtypeStruct((t, 3072), jnp.float32)],
        scratch=[pltpu.VMEM((72, 3072), jnp.float32), pltpu.VMEM((8, 128, 128), jnp.float32)],
        args=[gate, qkv, qkv, sm, conv_w, dtb, alog, nw, cs["tri"], cs["i64"], cs["strict"]])


def gdn_bwd(gate, qkv, pre, sm, ss, ts, do, conv_w, dtb, alog, nw, cs):
    t = gate.shape[0]
    nc = t // CHUNK

    def body(shared, gate_ref, qkv_ref, pre_ref, sm_ref, ss_ref, ts_ref, do_ref, cw_ref, dtb_ref, alog_ref,
             nw_ref, tri_ref, i64_ref, strict_ref,
             dgate_ref, dqkv_ref, dsm_ref, dcw_ref, ddtb_ref, dalog_ref, dnw_ref,
             dbuf, carry, ds_scr):
        i = pl.program_id(0)

        @pl.when(i == 0)
        def _():
            ds_scr[...] = jnp.zeros_like(ds_scr)
            dcw_ref[...] = jnp.zeros_like(dcw_ref)
            ddtb_ref[...] = jnp.zeros_like(ddtb_ref)
            dalog_ref[...] = jnp.zeros_like(dalog_ref)
            dnw_ref[...] = jnp.zeros_like(dnw_ref)

        q_pre, k_pre, v_pre, g3 = _gdn_split(lambda c0, c1: pre_ref[:, c0:c1], gate_ref)
        consts = (tri_ref[...], i64_ref[...], strict_ref[...], ts_ref[0])

        def f(q_pre, k_pre, v_pre, g3, smv, s, dtb, alog, nwv):
            return _gdn_chunk(q_pre, k_pre, v_pre, g3, smv, s, dtb, alog, nwv, *consts)[:2]

        _, vjp = jax.vjp(f, q_pre, k_pre, v_pre, g3, sm_ref[...], ss_ref[0], dtb_ref[...], alog_ref[...], nw_ref[...])
        do3 = jnp.stack([do_ref[:, 128 * h:128 * h + 128] for h in range(GDN_HEADS)])
        dq, dk, dv, dg3, dsm, ds, ddtb, dalog, dnw = vjp((do3, ds_scr[...]))
        ds_scr[...] = ds
        for h in range(GDN_HEADS):
            dgate_ref[:, 128 * h:128 * h + 128] = dg3[h].astype(dgate_ref.dtype)
        dsm_ref[...] = (dsm + shared["dsm_ssd"]).astype(dsm_ref.dtype)
        ddtb_ref[0:1, :] += ddtb
        dalog_ref[0:1, :] += dalog
        dnw_ref[0:1, :] += dnw
        ranges = [(base + 128 * h, base + 128 * h + 128) for base in (0, 1024, 2048) for h in range(GDN_HEADS)]
        dlist = [d[h] for d in (dq, dk, dv) for h in range(GDN_HEADS)]
        _conv_bwd(dlist, ranges, dbuf, carry, qkv_ref, cw_ref, dqkv_ref, dcw_ref, None, i == 0)

    rblk = lambda w: pl.BlockSpec((CHUNK, w), lambda i: (nc - 1 - i, 0))
    acc = lambda w: pl.BlockSpec((8, w), lambda i: (0, 0))
    f32 = jnp.float32
    return dict(
        body=body,
        in_specs=[rblk(1024), rblk(3072), rblk(3072), rblk(128),
                  pl.BlockSpec((1, 8, 128, 128), lambda i: (nc - 1 - i, 0, 0, 0)),
                  pl.BlockSpec((1, 8, CHUNK, CHUNK), lambda i: (nc - 1 - i, 0, 0, 0)), rblk(1024),
                  _full((CONV_K, 3072)), _full((1, 128)), _full((1, 128)), _full((1, 128)),
                  _full((64, 64)), _full((64, 64)), _full((64, 64))],
        out_specs=[rblk(1024), rblk(3072), rblk(128), acc(3072), acc(128), acc(128), acc(128)],
        out_shape=[jax.ShapeDtypeStruct((t, 1024), f32), jax.ShapeDtypeStruct((t, 3072), f32),
                   jax.ShapeDtypeStruct((t, 128), f32), jax.ShapeDtypeStruct((8, 3072), f32),
                   jax.ShapeDtypeStruct((8, 128), f32), jax.ShapeDtypeStruct((8, 128), f32),
                   jax.ShapeDtypeStruct((8, 128), f32)],
        scratch=[pltpu.VMEM((72, 3072), f32), pltpu.VMEM((8, 3072), f32), pltpu.VMEM((8, 128, 128), f32)],
        args=[gate, qkv, pre, sm, ss, ts, do, conv_w, dtb, alog, nw, cs["tri"], cs["i64"], cs["strict"]])


def _chunk_call(parts, name, nc):
    n_in = [len(p["args"]) for p in parts]
    n_out = [len(p["out_shape"]) for p in parts]
    n_scr = [len(p["scratch"]) for p in parts]

    def body(*refs):
        ins, outs, scr = refs[:sum(n_in)], refs[sum(n_in):sum(n_in) + sum(n_out)], refs[sum(n_in) + sum(n_out):]
        shared = {}
        for k, p in enumerate(parts):
            i0, o0, s0 = sum(n_in[:k]), sum(n_out[:k]), sum(n_scr[:k])
            p["body"](shared, *ins[i0:i0 + n_in[k]], *outs[o0:o0 + n_out[k]], *scr[s0:s0 + n_scr[k]])

    cat = lambda key: [v for p in parts for v in p[key]]
    return _pc(body, name=name, grid=(nc,), in_specs=cat("in_specs"), out_specs=cat("out_specs"),
               out_shape=cat("out_shape"), scratch_shapes=cat("scratch"),
               compiler_params=_cparams(("arbitrary",)))(*cat("args"))


def out_fwd_bwd(x, tgt, y_ssd, y_gdn, w_out, fnw):
    t = x.shape[0]
    tm = min(512, t)
    f32 = jnp.float32

    def body(x_ref, tgt_ref, ys_ref, yg_ref, w_ref, fnw_ref,
             dout_ref, dys_ref, dyg_ref, gw_ref, gfnw_ref, loss_ref, gw_acc):
        i = pl.program_id(0)

        @pl.when(i == 0)
        def _():
            gw_acc[...] = jnp.zeros_like(gw_acc)
            gfnw_ref[...] = jnp.zeros_like(gfnw_ref)
            loss_ref[...] = jnp.zeros_like(loss_ref)

        ys = ys_ref[...]
        yg = yg_ref[...]
        out = x_ref[...] + jnp.dot(ys, w_ref[0:1024, :], preferred_element_type=f32) \
            + jnp.dot(yg, w_ref[1024:2048, :], preferred_element_type=f32)
        rstd = lax.rsqrt(jnp.mean(out * out, axis=-1, keepdims=True) + EPS)
        yhat = out * rstd
        fw = fnw_ref[...]
        e = yhat * fw - tgt_ref[...]
        loss_ref[...] += 0.5 * jnp.sum(jnp.sum(e * e, axis=-1, keepdims=True) * (1.0 / D_MODEL), axis=0, keepdims=True)
        dyf = e * (1.0 / D_MODEL)
        gfnw_ref[0:1, :] += jnp.sum(dyf * yhat, axis=0, keepdims=True)
        dyhat = dyf * fw
        dout = rstd * (dyhat - yhat * jnp.mean(dyhat * yhat, axis=-1, keepdims=True))
        dout_ref[...] = dout
        db = dout.astype(_MM)
        dys_ref[...] = lax.dot_general(db, w_ref[0:1024, :], (((1,), (1,)), ((), ())), preferred_element_type=f32)
        dyg_ref[...] = lax.dot_general(db, w_ref[1024:2048, :], (((1,), (1,)), ((), ())), preferred_element_type=f32)
        gw_acc[0:1024, :] += lax.dot_general(ys, db, (((0,), (0,)), ((), ())), preferred_element_type=f32)
        gw_acc[1024:2048, :] += lax.dot_general(yg, db, (((0,), (0,)), ((), ())), preferred_element_type=f32)

        @pl.when(i == steps - 1)
        def _():
            gw_ref[...] = gw_acc[...].astype(gw_ref.dtype)

    steps = t // tm
    blk = pl.BlockSpec((tm, D_MODEL), lambda i: (i, 0))
    return _pc(
        body, name="out_fwd_bwd", grid=(steps,),
        in_specs=[blk, blk, blk, blk, _full((MIX_WIDTH, D_MODEL)), _full((1, D_MODEL))],
        out_specs=[blk, blk, blk, _full((MIX_WIDTH, D_MODEL)), _full((8, D_MODEL)), _full((1, 128))],
        out_shape=[jax.ShapeDtypeStruct((t, D_MODEL), f32)] * 3 +
                  [jax.ShapeDtypeStruct((MIX_WIDTH, D_MODEL), _MM), jax.ShapeDtypeStruct((8, D_MODEL), f32),
                   jax.ShapeDtypeStruct((1, 128), f32)],
        scratch_shapes=[pltpu.VMEM((MIX_WIDTH, D_MODEL), f32)],
        compiler_params=_cparams(("arbitrary",)),
    )(x, tgt, y_ssd, y_gdn, w_out, fnw)


def inproj_bwd_dx(x, dout, norm_w, w_perm, dgroups, scattered):
    t = x.shape[0]
    tm = min(256, t)
    f32 = jnp.float32

    def body(x_ref, dout_ref, nw_ref, w_ref, dz_ref, dxbc_ref, dgate_ref, dqkv_ref, dsm_ref, dx_ref, gnw_ref):
        i = pl.program_id(0)

        @pl.when(i == 0)
        def _():
            gnw_ref[...] = jnp.zeros_like(gnw_ref)

        du = None
        for (name, c0, c1), d_ref in zip(GROUPS, (dz_ref, dxbc_ref, dgate_ref, dqkv_ref, dsm_ref)):
            term = jnp.dot(d_ref[...].astype(_MM), w_ref[c0:c1, :], preferred_element_type=f32)
            du = term if du is None else du + term
        xf = x_ref[...]
        rstd = lax.rsqrt(jnp.mean(xf * xf, axis=-1, keepdims=True) + EPS)
        xhat = xf * rstd
        gnw_ref[0:1, :] += jnp.sum(du * xhat, axis=0, keepdims=True)
        dxh = du * nw_ref[...]
        dx_ref[...] = dout_ref[...] + rstd * (dxh - xhat * jnp.mean(dxh * xhat, axis=-1, keepdims=True))

    blk = lambda w: pl.BlockSpec((tm, w), lambda i: (i, 0))
    steps = t // tm
    kinds = ["scatter"] * len(scattered)
    hosted = _hosting(body, 9, 2, 0, kinds, lambda: pl.program_id(0) == 0, lambda: pl.program_id(0) == steps - 1)
    return _pc_comm(
        hosted, name="inproj_bwd_dx", grid=(steps,),
        in_specs=[blk(D_MODEL), blk(D_MODEL), _full((1, D_MODEL)), _full((PERM_DIM, D_MODEL))] +
                 [blk(c1 - c0) for _, c0, c1 in GROUPS] + [ANY] * len(scattered),
        out_specs=[blk(D_MODEL), _full((8, D_MODEL))] + [ANY] * len(scattered),
        out_shape=[jax.ShapeDtypeStruct((t, D_MODEL), f32), jax.ShapeDtypeStruct((8, D_MODEL), f32)] +
                  [_exchange_out_shape("scatter", a) for a in scattered],
        scratch_shapes=_exchange_sems(len(scattered)), compiler_params=_cparams(("arbitrary",)),
    )(x, dout, norm_w, w_perm, *dgroups, *scattered)


def grad_w_group(u, dg, name, scattered=()):
    t, n = dg.shape
    tn = 512 if n % 512 == 0 else n
    tm = 2048 if t % 2048 == 0 else t
    nj, nk = n // tn, t // tm
    f32 = jnp.float32

    def body(u_ref, d_ref, o_ref, acc):
        k = pl.program_id(1)

        @pl.when(k == 0)
        def _():
            acc[...] = jnp.zeros_like(acc)

        acc[...] += lax.dot_general(d_ref[...].astype(_MM), u_ref[...], (((0,), (0,)), ((), ())),
                                    preferred_element_type=f32)

        @pl.when(k == nk - 1)
        def _():
            o_ref[...] = acc[...].astype(o_ref.dtype)

    ne = len(scattered)
    hosted = _hosting(body, 2, 1, 1, ["scatter"] * ne,
                      lambda: (pl.program_id(0) == 0) & (pl.program_id(1) == 0),
                      lambda: (pl.program_id(0) == nj - 1) & (pl.program_id(1) == nk - 1))
    res = (_pc_comm if ne else _pc)(
        hosted, name=name, grid=(nj, nk),
        in_specs=[pl.BlockSpec((tm, D_MODEL), lambda j, k: (k, 0)),
                  pl.BlockSpec((tm, tn), lambda j, k: (k, j))] + [ANY] * ne,
        out_specs=[pl.BlockSpec((tn, D_MODEL), lambda j, k: (j, 0))] + [ANY] * ne,
        out_shape=[jax.ShapeDtypeStruct((n, D_MODEL), _MM)] + [_exchange_out_shape("scatter", a) for a in scattered],
        scratch_shapes=[pltpu.VMEM((tn, D_MODEL), f32)] + _exchange_sems(ne),
        compiler_params=_cparams(("arbitrary", "arbitrary")),
    )(u, dg, *scattered)
    return res if ne else res[0]


def _pad_lanes(v, off):
    n = v.shape[-1]
    return jnp.pad(v.reshape(1, n).astype(jnp.float32), ((0, 0), (off, 128 - off - n)))


REF_ROWS = dict(z=(0, 1024), xbc=(1024, 2560), dt=(2560, 2576), gate=(2576, 3600), qkv=(3600, 6672), ab=(6672, 6688))


def perm_w_in(gathered):
    wt = gathered.reshape(IN_DIM, gathered.shape[2])
    pieces = [wt[s:e] for s, e in (REF_ROWS[name] for name in ("z", "xbc", "gate", "qkv", "dt", "ab"))]
    pieces.append(jnp.zeros((PERM_DIM - IN_DIM, wt.shape[1]), wt.dtype))
    return jnp.concatenate(pieces, axis=0)


def unperm_w_in(gz, gxbc, ggate, gqkv, gsm):
    src = dict(z=gz, xbc=gxbc, dt=gsm[0:16], gate=ggate, qkv=gqkv, ab=gsm[16:32])
    slabs = []
    for k in range(N_DEV):
        a, b = k * W_IN_SHARD, (k + 1) * W_IN_SHARD
        parts = []
        for name, (s, e) in REF_ROWS.items():
            lo, hi = max(a, s), min(b, e)
            if lo < hi:
                parts.append(src[name][lo - s:hi - s])
        slabs.append(jnp.concatenate(parts, axis=0))
    return jnp.stack(slabs)


def all_gather(arrs, name):
    n = len(arrs)

    def body(*refs):
        ins, outs = refs[:n], refs[n:2 * n]
        send_sems, recv_sems, local_sems = refs[2 * n:]
        x, y, c, me = _me()
        sibling = (x, y, 1 - c)
        chips = [(1 - x, y), (x, 1 - y), (1 - x, 1 - y)]

        def idx(px, py, pc):
            return 4 * px + 2 * py + pc

        def copy(a, k, block, to, src=None):
            slot = outs[a].at[idx(*block)]
            return pltpu.make_async_remote_copy(src_ref=slot if src is None else src, dst_ref=slot,
                                                send_sem=send_sems.at[a, k], recv_sem=recv_sems.at[a, k],
                                                device_id=to, device_id_type=MESH)

        local = [pltpu.make_async_copy(ins[a], outs[a].at[me], local_sems.at[a]) for a in range(n)]
        for cp in local:
            cp.start()
        started = []
        for a in range(n):
            first = [copy(a, 0, (x, y, c), sibling, src=ins[a])]
            first += [copy(a, 1 + j, (x, y, c), (*chip, c), src=ins[a]) for j, chip in enumerate(chips)]
            for cp in first:
                cp.start()
            started += first
        for a in range(n):
            for j, chip in enumerate(chips):
                copy(a, 1 + j, (*chip, c), (x, y, c)).wait_recv()
                fwd = copy(a, 4 + j, (*chip, c), sibling)
                fwd.start()
                started.append(fwd)
        for a in range(n):
            copy(a, 0, sibling, (x, y, c)).wait_recv()
            for j, chip in enumerate(chips):
                copy(a, 4 + j, (*chip, 1 - c), (x, y, c)).wait_recv()
        for cp in started:
            cp.wait_send()
        for cp in local:
            cp.wait()

    return _pc_comm(
        body, name=name, in_specs=[ANY] * n, out_specs=[ANY] * n,
        out_shape=[jax.ShapeDtypeStruct((N_DEV,) + a.shape, a.dtype) for a in arrs],
        scratch_shapes=[pltpu.SemaphoreType.DMA((n, 7)), pltpu.SemaphoreType.DMA((n, 7)),
                        pltpu.SemaphoreType.DMA((n,))],
    )(*arrs)


def adamw_sum(recv, w, m, v, rows, name, cols=None):
    r, ccols = w.shape
    f32 = jnp.float32
    c1 = 1.0 / (1.0 - ADAM_B1 ** ADAM_STEP)
    c2 = 1.0 / (1.0 - ADAM_B2 ** ADAM_STEP)

    def body(recv_ref, w_ref, m_ref, v_ref, g_ref, d_ref, mo_ref, vo_ref):
        g = recv_ref[0].astype(f32)
        for k in range(1, N_DEV):
            g = g + recv_ref[k].astype(f32)
        mn = ADAM_B1 * m_ref[...] + (1.0 - ADAM_B1) * g
        vn = ADAM_B2 * v_ref[...] + (1.0 - ADAM_B2) * (g * g)
        g_ref[...] = g
        mo_ref[...] = mn
        vo_ref[...] = vn
        d_ref[...] = -ADAM_LR * ((mn * c1) / (jnp.sqrt(vn * c2) + ADAM_EPS) + ADAM_WD * w_ref[...])

    if cols is None:
        blk = pl.BlockSpec((rows, ccols), lambda i: (i, 0))
        rblk, steps = pl.BlockSpec((N_DEV, rows, ccols), lambda i: (0, i, 0)), r // rows
    else:
        blk = pl.BlockSpec((r, cols), lambda i: (0, i))
        rblk, steps = pl.BlockSpec((N_DEV, r, cols), lambda i: (0, 0, i)), ccols // cols
    return _pc(
        body, name=name, grid=(steps,),
        in_specs=[rblk, blk, blk, blk],
        out_specs=[blk] * 4, out_shape=[jax.ShapeDtypeStruct((r, ccols), f32)] * 4,
        compiler_params=_cparams(("arbitrary",)),
    )(recv, w, m, v)


SMALL = (("norm_w", 1, 1024, 0), ("ssd_conv_b", 1, 1536, 0), ("ssd_dt_bias", 1, 16, 0), ("ssd_a_log", 1, 16, 0),
         ("ssd_d", 1, 16, 0), ("ssd_norm_w", 1, 1024, 0), ("gdn_dt_bias", 1, 8, 16), ("gdn_a_log", 1, 8, 16),
         ("gdn_norm_w", 1, 128, 0), ("final_norm_w", 1, 1024, 0),
         ("ssd_conv_w", CONV_K, SSD_CONV_DIM // N_DEV, 0), ("gdn_conv_w", CONV_K, GDN_CONV_DIM // N_DEV, 0))


def _small_layout():
    out, off = [], 0
    for name, rows, n, lane0 in SMALL:
        stride = -(-(lane0 + n) // 128) * 128
        out.append((name, rows, n, lane0, stride, off))
        off += rows * stride
    return out, off


def scatter_small(accs):
    layout, total = _small_layout()
    f32 = jnp.float32

    def body(*refs):
        acc_refs, out_ref, slabs = refs[:len(layout)], refs[len(layout)], refs[len(layout) + 1]
        sems = refs[len(layout) + 2:]
        slabs[...] = jnp.zeros_like(slabs)
        for (name, rows, n, lane0, stride, off), acc in zip(layout, acc_refs):
            for k in range(N_DEV):
                if rows == 1:
                    slabs[k, :, off:off + stride] = acc[0:1, 0:stride]
                else:
                    for j in range(rows):
                        slabs[k, :, off + stride * j:off + stride * j + n] = acc[j:j + 1, n * k:n * k + n]
        start, wait = _exchange_ops("scatter", slabs, out_ref, *sems)
        start()
        wait()

    return _pc_comm(
        body, name="scatter_small_grads", out_specs=ANY, out_shape=jax.ShapeDtypeStruct((N_DEV, 1, total), f32),
        scratch_shapes=[pltpu.VMEM((N_DEV, 1, total), f32)] + _exchange_sems(1),
    )(*accs)


def adamw_small(recv, w, m, v):
    layout, total = _small_layout()
    f32 = jnp.float32
    c1 = 1.0 / (1.0 - ADAM_B1 ** ADAM_STEP)
    c2 = 1.0 / (1.0 - ADAM_B2 ** ADAM_STEP)
    np_ = len(layout)

    def body(*refs):
        recv_ref = refs[0]
        w_refs, m_refs, v_refs = refs[1:1 + np_], refs[1 + np_:1 + 2 * np_], refs[1 + 2 * np_:1 + 3 * np_]
        o_refs = refs[1 + 3 * np_:]
        g_all = recv_ref[0]
        for k in range(1, N_DEV):
            g_all = g_all + recv_ref[k]

        def update(g, wv, mv, vv):
            mn = ADAM_B1 * mv + (1.0 - ADAM_B1) * g
            vn = ADAM_B2 * vv + (1.0 - ADAM_B2) * (g * g)
            return g, -ADAM_LR * ((mn * c1) / (jnp.sqrt(vn * c2) + ADAM_EPS) + ADAM_WD * wv), mn, vn

        for p, (name, rows, n, lane0, stride, off) in enumerate(layout):
            outs = o_refs[4 * p:4 * p + 4]
            if rows == 1:
                res = update(g_all[:, off + lane0:off + lane0 + n], w_refs[p][...], m_refs[p][...], v_refs[p][...])
                for o, r in zip(outs, res):
                    o[...] = r
            else:
                for j in range(rows):
                    res = update(g_all[:, off + stride * j:off + stride * j + n], w_refs[p][0, j:j + 1, :],
                                 m_refs[p][0, j:j + 1, :], v_refs[p][0, j:j + 1, :])
                    for o, r in zip(outs, res):
                        o[0, j:j + 1, :] = r

    names = [e[0] for e in layout]
    ins = [recv] + [d[nm] for d in (w, m, v) for nm in names]
    out_shape = [jax.ShapeDtypeStruct(w[nm].shape, f32) for nm in names for _ in range(4)]
    res = _pc(body, name="adamw_small", out_shape=out_shape)(*ins)
    return {nm: tuple(res[4 * p:4 * p + 4]) for p, nm in enumerate(names)}


SHARD = (("ssd_conv_w", CONV_K * SSD_CONV_DIM // N_DEV), ("gdn_conv_w", CONV_K * GDN_CONV_DIM // N_DEV))
SHARD_ROWS = 24


def _rows_of(size):
    return -(-size // 128)


def _pack(vals, layout, total_rows):
    parts = []
    for (name, size), val in zip(layout, vals):
        flat = val.reshape(-1).astype(jnp.float32)
        parts.append(jnp.pad(flat, (0, _rows_of(size) * 128 - size)).reshape(-1, 128))
    used = sum(_rows_of(s) for _, s in layout)
    parts.append(jnp.zeros((total_rows - used, 128), jnp.float32))
    return jnp.concatenate(parts, axis=0)


def _conv_full(gathered_flat, ccols):
    return gathered_flat.reshape(N_DEV, CONV_K, ccols // N_DEV).transpose(1, 0, 2).reshape(CONV_K, ccols)


def kernel(x, norm_w, w_in, ssd_conv_w, ssd_conv_b, ssd_dt_bias, ssd_a_log, ssd_d, ssd_norm_w, gdn_conv_w, gdn_dt_bias, gdn_a_log, gdn_norm_w, w_out, final_norm_w, loss_target, m_norm_w, m_w_in, m_ssd_conv_w, m_ssd_conv_b, m_ssd_dt_bias, m_ssd_a_log, m_ssd_d, m_ssd_norm_w, m_gdn_conv_w, m_gdn_dt_bias, m_gdn_a_log, m_gdn_norm_w, m_w_out, m_final_norm_w, v_norm_w, v_w_in, v_ssd_conv_w, v_ssd_conv_b, v_ssd_dt_bias, v_ssd_a_log, v_ssd_d, v_ssd_norm_w, v_gdn_conv_w, v_gdn_dt_bias, v_gdn_a_log, v_gdn_norm_w, v_w_out, v_final_norm_w):
    f32 = jnp.float32
    w = dict(norm_w=norm_w, w_in=w_in, ssd_conv_w=ssd_conv_w, ssd_conv_b=ssd_conv_b, ssd_dt_bias=ssd_dt_bias,
             ssd_a_log=ssd_a_log, ssd_d=ssd_d, ssd_norm_w=ssd_norm_w, gdn_conv_w=gdn_conv_w, gdn_dt_bias=gdn_dt_bias,
             gdn_a_log=gdn_a_log, gdn_norm_w=gdn_norm_w, w_out=w_out, final_norm_w=final_norm_w)
    m = dict(norm_w=m_norm_w, w_in=m_w_in, ssd_conv_w=m_ssd_conv_w, ssd_conv_b=m_ssd_conv_b, ssd_dt_bias=m_ssd_dt_bias,
             ssd_a_log=m_ssd_a_log, ssd_d=m_ssd_d, ssd_norm_w=m_ssd_norm_w, gdn_conv_w=m_gdn_conv_w,
             gdn_dt_bias=m_gdn_dt_bias, gdn_a_log=m_gdn_a_log, gdn_norm_w=m_gdn_norm_w, w_out=m_w_out,
             final_norm_w=m_final_norm_w)
    v = dict(norm_w=v_norm_w, w_in=v_w_in, ssd_conv_w=v_ssd_conv_w, ssd_conv_b=v_ssd_conv_b, ssd_dt_bias=v_ssd_dt_bias,
             ssd_a_log=v_ssd_a_log, ssd_d=v_ssd_d, ssd_norm_w=v_ssd_norm_w, gdn_conv_w=v_gdn_conv_w,
             gdn_dt_bias=v_gdn_dt_bias, gdn_a_log=v_gdn_a_log, gdn_norm_w=v_gdn_norm_w, w_out=v_w_out,
             final_norm_w=v_final_norm_w)
    names = list(w)
    shapes = {n: w[n].shape for n in names}

    xl, tgt = x[0], loss_target[0]
    cs = _consts()
    dtb_s = _pad_lanes(ssd_dt_bias, 0)
    alog_s = _pad_lanes(ssd_a_log, 0)
    dpar = _pad_lanes(ssd_d, 0)
    dtb_g = _pad_lanes(gdn_dt_bias, 16)
    alog_g = _pad_lanes(gdn_a_log, 16)
    nw_g = gdn_norm_w.reshape(1, 128)
    nw_s = ssd_norm_w.reshape(1, 1024)
    cb_s = ssd_conv_b.reshape(1, 1536)
    nw1 = norm_w.reshape(1, D_MODEL)

    (g_w_in,) = all_gather([w_in[0].T.astype(_MM)], "gather_w_in")
    w_perm = perm_w_in(g_w_in)
    conv_pack = _pack([w["ssd_conv_w"], w["gdn_conv_w"]], SHARD, SHARD_ROWS)
    u, z, xbc, gate, qkv, sm, g_w_out, g_conv = inproj_fwd(xl, nw1, w_perm, [w_out[0].astype(_MM), conv_pack])
    w_out_full = g_w_out.reshape(MIX_WIDTH, D_MODEL)
    ssd_cw = _conv_full(g_conv[:, 0:6].reshape(N_DEV, -1), SSD_CONV_DIM)
    gdn_cw = _conv_full(g_conv[:, 6:18].reshape(N_DEV, -1), GDN_CONV_DIM)

    nc = xl.shape[0] // CHUNK
    y_ssd, hs, pre_s, y_gdn, ss, ts, pre_g = _chunk_call(
        [ssd_fwd(z, xbc, sm, ssd_cw, cb_s, dtb_s, alog_s, dpar, nw_s, cs),
         gdn_fwd(gate, qkv, sm, gdn_cw, dtb_g, alog_g, nw_g, cs)], "scan_fwd", nc)
    dout, dys, dyg, g_wout, g_fnw, loss_l = out_fwd_bwd(xl, tgt, y_ssd, y_gdn, w_out_full,
                                                        final_norm_w.reshape(1, D_MODEL))
    (dz, dxbc, g_cw_s, g_cb_s, g_dtb_s, g_alog_s, g_d, g_nw_s,
     dgate, dqkv, dsm, g_cw_g, g_dtb_g, g_alog_g, g_nw_g) = _chunk_call(
        [ssd_bwd(z, xbc, pre_s, sm, hs, dys, ssd_cw, dtb_s, alog_s, dpar, nw_s, cs),
         gdn_bwd(gate, qkv, pre_g, sm, ss, ts, dyg, gdn_cw, dtb_g, alog_g, nw_g, cs)], "scan_bwd", nc)

    t_w_out = g_wout.reshape(N_DEV, MIX_WIDTH // N_DEV, D_MODEL)
    gws = {}
    for dg, (name, _, _) in zip((dz, dxbc, dgate, dsm), (GROUPS[0], GROUPS[1], GROUPS[2], GROUPS[4])):
        gws[name] = grad_w_group(u, dg, "grad_w_in_" + name)
    gws["qkv"], r_w_out = grad_w_group(u, dqkv, "grad_w_in_qkv", [t_w_out])
    t_w_in = unperm_w_in(gws["z"], gws["xbc"], gws["gate"], gws["qkv"], gws["sm"])
    dx, g_nw, r_w_in = inproj_bwd_dx(xl, dout, nw1, w_perm, (dz, dxbc, dgate, dqkv, dsm), [t_w_in])

    accs = dict(norm_w=g_nw, ssd_conv_b=g_cb_s, ssd_dt_bias=g_dtb_s, ssd_a_log=g_alog_s, ssd_d=g_d,
                ssd_norm_w=g_nw_s, gdn_dt_bias=g_dtb_g, gdn_a_log=g_alog_g, gdn_norm_w=g_nw_g, final_norm_w=g_fnw,
                ssd_conv_w=g_cw_s, gdn_conv_w=g_cw_g)
    r_small = scatter_small([accs[e[0]] for e in SMALL])

    o_w_in = adamw_sum(r_w_in, w_in[0].T, m_w_in[0].T, v_w_in[0].T, None, "adamw_w_in", cols=256)
    o_w_out = adamw_sum(r_w_out, w_out[0], m_w_out[0], v_w_out[0], 64, "adamw_w_out")
    row = lambda d: {n: (a.reshape(1, -1) if a.ndim == 1 else a) for n, a in d.items()}
    o_small = adamw_small(r_small, row(w), row(m), row(v))

    loss = lax.psum(loss_l[0, 0], ("x", "y", "c"))
    outs = [loss, dx[None]]
    for k in range(4):
        parts = {n: o_small[n][k] for n in o_small}
        parts["w_in"] = o_w_in[k].T
        parts["w_out"] = o_w_out[k]
        outs += [parts[n].reshape(shapes[n]) for n in names]
    return tuple(outs)
```

```python
import functools

import jax
import jax.numpy as jnp
import numpy as np
from jax import lax
from jax.experimental import pallas as pl
from jax.experimental.pallas import tpu as pltpu

_MM = jnp.bfloat16

D_MODEL = 1024
CHUNK = 64
CONV_K = 4
EPS = 1e-6
SSD_CONV_DIM = 1536
GDN_HEADS = 8
GDN_DK = 128
GDN_CONV_DIM = 3072
MIX_WIDTH = 2048
IN_DIM = 6688
N_DEV = 8
W_IN_SHARD = IN_DIM // N_DEV
PERM_DIM = 6784
HI = lax.Precision.HIGHEST
HIGH = lax.Precision.HIGH
VMEM_LIMIT = 56 * 1024 * 1024

ADAM_LR = 0.001
ADAM_B1 = 0.9
ADAM_B2 = 0.999
ADAM_EPS = 1e-08
ADAM_WD = 0.01
ADAM_STEP = 10


def _pc(body, **kw):
    return pl.pallas_call(body, **kw)


def _pc_comm(body, **kw):
    return pl.pallas_call(body, **kw)


def _cparams(sem):
    return pltpu.CompilerParams(dimension_semantics=sem, vmem_limit_bytes=VMEM_LIMIT)


def _sig(x):
    return 0.5 * jnp.tanh(0.5 * x) + 0.5


@jax.custom_vjp
def _sigmoid(x):
    return _sig(x)


def _sigmoid_fwd(x):
    s = _sig(x)
    return s, s


def _sigmoid_bwd(s, g):
    return (g * s * (1.0 - s),)


_sigmoid.defvjp(_sigmoid_fwd, _sigmoid_bwd)


@jax.custom_vjp
def _silu(x):
    return x * _sig(x)


def _silu_fwd(x):
    s = _sig(x)
    return x * s, (x, s)


def _silu_bwd(res, g):
    x, s = res
    return (g * (s * (1.0 + x * (1.0 - s))),)


_silu.defvjp(_silu_fwd, _silu_bwd)


def _softplus_impl(x):
    return jnp.maximum(x, 0.0) + jnp.log(1.0 + jnp.exp(-jnp.abs(x)))


@jax.custom_vjp
def _softplus(x):
    return _softplus_impl(x)


def _softplus_fwd(x):
    return _softplus_impl(x), x


def _softplus_bwd(x, g):
    return (g * _sig(x),)


_softplus.defvjp(_softplus_fwd, _softplus_bwd)


def _lane_bcast_impl(x, k):
    return jnp.broadcast_to(x[..., k:k + 1], x.shape)


@functools.partial(jax.custom_vjp, nondiff_argnums=(1,))
def _lane_bcast(x, k):
    return _lane_bcast_impl(x, k)


def _lane_bcast_fwd(x, k):
    return _lane_bcast_impl(x, k), None


def _lane_bcast_bwd(k, _, g):
    lane = lax.broadcasted_iota(jnp.int32, g.shape, g.ndim - 1)
    return (jnp.where(lane == k, jnp.sum(g, axis=-1, keepdims=True), 0.0),)


_lane_bcast.defvjp(_lane_bcast_fwd, _lane_bcast_bwd)


def _mm(a, b):
    return jnp.dot(a.astype(_MM), b.astype(_MM), preferred_element_type=jnp.float32)


def _mm_nt(a, b):
    return lax.dot_general(a.astype(_MM), b.astype(_MM), (((1,), (1,)), ((), ())),
                           preferred_element_type=jnp.float32)


def _mm_tn(a, b):
    return lax.dot_general(a.astype(_MM), b.astype(_MM), (((0,), (0,)), ((), ())),
                           preferred_element_type=jnp.float32)


def _dot_hi(a, b):
    return jnp.dot(a, b, precision=HI, preferred_element_type=jnp.float32)


def _bmm(a, b):
    return lax.dot_general(a.astype(_MM), b.astype(_MM), (((2,), (1,)), ((0,), (0,))),
                           preferred_element_type=jnp.float32)


def _bmm_nt(a, b):
    return lax.dot_general(a.astype(_MM), b.astype(_MM), (((2,), (2,)), ((0,), (0,))),
                           preferred_element_type=jnp.float32)


def _bmm_tn(a, b):
    return lax.dot_general(a.astype(_MM), b.astype(_MM), (((1,), (1,)), ((0,), (0,))),
                           preferred_element_type=jnp.float32)


def _bmm_hi(a, b):
    return lax.dot_general(a, b, (((2,), (1,)), ((0,), (0,))), precision=HIGH, preferred_element_type=jnp.float32)


def _bmm_nt_hi(a, b):
    return lax.dot_general(a, b, (((2,), (2,)), ((0,), (0,))), precision=HIGH, preferred_element_type=jnp.float32)


def _bmm_tn_hi(a, b):
    return lax.dot_general(a, b, (((1,), (1,)), ((0,), (0,))), precision=HIGH, preferred_element_type=jnp.float32)


def _consts():
    l = np.arange(CHUNK)
    tri = (l[:, None] >= l[None, :]).astype(np.float32)
    lane = np.arange(128)
    i2 =(l[:, None] == (lane[None, :] % 64)).astype(np.float32)
    mask2 = (l[:, None] >= (lane[None, :] % 64)).astype(np.float32)
    lo = (lane < 64).astype(np.float32)[None, :]
    i64 = np.eye(CHUNK, dtype=np.float32)
    strict = (l[:, None] > l[None, :]).astype(np.float32)
    return dict(tri=jnp.asarray(tri), i2=jnp.asarray(i2), mask2=jnp.asarray(mask2), lo=jnp.asarray(lo),
                i64=jnp.asarray(i64), strict=jnp.asarray(strict))


def _ssd_chunk(xs_pre, b_pre, c_pre, z, sm, ht, dtb, alog, dpar, nw, tri, i2, mask2, lo):
    lane = lax.broadcasted_iota(jnp.int32, (1, 128), 1)
    m16 = lane < 16
    dt = jnp.where(m16, _softplus(sm + dtb), 0.0)
    a_neg = -jnp.exp(alog)
    cum = _dot_hi(tri, dt * a_neg)
    row = lax.broadcasted_iota(jnp.int32, (CHUNK, 1), 0)
    is_last = row == CHUNK - 1
    hi = 1.0 - lo
    bm = [_silu(b) for b in b_pre]
    cm = [_silu(c) for c in c_pre]
    cb2 = [_mm_nt(cm[g], jnp.concatenate([bm[g], bm[g]], axis=0)) for g in range(2)]
    yg, ht_next = [], []
    for j in range(8):
        g = j // 4
        pair = lambda v, j=j: jnp.where(lo > 0.5, _lane_bcast(v, 2 * j), _lane_bcast(v, 2 * j + 1))
        xs = _silu(xs_pre[j])
        dte = pair(dt)
        cume = pair(cum)
        cum_last = jnp.sum(jnp.where(is_last, cume, 0.0), axis=0, keepdims=True)
        xdt = xs * dte
        rowv = jnp.sum(cume * i2, axis=0, keepdims=True)
        lm = jnp.exp(jnp.where(mask2 > 0.5, cume - rowv, -jnp.inf))
        m = cb2[g] * lm
        xblk = jnp.concatenate([xdt * lo, xdt * hi], axis=0)
        y = _mm(m, xblk)
        y = y + _mm(cm[g], ht[j]) * jnp.exp(cume)
        y = y + pair(dpar) * xs
        yg.append(y * _silu(z[j]))
        st = _mm_tn(bm[g], xdt * jnp.exp(cum_last - cume))
        ht_next.append(ht[j] * jnp.exp(cum_last) + st)
    outs = []
    for g in range(2):
        ss = sum(jnp.sum(yg[j] * yg[j], axis=-1, keepdims=True) for j in range(4 * g, 4 * g + 4))
        rs = lax.rsqrt(ss * (1.0 / 512.0) + EPS)
        for j in range(4 * g, 4 * g + 4):
            outs.append(yg[j] * rs * nw[j])
    return outs, ht_next


def _tri_inverse(a):
    eye = jnp.eye(CHUNK, dtype=jnp.float32)[None]
    p = eye - a
    ap = a
    for _ in range(5):
        ap = _bmm_hi(ap, ap)
        p = p + _bmm_hi(p, ap)
    return p


@jax.custom_vjp
def _solve(a, r1, r2, t):
    return _bmm_hi(t, r1), _bmm_hi(t, r2)


def _solve_fwd(a, r1, r2, t):
    u, w = _bmm_hi(t, r1), _bmm_hi(t, r2)
    return (u, w), (t, u, w)


def _solve_bwd(res, cts):
    t, u, w = res
    du, dw = cts
    dr1 = _bmm_tn_hi(t, du)
    dr2 = _bmm_tn_hi(t, dw)
    da = -(_bmm_nt_hi(dr1, u) + _bmm_nt_hi(dr2, w))
    return da, dr1, dr2, jnp.zeros_like(t)


_solve.defvjp(_solve_fwd, _solve_bwd)


def _gdn_chunk(q_pre, k_pre, v_pre, gate, sm, s, dtb, alog, nw, tri, i64, strict, t_in=None):
    lane = lax.broadcasted_iota(jnp.int32, (1, 128), 1)
    m_a = (lane >= 16) & (lane < 24)
    g_full = jnp.where(m_a, -jnp.exp(alog) * _softplus(sm + dtb), 0.0)
    gc = _dot_hi(tri, g_full)
    sig = _sigmoid(sm)
    gc3 = jnp.stack([_lane_bcast(gc, 16 + h) for h in range(GDN_HEADS)])
    beta3 = jnp.stack([_lane_bcast(sig, 24 + h) for h in range(GDN_HEADS)])
    q = _silu(q_pre)
    q = q * lax.rsqrt(jnp.sum(q * q, axis=-1, keepdims=True) + EPS) * (GDN_DK ** -0.5)
    k = _silu(k_pre)
    k = k * lax.rsqrt(jnp.sum(k * k, axis=-1, keepdims=True) + EPS)
    v = _silu(v_pre)
    gcl = gc3[:, :, :CHUNK]
    gc_row = jnp.sum(gcl * i64[None], axis=1, keepdims=True)
    incl = (strict + i64)[None] > 0.5
    decay = jnp.exp(jnp.where(incl, gcl - gc_row, -jnp.inf))
    kb = k * beta3
    a = jnp.where(strict[None] > 0.5, _bmm_nt(kb, k) * decay, 0.0)
    egc = jnp.exp(gc3)
    t = _tri_inverse(a) if t_in is None else t_in
    u, w = _solve(a, v * beta3, kb * egc, t)
    attn = _bmm_nt(q, k) * decay
    row = lax.broadcasted_iota(jnp.int32, (1, CHUNK, 1), 1)
    gl = jnp.sum(jnp.where(row == CHUNK - 1, gc3, 0.0), axis=1, keepdims=True)
    q_dec = q * egc
    k_dec = k * jnp.exp(gl - gc3)
    v_new = u - _bmm(w, s)
    o = _bmm(q_dec, s) + _bmm(attn, v_new)
    s_next = s * jnp.exp(gl) + _bmm_tn(k_dec, v_new)
    on = o * lax.rsqrt(jnp.mean(o * o, axis=-1, keepdims=True) + EPS) * nw
    return on * _silu(gate), s_next, t


def _conv_fwd(pbuf, w_ref, c0, c1):
    acc = None
    for j in range(CONV_K):
        term = w_ref[j:j + 1, c0:c1] * pbuf[5 + j:69 + j, c0:c1]
        acc = term if acc is None else acc + term
    return acc


MESH = pl.DeviceIdType.MESH
ANY = pl.BlockSpec(memory_space=pl.ANY)


def _me():
    x, y, c = lax.axis_index("x"), lax.axis_index("y"), lax.axis_index("c")
    return x, y, c, 4 * x + 2 * y + c


def _peer(r):
    x, y, c, _ = _me()
    px = 1 - x if r & 4 else x
    py = 1 - y if r & 2 else y
    pc = 1 - c if r & 1 else c
    return (px, py, pc), 4 * px + 2 * py + pc


def _exchange_ops(kind, in_ref, out_ref, send_sems, recv_sems, local_sem):
    me = _me()[3]
    local = pltpu.make_async_copy(in_ref.at[me] if kind == "scatter" else in_ref, out_ref.at[me], local_sem)
    sends, recvs = [], []
    for r in range(1, N_DEV):
        peer, pidx = _peer(r)
        src = in_ref.at[pidx] if kind == "scatter" else in_ref
        sems = dict(send_sem=send_sems.at[r - 1], recv_sem=recv_sems.at[r - 1], device_id=peer, device_id_type=MESH)
        sends.append(pltpu.make_async_remote_copy(src_ref=src, dst_ref=out_ref.at[me], **sems))
        recvs.append(pltpu.make_async_remote_copy(src_ref=src, dst_ref=out_ref.at[pidx], **sems))

    def start():
        local.start()
        for cp in sends:
            cp.start()

    def wait():
        for cp in recvs:
            cp.wait_recv()
        for cp in sends:
            cp.wait_send()
        local.wait()

    return start, wait


def _exchange_sems(n):
    return [pltpu.SemaphoreType.DMA((N_DEV - 1,)), pltpu.SemaphoreType.DMA((N_DEV - 1,)),
            pltpu.SemaphoreType.DMA(())] * n


def _exchange_out_shape(kind, a):
    return jax.ShapeDtypeStruct(a.shape if kind == "scatter" else (N_DEV,) + a.shape, a.dtype)


def _hosting(body, n_in, n_out, n_scratch, kinds, first, last):
    ne = len(kinds)

    def wrapped(*refs):
        ins, ex_in = refs[:n_in], refs[n_in:n_in + ne]
        o0 = n_in + ne
        outs, ex_out = refs[o0:o0 + n_out], refs[o0 + n_out:o0 + n_out + ne]
        s0 = o0 + n_out + ne
        scr, sems = refs[s0:s0 + n_scratch], refs[s0 + n_scratch:]
        ops = [_exchange_ops(kinds[e], ex_in[e], ex_out[e], *sems[3 * e:3 * e + 3]) for e in range(ne)]

        @pl.when(first())
        def _():
            for start, _ in ops:
                start()

        body(*ins, *outs, *scr)

        @pl.when(last())
        def _():
            for _, wait in ops:
                wait()

    return wrapped


GROUPS = (("z", 0, 1024), ("xbc", 1024, 2560), ("gate", 2560, 3584), ("qkv", 3584, 6656), ("sm", 6656, 6784))


def inproj_fwd(x, norm_w, w_perm, gathered):
    t = x.shape[0]
    tm = min(512, t)
    steps = t // tm
    kinds = ["gather"] * len(gathered)

    def body(x_ref, nw_ref, w_ref, u_ref, z_ref, xbc_ref, gate_ref, qkv_ref, sm_ref):
        xf = x_ref[...]
        rstd = lax.rsqrt(jnp.mean(xf * xf, axis=-1, keepdims=True) + EPS)
        u = (xf * rstd * nw_ref[...]).astype(_MM)
        u_ref[...] = u
        for (name, c0, c1), o_ref in zip(GROUPS, (z_ref, xbc_ref, gate_ref, qkv_ref, sm_ref)):
            o_ref[...] = lax.dot_general(u, w_ref[c0:c1, :], (((1,), (1,)), ((), ())),
                                         preferred_element_type=jnp.float32)

    outs = [jax.ShapeDtypeStruct((t, D_MODEL), _MM)] + [jax.ShapeDtypeStruct((t, c1 - c0), jnp.float32)
                                                        for _, c0, c1 in GROUPS]
    hosted = _hosting(body, 3, 6, 0, kinds, lambda: pl.program_id(0) == 0, lambda: pl.program_id(0) == steps - 1)
    return _pc_comm(
        hosted, name="inproj_fwd", grid=(steps,),
        in_specs=[pl.BlockSpec((tm, D_MODEL), lambda i: (i, 0)),
                  pl.BlockSpec((1, D_MODEL), lambda i: (0, 0)),
                  pl.BlockSpec((PERM_DIM, D_MODEL), lambda i: (0, 0), pipeline_mode=pl.Buffered(1))] +
                 [ANY] * len(gathered),
        out_specs=[pl.BlockSpec((tm, D_MODEL), lambda i: (i, 0))] +
                  [pl.BlockSpec((tm, c1 - c0), lambda i: (i, 0)) for _, c0, c1 in GROUPS] + [ANY] * len(gathered),
        out_shape=outs + [_exchange_out_shape("gather", a) for a in gathered],
        scratch_shapes=_exchange_sems(len(gathered)), compiler_params=_cparams(("arbitrary",)),
    )(x, norm_w, w_perm, *gathered)


SUB_FWD = 4
SUB_BWD = 2


def _halo_spec(width, idx_fn):
    return pl.BlockSpec((8, width), lambda i: (jnp.maximum(idx_fn(i) * (SUB_FWD * CHUNK // 8) - 1, 0), 0))


def _when_first(shared, fn):
    if shared["first"] is not False:
        pl.when(shared["first"])(fn)


def _full(shape):
    nd = len(shape)
    return pl.BlockSpec(shape, lambda i: (0,) * nd)


def _ssd_split(pre_fn, z_ref, sm_ref):
    xs_pre = [pre_fn(128 * j, 128 * j + 128) for j in range(8)]
    b_pre = [pre_fn(1024 + 128 * g, 1152 + 128 * g) for g in range(2)]
    c_pre = [pre_fn(1280 + 128 * g, 1408 + 128 * g) for g in range(2)]
    z = [z_ref[:, 128 * j:128 * j + 128] for j in range(8)]
    return xs_pre, b_pre, c_pre, z, sm_ref[...]


def ssd_fwd(z, xbc, sm, conv_w, conv_b, dtb, alog, dpar, nw, cs):
    t = z.shape[0]
    nc = t // CHUNK

    def body(shared, z_ref, xbc_ref, halo_ref, sm_ref, cw_ref, cb_ref, dtb_ref, alog_ref, dpar_ref, nw_ref,
             tri_ref, i2_ref, mask2_ref, lo_ref, y_ref, hs_ref, pre_ref, pbuf, ht_scr):
        def init():
            ht_scr[...] = jnp.zeros_like(ht_scr)

        _when_first(shared, init)
        pbuf[0:8, :] = jnp.where(shared["first"], 0.0, halo_ref[...])
        pbuf[8:72, :] = xbc_ref[...]

        def pre_fn(c0, c1):
            pre = _conv_fwd(pbuf, cw_ref, c0, c1) + cb_ref[:, c0:c1]
            pre_ref[:, c0:c1] = pre
            return pre

        xs_pre, b_pre, c_pre, zz, smv = _ssd_split(pre_fn, z_ref, sm_ref)
        ht = [ht_scr[:, 128 * j:128 * j + 128] for j in range(8)]
        hs_ref[0] = ht_scr[...]
        nwl = [nw_ref[:, 128 * j:128 * j + 128] for j in range(8)]
        outs, ht_next = _ssd_chunk(xs_pre, b_pre, c_pre, zz, smv, ht, dtb_ref[...], alog_ref[...], dpar_ref[...],
                                   nwl, tri_ref[...], i2_ref[...], mask2_ref[...], lo_ref[...])
        for j in range(8):
            y_ref[:, 128 * j:128 * j + 128] = outs[j].astype(y_ref.dtype)
            ht_scr[:, 128 * j:128 * j + 128] = ht_next[j]

    blk = lambda w: pl.BlockSpec((SUB_FWD * CHUNK, w), lambda i: (i, 0))
    return dict(
        body=body,
        in_kinds=["rows", "rows", ("halo", 1), "rows"] + ["full"] * 10, out_kinds=["rows", "state", "rows"],
        in_specs=[blk(1024), blk(1536), _halo_spec(1536, lambda i: i), blk(128),
                  _full((CONV_K, 1536)), _full((1, 1536)), _full((1, 128)), _full((1, 128)), _full((1, 128)),
                  _full((1, 1024)), _full((64, 64)), _full((64, 128)), _full((64, 128)),
                  _full((1, 128))],
        out_specs=[blk(1024), pl.BlockSpec((SUB_FWD, 128, 1024), lambda i: (i, 0, 0)), blk(1536)],
        out_shape=[jax.ShapeDtypeStruct((t, 1024), _MM), jax.ShapeDtypeStruct((nc, 128, 1024), jnp.float32),
                   jax.ShapeDtypeStruct((t, 1536), jnp.float32)],
        scratch=[pltpu.VMEM((72, 1536), jnp.float32), pltpu.VMEM((128, 1024), jnp.float32)],
        args=[z, xbc, xbc, sm, conv_w, conv_b, dtb, alog, dpar, nw, cs["tri"], cs["i2"], cs["mask2"], cs["lo"]])


def _conv_bwd(dpre_list, col_ranges, dbuf, carry, x_ref, cw_ref, dx_ref, dcw_ref, dcb_ref, first):
    for dpre, (c0, c1) in zip(dpre_list, col_ranges):
        dbuf[0:64, c0:c1] = dpre
    dbuf[64:72, :] = jnp.where(first, 0.0, carry[...])
    carry[...] = dbuf[0:8, :]
    for (c0, c1) in col_ranges:
        xin = x_ref[:, c0:c1]
        acc = None
        for j in range(CONV_K):
            sh = dbuf[3 - j:67 - j, c0:c1]
            term = cw_ref[j:j + 1, c0:c1] * sh
            acc = term if acc is None else acc + term
            dcw_ref[j:j + 1, c0:c1] += jnp.sum(xin * sh, axis=0, keepdims=True)
        dx_ref[:, c0:c1] = acc.astype(dx_ref.dtype)
        if dcb_ref is not None:
            dcb_ref[0:1, c0:c1] += jnp.sum(dbuf[0:64, c0:c1], axis=0, keepdims=True)


def ssd_bwd(z, xbc, pre, sm, hs, dy, conv_w, dtb, alog, dpar, nw, cs):
    t = z.shape[0]
    nc = t // CHUNK

    def body(shared, z_ref, xbc_ref, pre_ref, sm_ref, hs_ref, dy_ref, cw_ref, dtb_ref, alog_ref, dpar_ref, nw_ref,
             tri_ref, i2_ref, mask2_ref, lo_ref,
             dz_ref, dxbc_ref, dcw_ref, dcb_ref, ddtb_ref, dalog_ref, ddpar_ref, dnw_ref,
             dbuf, carry, dht_scr):
        def init():
            dht_scr[...] = jnp.zeros_like(dht_scr)
            dcw_ref[...] = jnp.zeros_like(dcw_ref)
            dcb_ref[...] = jnp.zeros_like(dcb_ref)
            ddtb_ref[...] = jnp.zeros_like(ddtb_ref)
            dalog_ref[...] = jnp.zeros_like(dalog_ref)
            ddpar_ref[...] = jnp.zeros_like(ddpar_ref)
            dnw_ref[...] = jnp.zeros_like(dnw_ref)

        _when_first(shared, init)
        pre_fn = lambda c0, c1: pre_ref[:, c0:c1]
        xs_pre, b_pre, c_pre, zz, smv = _ssd_split(pre_fn, z_ref, sm_ref)
        ht = [hs_ref[0, :, 128 * j:128 * j + 128] for j in range(8)]
        nwl = [nw_ref[:, 128 * j:128 * j + 128] for j in range(8)]
        consts = (tri_ref[...], i2_ref[...], mask2_ref[...], lo_ref[...])

        def f(xs_pre, b_pre, c_pre, zz, smv, ht, dtb, alog, dpar, nwl):
            return _ssd_chunk(xs_pre, b_pre, c_pre, zz, smv, ht, dtb, alog, dpar, nwl, *consts)

        _, vjp = jax.vjp(f, xs_pre, b_pre, c_pre, zz, smv, ht, dtb_ref[...], alog_ref[...], dpar_ref[...], nwl)
        dys = [dy_ref[:, 128 * j:128 * j + 128] for j in range(8)]
        dhts = [dht_scr[:, 128 * j:128 * j + 128] for j in range(8)]
        dxs, db, dc, dzz, dsm, dht, ddtb, dalog, ddpar, dnwl = vjp((dys, dhts))
        for j in range(8):
            dz_ref[:, 128 * j:128 * j + 128] = dzz[j].astype(dz_ref.dtype)
            dht_scr[:, 128 * j:128 * j + 128] = dht[j]
            dnw_ref[0:1, 128 * j:128 * j + 128] += dnwl[j]
        shared["dsm_ssd"] = dsm
        ddtb_ref[0:1, :] += ddtb
        dalog_ref[0:1, :] += dalog
        ddpar_ref[0:1, :] += ddpar
        ranges = ([(128 * j, 128 * j + 128) for j in range(8)] + [(1024 + 128 * g, 1152 + 128 * g) for g in range(2)]
                  + [(1280 + 128 * g, 1408 + 128 * g) for g in range(2)])
        _conv_bwd(dxs + db + dc, ranges, dbuf, carry, xbc_ref, cw_ref, dxbc_ref, dcw_ref, dcb_ref, shared["first"])

    ns = nc // SUB_BWD
    rblk = lambda w: pl.BlockSpec((SUB_BWD * CHUNK, w), lambda i: (ns - 1 - i, 0))
    acc = lambda w: pl.BlockSpec((8, w), lambda i: (0, 0))
    f32 = jnp.float32
    return dict(
        body=body,
        in_kinds=["rows"] * 4 + ["state", "rows"] + ["full"] * 9, out_kinds=["rows", "rows"] + ["full"] * 6,
        in_specs=[rblk(1024), rblk(1536), rblk(1536), rblk(128),
                  pl.BlockSpec((SUB_BWD, 128, 1024), lambda i: (ns - 1 - i, 0, 0)), rblk(1024),
                  _full((CONV_K, 1536)), _full((1, 128)), _full((1, 128)), _full((1, 128)),
                  _full((1, 1024)), _full((64, 64)), _full((64, 128)), _full((64, 128)),
                  _full((1, 128))],
        out_specs=[rblk(1024), rblk(1536), acc(1536), acc(1536), acc(128), acc(128), acc(128), acc(1024)],
        out_shape=[jax.ShapeDtypeStruct((t, 1024), f32), jax.ShapeDtypeStruct((t, 1536), f32),
                   jax.ShapeDtypeStruct((8, 1536), f32),
                   jax.ShapeDtypeStruct((8, 1536), f32), jax.ShapeDtypeStruct((8, 128), f32),
                   jax.ShapeDtypeStruct((8, 128), f32), jax.ShapeDtypeStruct((8, 128), f32),
                   jax.ShapeDtypeStruct((8, 1024), f32)],
        scratch=[pltpu.VMEM((72, 1536), f32), pltpu.VMEM((8, 1536), f32), pltpu.VMEM((128, 1024), f32)],
        args=[z, xbc, pre, sm, hs, dy, conv_w, dtb, alog, dpar, nw, cs["tri"], cs["i2"], cs["mask2"], cs["lo"]])


def _gdn_split(pre_fn, gate_ref):
    def heads(base):
        return jnp.stack([pre_fn(base + 128 * h, base + 128 * h + 128) for h in range(GDN_HEADS)])
    gate = jnp.stack([gate_ref[:, 128 * h:128 * h + 128] for h in range(GDN_HEADS)])
    return heads(0), heads(1024), heads(2048), gate


def gdn_fwd(gate, qkv, sm, conv_w, dtb, alog, nw, cs):
    t = gate.shape[0]
    nc = t // CHUNK

    def body(shared, gate_ref, qkv_ref, halo_ref, sm_ref, cw_ref, dtb_ref, alog_ref, nw_ref,
             tri_ref, i64_ref, strict_ref, o_ref, ss_ref, ts_ref, pre_ref, pbuf, s_scr):
        def init():
            s_scr[...] = jnp.zeros_like(s_scr)

        _when_first(shared, init)
        pbuf[0:8, :] = jnp.where(shared["first"], 0.0, halo_ref[...])
        pbuf[8:72, :] = qkv_ref[...]

        def pre_fn(c0, c1):
            pre = _conv_fwd(pbuf, cw_ref, c0, c1)
            pre_ref[:, c0:c1] = pre
            return pre

        q_pre, k_pre, v_pre, g3 = _gdn_split(pre_fn, gate_ref)
        s = s_scr[...]
        ss_ref[0] = s
        out, s_next, tinv = _gdn_chunk(q_pre, k_pre, v_pre, g3, sm_ref[...], s, dtb_ref[...], alog_ref[...],
                                       nw_ref[...], tri_ref[...], i64_ref[...], strict_ref[...])
        ts_ref[0] = tinv
        s_scr[...] = s_next
        for h in range(GDN_HEADS):
            o_ref[:, 128 * h:128 * h + 128] = out[h].astype(o_ref.dtype)

    blk = lambda w: pl.BlockSpec((SUB_FWD * CHUNK, w), lambda i: (i, 0))
    return dict(
        body=body,
        in_kinds=["rows", "rows", ("halo", 1), "rows"] + ["full"] * 7, out_kinds=["rows", "state", "state", "rows"],
        in_specs=[blk(1024), blk(3072), _halo_spec(3072, lambda i: i), blk(128),
                  _full((CONV_K, 3072)), _full((1, 128)), _full((1, 128)), _full((1, 128)),
                  _full((64, 64)), _full((64, 64)), _full((64, 64))],
        out_specs=[blk(1024), pl.BlockSpec((SUB_FWD, 8, 128, 128), lambda i: (i, 0, 0, 0)),
                   pl.BlockSpec((SUB_FWD, 8, CHUNK, CHUNK), lambda i: (i, 0, 0, 0)), blk(3072)],
        out_shape=[jax.ShapeDtypeStruct((t, 1024), _MM), jax.ShapeDtypeStruct((nc, 8, 128, 128), jnp.float32),
                   jax.ShapeDtypeStruct((nc, 8, CHUNK, CHUNK), jnp.float32),
                   jax.ShapeDtypeStruct((t, 3072), jnp.float32)],
        scratch=[pltpu.VMEM((72, 3072), jnp.float32), pltpu.VMEM((8, 128, 128), jnp.float32)],
        args=[gate, qkv, qkv, sm, conv_w, dtb, alog, nw, cs["tri"], cs["i64"], cs["strict"]])


def gdn_bwd(gate, qkv, pre, sm, ss, ts, do, conv_w, dtb, alog, nw, cs):
    t = gate.shape[0]
    nc = t // CHUNK

    def body(shared, gate_ref, qkv_ref, pre_ref, sm_ref, ss_ref, ts_ref, do_ref, cw_ref, dtb_ref, alog_ref,
             nw_ref, tri_ref, i64_ref, strict_ref,
             dgate_ref, dqkv_ref, dsm_ref, dcw_ref, ddtb_ref, dalog_ref, dnw_ref,
             dbuf, carry, ds_scr):
        def init():
            ds_scr[...] = jnp.zeros_like(ds_scr)
            dcw_ref[...] = jnp.zeros_like(dcw_ref)
            ddtb_ref[...] = jnp.zeros_like(ddtb_ref)
            dalog_ref[...] = jnp.zeros_like(dalog_ref)
            dnw_ref[...] = jnp.zeros_like(dnw_ref)

        _when_first(shared, init)

        q_pre, k_pre, v_pre, g3 = _gdn_split(lambda c0, c1: pre_ref[:, c0:c1], gate_ref)
        consts = (tri_ref[...], i64_ref[...], strict_ref[...], ts_ref[0])

        def f(q_pre, k_pre, v_pre, g3, smv, s, dtb, alog, nwv):
            return _gdn_chunk(q_pre, k_pre, v_pre, g3, smv, s, dtb, alog, nwv, *consts)[:2]

        _, vjp = jax.vjp(f, q_pre, k_pre, v_pre, g3, sm_ref[...], ss_ref[0], dtb_ref[...], alog_ref[...], nw_ref[...])
        do3 = jnp.stack([do_ref[:, 128 * h:128 * h + 128] for h in range(GDN_HEADS)])
        dq, dk, dv, dg3, dsm, ds, ddtb, dalog, dnw = vjp((do3, ds_scr[...]))
        ds_scr[...] = ds
        for h in range(GDN_HEADS):
            dgate_ref[:, 128 * h:128 * h + 128] = dg3[h].astype(dgate_ref.dtype)
        dsm_ref[...] = (dsm + shared["dsm_ssd"]).astype(dsm_ref.dtype)
        ddtb_ref[0:1, :] += ddtb
        dalog_ref[0:1, :] += dalog
        dnw_ref[0:1, :] += dnw
        ranges = [(base + 128 * h, base + 128 * h + 128) for base in (0, 1024, 2048) for h in range(GDN_HEADS)]
        dlist = [d[h] for d in (dq, dk, dv) for h in range(GDN_HEADS)]
        _conv_bwd(dlist, ranges, dbuf, carry, qkv_ref, cw_ref, dqkv_ref, dcw_ref, None, shared["first"])

    ns = nc // SUB_BWD
    rblk = lambda w: pl.BlockSpec((SUB_BWD * CHUNK, w), lambda i: (ns - 1 - i, 0))
    acc = lambda w: pl.BlockSpec((8, w), lambda i: (0, 0))
    f32 = jnp.float32
    return dict(
        body=body,
        in_kinds=["rows"] * 4 + ["state", "state", "rows"] + ["full"] * 7, out_kinds=["rows"] * 3 + ["full"] * 4,
        in_specs=[rblk(1024), rblk(3072), rblk(3072), rblk(128),
                  pl.BlockSpec((SUB_BWD, 8, 128, 128), lambda i: (ns - 1 - i, 0, 0, 0)),
                  pl.BlockSpec((SUB_BWD, 8, CHUNK, CHUNK), lambda i: (ns - 1 - i, 0, 0, 0)), rblk(1024),
                  _full((CONV_K, 3072)), _full((1, 128)), _full((1, 128)), _full((1, 128)),
                  _full((64, 64)), _full((64, 64)), _full((64, 64))],
        out_specs=[rblk(1024), rblk(3072), rblk(128), acc(3072), acc(128), acc(128), acc(128)],
        out_shape=[jax.ShapeDtypeStruct((t, 1024), f32), jax.ShapeDtypeStruct((t, 3072), f32),
                   jax.ShapeDtypeStruct((t, 128), f32), jax.ShapeDtypeStruct((8, 3072), f32),
                   jax.ShapeDtypeStruct((8, 128), f32), jax.ShapeDtypeStruct((8, 128), f32),
                   jax.ShapeDtypeStruct((8, 128), f32)],
        scratch=[pltpu.VMEM((72, 3072), f32), pltpu.VMEM((8, 3072), f32), pltpu.VMEM((8, 128, 128), f32)],
        args=[gate, qkv, pre, sm, ss, ts, do, conv_w, dtb, alog, nw, cs["tri"], cs["i64"], cs["strict"]])


def _chunk_call(parts, name, nc, reverse):
    n_in = [len(p["args"]) for p in parts]
    n_out = [len(p["out_shape"]) for p in parts]
    n_scr = [len(p["scratch"]) for p in parts]
    sub = SUB_BWD if reverse else SUB_FWD
    order = list(range(sub))[::-1] if reverse else list(range(sub))

    def view(ref, kind, s, refs):
        if kind == "rows":
            return ref.at[pl.ds(CHUNK * s, CHUNK)]
        if kind == "state":
            return ref.at[pl.ds(s, 1)]
        if kind == "full":
            return ref
        src = refs[kind[1]]
        return ref if s == 0 else src.at[pl.ds(CHUNK * s - 8, 8)]

    def body(*refs):
        ins, outs, scr = refs[:sum(n_in)], refs[sum(n_in):sum(n_in) + sum(n_out)], refs[sum(n_in) + sum(n_out):]
        for s in order:
            shared = {"first": (pl.program_id(0) == 0) if s == order[0] else False}
            for k, p in enumerate(parts):
                i0, o0, s0 = sum(n_in[:k]), sum(n_out[:k]), sum(n_scr[:k])
                p_ins = ins[i0:i0 + n_in[k]]
                p["body"](shared,
                          *[view(r, kd, s, p_ins) for r, kd in zip(p_ins, p["in_kinds"])],
                          *[view(r, kd, s, None) for r, kd in zip(outs[o0:o0 + n_out[k]], p["out_kinds"])],
                          *scr[s0:s0 + n_scr[k]])

    cat = lambda key: [v for p in parts for v in p[key]]
    return _pc(body, name=name, grid=(nc // sub,), in_specs=cat("in_specs"), out_specs=cat("out_specs"),
               out_shape=cat("out_shape"), scratch_shapes=cat("scratch"),
               compiler_params=_cparams(("arbitrary",)))(*cat("args"))


def out_fwd_bwd(x, tgt, y_ssd, y_gdn, w_out, fnw):
    t = x.shape[0]
    tm = min(512, t)
    f32 = jnp.float32

    def body(x_ref, tgt_ref, ys_ref, yg_ref, w_ref, fnw_ref,
             dout_ref, dys_ref, dyg_ref, gw_ref, gfnw_ref, loss_ref, gw_acc):
        i = pl.program_id(0)

        @pl.when(i == 0)
        def _():
            gw_acc[...] = jnp.zeros_like(gw_acc)
            gfnw_ref[...] = jnp.zeros_like(gfnw_ref)
            loss_ref[...] = jnp.zeros_like(loss_ref)

        ys = ys_ref[...]
        yg = yg_ref[...]
        out = x_ref[...] + jnp.dot(ys, w_ref[0:1024, :], preferred_element_type=f32) \
            + jnp.dot(yg, w_ref[1024:2048, :], preferred_element_type=f32)
        rstd = lax.rsqrt(jnp.mean(out * out, axis=-1, keepdims=True) + EPS)
        yhat = out * rstd
        fw = fnw_ref[...]
        e = yhat * fw - tgt_ref[...]
        loss_ref[...] += 0.5 * jnp.sum(jnp.sum(e * e, axis=-1, keepdims=True) * (1.0 / D_MODEL), axis=0, keepdims=True)
        dyf = e * (1.0 / D_MODEL)
        gfnw_ref[0:1, :] += jnp.sum(dyf * yhat, axis=0, keepdims=True)
        dyhat = dyf * fw
        dout = rstd * (dyhat - yhat * jnp.mean(dyhat * yhat, axis=-1, keepdims=True))
        dout_ref[...] = dout
        db = dout.astype(_MM)
        dys_ref[...] = lax.dot_general(db, w_ref[0:1024, :], (((1,), (1,)), ((), ())), preferred_element_type=f32)
        dyg_ref[...] = lax.dot_general(db, w_ref[1024:2048, :], (((1,), (1,)), ((), ())), preferred_element_type=f32)
        gw_acc[0:1024, :] += lax.dot_general(ys, db, (((0,), (0,)), ((), ())), preferred_element_type=f32)
        gw_acc[1024:2048, :] += lax.dot_general(yg, db, (((0,), (0,)), ((), ())), preferred_element_type=f32)

        @pl.when(i == steps - 1)
        def _():
            gw_ref[...] = gw_acc[...].astype(gw_ref.dtype)

    steps = t // tm
    blk = pl.BlockSpec((tm, D_MODEL), lambda i: (i, 0))
    return _pc(
        body, name="out_fwd_bwd", grid=(steps,),
        in_specs=[blk, blk, blk, blk, _full((MIX_WIDTH, D_MODEL)), _full((1, D_MODEL))],
        out_specs=[blk, blk, blk, _full((MIX_WIDTH, D_MODEL)), _full((8, D_MODEL)), _full((1, 128))],
        out_shape=[jax.ShapeDtypeStruct((t, D_MODEL), f32)] * 3 +
                  [jax.ShapeDtypeStruct((MIX_WIDTH, D_MODEL), _MM), jax.ShapeDtypeStruct((8, D_MODEL), f32),
                   jax.ShapeDtypeStruct((1, 128), f32)],
        scratch_shapes=[pltpu.VMEM((MIX_WIDTH, D_MODEL), f32)],
        compiler_params=_cparams(("arbitrary",)),
    )(x, tgt, y_ssd, y_gdn, w_out, fnw)


def inproj_bwd_dx(x, dout, norm_w, w_perm, dgroups, scattered):
    t = x.shape[0]
    tm = min(256, t)
    f32 = jnp.float32

    def body(x_ref, dout_ref, nw_ref, w_ref, dz_ref, dxbc_ref, dgate_ref, dqkv_ref, dsm_ref, dx_ref, gnw_ref):
        i = pl.program_id(0)

        @pl.when(i == 0)
        def _():
            gnw_ref[...] = jnp.zeros_like(gnw_ref)

        du = None
        for (name, c0, c1), d_ref in zip(GROUPS, (dz_ref, dxbc_ref, dgate_ref, dqkv_ref, dsm_ref)):
            term = jnp.dot(d_ref[...].astype(_MM), w_ref[c0:c1, :], preferred_element_type=f32)
            du = term if du is None else du + term
        xf = x_ref[...]
        rstd = lax.rsqrt(jnp.mean(xf * xf, axis=-1, keepdims=True) + EPS)
        xhat = xf * rstd
        gnw_ref[0:1, :] += jnp.sum(du * xhat, axis=0, keepdims=True)
        dxh = du * nw_ref[...]
        dx_ref[...] = dout_ref[...] + rstd * (dxh - xhat * jnp.mean(dxh * xhat, axis=-1, keepdims=True))

    blk = lambda w: pl.BlockSpec((tm, w), lambda i: (i, 0))
    steps = t // tm
    kinds = ["scatter"] * len(scattered)
    hosted = _hosting(body, 9, 2, 0, kinds, lambda: pl.program_id(0) == 0, lambda: pl.program_id(0) == steps - 1)
    return _pc_comm(
        hosted, name="inproj_bwd_dx", grid=(steps,),
        in_specs=[blk(D_MODEL), blk(D_MODEL), _full((1, D_MODEL)), _full((PERM_DIM, D_MODEL))] +
                 [blk(c1 - c0) for _, c0, c1 in GROUPS] + [ANY] * len(scattered),
        out_specs=[blk(D_MODEL), _full((8, D_MODEL))] + [ANY] * len(scattered),
        out_shape=[jax.ShapeDtypeStruct((t, D_MODEL), f32), jax.ShapeDtypeStruct((8, D_MODEL), f32)] +
                  [_exchange_out_shape("scatter", a) for a in scattered],
        scratch_shapes=_exchange_sems(len(scattered)), compiler_params=_cparams(("arbitrary",)),
    )(x, dout, norm_w, w_perm, *dgroups, *scattered)


def grad_w_group(u, dg, name, scattered=()):
    t, n = dg.shape
    tn = 512 if n % 512 == 0 else n
    tm = 2048 if t % 2048 == 0 else t
    nj, nk = n // tn, t // tm
    f32 = jnp.float32

    def body(u_ref, d_ref, o_ref, acc):
        k = pl.program_id(1)

        @pl.when(k == 0)
        def _():
            acc[...] = jnp.zeros_like(acc)

        acc[...] += lax.dot_general(d_ref[...].astype(_MM), u_ref[...], (((0,), (0,)), ((), ())),
                                    preferred_element_type=f32)

        @pl.when(k == nk - 1)
        def _():
            o_ref[...] = acc[...].astype(o_ref.dtype)

    ne = len(scattered)
    hosted = _hosting(body, 2, 1, 1, ["scatter"] * ne,
                      lambda: (pl.program_id(0) == 0) & (pl.program_id(1) == 0),
                      lambda: (pl.program_id(0) == nj - 1) & (pl.program_id(1) == nk - 1))
    res = (_pc_comm if ne else _pc)(
        hosted, name=name, grid=(nj, nk),
        in_specs=[pl.BlockSpec((tm, D_MODEL), lambda j, k: (k, 0)),
                  pl.BlockSpec((tm, tn), lambda j, k: (k, j))] + [ANY] * ne,
        out_specs=[pl.BlockSpec((tn, D_MODEL), lambda j, k: (j, 0))] + [ANY] * ne,
        out_shape=[jax.ShapeDtypeStruct((n, D_MODEL), _MM)] + [_exchange_out_shape("scatter", a) for a in scattered],
        scratch_shapes=[pltpu.VMEM((tn, D_MODEL), f32)] + _exchange_sems(ne),
        compiler_params=_cparams(("arbitrary", "arbitrary")),
    )(u, dg, *scattered)
    return res if ne else res[0]


def _pad_lanes(v, off):
    n = v.shape[-1]
    return jnp.pad(v.reshape(1, n).astype(jnp.float32), ((0, 0), (off, 128 - off - n)))


REF_ROWS = dict(z=(0, 1024), xbc=(1024, 2560), dt=(2560, 2576), gate=(2576, 3600), qkv=(3600, 6672), ab=(6672, 6688))


def perm_w_in(gathered):
    wt = gathered.reshape(IN_DIM, gathered.shape[2])
    pieces = [wt[s:e] for s, e in (REF_ROWS[name] for name in ("z", "xbc", "gate", "qkv", "dt", "ab"))]
    pieces.append(jnp.zeros((PERM_DIM - IN_DIM, wt.shape[1]), wt.dtype))
    return jnp.concatenate(pieces, axis=0)


def unperm_w_in(gz, gxbc, ggate, gqkv, gsm):
    src = dict(z=gz, xbc=gxbc, dt=gsm[0:16], gate=ggate, qkv=gqkv, ab=gsm[16:32])
    slabs = []
    for k in range(N_DEV):
        a, b = k * W_IN_SHARD, (k + 1) * W_IN_SHARD
        parts = []
        for name, (s, e) in REF_ROWS.items():
            lo, hi = max(a, s), min(b, e)
            if lo < hi:
                parts.append(src[name][lo - s:hi - s])
        slabs.append(jnp.concatenate(parts, axis=0))
    return jnp.stack(slabs)


def all_gather(arrs, name):
    n = len(arrs)

    def body(*refs):
        ins, outs = refs[:n], refs[n:2 * n]
        send_sems, recv_sems, local_sems = refs[2 * n:]
        x, y, c, me = _me()
        sibling = (x, y, 1 - c)
        chips = [(1 - x, y), (x, 1 - y), (1 - x, 1 - y)]

        def idx(px, py, pc):
            return 4 * px + 2 * py + pc

        def copy(a, k, block, to, src=None):
            slot = outs[a].at[idx(*block)]
            return pltpu.make_async_remote_copy(src_ref=slot if src is None else src, dst_ref=slot,
                                                send_sem=send_sems.at[a, k], recv_sem=recv_sems.at[a, k],
                                                device_id=to, device_id_type=MESH)

        local = [pltpu.make_async_copy(ins[a], outs[a].at[me], local_sems.at[a]) for a in range(n)]
        for cp in local:
            cp.start()
        started = []
        for a in range(n):
            first = [copy(a, 0, (x, y, c), sibling, src=ins[a])]
            first += [copy(a, 1 + j, (x, y, c), (*chip, c), src=ins[a]) for j, chip in enumerate(chips)]
            for cp in first:
                cp.start()
            started += first
        for a in range(n):
            for j, chip in enumerate(chips):
                copy(a, 1 + j, (*chip, c), (x, y, c)).wait_recv()
                fwd = copy(a, 4 + j, (*chip, c), sibling)
                fwd.start()
                started.append(fwd)
        for a in range(n):
            copy(a, 0, sibling, (x, y, c)).wait_recv()
            for j, chip in enumerate(chips):
                copy(a, 4 + j, (*chip, 1 - c), (x, y, c)).wait_recv()
        for cp in started:
            cp.wait_send()
        for cp in local:
            cp.wait()

    return _pc_comm(
        body, name=name, in_specs=[ANY] * n, out_specs=[ANY] * n,
        out_shape=[jax.ShapeDtypeStruct((N_DEV,) + a.shape, a.dtype) for a in arrs],
        scratch_shapes=[pltpu.SemaphoreType.DMA((n, 7)), pltpu.SemaphoreType.DMA((n, 7)),
                        pltpu.SemaphoreType.DMA((n,))],
    )(*arrs)


def adamw_sum(recv, w, m, v, rows, name, cols=None):
    r, ccols = w.shape
    f32 = jnp.float32
    c1 = 1.0 / (1.0 - ADAM_B1 ** ADAM_STEP)
    c2 = 1.0 / (1.0 - ADAM_B2 ** ADAM_STEP)

    def body(recv_ref, w_ref, m_ref, v_ref, g_ref, d_ref, mo_ref, vo_ref):
        g = recv_ref[0].astype(f32)
        for k in range(1, N_DEV):
            g = g + recv_ref[k].astype(f32)
        mn = ADAM_B1 * m_ref[...] + (1.0 - ADAM_B1) * g
        vn = ADAM_B2 * v_ref[...] + (1.0 - ADAM_B2) * (g * g)
        g_ref[...] = g
        mo_ref[...] = mn
        vo_ref[...] = vn
        d_ref[...] = -ADAM_LR * ((mn * c1) / (jnp.sqrt(vn * c2) + ADAM_EPS) + ADAM_WD * w_ref[...])

    if cols is None:
        blk = pl.BlockSpec((rows, ccols), lambda i: (i, 0))
        rblk, steps = pl.BlockSpec((N_DEV, rows, ccols), lambda i: (0, i, 0)), r // rows
    else:
        blk = pl.BlockSpec((r, cols), lambda i: (0, i))
        rblk, steps = pl.BlockSpec((N_DEV, r, cols), lambda i: (0, 0, i)), ccols // cols
    return _pc(
        body, name=name, grid=(steps,),
        in_specs=[rblk, blk, blk, blk],
        out_specs=[blk] * 4, out_shape=[jax.ShapeDtypeStruct((r, ccols), f32)] * 4,
        compiler_params=_cparams(("arbitrary",)),
    )(recv, w, m, v)


SMALL = (("norm_w", 1, 1024, 0), ("ssd_conv_b", 1, 1536, 0), ("ssd_dt_bias", 1, 16, 0), ("ssd_a_log", 1, 16, 0),
         ("ssd_d", 1, 16, 0), ("ssd_norm_w", 1, 1024, 0), ("gdn_dt_bias", 1, 8, 16), ("gdn_a_log", 1, 8, 16),
         ("gdn_norm_w", 1, 128, 0), ("final_norm_w", 1, 1024, 0),
         ("ssd_conv_w", CONV_K, SSD_CONV_DIM // N_DEV, 0), ("gdn_conv_w", CONV_K, GDN_CONV_DIM // N_DEV, 0))


def _small_layout():
    out, off = [], 0
    for name, rows, n, lane0 in SMALL:
        stride = -(-(lane0 + n) // 128) * 128
        out.append((name, rows, n, lane0, stride, off))
        off += rows * stride
    return out, off


def scatter_small(accs):
    layout, total = _small_layout()
    f32 = jnp.float32

    def body(*refs):
        acc_refs, out_ref, slabs = refs[:len(layout)], refs[len(layout)], refs[len(layout) + 1]
        sems = refs[len(layout) + 2:]
        slabs[...] = jnp.zeros_like(slabs)
        for (name, rows, n, lane0, stride, off), acc in zip(layout, acc_refs):
            for k in range(N_DEV):
                if rows == 1:
                    slabs[k, :, off:off + stride] = acc[0:1, 0:stride]
                else:
                    for j in range(rows):
                        slabs[k, :, off + stride * j:off + stride * j + n] = acc[j:j + 1, n * k:n * k + n]
        start, wait = _exchange_ops("scatter", slabs, out_ref, *sems)
        start()
        wait()

    return _pc_comm(
        body, name="scatter_small_grads", out_specs=ANY, out_shape=jax.ShapeDtypeStruct((N_DEV, 1, total), f32),
        scratch_shapes=[pltpu.VMEM((N_DEV, 1, total), f32)] + _exchange_sems(1),
    )(*accs)


def adamw_small(recv, w, m, v):
    layout, total = _small_layout()
    f32 = jnp.float32
    c1 = 1.0 / (1.0 - ADAM_B1 ** ADAM_STEP)
    c2 = 1.0 / (1.0 - ADAM_B2 ** ADAM_STEP)
    np_ = len(layout)

    def body(*refs):
        recv_ref = refs[0]
        w_refs, m_refs, v_refs = refs[1:1 + np_], refs[1 + np_:1 + 2 * np_], refs[1 + 2 * np_:1 + 3 * np_]
        o_refs = refs[1 + 3 * np_:]
        g_all = recv_ref[0]
        for k in range(1, N_DEV):
            g_all = g_all + recv_ref[k]

        def update(g, wv, mv, vv):
            mn = ADAM_B1 * mv + (1.0 - ADAM_B1) * g
            vn = ADAM_B2 * vv + (1.0 - ADAM_B2) * (g * g)
            return g, -ADAM_LR * ((mn * c1) / (jnp.sqrt(vn * c2) + ADAM_EPS) + ADAM_WD * wv), mn, vn

        for p, (name, rows, n, lane0, stride, off) in enumerate(layout):
            outs = o_refs[4 * p:4 * p + 4]
            if rows == 1:
                res = update(g_all[:, off + lane0:off + lane0 + n], w_refs[p][...], m_refs[p][...], v_refs[p][...])
                for o, r in zip(outs, res):
                    o[...] = r
            else:
                for j in range(rows):
                    res = update(g_all[:, off + stride * j:off + stride * j + n], w_refs[p][0, j:j + 1, :],
                                 m_refs[p][0, j:j + 1, :], v_refs[p][0, j:j + 1, :])
                    for o, r in zip(outs, res):
                        o[0, j:j + 1, :] = r

    names = [e[0] for e in layout]
    ins = [recv] + [d[nm] for d in (w, m, v) for nm in names]
    out_shape = [jax.ShapeDtypeStruct(w[nm].shape, f32) for nm in names for _ in range(4)]
    res = _pc(body, name="adamw_small", out_shape=out_shape)(*ins)
    return {nm: tuple(res[4 * p:4 * p + 4]) for p, nm in enumerate(names)}


SHARD = (("ssd_conv_w", CONV_K * SSD_CONV_DIM // N_DEV), ("gdn_conv_w", CONV_K * GDN_CONV_DIM // N_DEV))
SHARD_ROWS = 24


def _rows_of(size):
    return -(-size // 128)


def _pack(vals, layout, total_rows):
    parts = []
    for (name, size), val in zip(layout, vals):
        flat = val.reshape(-1).astype(jnp.float32)
        parts.append(jnp.pad(flat, (0, _rows_of(size) * 128 - size)).reshape(-1, 128))
    used = sum(_rows_of(s) for _, s in layout)
    parts.append(jnp.zeros((total_rows - used, 128), jnp.float32))
    return jnp.concatenate(parts, axis=0)


def _conv_full(gathered_flat, ccols):
    return gathered_flat.reshape(N_DEV, CONV_K, ccols // N_DEV).transpose(1, 0, 2).reshape(CONV_K, ccols)


def kernel(x, norm_w, w_in, ssd_conv_w, ssd_conv_b, ssd_dt_bias, ssd_a_log, ssd_d, ssd_norm_w, gdn_conv_w, gdn_dt_bias, gdn_a_log, gdn_norm_w, w_out, final_norm_w, loss_target, m_norm_w, m_w_in, m_ssd_conv_w, m_ssd_conv_b, m_ssd_dt_bias, m_ssd_a_log, m_ssd_d, m_ssd_norm_w, m_gdn_conv_w, m_gdn_dt_bias, m_gdn_a_log, m_gdn_norm_w, m_w_out, m_final_norm_w, v_norm_w, v_w_in, v_ssd_conv_w, v_ssd_conv_b, v_ssd_dt_bias, v_ssd_a_log, v_ssd_d, v_ssd_norm_w, v_gdn_conv_w, v_gdn_dt_bias, v_gdn_a_log, v_gdn_norm_w, v_w_out, v_final_norm_w):
    f32 = jnp.float32
    w = dict(norm_w=norm_w, w_in=w_in, ssd_conv_w=ssd_conv_w, ssd_conv_b=ssd_conv_b, ssd_dt_bias=ssd_dt_bias,
             ssd_a_log=ssd_a_log, ssd_d=ssd_d, ssd_norm_w=ssd_norm_w, gdn_conv_w=gdn_conv_w, gdn_dt_bias=gdn_dt_bias,
             gdn_a_log=gdn_a_log, gdn_norm_w=gdn_norm_w, w_out=w_out, final_norm_w=final_norm_w)
    m = dict(norm_w=m_norm_w, w_in=m_w_in, ssd_conv_w=m_ssd_conv_w, ssd_conv_b=m_ssd_conv_b, ssd_dt_bias=m_ssd_dt_bias,
             ssd_a_log=m_ssd_a_log, ssd_d=m_ssd_d, ssd_norm_w=m_ssd_norm_w, gdn_conv_w=m_gdn_conv_w,
             gdn_dt_bias=m_gdn_dt_bias, gdn_a_log=m_gdn_a_log, gdn_norm_w=m_gdn_norm_w, w_out=m_w_out,
             final_norm_w=m_final_norm_w)
    v = dict(norm_w=v_norm_w, w_in=v_w_in, ssd_conv_w=v_ssd_conv_w, ssd_conv_b=v_ssd_conv_b, ssd_dt_bias=v_ssd_dt_bias,
             ssd_a_log=v_ssd_a_log, ssd_d=v_ssd_d, ssd_norm_w=v_ssd_norm_w, gdn_conv_w=v_gdn_conv_w,
             gdn_dt_bias=v_gdn_dt_bias, gdn_a_log=v_gdn_a_log, gdn_norm_w=v_gdn_norm_w, w_out=v_w_out,
             final_norm_w=v_final_norm_w)
    names = list(w)
    shapes = {n: w[n].shape for n in names}

    xl, tgt = x[0], loss_target[0]
    cs = _consts()
    dtb_s = _pad_lanes(ssd_dt_bias, 0)
    alog_s = _pad_lanes(ssd_a_log, 0)
    dpar = _pad_lanes(ssd_d, 0)
    dtb_g = _pad_lanes(gdn_dt_bias, 16)
    alog_g = _pad_lanes(gdn_a_log, 16)
    nw_g = gdn_norm_w.reshape(1, 128)
    nw_s = ssd_norm_w.reshape(1, 1024)
    cb_s = ssd_conv_b.reshape(1, 1536)
    nw1 = norm_w.reshape(1, D_MODEL)

    (g_w_in,) = all_gather([w_in[0].T.astype(_MM)], "gather_w_in")
    w_perm = perm_w_in(g_w_in)
    conv_pack = _pack([w["ssd_conv_w"], w["gdn_conv_w"]], SHARD, SHARD_ROWS)
    u, z, xbc, gate, qkv, sm, g_w_out, g_conv = inproj_fwd(xl, nw1, w_perm, [w_out[0].astype(_MM), conv_pack])
    w_out_full = g_w_out.reshape(MIX_WIDTH, D_MODEL)
    ssd_cw = _conv_full(g_conv[:, 0:6].reshape(N_DEV, -1), SSD_CONV_DIM)
    gdn_cw = _conv_full(g_conv[:, 6:18].reshape(N_DEV, -1), GDN_CONV_DIM)

    nc = xl.shape[0] // CHUNK
    y_ssd, hs, pre_s, y_gdn, ss, ts, pre_g = _chunk_call(
        [ssd_fwd(z, xbc, sm, ssd_cw, cb_s, dtb_s, alog_s, dpar, nw_s, cs),
         gdn_fwd(gate, qkv, sm, gdn_cw, dtb_g, alog_g, nw_g, cs)], "scan_fwd", nc, False)
    dout, dys, dyg, g_wout, g_fnw, loss_l = out_fwd_bwd(xl, tgt, y_ssd, y_gdn, w_out_full,
                                                        final_norm_w.reshape(1, D_MODEL))
    (dz, dxbc, g_cw_s, g_cb_s, g_dtb_s, g_alog_s, g_d, g_nw_s,
     dgate, dqkv, dsm, g_cw_g, g_dtb_g, g_alog_g, g_nw_g) = _chunk_call(
        [ssd_bwd(z, xbc, pre_s, sm, hs, dys, ssd_cw, dtb_s, alog_s, dpar, nw_s, cs),
         gdn_bwd(gate, qkv, pre_g, sm, ss, ts, dyg, gdn_cw, dtb_g, alog_g, nw_g, cs)], "scan_bwd", nc, True)

    t_w_out = g_wout.reshape(N_DEV, MIX_WIDTH // N_DEV, D_MODEL)
    gws = {}
    for dg, (name, _, _) in zip((dz, dxbc, dgate, dsm), (GROUPS[0], GROUPS[1], GROUPS[2], GROUPS[4])):
        gws[name] = grad_w_group(u, dg, "grad_w_in_" + name)
    gws["qkv"], r_w_out = grad_w_group(u, dqkv, "grad_w_in_qkv", [t_w_out])
    t_w_in = unperm_w_in(gws["z"], gws["xbc"], gws["gate"], gws["qkv"], gws["sm"])
    dx, g_nw, r_w_in = inproj_bwd_dx(xl, dout, nw1, w_perm, (dz, dxbc, dgate, dqkv, dsm), [t_w_in])

    accs = dict(norm_w=g_nw, ssd_conv_b=g_cb_s, ssd_dt_bias=g_dtb_s, ssd_a_log=g_alog_s, ssd_d=g_d,
                ssd_norm_w=g_nw_s, gdn_dt_bias=g_dtb_g, gdn_a_log=g_alog_g, gdn_norm_w=g_nw_g, final_norm_w=g_fnw,
                ssd_conv_w=g_cw_s, gdn_conv_w=g_cw_g)
    r_small = scatter_small([accs[e[0]] for e in SMALL])

    o_w_in = adamw_sum(r_w_in, w_in[0].T, m_w_in[0].T, v_w_in[0].T, None, "adamw_w_in", cols=256)
    o_w_out = adamw_sum(r_w_out, w_out[0], m_w_out[0], v_w_out[0], 64, "adamw_w_out")
    row = lambda d: {n: (a.reshape(1, -1) if a.ndim == 1 else a) for n, a in d.items()}
    o_small = adamw_small(r_small, row(w), row(m), row(v))

    loss = lax.psum(loss_l[0, 0], ("x", "y", "c"))
    outs = [loss, dx[None]]
    for k in range(4):
        parts = {n: o_small[n][k] for n in o_small}
        parts["w_in"] = o_w_in[k].T
        parts["w_out"] = o_w_out[k]
        outs += [parts[n].reshape(shapes[n]) for n in names]
    return tuple(outs)
```

```python
import functools

import jax
import jax.numpy as jnp
import numpy as np
from jax import lax
from jax.experimental import pallas as pl
from jax.experimental.pallas import tpu as pltpu

_MM = jnp.bfloat16

D_MODEL = 1024
CHUNK = 64
CONV_K = 4
EPS = 1e-6
SSD_CONV_DIM = 1536
GDN_HEADS = 8
GDN_DK = 128
GDN_CONV_DIM = 3072
MIX_WIDTH = 2048
IN_DIM = 6688
N_DEV = 8
W_IN_SHARD = IN_DIM // N_DEV
HI = lax.Precision.HIGHEST
HIGH = lax.Precision.HIGH
VMEM_LIMIT = 56 * 1024 * 1024

ADAM_LR = 0.001
ADAM_B1 = 0.9
ADAM_B2 = 0.999
ADAM_EPS = 1e-08
ADAM_WD = 0.01
ADAM_STEP = 10


def _pc(body, **kw):
    return pl.pallas_call(body, **kw)


def _pc_comm(body, **kw):
    return pl.pallas_call(body, **kw)


def _cparams(sem):
    return pltpu.CompilerParams(dimension_semantics=sem, vmem_limit_bytes=VMEM_LIMIT)


def _sig(x):
    return 0.5 * jnp.tanh(0.5 * x) + 0.5


@jax.custom_vjp
def _sigmoid(x):
    return _sig(x)


def _sigmoid_fwd(x):
    s = _sig(x)
    return s, s


def _sigmoid_bwd(s, g):
    return (g * s * (1.0 - s),)


_sigmoid.defvjp(_sigmoid_fwd, _sigmoid_bwd)


@jax.custom_vjp
def _silu(x):
    return x * _sig(x)


def _silu_fwd(x):
    s = _sig(x)
    return x * s, (x, s)


def _silu_bwd(res, g):
    x, s = res
    return (g * (s * (1.0 + x * (1.0 - s))),)


_silu.defvjp(_silu_fwd, _silu_bwd)


def _softplus_impl(x):
    return jnp.maximum(x, 0.0) + jnp.log(1.0 + jnp.exp(-jnp.abs(x)))


@jax.custom_vjp
def _softplus(x):
    return _softplus_impl(x)


def _softplus_fwd(x):
    return _softplus_impl(x), x


def _softplus_bwd(x, g):
    return (g * _sig(x),)


_softplus.defvjp(_softplus_fwd, _softplus_bwd)


def _lane_bcast_impl(x, k):
    return jnp.broadcast_to(x[..., k:k + 1], x.shape)


@functools.partial(jax.custom_vjp, nondiff_argnums=(1,))
def _lane_bcast(x, k):
    return _lane_bcast_impl(x, k)


def _lane_bcast_fwd(x, k):
    return _lane_bcast_impl(x, k), None


def _lane_bcast_bwd(k, _, g):
    lane = lax.broadcasted_iota(jnp.int32, g.shape, g.ndim - 1)
    return (jnp.where(lane == k, jnp.sum(g, axis=-1, keepdims=True), 0.0),)


_lane_bcast.defvjp(_lane_bcast_fwd, _lane_bcast_bwd)


def _mm(a, b):
    return jnp.dot(a.astype(_MM), b.astype(_MM), preferred_element_type=jnp.float32)


def _mm_nt(a, b):
    return lax.dot_general(a.astype(_MM), b.astype(_MM), (((1,), (1,)), ((), ())),
                           preferred_element_type=jnp.float32)


def _mm_tn(a, b):
    return lax.dot_general(a.astype(_MM), b.astype(_MM), (((0,), (0,)), ((), ())),
                           preferred_element_type=jnp.float32)


def _dot_hi(a, b):
    return jnp.dot(a, b, precision=HI, preferred_element_type=jnp.float32)


def _bmm(a, b):
    return lax.dot_general(a.astype(_MM), b.astype(_MM), (((2,), (1,)), ((0,), (0,))),
                           preferred_element_type=jnp.float32)


def _bmm_nt(a, b):
    return lax.dot_general(a.astype(_MM), b.astype(_MM), (((2,), (2,)), ((0,), (0,))),
                           preferred_element_type=jnp.float32)


def _bmm_tn(a, b):
    return lax.dot_general(a.astype(_MM), b.astype(_MM), (((1,), (1,)), ((0,), (0,))),
                           preferred_element_type=jnp.float32)


def _bmm_hi(a, b):
    return lax.dot_general(a, b, (((2,), (1,)), ((0,), (0,))), precision=HIGH, preferred_element_type=jnp.float32)


def _bmm_nt_hi(a, b):
    return lax.dot_general(a, b, (((2,), (2,)), ((0,), (0,))), precision=HIGH, preferred_element_type=jnp.float32)


def _bmm_tn_hi(a, b):
    return lax.dot_general(a, b, (((1,), (1,)), ((0,), (0,))), precision=HIGH, preferred_element_type=jnp.float32)


def _consts():
    l = np.arange(CHUNK)
    tri = (l[:, None] >= l[None, :]).astype(np.float32)
    lane = np.arange(128)
    i2 =(l[:, None] == (lane[None, :] % 64)).astype(np.float32)
    mask2 = (l[:, None] >= (lane[None, :] % 64)).astype(np.float32)
    lo = (lane < 64).astype(np.float32)[None, :]
    i64 = np.eye(CHUNK, dtype=np.float32)
    strict = (l[:, None] > l[None, :]).astype(np.float32)
    return dict(tri=jnp.asarray(tri), i2=jnp.asarray(i2), mask2=jnp.asarray(mask2), lo=jnp.asarray(lo),
                i64=jnp.asarray(i64), strict=jnp.asarray(strict))


def _ssd_chunk(xs_pre, b_pre, c_pre, z, sm, ht, dtb, alog, dpar, nw, tri, i2, mask2, lo):
    lane = lax.broadcasted_iota(jnp.int32, (1, 128), 1)
    m16 = lane < 16
    dt = jnp.where(m16, _softplus(sm + dtb), 0.0)
    a_neg = -jnp.exp(alog)
    cum = _dot_hi(tri, dt * a_neg)
    row = lax.broadcasted_iota(jnp.int32, (CHUNK, 1), 0)
    is_last = row == CHUNK - 1
    hi = 1.0 - lo
    bm = [_silu(b) for b in b_pre]
    cm = [_silu(c) for c in c_pre]
    cb2 = [_mm_nt(cm[g], jnp.concatenate([bm[g], bm[g]], axis=0)) for g in range(2)]
    yg, ht_next = [], []
    for j in range(8):
        g = j // 4
        pair = lambda v, j=j: jnp.where(lo > 0.5, _lane_bcast(v, 2 * j), _lane_bcast(v, 2 * j + 1))
        xs = _silu(xs_pre[j])
        dte = pair(dt)
        cume = pair(cum)
        cum_last = jnp.sum(jnp.where(is_last, cume, 0.0), axis=0, keepdims=True)
        xdt = xs * dte
        rowv = jnp.sum(cume * i2, axis=0, keepdims=True)
        lm = jnp.exp(jnp.where(mask2 > 0.5, cume - rowv, -jnp.inf))
        m = cb2[g] * lm
        xblk = jnp.concatenate([xdt * lo, xdt * hi], axis=0)
        y = _mm(m, xblk)
        y = y + _mm(cm[g], ht[j]) * jnp.exp(cume)
        y = y + pair(dpar) * xs
        yg.append(y * _silu(z[j]))
        st = _mm_tn(bm[g], xdt * jnp.exp(cum_last - cume))
        ht_next.append(ht[j] * jnp.exp(cum_last) + st)
    outs = []
    for g in range(2):
        ss = sum(jnp.sum(yg[j] * yg[j], axis=-1, keepdims=True) for j in range(4 * g, 4 * g + 4))
        rs = lax.rsqrt(ss * (1.0 / 512.0) + EPS)
        for j in range(4 * g, 4 * g + 4):
            outs.append(yg[j] * rs * nw[j])
    return outs, ht_next


def _tri_inverse(a):
    eye = jnp.eye(CHUNK, dtype=jnp.float32)[None]
    p = eye - a
    ap = a
    for _ in range(5):
        ap = _bmm_hi(ap, ap)
        p = p + _bmm_hi(p, ap)
    return p


@jax.custom_vjp
def _solve(a, r1, r2, t):
    return _bmm_hi(t, r1), _bmm_hi(t, r2)


def _solve_fwd(a, r1, r2, t):
    u, w = _bmm_hi(t, r1), _bmm_hi(t, r2)
    return (u, w), (t, u, w)


def _solve_bwd(res, cts):
    t, u, w = res
    du, dw = cts
    dr1 = _bmm_tn_hi(t, du)
    dr2 = _bmm_tn_hi(t, dw)
    da = -(_bmm_nt_hi(dr1, u) + _bmm_nt_hi(dr2, w))
    return da, dr1, dr2, jnp.zeros_like(t)


_solve.defvjp(_solve_fwd, _solve_bwd)


def _gdn_chunk(q_pre, k_pre, v_pre, gate, sm, s, dtb, alog, nw, tri, i64, strict, t_in=None):
    lane = lax.broadcasted_iota(jnp.int32, (1, 128), 1)
    m_a = (lane >= 16) & (lane < 24)
    g_full = jnp.where(m_a, -jnp.exp(alog) * _softplus(sm + dtb), 0.0)
    gc = _dot_hi(tri, g_full)
    sig = _sigmoid(sm)
    gc3 = jnp.stack([_lane_bcast(gc, 16 + h) for h in range(GDN_HEADS)])
    beta3 = jnp.stack([_lane_bcast(sig, 24 + h) for h in range(GDN_HEADS)])
    q = _silu(q_pre)
    q = q * lax.rsqrt(jnp.sum(q * q, axis=-1, keepdims=True) + EPS) * (GDN_DK ** -0.5)
    k = _silu(k_pre)
    k = k * lax.rsqrt(jnp.sum(k * k, axis=-1, keepdims=True) + EPS)
    v = _silu(v_pre)
    gcl = gc3[:, :, :CHUNK]
    gc_row = jnp.sum(gcl * i64[None], axis=1, keepdims=True)
    incl = (strict + i64)[None] > 0.5
    decay = jnp.exp(jnp.where(incl, gcl - gc_row, -jnp.inf))
    kb = k * beta3
    a = jnp.where(strict[None] > 0.5, _bmm_nt(kb, k) * decay, 0.0)
    egc = jnp.exp(gc3)
    t = _tri_inverse(a) if t_in is None else t_in
    u, w = _solve(a, v * beta3, kb * egc, t)
    attn = _bmm_nt(q, k) * decay
    row = lax.broadcasted_iota(jnp.int32, (1, CHUNK, 1), 1)
    gl = jnp.sum(jnp.where(row == CHUNK - 1, gc3, 0.0), axis=1, keepdims=True)
    q_dec = q * egc
    k_dec = k * jnp.exp(gl - gc3)
    v_new = u - _bmm(w, s)
    o = _bmm(q_dec, s) + _bmm(attn, v_new)
    s_next = s * jnp.exp(gl) + _bmm_tn(k_dec, v_new)
    on = o * lax.rsqrt(jnp.mean(o * o, axis=-1, keepdims=True) + EPS) * nw
    return on * _silu(gate), s_next, t


def _conv_fwd(pbuf, w_ref, c0, c1):
    acc = None
    for j in range(CONV_K):
        term = w_ref[j:j + 1, c0:c1] * pbuf[5 + j:69 + j, c0:c1]
        acc = term if acc is None else acc + term
    return acc


MESH = pl.DeviceIdType.MESH
ANY = pl.BlockSpec(memory_space=pl.ANY)


def _me():
    x, y, c = lax.axis_index("x"), lax.axis_index("y"), lax.axis_index("c")
    return x, y, c, 4 * x + 2 * y + c


def _peer(r):
    x, y, c, _ = _me()
    px = 1 - x if r & 4 else x
    py = 1 - y if r & 2 else y
    pc = 1 - c if r & 1 else c
    return (px, py, pc), 4 * px + 2 * py + pc


def _exchange_ops(kind, in_ref, out_ref, send_sems, recv_sems, local_sem):
    me = _me()[3]
    local = pltpu.make_async_copy(in_ref.at[me] if kind == "scatter" else in_ref, out_ref.at[me], local_sem)
    sends, recvs = [], []
    for r in range(1, N_DEV):
        peer, pidx = _peer(r)
        src = in_ref.at[pidx] if kind == "scatter" else in_ref
        sems = dict(send_sem=send_sems.at[r - 1], recv_sem=recv_sems.at[r - 1], device_id=peer, device_id_type=MESH)
        sends.append(pltpu.make_async_remote_copy(src_ref=src, dst_ref=out_ref.at[me], **sems))
        recvs.append(pltpu.make_async_remote_copy(src_ref=src, dst_ref=out_ref.at[pidx], **sems))

    def start():
        local.start()
        for cp in sends:
            cp.start()

    def wait():
        for cp in recvs:
            cp.wait_recv()
        for cp in sends:
            cp.wait_send()
        local.wait()

    return start, wait


def _exchange_sems(n):
    return [pltpu.SemaphoreType.DMA((N_DEV - 1,)), pltpu.SemaphoreType.DMA((N_DEV - 1,)),
            pltpu.SemaphoreType.DMA(())] * n


def _exchange_out_shape(kind, a):
    return jax.ShapeDtypeStruct(a.shape if kind == "scatter" else (N_DEV,) + a.shape, a.dtype)


def _hosting(body, n_in, n_out, n_scratch, kinds, first, last):
    ne = len(kinds)

    def wrapped(*refs):
        ins, ex_in = refs[:n_in], refs[n_in:n_in + ne]
        o0 = n_in + ne
        outs, ex_out = refs[o0:o0 + n_out], refs[o0 + n_out:o0 + n_out + ne]
        s0 = o0 + n_out + ne
        scr, sems = refs[s0:s0 + n_scratch], refs[s0 + n_scratch:]
        ops = [_exchange_ops(kinds[e], ex_in[e], ex_out[e], *sems[3 * e:3 * e + 3]) for e in range(ne)]

        @pl.when(first())
        def _():
            for start, _ in ops:
                start()

        body(*ins, *outs, *scr)

        @pl.when(last())
        def _():
            for _, wait in ops:
                wait()

    return wrapped


GROUPS = (("z", 0, 1024), ("xbc", 1024, 2560), ("gate", 2560, 3584), ("qkv", 3584, 6656), ("sm", 6656, 6784))
GROUP_ROWS = dict(z=((0, 1024),), xbc=((1024, 2560),), gate=((2576, 3600),), qkv=((3600, 6672),),
                  sm=((2560, 2576), (6672, 6688)))


def _w_rows(w_ref, name, width):
    pieces = [w_ref[a:b, :] for a, b in GROUP_ROWS[name]]
    n = sum(b - a for a, b in GROUP_ROWS[name])
    if n < width:
        pieces.append(jnp.zeros((width - n, D_MODEL), w_ref.dtype))
    return pieces[0] if len(pieces) == 1 else jnp.concatenate(pieces, axis=0)


def inproj_fwd(x, norm_w, w_perm, gathered):
    t = x.shape[0]
    tm = min(512, t)
    steps = t // tm
    kinds = ["gather"] * len(gathered)

    def body(x_ref, nw_ref, w_ref, u_ref, z_ref, xbc_ref, gate_ref, qkv_ref, sm_ref):
        xf = x_ref[...]
        rstd = lax.rsqrt(jnp.mean(xf * xf, axis=-1, keepdims=True) + EPS)
        u = (xf * rstd * nw_ref[...]).astype(_MM)
        u_ref[...] = u
        for (name, c0, c1), o_ref in zip(GROUPS, (z_ref, xbc_ref, gate_ref, qkv_ref, sm_ref)):
            o_ref[...] = lax.dot_general(u, _w_rows(w_ref, name, c1 - c0), (((1,), (1,)), ((), ())),
                                         preferred_element_type=jnp.float32)

    outs = [jax.ShapeDtypeStruct((t, D_MODEL), _MM)] + [jax.ShapeDtypeStruct((t, c1 - c0), jnp.float32)
                                                        for _, c0, c1 in GROUPS]
    hosted = _hosting(body, 3, 6, 0, kinds, lambda: pl.program_id(0) == 0, lambda: pl.program_id(0) == steps - 1)
    return _pc_comm(
        hosted, name="inproj_fwd", grid=(steps,),
        in_specs=[pl.BlockSpec((tm, D_MODEL), lambda i: (i, 0)),
                  pl.BlockSpec((1, D_MODEL), lambda i: (0, 0)),
                  pl.BlockSpec((IN_DIM, D_MODEL), lambda i: (0, 0), pipeline_mode=pl.Buffered(1))] +
                 [ANY] * len(gathered),
        out_specs=[pl.BlockSpec((tm, D_MODEL), lambda i: (i, 0))] +
                  [pl.BlockSpec((tm, c1 - c0), lambda i: (i, 0)) for _, c0, c1 in GROUPS] + [ANY] * len(gathered),
        out_shape=outs + [_exchange_out_shape("gather", a) for a in gathered],
        scratch_shapes=_exchange_sems(len(gathered)), compiler_params=_cparams(("arbitrary",)),
    )(x, norm_w, w_perm, *gathered)


SUB_FWD = 4
SUB_BWD = 2


def _halo_spec(width, idx_fn):
    return pl.BlockSpec((8, width), lambda i: (jnp.maximum(idx_fn(i) * (SUB_FWD * CHUNK // 8) - 1, 0), 0))


def _when_first(shared, fn):
    if shared["first"] is not False:
        pl.when(shared["first"])(fn)


def _full(shape):
    nd = len(shape)
    return pl.BlockSpec(shape, lambda i: (0,) * nd)


def _ssd_split(pre_fn, z_ref, sm_ref):
    xs_pre = [pre_fn(128 * j, 128 * j + 128) for j in range(8)]
    b_pre = [pre_fn(1024 + 128 * g, 1152 + 128 * g) for g in range(2)]
    c_pre = [pre_fn(1280 + 128 * g, 1408 + 128 * g) for g in range(2)]
    z = [z_ref[:, 128 * j:128 * j + 128] for j in range(8)]
    return xs_pre, b_pre, c_pre, z, sm_ref[...]


def ssd_fwd(z, xbc, sm, conv_w, conv_b, dtb, alog, dpar, nw, cs):
    t = z.shape[0]
    nc = t // CHUNK

    def body(shared, z_ref, xbc_ref, halo_ref, sm_ref, cw_ref, cb_ref, dtb_ref, alog_ref, dpar_ref, nw_ref,
             tri_ref, i2_ref, mask2_ref, lo_ref, y_ref, hs_ref, pre_ref, pbuf, ht_scr):
        def init():
            ht_scr[...] = jnp.zeros_like(ht_scr)

        _when_first(shared, init)
        pbuf[0:8, :] = jnp.where(shared["first"], 0.0, halo_ref[...])
        pbuf[8:72, :] = xbc_ref[...]

        def pre_fn(c0, c1):
            pre = _conv_fwd(pbuf, cw_ref, c0, c1) + cb_ref[:, c0:c1]
            pre_ref[:, c0:c1] = pre
            return pre

        xs_pre, b_pre, c_pre, zz, smv = _ssd_split(pre_fn, z_ref, sm_ref)
        ht = [ht_scr[:, 128 * j:128 * j + 128] for j in range(8)]
        hs_ref[0] = ht_scr[...]
        nwl = [nw_ref[:, 128 * j:128 * j + 128] for j in range(8)]
        outs, ht_next = _ssd_chunk(xs_pre, b_pre, c_pre, zz, smv, ht, dtb_ref[...], alog_ref[...], dpar_ref[...],
                                   nwl, tri_ref[...], i2_ref[...], mask2_ref[...], lo_ref[...])
        for j in range(8):
            y_ref[:, 128 * j:128 * j + 128] = outs[j].astype(y_ref.dtype)
            ht_scr[:, 128 * j:128 * j + 128] = ht_next[j]

    blk = lambda w: pl.BlockSpec((SUB_FWD * CHUNK, w), lambda i: (i, 0))
    return dict(
        body=body,
        in_kinds=["rows", "rows", ("halo", 1), "rows"] + ["full"] * 10, out_kinds=["rows", "state", "rows"],
        in_specs=[blk(1024), blk(1536), _halo_spec(1536, lambda i: i), blk(128),
                  _full((CONV_K, 1536)), _full((1, 1536)), _full((1, 128)), _full((1, 128)), _full((1, 128)),
                  _full((1, 1024)), _full((64, 64)), _full((64, 128)), _full((64, 128)),
                  _full((1, 128))],
        out_specs=[blk(1024), pl.BlockSpec((SUB_FWD, 128, 1024), lambda i: (i, 0, 0)), blk(1536)],
        out_shape=[jax.ShapeDtypeStruct((t, 1024), _MM), jax.ShapeDtypeStruct((nc, 128, 1024), jnp.float32),
                   jax.ShapeDtypeStruct((t, 1536), jnp.float32)],
        scratch=[pltpu.VMEM((72, 1536), jnp.float32), pltpu.VMEM((128, 1024), jnp.float32)],
        args=[z, xbc, xbc, sm, conv_w, conv_b, dtb, alog, dpar, nw, cs["tri"], cs["i2"], cs["mask2"], cs["lo"]])


def _conv_bwd(dpre_list, col_ranges, dbuf, carry, x_ref, cw_ref, dx_ref, dcw_ref, dcb_ref, first):
    for dpre, (c0, c1) in zip(dpre_list, col_ranges):
        dbuf[0:64, c0:c1] = dpre
    dbuf[64:72, :] = jnp.where(first, 0.0, carry[...])
    carry[...] = dbuf[0:8, :]
    for (c0, c1) in col_ranges:
        xin = x_ref[:, c0:c1]
        acc = None
        for j in range(CONV_K):
            sh = dbuf[3 - j:67 - j, c0:c1]
            term = cw_ref[j:j + 1, c0:c1] * sh
            acc = term if acc is None else acc + term
            dcw_ref[j:j + 1, c0:c1] += jnp.sum(xin * sh, axis=0, keepdims=True)
        dx_ref[:, c0:c1] = acc.astype(dx_ref.dtype)
        if dcb_ref is not None:
            dcb_ref[0:1, c0:c1] += jnp.sum(dbuf[0:64, c0:c1], axis=0, keepdims=True)


def ssd_bwd(z, xbc, pre, sm, hs, dy, conv_w, dtb, alog, dpar, nw, cs):
    t = z.shape[0]
    nc = t // CHUNK

    def body(shared, z_ref, xbc_ref, pre_ref, sm_ref, hs_ref, dy_ref, cw_ref, dtb_ref, alog_ref, dpar_ref, nw_ref,
             tri_ref, i2_ref, mask2_ref, lo_ref,
             dz_ref, dxbc_ref, dcw_ref, dcb_ref, ddtb_ref, dalog_ref, ddpar_ref, dnw_ref,
             dbuf, carry, dht_scr):
        def init():
            dht_scr[...] = jnp.zeros_like(dht_scr)
            dcw_ref[...] = jnp.zeros_like(dcw_ref)
            dcb_ref[...] = jnp.zeros_like(dcb_ref)
            ddtb_ref[...] = jnp.zeros_like(ddtb_ref)
            dalog_ref[...] = jnp.zeros_like(dalog_ref)
            ddpar_ref[...] = jnp.zeros_like(ddpar_ref)
            dnw_ref[...] = jnp.zeros_like(dnw_ref)

        _when_first(shared, init)
        pre_fn = lambda c0, c1: pre_ref[:, c0:c1]
        xs_pre, b_pre, c_pre, zz, smv = _ssd_split(pre_fn, z_ref, sm_ref)
        ht = [hs_ref[0, :, 128 * j:128 * j + 128] for j in range(8)]
        nwl = [nw_ref[:, 128 * j:128 * j + 128] for j in range(8)]
        consts = (tri_ref[...], i2_ref[...], mask2_ref[...], lo_ref[...])

        def f(xs_pre, b_pre, c_pre, zz, smv, ht, dtb, alog, dpar, nwl):
            return _ssd_chunk(xs_pre, b_pre, c_pre, zz, smv, ht, dtb, alog, dpar, nwl, *consts)

        _, vjp = jax.vjp(f, xs_pre, b_pre, c_pre, zz, smv, ht, dtb_ref[...], alog_ref[...], dpar_ref[...], nwl)
        dys = [dy_ref[:, 128 * j:128 * j + 128] for j in range(8)]
        dhts = [dht_scr[:, 128 * j:128 * j + 128] for j in range(8)]
        dxs, db, dc, dzz, dsm, dht, ddtb, dalog, ddpar, dnwl = vjp((dys, dhts))
        for j in range(8):
            dz_ref[:, 128 * j:128 * j + 128] = dzz[j].astype(dz_ref.dtype)
            dht_scr[:, 128 * j:128 * j + 128] = dht[j]
            dnw_ref[0:1, 128 * j:128 * j + 128] += dnwl[j]
        shared["dsm_ssd"] = dsm
        ddtb_ref[0:1, :] += ddtb
        dalog_ref[0:1, :] += dalog
        ddpar_ref[0:1, :] += ddpar
        ranges = ([(128 * j, 128 * j + 128) for j in range(8)] + [(1024 + 128 * g, 1152 + 128 * g) for g in range(2)]
                  + [(1280 + 128 * g, 1408 + 128 * g) for g in range(2)])
        _conv_bwd(dxs + db + dc, ranges, dbuf, carry, xbc_ref, cw_ref, dxbc_ref, dcw_ref, dcb_ref, shared["first"])

    ns = nc // SUB_BWD
    rblk = lambda w: pl.BlockSpec((SUB_BWD * CHUNK, w), lambda i: (ns - 1 - i, 0))
    acc = lambda w: pl.BlockSpec((8, w), lambda i: (0, 0))
    f32 = jnp.float32
    return dict(
        body=body,
        in_kinds=["rows"] * 4 + ["state", "rows"] + ["full"] * 9, out_kinds=["rows", "rows"] + ["full"] * 6,
        in_specs=[rblk(1024), rblk(1536), rblk(1536), rblk(128),
                  pl.BlockSpec((SUB_BWD, 128, 1024), lambda i: (ns - 1 - i, 0, 0)), rblk(1024),
                  _full((CONV_K, 1536)), _full((1, 128)), _full((1, 128)), _full((1, 128)),
                  _full((1, 1024)), _full((64, 64)), _full((64, 128)), _full((64, 128)),
                  _full((1, 128))],
        out_specs=[rblk(1024), rblk(1536), acc(1536), acc(1536), acc(128), acc(128), acc(128), acc(1024)],
        out_shape=[jax.ShapeDtypeStruct((t, 1024), f32), jax.ShapeDtypeStruct((t, 1536), f32),
                   jax.ShapeDtypeStruct((8, 1536), f32),
                   jax.ShapeDtypeStruct((8, 1536), f32), jax.ShapeDtypeStruct((8, 128), f32),
                   jax.ShapeDtypeStruct((8, 128), f32), jax.ShapeDtypeStruct((8, 128), f32),
                   jax.ShapeDtypeStruct((8, 1024), f32)],
        scratch=[pltpu.VMEM((72, 1536), f32), pltpu.VMEM((8, 1536), f32), pltpu.VMEM((128, 1024), f32)],
        args=[z, xbc, pre, sm, hs, dy, conv_w, dtb, alog, dpar, nw, cs["tri"], cs["i2"], cs["mask2"], cs["lo"]])


def _gdn_split(pre_fn, gate_ref):
    def heads(base):
        return jnp.stack([pre_fn(base + 128 * h, base + 128 * h + 128) for h in range(GDN_HEADS)])
    gate = jnp.stack([gate_ref[:, 128 * h:128 * h + 128] for h in range(GDN_HEADS)])
    return heads(0), heads(1024), heads(2048), gate


def gdn_fwd(gate, qkv, sm, conv_w, dtb, alog, nw, cs):
    t = gate.shape[0]
    nc = t // CHUNK

    def body(shared, gate_ref, qkv_ref, halo_ref, sm_ref, cw_ref, dtb_ref, alog_ref, nw_ref,
             tri_ref, i64_ref, strict_ref, o_ref, ss_ref, ts_ref, pre_ref, pbuf, s_scr):
        def init():
            s_scr[...] = jnp.zeros_like(s_scr)

        _when_first(shared, init)
        pbuf[0:8, :] = jnp.where(shared["first"], 0.0, halo_ref[...])
        pbuf[8:72, :] = qkv_ref[...]

        def pre_fn(c0, c1):
            pre = _conv_fwd(pbuf, cw_ref, c0, c1)
            pre_ref[:, c0:c1] = pre
            return pre

        q_pre, k_pre, v_pre, g3 = _gdn_split(pre_fn, gate_ref)
        s = s_scr[...]
        ss_ref[0] = s
        out, s_next, tinv = _gdn_chunk(q_pre, k_pre, v_pre, g3, sm_ref[...], s, dtb_ref[...], alog_ref[...],
                                       nw_ref[...], tri_ref[...], i64_ref[...], strict_ref[...])
        ts_ref[0] = tinv
        s_scr[...] = s_next
        for h in range(GDN_HEADS):
            o_ref[:, 128 * h:128 * h + 128] = out[h].astype(o_ref.dtype)

    blk = lambda w: pl.BlockSpec((SUB_FWD * CHUNK, w), lambda i: (i, 0))
    return dict(
        body=body,
        in_kinds=["rows", "rows", ("halo", 1), "rows"] + ["full"] * 7, out_kinds=["rows", "state", "state", "rows"],
        in_specs=[blk(1024), blk(3072), _halo_spec(3072, lambda i: i), blk(128),
                  _full((CONV_K, 3072)), _full((1, 128)), _full((1, 128)), _full((1, 128)),
                  _full((64, 64)), _full((64, 64)), _full((64, 64))],
        out_specs=[blk(1024), pl.BlockSpec((SUB_FWD, 8, 128, 128), lambda i: (i, 0, 0, 0)),
                   pl.BlockSpec((SUB_FWD, 8, CHUNK, CHUNK), lambda i: (i, 0, 0, 0)), blk(3072)],
        out_shape=[jax.ShapeDtypeStruct((t, 1024), _MM), jax.ShapeDtypeStruct((nc, 8, 128, 128), jnp.float32),
                   jax.ShapeDtypeStruct((nc, 8, CHUNK, CHUNK), jnp.float32),
                   jax.ShapeDtypeStruct((t, 3072), jnp.float32)],
        scratch=[pltpu.VMEM((72, 3072), jnp.float32), pltpu.VMEM((8, 128, 128), jnp.float32)],
        args=[gate, qkv, qkv, sm, conv_w, dtb, alog, nw, cs["tri"], cs["i64"], cs["strict"]])


def gdn_bwd(gate, qkv, pre, sm, ss, ts, do, conv_w, dtb, alog, nw, cs):
    t = gate.shape[0]
    nc = t // CHUNK

    def body(shared, gate_ref, qkv_ref, pre_ref, sm_ref, ss_ref, ts_ref, do_ref, cw_ref, dtb_ref, alog_ref,
             nw_ref, tri_ref, i64_ref, strict_ref,
             dgate_ref, dqkv_ref, dsm_ref, dcw_ref, ddtb_ref, dalog_ref, dnw_ref,
             dbuf, carry, ds_scr):
        def init():
            ds_scr[...] = jnp.zeros_like(ds_scr)
            dcw_ref[...] = jnp.zeros_like(dcw_ref)
            ddtb_ref[...] = jnp.zeros_like(ddtb_ref)
            dalog_ref[...] = jnp.zeros_like(dalog_ref)
            dnw_ref[...] = jnp.zeros_like(dnw_ref)

        _when_first(shared, init)

        q_pre, k_pre, v_pre, g3 = _gdn_split(lambda c0, c1: pre_ref[:, c0:c1], gate_ref)
        consts = (tri_ref[...], i64_ref[...], strict_ref[...], ts_ref[0])

        def f(q_pre, k_pre, v_pre, g3, smv, s, dtb, alog, nwv):
            return _gdn_chunk(q_pre, k_pre, v_pre, g3, smv, s, dtb, alog, nwv, *consts)[:2]

        _, vjp = jax.vjp(f, q_pre, k_pre, v_pre, g3, sm_ref[...], ss_ref[0], dtb_ref[...], alog_ref[...], nw_ref[...])
        do3 = jnp.stack([do_ref[:, 128 * h:128 * h + 128] for h in range(GDN_HEADS)])
        dq, dk, dv, dg3, dsm, ds, ddtb, dalog, dnw = vjp((do3, ds_scr[...]))
        ds_scr[...] = ds
        for h in range(GDN_HEADS):
            dgate_ref[:, 128 * h:128 * h + 128] = dg3[h].astype(dgate_ref.dtype)
        dsm_ref[...] = (dsm + shared["dsm_ssd"]).astype(dsm_ref.dtype)
        ddtb_ref[0:1, :] += ddtb
        dalog_ref[0:1, :] += dalog
        dnw_ref[0:1, :] += dnw
        ranges = [(base + 128 * h, base + 128 * h + 128) for base in (0, 1024, 2048) for h in range(GDN_HEADS)]
        dlist = [d[h] for d in (dq, dk, dv) for h in range(GDN_HEADS)]
        _conv_bwd(dlist, ranges, dbuf, carry, qkv_ref, cw_ref, dqkv_ref, dcw_ref, None, shared["first"])

    ns = nc // SUB_BWD
    rblk = lambda w: pl.BlockSpec((SUB_BWD * CHUNK, w), lambda i: (ns - 1 - i, 0))
    acc = lambda w: pl.BlockSpec((8, w), lambda i: (0, 0))
    f32 = jnp.float32
    return dict(
        body=body,
        in_kinds=["rows"] * 4 + ["state", "state", "rows"] + ["full"] * 7, out_kinds=["rows"] * 3 + ["full"] * 4,
        in_specs=[rblk(1024), rblk(3072), rblk(3072), rblk(128),
                  pl.BlockSpec((SUB_BWD, 8, 128, 128), lambda i: (ns - 1 - i, 0, 0, 0)),
                  pl.BlockSpec((SUB_BWD, 8, CHUNK, CHUNK), lambda i: (ns - 1 - i, 0, 0, 0)), rblk(1024),
                  _full((CONV_K, 3072)), _full((1, 128)), _full((1, 128)), _full((1, 128)),
                  _full((64, 64)), _full((64, 64)), _full((64, 64))],
        out_specs=[rblk(1024), rblk(3072), rblk(128), acc(3072), acc(128), acc(128), acc(128)],
        out_shape=[jax.ShapeDtypeStruct((t, 1024), f32), jax.ShapeDtypeStruct((t, 3072), f32),
                   jax.ShapeDtypeStruct((t, 128), f32), jax.ShapeDtypeStruct((8, 3072), f32),
                   jax.ShapeDtypeStruct((8, 128), f32), jax.ShapeDtypeStruct((8, 128), f32),
                   jax.ShapeDtypeStruct((8, 128), f32)],
        scratch=[pltpu.VMEM((72, 3072), f32), pltpu.VMEM((8, 3072), f32), pltpu.VMEM((8, 128, 128), f32)],
        args=[gate, qkv, pre, sm, ss, ts, do, conv_w, dtb, alog, nw, cs["tri"], cs["i64"], cs["strict"]])


def _chunk_call(parts, name, nc, reverse):
    n_in = [len(p["args"]) for p in parts]
    n_out = [len(p["out_shape"]) for p in parts]
    n_scr = [len(p["scratch"]) for p in parts]
    sub = SUB_BWD if reverse else SUB_FWD
    order = list(range(sub))[::-1] if reverse else list(range(sub))

    def view(ref, kind, s, refs):
        if kind == "rows":
            return ref.at[pl.ds(CHUNK * s, CHUNK)]
        if kind == "state":
            return ref.at[pl.ds(s, 1)]
        if kind == "full":
            return ref
        src = refs[kind[1]]
        return ref if s == 0 else src.at[pl.ds(CHUNK * s - 8, 8)]

    def body(*refs):
        ins, outs, scr = refs[:sum(n_in)], refs[sum(n_in):sum(n_in) + sum(n_out)], refs[sum(n_in) + sum(n_out):]
        for s in order:
            shared = {"first": (pl.program_id(0) == 0) if s == order[0] else False}
            for k, p in enumerate(parts):
                i0, o0, s0 = sum(n_in[:k]), sum(n_out[:k]), sum(n_scr[:k])
                p_ins = ins[i0:i0 + n_in[k]]
                p["body"](shared,
                          *[view(r, kd, s, p_ins) for r, kd in zip(p_ins, p["in_kinds"])],
                          *[view(r, kd, s, None) for r, kd in zip(outs[o0:o0 + n_out[k]], p["out_kinds"])],
                          *scr[s0:s0 + n_scr[k]])

    cat = lambda key: [v for p in parts for v in p[key]]
    return _pc(body, name=name, grid=(nc // sub,), in_specs=cat("in_specs"), out_specs=cat("out_specs"),
               out_shape=cat("out_shape"), scratch_shapes=cat("scratch"),
               compiler_params=_cparams(("arbitrary",)))(*cat("args"))


def out_fwd_bwd(x, tgt, y_ssd, y_gdn, w_out, fnw):
    t = x.shape[0]
    tm = min(512, t)
    f32 = jnp.float32

    def body(x_ref, tgt_ref, ys_ref, yg_ref, w_ref, fnw_ref,
             dout_ref, dys_ref, dyg_ref, gw_ref, gfnw_ref, loss_ref, gw_acc):
        i = pl.program_id(0)

        @pl.when(i == 0)
        def _():
            gw_acc[...] = jnp.zeros_like(gw_acc)
            gfnw_ref[...] = jnp.zeros_like(gfnw_ref)
            loss_ref[...] = jnp.zeros_like(loss_ref)

        ys = ys_ref[...]
        yg = yg_ref[...]
        out = x_ref[...] + jnp.dot(ys, w_ref[0:1024, :], preferred_element_type=f32) \
            + jnp.dot(yg, w_ref[1024:2048, :], preferred_element_type=f32)
        rstd = lax.rsqrt(jnp.mean(out * out, axis=-1, keepdims=True) + EPS)
        yhat = out * rstd
        fw = fnw_ref[...]
        e = yhat * fw - tgt_ref[...]
        loss_ref[...] += 0.5 * jnp.sum(jnp.sum(e * e, axis=-1, keepdims=True) * (1.0 / D_MODEL), axis=0, keepdims=True)
        dyf = e * (1.0 / D_MODEL)
        gfnw_ref[0:1, :] += jnp.sum(dyf * yhat, axis=0, keepdims=True)
        dyhat = dyf * fw
        dout = rstd * (dyhat - yhat * jnp.mean(dyhat * yhat, axis=-1, keepdims=True))
        dout_ref[...] = dout
        db = dout.astype(_MM)
        dys_ref[...] = lax.dot_general(db, w_ref[0:1024, :], (((1,), (1,)), ((), ())), preferred_element_type=f32)
        dyg_ref[...] = lax.dot_general(db, w_ref[1024:2048, :], (((1,), (1,)), ((), ())), preferred_element_type=f32)
        gw_acc[0:1024, :] += lax.dot_general(ys, db, (((0,), (0,)), ((), ())), preferred_element_type=f32)
        gw_acc[1024:2048, :] += lax.dot_general(yg, db, (((0,), (0,)), ((), ())), preferred_element_type=f32)

        @pl.when(i == steps - 1)
        def _():
            gw_ref[...] = gw_acc[...].astype(gw_ref.dtype)

    steps = t // tm
    blk = pl.BlockSpec((tm, D_MODEL), lambda i: (i, 0))
    return _pc(
        body, name="out_fwd_bwd", grid=(steps,),
        in_specs=[blk, blk, blk, blk, _full((MIX_WIDTH, D_MODEL)), _full((1, D_MODEL))],
        out_specs=[blk, blk, blk, _full((MIX_WIDTH, D_MODEL)), _full((8, D_MODEL)), _full((1, 128))],
        out_shape=[jax.ShapeDtypeStruct((t, D_MODEL), f32)] * 3 +
                  [jax.ShapeDtypeStruct((MIX_WIDTH, D_MODEL), _MM), jax.ShapeDtypeStruct((8, D_MODEL), f32),
                   jax.ShapeDtypeStruct((1, 128), f32)],
        scratch_shapes=[pltpu.VMEM((MIX_WIDTH, D_MODEL), f32)],
        compiler_params=_cparams(("arbitrary",)),
    )(x, tgt, y_ssd, y_gdn, w_out, fnw)


def inproj_bwd_dx(x, dout, norm_w, w_perm, dgroups, scattered):
    t = x.shape[0]
    tm = min(256, t)
    f32 = jnp.float32

    def body(x_ref, dout_ref, nw_ref, w_ref, dz_ref, dxbc_ref, dgate_ref, dqkv_ref, dsm_ref, dx_ref, gnw_ref):
        i = pl.program_id(0)

        @pl.when(i == 0)
        def _():
            gnw_ref[...] = jnp.zeros_like(gnw_ref)

        du = None
        for (name, c0, c1), d_ref in zip(GROUPS, (dz_ref, dxbc_ref, dgate_ref, dqkv_ref, dsm_ref)):
            term = jnp.dot(d_ref[...].astype(_MM), _w_rows(w_ref, name, c1 - c0), preferred_element_type=f32)
            du = term if du is None else du + term
        xf = x_ref[...]
        rstd = lax.rsqrt(jnp.mean(xf * xf, axis=-1, keepdims=True) + EPS)
        xhat = xf * rstd
        gnw_ref[0:1, :] += jnp.sum(du * xhat, axis=0, keepdims=True)
        dxh = du * nw_ref[...]
        dx_ref[...] = dout_ref[...] + rstd * (dxh - xhat * jnp.mean(dxh * xhat, axis=-1, keepdims=True))

    blk = lambda w: pl.BlockSpec((tm, w), lambda i: (i, 0))
    steps = t // tm
    kinds = ["scatter"] * len(scattered)
    hosted = _hosting(body, 9, 2, 0, kinds, lambda: pl.program_id(0) == 0, lambda: pl.program_id(0) == steps - 1)
    return _pc_comm(
        hosted, name="inproj_bwd_dx", grid=(steps,),
        in_specs=[blk(D_MODEL), blk(D_MODEL), _full((1, D_MODEL)), _full((IN_DIM, D_MODEL))] +
                 [blk(c1 - c0) for _, c0, c1 in GROUPS] + [ANY] * len(scattered),
        out_specs=[blk(D_MODEL), _full((8, D_MODEL))] + [ANY] * len(scattered),
        out_shape=[jax.ShapeDtypeStruct((t, D_MODEL), f32), jax.ShapeDtypeStruct((8, D_MODEL), f32)] +
                  [_exchange_out_shape("scatter", a) for a in scattered],
        scratch_shapes=_exchange_sems(len(scattered)), compiler_params=_cparams(("arbitrary",)),
    )(x, dout, norm_w, w_perm, *dgroups, *scattered)


def grad_w_group(u, dg, name, scattered=()):
    t, n = dg.shape
    tn = 512 if n % 512 == 0 else n
    tm = 2048 if t % 2048 == 0 else t
    nj, nk = n // tn, t // tm
    f32 = jnp.float32

    def body(u_ref, d_ref, o_ref, acc):
        k = pl.program_id(1)

        @pl.when(k == 0)
        def _():
            acc[...] = jnp.zeros_like(acc)

        acc[...] += lax.dot_general(d_ref[...].astype(_MM), u_ref[...], (((0,), (0,)), ((), ())),
                                    preferred_element_type=f32)

        @pl.when(k == nk - 1)
        def _():
            o_ref[...] = acc[...].astype(o_ref.dtype)

    ne = len(scattered)
    hosted = _hosting(body, 2, 1, 1, ["scatter"] * ne,
                      lambda: (pl.program_id(0) == 0) & (pl.program_id(1) == 0),
                      lambda: (pl.program_id(0) == nj - 1) & (pl.program_id(1) == nk - 1))
    res = (_pc_comm if ne else _pc)(
        hosted, name=name, grid=(nj, nk),
        in_specs=[pl.BlockSpec((tm, D_MODEL), lambda j, k: (k, 0)),
                  pl.BlockSpec((tm, tn), lambda j, k: (k, j))] + [ANY] * ne,
        out_specs=[pl.BlockSpec((tn, D_MODEL), lambda j, k: (j, 0))] + [ANY] * ne,
        out_shape=[jax.ShapeDtypeStruct((n, D_MODEL), _MM)] + [_exchange_out_shape("scatter", a) for a in scattered],
        scratch_shapes=[pltpu.VMEM((tn, D_MODEL), f32)] + _exchange_sems(ne),
        compiler_params=_cparams(("arbitrary", "arbitrary")),
    )(u, dg, *scattered)
    return res if ne else res[0]


def _pad_lanes(v, off):
    n = v.shape[-1]
    return jnp.pad(v.reshape(1, n).astype(jnp.float32), ((0, 0), (off, 128 - off - n)))


REF_ROWS = dict(z=(0, 1024), xbc=(1024, 2560), dt=(2560, 2576), gate=(2576, 3600), qkv=(3600, 6672), ab=(6672, 6688))


def unperm_w_in(gz, gxbc, ggate, gqkv, gsm):
    src = dict(z=gz, xbc=gxbc, dt=gsm[0:16], gate=ggate, qkv=gqkv, ab=gsm[16:32])
    slabs = []
    for k in range(N_DEV):
        a, b = k * W_IN_SHARD, (k + 1) * W_IN_SHARD
        parts = []
        for name, (s, e) in REF_ROWS.items():
            lo, hi = max(a, s), min(b, e)
            if lo < hi:
                parts.append(src[name][lo - s:hi - s])
        slabs.append(jnp.concatenate(parts, axis=0))
    return jnp.stack(slabs)


def all_gather(arrs, name):
    n = len(arrs)

    def body(*refs):
        ins, outs = refs[:n], refs[n:2 * n]
        send_sems, recv_sems, local_sems = refs[2 * n:]
        x, y, c, me = _me()
        sibling = (x, y, 1 - c)
        chips = [(1 - x, y), (x, 1 - y), (1 - x, 1 - y)]

        def idx(px, py, pc):
            return 4 * px + 2 * py + pc

        def copy(a, k, block, to, src=None):
            slot = outs[a].at[idx(*block)]
            return pltpu.make_async_remote_copy(src_ref=slot if src is None else src, dst_ref=slot,
                                                send_sem=send_sems.at[a, k], recv_sem=recv_sems.at[a, k],
                                                device_id=to, device_id_type=MESH)

        local = [pltpu.make_async_copy(ins[a], outs[a].at[me], local_sems.at[a]) for a in range(n)]
        for cp in local:
            cp.start()
        started = []
        for a in range(n):
            first = [copy(a, 0, (x, y, c), sibling, src=ins[a])]
            first += [copy(a, 1 + j, (x, y, c), (*chip, c), src=ins[a]) for j, chip in enumerate(chips)]
            for cp in first:
                cp.start()
            started += first
        for a in range(n):
            for j, chip in enumerate(chips):
                copy(a, 1 + j, (*chip, c), (x, y, c)).wait_recv()
                fwd = copy(a, 4 + j, (*chip, c), sibling)
                fwd.start()
                started.append(fwd)
        for a in range(n):
            copy(a, 0, sibling, (x, y, c)).wait_recv()
            for j, chip in enumerate(chips):
                copy(a, 4 + j, (*chip, 1 - c), (x, y, c)).wait_recv()
        for cp in started:
            cp.wait_send()
        for cp in local:
            cp.wait()

    return _pc_comm(
        body, name=name, in_specs=[ANY] * n, out_specs=[ANY] * n,
        out_shape=[jax.ShapeDtypeStruct((N_DEV,) + a.shape, a.dtype) for a in arrs],
        scratch_shapes=[pltpu.SemaphoreType.DMA((n, 7)), pltpu.SemaphoreType.DMA((n, 7)),
                        pltpu.SemaphoreType.DMA((n,))],
    )(*arrs)


def adamw_sum(recv, w, m, v, rows, name, cols=None):
    r, ccols = w.shape
    f32 = jnp.float32
    c1 = 1.0 / (1.0 - ADAM_B1 ** ADAM_STEP)
    c2 = 1.0 / (1.0 - ADAM_B2 ** ADAM_STEP)

    def body(recv_ref, w_ref, m_ref, v_ref, g_ref, d_ref, mo_ref, vo_ref):
        g = recv_ref[0].astype(f32)
        for k in range(1, N_DEV):
            g = g + recv_ref[k].astype(f32)
        mn = ADAM_B1 * m_ref[...] + (1.0 - ADAM_B1) * g
        vn = ADAM_B2 * v_ref[...] + (1.0 - ADAM_B2) * (g * g)
        g_ref[...] = g
        mo_ref[...] = mn
        vo_ref[...] = vn
        d_ref[...] = -ADAM_LR * ((mn * c1) / (jnp.sqrt(vn * c2) + ADAM_EPS) + ADAM_WD * w_ref[...])

    if cols is None:
        blk = pl.BlockSpec((rows, ccols), lambda i: (i, 0))
        rblk, steps = pl.BlockSpec((N_DEV, rows, ccols), lambda i: (0, i, 0)), r // rows
    else:
        blk = pl.BlockSpec((r, cols), lambda i: (0, i))
        rblk, steps = pl.BlockSpec((N_DEV, r, cols), lambda i: (0, 0, i)), ccols // cols
    return _pc(
        body, name=name, grid=(steps,),
        in_specs=[rblk, blk, blk, blk],
        out_specs=[blk] * 4, out_shape=[jax.ShapeDtypeStruct((r, ccols), f32)] * 4,
        compiler_params=_cparams(("arbitrary",)),
    )(recv, w, m, v)


SMALL = (("norm_w", 1, 1024, 0), ("ssd_conv_b", 1, 1536, 0), ("ssd_dt_bias", 1, 16, 0), ("ssd_a_log", 1, 16, 0),
         ("ssd_d", 1, 16, 0), ("ssd_norm_w", 1, 1024, 0), ("gdn_dt_bias", 1, 8, 16), ("gdn_a_log", 1, 8, 16),
         ("gdn_norm_w", 1, 128, 0), ("final_norm_w", 1, 1024, 0),
         ("ssd_conv_w", CONV_K, SSD_CONV_DIM // N_DEV, 0), ("gdn_conv_w", CONV_K, GDN_CONV_DIM // N_DEV, 0))


def _small_layout():
    out, off = [], 0
    for name, rows, n, lane0 in SMALL + (("loss", 1, 128, 0),):
        stride = -(-(lane0 + n) // 128) * 128
        out.append((name, rows, n, lane0, stride, off))
        off += rows * stride
    return out, off


def scatter_small(accs):
    layout, total = _small_layout()
    f32 = jnp.float32

    def body(*refs):
        acc_refs, out_ref, slabs = refs[:len(layout)], refs[len(layout)], refs[len(layout) + 1]
        sems = refs[len(layout) + 2:]
        slabs[...] = jnp.zeros_like(slabs)
        for (name, rows, n, lane0, stride, off), acc in zip(layout, acc_refs):
            for k in range(N_DEV):
                if rows == 1:
                    slabs[k, :, off:off + stride] = acc[0:1, 0:stride]
                else:
                    for j in range(rows):
                        slabs[k, :, off + stride * j:off + stride * j + n] = acc[j:j + 1, n * k:n * k + n]
        start, wait = _exchange_ops("scatter", slabs, out_ref, *sems)
        start()
        wait()

    return _pc_comm(
        body, name="scatter_small_grads", out_specs=ANY, out_shape=jax.ShapeDtypeStruct((N_DEV, 1, total), f32),
        scratch_shapes=[pltpu.VMEM((N_DEV, 1, total), f32)] + _exchange_sems(1),
    )(*accs)


def adamw_small(recv, w, m, v):
    layout, total = _small_layout()
    loss_off = layout[-1][5]
    layout = layout[:-1]
    f32 = jnp.float32
    c1 = 1.0 / (1.0 - ADAM_B1 ** ADAM_STEP)
    c2 = 1.0 / (1.0 - ADAM_B2 ** ADAM_STEP)
    np_ = len(layout)

    def body(*refs):
        recv_ref = refs[0]
        w_refs, m_refs, v_refs = refs[1:1 + np_], refs[1 + np_:1 + 2 * np_], refs[1 + 2 * np_:1 + 3 * np_]
        o_refs = refs[1 + 3 * np_:]
        g_all = recv_ref[0]
        for k in range(1, N_DEV):
            g_all = g_all + recv_ref[k]
        o_refs[4 * np_][...] = g_all[:, loss_off:loss_off + 128]

        def update(g, wv, mv, vv):
            mn = ADAM_B1 * mv + (1.0 - ADAM_B1) * g
            vn = ADAM_B2 * vv + (1.0 - ADAM_B2) * (g * g)
            return g, -ADAM_LR * ((mn * c1) / (jnp.sqrt(vn * c2) + ADAM_EPS) + ADAM_WD * wv), mn, vn

        for p, (name, rows, n, lane0, stride, off) in enumerate(layout):
            outs = o_refs[4 * p:4 * p + 4]
            if rows == 1:
                res = update(g_all[:, off + lane0:off + lane0 + n], w_refs[p][...], m_refs[p][...], v_refs[p][...])
                for o, r in zip(outs, res):
                    o[...] = r
            else:
                for j in range(rows):
                    res = update(g_all[:, off + stride * j:off + stride * j + n], w_refs[p][0, j:j + 1, :],
                                 m_refs[p][0, j:j + 1, :], v_refs[p][0, j:j + 1, :])
                    for o, r in zip(outs, res):
                        o[0, j:j + 1, :] = r

    names = [e[0] for e in layout]
    ins = [recv] + [d[nm] for d in (w, m, v) for nm in names]
    out_shape = [jax.ShapeDtypeStruct(w[nm].shape, f32) for nm in names for _ in range(4)]
    out_shape.append(jax.ShapeDtypeStruct((1, 128), f32))
    res = _pc(body, name="adamw_small", out_shape=out_shape)(*ins)
    return {nm: tuple(res[4 * p:4 * p + 4]) for p, nm in enumerate(names)}, res[4 * np_]


SHARD = (("ssd_conv_w", CONV_K * SSD_CONV_DIM // N_DEV), ("gdn_conv_w", CONV_K * GDN_CONV_DIM // N_DEV))
SHARD_ROWS = 24


def _rows_of(size):
    return -(-size // 128)


def _pack(vals, layout, total_rows):
    parts = []
    for (name, size), val in zip(layout, vals):
        flat = val.reshape(-1).astype(jnp.float32)
        parts.append(jnp.pad(flat, (0, _rows_of(size) * 128 - size)).reshape(-1, 128))
    used = sum(_rows_of(s) for _, s in layout)
    parts.append(jnp.zeros((total_rows - used, 128), jnp.float32))
    return jnp.concatenate(parts, axis=0)


def _conv_full(gathered_flat, ccols):
    return gathered_flat.reshape(N_DEV, CONV_K, ccols // N_DEV).transpose(1, 0, 2).reshape(CONV_K, ccols)


def kernel(x, norm_w, w_in, ssd_conv_w, ssd_conv_b, ssd_dt_bias, ssd_a_log, ssd_d, ssd_norm_w, gdn_conv_w, gdn_dt_bias, gdn_a_log, gdn_norm_w, w_out, final_norm_w, loss_target, m_norm_w, m_w_in, m_ssd_conv_w, m_ssd_conv_b, m_ssd_dt_bias, m_ssd_a_log, m_ssd_d, m_ssd_norm_w, m_gdn_conv_w, m_gdn_dt_bias, m_gdn_a_log, m_gdn_norm_w, m_w_out, m_final_norm_w, v_norm_w, v_w_in, v_ssd_conv_w, v_ssd_conv_b, v_ssd_dt_bias, v_ssd_a_log, v_ssd_d, v_ssd_norm_w, v_gdn_conv_w, v_gdn_dt_bias, v_gdn_a_log, v_gdn_norm_w, v_w_out, v_final_norm_w):
    f32 = jnp.float32
    w = dict(norm_w=norm_w, w_in=w_in, ssd_conv_w=ssd_conv_w, ssd_conv_b=ssd_conv_b, ssd_dt_bias=ssd_dt_bias,
             ssd_a_log=ssd_a_log, ssd_d=ssd_d, ssd_norm_w=ssd_norm_w, gdn_conv_w=gdn_conv_w, gdn_dt_bias=gdn_dt_bias,
             gdn_a_log=gdn_a_log, gdn_norm_w=gdn_norm_w, w_out=w_out, final_norm_w=final_norm_w)
    m = dict(norm_w=m_norm_w, w_in=m_w_in, ssd_conv_w=m_ssd_conv_w, ssd_conv_b=m_ssd_conv_b, ssd_dt_bias=m_ssd_dt_bias,
             ssd_a_log=m_ssd_a_log, ssd_d=m_ssd_d, ssd_norm_w=m_ssd_norm_w, gdn_conv_w=m_gdn_conv_w,
             gdn_dt_bias=m_gdn_dt_bias, gdn_a_log=m_gdn_a_log, gdn_norm_w=m_gdn_norm_w, w_out=m_w_out,
             final_norm_w=m_final_norm_w)
    v = dict(norm_w=v_norm_w, w_in=v_w_in, ssd_conv_w=v_ssd_conv_w, ssd_conv_b=v_ssd_conv_b, ssd_dt_bias=v_ssd_dt_bias,
             ssd_a_log=v_ssd_a_log, ssd_d=v_ssd_d, ssd_norm_w=v_ssd_norm_w, gdn_conv_w=v_gdn_conv_w,
             gdn_dt_bias=v_gdn_dt_bias, gdn_a_log=v_gdn_a_log, gdn_norm_w=v_gdn_norm_w, w_out=v_w_out,
             final_norm_w=v_final_norm_w)
    names = list(w)
    shapes = {n: w[n].shape for n in names}

    xl, tgt = x[0], loss_target[0]
    cs = _consts()
    dtb_s = _pad_lanes(ssd_dt_bias, 0)
    alog_s = _pad_lanes(ssd_a_log, 0)
    dpar = _pad_lanes(ssd_d, 0)
    dtb_g = _pad_lanes(gdn_dt_bias, 16)
    alog_g = _pad_lanes(gdn_a_log, 16)
    nw_g = gdn_norm_w.reshape(1, 128)
    nw_s = ssd_norm_w.reshape(1, 1024)
    cb_s = ssd_conv_b.reshape(1, 1536)
    nw1 = norm_w.reshape(1, D_MODEL)

    (g_w_in,) = all_gather([w_in[0].T.astype(_MM)], "gather_w_in")
    w_perm = g_w_in.reshape(IN_DIM, D_MODEL)
    conv_pack = _pack([w["ssd_conv_w"], w["gdn_conv_w"]], SHARD, SHARD_ROWS)
    u, z, xbc, gate, qkv, sm, g_w_out, g_conv = inproj_fwd(xl, nw1, w_perm, [w_out[0].astype(_MM), conv_pack])
    w_out_full = g_w_out.reshape(MIX_WIDTH, D_MODEL)
    ssd_cw = _conv_full(g_conv[:, 0:6].reshape(N_DEV, -1), SSD_CONV_DIM)
    gdn_cw = _conv_full(g_conv[:, 6:18].reshape(N_DEV, -1), GDN_CONV_DIM)

    nc = xl.shape[0] // CHUNK
    y_ssd, hs, pre_s, y_gdn, ss, ts, pre_g = _chunk_call(
        [ssd_fwd(z, xbc, sm, ssd_cw, cb_s, dtb_s, alog_s, dpar, nw_s, cs),
         gdn_fwd(gate, qkv, sm, gdn_cw, dtb_g, alog_g, nw_g, cs)], "scan_fwd", nc, False)
    dout, dys, dyg, g_wout, g_fnw, loss_l = out_fwd_bwd(xl, tgt, y_ssd, y_gdn, w_out_full,
                                                        final_norm_w.reshape(1, D_MODEL))
    (dz, dxbc, g_cw_s, g_cb_s, g_dtb_s, g_alog_s, g_d, g_nw_s,
     dgate, dqkv, dsm, g_cw_g, g_dtb_g, g_alog_g, g_nw_g) = _chunk_call(
        [ssd_bwd(z, xbc, pre_s, sm, hs, dys, ssd_cw, dtb_s, alog_s, dpar, nw_s, cs),
         gdn_bwd(gate, qkv, pre_g, sm, ss, ts, dyg, gdn_cw, dtb_g, alog_g, nw_g, cs)], "scan_bwd", nc, True)

    t_w_out = g_wout.reshape(N_DEV, MIX_WIDTH // N_DEV, D_MODEL)
    gws = {}
    for dg, (name, _, _) in zip((dz, dxbc, dgate, dsm), (GROUPS[0], GROUPS[1], GROUPS[2], GROUPS[4])):
        gws[name] = grad_w_group(u, dg, "grad_w_in_" + name)
    gws["qkv"], r_w_out = grad_w_group(u, dqkv, "grad_w_in_qkv", [t_w_out])
    t_w_in = unperm_w_in(gws["z"], gws["xbc"], gws["gate"], gws["qkv"], gws["sm"])
    dx, g_nw, r_w_in = inproj_bwd_dx(xl, dout, nw1, w_perm, (dz, dxbc, dgate, dqkv, dsm), [t_w_in])

    accs = dict(norm_w=g_nw, ssd_conv_b=g_cb_s, ssd_dt_bias=g_dtb_s, ssd_a_log=g_alog_s, ssd_d=g_d,
                ssd_norm_w=g_nw_s, gdn_dt_bias=g_dtb_g, gdn_a_log=g_alog_g, gdn_norm_w=g_nw_g, final_norm_w=g_fnw,
                ssd_conv_w=g_cw_s, gdn_conv_w=g_cw_g)
    r_small = scatter_small([accs[e[0]] for e in SMALL] + [loss_l])

    o_w_in = adamw_sum(r_w_in, w_in[0].T, m_w_in[0].T, v_w_in[0].T, None, "adamw_w_in", cols=256)
    o_w_out = adamw_sum(r_w_out, w_out[0], m_w_out[0], v_w_out[0], 64, "adamw_w_out")
    row = lambda d: {n: (a.reshape(1, -1) if a.ndim == 1 else a) for n, a in d.items()}
    o_small, loss_sum = adamw_small(r_small, row(w), row(m), row(v))

    loss = loss_sum[0, 0]
    outs = [loss, dx[None]]
    for k in range(4):
        parts = {n: o_small[n][k] for n in o_small}
        parts["w_in"] = o_w_in[k].T
        parts["w_out"] = o_w_out[k]
        outs += [parts[n].reshape(shapes[n]) for n in names]
    return tuple(outs)
```

```python
import functools

import jax
import jax.numpy as jnp
import numpy as np
from jax import lax
from jax.experimental import pallas as pl
from jax.experimental.pallas import tpu as pltpu

_MM = jnp.bfloat16

D_MODEL = 1024
CHUNK = 64
CONV_K = 4
EPS = 1e-6
SSD_CONV_DIM = 1536
GDN_HEADS = 8
GDN_DK = 128
GDN_CONV_DIM = 3072
MIX_WIDTH = 2048
IN_DIM = 6688
N_DEV = 8
W_IN_SHARD = IN_DIM // N_DEV
HI = lax.Precision.HIGHEST
HIGH = lax.Precision.HIGH
VMEM_LIMIT = 56 * 1024 * 1024

ADAM_LR = 0.001
ADAM_B1 = 0.9
ADAM_B2 = 0.999
ADAM_EPS = 1e-08
ADAM_WD = 0.01
ADAM_STEP = 10


def _pc(body, **kw):
    return pl.pallas_call(body, **kw)


def _pc_comm(body, **kw):
    return pl.pallas_call(body, **kw)


def _cparams(sem):
    return pltpu.CompilerParams(dimension_semantics=sem, vmem_limit_bytes=VMEM_LIMIT)


def _sig(x):
    return 0.5 * jnp.tanh(0.5 * x) + 0.5


@jax.custom_vjp
def _sigmoid(x):
    return _sig(x)


def _sigmoid_fwd(x):
    s = _sig(x)
    return s, s


def _sigmoid_bwd(s, g):
    return (g * s * (1.0 - s),)


_sigmoid.defvjp(_sigmoid_fwd, _sigmoid_bwd)


@jax.custom_vjp
def _silu(x):
    return x * _sig(x)


def _silu_fwd(x):
    s = _sig(x)
    return x * s, (x, s)


def _silu_bwd(res, g):
    x, s = res
    return (g * (s * (1.0 + x * (1.0 - s))),)


_silu.defvjp(_silu_fwd, _silu_bwd)


def _softplus_impl(x):
    return jnp.maximum(x, 0.0) + jnp.log(1.0 + jnp.exp(-jnp.abs(x)))


@jax.custom_vjp
def _softplus(x):
    return _softplus_impl(x)


def _softplus_fwd(x):
    return _softplus_impl(x), x


def _softplus_bwd(x, g):
    return (g * _sig(x),)


_softplus.defvjp(_softplus_fwd, _softplus_bwd)


def _lane_bcast_impl(x, k):
    return jnp.broadcast_to(x[..., k:k + 1], x.shape)


@functools.partial(jax.custom_vjp, nondiff_argnums=(1,))
def _lane_bcast(x, k):
    return _lane_bcast_impl(x, k)


def _lane_bcast_fwd(x, k):
    return _lane_bcast_impl(x, k), None


def _lane_bcast_bwd(k, _, g):
    lane = lax.broadcasted_iota(jnp.int32, g.shape, g.ndim - 1)
    return (jnp.where(lane == k, jnp.sum(g, axis=-1, keepdims=True), 0.0),)


_lane_bcast.defvjp(_lane_bcast_fwd, _lane_bcast_bwd)


def _mm(a, b):
    return jnp.dot(a.astype(_MM), b.astype(_MM), preferred_element_type=jnp.float32)


def _mm_nt(a, b):
    return lax.dot_general(a.astype(_MM), b.astype(_MM), (((1,), (1,)), ((), ())),
                           preferred_element_type=jnp.float32)


def _mm_tn(a, b):
    return lax.dot_general(a.astype(_MM), b.astype(_MM), (((0,), (0,)), ((), ())),
                           preferred_element_type=jnp.float32)


def _dot_hi(a, b):
    return jnp.dot(a, b, precision=HI, preferred_element_type=jnp.float32)


def _bmm(a, b):
    return lax.dot_general(a.astype(_MM), b.astype(_MM), (((2,), (1,)), ((0,), (0,))),
                           preferred_element_type=jnp.float32)


def _bmm_nt(a, b):
    return lax.dot_general(a.astype(_MM), b.astype(_MM), (((2,), (2,)), ((0,), (0,))),
                           preferred_element_type=jnp.float32)


def _bmm_tn(a, b):
    return lax.dot_general(a.astype(_MM), b.astype(_MM), (((1,), (1,)), ((0,), (0,))),
                           preferred_element_type=jnp.float32)


def _bmm_hi(a, b):
    return lax.dot_general(a, b, (((2,), (1,)), ((0,), (0,))), precision=HIGH, preferred_element_type=jnp.float32)


def _bmm_nt_hi(a, b):
    return lax.dot_general(a, b, (((2,), (2,)), ((0,), (0,))), precision=HIGH, preferred_element_type=jnp.float32)


def _bmm_tn_hi(a, b):
    return lax.dot_general(a, b, (((1,), (1,)), ((0,), (0,))), precision=HIGH, preferred_element_type=jnp.float32)


def _consts():
    l = np.arange(CHUNK)
    tri = (l[:, None] >= l[None, :]).astype(np.float32)
    lane = np.arange(128)
    i2 =(l[:, None] == (lane[None, :] % 64)).astype(np.float32)
    mask2 = (l[:, None] >= (lane[None, :] % 64)).astype(np.float32)
    lo = (lane < 64).astype(np.float32)[None, :]
    i64 = np.eye(CHUNK, dtype=np.float32)
    strict = (l[:, None] > l[None, :]).astype(np.float32)
    return dict(tri=jnp.asarray(tri), i2=jnp.asarray(i2), mask2=jnp.asarray(mask2), lo=jnp.asarray(lo),
                i64=jnp.asarray(i64), strict=jnp.asarray(strict))


def _ssd_chunk(xs_pre, b_pre, c_pre, z, sm, ht, dtb, alog, dpar, nw, tri, i2, mask2, lo):
    lane = lax.broadcasted_iota(jnp.int32, (1, 128), 1)
    m16 = lane < 16
    dt = jnp.where(m16, _softplus(sm + dtb), 0.0)
    a_neg = -jnp.exp(alog)
    cum = _dot_hi(tri, dt * a_neg)
    row = lax.broadcasted_iota(jnp.int32, (CHUNK, 1), 0)
    is_last = row == CHUNK - 1
    hi = 1.0 - lo
    bm = [_silu(b) for b in b_pre]
    cm = [_silu(c) for c in c_pre]
    cb2 = [_mm_nt(cm[g], jnp.concatenate([bm[g], bm[g]], axis=0)) for g in range(2)]
    yg, ht_next = [], []
    for j in range(8):
        g = j // 4
        pair = lambda v, j=j: jnp.where(lo > 0.5, _lane_bcast(v, 2 * j), _lane_bcast(v, 2 * j + 1))
        xs = _silu(xs_pre[j])
        dte = pair(dt)
        cume = pair(cum)
        cum_last = jnp.sum(jnp.where(is_last, cume, 0.0), axis=0, keepdims=True)
        xdt = xs * dte
        rowv = jnp.sum(cume * i2, axis=0, keepdims=True)
        lm = jnp.exp(jnp.where(mask2 > 0.5, cume - rowv, -jnp.inf))
        m = cb2[g] * lm
        xblk = jnp.concatenate([xdt * lo, xdt * hi], axis=0)
        y = _mm(m, xblk)
        y = y + _mm(cm[g], ht[j]) * jnp.exp(cume)
        y = y + pair(dpar) * xs
        yg.append(y * _silu(z[j]))
        st = _mm_tn(bm[g], xdt * jnp.exp(cum_last - cume))
        ht_next.append(ht[j] * jnp.exp(cum_last) + st)
    outs = []
    for g in range(2):
        ss = sum(jnp.sum(yg[j] * yg[j], axis=-1, keepdims=True) for j in range(4 * g, 4 * g + 4))
        rs = lax.rsqrt(ss * (1.0 / 512.0) + EPS)
        for j in range(4 * g, 4 * g + 4):
            outs.append(yg[j] * rs * nw[j])
    return outs, ht_next


def _tri_inverse(a):
    eye = jnp.eye(CHUNK, dtype=jnp.float32)[None]
    p = eye - a
    x = _bmm_hi(a, a)
    for _ in range(4):
        both = _bmm_hi(jnp.concatenate([p, x], axis=1), x)
        p = p + both[:, :CHUNK]
        x = both[:, CHUNK:]
    return p + _bmm_hi(p, x)


def _solve_apply(t, r1, r2):
    both = _bmm_hi(t, jnp.concatenate([r1, r2], axis=-1))
    n = r1.shape[-1]
    return both[..., :n], both[..., n:]


@jax.custom_vjp
def _solve(a, r1, r2, t):
    return _solve_apply(t, r1, r2)


def _solve_fwd(a, r1, r2, t):
    u, w = _bmm_hi(t, r1), _bmm_hi(t, r2)
    return (u, w), (t, u, w)


def _solve_bwd(res, cts):
    t, u, w = res
    du, dw = cts
    dr1 = _bmm_tn_hi(t, du)
    dr2 = _bmm_tn_hi(t, dw)
    da = -(_bmm_nt_hi(dr1, u) + _bmm_nt_hi(dr2, w))
    return da, dr1, dr2, jnp.zeros_like(t)


_solve.defvjp(_solve_fwd, _solve_bwd)


def _gdn_chunk(q_pre, k_pre, v_pre, gate, sm, s, dtb, alog, nw, tri, i64, strict, t_in=None):
    lane = lax.broadcasted_iota(jnp.int32, (1, 128), 1)
    m_a = (lane >= 16) & (lane < 24)
    g_full = jnp.where(m_a, -jnp.exp(alog) * _softplus(sm + dtb), 0.0)
    gc = _dot_hi(tri, g_full)
    sig = _sigmoid(sm)
    gc3 = jnp.stack([_lane_bcast(gc, 16 + h) for h in range(GDN_HEADS)])
    beta3 = jnp.stack([_lane_bcast(sig, 24 + h) for h in range(GDN_HEADS)])
    q = _silu(q_pre)
    q = q * lax.rsqrt(jnp.sum(q * q, axis=-1, keepdims=True) + EPS) * (GDN_DK ** -0.5)
    k = _silu(k_pre)
    k = k * lax.rsqrt(jnp.sum(k * k, axis=-1, keepdims=True) + EPS)
    v = _silu(v_pre)
    gcl = gc3[:, :, :CHUNK]
    gc_row = jnp.sum(gcl * i64[None], axis=1, keepdims=True)
    incl = (strict + i64)[None] > 0.5
    decay = jnp.exp(jnp.where(incl, gcl - gc_row, -jnp.inf))
    kb = k * beta3
    a = jnp.where(strict[None] > 0.5, _bmm_nt(kb, k) * decay, 0.0)
    egc = jnp.exp(gc3)
    t = _tri_inverse(a) if t_in is None else t_in
    u, w = _solve(a, v * beta3, kb * egc, t)
    attn = _bmm_nt(q, k) * decay
    row = lax.broadcasted_iota(jnp.int32, (1, CHUNK, 1), 1)
    gl = jnp.sum(jnp.where(row == CHUNK - 1, gc3, 0.0), axis=1, keepdims=True)
    q_dec = q * egc
    k_dec = k * jnp.exp(gl - gc3)
    v_new = u - _bmm(w, s)
    o = _bmm(q_dec, s) + _bmm(attn, v_new)
    s_next = s * jnp.exp(gl) + _bmm_tn(k_dec, v_new)
    on = o * lax.rsqrt(jnp.mean(o * o, axis=-1, keepdims=True) + EPS) * nw
    return on * _silu(gate), s_next, t


def _conv_fwd(pbuf, w_ref, c0, c1):
    blk = pbuf[:, c0:c1]
    acc = w_ref[CONV_K - 1:CONV_K, c0:c1] * blk[8:72]
    for j in range(CONV_K - 1):
        acc = acc + w_ref[j:j + 1, c0:c1] * pltpu.roll(blk, CONV_K - 1 - j, axis=0)[8:72]
    return acc


MESH = pl.DeviceIdType.MESH
ANY = pl.BlockSpec(memory_space=pl.ANY)


def _me():
    x, y, c = lax.axis_index("x"), lax.axis_index("y"), lax.axis_index("c")
    return x, y, c, 4 * x + 2 * y + c


def _peer(r):
    x, y, c, _ = _me()
    px = 1 - x if r & 4 else x
    py = 1 - y if r & 2 else y
    pc = 1 - c if r & 1 else c
    return (px, py, pc), 4 * px + 2 * py + pc


def _exchange_ops(kind, in_ref, out_ref, send_sems, recv_sems, local_sem):
    me = _me()[3]
    local = pltpu.make_async_copy(in_ref.at[me] if kind == "scatter" else in_ref, out_ref.at[me], local_sem)
    sends, recvs = [], []
    for r in range(1, N_DEV):
        peer, pidx = _peer(r)
        src = in_ref.at[pidx] if kind == "scatter" else in_ref
        sems = dict(send_sem=send_sems.at[r - 1], recv_sem=recv_sems.at[r - 1], device_id=peer, device_id_type=MESH)
        sends.append(pltpu.make_async_remote_copy(src_ref=src, dst_ref=out_ref.at[me], **sems))
        recvs.append(pltpu.make_async_remote_copy(src_ref=src, dst_ref=out_ref.at[pidx], **sems))

    def start():
        local.start()
        for cp in sends:
            cp.start()

    def wait():
        for cp in recvs:
            cp.wait_recv()
        for cp in sends:
            cp.wait_send()
        local.wait()

    return start, wait


def _exchange_sems(n):
    return [pltpu.SemaphoreType.DMA((N_DEV - 1,)), pltpu.SemaphoreType.DMA((N_DEV - 1,)),
            pltpu.SemaphoreType.DMA(())] * n


def _exchange_out_shape(kind, a):
    return jax.ShapeDtypeStruct(a.shape if kind == "scatter" else (N_DEV,) + a.shape, a.dtype)


def _hosting(body, n_in, n_out, n_scratch, kinds, first, last):
    ne = len(kinds)

    def wrapped(*refs):
        ins, ex_in = refs[:n_in], refs[n_in:n_in + ne]
        o0 = n_in + ne
        outs, ex_out = refs[o0:o0 + n_out], refs[o0 + n_out:o0 + n_out + ne]
        s0 = o0 + n_out + ne
        scr, sems = refs[s0:s0 + n_scratch], refs[s0 + n_scratch:]
        ops = [_exchange_ops(kinds[e], ex_in[e], ex_out[e], *sems[3 * e:3 * e + 3]) for e in range(ne)]

        @pl.when(first())
        def _():
            for start, _ in ops:
                start()

        body(*ins, *outs, *scr)

        @pl.when(last())
        def _():
            for _, wait in ops:
                wait()

    return wrapped


GROUPS = (("z", 0, 1024), ("xbc", 1024, 2560), ("gate", 2560, 3584), ("qkv", 3584, 6656), ("sm", 6656, 6784))
GROUP_ROWS = dict(z=((0, 1024),), xbc=((1024, 2560),), gate=((2576, 3600),), qkv=((3600, 6672),),
                  sm=((2560, 2576), (6672, 6688)))


def _w_rows(w_ref, name, width):
    pieces = [w_ref[a:b, :] for a, b in GROUP_ROWS[name]]
    n = sum(b - a for a, b in GROUP_ROWS[name])
    if n < width:
        pieces.append(jnp.zeros((width - n, D_MODEL), w_ref.dtype))
    return pieces[0] if len(pieces) == 1 else jnp.concatenate(pieces, axis=0)


def inproj_fwd(x, norm_w, w_perm, gathered):
    t = x.shape[0]
    tm = min(512, t)
    steps = t // tm
    kinds = ["gather"] * len(gathered)

    def body(x_ref, nw_ref, w_ref, u_ref, z_ref, xbc_ref, gate_ref, qkv_ref, sm_ref):
        xf = x_ref[...]
        rstd = lax.rsqrt(jnp.mean(xf * xf, axis=-1, keepdims=True) + EPS)
        u = (xf * rstd * nw_ref[...]).astype(_MM)
        u_ref[...] = u
        for (name, c0, c1), o_ref in zip(GROUPS, (z_ref, xbc_ref, gate_ref, qkv_ref, sm_ref)):
            o_ref[...] = lax.dot_general(u, _w_rows(w_ref, name, c1 - c0), (((1,), (1,)), ((), ())),
                                         preferred_element_type=jnp.float32)

    outs = [jax.ShapeDtypeStruct((t, D_MODEL), _MM)] + [jax.ShapeDtypeStruct((t, c1 - c0), jnp.float32)
                                                        for _, c0, c1 in GROUPS]
    hosted = _hosting(body, 3, 6, 0, kinds, lambda: pl.program_id(0) == 0, lambda: pl.program_id(0) == steps - 1)
    return _pc_comm(
        hosted, name="inproj_fwd", grid=(steps,),
        in_specs=[pl.BlockSpec((tm, D_MODEL), lambda i: (i, 0)),
                  pl.BlockSpec((1, D_MODEL), lambda i: (0, 0)),
                  pl.BlockSpec((IN_DIM, D_MODEL), lambda i: (0, 0), pipeline_mode=pl.Buffered(1))] +
                 [ANY] * len(gathered),
        out_specs=[pl.BlockSpec((tm, D_MODEL), lambda i: (i, 0))] +
                  [pl.BlockSpec((tm, c1 - c0), lambda i: (i, 0)) for _, c0, c1 in GROUPS] + [ANY] * len(gathered),
        out_shape=outs + [_exchange_out_shape("gather", a) for a in gathered],
        scratch_shapes=_exchange_sems(len(gathered)), compiler_params=_cparams(("arbitrary",)),
    )(x, norm_w, w_perm, *gathered)


SUB_FWD = 4
SUB_BWD = 2


def _halo_spec(width, idx_fn):
    return pl.BlockSpec((8, width), lambda i: (jnp.maximum(idx_fn(i) * (SUB_FWD * CHUNK // 8) - 1, 0), 0))


def _when_first(shared, fn):
    if shared["first"] is not False:
        pl.when(shared["first"])(fn)


def _full(shape):
    nd = len(shape)
    return pl.BlockSpec(shape, lambda i: (0,) * nd)


def _ssd_split(pre_fn, z_ref, sm_ref):
    xs_pre = [pre_fn(128 * j, 128 * j + 128) for j in range(8)]
    b_pre = [pre_fn(1024 + 128 * g, 1152 + 128 * g) for g in range(2)]
    c_pre = [pre_fn(1280 + 128 * g, 1408 + 128 * g) for g in range(2)]
    z = [z_ref[:, 128 * j:128 * j + 128] for j in range(8)]
    return xs_pre, b_pre, c_pre, z, sm_ref[...]


def ssd_fwd(z, xbc, sm, conv_w, conv_b, dtb, alog, dpar, nw, cs):
    t = z.shape[0]
    nc = t // CHUNK

    def body(shared, z_ref, xbc_ref, halo_ref, sm_ref, cw_ref, cb_ref, dtb_ref, alog_ref, dpar_ref, nw_ref,
             tri_ref, i2_ref, mask2_ref, lo_ref, y_ref, hs_ref, pre_ref, pbuf, ht_scr):
        def init():
            ht_scr[...] = jnp.zeros_like(ht_scr)

        _when_first(shared, init)
        pbuf[0:8, :] = jnp.where(shared["first"], 0.0, halo_ref[...])
        pbuf[8:72, :] = xbc_ref[...]

        def pre_fn(c0, c1):
            pre = _conv_fwd(pbuf, cw_ref, c0, c1) + cb_ref[:, c0:c1]
            pre_ref[:, c0:c1] = pre
            return pre

        xs_pre, b_pre, c_pre, zz, smv = _ssd_split(pre_fn, z_ref, sm_ref)
        ht = [ht_scr[:, 128 * j:128 * j + 128] for j in range(8)]
        hs_ref[0] = ht_scr[...]
        nwl = [nw_ref[:, 128 * j:128 * j + 128] for j in range(8)]
        outs, ht_next = _ssd_chunk(xs_pre, b_pre, c_pre, zz, smv, ht, dtb_ref[...], alog_ref[...], dpar_ref[...],
                                   nwl, tri_ref[...], i2_ref[...], mask2_ref[...], lo_ref[...])
        for j in range(8):
            y_ref[:, 128 * j:128 * j + 128] = outs[j].astype(y_ref.dtype)
            ht_scr[:, 128 * j:128 * j + 128] = ht_next[j]

    blk = lambda w: pl.BlockSpec((SUB_FWD * CHUNK, w), lambda i: (i, 0))
    return dict(
        body=body,
        in_kinds=["rows", "rows", ("halo", 1), "rows"] + ["full"] * 10, out_kinds=["rows", "state", "rows"],
        in_specs=[blk(1024), blk(1536), _halo_spec(1536, lambda i: i), blk(128),
                  _full((CONV_K, 1536)), _full((1, 1536)), _full((1, 128)), _full((1, 128)), _full((1, 128)),
                  _full((1, 1024)), _full((64, 64)), _full((64, 128)), _full((64, 128)),
                  _full((1, 128))],
        out_specs=[blk(1024), pl.BlockSpec((SUB_FWD, 128, 1024), lambda i: (i, 0, 0)), blk(1536)],
        out_shape=[jax.ShapeDtypeStruct((t, 1024), _MM), jax.ShapeDtypeStruct((nc, 128, 1024), jnp.float32),
                   jax.ShapeDtypeStruct((t, 1536), jnp.float32)],
        scratch=[pltpu.VMEM((72, 1536), jnp.float32), pltpu.VMEM((128, 1024), jnp.float32)],
        args=[z, xbc, xbc, sm, conv_w, conv_b, dtb, alog, dpar, nw, cs["tri"], cs["i2"], cs["mask2"], cs["lo"]])


def _conv_bwd(dpre_list, col_ranges, dbuf, carry, x_ref, cw_ref, dx_ref, dcw_ref, dcb_ref, first):
    for dpre, (c0, c1) in zip(dpre_list, col_ranges):
        dbuf[0:64, c0:c1] = dpre
    dbuf[64:72, :] = jnp.where(first, 0.0, carry[...])
    carry[...] = dbuf[0:8, :]
    for (c0, c1) in col_ranges:
        xin = x_ref[:, c0:c1]
        blk = dbuf[:, c0:c1]
        acc = None
        for j in range(CONV_K):
            sh = blk[0:64] if j == CONV_K - 1 else pltpu.roll(blk, 72 - (CONV_K - 1 - j), axis=0)[0:64]
            term = cw_ref[j:j + 1, c0:c1] * sh
            acc = term if acc is None else acc + term
            dcw_ref[j:j + 1, c0:c1] += jnp.sum(xin * sh, axis=0, keepdims=True)
        dx_ref[:, c0:c1] = acc.astype(dx_ref.dtype)
        if dcb_ref is not None:
            dcb_ref[0:1, c0:c1] += jnp.sum(dbuf[0:64, c0:c1], axis=0, keepdims=True)


def ssd_bwd(z, xbc, pre, sm, hs, dy, conv_w, dtb, alog, dpar, nw, cs):
    t = z.shape[0]
    nc = t // CHUNK

    def body(shared, z_ref, xbc_ref, pre_ref, sm_ref, hs_ref, dy_ref, cw_ref, dtb_ref, alog_ref, dpar_ref, nw_ref,
             tri_ref, i2_ref, mask2_ref, lo_ref,
             dz_ref, dxbc_ref, dcw_ref, dcb_ref, ddtb_ref, dalog_ref, ddpar_ref, dnw_ref,
             dbuf, carry, dht_scr):
        def init():
            dht_scr[...] = jnp.zeros_like(dht_scr)
            dcw_ref[...] = jnp.zeros_like(dcw_ref)
            dcb_ref[...] = jnp.zeros_like(dcb_ref)
            ddtb_ref[...] = jnp.zeros_like(ddtb_ref)
            dalog_ref[...] = jnp.zeros_like(dalog_ref)
            ddpar_ref[...] = jnp.zeros_like(ddpar_ref)
            dnw_ref[...] = jnp.zeros_like(dnw_ref)

        _when_first(shared, init)
        pre_fn = lambda c0, c1: pre_ref[:, c0:c1]
        xs_pre, b_pre, c_pre, zz, smv = _ssd_split(pre_fn, z_ref, sm_ref)
        ht = [hs_ref[0, :, 128 * j:128 * j + 128] for j in range(8)]
        nwl = [nw_ref[:, 128 * j:128 * j + 128] for j in range(8)]
        consts = (tri_ref[...], i2_ref[...], mask2_ref[...], lo_ref[...])

        def f(xs_pre, b_pre, c_pre, zz, smv, ht, dtb, alog, dpar, nwl):
            return _ssd_chunk(xs_pre, b_pre, c_pre, zz, smv, ht, dtb, alog, dpar, nwl, *consts)

        _, vjp = jax.vjp(f, xs_pre, b_pre, c_pre, zz, smv, ht, dtb_ref[...], alog_ref[...], dpar_ref[...], nwl)
        dys = [dy_ref[:, 128 * j:128 * j + 128] for j in range(8)]
        dhts = [dht_scr[:, 128 * j:128 * j + 128] for j in range(8)]
        dxs, db, dc, dzz, dsm, dht, ddtb, dalog, ddpar, dnwl = vjp((dys, dhts))
        for j in range(8):
            dz_ref[:, 128 * j:128 * j + 128] = dzz[j].astype(dz_ref.dtype)
            dht_scr[:, 128 * j:128 * j + 128] = dht[j]
            dnw_ref[0:1, 128 * j:128 * j + 128] += dnwl[j]
        shared["dsm_ssd"] = dsm
        ddtb_ref[0:1, :] += ddtb
        dalog_ref[0:1, :] += dalog
        ddpar_ref[0:1, :] += ddpar
        ranges = ([(128 * j, 128 * j + 128) for j in range(8)] + [(1024 + 128 * g, 1152 + 128 * g) for g in range(2)]
                  + [(1280 + 128 * g, 1408 + 128 * g) for g in range(2)])
        _conv_bwd(dxs + db + dc, ranges, dbuf, carry, xbc_ref, cw_ref, dxbc_ref, dcw_ref, dcb_ref, shared["first"])

    ns = nc // SUB_BWD
    rblk = lambda w: pl.BlockSpec((SUB_BWD * CHUNK, w), lambda i: (ns - 1 - i, 0))
    acc = lambda w: pl.BlockSpec((8, w), lambda i: (0, 0))
    f32 = jnp.float32
    return dict(
        body=body,
        in_kinds=["rows"] * 4 + ["state", "rows"] + ["full"] * 9, out_kinds=["rows", "rows"] + ["full"] * 6,
        in_specs=[rblk(1024), rblk(1536), rblk(1536), rblk(128),
                  pl.BlockSpec((SUB_BWD, 128, 1024), lambda i: (ns - 1 - i, 0, 0)), rblk(1024),
                  _full((CONV_K, 1536)), _full((1, 128)), _full((1, 128)), _full((1, 128)),
                  _full((1, 1024)), _full((64, 64)), _full((64, 128)), _full((64, 128)),
                  _full((1, 128))],
        out_specs=[rblk(1024), rblk(1536), acc(1536), acc(1536), acc(128), acc(128), acc(128), acc(1024)],
        out_shape=[jax.ShapeDtypeStruct((t, 1024), f32), jax.ShapeDtypeStruct((t, 1536), f32),
                   jax.ShapeDtypeStruct((8, 1536), f32),
                   jax.ShapeDtypeStruct((8, 1536), f32), jax.ShapeDtypeStruct((8, 128), f32),
                   jax.ShapeDtypeStruct((8, 128), f32), jax.ShapeDtypeStruct((8, 128), f32),
                   jax.ShapeDtypeStruct((8, 1024), f32)],
        scratch=[pltpu.VMEM((72, 1536), f32), pltpu.VMEM((8, 1536), f32), pltpu.VMEM((128, 1024), f32)],
        args=[z, xbc, pre, sm, hs, dy, conv_w, dtb, alog, dpar, nw, cs["tri"], cs["i2"], cs["mask2"], cs["lo"]])


def _gdn_split(pre_fn, gate_ref):
    def heads(base):
        return jnp.stack([pre_fn(base + 128 * h, base + 128 * h + 128) for h in range(GDN_HEADS)])
    gate = jnp.stack([gate_ref[:, 128 * h:128 * h + 128] for h in range(GDN_HEADS)])
    return heads(0), heads(1024), heads(2048), gate


def gdn_fwd(gate, qkv, sm, conv_w, dtb, alog, nw, cs):
    t = gate.shape[0]
    nc = t // CHUNK

    def body(shared, gate_ref, qkv_ref, halo_ref, sm_ref, cw_ref, dtb_ref, alog_ref, nw_ref,
             tri_ref, i64_ref, strict_ref, o_ref, ss_ref, ts_ref, pre_ref, pbuf, s_scr):
        def init():
            s_scr[...] = jnp.zeros_like(s_scr)

        _when_first(shared, init)
        pbuf[0:8, :] = jnp.where(shared["first"], 0.0, halo_ref[...])
        pbuf[8:72, :] = qkv_ref[...]

        def pre_fn(c0, c1):
            pre = _conv_fwd(pbuf, cw_ref, c0, c1)
            pre_ref[:, c0:c1] = pre
            return pre

        q_pre, k_pre, v_pre, g3 = _gdn_split(pre_fn, gate_ref)
        s = s_scr[...]
        ss_ref[0] = s
        out, s_next, tinv = _gdn_chunk(q_pre, k_pre, v_pre, g3, sm_ref[...], s, dtb_ref[...], alog_ref[...],
                                       nw_ref[...], tri_ref[...], i64_ref[...], strict_ref[...])
        ts_ref[0] = tinv
        s_scr[...] = s_next
        for h in range(GDN_HEADS):
            o_ref[:, 128 * h:128 * h + 128] = out[h].astype(o_ref.dtype)

    blk = lambda w: pl.BlockSpec((SUB_FWD * CHUNK, w), lambda i: (i, 0))
    return dict(
        body=body,
        in_kinds=["rows", "rows", ("halo", 1), "rows"] + ["full"] * 7, out_kinds=["rows", "state", "state", "rows"],
        in_specs=[blk(1024), blk(3072), _halo_spec(3072, lambda i: i), blk(128),
                  _full((CONV_K, 3072)), _full((1, 128)), _full((1, 128)), _full((1, 128)),
                  _full((64, 64)), _full((64, 64)), _full((64, 64))],
        out_specs=[blk(1024), pl.BlockSpec((SUB_FWD, 8, 128, 128), lambda i: (i, 0, 0, 0)),
                   pl.BlockSpec((SUB_FWD, 8, CHUNK, CHUNK), lambda i: (i, 0, 0, 0)), blk(3072)],
        out_shape=[jax.ShapeDtypeStruct((t, 1024), _MM), jax.ShapeDtypeStruct((nc, 8, 128, 128), jnp.float32),
                   jax.ShapeDtypeStruct((nc, 8, CHUNK, CHUNK), jnp.float32),
                   jax.ShapeDtypeStruct((t, 3072), jnp.float32)],
        scratch=[pltpu.VMEM((72, 3072), jnp.float32), pltpu.VMEM((8, 128, 128), jnp.float32)],
        args=[gate, qkv, qkv, sm, conv_w, dtb, alog, nw, cs["tri"], cs["i64"], cs["strict"]])


def gdn_bwd(gate, qkv, pre, sm, ss, ts, do, conv_w, dtb, alog, nw, cs):
    t = gate.shape[0]
    nc = t // CHUNK

    def body(shared, gate_ref, qkv_ref, pre_ref, sm_ref, ss_ref, ts_ref, do_ref, cw_ref, dtb_ref, alog_ref,
             nw_ref, tri_ref, i64_ref, strict_ref,
             dgate_ref, dqkv_ref, dsm_ref, dcw_ref, ddtb_ref, dalog_ref, dnw_ref,
             dbuf, carry, ds_scr):
        def init():
            ds_scr[...] = jnp.zeros_like(ds_scr)
            dcw_ref[...] = jnp.zeros_like(dcw_ref)
            ddtb_ref[...] = jnp.zeros_like(ddtb_ref)
            dalog_ref[...] = jnp.zeros_like(dalog_ref)
            dnw_ref[...] = jnp.zeros_like(dnw_ref)

        _when_first(shared, init)

        q_pre, k_pre, v_pre, g3 = _gdn_split(lambda c0, c1: pre_ref[:, c0:c1], gate_ref)
        consts = (tri_ref[...], i64_ref[...], strict_ref[...], ts_ref[0])

        def f(q_pre, k_pre, v_pre, g3, smv, s, dtb, alog, nwv):
            return _gdn_chunk(q_pre, k_pre, v_pre, g3, smv, s, dtb, alog, nwv, *consts)[:2]

        _, vjp = jax.vjp(f, q_pre, k_pre, v_pre, g3, sm_ref[...], ss_ref[0], dtb_ref[...], alog_ref[...], nw_ref[...])
        do3 = jnp.stack([do_ref[:, 128 * h:128 * h + 128] for h in range(GDN_HEADS)])
        dq, dk, dv, dg3, dsm, ds, ddtb, dalog, dnw = vjp((do3, ds_scr[...]))
        ds_scr[...] = ds
        for h in range(GDN_HEADS):
            dgate_ref[:, 128 * h:128 * h + 128] = dg3[h].astype(dgate_ref.dtype)
        dsm_ref[...] = (dsm + shared["dsm_ssd"]).astype(dsm_ref.dtype)
        ddtb_ref[0:1, :] += ddtb
        dalog_ref[0:1, :] += dalog
        dnw_ref[0:1, :] += dnw
        ranges = [(base + 128 * h, base + 128 * h + 128) for base in (0, 1024, 2048) for h in range(GDN_HEADS)]
        dlist = [d[h] for d in (dq, dk, dv) for h in range(GDN_HEADS)]
        _conv_bwd(dlist, ranges, dbuf, carry, qkv_ref, cw_ref, dqkv_ref, dcw_ref, None, shared["first"])

    ns = nc // SUB_BWD
    rblk = lambda w: pl.BlockSpec((SUB_BWD * CHUNK, w), lambda i: (ns - 1 - i, 0))
    acc = lambda w: pl.BlockSpec((8, w), lambda i: (0, 0))
    f32 = jnp.float32
    return dict(
        body=body,
        in_kinds=["rows"] * 4 + ["state", "state", "rows"] + ["full"] * 7, out_kinds=["rows"] * 3 + ["full"] * 4,
        in_specs=[rblk(1024), rblk(3072), rblk(3072), rblk(128),
                  pl.BlockSpec((SUB_BWD, 8, 128, 128), lambda i: (ns - 1 - i, 0, 0, 0)),
                  pl.BlockSpec((SUB_BWD, 8, CHUNK, CHUNK), lambda i: (ns - 1 - i, 0, 0, 0)), rblk(1024),
                  _full((CONV_K, 3072)), _full((1, 128)), _full((1, 128)), _full((1, 128)),
                  _full((64, 64)), _full((64, 64)), _full((64, 64))],
        out_specs=[rblk(1024), rblk(3072), rblk(128), acc(3072), acc(128), acc(128), acc(128)],
        out_shape=[jax.ShapeDtypeStruct((t, 1024), f32), jax.ShapeDtypeStruct((t, 3072), f32),
                   jax.ShapeDtypeStruct((t, 128), f32), jax.ShapeDtypeStruct((8, 3072), f32),
                   jax.ShapeDtypeStruct((8, 128), f32), jax.ShapeDtypeStruct((8, 128), f32),
                   jax.ShapeDtypeStruct((8, 128), f32)],
        scratch=[pltpu.VMEM((72, 3072), f32), pltpu.VMEM((8, 3072), f32), pltpu.VMEM((8, 128, 128), f32)],
        args=[gate, qkv, pre, sm, ss, ts, do, conv_w, dtb, alog, nw, cs["tri"], cs["i64"], cs["strict"]])


def _chunk_call(parts, name, nc, reverse):
    n_in = [len(p["args"]) for p in parts]
    n_out = [len(p["out_shape"]) for p in parts]
    n_scr = [len(p["scratch"]) for p in parts]
    sub = SUB_BWD if reverse else SUB_FWD
    order = list(range(sub))[::-1] if reverse else list(range(sub))

    def view(ref, kind, s, refs):
        if kind == "rows":
            return ref.at[pl.ds(CHUNK * s, CHUNK)]
        if kind == "state":
            return ref.at[pl.ds(s, 1)]
        if kind == "full":
            return ref
        src = refs[kind[1]]
        return ref if s == 0 else src.at[pl.ds(CHUNK * s - 8, 8)]

    def body(*refs):
        ins, outs, scr = refs[:sum(n_in)], refs[sum(n_in):sum(n_in) + sum(n_out)], refs[sum(n_in) + sum(n_out):]
        for s in order:
            shared = {"first": (pl.program_id(0) == 0) if s == order[0] else False}
            for k, p in enumerate(parts):
                i0, o0, s0 = sum(n_in[:k]), sum(n_out[:k]), sum(n_scr[:k])
                p_ins = ins[i0:i0 + n_in[k]]
                p["body"](shared,
                          *[view(r, kd, s, p_ins) for r, kd in zip(p_ins, p["in_kinds"])],
                          *[view(r, kd, s, None) for r, kd in zip(outs[o0:o0 + n_out[k]], p["out_kinds"])],
                          *scr[s0:s0 + n_scr[k]])

    cat = lambda key: [v for p in parts for v in p[key]]
    return _pc(body, name=name, grid=(nc // sub,), in_specs=cat("in_specs"), out_specs=cat("out_specs"),
               out_shape=cat("out_shape"), scratch_shapes=cat("scratch"),
               compiler_params=_cparams(("arbitrary",)))(*cat("args"))


def out_fwd_bwd(x, tgt, y_ssd, y_gdn, w_out, fnw):
    t = x.shape[0]
    tm = min(512, t)
    f32 = jnp.float32

    def body(x_ref, tgt_ref, ys_ref, yg_ref, w_ref, fnw_ref,
             dout_ref, dys_ref, dyg_ref, gw_ref, gfnw_ref, loss_ref, gw_acc):
        i = pl.program_id(0)

        @pl.when(i == 0)
        def _():
            gw_acc[...] = jnp.zeros_like(gw_acc)
            gfnw_ref[...] = jnp.zeros_like(gfnw_ref)
            loss_ref[...] = jnp.zeros_like(loss_ref)

        ys = ys_ref[...]
        yg = yg_ref[...]
        out = x_ref[...] + jnp.dot(ys, w_ref[0:1024, :], preferred_element_type=f32) \
            + jnp.dot(yg, w_ref[1024:2048, :], preferred_element_type=f32)
        rstd = lax.rsqrt(jnp.mean(out * out, axis=-1, keepdims=True) + EPS)
        yhat = out * rstd
        fw = fnw_ref[...]
        e = yhat * fw - tgt_ref[...]
        loss_ref[...] += 0.5 * jnp.sum(jnp.sum(e * e, axis=-1, keepdims=True) * (1.0 / D_MODEL), axis=0, keepdims=True)
        dyf = e * (1.0 / D_MODEL)
        gfnw_ref[0:1, :] += jnp.sum(dyf * yhat, axis=0, keepdims=True)
        dyhat = dyf * fw
        dout = rstd * (dyhat - yhat * jnp.mean(dyhat * yhat, axis=-1, keepdims=True))
        dout_ref[...] = dout
        db = dout.astype(_MM)
        dys_ref[...] = lax.dot_general(db, w_ref[0:1024, :], (((1,), (1,)), ((), ())), preferred_element_type=f32)
        dyg_ref[...] = lax.dot_general(db, w_ref[1024:2048, :], (((1,), (1,)), ((), ())), preferred_element_type=f32)
        gw_acc[0:1024, :] += lax.dot_general(ys, db, (((0,), (0,)), ((), ())), preferred_element_type=f32)
        gw_acc[1024:2048, :] += lax.dot_general(yg, db, (((0,), (0,)), ((), ())), preferred_element_type=f32)

        @pl.when(i == steps - 1)
        def _():
            gw_ref[...] = gw_acc[...].astype(gw_ref.dtype)

    steps = t // tm
    blk = pl.BlockSpec((tm, D_MODEL), lambda i: (i, 0))
    return _pc(
        body, name="out_fwd_bwd", grid=(steps,),
        in_specs=[blk, blk, blk, blk, _full((MIX_WIDTH, D_MODEL)), _full((1, D_MODEL))],
        out_specs=[blk, blk, blk, _full((MIX_WIDTH, D_MODEL)), _full((8, D_MODEL)), _full((1, 128))],
        out_shape=[jax.ShapeDtypeStruct((t, D_MODEL), f32)] * 3 +
                  [jax.ShapeDtypeStruct((MIX_WIDTH, D_MODEL), _MM), jax.ShapeDtypeStruct((8, D_MODEL), f32),
                   jax.ShapeDtypeStruct((1, 128), f32)],
        scratch_shapes=[pltpu.VMEM((MIX_WIDTH, D_MODEL), f32)],
        compiler_params=_cparams(("arbitrary",)),
    )(x, tgt, y_ssd, y_gdn, w_out, fnw)


def inproj_bwd_dx(x, dout, norm_w, w_perm, dgroups, scattered):
    t = x.shape[0]
    tm = min(256, t)
    f32 = jnp.float32

    def body(x_ref, dout_ref, nw_ref, w_ref, dz_ref, dxbc_ref, dgate_ref, dqkv_ref, dsm_ref, dx_ref, gnw_ref):
        i = pl.program_id(0)

        @pl.when(i == 0)
        def _():
            gnw_ref[...] = jnp.zeros_like(gnw_ref)

        du = None
        for (name, c0, c1), d_ref in zip(GROUPS, (dz_ref, dxbc_ref, dgate_ref, dqkv_ref, dsm_ref)):
            term = jnp.dot(d_ref[...].astype(_MM), _w_rows(w_ref, name, c1 - c0), preferred_element_type=f32)
            du = term if du is None else du + term
        xf = x_ref[...]
        rstd = lax.rsqrt(jnp.mean(xf * xf, axis=-1, keepdims=True) + EPS)
        xhat = xf * rstd
        gnw_ref[0:1, :] += jnp.sum(du * xhat, axis=0, keepdims=True)
        dxh = du * nw_ref[...]
        dx_ref[...] = dout_ref[...] + rstd * (dxh - xhat * jnp.mean(dxh * xhat, axis=-1, keepdims=True))

    blk = lambda w: pl.BlockSpec((tm, w), lambda i: (i, 0))
    steps = t // tm
    kinds = ["scatter"] * len(scattered)
    hosted = _hosting(body, 9, 2, 0, kinds, lambda: pl.program_id(0) == 0, lambda: pl.program_id(0) == steps - 1)
    return _pc_comm(
        hosted, name="inproj_bwd_dx", grid=(steps,),
        in_specs=[blk(D_MODEL), blk(D_MODEL), _full((1, D_MODEL)), _full((IN_DIM, D_MODEL))] +
                 [blk(c1 - c0) for _, c0, c1 in GROUPS] + [ANY] * len(scattered),
        out_specs=[blk(D_MODEL), _full((8, D_MODEL))] + [ANY] * len(scattered),
        out_shape=[jax.ShapeDtypeStruct((t, D_MODEL), f32), jax.ShapeDtypeStruct((8, D_MODEL), f32)] +
                  [_exchange_out_shape("scatter", a) for a in scattered],
        scratch_shapes=_exchange_sems(len(scattered)), compiler_params=_cparams(("arbitrary",)),
    )(x, dout, norm_w, w_perm, *dgroups, *scattered)


def grad_w_group(u, dg, name, scattered=()):
    t, n = dg.shape
    tn = 512 if n % 512 == 0 else n
    tm = 2048 if t % 2048 == 0 else t
    nj, nk = n // tn, t // tm
    f32 = jnp.float32

    def body(u_ref, d_ref, o_ref, acc):
        k = pl.program_id(1)

        @pl.when(k == 0)
        def _():
            acc[...] = jnp.zeros_like(acc)

        acc[...] += lax.dot_general(d_ref[...].astype(_MM), u_ref[...], (((0,), (0,)), ((), ())),
                                    preferred_element_type=f32)

        @pl.when(k == nk - 1)
        def _():
            o_ref[...] = acc[...].astype(o_ref.dtype)

    ne = len(scattered)
    hosted = _hosting(body, 2, 1, 1, ["scatter"] * ne,
                      lambda: (pl.program_id(0) == 0) & (pl.program_id(1) == 0),
                      lambda: (pl.program_id(0) == nj - 1) & (pl.program_id(1) == nk - 1))
    res = (_pc_comm if ne else _pc)(
        hosted, name=name, grid=(nj, nk),
        in_specs=[pl.BlockSpec((tm, D_MODEL), lambda j, k: (k, 0)),
                  pl.BlockSpec((tm, tn), lambda j, k: (k, j))] + [ANY] * ne,
        out_specs=[pl.BlockSpec((tn, D_MODEL), lambda j, k: (j, 0))] + [ANY] * ne,
        out_shape=[jax.ShapeDtypeStruct((n, D_MODEL), _MM)] + [_exchange_out_shape("scatter", a) for a in scattered],
        scratch_shapes=[pltpu.VMEM((tn, D_MODEL), f32)] + _exchange_sems(ne),
        compiler_params=_cparams(("arbitrary", "arbitrary")),
    )(u, dg, *scattered)
    return res if ne else res[0]


def _pad_lanes(v, off):
    n = v.shape[-1]
    return jnp.pad(v.reshape(1, n).astype(jnp.float32), ((0, 0), (off, 128 - off - n)))


REF_ROWS = dict(z=(0, 1024), xbc=(1024, 2560), dt=(2560, 2576), gate=(2576, 3600), qkv=(3600, 6672), ab=(6672, 6688))


def unperm_w_in(gz, gxbc, ggate, gqkv, gsm):
    src = dict(z=gz, xbc=gxbc, dt=gsm[0:16], gate=ggate, qkv=gqkv, ab=gsm[16:32])
    slabs = []
    for k in range(N_DEV):
        a, b = k * W_IN_SHARD, (k + 1) * W_IN_SHARD
        parts = []
        for name, (s, e) in REF_ROWS.items():
            lo, hi = max(a, s), min(b, e)
            if lo < hi:
                parts.append(src[name][lo - s:hi - s])
        slabs.append(jnp.concatenate(parts, axis=0))
    return jnp.stack(slabs)


def all_gather(arrs, name):
    n = len(arrs)

    def body(*refs):
        ins, outs = refs[:n], refs[n:2 * n]
        send_sems, recv_sems, local_sems = refs[2 * n:]
        x, y, c, me = _me()
        sibling = (x, y, 1 - c)
        chips = [(1 - x, y), (x, 1 - y), (1 - x, 1 - y)]

        def idx(px, py, pc):
            return 4 * px + 2 * py + pc

        def copy(a, k, block, to, src=None):
            slot = outs[a].at[idx(*block)]
            return pltpu.make_async_remote_copy(src_ref=slot if src is None else src, dst_ref=slot,
                                                send_sem=send_sems.at[a, k], recv_sem=recv_sems.at[a, k],
                                                device_id=to, device_id_type=MESH)

        local = [pltpu.make_async_copy(ins[a], outs[a].at[me], local_sems.at[a]) for a in range(n)]
        for cp in local:
            cp.start()
        started = []
        for a in range(n):
            first = [copy(a, 0, (x, y, c), sibling, src=ins[a])]
            first += [copy(a, 1 + j, (x, y, c), (*chip, c), src=ins[a]) for j, chip in enumerate(chips)]
            for cp in first:
                cp.start()
            started += first
        for a in range(n):
            for j, chip in enumerate(chips):
                copy(a, 1 + j, (*chip, c), (x, y, c)).wait_recv()
                fwd = copy(a, 4 + j, (*chip, c), sibling)
                fwd.start()
                started.append(fwd)
        for a in range(n):
            copy(a, 0, sibling, (x, y, c)).wait_recv()
            for j, chip in enumerate(chips):
                copy(a, 4 + j, (*chip, 1 - c), (x, y, c)).wait_recv()
        for cp in started:
            cp.wait_send()
        for cp in local:
            cp.wait()

    return _pc_comm(
        body, name=name, in_specs=[ANY] * n, out_specs=[ANY] * n,
        out_shape=[jax.ShapeDtypeStruct((N_DEV,) + a.shape, a.dtype) for a in arrs],
        scratch_shapes=[pltpu.SemaphoreType.DMA((n, 7)), pltpu.SemaphoreType.DMA((n, 7)),
                        pltpu.SemaphoreType.DMA((n,))],
    )(*arrs)


def adamw_sum(recv, w, m, v, rows, name, cols=None):
    r, ccols = w.shape
    f32 = jnp.float32
    c1 = 1.0 / (1.0 - ADAM_B1 ** ADAM_STEP)
    c2 = 1.0 / (1.0 - ADAM_B2 ** ADAM_STEP)

    def body(recv_ref, w_ref, m_ref, v_ref, g_ref, d_ref, mo_ref, vo_ref):
        g = recv_ref[0].astype(f32)
        for k in range(1, N_DEV):
            g = g + recv_ref[k].astype(f32)
        mn = ADAM_B1 * m_ref[...] + (1.0 - ADAM_B1) * g
        vn = ADAM_B2 * v_ref[...] + (1.0 - ADAM_B2) * (g * g)
        g_ref[...] = g
        mo_ref[...] = mn
        vo_ref[...] = vn
        d_ref[...] = -ADAM_LR * ((mn * c1) / (jnp.sqrt(vn * c2) + ADAM_EPS) + ADAM_WD * w_ref[...])

    if cols is None:
        blk = pl.BlockSpec((rows, ccols), lambda i: (i, 0))
        rblk, steps = pl.BlockSpec((N_DEV, rows, ccols), lambda i: (0, i, 0)), r // rows
    else:
        blk = pl.BlockSpec((r, cols), lambda i: (0, i))
        rblk, steps = pl.BlockSpec((N_DEV, r, cols), lambda i: (0, 0, i)), ccols // cols
    return _pc(
        body, name=name, grid=(steps,),
        in_specs=[rblk, blk, blk, blk],
        out_specs=[blk] * 4, out_shape=[jax.ShapeDtypeStruct((r, ccols), f32)] * 4,
        compiler_params=_cparams(("arbitrary",)),
    )(recv, w, m, v)


SMALL = (("norm_w", 1, 1024, 0), ("ssd_conv_b", 1, 1536, 0), ("ssd_dt_bias", 1, 16, 0), ("ssd_a_log", 1, 16, 0),
         ("ssd_d", 1, 16, 0), ("ssd_norm_w", 1, 1024, 0), ("gdn_dt_bias", 1, 8, 16), ("gdn_a_log", 1, 8, 16),
         ("gdn_norm_w", 1, 128, 0), ("final_norm_w", 1, 1024, 0),
         ("ssd_conv_w", CONV_K, SSD_CONV_DIM // N_DEV, 0), ("gdn_conv_w", CONV_K, GDN_CONV_DIM // N_DEV, 0))


def _small_layout():
    out, off = [], 0
    for name, rows, n, lane0 in SMALL + (("loss", 1, 128, 0),):
        stride = -(-(lane0 + n) // 128) * 128
        out.append((name, rows, n, lane0, stride, off))
        off += rows * stride
    return out, off


def scatter_small(accs):
    layout, total = _small_layout()
    f32 = jnp.float32

    def body(*refs):
        acc_refs, out_ref, slabs = refs[:len(layout)], refs[len(layout)], refs[len(layout) + 1]
        sems = refs[len(layout) + 2:]
        slabs[...] = jnp.zeros_like(slabs)
        for (name, rows, n, lane0, stride, off), acc in zip(layout, acc_refs):
            for k in range(N_DEV):
                if rows == 1:
                    slabs[k, :, off:off + stride] = acc[0:1, 0:stride]
                else:
                    for j in range(rows):
                        slabs[k, :, off + stride * j:off + stride * j + n] = acc[j:j + 1, n * k:n * k + n]
        start, wait = _exchange_ops("scatter", slabs, out_ref, *sems)
        start()
        wait()

    return _pc_comm(
        body, name="scatter_small_grads", out_specs=ANY, out_shape=jax.ShapeDtypeStruct((N_DEV, 1, total), f32),
        scratch_shapes=[pltpu.VMEM((N_DEV, 1, total), f32)] + _exchange_sems(1),
    )(*accs)


def adamw_small(recv, w, m, v):
    layout, total = _small_layout()
    loss_off = layout[-1][5]
    layout = layout[:-1]
    f32 = jnp.float32
    c1 = 1.0 / (1.0 - ADAM_B1 ** ADAM_STEP)
    c2 = 1.0 / (1.0 - ADAM_B2 ** ADAM_STEP)
    np_ = len(layout)

    def body(*refs):
        recv_ref = refs[0]
        w_refs, m_refs, v_refs = refs[1:1 + np_], refs[1 + np_:1 + 2 * np_], refs[1 + 2 * np_:1 + 3 * np_]
        o_refs = refs[1 + 3 * np_:]
        g_all = recv_ref[0]
        for k in range(1, N_DEV):
            g_all = g_all + recv_ref[k]
        o_refs[4 * np_][...] = g_all[:, loss_off:loss_off + 128]

        def update(g, wv, mv, vv):
            mn = ADAM_B1 * mv + (1.0 - ADAM_B1) * g
            vn = ADAM_B2 * vv + (1.0 - ADAM_B2) * (g * g)
            return g, -ADAM_LR * ((mn * c1) / (jnp.sqrt(vn * c2) + ADAM_EPS) + ADAM_WD * wv), mn, vn

        for p, (name, rows, n, lane0, stride, off) in enumerate(layout):
            outs = o_refs[4 * p:4 * p + 4]
            if rows == 1:
                res = update(g_all[:, off + lane0:off + lane0 + n], w_refs[p][...], m_refs[p][...], v_refs[p][...])
                for o, r in zip(outs, res):
                    o[...] = r
            else:
                for j in range(rows):
                    res = update(g_all[:, off + stride * j:off + stride * j + n], w_refs[p][0, j:j + 1, :],
                                 m_refs[p][0, j:j + 1, :], v_refs[p][0, j:j + 1, :])
                    for o, r in zip(outs, res):
                        o[0, j:j + 1, :] = r

    names = [e[0] for e in layout]
    ins = [recv] + [d[nm] for d in (w, m, v) for nm in names]
    out_shape = [jax.ShapeDtypeStruct(w[nm].shape, f32) for nm in names for _ in range(4)]
    out_shape.append(jax.ShapeDtypeStruct((1, 128), f32))
    res = _pc(body, name="adamw_small", out_shape=out_shape)(*ins)
    return {nm: tuple(res[4 * p:4 * p + 4]) for p, nm in enumerate(names)}, res[4 * np_]


SHARD = (("ssd_conv_w", CONV_K * SSD_CONV_DIM // N_DEV), ("gdn_conv_w", CONV_K * GDN_CONV_DIM // N_DEV))
SHARD_ROWS = 24


def _rows_of(size):
    return -(-size // 128)


def _pack(vals, layout, total_rows):
    parts = []
    for (name, size), val in zip(layout, vals):
        flat = val.reshape(-1).astype(jnp.float32)
        parts.append(jnp.pad(flat, (0, _rows_of(size) * 128 - size)).reshape(-1, 128))
    used = sum(_rows_of(s) for _, s in layout)
    parts.append(jnp.zeros((total_rows - used, 128), jnp.float32))
    return jnp.concatenate(parts, axis=0)


def _conv_full(gathered_flat, ccols):
    return gathered_flat.reshape(N_DEV, CONV_K, ccols // N_DEV).transpose(1, 0, 2).reshape(CONV_K, ccols)


def kernel(x, norm_w, w_in, ssd_conv_w, ssd_conv_b, ssd_dt_bias, ssd_a_log, ssd_d, ssd_norm_w, gdn_conv_w, gdn_dt_bias, gdn_a_log, gdn_norm_w, w_out, final_norm_w, loss_target, m_norm_w, m_w_in, m_ssd_conv_w, m_ssd_conv_b, m_ssd_dt_bias, m_ssd_a_log, m_ssd_d, m_ssd_norm_w, m_gdn_conv_w, m_gdn_dt_bias, m_gdn_a_log, m_gdn_norm_w, m_w_out, m_final_norm_w, v_norm_w, v_w_in, v_ssd_conv_w, v_ssd_conv_b, v_ssd_dt_bias, v_ssd_a_log, v_ssd_d, v_ssd_norm_w, v_gdn_conv_w, v_gdn_dt_bias, v_gdn_a_log, v_gdn_norm_w, v_w_out, v_final_norm_w):
    f32 = jnp.float32
    w = dict(norm_w=norm_w, w_in=w_in, ssd_conv_w=ssd_conv_w, ssd_conv_b=ssd_conv_b, ssd_dt_bias=ssd_dt_bias,
             ssd_a_log=ssd_a_log, ssd_d=ssd_d, ssd_norm_w=ssd_norm_w, gdn_conv_w=gdn_conv_w, gdn_dt_bias=gdn_dt_bias,
             gdn_a_log=gdn_a_log, gdn_norm_w=gdn_norm_w, w_out=w_out, final_norm_w=final_norm_w)
    m = dict(norm_w=m_norm_w, w_in=m_w_in, ssd_conv_w=m_ssd_conv_w, ssd_conv_b=m_ssd_conv_b, ssd_dt_bias=m_ssd_dt_bias,
             ssd_a_log=m_ssd_a_log, ssd_d=m_ssd_d, ssd_norm_w=m_ssd_norm_w, gdn_conv_w=m_gdn_conv_w,
             gdn_dt_bias=m_gdn_dt_bias, gdn_a_log=m_gdn_a_log, gdn_norm_w=m_gdn_norm_w, w_out=m_w_out,
             final_norm_w=m_final_norm_w)
    v = dict(norm_w=v_norm_w, w_in=v_w_in, ssd_conv_w=v_ssd_conv_w, ssd_conv_b=v_ssd_conv_b, ssd_dt_bias=v_ssd_dt_bias,
             ssd_a_log=v_ssd_a_log, ssd_d=v_ssd_d, ssd_norm_w=v_ssd_norm_w, gdn_conv_w=v_gdn_conv_w,
             gdn_dt_bias=v_gdn_dt_bias, gdn_a_log=v_gdn_a_log, gdn_norm_w=v_gdn_norm_w, w_out=v_w_out,
             final_norm_w=v_final_norm_w)
    names = list(w)
    shapes = {n: w[n].shape for n in names}

    xl, tgt = x[0], loss_target[0]
    cs = _consts()
    dtb_s = _pad_lanes(ssd_dt_bias, 0)
    alog_s = _pad_lanes(ssd_a_log, 0)
    dpar = _pad_lanes(ssd_d, 0)
    dtb_g = _pad_lanes(gdn_dt_bias, 16)
    alog_g = _pad_lanes(gdn_a_log, 16)
    nw_g = gdn_norm_w.reshape(1, 128)
    nw_s = ssd_norm_w.reshape(1, 1024)
    cb_s = ssd_conv_b.reshape(1, 1536)
    nw1 = norm_w.reshape(1, D_MODEL)

    (g_w_in,) = all_gather([w_in[0].T.astype(_MM)], "gather_w_in")
    w_perm = g_w_in.reshape(IN_DIM, D_MODEL)
    conv_pack = _pack([w["ssd_conv_w"], w["gdn_conv_w"]], SHARD, SHARD_ROWS)
    u, z, xbc, gate, qkv, sm, g_w_out, g_conv = inproj_fwd(xl, nw1, w_perm, [w_out[0].astype(_MM), conv_pack])
    w_out_full = g_w_out.reshape(MIX_WIDTH, D_MODEL)
    ssd_cw = _conv_full(g_conv[:, 0:6].reshape(N_DEV, -1), SSD_CONV_DIM)
    gdn_cw = _conv_full(g_conv[:, 6:18].reshape(N_DEV, -1), GDN_CONV_DIM)

    nc = xl.shape[0] // CHUNK
    y_ssd, hs, pre_s, y_gdn, ss, ts, pre_g = _chunk_call(
        [ssd_fwd(z, xbc, sm, ssd_cw, cb_s, dtb_s, alog_s, dpar, nw_s, cs),
         gdn_fwd(gate, qkv, sm, gdn_cw, dtb_g, alog_g, nw_g, cs)], "scan_fwd", nc, False)
    dout, dys, dyg, g_wout, g_fnw, loss_l = out_fwd_bwd(xl, tgt, y_ssd, y_gdn, w_out_full,
                                                        final_norm_w.reshape(1, D_MODEL))
    (dz, dxbc, g_cw_s, g_cb_s, g_dtb_s, g_alog_s, g_d, g_nw_s,
     dgate, dqkv, dsm, g_cw_g, g_dtb_g, g_alog_g, g_nw_g) = _chunk_call(
        [ssd_bwd(z, xbc, pre_s, sm, hs, dys, ssd_cw, dtb_s, alog_s, dpar, nw_s, cs),
         gdn_bwd(gate, qkv, pre_g, sm, ss, ts, dyg, gdn_cw, dtb_g, alog_g, nw_g, cs)], "scan_bwd", nc, True)

    t_w_out = g_wout.reshape(N_DEV, MIX_WIDTH // N_DEV, D_MODEL)
    gws = {}
    for dg, (name, _, _) in zip((dz, dxbc, dgate, dsm), (GROUPS[0], GROUPS[1], GROUPS[2], GROUPS[4])):
        gws[name] = grad_w_group(u, dg, "grad_w_in_" + name)
    gws["qkv"], r_w_out = grad_w_group(u, dqkv, "grad_w_in_qkv", [t_w_out])
    t_w_in = unperm_w_in(gws["z"], gws["xbc"], gws["gate"], gws["qkv"], gws["sm"])
    dx, g_nw, r_w_in = inproj_bwd_dx(xl, dout, nw1, w_perm, (dz, dxbc, dgate, dqkv, dsm), [t_w_in])

    accs = dict(norm_w=g_nw, ssd_conv_b=g_cb_s, ssd_dt_bias=g_dtb_s, ssd_a_log=g_alog_s, ssd_d=g_d,
                ssd_norm_w=g_nw_s, gdn_dt_bias=g_dtb_g, gdn_a_log=g_alog_g, gdn_norm_w=g_nw_g, final_norm_w=g_fnw,
                ssd_conv_w=g_cw_s, gdn_conv_w=g_cw_g)
    r_small = scatter_small([accs[e[0]] for e in SMALL] + [loss_l])

    o_w_in = adamw_sum(r_w_in, w_in[0].T, m_w_in[0].T, v_w_in[0].T, None, "adamw_w_in", cols=256)
    o_w_out = adamw_sum(r_w_out, w_out[0], m_w_out[0], v_w_out[0], 64, "adamw_w_out")
    row = lambda d: {n: (a.reshape(1, -1) if a.ndim == 1 else a) for n, a in d.items()}
    o_small, loss_sum = adamw_small(r_small, row(w), row(m), row(v))

    loss = loss_sum[0, 0]
    outs = [loss, dx[None]]
    for k in range(4):
        parts = {n: o_small[n][k] for n in o_small}
        parts["w_in"] = o_w_in[k].T
        parts["w_out"] = o_w_out[k]
        outs += [parts[n].reshape(shapes[n]) for n in names]
    return tuple(outs)
```

```python
import functools

import jax
import jax.numpy as jnp
import numpy as np
from jax import lax
from jax.experimental import pallas as pl
from jax.experimental.pallas import tpu as pltpu

_MM = jnp.bfloat16

D_MODEL = 1024
CHUNK = 64
CONV_K = 4
EPS = 1e-6
SSD_CONV_DIM = 1536
GDN_HEADS = 8
GDN_DK = 128
GDN_CONV_DIM = 3072
MIX_WIDTH = 2048
IN_DIM = 6688
N_DEV = 8
W_IN_SHARD = IN_DIM // N_DEV
HI = lax.Precision.HIGHEST
HIGH = lax.Precision.HIGH
VMEM_LIMIT = 56 * 1024 * 1024

ADAM_LR = 0.001
ADAM_B1 = 0.9
ADAM_B2 = 0.999
ADAM_EPS = 1e-08
ADAM_WD = 0.01
ADAM_STEP = 10


def _pc(body, **kw):
    return pl.pallas_call(body, **kw)


def _pc_comm(body, **kw):
    return pl.pallas_call(body, **kw)


def _cparams(sem):
    return pltpu.CompilerParams(dimension_semantics=sem, vmem_limit_bytes=VMEM_LIMIT)


def _sig(x):
    return 0.5 * jnp.tanh(0.5 * x) + 0.5


@jax.custom_vjp
def _sigmoid(x):
    return _sig(x)


def _sigmoid_fwd(x):
    s = _sig(x)
    return s, s


def _sigmoid_bwd(s, g):
    return (g * s * (1.0 - s),)


_sigmoid.defvjp(_sigmoid_fwd, _sigmoid_bwd)


@jax.custom_vjp
def _silu(x):
    return x * _sig(x)


def _silu_fwd(x):
    s = _sig(x)
    return x * s, (x, s)


def _silu_bwd(res, g):
    x, s = res
    return (g * (s * (1.0 + x * (1.0 - s))),)


_silu.defvjp(_silu_fwd, _silu_bwd)


def _softplus_impl(x):
    return jnp.maximum(x, 0.0) + jnp.log(1.0 + jnp.exp(-jnp.abs(x)))


@jax.custom_vjp
def _softplus(x):
    return _softplus_impl(x)


def _softplus_fwd(x):
    return _softplus_impl(x), x


def _softplus_bwd(x, g):
    return (g * _sig(x),)


_softplus.defvjp(_softplus_fwd, _softplus_bwd)


def _lane_bcast_impl(x, k):
    return jnp.broadcast_to(x[..., k:k + 1], x.shape)


@functools.partial(jax.custom_vjp, nondiff_argnums=(1,))
def _lane_bcast(x, k):
    return _lane_bcast_impl(x, k)


def _lane_bcast_fwd(x, k):
    return _lane_bcast_impl(x, k), None


def _lane_bcast_bwd(k, _, g):
    lane = lax.broadcasted_iota(jnp.int32, g.shape, g.ndim - 1)
    return (jnp.where(lane == k, jnp.sum(g, axis=-1, keepdims=True), 0.0),)


_lane_bcast.defvjp(_lane_bcast_fwd, _lane_bcast_bwd)


def _mm(a, b):
    return jnp.dot(a.astype(_MM), b.astype(_MM), preferred_element_type=jnp.float32)


def _mm_nt(a, b):
    return lax.dot_general(a.astype(_MM), b.astype(_MM), (((1,), (1,)), ((), ())),
                           preferred_element_type=jnp.float32)


def _mm_tn(a, b):
    return lax.dot_general(a.astype(_MM), b.astype(_MM), (((0,), (0,)), ((), ())),
                           preferred_element_type=jnp.float32)


def _dot_hi(a, b):
    return jnp.dot(a, b, precision=HI, preferred_element_type=jnp.float32)


def _bmm(a, b):
    return lax.dot_general(a.astype(_MM), b.astype(_MM), (((2,), (1,)), ((0,), (0,))),
                           preferred_element_type=jnp.float32)


def _bmm_nt(a, b):
    return lax.dot_general(a.astype(_MM), b.astype(_MM), (((2,), (2,)), ((0,), (0,))),
                           preferred_element_type=jnp.float32)


def _bmm_tn(a, b):
    return lax.dot_general(a.astype(_MM), b.astype(_MM), (((1,), (1,)), ((0,), (0,))),
                           preferred_element_type=jnp.float32)


def _bmm_hi(a, b):
    return lax.dot_general(a, b, (((2,), (1,)), ((0,), (0,))), precision=HIGH, preferred_element_type=jnp.float32)


def _bmm_nt_hi(a, b):
    return lax.dot_general(a, b, (((2,), (2,)), ((0,), (0,))), precision=HIGH, preferred_element_type=jnp.float32)


def _bmm_tn_hi(a, b):
    return lax.dot_general(a, b, (((1,), (1,)), ((0,), (0,))), precision=HIGH, preferred_element_type=jnp.float32)


def _consts():
    l = np.arange(CHUNK)
    tri = (l[:, None] >= l[None, :]).astype(np.float32)
    lane = np.arange(128)
    i2 =(l[:, None] == (lane[None, :] % 64)).astype(np.float32)
    mask2 = (l[:, None] >= (lane[None, :] % 64)).astype(np.float32)
    lo = (lane < 64).astype(np.float32)[None, :]
    i64 = np.eye(CHUNK, dtype=np.float32)
    strict = (l[:, None] > l[None, :]).astype(np.float32)
    return dict(tri=jnp.asarray(tri), i2=jnp.asarray(i2), mask2=jnp.asarray(mask2), lo=jnp.asarray(lo),
                i64=jnp.asarray(i64), strict=jnp.asarray(strict))


def _ssd_chunk(xs_pre, b_pre, c_pre, z, sm, ht, dtb, alog, dpar, nw, tri, i2, mask2, lo):
    lane = lax.broadcasted_iota(jnp.int32, (1, 128), 1)
    m16 = lane < 16
    dt = jnp.where(m16, _softplus(sm + dtb), 0.0)
    a_neg = -jnp.exp(alog)
    cum = _dot_hi(tri, dt * a_neg)
    row = lax.broadcasted_iota(jnp.int32, (CHUNK, 1), 0)
    is_last = row == CHUNK - 1
    hi = 1.0 - lo
    bm = [_silu(b) for b in b_pre]
    cm = [_silu(c) for c in c_pre]
    cb2 = [_mm_nt(cm[g], jnp.concatenate([bm[g], bm[g]], axis=0)) for g in range(2)]
    yg, ht_next = [], []
    for j in range(8):
        g = j // 4
        pair = lambda v, j=j: jnp.where(lo > 0.5, _lane_bcast(v, 2 * j), _lane_bcast(v, 2 * j + 1))
        xs = _silu(xs_pre[j])
        dte = pair(dt)
        cume = pair(cum)
        cum_last = jnp.sum(jnp.where(is_last, cume, 0.0), axis=0, keepdims=True)
        xdt = xs * dte
        rowv = jnp.sum(cume * i2, axis=0, keepdims=True)
        lm = jnp.exp(jnp.where(mask2 > 0.5, cume - rowv, -jnp.inf))
        m = cb2[g] * lm
        xblk = jnp.concatenate([xdt * lo, xdt * hi], axis=0)
        y = _mm(m, xblk)
        y = y + _mm(cm[g], ht[j]) * jnp.exp(cume)
        y = y + pair(dpar) * xs
        yg.append(y * _silu(z[j]))
        st = _mm_tn(bm[g], xdt * jnp.exp(cum_last - cume))
        ht_next.append(ht[j] * jnp.exp(cum_last) + st)
    outs = []
    for g in range(2):
        ss = sum(jnp.sum(yg[j] * yg[j], axis=-1, keepdims=True) for j in range(4 * g, 4 * g + 4))
        rs = lax.rsqrt(ss * (1.0 / 512.0) + EPS)
        for j in range(4 * g, 4 * g + 4):
            outs.append(yg[j] * rs * nw[j])
    return outs, ht_next


def _tri_inverse(a):
    eye = jnp.eye(CHUNK, dtype=jnp.float32)[None]
    p = eye - a
    x = _bmm_hi(a, a)
    for _ in range(4):
        both = _bmm_hi(jnp.concatenate([p, x], axis=1), x)
        p = p + both[:, :CHUNK]
        x = both[:, CHUNK:]
    return p + _bmm_hi(p, x)


def _solve_apply(t, r1, r2):
    both = _bmm_hi(t, jnp.concatenate([r1, r2], axis=-1))
    n = r1.shape[-1]
    return both[..., :n], both[..., n:]


@jax.custom_vjp
def _solve(a, r1, r2, t):
    return _solve_apply(t, r1, r2)


def _solve_fwd(a, r1, r2, t):
    u, w = _bmm_hi(t, r1), _bmm_hi(t, r2)
    return (u, w), (t, u, w)


def _solve_bwd(res, cts):
    t, u, w = res
    du, dw = cts
    dr1 = _bmm_tn_hi(t, du)
    dr2 = _bmm_tn_hi(t, dw)
    da = -(_bmm_nt_hi(dr1, u) + _bmm_nt_hi(dr2, w))
    return da, dr1, dr2, jnp.zeros_like(t)


_solve.defvjp(_solve_fwd, _solve_bwd)


def _gdn_chunk(q_pre, k_pre, v_pre, gate, sm, s, dtb, alog, nw, tri, i64, strict, t_in=None):
    lane = lax.broadcasted_iota(jnp.int32, (1, 128), 1)
    m_a = (lane >= 16) & (lane < 24)
    g_full = jnp.where(m_a, -jnp.exp(alog) * _softplus(sm + dtb), 0.0)
    gc = _dot_hi(tri, g_full)
    sig = _sigmoid(sm)
    gc3 = jnp.stack([_lane_bcast(gc, 16 + h) for h in range(GDN_HEADS)])
    beta3 = jnp.stack([_lane_bcast(sig, 24 + h) for h in range(GDN_HEADS)])
    q = _silu(q_pre)
    q = q * lax.rsqrt(jnp.sum(q * q, axis=-1, keepdims=True) + EPS) * (GDN_DK ** -0.5)
    k = _silu(k_pre)
    k = k * lax.rsqrt(jnp.sum(k * k, axis=-1, keepdims=True) + EPS)
    v = _silu(v_pre)
    gcl = gc3[:, :, :CHUNK]
    gc_row = jnp.sum(gcl * i64[None], axis=1, keepdims=True)
    incl = (strict + i64)[None] > 0.5
    decay = jnp.exp(jnp.where(incl, gcl - gc_row, -jnp.inf))
    kb = k * beta3
    a = jnp.where(strict[None] > 0.5, _bmm_nt(kb, k) * decay, 0.0)
    egc = jnp.exp(gc3)
    t = _tri_inverse(a) if t_in is None else t_in
    u, w = _solve(a, v * beta3, kb * egc, t)
    attn = _bmm_nt(q, k) * decay
    row = lax.broadcasted_iota(jnp.int32, (1, CHUNK, 1), 1)
    gl = jnp.sum(jnp.where(row == CHUNK - 1, gc3, 0.0), axis=1, keepdims=True)
    q_dec = q * egc
    k_dec = k * jnp.exp(gl - gc3)
    v_new = u - _bmm(w, s)
    o = _bmm(q_dec, s) + _bmm(attn, v_new)
    s_next = s * jnp.exp(gl) + _bmm_tn(k_dec, v_new)
    on = o * lax.rsqrt(jnp.mean(o * o, axis=-1, keepdims=True) + EPS) * nw
    return on * _silu(gate), s_next, t


def _conv_fwd(pbuf, w_ref, c0, c1):
    blk = pbuf[:, c0:c1]
    acc = w_ref[CONV_K - 1:CONV_K, c0:c1] * blk[8:72]
    for j in range(CONV_K - 1):
        acc = acc + w_ref[j:j + 1, c0:c1] * pltpu.roll(blk, CONV_K - 1 - j, axis=0)[8:72]
    return acc


MESH = pl.DeviceIdType.MESH
ANY = pl.BlockSpec(memory_space=pl.ANY)


def _me():
    x, y, c = lax.axis_index("x"), lax.axis_index("y"), lax.axis_index("c")
    return x, y, c, 4 * x + 2 * y + c


def _peer(r):
    x, y, c, _ = _me()
    px = 1 - x if r & 4 else x
    py = 1 - y if r & 2 else y
    pc = 1 - c if r & 1 else c
    return (px, py, pc), 4 * px + 2 * py + pc


def _exchange_ops(kind, in_ref, out_ref, send_sems, recv_sems, local_sem):
    me = _me()[3]
    local = pltpu.make_async_copy(in_ref.at[me] if kind == "scatter" else in_ref, out_ref.at[me], local_sem)
    sends, recvs = [], []
    for r in range(1, N_DEV):
        peer, pidx = _peer(r)
        src = in_ref.at[pidx] if kind == "scatter" else in_ref
        sems = dict(send_sem=send_sems.at[r - 1], recv_sem=recv_sems.at[r - 1], device_id=peer, device_id_type=MESH)
        sends.append(pltpu.make_async_remote_copy(src_ref=src, dst_ref=out_ref.at[me], **sems))
        recvs.append(pltpu.make_async_remote_copy(src_ref=src, dst_ref=out_ref.at[pidx], **sems))

    def start():
        local.start()
        for cp in sends:
            cp.start()

    def wait():
        for cp in recvs:
            cp.wait_recv()
        for cp in sends:
            cp.wait_send()
        local.wait()

    return start, wait


def _exchange_sems(n):
    return [pltpu.SemaphoreType.DMA((N_DEV - 1,)), pltpu.SemaphoreType.DMA((N_DEV - 1,)),
            pltpu.SemaphoreType.DMA(())] * n


def _exchange_out_shape(kind, a):
    return jax.ShapeDtypeStruct(a.shape if kind == "scatter" else (N_DEV,) + a.shape, a.dtype)


def _hosting(body, n_in, n_out, n_scratch, kinds, first, last):
    ne = len(kinds)

    def wrapped(*refs):
        ins, ex_in = refs[:n_in], refs[n_in:n_in + ne]
        o0 = n_in + ne
        outs, ex_out = refs[o0:o0 + n_out], refs[o0 + n_out:o0 + n_out + ne]
        s0 = o0 + n_out + ne
        scr, sems = refs[s0:s0 + n_scratch], refs[s0 + n_scratch:]
        ops = [_exchange_ops(kinds[e], ex_in[e], ex_out[e], *sems[3 * e:3 * e + 3]) for e in range(ne)]

        @pl.when(first())
        def _():
            for start, _ in ops:
                start()

        body(*ins, *outs, *scr)

        @pl.when(last())
        def _():
            for _, wait in ops:
                wait()

    return wrapped


GROUPS = (("z", 0, 1024), ("xbc", 1024, 2560), ("gate", 2560, 3584), ("qkv", 3584, 6656), ("sm", 6656, 6784))
GROUP_ROWS = dict(z=((0, 1024),), xbc=((1024, 2560),), gate=((2576, 3600),), qkv=((3600, 6672),),
                  sm=((2560, 2576), (6672, 6688)))


def _w_rows(w_ref, name, width):
    pieces = [w_ref[a:b, :] for a, b in GROUP_ROWS[name]]
    n = sum(b - a for a, b in GROUP_ROWS[name])
    if n < width:
        pieces.append(jnp.zeros((width - n, D_MODEL), w_ref.dtype))
    return pieces[0] if len(pieces) == 1 else jnp.concatenate(pieces, axis=0)


def inproj_fwd(x, norm_w, w_perm, gathered):
    t = x.shape[0]
    tm = min(512, t)
    steps = t // tm
    kinds = ["gather"] * len(gathered)

    def body(x_ref, nw_ref, w_ref, u_ref, z_ref, xbc_ref, gate_ref, qkv_ref, sm_ref):
        xf = x_ref[...]
        rstd = lax.rsqrt(jnp.mean(xf * xf, axis=-1, keepdims=True) + EPS)
        u = (xf * rstd * nw_ref[...]).astype(_MM)
        u_ref[...] = u
        for (name, c0, c1), o_ref in zip(GROUPS, (z_ref, xbc_ref, gate_ref, qkv_ref, sm_ref)):
            o_ref[...] = lax.dot_general(u, _w_rows(w_ref, name, c1 - c0), (((1,), (1,)), ((), ())),
                                         preferred_element_type=jnp.float32)

    outs = [jax.ShapeDtypeStruct((t, D_MODEL), _MM)] + [jax.ShapeDtypeStruct((t, c1 - c0), jnp.float32)
                                                        for _, c0, c1 in GROUPS]
    hosted = _hosting(body, 3, 6, 0, kinds, lambda: pl.program_id(0) == 0, lambda: pl.program_id(0) == steps - 1)
    return _pc_comm(
        hosted, name="inproj_fwd", grid=(steps,),
        in_specs=[pl.BlockSpec((tm, D_MODEL), lambda i: (i, 0)),
                  pl.BlockSpec((1, D_MODEL), lambda i: (0, 0)),
                  pl.BlockSpec((IN_DIM, D_MODEL), lambda i: (0, 0), pipeline_mode=pl.Buffered(1))] +
                 [ANY] * len(gathered),
        out_specs=[pl.BlockSpec((tm, D_MODEL), lambda i: (i, 0))] +
                  [pl.BlockSpec((tm, c1 - c0), lambda i: (i, 0)) for _, c0, c1 in GROUPS] + [ANY] * len(gathered),
        out_shape=outs + [_exchange_out_shape("gather", a) for a in gathered],
        scratch_shapes=_exchange_sems(len(gathered)), compiler_params=_cparams(("arbitrary",)),
    )(x, norm_w, w_perm, *gathered)


SUB_FWD = 4
SUB_BWD = 2


def _halo_spec(width, idx_fn):
    return pl.BlockSpec((8, width), lambda i: (jnp.maximum(idx_fn(i) * (SUB_FWD * CHUNK // 8) - 1, 0), 0))


def _when_first(shared, fn):
    if shared["first"] is not False:
        pl.when(shared["first"])(fn)


def _full(shape):
    nd = len(shape)
    return pl.BlockSpec(shape, lambda i: (0,) * nd)


def _ssd_split(pre_fn, z_ref, sm_ref):
    xs_pre = [pre_fn(128 * j, 128 * j + 128) for j in range(8)]
    b_pre = [pre_fn(1024 + 128 * g, 1152 + 128 * g) for g in range(2)]
    c_pre = [pre_fn(1280 + 128 * g, 1408 + 128 * g) for g in range(2)]
    z = [z_ref[:, 128 * j:128 * j + 128] for j in range(8)]
    return xs_pre, b_pre, c_pre, z, sm_ref[...]


def ssd_fwd(z, xbc, sm, conv_w, conv_b, dtb, alog, dpar, nw, cs):
    t = z.shape[0]
    nc = t // CHUNK

    def body(shared, z_ref, xbc_ref, halo_ref, sm_ref, cw_ref, cb_ref, dtb_ref, alog_ref, dpar_ref, nw_ref,
             tri_ref, i2_ref, mask2_ref, lo_ref, y_ref, hs_ref, pre_ref, pbuf, ht_scr):
        def init():
            ht_scr[...] = jnp.zeros_like(ht_scr)

        _when_first(shared, init)
        pbuf[0:8, :] = jnp.where(shared["first"], 0.0, halo_ref[...])
        pbuf[8:72, :] = xbc_ref[...]

        def pre_fn(c0, c1):
            pre = _conv_fwd(pbuf, cw_ref, c0, c1) + cb_ref[:, c0:c1]
            pre_ref[:, c0:c1] = pre
            return pre

        xs_pre, b_pre, c_pre, zz, smv = _ssd_split(pre_fn, z_ref, sm_ref)
        ht = [ht_scr[:, 128 * j:128 * j + 128] for j in range(8)]
        hs_ref[0] = ht_scr[...]
        nwl = [nw_ref[:, 128 * j:128 * j + 128] for j in range(8)]
        outs, ht_next = _ssd_chunk(xs_pre, b_pre, c_pre, zz, smv, ht, dtb_ref[...], alog_ref[...], dpar_ref[...],
                                   nwl, tri_ref[...], i2_ref[...], mask2_ref[...], lo_ref[...])
        for j in range(8):
            y_ref[:, 128 * j:128 * j + 128] = outs[j].astype(y_ref.dtype)
            ht_scr[:, 128 * j:128 * j + 128] = ht_next[j]

    blk = lambda w: pl.BlockSpec((SUB_FWD * CHUNK, w), lambda i: (i, 0))
    return dict(
        body=body,
        in_kinds=["rows", "rows", ("halo", 1), "rows"] + ["full"] * 10, out_kinds=["rows", "state", "rows"],
        in_specs=[blk(1024), blk(1536), _halo_spec(1536, lambda i: i), blk(128),
                  _full((CONV_K, 1536)), _full((1, 1536)), _full((1, 128)), _full((1, 128)), _full((1, 128)),
                  _full((1, 1024)), _full((64, 64)), _full((64, 128)), _full((64, 128)),
                  _full((1, 128))],
        out_specs=[blk(1024), pl.BlockSpec((SUB_FWD, 128, 1024), lambda i: (i, 0, 0)), blk(1536)],
        out_shape=[jax.ShapeDtypeStruct((t, 1024), _MM), jax.ShapeDtypeStruct((nc, 128, 1024), jnp.float32),
                   jax.ShapeDtypeStruct((t, 1536), jnp.float32)],
        scratch=[pltpu.VMEM((72, 1536), jnp.float32), pltpu.VMEM((128, 1024), jnp.float32)],
        args=[z, xbc, xbc, sm, conv_w, conv_b, dtb, alog, dpar, nw, cs["tri"], cs["i2"], cs["mask2"], cs["lo"]])


def _conv_bwd(dpre_list, col_ranges, dbuf, carry, x_ref, cw_ref, dx_ref, dcw_ref, dcb_ref, first):
    for dpre, (c0, c1) in zip(dpre_list, col_ranges):
        dbuf[0:64, c0:c1] = dpre
    dbuf[64:72, :] = jnp.where(first, 0.0, carry[...])
    carry[...] = dbuf[0:8, :]
    for (c0, c1) in col_ranges:
        xin = x_ref[:, c0:c1]
        blk = dbuf[:, c0:c1]
        acc = None
        for j in range(CONV_K):
            sh = blk[0:64] if j == CONV_K - 1 else pltpu.roll(blk, 72 - (CONV_K - 1 - j), axis=0)[0:64]
            term = cw_ref[j:j + 1, c0:c1] * sh
            acc = term if acc is None else acc + term
            dcw_ref[j:j + 1, c0:c1] += jnp.sum(xin * sh, axis=0, keepdims=True)
        dx_ref[:, c0:c1] = acc.astype(dx_ref.dtype)
        if dcb_ref is not None:
            dcb_ref[0:1, c0:c1] += jnp.sum(dbuf[0:64, c0:c1], axis=0, keepdims=True)


def ssd_bwd(z, xbc, pre, sm, hs, dy, conv_w, dtb, alog, dpar, nw, cs):
    t = z.shape[0]
    nc = t // CHUNK

    def body(shared, z_ref, xbc_ref, pre_ref, sm_ref, hs_ref, dy_ref, cw_ref, dtb_ref, alog_ref, dpar_ref, nw_ref,
             tri_ref, i2_ref, mask2_ref, lo_ref,
             dz_ref, dxbc_ref, dcw_ref, dcb_ref, ddtb_ref, dalog_ref, ddpar_ref, dnw_ref,
             dbuf, carry, dht_scr):
        def init():
            dht_scr[...] = jnp.zeros_like(dht_scr)
            dcw_ref[...] = jnp.zeros_like(dcw_ref)
            dcb_ref[...] = jnp.zeros_like(dcb_ref)
            ddtb_ref[...] = jnp.zeros_like(ddtb_ref)
            dalog_ref[...] = jnp.zeros_like(dalog_ref)
            ddpar_ref[...] = jnp.zeros_like(ddpar_ref)
            dnw_ref[...] = jnp.zeros_like(dnw_ref)

        _when_first(shared, init)
        pre_fn = lambda c0, c1: pre_ref[:, c0:c1]
        xs_pre, b_pre, c_pre, zz, smv = _ssd_split(pre_fn, z_ref, sm_ref)
        ht = [hs_ref[0, :, 128 * j:128 * j + 128] for j in range(8)]
        nwl = [nw_ref[:, 128 * j:128 * j + 128] for j in range(8)]
        consts = (tri_ref[...], i2_ref[...], mask2_ref[...], lo_ref[...])

        def f(xs_pre, b_pre, c_pre, zz, smv, ht, dtb, alog, dpar, nwl):
            return _ssd_chunk(xs_pre, b_pre, c_pre, zz, smv, ht, dtb, alog, dpar, nwl, *consts)

        _, vjp = jax.vjp(f, xs_pre, b_pre, c_pre, zz, smv, ht, dtb_ref[...], alog_ref[...], dpar_ref[...], nwl)
        dys = [dy_ref[:, 128 * j:128 * j + 128] for j in range(8)]
        dhts = [dht_scr[:, 128 * j:128 * j + 128] for j in range(8)]
        dxs, db, dc, dzz, dsm, dht, ddtb, dalog, ddpar, dnwl = vjp((dys, dhts))
        for j in range(8):
            dz_ref[:, 128 * j:128 * j + 128] = dzz[j].astype(dz_ref.dtype)
            dht_scr[:, 128 * j:128 * j + 128] = dht[j]
            dnw_ref[0:1, 128 * j:128 * j + 128] += dnwl[j]
        shared["dsm_ssd"] = dsm
        ddtb_ref[0:1, :] += ddtb
        dalog_ref[0:1, :] += dalog
        ddpar_ref[0:1, :] += ddpar
        ranges = ([(128 * j, 128 * j + 128) for j in range(8)] + [(1024 + 128 * g, 1152 + 128 * g) for g in range(2)]
                  + [(1280 + 128 * g, 1408 + 128 * g) for g in range(2)])
        _conv_bwd(dxs + db + dc, ranges, dbuf, carry, xbc_ref, cw_ref, dxbc_ref, dcw_ref, dcb_ref, shared["first"])

    ns = nc // SUB_BWD
    rblk = lambda w: pl.BlockSpec((SUB_BWD * CHUNK, w), lambda i: (ns - 1 - i, 0))
    acc = lambda w: pl.BlockSpec((8, w), lambda i: (0, 0))
    f32 = jnp.float32
    return dict(
        body=body,
        in_kinds=["rows"] * 4 + ["state", "rows"] + ["full"] * 9, out_kinds=["rows", "rows"] + ["full"] * 6,
        in_specs=[rblk(1024), rblk(1536), rblk(1536), rblk(128),
                  pl.BlockSpec((SUB_BWD, 128, 1024), lambda i: (ns - 1 - i, 0, 0)), rblk(1024),
                  _full((CONV_K, 1536)), _full((1, 128)), _full((1, 128)), _full((1, 128)),
                  _full((1, 1024)), _full((64, 64)), _full((64, 128)), _full((64, 128)),
                  _full((1, 128))],
        out_specs=[rblk(1024), rblk(1536), acc(1536), acc(1536), acc(128), acc(128), acc(128), acc(1024)],
        out_shape=[jax.ShapeDtypeStruct((t, 1024), f32), jax.ShapeDtypeStruct((t, 1536), f32),
                   jax.ShapeDtypeStruct((8, 1536), f32),
                   jax.ShapeDtypeStruct((8, 1536), f32), jax.ShapeDtypeStruct((8, 128), f32),
                   jax.ShapeDtypeStruct((8, 128), f32), jax.ShapeDtypeStruct((8, 128), f32),
                   jax.ShapeDtypeStruct((8, 1024), f32)],
        scratch=[pltpu.VMEM((72, 1536), f32), pltpu.VMEM((8, 1536), f32), pltpu.VMEM((128, 1024), f32)],
        args=[z, xbc, pre, sm, hs, dy, conv_w, dtb, alog, dpar, nw, cs["tri"], cs["i2"], cs["mask2"], cs["lo"]])


def _gdn_split(pre_fn, gate_ref):
    def heads(base):
        return jnp.stack([pre_fn(base + 128 * h, base + 128 * h + 128) for h in range(GDN_HEADS)])
    gate = jnp.stack([gate_ref[:, 128 * h:128 * h + 128] for h in range(GDN_HEADS)])
    return heads(0), heads(1024), heads(2048), gate


def gdn_fwd(gate, qkv, sm, conv_w, dtb, alog, nw, cs):
    t = gate.shape[0]
    nc = t // CHUNK

    def body(shared, gate_ref, qkv_ref, halo_ref, sm_ref, cw_ref, dtb_ref, alog_ref, nw_ref,
             tri_ref, i64_ref, strict_ref, o_ref, ss_ref, ts_ref, pre_ref, pbuf, s_scr):
        def init():
            s_scr[...] = jnp.zeros_like(s_scr)

        _when_first(shared, init)
        pbuf[0:8, :] = jnp.where(shared["first"], 0.0, halo_ref[...])
        pbuf[8:72, :] = qkv_ref[...]

        def pre_fn(c0, c1):
            pre = _conv_fwd(pbuf, cw_ref, c0, c1)
            pre_ref[:, c0:c1] = pre
            return pre

        q_pre, k_pre, v_pre, g3 = _gdn_split(pre_fn, gate_ref)
        s = s_scr[...]
        ss_ref[0] = s
        out, s_next, tinv = _gdn_chunk(q_pre, k_pre, v_pre, g3, sm_ref[...], s, dtb_ref[...], alog_ref[...],
                                       nw_ref[...], tri_ref[...], i64_ref[...], strict_ref[...])
        ts_ref[0] = tinv
        s_scr[...] = s_next
        for h in range(GDN_HEADS):
            o_ref[:, 128 * h:128 * h + 128] = out[h].astype(o_ref.dtype)

    blk = lambda w: pl.BlockSpec((SUB_FWD * CHUNK, w), lambda i: (i, 0))
    return dict(
        body=body,
        in_kinds=["rows", "rows", ("halo", 1), "rows"] + ["full"] * 7, out_kinds=["rows", "state", "state", "rows"],
        in_specs=[blk(1024), blk(3072), _halo_spec(3072, lambda i: i), blk(128),
                  _full((CONV_K, 3072)), _full((1, 128)), _full((1, 128)), _full((1, 128)),
                  _full((64, 64)), _full((64, 64)), _full((64, 64))],
        out_specs=[blk(1024), pl.BlockSpec((SUB_FWD, 8, 128, 128), lambda i: (i, 0, 0, 0)),
                   pl.BlockSpec((SUB_FWD, 8, CHUNK, CHUNK), lambda i: (i, 0, 0, 0)), blk(3072)],
        out_shape=[jax.ShapeDtypeStruct((t, 1024), _MM), jax.ShapeDtypeStruct((nc, 8, 128, 128), jnp.float32),
                   jax.ShapeDtypeStruct((nc, 8, CHUNK, CHUNK), jnp.float32),
                   jax.ShapeDtypeStruct((t, 3072), jnp.float32)],
        scratch=[pltpu.VMEM((72, 3072), jnp.float32), pltpu.VMEM((8, 128, 128), jnp.float32)],
        args=[gate, qkv, qkv, sm, conv_w, dtb, alog, nw, cs["tri"], cs["i64"], cs["strict"]])


def gdn_bwd(gate, qkv, pre, sm, ss, ts, do, conv_w, dtb, alog, nw, cs):
    t = gate.shape[0]
    nc = t // CHUNK

    def body(shared, gate_ref, qkv_ref, pre_ref, sm_ref, ss_ref, ts_ref, do_ref, cw_ref, dtb_ref, alog_ref,
             nw_ref, tri_ref, i64_ref, strict_ref,
             dgate_ref, dqkv_ref, dsm_ref, dcw_ref, ddtb_ref, dalog_ref, dnw_ref,
             dbuf, carry, ds_scr):
        def init():
            ds_scr[...] = jnp.zeros_like(ds_scr)
            dcw_ref[...] = jnp.zeros_like(dcw_ref)
            ddtb_ref[...] = jnp.zeros_like(ddtb_ref)
            dalog_ref[...] = jnp.zeros_like(dalog_ref)
            dnw_ref[...] = jnp.zeros_like(dnw_ref)

        _when_first(shared, init)

        q_pre, k_pre, v_pre, g3 = _gdn_split(lambda c0, c1: pre_ref[:, c0:c1], gate_ref)
        consts = (tri_ref[...], i64_ref[...], strict_ref[...], ts_ref[0])

        def f(q_pre, k_pre, v_pre, g3, smv, s, dtb, alog, nwv):
            return _gdn_chunk(q_pre, k_pre, v_pre, g3, smv, s, dtb, alog, nwv, *consts)[:2]

        _, vjp = jax.vjp(f, q_pre, k_pre, v_pre, g3, sm_ref[...], ss_ref[0], dtb_ref[...], alog_ref[...], nw_ref[...])
        do3 = jnp.stack([do_ref[:, 128 * h:128 * h + 128] for h in range(GDN_HEADS)])
        dq, dk, dv, dg3, dsm, ds, ddtb, dalog, dnw = vjp((do3, ds_scr[...]))
        ds_scr[...] = ds
        for h in range(GDN_HEADS):
            dgate_ref[:, 128 * h:128 * h + 128] = dg3[h].astype(dgate_ref.dtype)
        dsm_ref[...] = (dsm + shared["dsm_ssd"]).astype(dsm_ref.dtype)
        ddtb_ref[0:1, :] += ddtb
        dalog_ref[0:1, :] += dalog
        dnw_ref[0:1, :] += dnw
        ranges = [(base + 128 * h, base + 128 * h + 128) for base in (0, 1024, 2048) for h in range(GDN_HEADS)]
        dlist = [d[h] for d in (dq, dk, dv) for h in range(GDN_HEADS)]
        _conv_bwd(dlist, ranges, dbuf, carry, qkv_ref, cw_ref, dqkv_ref, dcw_ref, None, shared["first"])

    ns = nc // SUB_BWD
    rblk = lambda w: pl.BlockSpec((SUB_BWD * CHUNK, w), lambda i: (ns - 1 - i, 0))
    acc = lambda w: pl.BlockSpec((8, w), lambda i: (0, 0))
    f32 = jnp.float32
    return dict(
        body=body,
        in_kinds=["rows"] * 4 + ["state", "state", "rows"] + ["full"] * 7, out_kinds=["rows"] * 3 + ["full"] * 4,
        in_specs=[rblk(1024), rblk(3072), rblk(3072), rblk(128),
                  pl.BlockSpec((SUB_BWD, 8, 128, 128), lambda i: (ns - 1 - i, 0, 0, 0)),
                  pl.BlockSpec((SUB_BWD, 8, CHUNK, CHUNK), lambda i: (ns - 1 - i, 0, 0, 0)), rblk(1024),
                  _full((CONV_K, 3072)), _full((1, 128)), _full((1, 128)), _full((1, 128)),
                  _full((64, 64)), _full((64, 64)), _full((64, 64))],
        out_specs=[rblk(1024), rblk(3072), rblk(128), acc(3072), acc(128), acc(128), acc(128)],
        out_shape=[jax.ShapeDtypeStruct((t, 1024), f32), jax.ShapeDtypeStruct((t, 3072), f32),
                   jax.ShapeDtypeStruct((t, 128), f32), jax.ShapeDtypeStruct((8, 3072), f32),
                   jax.ShapeDtypeStruct((8, 128), f32), jax.ShapeDtypeStruct((8, 128), f32),
                   jax.ShapeDtypeStruct((8, 128), f32)],
        scratch=[pltpu.VMEM((72, 3072), f32), pltpu.VMEM((8, 3072), f32), pltpu.VMEM((8, 128, 128), f32)],
        args=[gate, qkv, pre, sm, ss, ts, do, conv_w, dtb, alog, nw, cs["tri"], cs["i64"], cs["strict"]])


def _chunk_call(parts, name, nc, reverse):
    n_in = [len(p["args"]) for p in parts]
    n_out = [len(p["out_shape"]) for p in parts]
    n_scr = [len(p["scratch"]) for p in parts]
    sub = SUB_BWD if reverse else SUB_FWD
    order = list(range(sub))[::-1] if reverse else list(range(sub))

    def view(ref, kind, s, refs):
        if kind == "rows":
            return ref.at[pl.ds(CHUNK * s, CHUNK)]
        if kind == "state":
            return ref.at[pl.ds(s, 1)]
        if kind == "full":
            return ref
        src = refs[kind[1]]
        return ref if s == 0 else src.at[pl.ds(CHUNK * s - 8, 8)]

    def body(*refs):
        ins, outs, scr = refs[:sum(n_in)], refs[sum(n_in):sum(n_in) + sum(n_out)], refs[sum(n_in) + sum(n_out):]
        for s in order:
            shared = {"first": (pl.program_id(0) == 0) if s == order[0] else False}
            for k, p in enumerate(parts):
                i0, o0, s0 = sum(n_in[:k]), sum(n_out[:k]), sum(n_scr[:k])
                p_ins = ins[i0:i0 + n_in[k]]
                p["body"](shared,
                          *[view(r, kd, s, p_ins) for r, kd in zip(p_ins, p["in_kinds"])],
                          *[view(r, kd, s, None) for r, kd in zip(outs[o0:o0 + n_out[k]], p["out_kinds"])],
                          *scr[s0:s0 + n_scr[k]])

    cat = lambda key: [v for p in parts for v in p[key]]
    return _pc(body, name=name, grid=(nc // sub,), in_specs=cat("in_specs"), out_specs=cat("out_specs"),
               out_shape=cat("out_shape"), scratch_shapes=cat("scratch"),
               compiler_params=_cparams(("arbitrary",)))(*cat("args"))


def out_fwd_bwd(x, tgt, y_ssd, y_gdn, w_out, fnw):
    t = x.shape[0]
    tm = min(512, t)
    f32 = jnp.float32

    def body(x_ref, tgt_ref, ys_ref, yg_ref, w_ref, fnw_ref,
             dout_ref, dys_ref, dyg_ref, gw_ref, gfnw_ref, loss_ref, gw_acc):
        i = pl.program_id(0)

        @pl.when(i == 0)
        def _():
            gw_acc[...] = jnp.zeros_like(gw_acc)
            gfnw_ref[...] = jnp.zeros_like(gfnw_ref)
            loss_ref[...] = jnp.zeros_like(loss_ref)

        ys = ys_ref[...]
        yg = yg_ref[...]
        out = x_ref[...] + jnp.dot(ys, w_ref[0:1024, :], preferred_element_type=f32) \
            + jnp.dot(yg, w_ref[1024:2048, :], preferred_element_type=f32)
        rstd = lax.rsqrt(jnp.mean(out * out, axis=-1, keepdims=True) + EPS)
        yhat = out * rstd
        fw = fnw_ref[...]
        e = yhat * fw - tgt_ref[...]
        loss_ref[...] += 0.5 * jnp.sum(jnp.sum(e * e, axis=-1, keepdims=True) * (1.0 / D_MODEL), axis=0, keepdims=True)
        dyf = e * (1.0 / D_MODEL)
        gfnw_ref[0:1, :] += jnp.sum(dyf * yhat, axis=0, keepdims=True)
        dyhat = dyf * fw
        dout = rstd * (dyhat - yhat * jnp.mean(dyhat * yhat, axis=-1, keepdims=True))
        dout_ref[...] = dout
        db = dout.astype(_MM)
        dys_ref[...] = lax.dot_general(db, w_ref[0:1024, :], (((1,), (1,)), ((), ())), preferred_element_type=f32)
        dyg_ref[...] = lax.dot_general(db, w_ref[1024:2048, :], (((1,), (1,)), ((), ())), preferred_element_type=f32)
        gw_acc[0:1024, :] += lax.dot_general(ys, db, (((0,), (0,)), ((), ())), preferred_element_type=f32)
        gw_acc[1024:2048, :] += lax.dot_general(yg, db, (((0,), (0,)), ((), ())), preferred_element_type=f32)

        @pl.when(i == steps - 1)
        def _():
            gw_ref[...] = gw_acc[...].astype(gw_ref.dtype)

    steps = t // tm
    blk = pl.BlockSpec((tm, D_MODEL), lambda i: (i, 0))
    return _pc(
        body, name="out_fwd_bwd", grid=(steps,),
        in_specs=[blk, blk, blk, blk, _full((MIX_WIDTH, D_MODEL)), _full((1, D_MODEL))],
        out_specs=[blk, blk, blk, _full((MIX_WIDTH, D_MODEL)), _full((8, D_MODEL)), _full((1, 128))],
        out_shape=[jax.ShapeDtypeStruct((t, D_MODEL), f32)] * 3 +
                  [jax.ShapeDtypeStruct((MIX_WIDTH, D_MODEL), _MM), jax.ShapeDtypeStruct((8, D_MODEL), f32),
                   jax.ShapeDtypeStruct((1, 128), f32)],
        scratch_shapes=[pltpu.VMEM((MIX_WIDTH, D_MODEL), f32)],
        compiler_params=_cparams(("arbitrary",)),
    )(x, tgt, y_ssd, y_gdn, w_out, fnw)


def inproj_bwd_dx(x, dout, norm_w, w_perm, dgroups, scattered):
    t = x.shape[0]
    tm = min(256, t)
    f32 = jnp.float32

    def body(x_ref, dout_ref, nw_ref, w_ref, dz_ref, dxbc_ref, dgate_ref, dqkv_ref, dsm_ref, dx_ref, gnw_ref):
        i = pl.program_id(0)

        @pl.when(i == 0)
        def _():
            gnw_ref[...] = jnp.zeros_like(gnw_ref)

        du = None
        for (name, c0, c1), d_ref in zip(GROUPS, (dz_ref, dxbc_ref, dgate_ref, dqkv_ref, dsm_ref)):
            term = jnp.dot(d_ref[...].astype(_MM), _w_rows(w_ref, name, c1 - c0), preferred_element_type=f32)
            du = term if du is None else du + term
        xf = x_ref[...]
        rstd = lax.rsqrt(jnp.mean(xf * xf, axis=-1, keepdims=True) + EPS)
        xhat = xf * rstd
        gnw_ref[0:1, :] += jnp.sum(du * xhat, axis=0, keepdims=True)
        dxh = du * nw_ref[...]
        dx_ref[...] = dout_ref[...] + rstd * (dxh - xhat * jnp.mean(dxh * xhat, axis=-1, keepdims=True))

    blk = lambda w: pl.BlockSpec((tm, w), lambda i: (i, 0))
    steps = t // tm
    kinds = ["scatter"] * len(scattered)
    hosted = _hosting(body, 9, 2, 0, kinds, lambda: pl.program_id(0) == 0, lambda: pl.program_id(0) == steps - 1)
    return _pc_comm(
        hosted, name="inproj_bwd_dx", grid=(steps,),
        in_specs=[blk(D_MODEL), blk(D_MODEL), _full((1, D_MODEL)), _full((IN_DIM, D_MODEL))] +
                 [blk(c1 - c0) for _, c0, c1 in GROUPS] + [ANY] * len(scattered),
        out_specs=[blk(D_MODEL), _full((8, D_MODEL))] + [ANY] * len(scattered),
        out_shape=[jax.ShapeDtypeStruct((t, D_MODEL), f32), jax.ShapeDtypeStruct((8, D_MODEL), f32)] +
                  [_exchange_out_shape("scatter", a) for a in scattered],
        scratch_shapes=_exchange_sems(len(scattered)), compiler_params=_cparams(("arbitrary",)),
    )(x, dout, norm_w, w_perm, *dgroups, *scattered)


def grad_w_group(u, dg, name, scattered=()):
    t, n = dg.shape
    tn = 512 if n % 512 == 0 else n
    tm = 4096 if t % 4096 == 0 else t
    nj, nk = n // tn, t // tm
    f32 = jnp.float32

    def body(u_ref, d_ref, o_ref, acc):
        k = pl.program_id(1)

        @pl.when(k == 0)
        def _():
            acc[...] = jnp.zeros_like(acc)

        acc[...] += lax.dot_general(d_ref[...].astype(_MM), u_ref[...], (((0,), (0,)), ((), ())),
                                    preferred_element_type=f32)

        @pl.when(k == nk - 1)
        def _():
            o_ref[...] = acc[...].astype(o_ref.dtype)

    ne = len(scattered)
    hosted = _hosting(body, 2, 1, 1, ["scatter"] * ne,
                      lambda: (pl.program_id(0) == 0) & (pl.program_id(1) == 0),
                      lambda: (pl.program_id(0) == nj - 1) & (pl.program_id(1) == nk - 1))
    res = (_pc_comm if ne else _pc)(
        hosted, name=name, grid=(nj, nk),
        in_specs=[pl.BlockSpec((tm, D_MODEL), lambda j, k: (k, 0)),
                  pl.BlockSpec((tm, tn), lambda j, k: (k, j))] + [ANY] * ne,
        out_specs=[pl.BlockSpec((tn, D_MODEL), lambda j, k: (j, 0))] + [ANY] * ne,
        out_shape=[jax.ShapeDtypeStruct((n, D_MODEL), _MM)] + [_exchange_out_shape("scatter", a) for a in scattered],
        scratch_shapes=[pltpu.VMEM((tn, D_MODEL), f32)] + _exchange_sems(ne),
        compiler_params=_cparams(("arbitrary", "arbitrary")),
    )(u, dg, *scattered)
    return res if ne else res[0]


def _pad_lanes(v, off):
    n = v.shape[-1]
    return jnp.pad(v.reshape(1, n).astype(jnp.float32), ((0, 0), (off, 128 - off - n)))


REF_ROWS = dict(z=(0, 1024), xbc=(1024, 2560), dt=(2560, 2576), gate=(2576, 3600), qkv=(3600, 6672), ab=(6672, 6688))


def unperm_w_in(gz, gxbc, ggate, gqkv, gsm):
    src = dict(z=gz, xbc=gxbc, dt=gsm[0:16], gate=ggate, qkv=gqkv, ab=gsm[16:32])
    slabs = []
    for k in range(N_DEV):
        a, b = k * W_IN_SHARD, (k + 1) * W_IN_SHARD
        parts = []
        for name, (s, e) in REF_ROWS.items():
            lo, hi = max(a, s), min(b, e)
            if lo < hi:
                parts.append(src[name][lo - s:hi - s])
        slabs.append(jnp.concatenate(parts, axis=0))
    return jnp.stack(slabs)


def all_gather(arrs, name):
    n = len(arrs)

    def body(*refs):
        ins, outs = refs[:n], refs[n:2 * n]
        send_sems, recv_sems, local_sems = refs[2 * n:]
        x, y, c, me = _me()
        sibling = (x, y, 1 - c)
        chips = [(1 - x, y), (x, 1 - y), (1 - x, 1 - y)]

        def idx(px, py, pc):
            return 4 * px + 2 * py + pc

        def copy(a, k, block, to, src=None):
            slot = outs[a].at[idx(*block)]
            return pltpu.make_async_remote_copy(src_ref=slot if src is None else src, dst_ref=slot,
                                                send_sem=send_sems.at[a, k], recv_sem=recv_sems.at[a, k],
                                                device_id=to, device_id_type=MESH)

        local = [pltpu.make_async_copy(ins[a], outs[a].at[me], local_sems.at[a]) for a in range(n)]
        for cp in local:
            cp.start()
        started = []
        for a in range(n):
            first = [copy(a, 0, (x, y, c), sibling, src=ins[a])]
            first += [copy(a, 1 + j, (x, y, c), (*chip, c), src=ins[a]) for j, chip in enumerate(chips)]
            for cp in first:
                cp.start()
            started += first
        for a in range(n):
            for j, chip in enumerate(chips):
                copy(a, 1 + j, (*chip, c), (x, y, c)).wait_recv()
                fwd = copy(a, 4 + j, (*chip, c), sibling)
                fwd.start()
                started.append(fwd)
        for a in range(n):
            copy(a, 0, sibling, (x, y, c)).wait_recv()
            for j, chip in enumerate(chips):
                copy(a, 4 + j, (*chip, 1 - c), (x, y, c)).wait_recv()
        for cp in started:
            cp.wait_send()
        for cp in local:
            cp.wait()

    return _pc_comm(
        body, name=name, in_specs=[ANY] * n, out_specs=[ANY] * n,
        out_shape=[jax.ShapeDtypeStruct((N_DEV,) + a.shape, a.dtype) for a in arrs],
        scratch_shapes=[pltpu.SemaphoreType.DMA((n, 7)), pltpu.SemaphoreType.DMA((n, 7)),
                        pltpu.SemaphoreType.DMA((n,))],
    )(*arrs)


def adamw_sum(recv, w, m, v, rows, name, cols=None):
    r, ccols = w.shape
    f32 = jnp.float32
    c1 = 1.0 / (1.0 - ADAM_B1 ** ADAM_STEP)
    c2 = 1.0 / (1.0 - ADAM_B2 ** ADAM_STEP)

    def body(recv_ref, w_ref, m_ref, v_ref, g_ref, d_ref, mo_ref, vo_ref):
        g = recv_ref[0].astype(f32)
        for k in range(1, N_DEV):
            g = g + recv_ref[k].astype(f32)
        mn = ADAM_B1 * m_ref[...] + (1.0 - ADAM_B1) * g
        vn = ADAM_B2 * v_ref[...] + (1.0 - ADAM_B2) * (g * g)
        g_ref[...] = g
        mo_ref[...] = mn
        vo_ref[...] = vn
        d_ref[...] = -ADAM_LR * ((mn * c1) / (jnp.sqrt(vn * c2) + ADAM_EPS) + ADAM_WD * w_ref[...])

    if cols is None:
        blk = pl.BlockSpec((rows, ccols), lambda i: (i, 0))
        rblk, steps = pl.BlockSpec((N_DEV, rows, ccols), lambda i: (0, i, 0)), r // rows
    else:
        blk = pl.BlockSpec((r, cols), lambda i: (0, i))
        rblk, steps = pl.BlockSpec((N_DEV, r, cols), lambda i: (0, 0, i)), ccols // cols
    return _pc(
        body, name=name, grid=(steps,),
        in_specs=[rblk, blk, blk, blk],
        out_specs=[blk] * 4, out_shape=[jax.ShapeDtypeStruct((r, ccols), f32)] * 4,
        compiler_params=_cparams(("arbitrary",)),
    )(recv, w, m, v)


SMALL = (("norm_w", 1, 1024, 0), ("ssd_conv_b", 1, 1536, 0), ("ssd_dt_bias", 1, 16, 0), ("ssd_a_log", 1, 16, 0),
         ("ssd_d", 1, 16, 0), ("ssd_norm_w", 1, 1024, 0), ("gdn_dt_bias", 1, 8, 16), ("gdn_a_log", 1, 8, 16),
         ("gdn_norm_w", 1, 128, 0), ("final_norm_w", 1, 1024, 0),
         ("ssd_conv_w", CONV_K, SSD_CONV_DIM // N_DEV, 0), ("gdn_conv_w", CONV_K, GDN_CONV_DIM // N_DEV, 0))


def _small_layout():
    out, off = [], 0
    for name, rows, n, lane0 in SMALL + (("loss", 1, 128, 0),):
        stride = -(-(lane0 + n) // 128) * 128
        out.append((name, rows, n, lane0, stride, off))
        off += rows * stride
    return out, off


def scatter_small(accs):
    layout, total = _small_layout()
    f32 = jnp.float32

    def body(*refs):
        acc_refs, out_ref, slabs = refs[:len(layout)], refs[len(layout)], refs[len(layout) + 1]
        sems = refs[len(layout) + 2:]
        slabs[...] = jnp.zeros_like(slabs)
        for (name, rows, n, lane0, stride, off), acc in zip(layout, acc_refs):
            for k in range(N_DEV):
                if rows == 1:
                    slabs[k, :, off:off + stride] = acc[0:1, 0:stride]
                else:
                    for j in range(rows):
                        slabs[k, :, off + stride * j:off + stride * j + n] = acc[j:j + 1, n * k:n * k + n]
        start, wait = _exchange_ops("scatter", slabs, out_ref, *sems)
        start()
        wait()

    return _pc_comm(
        body, name="scatter_small_grads", out_specs=ANY, out_shape=jax.ShapeDtypeStruct((N_DEV, 1, total), f32),
        scratch_shapes=[pltpu.VMEM((N_DEV, 1, total), f32)] + _exchange_sems(1),
    )(*accs)


def adamw_small(recv, w, m, v):
    layout, total = _small_layout()
    loss_off = layout[-1][5]
    layout = layout[:-1]
    f32 = jnp.float32
    c1 = 1.0 / (1.0 - ADAM_B1 ** ADAM_STEP)
    c2 = 1.0 / (1.0 - ADAM_B2 ** ADAM_STEP)
    np_ = len(layout)

    def body(*refs):
        recv_ref = refs[0]
        w_refs, m_refs, v_refs = refs[1:1 + np_], refs[1 + np_:1 + 2 * np_], refs[1 + 2 * np_:1 + 3 * np_]
        o_refs = refs[1 + 3 * np_:]
        g_all = recv_ref[0]
        for k in range(1, N_DEV):
            g_all = g_all + recv_ref[k]
        o_refs[4 * np_][...] = g_all[:, loss_off:loss_off + 128]

        def update(g, wv, mv, vv):
            mn = ADAM_B1 * mv + (1.0 - ADAM_B1) * g
            vn = ADAM_B2 * vv + (1.0 - ADAM_B2) * (g * g)
            return g, -ADAM_LR * ((mn * c1) / (jnp.sqrt(vn * c2) + ADAM_EPS) + ADAM_WD * wv), mn, vn

        for p, (name, rows, n, lane0, stride, off) in enumerate(layout):
            outs = o_refs[4 * p:4 * p + 4]
            if rows == 1:
                res = update(g_all[:, off + lane0:off + lane0 + n], w_refs[p][...], m_refs[p][...], v_refs[p][...])
                for o, r in zip(outs, res):
                    o[...] = r
            else:
                for j in range(rows):
                    res = update(g_all[:, off + stride * j:off + stride * j + n], w_refs[p][0, j:j + 1, :],
                                 m_refs[p][0, j:j + 1, :], v_refs[p][0, j:j + 1, :])
                    for o, r in zip(outs, res):
                        o[0, j:j + 1, :] = r

    names = [e[0] for e in layout]
    ins = [recv] + [d[nm] for d in (w, m, v) for nm in names]
    out_shape = [jax.ShapeDtypeStruct(w[nm].shape, f32) for nm in names for _ in range(4)]
    out_shape.append(jax.ShapeDtypeStruct((1, 128), f32))
    res = _pc(body, name="adamw_small", out_shape=out_shape)(*ins)
    return {nm: tuple(res[4 * p:4 * p + 4]) for p, nm in enumerate(names)}, res[4 * np_]


SHARD = (("ssd_conv_w", CONV_K * SSD_CONV_DIM // N_DEV), ("gdn_conv_w", CONV_K * GDN_CONV_DIM // N_DEV))
SHARD_ROWS = 24


def _rows_of(size):
    return -(-size // 128)


def _pack(vals, layout, total_rows):
    parts = []
    for (name, size), val in zip(layout, vals):
        flat = val.reshape(-1).astype(jnp.float32)
        parts.append(jnp.pad(flat, (0, _rows_of(size) * 128 - size)).reshape(-1, 128))
    used = sum(_rows_of(s) for _, s in layout)
    parts.append(jnp.zeros((total_rows - used, 128), jnp.float32))
    return jnp.concatenate(parts, axis=0)


def _conv_full(gathered_flat, ccols):
    return gathered_flat.reshape(N_DEV, CONV_K, ccols // N_DEV).transpose(1, 0, 2).reshape(CONV_K, ccols)


def kernel(x, norm_w, w_in, ssd_conv_w, ssd_conv_b, ssd_dt_bias, ssd_a_log, ssd_d, ssd_norm_w, gdn_conv_w, gdn_dt_bias, gdn_a_log, gdn_norm_w, w_out, final_norm_w, loss_target, m_norm_w, m_w_in, m_ssd_conv_w, m_ssd_conv_b, m_ssd_dt_bias, m_ssd_a_log, m_ssd_d, m_ssd_norm_w, m_gdn_conv_w, m_gdn_dt_bias, m_gdn_a_log, m_gdn_norm_w, m_w_out, m_final_norm_w, v_norm_w, v_w_in, v_ssd_conv_w, v_ssd_conv_b, v_ssd_dt_bias, v_ssd_a_log, v_ssd_d, v_ssd_norm_w, v_gdn_conv_w, v_gdn_dt_bias, v_gdn_a_log, v_gdn_norm_w, v_w_out, v_final_norm_w):
    f32 = jnp.float32
    w = dict(norm_w=norm_w, w_in=w_in, ssd_conv_w=ssd_conv_w, ssd_conv_b=ssd_conv_b, ssd_dt_bias=ssd_dt_bias,
             ssd_a_log=ssd_a_log, ssd_d=ssd_d, ssd_norm_w=ssd_norm_w, gdn_conv_w=gdn_conv_w, gdn_dt_bias=gdn_dt_bias,
             gdn_a_log=gdn_a_log, gdn_norm_w=gdn_norm_w, w_out=w_out, final_norm_w=final_norm_w)
    m = dict(norm_w=m_norm_w, w_in=m_w_in, ssd_conv_w=m_ssd_conv_w, ssd_conv_b=m_ssd_conv_b, ssd_dt_bias=m_ssd_dt_bias,
             ssd_a_log=m_ssd_a_log, ssd_d=m_ssd_d, ssd_norm_w=m_ssd_norm_w, gdn_conv_w=m_gdn_conv_w,
             gdn_dt_bias=m_gdn_dt_bias, gdn_a_log=m_gdn_a_log, gdn_norm_w=m_gdn_norm_w, w_out=m_w_out,
             final_norm_w=m_final_norm_w)
    v = dict(norm_w=v_norm_w, w_in=v_w_in, ssd_conv_w=v_ssd_conv_w, ssd_conv_b=v_ssd_conv_b, ssd_dt_bias=v_ssd_dt_bias,
             ssd_a_log=v_ssd_a_log, ssd_d=v_ssd_d, ssd_norm_w=v_ssd_norm_w, gdn_conv_w=v_gdn_conv_w,
             gdn_dt_bias=v_gdn_dt_bias, gdn_a_log=v_gdn_a_log, gdn_norm_w=v_gdn_norm_w, w_out=v_w_out,
             final_norm_w=v_final_norm_w)
    names = list(w)
    shapes = {n: w[n].shape for n in names}

    xl, tgt = x[0], loss_target[0]
    cs = _consts()
    dtb_s = _pad_lanes(ssd_dt_bias, 0)
    alog_s = _pad_lanes(ssd_a_log, 0)
    dpar = _pad_lanes(ssd_d, 0)
    dtb_g = _pad_lanes(gdn_dt_bias, 16)
    alog_g = _pad_lanes(gdn_a_log, 16)
    nw_g = gdn_norm_w.reshape(1, 128)
    nw_s = ssd_norm_w.reshape(1, 1024)
    cb_s = ssd_conv_b.reshape(1, 1536)
    nw1 = norm_w.reshape(1, D_MODEL)

    (g_w_in,) = all_gather([w_in[0].T.astype(_MM)], "gather_w_in")
    w_perm = g_w_in.reshape(IN_DIM, D_MODEL)
    conv_pack = _pack([w["ssd_conv_w"], w["gdn_conv_w"]], SHARD, SHARD_ROWS)
    u, z, xbc, gate, qkv, sm, g_w_out, g_conv = inproj_fwd(xl, nw1, w_perm, [w_out[0].astype(_MM), conv_pack])
    w_out_full = g_w_out.reshape(MIX_WIDTH, D_MODEL)
    ssd_cw = _conv_full(g_conv[:, 0:6].reshape(N_DEV, -1), SSD_CONV_DIM)
    gdn_cw = _conv_full(g_conv[:, 6:18].reshape(N_DEV, -1), GDN_CONV_DIM)

    nc = xl.shape[0] // CHUNK
    y_ssd, hs, pre_s, y_gdn, ss, ts, pre_g = _chunk_call(
        [ssd_fwd(z, xbc, sm, ssd_cw, cb_s, dtb_s, alog_s, dpar, nw_s, cs),
         gdn_fwd(gate, qkv, sm, gdn_cw, dtb_g, alog_g, nw_g, cs)], "scan_fwd", nc, False)
    dout, dys, dyg, g_wout, g_fnw, loss_l = out_fwd_bwd(xl, tgt, y_ssd, y_gdn, w_out_full,
                                                        final_norm_w.reshape(1, D_MODEL))
    (dz, dxbc, g_cw_s, g_cb_s, g_dtb_s, g_alog_s, g_d, g_nw_s,
     dgate, dqkv, dsm, g_cw_g, g_dtb_g, g_alog_g, g_nw_g) = _chunk_call(
        [ssd_bwd(z, xbc, pre_s, sm, hs, dys, ssd_cw, dtb_s, alog_s, dpar, nw_s, cs),
         gdn_bwd(gate, qkv, pre_g, sm, ss, ts, dyg, gdn_cw, dtb_g, alog_g, nw_g, cs)], "scan_bwd", nc, True)

    t_w_out = g_wout.reshape(N_DEV, MIX_WIDTH // N_DEV, D_MODEL)
    gws = {}
    for dg, (name, _, _) in zip((dz, dxbc, dgate, dsm), (GROUPS[0], GROUPS[1], GROUPS[2], GROUPS[4])):
        gws[name] = grad_w_group(u, dg, "grad_w_in_" + name)
    gws["qkv"], r_w_out = grad_w_group(u, dqkv, "grad_w_in_qkv", [t_w_out])
    t_w_in = unperm_w_in(gws["z"], gws["xbc"], gws["gate"], gws["qkv"], gws["sm"])
    dx, g_nw, r_w_in = inproj_bwd_dx(xl, dout, nw1, w_perm, (dz, dxbc, dgate, dqkv, dsm), [t_w_in])

    accs = dict(norm_w=g_nw, ssd_conv_b=g_cb_s, ssd_dt_bias=g_dtb_s, ssd_a_log=g_alog_s, ssd_d=g_d,
                ssd_norm_w=g_nw_s, gdn_dt_bias=g_dtb_g, gdn_a_log=g_alog_g, gdn_norm_w=g_nw_g, final_norm_w=g_fnw,
                ssd_conv_w=g_cw_s, gdn_conv_w=g_cw_g)
    r_small = scatter_small([accs[e[0]] for e in SMALL] + [loss_l])

    o_w_in = adamw_sum(r_w_in, w_in[0].T, m_w_in[0].T, v_w_in[0].T, None, "adamw_w_in", cols=256)
    o_w_out = adamw_sum(r_w_out, w_out[0], m_w_out[0], v_w_out[0], 64, "adamw_w_out")
    row = lambda d: {n: (a.reshape(1, -1) if a.ndim == 1 else a) for n, a in d.items()}
    o_small, loss_sum = adamw_small(r_small, row(w), row(m), row(v))

    loss = loss_sum[0, 0]
    outs = [loss, dx[None]]
    for k in range(4):
        parts = {n: o_small[n][k] for n in o_small}
        parts["w_in"] = o_w_in[k].T
        parts["w_out"] = o_w_out[k]
        outs += [parts[n].reshape(shapes[n]) for n in names]
    return tuple(outs)
```

```python
import functools

import jax
import jax.numpy as jnp
import numpy as np
from jax import lax
from jax.experimental import pallas as pl
from jax.experimental.pallas import tpu as pltpu

_MM = jnp.bfloat16

D_MODEL = 1024
CHUNK = 64
CONV_K = 4
EPS = 1e-6
SSD_CONV_DIM = 1536
GDN_HEADS = 8
GDN_DK = 128
GDN_CONV_DIM = 3072
MIX_WIDTH = 2048
IN_DIM = 6688
N_DEV = 8
W_IN_SHARD = IN_DIM // N_DEV
HI = lax.Precision.HIGHEST
HIGH = lax.Precision.HIGH
VMEM_LIMIT = 56 * 1024 * 1024

ADAM_LR = 0.001
ADAM_B1 = 0.9
ADAM_B2 = 0.999
ADAM_EPS = 1e-08
ADAM_WD = 0.01
ADAM_STEP = 10


def _pc(body, **kw):
    return pl.pallas_call(body, **kw)


def _pc_comm(body, **kw):
    return pl.pallas_call(body, **kw)


def _cparams(sem):
    return pltpu.CompilerParams(dimension_semantics=sem, vmem_limit_bytes=VMEM_LIMIT)


def _sig(x):
    return 0.5 * jnp.tanh(0.5 * x) + 0.5


@jax.custom_vjp
def _sigmoid(x):
    return _sig(x)


def _sigmoid_fwd(x):
    s = _sig(x)
    return s, s


def _sigmoid_bwd(s, g):
    return (g * s * (1.0 - s),)


_sigmoid.defvjp(_sigmoid_fwd, _sigmoid_bwd)


@jax.custom_vjp
def _silu(x):
    return x * _sig(x)


def _silu_fwd(x):
    s = _sig(x)
    return x * s, (x, s)


def _silu_bwd(res, g):
    x, s = res
    return (g * (s * (1.0 + x * (1.0 - s))),)


_silu.defvjp(_silu_fwd, _silu_bwd)


def _softplus_impl(x):
    return jnp.maximum(x, 0.0) + jnp.log(1.0 + jnp.exp(-jnp.abs(x)))


@jax.custom_vjp
def _softplus(x):
    return _softplus_impl(x)


def _softplus_fwd(x):
    return _softplus_impl(x), x


def _softplus_bwd(x, g):
    return (g * _sig(x),)


_softplus.defvjp(_softplus_fwd, _softplus_bwd)


def _lane_bcast_impl(x, k):
    return jnp.broadcast_to(x[..., k:k + 1], x.shape)


@functools.partial(jax.custom_vjp, nondiff_argnums=(1,))
def _lane_bcast(x, k):
    return _lane_bcast_impl(x, k)


def _lane_bcast_fwd(x, k):
    return _lane_bcast_impl(x, k), None


def _lane_bcast_bwd(k, _, g):
    lane = lax.broadcasted_iota(jnp.int32, g.shape, g.ndim - 1)
    return (jnp.where(lane == k, jnp.sum(g, axis=-1, keepdims=True), 0.0),)


_lane_bcast.defvjp(_lane_bcast_fwd, _lane_bcast_bwd)


def _mm(a, b):
    return jnp.dot(a.astype(_MM), b.astype(_MM), preferred_element_type=jnp.float32)


def _mm_nt(a, b):
    return lax.dot_general(a.astype(_MM), b.astype(_MM), (((1,), (1,)), ((), ())),
                           preferred_element_type=jnp.float32)


def _mm_tn(a, b):
    return lax.dot_general(a.astype(_MM), b.astype(_MM), (((0,), (0,)), ((), ())),
                           preferred_element_type=jnp.float32)


def _dot_hi(a, b):
    return jnp.dot(a, b, precision=HI, preferred_element_type=jnp.float32)


def _bmm(a, b):
    return lax.dot_general(a.astype(_MM), b.astype(_MM), (((2,), (1,)), ((0,), (0,))),
                           preferred_element_type=jnp.float32)


def _bmm_nt(a, b):
    return lax.dot_general(a.astype(_MM), b.astype(_MM), (((2,), (2,)), ((0,), (0,))),
                           preferred_element_type=jnp.float32)


def _bmm_tn(a, b):
    return lax.dot_general(a.astype(_MM), b.astype(_MM), (((1,), (1,)), ((0,), (0,))),
                           preferred_element_type=jnp.float32)


def _bmm_hi(a, b):
    return lax.dot_general(a, b, (((2,), (1,)), ((0,), (0,))), precision=HIGH, preferred_element_type=jnp.float32)


def _bmm_nt_hi(a, b):
    return lax.dot_general(a, b, (((2,), (2,)), ((0,), (0,))), precision=HIGH, preferred_element_type=jnp.float32)


def _bmm_tn_hi(a, b):
    return lax.dot_general(a, b, (((1,), (1,)), ((0,), (0,))), precision=HIGH, preferred_element_type=jnp.float32)


def _consts():
    l = np.arange(CHUNK)
    tri = (l[:, None] >= l[None, :]).astype(np.float32)
    lane = np.arange(128)
    i2 =(l[:, None] == (lane[None, :] % 64)).astype(np.float32)
    mask2 = (l[:, None] >= (lane[None, :] % 64)).astype(np.float32)
    lo = (lane < 64).astype(np.float32)[None, :]
    i64 = np.eye(CHUNK, dtype=np.float32)
    strict = (l[:, None] > l[None, :]).astype(np.float32)
    return dict(tri=jnp.asarray(tri), i2=jnp.asarray(i2), mask2=jnp.asarray(mask2), lo=jnp.asarray(lo),
                i64=jnp.asarray(i64), strict=jnp.asarray(strict))


def _ssd_chunk(xs_pre, b_pre, c_pre, z, sm, ht, dtb, alog, dpar, nw, tri, i2, mask2, lo):
    lane = lax.broadcasted_iota(jnp.int32, (1, 128), 1)
    m16 = lane < 16
    dt = jnp.where(m16, _softplus(sm + dtb), 0.0)
    a_neg = -jnp.exp(alog)
    cum = _dot_hi(tri, dt * a_neg)
    row = lax.broadcasted_iota(jnp.int32, (CHUNK, 1), 0)
    is_last = row == CHUNK - 1
    hi = 1.0 - lo
    bm = [_silu(b) for b in b_pre]
    cm = [_silu(c) for c in c_pre]
    cb2 = [_mm_nt(cm[g], jnp.concatenate([bm[g], bm[g]], axis=0)) for g in range(2)]
    yg, ht_next = [], []
    for j in range(8):
        g = j // 4
        pair = lambda v, j=j: jnp.where(lo > 0.5, _lane_bcast(v, 2 * j), _lane_bcast(v, 2 * j + 1))
        xs = _silu(xs_pre[j])
        dte = pair(dt)
        cume = pair(cum)
        cum_last = jnp.sum(jnp.where(is_last, cume, 0.0), axis=0, keepdims=True)
        xdt = xs * dte
        rowv = jnp.sum(cume * i2, axis=0, keepdims=True)
        lm = jnp.exp(jnp.where(mask2 > 0.5, cume - rowv, -jnp.inf))
        m = cb2[g] * lm
        xblk = jnp.concatenate([xdt * lo, xdt * hi], axis=0)
        y = _mm(m, xblk)
        y = y + _mm(cm[g], ht[j]) * jnp.exp(cume)
        y = y + pair(dpar) * xs
        yg.append(y * _silu(z[j]))
        st = _mm_tn(bm[g], xdt * jnp.exp(cum_last - cume))
        ht_next.append(ht[j] * jnp.exp(cum_last) + st)
    outs = []
    for g in range(2):
        ss = sum(jnp.sum(yg[j] * yg[j], axis=-1, keepdims=True) for j in range(4 * g, 4 * g + 4))
        rs = lax.rsqrt(ss * (1.0 / 512.0) + EPS)
        for j in range(4 * g, 4 * g + 4):
            outs.append(yg[j] * rs * nw[j])
    return outs, ht_next


def _tri_inverse(a):
    eye = jnp.eye(CHUNK, dtype=jnp.float32)[None]
    p = eye - a
    x = _bmm_hi(a, a)
    for _ in range(4):
        both = _bmm_hi(jnp.concatenate([p, x], axis=1), x)
        p = p + both[:, :CHUNK]
        x = both[:, CHUNK:]
    return p + _bmm_hi(p, x)


def _solve_apply(t, r1, r2):
    both = _bmm_hi(t, jnp.concatenate([r1, r2], axis=-1))
    n = r1.shape[-1]
    return both[..., :n], both[..., n:]


@jax.custom_vjp
def _solve(a, r1, r2, t):
    return _solve_apply(t, r1, r2)


def _solve_fwd(a, r1, r2, t):
    u, w = _bmm_hi(t, r1), _bmm_hi(t, r2)
    return (u, w), (t, u, w)


def _solve_bwd(res, cts):
    t, u, w = res
    du, dw = cts
    dr1 = _bmm_tn_hi(t, du)
    dr2 = _bmm_tn_hi(t, dw)
    da = -(_bmm_nt_hi(dr1, u) + _bmm_nt_hi(dr2, w))
    return da, dr1, dr2, jnp.zeros_like(t)


_solve.defvjp(_solve_fwd, _solve_bwd)


def _gdn_chunk(q_pre, k_pre, v_pre, gate, sm, s, dtb, alog, nw, tri, i64, strict, t_in=None):
    lane = lax.broadcasted_iota(jnp.int32, (1, 128), 1)
    m_a = (lane >= 16) & (lane < 24)
    g_full = jnp.where(m_a, -jnp.exp(alog) * _softplus(sm + dtb), 0.0)
    gc = _dot_hi(tri, g_full)
    sig = _sigmoid(sm)
    heads = lambda f: jnp.concatenate([f(h)[None] for h in range(GDN_HEADS)], axis=0)
    gc3 = heads(lambda h: _lane_bcast(gc, 16 + h))
    beta3 = heads(lambda h: _lane_bcast(sig, 24 + h))
    q = _silu(q_pre)
    q = q * lax.rsqrt(jnp.sum(q * q, axis=-1, keepdims=True) + EPS) * (GDN_DK ** -0.5)
    k = _silu(k_pre)
    k = k * lax.rsqrt(jnp.sum(k * k, axis=-1, keepdims=True) + EPS)
    v = _silu(v_pre)
    gcl = gc3[:, :, :CHUNK]
    gc_row = jnp.sum(gcl * i64[None], axis=1, keepdims=True)
    incl = (strict + i64)[None] > 0.5
    decay = jnp.exp(jnp.where(incl, gcl - gc_row, -jnp.inf))
    kb = k * beta3
    a = jnp.where(strict[None] > 0.5, _bmm_nt(kb, k) * decay, 0.0)
    egc = jnp.exp(gc3)
    t = _tri_inverse(a) if t_in is None else t_in
    u, w = _solve(a, v * beta3, kb * egc, t)
    attn = _bmm_nt(q, k) * decay
    row = lax.broadcasted_iota(jnp.int32, (1, CHUNK, 1), 1)
    gl = jnp.sum(jnp.where(row == CHUNK - 1, gc3, 0.0), axis=1, keepdims=True)
    q_dec = q * egc
    k_dec = k * jnp.exp(gl - gc3)
    v_new = u - _bmm(w, s)
    o = _bmm(q_dec, s) + _bmm(attn, v_new)
    s_next = s * jnp.exp(gl) + _bmm_tn(k_dec, v_new)
    on = o * lax.rsqrt(jnp.mean(o * o, axis=-1, keepdims=True) + EPS) * nw
    return on * _silu(gate), s_next, t


def _conv_fwd(pbuf, w_ref, c0, c1):
    blk = pbuf[:, c0:c1]
    acc = w_ref[CONV_K - 1:CONV_K, c0:c1] * blk[8:72]
    for j in range(CONV_K - 1):
        acc = acc + w_ref[j:j + 1, c0:c1] * pltpu.roll(blk, CONV_K - 1 - j, axis=0)[8:72]
    return acc


MESH = pl.DeviceIdType.MESH
ANY = pl.BlockSpec(memory_space=pl.ANY)


def _me():
    x, y, c = lax.axis_index("x"), lax.axis_index("y"), lax.axis_index("c")
    return x, y, c, 4 * x + 2 * y + c


def _peer(r):
    x, y, c, _ = _me()
    px = 1 - x if r & 4 else x
    py = 1 - y if r & 2 else y
    pc = 1 - c if r & 1 else c
    return (px, py, pc), 4 * px + 2 * py + pc


def _exchange_ops(kind, in_ref, out_ref, send_sems, recv_sems, local_sem):
    me = _me()[3]
    local = pltpu.make_async_copy(in_ref.at[me] if kind == "scatter" else in_ref, out_ref.at[me], local_sem)
    sends, recvs = [], []
    for r in range(1, N_DEV):
        peer, pidx = _peer(r)
        src = in_ref.at[pidx] if kind == "scatter" else in_ref
        sems = dict(send_sem=send_sems.at[r - 1], recv_sem=recv_sems.at[r - 1], device_id=peer, device_id_type=MESH)
        sends.append(pltpu.make_async_remote_copy(src_ref=src, dst_ref=out_ref.at[me], **sems))
        recvs.append(pltpu.make_async_remote_copy(src_ref=src, dst_ref=out_ref.at[pidx], **sems))

    def start():
        local.start()
        for cp in sends:
            cp.start()

    def wait():
        for cp in recvs:
            cp.wait_recv()
        for cp in sends:
            cp.wait_send()
        local.wait()

    return start, wait


def _exchange_sems(n):
    return [pltpu.SemaphoreType.DMA((N_DEV - 1,)), pltpu.SemaphoreType.DMA((N_DEV - 1,)),
            pltpu.SemaphoreType.DMA(())] * n


def _exchange_out_shape(kind, a):
    return jax.ShapeDtypeStruct(a.shape if kind == "scatter" else (N_DEV,) + a.shape, a.dtype)


def _hosting(body, n_in, n_out, n_scratch, kinds, first, last):
    ne = len(kinds)

    def wrapped(*refs):
        ins, ex_in = refs[:n_in], refs[n_in:n_in + ne]
        o0 = n_in + ne
        outs, ex_out = refs[o0:o0 + n_out], refs[o0 + n_out:o0 + n_out + ne]
        s0 = o0 + n_out + ne
        scr, sems = refs[s0:s0 + n_scratch], refs[s0 + n_scratch:]
        ops = [_exchange_ops(kinds[e], ex_in[e], ex_out[e], *sems[3 * e:3 * e + 3]) for e in range(ne)]

        @pl.when(first())
        def _():
            for start, _ in ops:
                start()

        body(*ins, *outs, *scr)

        @pl.when(last())
        def _():
            for _, wait in ops:
                wait()

    return wrapped


GROUPS = (("z", 0, 1024), ("xbc", 1024, 2560), ("gate", 2560, 3584), ("qkv", 3584, 6656), ("sm", 6656, 6784))
GROUP_ROWS = dict(z=((0, 1024),), xbc=((1024, 2560),), gate=((2576, 3600),), qkv=((3600, 6672),),
                  sm=((2560, 2576), (6672, 6688)))


def _w_rows(w_ref, name, width):
    pieces = [w_ref[a:b, :] for a, b in GROUP_ROWS[name]]
    n = sum(b - a for a, b in GROUP_ROWS[name])
    if n < width:
        pieces.append(jnp.zeros((width - n, D_MODEL), w_ref.dtype))
    return pieces[0] if len(pieces) == 1 else jnp.concatenate(pieces, axis=0)


def inproj_fwd(x, norm_w, w_perm, gathered):
    t = x.shape[0]
    tm = min(512, t)
    steps = t // tm
    kinds = ["gather"] * len(gathered)

    def body(x_ref, nw_ref, w_ref, u_ref, z_ref, xbc_ref, gate_ref, qkv_ref, sm_ref):
        xf = x_ref[...]
        rstd = lax.rsqrt(jnp.mean(xf * xf, axis=-1, keepdims=True) + EPS)
        u = (xf * rstd * nw_ref[...]).astype(_MM)
        u_ref[...] = u
        for (name, c0, c1), o_ref in zip(GROUPS, (z_ref, xbc_ref, gate_ref, qkv_ref, sm_ref)):
            o_ref[...] = lax.dot_general(u, _w_rows(w_ref, name, c1 - c0), (((1,), (1,)), ((), ())),
                                         preferred_element_type=jnp.float32)

    outs = [jax.ShapeDtypeStruct((t, D_MODEL), _MM)] + [jax.ShapeDtypeStruct((t, c1 - c0), jnp.float32)
                                                        for _, c0, c1 in GROUPS]
    hosted = _hosting(body, 3, 6, 0, kinds, lambda: pl.program_id(0) == 0, lambda: pl.program_id(0) == steps - 1)
    return _pc_comm(
        hosted, name="inproj_fwd", grid=(steps,),
        in_specs=[pl.BlockSpec((tm, D_MODEL), lambda i: (i, 0)),
                  pl.BlockSpec((1, D_MODEL), lambda i: (0, 0)),
                  pl.BlockSpec((IN_DIM, D_MODEL), lambda i: (0, 0), pipeline_mode=pl.Buffered(1))] +
                 [ANY] * len(gathered),
        out_specs=[pl.BlockSpec((tm, D_MODEL), lambda i: (i, 0))] +
                  [pl.BlockSpec((tm, c1 - c0), lambda i: (i, 0)) for _, c0, c1 in GROUPS] + [ANY] * len(gathered),
        out_shape=outs + [_exchange_out_shape("gather", a) for a in gathered],
        scratch_shapes=_exchange_sems(len(gathered)), compiler_params=_cparams(("arbitrary",)),
    )(x, norm_w, w_perm, *gathered)


SUB_FWD = 4
SUB_BWD = 2


def _halo_spec(width, idx_fn):
    return pl.BlockSpec((8, width), lambda i: (jnp.maximum(idx_fn(i) * (SUB_FWD * CHUNK // 8) - 1, 0), 0))


def _when_first(shared, fn):
    if shared["first"] is not False:
        pl.when(shared["first"])(fn)


def _full(shape):
    nd = len(shape)
    return pl.BlockSpec(shape, lambda i: (0,) * nd)


def _ssd_split(pre_fn, z_ref, sm_ref):
    xs_pre = [pre_fn(128 * j, 128 * j + 128) for j in range(8)]
    b_pre = [pre_fn(1024 + 128 * g, 1152 + 128 * g) for g in range(2)]
    c_pre = [pre_fn(1280 + 128 * g, 1408 + 128 * g) for g in range(2)]
    z = [z_ref[:, 128 * j:128 * j + 128] for j in range(8)]
    return xs_pre, b_pre, c_pre, z, sm_ref[...]


def ssd_fwd(z, xbc, sm, conv_w, conv_b, dtb, alog, dpar, nw, cs):
    t = z.shape[0]
    nc = t // CHUNK

    def body(shared, z_ref, xbc_ref, halo_ref, sm_ref, cw_ref, cb_ref, dtb_ref, alog_ref, dpar_ref, nw_ref,
             tri_ref, i2_ref, mask2_ref, lo_ref, y_ref, hs_ref, pre_ref, pbuf, ht_scr):
        def init():
            ht_scr[...] = jnp.zeros_like(ht_scr)

        _when_first(shared, init)
        pbuf[0:8, :] = jnp.where(shared["first"], 0.0, halo_ref[...])
        pbuf[8:72, :] = xbc_ref[...]

        def pre_fn(c0, c1):
            pre = _conv_fwd(pbuf, cw_ref, c0, c1) + cb_ref[:, c0:c1]
            pre_ref[:, c0:c1] = pre
            return pre

        xs_pre, b_pre, c_pre, zz, smv = _ssd_split(pre_fn, z_ref, sm_ref)
        ht = [ht_scr[:, 128 * j:128 * j + 128] for j in range(8)]
        hs_ref[0] = ht_scr[...]
        nwl = [nw_ref[:, 128 * j:128 * j + 128] for j in range(8)]
        outs, ht_next = _ssd_chunk(xs_pre, b_pre, c_pre, zz, smv, ht, dtb_ref[...], alog_ref[...], dpar_ref[...],
                                   nwl, tri_ref[...], i2_ref[...], mask2_ref[...], lo_ref[...])
        for j in range(8):
            y_ref[:, 128 * j:128 * j + 128] = outs[j].astype(y_ref.dtype)
            ht_scr[:, 128 * j:128 * j + 128] = ht_next[j]

    blk = lambda w: pl.BlockSpec((SUB_FWD * CHUNK, w), lambda i: (i, 0))
    return dict(
        body=body,
        in_kinds=["rows", "rows", ("halo", 1), "rows"] + ["full"] * 10, out_kinds=["rows", "state", "rows"],
        in_specs=[blk(1024), blk(1536), _halo_spec(1536, lambda i: i), blk(128),
                  _full((CONV_K, 1536)), _full((1, 1536)), _full((1, 128)), _full((1, 128)), _full((1, 128)),
                  _full((1, 1024)), _full((64, 64)), _full((64, 128)), _full((64, 128)),
                  _full((1, 128))],
        out_specs=[blk(1024), pl.BlockSpec((SUB_FWD, 128, 1024), lambda i: (i, 0, 0)), blk(1536)],
        out_shape=[jax.ShapeDtypeStruct((t, 1024), _MM), jax.ShapeDtypeStruct((nc, 128, 1024), jnp.float32),
                   jax.ShapeDtypeStruct((t, 1536), jnp.float32)],
        scratch=[pltpu.VMEM((72, 1536), jnp.float32), pltpu.VMEM((128, 1024), jnp.float32)],
        args=[z, xbc, xbc, sm, conv_w, conv_b, dtb, alog, dpar, nw, cs["tri"], cs["i2"], cs["mask2"], cs["lo"]])


def _conv_bwd(dpre_list, col_ranges, dbuf, carry, x_ref, cw_ref, dx_ref, dcw_ref, dcb_ref, first):
    for dpre, (c0, c1) in zip(dpre_list, col_ranges):
        dbuf[0:64, c0:c1] = dpre
    dbuf[64:72, :] = jnp.where(first, 0.0, carry[...])
    carry[...] = dbuf[0:8, :]
    for (c0, c1) in col_ranges:
        xin = x_ref[:, c0:c1]
        blk = dbuf[:, c0:c1]
        acc = None
        for j in range(CONV_K):
            sh = blk[0:64] if j == CONV_K - 1 else pltpu.roll(blk, 72 - (CONV_K - 1 - j), axis=0)[0:64]
            term = cw_ref[j:j + 1, c0:c1] * sh
            acc = term if acc is None else acc + term
            dcw_ref[j:j + 1, c0:c1] += jnp.sum(xin * sh, axis=0, keepdims=True)
        dx_ref[:, c0:c1] = acc.astype(dx_ref.dtype)
        if dcb_ref is not None:
            dcb_ref[0:1, c0:c1] += jnp.sum(dbuf[0:64, c0:c1], axis=0, keepdims=True)


def ssd_bwd(z, xbc, pre, sm, hs, dy, conv_w, dtb, alog, dpar, nw, cs):
    t = z.shape[0]
    nc = t // CHUNK

    def body(shared, z_ref, xbc_ref, pre_ref, sm_ref, hs_ref, dy_ref, cw_ref, dtb_ref, alog_ref, dpar_ref, nw_ref,
             tri_ref, i2_ref, mask2_ref, lo_ref,
             dz_ref, dxbc_ref, dcw_ref, dcb_ref, ddtb_ref, dalog_ref, ddpar_ref, dnw_ref,
             dbuf, carry, dht_scr):
        def init():
            dht_scr[...] = jnp.zeros_like(dht_scr)
            dcw_ref[...] = jnp.zeros_like(dcw_ref)
            dcb_ref[...] = jnp.zeros_like(dcb_ref)
            ddtb_ref[...] = jnp.zeros_like(ddtb_ref)
            dalog_ref[...] = jnp.zeros_like(dalog_ref)
            ddpar_ref[...] = jnp.zeros_like(ddpar_ref)
            dnw_ref[...] = jnp.zeros_like(dnw_ref)

        _when_first(shared, init)
        pre_fn = lambda c0, c1: pre_ref[:, c0:c1]
        xs_pre, b_pre, c_pre, zz, smv = _ssd_split(pre_fn, z_ref, sm_ref)
        ht = [hs_ref[0, :, 128 * j:128 * j + 128] for j in range(8)]
        nwl = [nw_ref[:, 128 * j:128 * j + 128] for j in range(8)]
        consts = (tri_ref[...], i2_ref[...], mask2_ref[...], lo_ref[...])

        def f(xs_pre, b_pre, c_pre, zz, smv, ht, dtb, alog, dpar, nwl):
            return _ssd_chunk(xs_pre, b_pre, c_pre, zz, smv, ht, dtb, alog, dpar, nwl, *consts)

        _, vjp = jax.vjp(f, xs_pre, b_pre, c_pre, zz, smv, ht, dtb_ref[...], alog_ref[...], dpar_ref[...], nwl)
        dys = [dy_ref[:, 128 * j:128 * j + 128] for j in range(8)]
        dhts = [dht_scr[:, 128 * j:128 * j + 128] for j in range(8)]
        dxs, db, dc, dzz, dsm, dht, ddtb, dalog, ddpar, dnwl = vjp((dys, dhts))
        for j in range(8):
            dz_ref[:, 128 * j:128 * j + 128] = dzz[j].astype(dz_ref.dtype)
            dht_scr[:, 128 * j:128 * j + 128] = dht[j]
            dnw_ref[0:1, 128 * j:128 * j + 128] += dnwl[j]
        shared["dsm_ssd"] = dsm
        ddtb_ref[0:1, :] += ddtb
        dalog_ref[0:1, :] += dalog
        ddpar_ref[0:1, :] += ddpar
        ranges = ([(128 * j, 128 * j + 128) for j in range(8)] + [(1024 + 128 * g, 1152 + 128 * g) for g in range(2)]
                  + [(1280 + 128 * g, 1408 + 128 * g) for g in range(2)])
        _conv_bwd(dxs + db + dc, ranges, dbuf, carry, xbc_ref, cw_ref, dxbc_ref, dcw_ref, dcb_ref, shared["first"])

    ns = nc // SUB_BWD
    rblk = lambda w: pl.BlockSpec((SUB_BWD * CHUNK, w), lambda i: (ns - 1 - i, 0))
    acc = lambda w: pl.BlockSpec((8, w), lambda i: (0, 0))
    f32 = jnp.float32
    return dict(
        body=body,
        in_kinds=["rows"] * 4 + ["state", "rows"] + ["full"] * 9, out_kinds=["rows", "rows"] + ["full"] * 6,
        in_specs=[rblk(1024), rblk(1536), rblk(1536), rblk(128),
                  pl.BlockSpec((SUB_BWD, 128, 1024), lambda i: (ns - 1 - i, 0, 0)), rblk(1024),
                  _full((CONV_K, 1536)), _full((1, 128)), _full((1, 128)), _full((1, 128)),
                  _full((1, 1024)), _full((64, 64)), _full((64, 128)), _full((64, 128)),
                  _full((1, 128))],
        out_specs=[rblk(1024), rblk(1536), acc(1536), acc(1536), acc(128), acc(128), acc(128), acc(1024)],
        out_shape=[jax.ShapeDtypeStruct((t, 1024), f32), jax.ShapeDtypeStruct((t, 1536), f32),
                   jax.ShapeDtypeStruct((8, 1536), f32),
                   jax.ShapeDtypeStruct((8, 1536), f32), jax.ShapeDtypeStruct((8, 128), f32),
                   jax.ShapeDtypeStruct((8, 128), f32), jax.ShapeDtypeStruct((8, 128), f32),
                   jax.ShapeDtypeStruct((8, 1024), f32)],
        scratch=[pltpu.VMEM((72, 1536), f32), pltpu.VMEM((8, 1536), f32), pltpu.VMEM((128, 1024), f32)],
        args=[z, xbc, pre, sm, hs, dy, conv_w, dtb, alog, dpar, nw, cs["tri"], cs["i2"], cs["mask2"], cs["lo"]])


def _gdn_split(pre_fn, gate_ref):
    def heads(base):
        return jnp.stack([pre_fn(base + 128 * h, base + 128 * h + 128) for h in range(GDN_HEADS)])
    gate = jnp.stack([gate_ref[:, 128 * h:128 * h + 128] for h in range(GDN_HEADS)])
    return heads(0), heads(1024), heads(2048), gate


def gdn_fwd(gate, qkv, sm, conv_w, dtb, alog, nw, cs):
    t = gate.shape[0]
    nc = t // CHUNK

    def body(shared, gate_ref, qkv_ref, halo_ref, sm_ref, cw_ref, dtb_ref, alog_ref, nw_ref,
             tri_ref, i64_ref, strict_ref, o_ref, ss_ref, ts_ref, pre_ref, pbuf, s_scr):
        def init():
            s_scr[...] = jnp.zeros_like(s_scr)

        _when_first(shared, init)
        pbuf[0:8, :] = jnp.where(shared["first"], 0.0, halo_ref[...])
        pbuf[8:72, :] = qkv_ref[...]

        def pre_fn(c0, c1):
            pre = _conv_fwd(pbuf, cw_ref, c0, c1)
            pre_ref[:, c0:c1] = pre
            return pre

        q_pre, k_pre, v_pre, g3 = _gdn_split(pre_fn, gate_ref)
        s = s_scr[...]
        ss_ref[0] = s
        out, s_next, tinv = _gdn_chunk(q_pre, k_pre, v_pre, g3, sm_ref[...], s, dtb_ref[...], alog_ref[...],
                                       nw_ref[...], tri_ref[...], i64_ref[...], strict_ref[...])
        ts_ref[0] = tinv
        s_scr[...] = s_next
        for h in range(GDN_HEADS):
            o_ref[:, 128 * h:128 * h + 128] = out[h].astype(o_ref.dtype)

    blk = lambda w: pl.BlockSpec((SUB_FWD * CHUNK, w), lambda i: (i, 0))
    return dict(
        body=body,
        in_kinds=["rows", "rows", ("halo", 1), "rows"] + ["full"] * 7, out_kinds=["rows", "state", "state", "rows"],
        in_specs=[blk(1024), blk(3072), _halo_spec(3072, lambda i: i), blk(128),
                  _full((CONV_K, 3072)), _full((1, 128)), _full((1, 128)), _full((1, 128)),
                  _full((64, 64)), _full((64, 64)), _full((64, 64))],
        out_specs=[blk(1024), pl.BlockSpec((SUB_FWD, 8, 128, 128), lambda i: (i, 0, 0, 0)),
                   pl.BlockSpec((SUB_FWD, 8, CHUNK, CHUNK), lambda i: (i, 0, 0, 0)), blk(3072)],
        out_shape=[jax.ShapeDtypeStruct((t, 1024), _MM), jax.ShapeDtypeStruct((nc, 8, 128, 128), jnp.float32),
                   jax.ShapeDtypeStruct((nc, 8, CHUNK, CHUNK), jnp.float32),
                   jax.ShapeDtypeStruct((t, 3072), jnp.float32)],
        scratch=[pltpu.VMEM((72, 3072), jnp.float32), pltpu.VMEM((8, 128, 128), jnp.float32)],
        args=[gate, qkv, qkv, sm, conv_w, dtb, alog, nw, cs["tri"], cs["i64"], cs["strict"]])


def gdn_bwd(gate, qkv, pre, sm, ss, ts, do, conv_w, dtb, alog, nw, cs):
    t = gate.shape[0]
    nc = t // CHUNK

    def body(shared, gate_ref, qkv_ref, pre_ref, sm_ref, ss_ref, ts_ref, do_ref, cw_ref, dtb_ref, alog_ref,
             nw_ref, tri_ref, i64_ref, strict_ref,
             dgate_ref, dqkv_ref, dsm_ref, dcw_ref, ddtb_ref, dalog_ref, dnw_ref,
             dbuf, carry, ds_scr):
        def init():
            ds_scr[...] = jnp.zeros_like(ds_scr)
            dcw_ref[...] = jnp.zeros_like(dcw_ref)
            ddtb_ref[...] = jnp.zeros_like(ddtb_ref)
            dalog_ref[...] = jnp.zeros_like(dalog_ref)
            dnw_ref[...] = jnp.zeros_like(dnw_ref)

        _when_first(shared, init)

        q_pre, k_pre, v_pre, g3 = _gdn_split(lambda c0, c1: pre_ref[:, c0:c1], gate_ref)
        consts = (tri_ref[...], i64_ref[...], strict_ref[...], ts_ref[0])

        def f(q_pre, k_pre, v_pre, g3, smv, s, dtb, alog, nwv):
            return _gdn_chunk(q_pre, k_pre, v_pre, g3, smv, s, dtb, alog, nwv, *consts)[:2]

        _, vjp = jax.vjp(f, q_pre, k_pre, v_pre, g3, sm_ref[...], ss_ref[0], dtb_ref[...], alog_ref[...], nw_ref[...])
        do3 = jnp.stack([do_ref[:, 128 * h:128 * h + 128] for h in range(GDN_HEADS)])
        dq, dk, dv, dg3, dsm, ds, ddtb, dalog, dnw = vjp((do3, ds_scr[...]))
        ds_scr[...] = ds
        for h in range(GDN_HEADS):
            dgate_ref[:, 128 * h:128 * h + 128] = dg3[h].astype(dgate_ref.dtype)
        dsm_ref[...] = (dsm + shared["dsm_ssd"]).astype(dsm_ref.dtype)
        ddtb_ref[0:1, :] += ddtb
        dalog_ref[0:1, :] += dalog
        dnw_ref[0:1, :] += dnw
        ranges = [(base + 128 * h, base + 128 * h + 128) for base in (0, 1024, 2048) for h in range(GDN_HEADS)]
        dlist = [d[h] for d in (dq, dk, dv) for h in range(GDN_HEADS)]
        _conv_bwd(dlist, ranges, dbuf, carry, qkv_ref, cw_ref, dqkv_ref, dcw_ref, None, shared["first"])

    ns = nc // SUB_BWD
    rblk = lambda w: pl.BlockSpec((SUB_BWD * CHUNK, w), lambda i: (ns - 1 - i, 0))
    acc = lambda w: pl.BlockSpec((8, w), lambda i: (0, 0))
    f32 = jnp.float32
    return dict(
        body=body,
        in_kinds=["rows"] * 4 + ["state", "state", "rows"] + ["full"] * 7, out_kinds=["rows"] * 3 + ["full"] * 4,
        in_specs=[rblk(1024), rblk(3072), rblk(3072), rblk(128),
                  pl.BlockSpec((SUB_BWD, 8, 128, 128), lambda i: (ns - 1 - i, 0, 0, 0)),
                  pl.BlockSpec((SUB_BWD, 8, CHUNK, CHUNK), lambda i: (ns - 1 - i, 0, 0, 0)), rblk(1024),
                  _full((CONV_K, 3072)), _full((1, 128)), _full((1, 128)), _full((1, 128)),
                  _full((64, 64)), _full((64, 64)), _full((64, 64))],
        out_specs=[rblk(1024), rblk(3072), rblk(128), acc(3072), acc(128), acc(128), acc(128)],
        out_shape=[jax.ShapeDtypeStruct((t, 1024), f32), jax.ShapeDtypeStruct((t, 3072), f32),
                   jax.ShapeDtypeStruct((t, 128), f32), jax.ShapeDtypeStruct((8, 3072), f32),
                   jax.ShapeDtypeStruct((8, 128), f32), jax.ShapeDtypeStruct((8, 128), f32),
                   jax.ShapeDtypeStruct((8, 128), f32)],
        scratch=[pltpu.VMEM((72, 3072), f32), pltpu.VMEM((8, 3072), f32), pltpu.VMEM((8, 128, 128), f32)],
        args=[gate, qkv, pre, sm, ss, ts, do, conv_w, dtb, alog, nw, cs["tri"], cs["i64"], cs["strict"]])


def _chunk_call(parts, name, nc, reverse):
    n_in = [len(p["args"]) for p in parts]
    n_out = [len(p["out_shape"]) for p in parts]
    n_scr = [len(p["scratch"]) for p in parts]
    sub = SUB_BWD if reverse else SUB_FWD
    order = list(range(sub))[::-1] if reverse else list(range(sub))

    def view(ref, kind, s, refs):
        if kind == "rows":
            return ref.at[pl.ds(CHUNK * s, CHUNK)]
        if kind == "state":
            return ref.at[pl.ds(s, 1)]
        if kind == "full":
            return ref
        src = refs[kind[1]]
        return ref if s == 0 else src.at[pl.ds(CHUNK * s - 8, 8)]

    def body(*refs):
        ins, outs, scr = refs[:sum(n_in)], refs[sum(n_in):sum(n_in) + sum(n_out)], refs[sum(n_in) + sum(n_out):]
        for s in order:
            shared = {"first": (pl.program_id(0) == 0) if s == order[0] else False}
            for k, p in enumerate(parts):
                i0, o0, s0 = sum(n_in[:k]), sum(n_out[:k]), sum(n_scr[:k])
                p_ins = ins[i0:i0 + n_in[k]]
                p["body"](shared,
                          *[view(r, kd, s, p_ins) for r, kd in zip(p_ins, p["in_kinds"])],
                          *[view(r, kd, s, None) for r, kd in zip(outs[o0:o0 + n_out[k]], p["out_kinds"])],
                          *scr[s0:s0 + n_scr[k]])

    cat = lambda key: [v for p in parts for v in p[key]]
    return _pc(body, name=name, grid=(nc // sub,), in_specs=cat("in_specs"), out_specs=cat("out_specs"),
               out_shape=cat("out_shape"), scratch_shapes=cat("scratch"),
               compiler_params=_cparams(("arbitrary",)))(*cat("args"))


def out_fwd_bwd(x, tgt, y_ssd, y_gdn, w_out, fnw):
    t = x.shape[0]
    tm = min(512, t)
    f32 = jnp.float32

    def body(x_ref, tgt_ref, ys_ref, yg_ref, w_ref, fnw_ref,
             dout_ref, dys_ref, dyg_ref, gw_ref, gfnw_ref, loss_ref, gw_acc):
        i = pl.program_id(0)

        @pl.when(i == 0)
        def _():
            gw_acc[...] = jnp.zeros_like(gw_acc)
            gfnw_ref[...] = jnp.zeros_like(gfnw_ref)
            loss_ref[...] = jnp.zeros_like(loss_ref)

        ys = ys_ref[...]
        yg = yg_ref[...]
        out = x_ref[...] + jnp.dot(ys, w_ref[0:1024, :], preferred_element_type=f32) \
            + jnp.dot(yg, w_ref[1024:2048, :], preferred_element_type=f32)
        rstd = lax.rsqrt(jnp.mean(out * out, axis=-1, keepdims=True) + EPS)
        yhat = out * rstd
        fw = fnw_ref[...]
        e = yhat * fw - tgt_ref[...]
        loss_ref[...] += 0.5 * jnp.sum(jnp.sum(e * e, axis=-1, keepdims=True) * (1.0 / D_MODEL), axis=0, keepdims=True)
        dyf = e * (1.0 / D_MODEL)
        gfnw_ref[0:1, :] += jnp.sum(dyf * yhat, axis=0, keepdims=True)
        dyhat = dyf * fw
        dout = rstd * (dyhat - yhat * jnp.mean(dyhat * yhat, axis=-1, keepdims=True))
        dout_ref[...] = dout
        db = dout.astype(_MM)
        dys_ref[...] = lax.dot_general(db, w_ref[0:1024, :], (((1,), (1,)), ((), ())), preferred_element_type=f32)
        dyg_ref[...] = lax.dot_general(db, w_ref[1024:2048, :], (((1,), (1,)), ((), ())), preferred_element_type=f32)
        gw_acc[0:1024, :] += lax.dot_general(ys, db, (((0,), (0,)), ((), ())), preferred_element_type=f32)
        gw_acc[1024:2048, :] += lax.dot_general(yg, db, (((0,), (0,)), ((), ())), preferred_element_type=f32)

        @pl.when(i == steps - 1)
        def _():
            gw_ref[...] = gw_acc[...].astype(gw_ref.dtype)

    steps = t // tm
    blk = pl.BlockSpec((tm, D_MODEL), lambda i: (i, 0))
    return _pc(
        body, name="out_fwd_bwd", grid=(steps,),
        in_specs=[blk, blk, blk, blk, _full((MIX_WIDTH, D_MODEL)), _full((1, D_MODEL))],
        out_specs=[blk, blk, blk, _full((MIX_WIDTH, D_MODEL)), _full((8, D_MODEL)), _full((1, 128))],
        out_shape=[jax.ShapeDtypeStruct((t, D_MODEL), f32)] * 3 +
                  [jax.ShapeDtypeStruct((MIX_WIDTH, D_MODEL), _MM), jax.ShapeDtypeStruct((8, D_MODEL), f32),
                   jax.ShapeDtypeStruct((1, 128), f32)],
        scratch_shapes=[pltpu.VMEM((MIX_WIDTH, D_MODEL), f32)],
        compiler_params=_cparams(("arbitrary",)),
    )(x, tgt, y_ssd, y_gdn, w_out, fnw)


def inproj_bwd_dx(x, dout, norm_w, w_perm, dgroups, scattered):
    t = x.shape[0]
    tm = min(256, t)
    f32 = jnp.float32

    def body(x_ref, dout_ref, nw_ref, w_ref, dz_ref, dxbc_ref, dgate_ref, dqkv_ref, dsm_ref, dx_ref, gnw_ref):
        i = pl.program_id(0)

        @pl.when(i == 0)
        def _():
            gnw_ref[...] = jnp.zeros_like(gnw_ref)

        du = None
        for (name, c0, c1), d_ref in zip(GROUPS, (dz_ref, dxbc_ref, dgate_ref, dqkv_ref, dsm_ref)):
            term = jnp.dot(d_ref[...].astype(_MM), _w_rows(w_ref, name, c1 - c0), preferred_element_type=f32)
            du = term if du is None else du + term
        xf = x_ref[...]
        rstd = lax.rsqrt(jnp.mean(xf * xf, axis=-1, keepdims=True) + EPS)
        xhat = xf * rstd
        gnw_ref[0:1, :] += jnp.sum(du * xhat, axis=0, keepdims=True)
        dxh = du * nw_ref[...]
        dx_ref[...] = dout_ref[...] + rstd * (dxh - xhat * jnp.mean(dxh * xhat, axis=-1, keepdims=True))

    blk = lambda w: pl.BlockSpec((tm, w), lambda i: (i, 0))
    steps = t // tm
    kinds = ["scatter"] * len(scattered)
    hosted = _hosting(body, 9, 2, 0, kinds, lambda: pl.program_id(0) == 0, lambda: pl.program_id(0) == steps - 1)
    return _pc_comm(
        hosted, name="inproj_bwd_dx", grid=(steps,),
        in_specs=[blk(D_MODEL), blk(D_MODEL), _full((1, D_MODEL)), _full((IN_DIM, D_MODEL))] +
                 [blk(c1 - c0) for _, c0, c1 in GROUPS] + [ANY] * len(scattered),
        out_specs=[blk(D_MODEL), _full((8, D_MODEL))] + [ANY] * len(scattered),
        out_shape=[jax.ShapeDtypeStruct((t, D_MODEL), f32), jax.ShapeDtypeStruct((8, D_MODEL), f32)] +
                  [_exchange_out_shape("scatter", a) for a in scattered],
        scratch_shapes=_exchange_sems(len(scattered)), compiler_params=_cparams(("arbitrary",)),
    )(x, dout, norm_w, w_perm, *dgroups, *scattered)


def grad_w_group(u, dg, name, scattered=()):
    t, n = dg.shape
    tn = 512 if n % 512 == 0 else n
    tm = 4096 if t % 4096 == 0 else t
    nj, nk = n // tn, t // tm
    f32 = jnp.float32

    def body(u_ref, d_ref, o_ref, acc):
        k = pl.program_id(1)

        @pl.when(k == 0)
        def _():
            acc[...] = jnp.zeros_like(acc)

        acc[...] += lax.dot_general(d_ref[...].astype(_MM), u_ref[...], (((0,), (0,)), ((), ())),
                                    preferred_element_type=f32)

        @pl.when(k == nk - 1)
        def _():
            o_ref[...] = acc[...].astype(o_ref.dtype)

    ne = len(scattered)
    hosted = _hosting(body, 2, 1, 1, ["scatter"] * ne,
                      lambda: (pl.program_id(0) == 0) & (pl.program_id(1) == 0),
                      lambda: (pl.program_id(0) == nj - 1) & (pl.program_id(1) == nk - 1))
    res = (_pc_comm if ne else _pc)(
        hosted, name=name, grid=(nj, nk),
        in_specs=[pl.BlockSpec((tm, D_MODEL), lambda j, k: (k, 0)),
                  pl.BlockSpec((tm, tn), lambda j, k: (k, j))] + [ANY] * ne,
        out_specs=[pl.BlockSpec((tn, D_MODEL), lambda j, k: (j, 0))] + [ANY] * ne,
        out_shape=[jax.ShapeDtypeStruct((n, D_MODEL), _MM)] + [_exchange_out_shape("scatter", a) for a in scattered],
        scratch_shapes=[pltpu.VMEM((tn, D_MODEL), f32)] + _exchange_sems(ne),
        compiler_params=_cparams(("arbitrary", "arbitrary")),
    )(u, dg, *scattered)
    return res if ne else res[0]


def _pad_lanes(v, off):
    n = v.shape[-1]
    return jnp.pad(v.reshape(1, n).astype(jnp.float32), ((0, 0), (off, 128 - off - n)))


REF_ROWS = dict(z=(0, 1024), xbc=(1024, 2560), dt=(2560, 2576), gate=(2576, 3600), qkv=(3600, 6672), ab=(6672, 6688))


def unperm_w_in(gz, gxbc, ggate, gqkv, gsm):
    src = dict(z=gz, xbc=gxbc, dt=gsm[0:16], gate=ggate, qkv=gqkv, ab=gsm[16:32])
    slabs = []
    for k in range(N_DEV):
        a, b = k * W_IN_SHARD, (k + 1) * W_IN_SHARD
        parts = []
        for name, (s, e) in REF_ROWS.items():
            lo, hi = max(a, s), min(b, e)
            if lo < hi:
                parts.append(src[name][lo - s:hi - s])
        slabs.append(jnp.concatenate(parts, axis=0))
    return jnp.stack(slabs)


def all_gather(arrs, name):
    n = len(arrs)

    def body(*refs):
        ins, outs = refs[:n], refs[n:2 * n]
        send_sems, recv_sems, local_sems = refs[2 * n:]
        x, y, c, me = _me()
        sibling = (x, y, 1 - c)
        chips = [(1 - x, y), (x, 1 - y), (1 - x, 1 - y)]

        def idx(px, py, pc):
            return 4 * px + 2 * py + pc

        def copy(a, k, block, to, src=None):
            slot = outs[a].at[idx(*block)]
            return pltpu.make_async_remote_copy(src_ref=slot if src is None else src, dst_ref=slot,
                                                send_sem=send_sems.at[a, k], recv_sem=recv_sems.at[a, k],
                                                device_id=to, device_id_type=MESH)

        local = [pltpu.make_async_copy(ins[a], outs[a].at[me], local_sems.at[a]) for a in range(n)]
        for cp in local:
            cp.start()
        started = []
        for a in range(n):
            first = [copy(a, 0, (x, y, c), sibling, src=ins[a])]
            first += [copy(a, 1 + j, (x, y, c), (*chip, c), src=ins[a]) for j, chip in enumerate(chips)]
            for cp in first:
                cp.start()
            started += first
        for a in range(n):
            for j, chip in enumerate(chips):
                copy(a, 1 + j, (*chip, c), (x, y, c)).wait_recv()
                fwd = copy(a, 4 + j, (*chip, c), sibling)
                fwd.start()
                started.append(fwd)
        for a in range(n):
            copy(a, 0, sibling, (x, y, c)).wait_recv()
            for j, chip in enumerate(chips):
                copy(a, 4 + j, (*chip, 1 - c), (x, y, c)).wait_recv()
        for cp in started:
            cp.wait_send()
        for cp in local:
            cp.wait()

    return _pc_comm(
        body, name=name, in_specs=[ANY] * n, out_specs=[ANY] * n,
        out_shape=[jax.ShapeDtypeStruct((N_DEV,) + a.shape, a.dtype) for a in arrs],
        scratch_shapes=[pltpu.SemaphoreType.DMA((n, 7)), pltpu.SemaphoreType.DMA((n, 7)),
                        pltpu.SemaphoreType.DMA((n,))],
    )(*arrs)


def adamw_sum(recv, w, m, v, rows, name, cols=None):
    r, ccols = w.shape
    f32 = jnp.float32
    c1 = 1.0 / (1.0 - ADAM_B1 ** ADAM_STEP)
    c2 = 1.0 / (1.0 - ADAM_B2 ** ADAM_STEP)

    def body(recv_ref, w_ref, m_ref, v_ref, g_ref, d_ref, mo_ref, vo_ref):
        g = recv_ref[0].astype(f32)
        for k in range(1, N_DEV):
            g = g + recv_ref[k].astype(f32)
        mn = ADAM_B1 * m_ref[...] + (1.0 - ADAM_B1) * g
        vn = ADAM_B2 * v_ref[...] + (1.0 - ADAM_B2) * (g * g)
        g_ref[...] = g
        mo_ref[...] = mn
        vo_ref[...] = vn
        d_ref[...] = -ADAM_LR * ((mn * c1) / (jnp.sqrt(vn * c2) + ADAM_EPS) + ADAM_WD * w_ref[...])

    if cols is None:
        blk = pl.BlockSpec((rows, ccols), lambda i: (i, 0))
        rblk, steps = pl.BlockSpec((N_DEV, rows, ccols), lambda i: (0, i, 0)), r // rows
    else:
        blk = pl.BlockSpec((r, cols), lambda i: (0, i))
        rblk, steps = pl.BlockSpec((N_DEV, r, cols), lambda i: (0, 0, i)), ccols // cols
    return _pc(
        body, name=name, grid=(steps,),
        in_specs=[rblk, blk, blk, blk],
        out_specs=[blk] * 4, out_shape=[jax.ShapeDtypeStruct((r, ccols), f32)] * 4,
        compiler_params=_cparams(("arbitrary",)),
    )(recv, w, m, v)


SMALL = (("norm_w", 1, 1024, 0), ("ssd_conv_b", 1, 1536, 0), ("ssd_dt_bias", 1, 16, 0), ("ssd_a_log", 1, 16, 0),
         ("ssd_d", 1, 16, 0), ("ssd_norm_w", 1, 1024, 0), ("gdn_dt_bias", 1, 8, 16), ("gdn_a_log", 1, 8, 16),
         ("gdn_norm_w", 1, 128, 0), ("final_norm_w", 1, 1024, 0),
         ("ssd_conv_w", CONV_K, SSD_CONV_DIM // N_DEV, 0), ("gdn_conv_w", CONV_K, GDN_CONV_DIM // N_DEV, 0))


def _small_layout():
    out, off = [], 0
    for name, rows, n, lane0 in SMALL + (("loss", 1, 128, 0),):
        stride = -(-(lane0 + n) // 128) * 128
        out.append((name, rows, n, lane0, stride, off))
        off += rows * stride
    return out, off


def scatter_small(accs):
    layout, total = _small_layout()
    f32 = jnp.float32

    def body(*refs):
        acc_refs, out_ref, slabs = refs[:len(layout)], refs[len(layout)], refs[len(layout) + 1]
        sems = refs[len(layout) + 2:]
        slabs[...] = jnp.zeros_like(slabs)
        for (name, rows, n, lane0, stride, off), acc in zip(layout, acc_refs):
            for k in range(N_DEV):
                if rows == 1:
                    slabs[k, :, off:off + stride] = acc[0:1, 0:stride]
                else:
                    for j in range(rows):
                        slabs[k, :, off + stride * j:off + stride * j + n] = acc[j:j + 1, n * k:n * k + n]
        start, wait = _exchange_ops("scatter", slabs, out_ref, *sems)
        start()
        wait()

    return _pc_comm(
        body, name="scatter_small_grads", out_specs=ANY, out_shape=jax.ShapeDtypeStruct((N_DEV, 1, total), f32),
        scratch_shapes=[pltpu.VMEM((N_DEV, 1, total), f32)] + _exchange_sems(1),
    )(*accs)


def adamw_small(recv, w, m, v):
    layout, total = _small_layout()
    loss_off = layout[-1][5]
    layout = layout[:-1]
    f32 = jnp.float32
    c1 = 1.0 / (1.0 - ADAM_B1 ** ADAM_STEP)
    c2 = 1.0 / (1.0 - ADAM_B2 ** ADAM_STEP)
    np_ = len(layout)

    def body(*refs):
        recv_ref = refs[0]
        w_refs, m_refs, v_refs = refs[1:1 + np_], refs[1 + np_:1 + 2 * np_], refs[1 + 2 * np_:1 + 3 * np_]
        o_refs = refs[1 + 3 * np_:]
        g_all = recv_ref[0]
        for k in range(1, N_DEV):
            g_all = g_all + recv_ref[k]
        o_refs[4 * np_][...] = g_all[:, loss_off:loss_off + 128]

        def update(g, wv, mv, vv):
            mn = ADAM_B1 * mv + (1.0 - ADAM_B1) * g
            vn = ADAM_B2 * vv + (1.0 - ADAM_B2) * (g * g)
            return g, -ADAM_LR * ((mn * c1) / (jnp.sqrt(vn * c2) + ADAM_EPS) + ADAM_WD * wv), mn, vn

        for p, (name, rows, n, lane0, stride, off) in enumerate(layout):
            outs = o_refs[4 * p:4 * p + 4]
            if rows == 1:
                res = update(g_all[:, off + lane0:off + lane0 + n], w_refs[p][...], m_refs[p][...], v_refs[p][...])
                for o, r in zip(outs, res):
                    o[...] = r
            else:
                for j in range(rows):
                    res = update(g_all[:, off + stride * j:off + stride * j + n], w_refs[p][0, j:j + 1, :],
                                 m_refs[p][0, j:j + 1, :], v_refs[p][0, j:j + 1, :])
                    for o, r in zip(outs, res):
                        o[0, j:j + 1, :] = r

    names = [e[0] for e in layout]
    ins = [recv] + [d[nm] for d in (w, m, v) for nm in names]
    out_shape = [jax.ShapeDtypeStruct(w[nm].shape, f32) for nm in names for _ in range(4)]
    out_shape.append(jax.ShapeDtypeStruct((1, 128), f32))
    res = _pc(body, name="adamw_small", out_shape=out_shape)(*ins)
    return {nm: tuple(res[4 * p:4 * p + 4]) for p, nm in enumerate(names)}, res[4 * np_]


SHARD = (("ssd_conv_w", CONV_K * SSD_CONV_DIM // N_DEV), ("gdn_conv_w", CONV_K * GDN_CONV_DIM // N_DEV))
SHARD_ROWS = 24


def _rows_of(size):
    return -(-size // 128)


def _pack(vals, layout, total_rows):
    parts = []
    for (name, size), val in zip(layout, vals):
        flat = val.reshape(-1).astype(jnp.float32)
        parts.append(jnp.pad(flat, (0, _rows_of(size) * 128 - size)).reshape(-1, 128))
    used = sum(_rows_of(s) for _, s in layout)
    parts.append(jnp.zeros((total_rows - used, 128), jnp.float32))
    return jnp.concatenate(parts, axis=0)


def _conv_full(gathered_flat, ccols):
    return gathered_flat.reshape(N_DEV, CONV_K, ccols // N_DEV).transpose(1, 0, 2).reshape(CONV_K, ccols)


def kernel(x, norm_w, w_in, ssd_conv_w, ssd_conv_b, ssd_dt_bias, ssd_a_log, ssd_d, ssd_norm_w, gdn_conv_w, gdn_dt_bias, gdn_a_log, gdn_norm_w, w_out, final_norm_w, loss_target, m_norm_w, m_w_in, m_ssd_conv_w, m_ssd_conv_b, m_ssd_dt_bias, m_ssd_a_log, m_ssd_d, m_ssd_norm_w, m_gdn_conv_w, m_gdn_dt_bias, m_gdn_a_log, m_gdn_norm_w, m_w_out, m_final_norm_w, v_norm_w, v_w_in, v_ssd_conv_w, v_ssd_conv_b, v_ssd_dt_bias, v_ssd_a_log, v_ssd_d, v_ssd_norm_w, v_gdn_conv_w, v_gdn_dt_bias, v_gdn_a_log, v_gdn_norm_w, v_w_out, v_final_norm_w):
    f32 = jnp.float32
    w = dict(norm_w=norm_w, w_in=w_in, ssd_conv_w=ssd_conv_w, ssd_conv_b=ssd_conv_b, ssd_dt_bias=ssd_dt_bias,
             ssd_a_log=ssd_a_log, ssd_d=ssd_d, ssd_norm_w=ssd_norm_w, gdn_conv_w=gdn_conv_w, gdn_dt_bias=gdn_dt_bias,
             gdn_a_log=gdn_a_log, gdn_norm_w=gdn_norm_w, w_out=w_out, final_norm_w=final_norm_w)
    m = dict(norm_w=m_norm_w, w_in=m_w_in, ssd_conv_w=m_ssd_conv_w, ssd_conv_b=m_ssd_conv_b, ssd_dt_bias=m_ssd_dt_bias,
             ssd_a_log=m_ssd_a_log, ssd_d=m_ssd_d, ssd_norm_w=m_ssd_norm_w, gdn_conv_w=m_gdn_conv_w,
             gdn_dt_bias=m_gdn_dt_bias, gdn_a_log=m_gdn_a_log, gdn_norm_w=m_gdn_norm_w, w_out=m_w_out,
             final_norm_w=m_final_norm_w)
    v = dict(norm_w=v_norm_w, w_in=v_w_in, ssd_conv_w=v_ssd_conv_w, ssd_conv_b=v_ssd_conv_b, ssd_dt_bias=v_ssd_dt_bias,
             ssd_a_log=v_ssd_a_log, ssd_d=v_ssd_d, ssd_norm_w=v_ssd_norm_w, gdn_conv_w=v_gdn_conv_w,
             gdn_dt_bias=v_gdn_dt_bias, gdn_a_log=v_gdn_a_log, gdn_norm_w=v_gdn_norm_w, w_out=v_w_out,
             final_norm_w=v_final_norm_w)
    names = list(w)
    shapes = {n: w[n].shape for n in names}

    xl, tgt = x[0], loss_target[0]
    cs = _consts()
    dtb_s = _pad_lanes(ssd_dt_bias, 0)
    alog_s = _pad_lanes(ssd_a_log, 0)
    dpar = _pad_lanes(ssd_d, 0)
    dtb_g = _pad_lanes(gdn_dt_bias, 16)
    alog_g = _pad_lanes(gdn_a_log, 16)
    nw_g = gdn_norm_w.reshape(1, 128)
    nw_s = ssd_norm_w.reshape(1, 1024)
    cb_s = ssd_conv_b.reshape(1, 1536)
    nw1 = norm_w.reshape(1, D_MODEL)

    (g_w_in,) = all_gather([w_in[0].T.astype(_MM)], "gather_w_in")
    w_perm = g_w_in.reshape(IN_DIM, D_MODEL)
    conv_pack = _pack([w["ssd_conv_w"], w["gdn_conv_w"]], SHARD, SHARD_ROWS)
    u, z, xbc, gate, qkv, sm, g_w_out, g_conv = inproj_fwd(xl, nw1, w_perm, [w_out[0].astype(_MM), conv_pack])
    w_out_full = g_w_out.reshape(MIX_WIDTH, D_MODEL)
    ssd_cw = _conv_full(g_conv[:, 0:6].reshape(N_DEV, -1), SSD_CONV_DIM)
    gdn_cw = _conv_full(g_conv[:, 6:18].reshape(N_DEV, -1), GDN_CONV_DIM)

    nc = xl.shape[0] // CHUNK
    y_ssd, hs, pre_s, y_gdn, ss, ts, pre_g = _chunk_call(
        [ssd_fwd(z, xbc, sm, ssd_cw, cb_s, dtb_s, alog_s, dpar, nw_s, cs),
         gdn_fwd(gate, qkv, sm, gdn_cw, dtb_g, alog_g, nw_g, cs)], "scan_fwd", nc, False)
    dout, dys, dyg, g_wout, g_fnw, loss_l = out_fwd_bwd(xl, tgt, y_ssd, y_gdn, w_out_full,
                                                        final_norm_w.reshape(1, D_MODEL))
    (dz, dxbc, g_cw_s, g_cb_s, g_dtb_s, g_alog_s, g_d, g_nw_s,
     dgate, dqkv, dsm, g_cw_g, g_dtb_g, g_alog_g, g_nw_g) = _chunk_call(
        [ssd_bwd(z, xbc, pre_s, sm, hs, dys, ssd_cw, dtb_s, alog_s, dpar, nw_s, cs),
         gdn_bwd(gate, qkv, pre_g, sm, ss, ts, dyg, gdn_cw, dtb_g, alog_g, nw_g, cs)], "scan_bwd", nc, True)

    t_w_out = g_wout.reshape(N_DEV, MIX_WIDTH // N_DEV, D_MODEL)
    gws = {}
    for dg, (name, _, _) in zip((dz, dxbc, dgate, dsm), (GROUPS[0], GROUPS[1], GROUPS[2], GROUPS[4])):
        gws[name] = grad_w_group(u, dg, "grad_w_in_" + name)
    gws["qkv"], r_w_out = grad_w_group(u, dqkv, "grad_w_in_qkv", [t_w_out])
    t_w_in = unperm_w_in(gws["z"], gws["xbc"], gws["gate"], gws["qkv"], gws["sm"])
    dx, g_nw, r_w_in = inproj_bwd_dx(xl, dout, nw1, w_perm, (dz, dxbc, dgate, dqkv, dsm), [t_w_in])

    accs = dict(norm_w=g_nw, ssd_conv_b=g_cb_s, ssd_dt_bias=g_dtb_s, ssd_a_log=g_alog_s, ssd_d=g_d,
                ssd_norm_w=g_nw_s, gdn_dt_bias=g_dtb_g, gdn_a_log=g_alog_g, gdn_norm_w=g_nw_g, final_norm_w=g_fnw,
                ssd_conv_w=g_cw_s, gdn_conv_w=g_cw_g)
    r_small = scatter_small([accs[e[0]] for e in SMALL] + [loss_l])

    o_w_in = adamw_sum(r_w_in, w_in[0].T, m_w_in[0].T, v_w_in[0].T, None, "adamw_w_in", cols=256)
    o_w_out = adamw_sum(r_w_out, w_out[0], m_w_out[0], v_w_out[0], 64, "adamw_w_out")
    row = lambda d: {n: (a.reshape(1, -1) if a.ndim == 1 else a) for n, a in d.items()}
    o_small, loss_sum = adamw_small(r_small, row(w), row(m), row(v))

    loss = loss_sum[0, 0]
    outs = [loss, dx[None]]
    for k in range(4):
        parts = {n: o_small[n][k] for n in o_small}
        parts["w_in"] = o_w_in[k].T
        parts["w_out"] = o_w_out[k]
        outs += [parts[n].reshape(shapes[n]) for n in names]
    return tuple(outs)
```

```python
import functools

import jax
import jax.numpy as jnp
import numpy as np
from jax import lax
from jax.experimental import pallas as pl
from jax.experimental.pallas import tpu as pltpu

_MM = jnp.bfloat16

D_MODEL = 1024
CHUNK = 64
CONV_K = 4
EPS = 1e-6
SSD_CONV_DIM = 1536
GDN_HEADS = 8
GDN_DK = 128
GDN_CONV_DIM = 3072
MIX_WIDTH = 2048
IN_DIM = 6688
N_DEV = 8
W_IN_SHARD = IN_DIM // N_DEV
HI = lax.Precision.HIGHEST
HIGH = lax.Precision.HIGH
VMEM_LIMIT = 56 * 1024 * 1024

ADAM_LR = 0.001
ADAM_B1 = 0.9
ADAM_B2 = 0.999
ADAM_EPS = 1e-08
ADAM_WD = 0.01
ADAM_STEP = 10


def _pc(body, **kw):
    return pl.pallas_call(body, **kw)


def _pc_comm(body, **kw):
    return pl.pallas_call(body, **kw)


def _cparams(sem):
    return pltpu.CompilerParams(dimension_semantics=sem, vmem_limit_bytes=VMEM_LIMIT)


def _sig(x):
    return 0.5 * jnp.tanh(0.5 * x) + 0.5


@jax.custom_vjp
def _sigmoid(x):
    return _sig(x)


def _sigmoid_fwd(x):
    s = _sig(x)
    return s, s


def _sigmoid_bwd(s, g):
    return (g * s * (1.0 - s),)


_sigmoid.defvjp(_sigmoid_fwd, _sigmoid_bwd)


@jax.custom_vjp
def _silu(x):
    return x * _sig(x)


def _silu_fwd(x):
    s = _sig(x)
    return x * s, (x, s)


def _silu_bwd(res, g):
    x, s = res
    return (g * (s * (1.0 + x * (1.0 - s))),)


_silu.defvjp(_silu_fwd, _silu_bwd)


def _softplus_impl(x):
    return jnp.maximum(x, 0.0) + jnp.log(1.0 + jnp.exp(-jnp.abs(x)))


@jax.custom_vjp
def _softplus(x):
    return _softplus_impl(x)


def _softplus_fwd(x):
    return _softplus_impl(x), x


def _softplus_bwd(x, g):
    return (g * _sig(x),)


_softplus.defvjp(_softplus_fwd, _softplus_bwd)


def _lane_bcast_impl(x, k):
    return jnp.broadcast_to(x[..., k:k + 1], x.shape)


@functools.partial(jax.custom_vjp, nondiff_argnums=(1,))
def _lane_bcast(x, k):
    return _lane_bcast_impl(x, k)


def _lane_bcast_fwd(x, k):
    return _lane_bcast_impl(x, k), None


def _lane_bcast_bwd(k, _, g):
    lane = lax.broadcasted_iota(jnp.int32, g.shape, g.ndim - 1)
    return (jnp.where(lane == k, jnp.sum(g, axis=-1, keepdims=True), 0.0),)


_lane_bcast.defvjp(_lane_bcast_fwd, _lane_bcast_bwd)


def _mm(a, b):
    return jnp.dot(a.astype(_MM), b.astype(_MM), preferred_element_type=jnp.float32)


def _mm_nt(a, b):
    return lax.dot_general(a.astype(_MM), b.astype(_MM), (((1,), (1,)), ((), ())),
                           preferred_element_type=jnp.float32)


def _mm_tn(a, b):
    return lax.dot_general(a.astype(_MM), b.astype(_MM), (((0,), (0,)), ((), ())),
                           preferred_element_type=jnp.float32)


def _dot_hi(a, b):
    return jnp.dot(a, b, precision=HI, preferred_element_type=jnp.float32)


def _bmm(a, b):
    return lax.dot_general(a.astype(_MM), b.astype(_MM), (((2,), (1,)), ((0,), (0,))),
                           preferred_element_type=jnp.float32)


def _bmm_nt(a, b):
    return lax.dot_general(a.astype(_MM), b.astype(_MM), (((2,), (2,)), ((0,), (0,))),
                           preferred_element_type=jnp.float32)


def _bmm_tn(a, b):
    return lax.dot_general(a.astype(_MM), b.astype(_MM), (((1,), (1,)), ((0,), (0,))),
                           preferred_element_type=jnp.float32)


def _bmm_hi(a, b):
    return lax.dot_general(a, b, (((2,), (1,)), ((0,), (0,))), precision=HIGH, preferred_element_type=jnp.float32)


def _bmm_nt_hi(a, b):
    return lax.dot_general(a, b, (((2,), (2,)), ((0,), (0,))), precision=HIGH, preferred_element_type=jnp.float32)


def _bmm_tn_hi(a, b):
    return lax.dot_general(a, b, (((1,), (1,)), ((0,), (0,))), precision=HIGH, preferred_element_type=jnp.float32)


def _consts():
    l = np.arange(CHUNK)
    tri = (l[:, None] >= l[None, :]).astype(np.float32)
    lane = np.arange(128)
    i2 =(l[:, None] == (lane[None, :] % 64)).astype(np.float32)
    mask2 = (l[:, None] >= (lane[None, :] % 64)).astype(np.float32)
    lo = (lane < 64).astype(np.float32)[None, :]
    i64 = np.eye(CHUNK, dtype=np.float32)
    strict = (l[:, None] > l[None, :]).astype(np.float32)
    return dict(tri=jnp.asarray(tri), i2=jnp.asarray(i2), mask2=jnp.asarray(mask2), lo=jnp.asarray(lo),
                i64=jnp.asarray(i64), strict=jnp.asarray(strict))


def _ssd_chunk(xs_pre, b_pre, c_pre, z, sm, ht, dtb, alog, dpar, nw, tri, i2, mask2, lo):
    lane = lax.broadcasted_iota(jnp.int32, (1, 128), 1)
    m16 = lane < 16
    dt = jnp.where(m16, _softplus(sm + dtb), 0.0)
    a_neg = -jnp.exp(alog)
    cum = _dot_hi(tri, dt * a_neg)
    row = lax.broadcasted_iota(jnp.int32, (CHUNK, 1), 0)
    is_last = row == CHUNK - 1
    hi = 1.0 - lo
    bm = [_silu(b) for b in b_pre]
    cm = [_silu(c) for c in c_pre]
    cb2 = [_mm_nt(cm[g], jnp.concatenate([bm[g], bm[g]], axis=0)) for g in range(2)]
    yg, ht_next = [], []
    for j in range(8):
        g = j // 4
        pair = lambda v, j=j: jnp.where(lo > 0.5, _lane_bcast(v, 2 * j), _lane_bcast(v, 2 * j + 1))
        xs = _silu(xs_pre[j])
        dte = pair(dt)
        cume = pair(cum)
        cum_last = jnp.sum(jnp.where(is_last, cume, 0.0), axis=0, keepdims=True)
        xdt = xs * dte
        rowv = jnp.sum(cume * i2, axis=0, keepdims=True)
        lm = jnp.exp(jnp.where(mask2 > 0.5, cume - rowv, -jnp.inf))
        m = cb2[g] * lm
        xblk = jnp.concatenate([xdt * lo, xdt * hi], axis=0)
        y = _mm(m, xblk)
        y = y + _mm(cm[g], ht[j]) * jnp.exp(cume)
        y = y + pair(dpar) * xs
        yg.append(y * _silu(z[j]))
        st = _mm_tn(bm[g], xdt * jnp.exp(cum_last - cume))
        ht_next.append(ht[j] * jnp.exp(cum_last) + st)
    outs = []
    for g in range(2):
        ss = sum(jnp.sum(yg[j] * yg[j], axis=-1, keepdims=True) for j in range(4 * g, 4 * g + 4))
        rs = lax.rsqrt(ss * (1.0 / 512.0) + EPS)
        for j in range(4 * g, 4 * g + 4):
            outs.append(yg[j] * rs * nw[j])
    return outs, ht_next


def _tri_inverse(a):
    eye = jnp.eye(CHUNK, dtype=jnp.float32)[None]
    p = eye - a
    x = _bmm_hi(a, a)
    for _ in range(4):
        both = _bmm_hi(jnp.concatenate([p, x], axis=1), x)
        p = p + both[:, :CHUNK]
        x = both[:, CHUNK:]
    return p + _bmm_hi(p, x)


def _solve_apply(t, r1, r2):
    both = _bmm_hi(t, jnp.concatenate([r1, r2], axis=-1))
    n = r1.shape[-1]
    return both[..., :n], both[..., n:]


@jax.custom_vjp
def _solve(a, r1, r2, t):
    return _solve_apply(t, r1, r2)


def _solve_fwd(a, r1, r2, t):
    u, w = _bmm_hi(t, r1), _bmm_hi(t, r2)
    return (u, w), (t, u, w)


def _solve_bwd(res, cts):
    t, u, w = res
    du, dw = cts
    dr1 = _bmm_tn_hi(t, du)
    dr2 = _bmm_tn_hi(t, dw)
    da = -(_bmm_nt_hi(dr1, u) + _bmm_nt_hi(dr2, w))
    return da, dr1, dr2, jnp.zeros_like(t)


_solve.defvjp(_solve_fwd, _solve_bwd)


def _gdn_chunk(q_pre, k_pre, v_pre, gate, sm, s, dtb, alog, nw, tri, i64, strict, t_in=None):
    lane = lax.broadcasted_iota(jnp.int32, (1, 128), 1)
    m_a = (lane >= 16) & (lane < 24)
    g_full = jnp.where(m_a, -jnp.exp(alog) * _softplus(sm + dtb), 0.0)
    gc = _dot_hi(tri, g_full)
    sig = _sigmoid(sm)
    heads = lambda f: jnp.concatenate([f(h)[None] for h in range(GDN_HEADS)], axis=0)
    gc3 = heads(lambda h: _lane_bcast(gc, 16 + h))
    beta3 = heads(lambda h: _lane_bcast(sig, 24 + h))
    q = _silu(q_pre)
    q = q * lax.rsqrt(jnp.sum(q * q, axis=-1, keepdims=True) + EPS) * (GDN_DK ** -0.5)
    k = _silu(k_pre)
    k = k * lax.rsqrt(jnp.sum(k * k, axis=-1, keepdims=True) + EPS)
    v = _silu(v_pre)
    gcl = gc3[:, :, :CHUNK]
    gc_row = jnp.sum(gcl * i64[None], axis=1, keepdims=True)
    incl = (strict + i64)[None] > 0.5
    decay = jnp.exp(jnp.where(incl, gcl - gc_row, -jnp.inf))
    kb = k * beta3
    a = jnp.where(strict[None] > 0.5, _bmm_nt(kb, k) * decay, 0.0)
    egc = jnp.exp(gc3)
    t = _tri_inverse(a) if t_in is None else t_in
    u, w = _solve(a, v * beta3, kb * egc, t)
    attn = _bmm_nt(q, k) * decay
    row = lax.broadcasted_iota(jnp.int32, (1, CHUNK, 1), 1)
    gl = jnp.sum(jnp.where(row == CHUNK - 1, gc3, 0.0), axis=1, keepdims=True)
    q_dec = q * egc
    k_dec = k * jnp.exp(gl - gc3)
    v_new = u - _bmm(w, s)
    o = _bmm(q_dec, s) + _bmm(attn, v_new)
    s_next = s * jnp.exp(gl) + _bmm_tn(k_dec, v_new)
    on = o * lax.rsqrt(jnp.mean(o * o, axis=-1, keepdims=True) + EPS) * nw
    return on * _silu(gate), s_next, t


def _conv_fwd(pbuf, w_ref, c0, c1):
    blk = pbuf[:, c0:c1]
    acc = w_ref[CONV_K - 1:CONV_K, c0:c1] * blk[8:72]
    for j in range(CONV_K - 1):
        acc = acc + w_ref[j:j + 1, c0:c1] * pltpu.roll(blk, CONV_K - 1 - j, axis=0)[8:72]
    return acc


MESH = pl.DeviceIdType.MESH
ANY = pl.BlockSpec(memory_space=pl.ANY)


def _me():
    x, y, c = lax.axis_index("x"), lax.axis_index("y"), lax.axis_index("c")
    return x, y, c, 4 * x + 2 * y + c


def _peer(r):
    x, y, c, _ = _me()
    px = 1 - x if r & 4 else x
    py = 1 - y if r & 2 else y
    pc = 1 - c if r & 1 else c
    return (px, py, pc), 4 * px + 2 * py + pc


def _exchange_ops(kind, in_ref, out_ref, send_sems, recv_sems, local_sem):
    me = _me()[3]
    local = pltpu.make_async_copy(in_ref.at[me] if kind == "scatter" else in_ref, out_ref.at[me], local_sem)
    sends, recvs = [], []
    for r in range(1, N_DEV):
        peer, pidx = _peer(r)
        src = in_ref.at[pidx] if kind == "scatter" else in_ref
        sems = dict(send_sem=send_sems.at[r - 1], recv_sem=recv_sems.at[r - 1], device_id=peer, device_id_type=MESH)
        sends.append(pltpu.make_async_remote_copy(src_ref=src, dst_ref=out_ref.at[me], **sems))
        recvs.append(pltpu.make_async_remote_copy(src_ref=src, dst_ref=out_ref.at[pidx], **sems))

    def start():
        local.start()
        for cp in sends:
            cp.start()

    def wait():
        for cp in recvs:
            cp.wait_recv()
        for cp in sends:
            cp.wait_send()
        local.wait()

    return start, wait


def _exchange_sems(n):
    return [pltpu.SemaphoreType.DMA((N_DEV - 1,)), pltpu.SemaphoreType.DMA((N_DEV - 1,)),
            pltpu.SemaphoreType.DMA(())] * n


def _exchange_out_shape(kind, a):
    return jax.ShapeDtypeStruct(a.shape if kind == "scatter" else (N_DEV,) + a.shape, a.dtype)


def _hosting(body, n_in, n_out, n_scratch, kinds, first, last):
    ne = len(kinds)

    def wrapped(*refs):
        ins, ex_in = refs[:n_in], refs[n_in:n_in + ne]
        o0 = n_in + ne
        outs, ex_out = refs[o0:o0 + n_out], refs[o0 + n_out:o0 + n_out + ne]
        s0 = o0 + n_out + ne
        scr, sems = refs[s0:s0 + n_scratch], refs[s0 + n_scratch:]
        ops = [_exchange_ops(kinds[e], ex_in[e], ex_out[e], *sems[3 * e:3 * e + 3]) for e in range(ne)]

        @pl.when(first())
        def _():
            for start, _ in ops:
                start()

        body(*ins, *outs, *scr)

        @pl.when(last())
        def _():
            for _, wait in ops:
                wait()

    return wrapped


GROUPS = (("z", 0, 1024), ("xbc", 1024, 2560), ("gate", 2560, 3584), ("qkv", 3584, 6656), ("sm", 6656, 6784))
GROUP_ROWS = dict(z=((0, 1024),), xbc=((1024, 2560),), gate=((2576, 3600),), qkv=((3600, 6672),),
                  sm=((2560, 2576), (6672, 6688)))


def _w_rows(w_ref, name, width):
    pieces = [w_ref[a:b, :] for a, b in GROUP_ROWS[name]]
    n = sum(b - a for a, b in GROUP_ROWS[name])
    if n < width:
        pieces.append(jnp.zeros((width - n, D_MODEL), w_ref.dtype))
    return pieces[0] if len(pieces) == 1 else jnp.concatenate(pieces, axis=0)


def inproj_fwd(x, norm_w, w_perm, gathered):
    t = x.shape[0]
    tm = min(512, t)
    steps = t // tm
    kinds = ["gather"] * len(gathered)

    def body(x_ref, nw_ref, w_ref, u_ref, z_ref, xbc_ref, gate_ref, qkv_ref, sm_ref):
        xf = x_ref[...]
        rstd = lax.rsqrt(jnp.mean(xf * xf, axis=-1, keepdims=True) + EPS)
        u = (xf * rstd * nw_ref[...]).astype(_MM)
        u_ref[...] = u
        for (name, c0, c1), o_ref in zip(GROUPS, (z_ref, xbc_ref, gate_ref, qkv_ref, sm_ref)):
            o_ref[...] = lax.dot_general(u, _w_rows(w_ref, name, c1 - c0), (((1,), (1,)), ((), ())),
                                         preferred_element_type=jnp.float32)

    outs = [jax.ShapeDtypeStruct((t, D_MODEL), _MM)] + [jax.ShapeDtypeStruct((t, c1 - c0), jnp.float32)
                                                        for _, c0, c1 in GROUPS]
    hosted = _hosting(body, 3, 6, 0, kinds, lambda: pl.program_id(0) == 0, lambda: pl.program_id(0) == steps - 1)
    return _pc_comm(
        hosted, name="inproj_fwd", grid=(steps,),
        in_specs=[pl.BlockSpec((tm, D_MODEL), lambda i: (i, 0)),
                  pl.BlockSpec((1, D_MODEL), lambda i: (0, 0)),
                  pl.BlockSpec((IN_DIM, D_MODEL), lambda i: (0, 0), pipeline_mode=pl.Buffered(1))] +
                 [ANY] * len(gathered),
        out_specs=[pl.BlockSpec((tm, D_MODEL), lambda i: (i, 0))] +
                  [pl.BlockSpec((tm, c1 - c0), lambda i: (i, 0)) for _, c0, c1 in GROUPS] + [ANY] * len(gathered),
        out_shape=outs + [_exchange_out_shape("gather", a) for a in gathered],
        scratch_shapes=_exchange_sems(len(gathered)), compiler_params=_cparams(("arbitrary",)),
    )(x, norm_w, w_perm, *gathered)


SUB_FWD = 4
SUB_BWD = 2


def _halo_spec(width, idx_fn):
    return pl.BlockSpec((8, width), lambda i: (jnp.maximum(idx_fn(i) * (SUB_FWD * CHUNK // 8) - 1, 0), 0))


def _when_first(shared, fn):
    if shared["first"] is not False:
        pl.when(shared["first"])(fn)


def _full(shape):
    nd = len(shape)
    return pl.BlockSpec(shape, lambda i: (0,) * nd)


def _ssd_split(pre_fn, z_ref, sm_ref):
    xs_pre = [pre_fn(128 * j, 128 * j + 128) for j in range(8)]
    b_pre = [pre_fn(1024 + 128 * g, 1152 + 128 * g) for g in range(2)]
    c_pre = [pre_fn(1280 + 128 * g, 1408 + 128 * g) for g in range(2)]
    z = [z_ref[:, 128 * j:128 * j + 128] for j in range(8)]
    return xs_pre, b_pre, c_pre, z, sm_ref[...]


def ssd_fwd(z, xbc, sm, conv_w, conv_b, dtb, alog, dpar, nw, cs):
    t = z.shape[0]
    nc = t // CHUNK

    def body(shared, z_ref, xbc_ref, halo_ref, sm_ref, cw_ref, cb_ref, dtb_ref, alog_ref, dpar_ref, nw_ref,
             tri_ref, i2_ref, mask2_ref, lo_ref, y_ref, hs_ref, pre_ref, pbuf, ht_scr):
        def init():
            ht_scr[...] = jnp.zeros_like(ht_scr)

        _when_first(shared, init)
        pbuf[0:8, :] = jnp.where(shared["first"], 0.0, halo_ref[...])
        pbuf[8:72, :] = xbc_ref[...]

        def pre_fn(c0, c1):
            pre = _conv_fwd(pbuf, cw_ref, c0, c1) + cb_ref[:, c0:c1]
            pre_ref[:, c0:c1] = pre
            return pre

        xs_pre, b_pre, c_pre, zz, smv = _ssd_split(pre_fn, z_ref, sm_ref)
        ht = [ht_scr[:, 128 * j:128 * j + 128] for j in range(8)]
        hs_ref[0] = ht_scr[...]
        nwl = [nw_ref[:, 128 * j:128 * j + 128] for j in range(8)]
        outs, ht_next = _ssd_chunk(xs_pre, b_pre, c_pre, zz, smv, ht, dtb_ref[...], alog_ref[...], dpar_ref[...],
                                   nwl, tri_ref[...], i2_ref[...], mask2_ref[...], lo_ref[...])
        for j in range(8):
            y_ref[:, 128 * j:128 * j + 128] = outs[j].astype(y_ref.dtype)
            ht_scr[:, 128 * j:128 * j + 128] = ht_next[j]

    blk = lambda w: pl.BlockSpec((SUB_FWD * CHUNK, w), lambda i: (i, 0))
    return dict(
        body=body,
        in_kinds=["rows", "rows", ("halo", 1), "rows"] + ["full"] * 10, out_kinds=["rows", "state", "rows"],
        in_specs=[blk(1024), blk(1536), _halo_spec(1536, lambda i: i), blk(128),
                  _full((CONV_K, 1536)), _full((1, 1536)), _full((1, 128)), _full((1, 128)), _full((1, 128)),
                  _full((1, 1024)), _full((64, 64)), _full((64, 128)), _full((64, 128)),
                  _full((1, 128))],
        out_specs=[blk(1024), pl.BlockSpec((SUB_FWD, 128, 1024), lambda i: (i, 0, 0)), blk(1536)],
        out_shape=[jax.ShapeDtypeStruct((t, 1024), _MM), jax.ShapeDtypeStruct((nc, 128, 1024), jnp.float32),
                   jax.ShapeDtypeStruct((t, 1536), jnp.float32)],
        scratch=[pltpu.VMEM((72, 1536), jnp.float32), pltpu.VMEM((128, 1024), jnp.float32)],
        args=[z, xbc, xbc, sm, conv_w, conv_b, dtb, alog, dpar, nw, cs["tri"], cs["i2"], cs["mask2"], cs["lo"]])


def _conv_bwd(dpre_list, col_ranges, dbuf, carry, x_ref, cw_ref, dx_ref, dcw_ref, dcb_ref, first):
    for dpre, (c0, c1) in zip(dpre_list, col_ranges):
        dbuf[0:64, c0:c1] = dpre
    dbuf[64:72, :] = jnp.where(first, 0.0, carry[...])
    carry[...] = dbuf[0:8, :]
    for (c0, c1) in col_ranges:
        xin = x_ref[:, c0:c1]
        blk = dbuf[:, c0:c1]
        acc = None
        for j in range(CONV_K):
            sh = blk[0:64] if j == CONV_K - 1 else pltpu.roll(blk, 72 - (CONV_K - 1 - j), axis=0)[0:64]
            term = cw_ref[j:j + 1, c0:c1] * sh
            acc = term if acc is None else acc + term
            dcw_ref[j:j + 1, c0:c1] += jnp.sum(xin * sh, axis=0, keepdims=True)
        dx_ref[:, c0:c1] = acc.astype(dx_ref.dtype)
        if dcb_ref is not None:
            dcb_ref[0:1, c0:c1] += jnp.sum(dbuf[0:64, c0:c1], axis=0, keepdims=True)


def ssd_bwd(z, xbc, pre, sm, hs, dy, conv_w, dtb, alog, dpar, nw, cs):
    t = z.shape[0]
    nc = t // CHUNK

    def body(shared, z_ref, xbc_ref, pre_ref, sm_ref, hs_ref, dy_ref, cw_ref, dtb_ref, alog_ref, dpar_ref, nw_ref,
             tri_ref, i2_ref, mask2_ref, lo_ref,
             dz_ref, dxbc_ref, dcw_ref, dcb_ref, ddtb_ref, dalog_ref, ddpar_ref, dnw_ref,
             dbuf, carry, dht_scr):
        def init():
            dht_scr[...] = jnp.zeros_like(dht_scr)
            dcw_ref[...] = jnp.zeros_like(dcw_ref)
            dcb_ref[...] = jnp.zeros_like(dcb_ref)
            ddtb_ref[...] = jnp.zeros_like(ddtb_ref)
            dalog_ref[...] = jnp.zeros_like(dalog_ref)
            ddpar_ref[...] = jnp.zeros_like(ddpar_ref)
            dnw_ref[...] = jnp.zeros_like(dnw_ref)

        _when_first(shared, init)
        pre_fn = lambda c0, c1: pre_ref[:, c0:c1]
        xs_pre, b_pre, c_pre, zz, smv = _ssd_split(pre_fn, z_ref, sm_ref)
        ht = [hs_ref[0, :, 128 * j:128 * j + 128] for j in range(8)]
        nwl = [nw_ref[:, 128 * j:128 * j + 128] for j in range(8)]
        consts = (tri_ref[...], i2_ref[...], mask2_ref[...], lo_ref[...])

        def f(xs_pre, b_pre, c_pre, zz, smv, ht, dtb, alog, dpar, nwl):
            return _ssd_chunk(xs_pre, b_pre, c_pre, zz, smv, ht, dtb, alog, dpar, nwl, *consts)

        _, vjp = jax.vjp(f, xs_pre, b_pre, c_pre, zz, smv, ht, dtb_ref[...], alog_ref[...], dpar_ref[...], nwl)
        dys = [dy_ref[:, 128 * j:128 * j + 128] for j in range(8)]
        dhts = [dht_scr[:, 128 * j:128 * j + 128] for j in range(8)]
        dxs, db, dc, dzz, dsm, dht, ddtb, dalog, ddpar, dnwl = vjp((dys, dhts))
        for j in range(8):
            dz_ref[:, 128 * j:128 * j + 128] = dzz[j].astype(dz_ref.dtype)
            dht_scr[:, 128 * j:128 * j + 128] = dht[j]
            dnw_ref[0:1, 128 * j:128 * j + 128] += dnwl[j]
        shared["dsm_ssd"] = dsm
        ddtb_ref[0:1, :] += ddtb
        dalog_ref[0:1, :] += dalog
        ddpar_ref[0:1, :] += ddpar
        ranges = ([(128 * j, 128 * j + 128) for j in range(8)] + [(1024 + 128 * g, 1152 + 128 * g) for g in range(2)]
                  + [(1280 + 128 * g, 1408 + 128 * g) for g in range(2)])
        _conv_bwd(dxs + db + dc, ranges, dbuf, carry, xbc_ref, cw_ref, dxbc_ref, dcw_ref, dcb_ref, shared["first"])

    ns = nc // SUB_BWD
    rblk = lambda w: pl.BlockSpec((SUB_BWD * CHUNK, w), lambda i: (ns - 1 - i, 0))
    acc = lambda w: pl.BlockSpec((8, w), lambda i: (0, 0))
    f32 = jnp.float32
    return dict(
        body=body,
        in_kinds=["rows"] * 4 + ["state", "rows"] + ["full"] * 9, out_kinds=["rows", "rows"] + ["full"] * 6,
        in_specs=[rblk(1024), rblk(1536), rblk(1536), rblk(128),
                  pl.BlockSpec((SUB_BWD, 128, 1024), lambda i: (ns - 1 - i, 0, 0)), rblk(1024),
                  _full((CONV_K, 1536)), _full((1, 128)), _full((1, 128)), _full((1, 128)),
                  _full((1, 1024)), _full((64, 64)), _full((64, 128)), _full((64, 128)),
                  _full((1, 128))],
        out_specs=[rblk(1024), rblk(1536), acc(1536), acc(1536), acc(128), acc(128), acc(128), acc(1024)],
        out_shape=[jax.ShapeDtypeStruct((t, 1024), f32), jax.ShapeDtypeStruct((t, 1536), f32),
                   jax.ShapeDtypeStruct((8, 1536), f32),
                   jax.ShapeDtypeStruct((8, 1536), f32), jax.ShapeDtypeStruct((8, 128), f32),
                   jax.ShapeDtypeStruct((8, 128), f32), jax.ShapeDtypeStruct((8, 128), f32),
                   jax.ShapeDtypeStruct((8, 1024), f32)],
        scratch=[pltpu.VMEM((72, 1536), f32), pltpu.VMEM((8, 1536), f32), pltpu.VMEM((128, 1024), f32)],
        args=[z, xbc, pre, sm, hs, dy, conv_w, dtb, alog, dpar, nw, cs["tri"], cs["i2"], cs["mask2"], cs["lo"]])


def _gdn_split(pre_fn, gate_ref):
    def heads(base):
        return jnp.stack([pre_fn(base + 128 * h, base + 128 * h + 128) for h in range(GDN_HEADS)])
    gate = jnp.stack([gate_ref[:, 128 * h:128 * h + 128] for h in range(GDN_HEADS)])
    return heads(0), heads(1024), heads(2048), gate


def gdn_fwd(gate, qkv, sm, conv_w, dtb, alog, nw, cs):
    t = gate.shape[0]
    nc = t // CHUNK

    def body(shared, gate_ref, qkv_ref, halo_ref, sm_ref, cw_ref, dtb_ref, alog_ref, nw_ref,
             tri_ref, i64_ref, strict_ref, o_ref, ss_ref, ts_ref, pre_ref, pbuf, s_scr):
        def init():
            s_scr[...] = jnp.zeros_like(s_scr)

        _when_first(shared, init)
        pbuf[0:8, :] = jnp.where(shared["first"], 0.0, halo_ref[...])
        pbuf[8:72, :] = qkv_ref[...]

        def pre_fn(c0, c1):
            pre = _conv_fwd(pbuf, cw_ref, c0, c1)
            pre_ref[:, c0:c1] = pre
            return pre

        q_pre, k_pre, v_pre, g3 = _gdn_split(pre_fn, gate_ref)
        s = s_scr[...]
        ss_ref[0] = s
        out, s_next, tinv = _gdn_chunk(q_pre, k_pre, v_pre, g3, sm_ref[...], s, dtb_ref[...], alog_ref[...],
                                       nw_ref[...], tri_ref[...], i64_ref[...], strict_ref[...])
        ts_ref[0] = tinv
        s_scr[...] = s_next
        for h in range(GDN_HEADS):
            o_ref[:, 128 * h:128 * h + 128] = out[h].astype(o_ref.dtype)

    blk = lambda w: pl.BlockSpec((SUB_FWD * CHUNK, w), lambda i: (i, 0))
    return dict(
        body=body,
        in_kinds=["rows", "rows", ("halo", 1), "rows"] + ["full"] * 7, out_kinds=["rows", "state", "state", "rows"],
        in_specs=[blk(1024), blk(3072), _halo_spec(3072, lambda i: i), blk(128),
                  _full((CONV_K, 3072)), _full((1, 128)), _full((1, 128)), _full((1, 128)),
                  _full((64, 64)), _full((64, 64)), _full((64, 64))],
        out_specs=[blk(1024), pl.BlockSpec((SUB_FWD, 8, 128, 128), lambda i: (i, 0, 0, 0)),
                   pl.BlockSpec((SUB_FWD, 8, CHUNK, CHUNK), lambda i: (i, 0, 0, 0)), blk(3072)],
        out_shape=[jax.ShapeDtypeStruct((t, 1024), _MM), jax.ShapeDtypeStruct((nc, 8, 128, 128), jnp.float32),
                   jax.ShapeDtypeStruct((nc, 8, CHUNK, CHUNK), jnp.float32),
                   jax.ShapeDtypeStruct((t, 3072), jnp.float32)],
        scratch=[pltpu.VMEM((72, 3072), jnp.float32), pltpu.VMEM((8, 128, 128), jnp.float32)],
        args=[gate, qkv, qkv, sm, conv_w, dtb, alog, nw, cs["tri"], cs["i64"], cs["strict"]])


def gdn_bwd(gate, qkv, pre, sm, ss, ts, do, conv_w, dtb, alog, nw, cs):
    t = gate.shape[0]
    nc = t // CHUNK

    def body(shared, gate_ref, qkv_ref, pre_ref, sm_ref, ss_ref, ts_ref, do_ref, cw_ref, dtb_ref, alog_ref,
             nw_ref, tri_ref, i64_ref, strict_ref,
             dgate_ref, dqkv_ref, dsm_ref, dcw_ref, ddtb_ref, dalog_ref, dnw_ref,
             dbuf, carry, ds_scr):
        def init():
            ds_scr[...] = jnp.zeros_like(ds_scr)
            dcw_ref[...] = jnp.zeros_like(dcw_ref)
            ddtb_ref[...] = jnp.zeros_like(ddtb_ref)
            dalog_ref[...] = jnp.zeros_like(dalog_ref)
            dnw_ref[...] = jnp.zeros_like(dnw_ref)

        _when_first(shared, init)

        q_pre, k_pre, v_pre, g3 = _gdn_split(lambda c0, c1: pre_ref[:, c0:c1], gate_ref)
        consts = (tri_ref[...], i64_ref[...], strict_ref[...], ts_ref[0])

        def f(q_pre, k_pre, v_pre, g3, smv, s, dtb, alog, nwv):
            return _gdn_chunk(q_pre, k_pre, v_pre, g3, smv, s, dtb, alog, nwv, *consts)[:2]

        _, vjp = jax.vjp(f, q_pre, k_pre, v_pre, g3, sm_ref[...], ss_ref[0], dtb_ref[...], alog_ref[...], nw_ref[...])
        do3 = jnp.stack([do_ref[:, 128 * h:128 * h + 128] for h in range(GDN_HEADS)])
        dq, dk, dv, dg3, dsm, ds, ddtb, dalog, dnw = vjp((do3, ds_scr[...]))
        ds_scr[...] = ds
        for h in range(GDN_HEADS):
            dgate_ref[:, 128 * h:128 * h + 128] = dg3[h].astype(dgate_ref.dtype)
        dsm_ref[...] = (dsm + shared["dsm_ssd"]).astype(dsm_ref.dtype)
        ddtb_ref[0:1, :] += ddtb
        dalog_ref[0:1, :] += dalog
        dnw_ref[0:1, :] += dnw
        ranges = [(base + 128 * h, base + 128 * h + 128) for base in (0, 1024, 2048) for h in range(GDN_HEADS)]
        dlist = [d[h] for d in (dq, dk, dv) for h in range(GDN_HEADS)]
        _conv_bwd(dlist, ranges, dbuf, carry, qkv_ref, cw_ref, dqkv_ref, dcw_ref, None, shared["first"])

    ns = nc // SUB_BWD
    rblk = lambda w: pl.BlockSpec((SUB_BWD * CHUNK, w), lambda i: (ns - 1 - i, 0))
    acc = lambda w: pl.BlockSpec((8, w), lambda i: (0, 0))
    f32 = jnp.float32
    return dict(
        body=body,
        in_kinds=["rows"] * 4 + ["state", "state", "rows"] + ["full"] * 7, out_kinds=["rows"] * 3 + ["full"] * 4,
        in_specs=[rblk(1024), rblk(3072), rblk(3072), rblk(128),
                  pl.BlockSpec((SUB_BWD, 8, 128, 128), lambda i: (ns - 1 - i, 0, 0, 0)),
                  pl.BlockSpec((SUB_BWD, 8, CHUNK, CHUNK), lambda i: (ns - 1 - i, 0, 0, 0)), rblk(1024),
                  _full((CONV_K, 3072)), _full((1, 128)), _full((1, 128)), _full((1, 128)),
                  _full((64, 64)), _full((64, 64)), _full((64, 64))],
        out_specs=[rblk(1024), rblk(3072), rblk(128), acc(3072), acc(128), acc(128), acc(128)],
        out_shape=[jax.ShapeDtypeStruct((t, 1024), f32), jax.ShapeDtypeStruct((t, 3072), f32),
                   jax.ShapeDtypeStruct((t, 128), f32), jax.ShapeDtypeStruct((8, 3072), f32),
                   jax.ShapeDtypeStruct((8, 128), f32), jax.ShapeDtypeStruct((8, 128), f32),
                   jax.ShapeDtypeStruct((8, 128), f32)],
        scratch=[pltpu.VMEM((72, 3072), f32), pltpu.VMEM((8, 3072), f32), pltpu.VMEM((8, 128, 128), f32)],
        args=[gate, qkv, pre, sm, ss, ts, do, conv_w, dtb, alog, nw, cs["tri"], cs["i64"], cs["strict"]])


def _chunk_call(parts, name, nc, reverse):
    n_in = [len(p["args"]) for p in parts]
    n_out = [len(p["out_shape"]) for p in parts]
    n_scr = [len(p["scratch"]) for p in parts]
    sub = SUB_BWD if reverse else SUB_FWD
    order = list(range(sub))[::-1] if reverse else list(range(sub))

    def view(ref, kind, s, refs):
        if kind == "rows":
            return ref.at[pl.ds(CHUNK * s, CHUNK)]
        if kind == "state":
            return ref.at[pl.ds(s, 1)]
        if kind == "full":
            return ref
        src = refs[kind[1]]
        return ref if s == 0 else src.at[pl.ds(CHUNK * s - 8, 8)]

    def body(*refs):
        ins, outs, scr = refs[:sum(n_in)], refs[sum(n_in):sum(n_in) + sum(n_out)], refs[sum(n_in) + sum(n_out):]
        for s in order:
            shared = {"first": (pl.program_id(0) == 0) if s == order[0] else False}
            for k, p in enumerate(parts):
                i0, o0, s0 = sum(n_in[:k]), sum(n_out[:k]), sum(n_scr[:k])
                p_ins = ins[i0:i0 + n_in[k]]
                p["body"](shared,
                          *[view(r, kd, s, p_ins) for r, kd in zip(p_ins, p["in_kinds"])],
                          *[view(r, kd, s, None) for r, kd in zip(outs[o0:o0 + n_out[k]], p["out_kinds"])],
                          *scr[s0:s0 + n_scr[k]])

    cat = lambda key: [v for p in parts for v in p[key]]
    return _pc(body, name=name, grid=(nc // sub,), in_specs=cat("in_specs"), out_specs=cat("out_specs"),
               out_shape=cat("out_shape"), scratch_shapes=cat("scratch"),
               compiler_params=_cparams(("arbitrary",)))(*cat("args"))


def out_fwd_bwd(x, tgt, y_ssd, y_gdn, w_out, fnw):
    t = x.shape[0]
    tm = min(512, t)
    f32 = jnp.float32

    def body(x_ref, tgt_ref, ys_ref, yg_ref, w_ref, fnw_ref,
             dout_ref, dys_ref, dyg_ref, gw_ref, gfnw_ref, loss_ref, gw_acc):
        i = pl.program_id(0)

        @pl.when(i == 0)
        def _():
            gw_acc[...] = jnp.zeros_like(gw_acc)
            gfnw_ref[...] = jnp.zeros_like(gfnw_ref)
            loss_ref[...] = jnp.zeros_like(loss_ref)

        ys = ys_ref[...]
        yg = yg_ref[...]
        out = x_ref[...] + jnp.dot(ys, w_ref[0:1024, :], preferred_element_type=f32) \
            + jnp.dot(yg, w_ref[1024:2048, :], preferred_element_type=f32)
        rstd = lax.rsqrt(jnp.mean(out * out, axis=-1, keepdims=True) + EPS)
        yhat = out * rstd
        fw = fnw_ref[...]
        e = yhat * fw - tgt_ref[...]
        loss_ref[...] += 0.5 * jnp.sum(jnp.sum(e * e, axis=-1, keepdims=True) * (1.0 / D_MODEL), axis=0, keepdims=True)
        dyf = e * (1.0 / D_MODEL)
        gfnw_ref[0:1, :] += jnp.sum(dyf * yhat, axis=0, keepdims=True)
        dyhat = dyf * fw
        dout = rstd * (dyhat - yhat * jnp.mean(dyhat * yhat, axis=-1, keepdims=True))
        dout_ref[...] = dout
        db = dout.astype(_MM)
        dys_ref[...] = lax.dot_general(db, w_ref[0:1024, :], (((1,), (1,)), ((), ())), preferred_element_type=f32)
        dyg_ref[...] = lax.dot_general(db, w_ref[1024:2048, :], (((1,), (1,)), ((), ())), preferred_element_type=f32)
        gw_acc[0:1024, :] += lax.dot_general(ys, db, (((0,), (0,)), ((), ())), preferred_element_type=f32)
        gw_acc[1024:2048, :] += lax.dot_general(yg, db, (((0,), (0,)), ((), ())), preferred_element_type=f32)

        @pl.when(i == steps - 1)
        def _():
            gw_ref[...] = gw_acc[...].astype(gw_ref.dtype)

    steps = t // tm
    blk = pl.BlockSpec((tm, D_MODEL), lambda i: (i, 0))
    return _pc(
        body, name="out_fwd_bwd", grid=(steps,),
        in_specs=[blk, blk, blk, blk, _full((MIX_WIDTH, D_MODEL)), _full((1, D_MODEL))],
        out_specs=[blk, blk, blk, _full((MIX_WIDTH, D_MODEL)), _full((8, D_MODEL)), _full((1, 128))],
        out_shape=[jax.ShapeDtypeStruct((t, D_MODEL), f32)] * 3 +
                  [jax.ShapeDtypeStruct((MIX_WIDTH, D_MODEL), _MM), jax.ShapeDtypeStruct((8, D_MODEL), f32),
                   jax.ShapeDtypeStruct((1, 128), f32)],
        scratch_shapes=[pltpu.VMEM((MIX_WIDTH, D_MODEL), f32)],
        compiler_params=_cparams(("arbitrary",)),
    )(x, tgt, y_ssd, y_gdn, w_out, fnw)


def inproj_bwd_dx(x, dout, norm_w, w_perm, dgroups, scattered):
    t = x.shape[0]
    tm = min(512, t)
    f32 = jnp.float32

    def body(x_ref, dout_ref, nw_ref, w_ref, dz_ref, dxbc_ref, dgate_ref, dqkv_ref, dsm_ref, dx_ref, gnw_ref):
        i = pl.program_id(0)

        @pl.when(i == 0)
        def _():
            gnw_ref[...] = jnp.zeros_like(gnw_ref)

        du = None
        for (name, c0, c1), d_ref in zip(GROUPS, (dz_ref, dxbc_ref, dgate_ref, dqkv_ref, dsm_ref)):
            term = jnp.dot(d_ref[...].astype(_MM), _w_rows(w_ref, name, c1 - c0), preferred_element_type=f32)
            du = term if du is None else du + term
        xf = x_ref[...]
        rstd = lax.rsqrt(jnp.mean(xf * xf, axis=-1, keepdims=True) + EPS)
        xhat = xf * rstd
        gnw_ref[0:1, :] += jnp.sum(du * xhat, axis=0, keepdims=True)
        dxh = du * nw_ref[...]
        dx_ref[...] = dout_ref[...] + rstd * (dxh - xhat * jnp.mean(dxh * xhat, axis=-1, keepdims=True))

    blk = lambda w: pl.BlockSpec((tm, w), lambda i: (i, 0))
    steps = t // tm
    kinds = ["scatter"] * len(scattered)
    hosted = _hosting(body, 9, 2, 0, kinds, lambda: pl.program_id(0) == 0, lambda: pl.program_id(0) == steps - 1)
    return _pc_comm(
        hosted, name="inproj_bwd_dx", grid=(steps,),
        in_specs=[blk(D_MODEL), blk(D_MODEL), _full((1, D_MODEL)),
                  pl.BlockSpec((IN_DIM, D_MODEL), lambda i: (0, 0), pipeline_mode=pl.Buffered(1))] +
                 [blk(c1 - c0) for _, c0, c1 in GROUPS] + [ANY] * len(scattered),
        out_specs=[blk(D_MODEL), _full((8, D_MODEL))] + [ANY] * len(scattered),
        out_shape=[jax.ShapeDtypeStruct((t, D_MODEL), f32), jax.ShapeDtypeStruct((8, D_MODEL), f32)] +
                  [_exchange_out_shape("scatter", a) for a in scattered],
        scratch_shapes=_exchange_sems(len(scattered)), compiler_params=_cparams(("arbitrary",)),
    )(x, dout, norm_w, w_perm, *dgroups, *scattered)


def grad_w_group(u, dg, name, scattered=()):
    t, n = dg.shape
    tn = 512 if n % 512 == 0 else n
    tm = 4096 if t % 4096 == 0 else t
    nj, nk = n // tn, t // tm
    f32 = jnp.float32

    def body(u_ref, d_ref, o_ref, acc):
        k = pl.program_id(1)

        @pl.when(k == 0)
        def _():
            acc[...] = jnp.zeros_like(acc)

        acc[...] += lax.dot_general(d_ref[...].astype(_MM), u_ref[...], (((0,), (0,)), ((), ())),
                                    preferred_element_type=f32)

        @pl.when(k == nk - 1)
        def _():
            o_ref[...] = acc[...].astype(o_ref.dtype)

    ne = len(scattered)
    hosted = _hosting(body, 2, 1, 1, ["scatter"] * ne,
                      lambda: (pl.program_id(0) == 0) & (pl.program_id(1) == 0),
                      lambda: (pl.program_id(0) == nj - 1) & (pl.program_id(1) == nk - 1))
    res = (_pc_comm if ne else _pc)(
        hosted, name=name, grid=(nj, nk),
        in_specs=[pl.BlockSpec((tm, D_MODEL), lambda j, k: (k, 0)),
                  pl.BlockSpec((tm, tn), lambda j, k: (k, j))] + [ANY] * ne,
        out_specs=[pl.BlockSpec((tn, D_MODEL), lambda j, k: (j, 0))] + [ANY] * ne,
        out_shape=[jax.ShapeDtypeStruct((n, D_MODEL), _MM)] + [_exchange_out_shape("scatter", a) for a in scattered],
        scratch_shapes=[pltpu.VMEM((tn, D_MODEL), f32)] + _exchange_sems(ne),
        compiler_params=_cparams(("arbitrary", "arbitrary")),
    )(u, dg, *scattered)
    return res if ne else res[0]


def _pad_lanes(v, off):
    n = v.shape[-1]
    return jnp.pad(v.reshape(1, n).astype(jnp.float32), ((0, 0), (off, 128 - off - n)))


REF_ROWS = dict(z=(0, 1024), xbc=(1024, 2560), dt=(2560, 2576), gate=(2576, 3600), qkv=(3600, 6672), ab=(6672, 6688))


def unperm_w_in(gz, gxbc, ggate, gqkv, gsm):
    src = dict(z=gz, xbc=gxbc, dt=gsm[0:16], gate=ggate, qkv=gqkv, ab=gsm[16:32])
    slabs = []
    for k in range(N_DEV):
        a, b = k * W_IN_SHARD, (k + 1) * W_IN_SHARD
        parts = []
        for name, (s, e) in REF_ROWS.items():
            lo, hi = max(a, s), min(b, e)
            if lo < hi:
                parts.append(src[name][lo - s:hi - s])
        slabs.append(jnp.concatenate(parts, axis=0))
    return jnp.stack(slabs)


def all_gather(arrs, name):
    n = len(arrs)

    def body(*refs):
        ins, outs = refs[:n], refs[n:2 * n]
        send_sems, recv_sems, local_sems = refs[2 * n:]
        x, y, c, me = _me()
        sibling = (x, y, 1 - c)
        chips = [(1 - x, y), (x, 1 - y), (1 - x, 1 - y)]

        def idx(px, py, pc):
            return 4 * px + 2 * py + pc

        def copy(a, k, block, to, src=None):
            slot = outs[a].at[idx(*block)]
            return pltpu.make_async_remote_copy(src_ref=slot if src is None else src, dst_ref=slot,
                                                send_sem=send_sems.at[a, k], recv_sem=recv_sems.at[a, k],
                                                device_id=to, device_id_type=MESH)

        local = [pltpu.make_async_copy(ins[a], outs[a].at[me], local_sems.at[a]) for a in range(n)]
        for cp in local:
            cp.start()
        started = []
        for a in range(n):
            first = [copy(a, 0, (x, y, c), sibling, src=ins[a])]
            first += [copy(a, 1 + j, (x, y, c), (*chip, c), src=ins[a]) for j, chip in enumerate(chips)]
            for cp in first:
                cp.start()
            started += first
        for a in range(n):
            for j, chip in enumerate(chips):
                copy(a, 1 + j, (*chip, c), (x, y, c)).wait_recv()
                fwd = copy(a, 4 + j, (*chip, c), sibling)
                fwd.start()
                started.append(fwd)
        for a in range(n):
            copy(a, 0, sibling, (x, y, c)).wait_recv()
            for j, chip in enumerate(chips):
                copy(a, 4 + j, (*chip, 1 - c), (x, y, c)).wait_recv()
        for cp in started:
            cp.wait_send()
        for cp in local:
            cp.wait()

    return _pc_comm(
        body, name=name, in_specs=[ANY] * n, out_specs=[ANY] * n,
        out_shape=[jax.ShapeDtypeStruct((N_DEV,) + a.shape, a.dtype) for a in arrs],
        scratch_shapes=[pltpu.SemaphoreType.DMA((n, 7)), pltpu.SemaphoreType.DMA((n, 7)),
                        pltpu.SemaphoreType.DMA((n,))],
    )(*arrs)


def adamw_sum(recv, w, m, v, rows, name, cols=None):
    r, ccols = w.shape
    f32 = jnp.float32
    c1 = 1.0 / (1.0 - ADAM_B1 ** ADAM_STEP)
    c2 = 1.0 / (1.0 - ADAM_B2 ** ADAM_STEP)

    def body(recv_ref, w_ref, m_ref, v_ref, g_ref, d_ref, mo_ref, vo_ref):
        g = recv_ref[0].astype(f32)
        for k in range(1, N_DEV):
            g = g + recv_ref[k].astype(f32)
        mn = ADAM_B1 * m_ref[...] + (1.0 - ADAM_B1) * g
        vn = ADAM_B2 * v_ref[...] + (1.0 - ADAM_B2) * (g * g)
        g_ref[...] = g
        mo_ref[...] = mn
        vo_ref[...] = vn
        d_ref[...] = -ADAM_LR * ((mn * c1) / (jnp.sqrt(vn * c2) + ADAM_EPS) + ADAM_WD * w_ref[...])

    if cols is None:
        blk = pl.BlockSpec((rows, ccols), lambda i: (i, 0))
        rblk, steps = pl.BlockSpec((N_DEV, rows, ccols), lambda i: (0, i, 0)), r // rows
    else:
        blk = pl.BlockSpec((r, cols), lambda i: (0, i))
        rblk, steps = pl.BlockSpec((N_DEV, r, cols), lambda i: (0, 0, i)), ccols // cols
    return _pc(
        body, name=name, grid=(steps,),
        in_specs=[rblk, blk, blk, blk],
        out_specs=[blk] * 4, out_shape=[jax.ShapeDtypeStruct((r, ccols), f32)] * 4,
        compiler_params=_cparams(("arbitrary",)),
    )(recv, w, m, v)


SMALL = (("norm_w", 1, 1024, 0), ("ssd_conv_b", 1, 1536, 0), ("ssd_dt_bias", 1, 16, 0), ("ssd_a_log", 1, 16, 0),
         ("ssd_d", 1, 16, 0), ("ssd_norm_w", 1, 1024, 0), ("gdn_dt_bias", 1, 8, 16), ("gdn_a_log", 1, 8, 16),
         ("gdn_norm_w", 1, 128, 0), ("final_norm_w", 1, 1024, 0),
         ("ssd_conv_w", CONV_K, SSD_CONV_DIM // N_DEV, 0), ("gdn_conv_w", CONV_K, GDN_CONV_DIM // N_DEV, 0))


def _small_layout():
    out, off = [], 0
    for name, rows, n, lane0 in SMALL + (("loss", 1, 128, 0),):
        stride = -(-(lane0 + n) // 128) * 128
        out.append((name, rows, n, lane0, stride, off))
        off += rows * stride
    return out, off


def scatter_small(accs):
    layout, total = _small_layout()
    f32 = jnp.float32

    def body(*refs):
        acc_refs, out_ref, slabs = refs[:len(layout)], refs[len(layout)], refs[len(layout) + 1]
        sems = refs[len(layout) + 2:]
        slabs[...] = jnp.zeros_like(slabs)
        for (name, rows, n, lane0, stride, off), acc in zip(layout, acc_refs):
            for k in range(N_DEV):
                if rows == 1:
                    slabs[k, :, off:off + stride] = acc[0:1, 0:stride]
                else:
                    for j in range(rows):
                        slabs[k, :, off + stride * j:off + stride * j + n] = acc[j:j + 1, n * k:n * k + n]
        start, wait = _exchange_ops("scatter", slabs, out_ref, *sems)
        start()
        wait()

    return _pc_comm(
        body, name="scatter_small_grads", out_specs=ANY, out_shape=jax.ShapeDtypeStruct((N_DEV, 1, total), f32),
        scratch_shapes=[pltpu.VMEM((N_DEV, 1, total), f32)] + _exchange_sems(1),
    )(*accs)


def adamw_small(recv, w, m, v):
    layout, total = _small_layout()
    loss_off = layout[-1][5]
    layout = layout[:-1]
    f32 = jnp.float32
    c1 = 1.0 / (1.0 - ADAM_B1 ** ADAM_STEP)
    c2 = 1.0 / (1.0 - ADAM_B2 ** ADAM_STEP)
    np_ = len(layout)

    def body(*refs):
        recv_ref = refs[0]
        w_refs, m_refs, v_refs = refs[1:1 + np_], refs[1 + np_:1 + 2 * np_], refs[1 + 2 * np_:1 + 3 * np_]
        o_refs = refs[1 + 3 * np_:]
        g_all = recv_ref[0]
        for k in range(1, N_DEV):
            g_all = g_all + recv_ref[k]
        o_refs[4 * np_][...] = g_all[:, loss_off:loss_off + 128]

        def update(g, wv, mv, vv):
            mn = ADAM_B1 * mv + (1.0 - ADAM_B1) * g
            vn = ADAM_B2 * vv + (1.0 - ADAM_B2) * (g * g)
            return g, -ADAM_LR * ((mn * c1) / (jnp.sqrt(vn * c2) + ADAM_EPS) + ADAM_WD * wv), mn, vn

        for p, (name, rows, n, lane0, stride, off) in enumerate(layout):
            outs = o_refs[4 * p:4 * p + 4]
            if rows == 1:
                res = update(g_all[:, off + lane0:off + lane0 + n], w_refs[p][...], m_refs[p][...], v_refs[p][...])
                for o, r in zip(outs, res):
                    o[...] = r
            else:
                for j in range(rows):
                    res = update(g_all[:, off + stride * j:off + stride * j + n], w_refs[p][0, j:j + 1, :],
                                 m_refs[p][0, j:j + 1, :], v_refs[p][0, j:j + 1, :])
                    for o, r in zip(outs, res):
                        o[0, j:j + 1, :] = r

    names = [e[0] for e in layout]
    ins = [recv] + [d[nm] for d in (w, m, v) for nm in names]
    out_shape = [jax.ShapeDtypeStruct(w[nm].shape, f32) for nm in names for _ in range(4)]
    out_shape.append(jax.ShapeDtypeStruct((1, 128), f32))
    res = _pc(body, name="adamw_small", out_shape=out_shape)(*ins)
    return {nm: tuple(res[4 * p:4 * p + 4]) for p, nm in enumerate(names)}, res[4 * np_]


SHARD = (("ssd_conv_w", CONV_K * SSD_CONV_DIM // N_DEV), ("gdn_conv_w", CONV_K * GDN_CONV_DIM // N_DEV))
SHARD_ROWS = 24


def _rows_of(size):
    return -(-size // 128)


def _pack(vals, layout, total_rows):
    parts = []
    for (name, size), val in zip(layout, vals):
        flat = val.reshape(-1).astype(jnp.float32)
        parts.append(jnp.pad(flat, (0, _rows_of(size) * 128 - size)).reshape(-1, 128))
    used = sum(_rows_of(s) for _, s in layout)
    parts.append(jnp.zeros((total_rows - used, 128), jnp.float32))
    return jnp.concatenate(parts, axis=0)


def _conv_full(gathered_flat, ccols):
    return gathered_flat.reshape(N_DEV, CONV_K, ccols // N_DEV).transpose(1, 0, 2).reshape(CONV_K, ccols)


def kernel(x, norm_w, w_in, ssd_conv_w, ssd_conv_b, ssd_dt_bias, ssd_a_log, ssd_d, ssd_norm_w, gdn_conv_w, gdn_dt_bias, gdn_a_log, gdn_norm_w, w_out, final_norm_w, loss_target, m_norm_w, m_w_in, m_ssd_conv_w, m_ssd_conv_b, m_ssd_dt_bias, m_ssd_a_log, m_ssd_d, m_ssd_norm_w, m_gdn_conv_w, m_gdn_dt_bias, m_gdn_a_log, m_gdn_norm_w, m_w_out, m_final_norm_w, v_norm_w, v_w_in, v_ssd_conv_w, v_ssd_conv_b, v_ssd_dt_bias, v_ssd_a_log, v_ssd_d, v_ssd_norm_w, v_gdn_conv_w, v_gdn_dt_bias, v_gdn_a_log, v_gdn_norm_w, v_w_out, v_final_norm_w):
    f32 = jnp.float32
    w = dict(norm_w=norm_w, w_in=w_in, ssd_conv_w=ssd_conv_w, ssd_conv_b=ssd_conv_b, ssd_dt_bias=ssd_dt_bias,
             ssd_a_log=ssd_a_log, ssd_d=ssd_d, ssd_norm_w=ssd_norm_w, gdn_conv_w=gdn_conv_w, gdn_dt_bias=gdn_dt_bias,
             gdn_a_log=gdn_a_log, gdn_norm_w=gdn_norm_w, w_out=w_out, final_norm_w=final_norm_w)
    m = dict(norm_w=m_norm_w, w_in=m_w_in, ssd_conv_w=m_ssd_conv_w, ssd_conv_b=m_ssd_conv_b, ssd_dt_bias=m_ssd_dt_bias,
             ssd_a_log=m_ssd_a_log, ssd_d=m_ssd_d, ssd_norm_w=m_ssd_norm_w, gdn_conv_w=m_gdn_conv_w,
             gdn_dt_bias=m_gdn_dt_bias, gdn_a_log=m_gdn_a_log, gdn_norm_w=m_gdn_norm_w, w_out=m_w_out,
             final_norm_w=m_final_norm_w)
    v = dict(norm_w=v_norm_w, w_in=v_w_in, ssd_conv_w=v_ssd_conv_w, ssd_conv_b=v_ssd_conv_b, ssd_dt_bias=v_ssd_dt_bias,
             ssd_a_log=v_ssd_a_log, ssd_d=v_ssd_d, ssd_norm_w=v_ssd_norm_w, gdn_conv_w=v_gdn_conv_w,
             gdn_dt_bias=v_gdn_dt_bias, gdn_a_log=v_gdn_a_log, gdn_norm_w=v_gdn_norm_w, w_out=v_w_out,
             final_norm_w=v_final_norm_w)
    names = list(w)
    shapes = {n: w[n].shape for n in names}

    xl, tgt = x[0], loss_target[0]
    cs = _consts()
    dtb_s = _pad_lanes(ssd_dt_bias, 0)
    alog_s = _pad_lanes(ssd_a_log, 0)
    dpar = _pad_lanes(ssd_d, 0)
    dtb_g = _pad_lanes(gdn_dt_bias, 16)
    alog_g = _pad_lanes(gdn_a_log, 16)
    nw_g = gdn_norm_w.reshape(1, 128)
    nw_s = ssd_norm_w.reshape(1, 1024)
    cb_s = ssd_conv_b.reshape(1, 1536)
    nw1 = norm_w.reshape(1, D_MODEL)

    (g_w_in,) = all_gather([w_in[0].T.astype(_MM)], "gather_w_in")
    w_perm = g_w_in.reshape(IN_DIM, D_MODEL)
    conv_pack = _pack([w["ssd_conv_w"], w["gdn_conv_w"]], SHARD, SHARD_ROWS)
    u, z, xbc, gate, qkv, sm, g_w_out, g_conv = inproj_fwd(xl, nw1, w_perm, [w_out[0].astype(_MM), conv_pack])
    w_out_full = g_w_out.reshape(MIX_WIDTH, D_MODEL)
    ssd_cw = _conv_full(g_conv[:, 0:6].reshape(N_DEV, -1), SSD_CONV_DIM)
    gdn_cw = _conv_full(g_conv[:, 6:18].reshape(N_DEV, -1), GDN_CONV_DIM)

    nc = xl.shape[0] // CHUNK
    y_ssd, hs, pre_s, y_gdn, ss, ts, pre_g = _chunk_call(
        [ssd_fwd(z, xbc, sm, ssd_cw, cb_s, dtb_s, alog_s, dpar, nw_s, cs),
         gdn_fwd(gate, qkv, sm, gdn_cw, dtb_g, alog_g, nw_g, cs)], "scan_fwd", nc, False)
    dout, dys, dyg, g_wout, g_fnw, loss_l = out_fwd_bwd(xl, tgt, y_ssd, y_gdn, w_out_full,
                                                        final_norm_w.reshape(1, D_MODEL))
    (dz, dxbc, g_cw_s, g_cb_s, g_dtb_s, g_alog_s, g_d, g_nw_s,
     dgate, dqkv, dsm, g_cw_g, g_dtb_g, g_alog_g, g_nw_g) = _chunk_call(
        [ssd_bwd(z, xbc, pre_s, sm, hs, dys, ssd_cw, dtb_s, alog_s, dpar, nw_s, cs),
         gdn_bwd(gate, qkv, pre_g, sm, ss, ts, dyg, gdn_cw, dtb_g, alog_g, nw_g, cs)], "scan_bwd", nc, True)

    t_w_out = g_wout.reshape(N_DEV, MIX_WIDTH // N_DEV, D_MODEL)
    gws = {}
    for dg, (name, _, _) in zip((dz, dxbc, dgate, dsm), (GROUPS[0], GROUPS[1], GROUPS[2], GROUPS[4])):
        gws[name] = grad_w_group(u, dg, "grad_w_in_" + name)
    gws["qkv"], r_w_out = grad_w_group(u, dqkv, "grad_w_in_qkv", [t_w_out])
    t_w_in = unperm_w_in(gws["z"], gws["xbc"], gws["gate"], gws["qkv"], gws["sm"])
    dx, g_nw, r_w_in = inproj_bwd_dx(xl, dout, nw1, w_perm, (dz, dxbc, dgate, dqkv, dsm), [t_w_in])

    accs = dict(norm_w=g_nw, ssd_conv_b=g_cb_s, ssd_dt_bias=g_dtb_s, ssd_a_log=g_alog_s, ssd_d=g_d,
                ssd_norm_w=g_nw_s, gdn_dt_bias=g_dtb_g, gdn_a_log=g_alog_g, gdn_norm_w=g_nw_g, final_norm_w=g_fnw,
                ssd_conv_w=g_cw_s, gdn_conv_w=g_cw_g)
    r_small = scatter_small([accs[e[0]] for e in SMALL] + [loss_l])

    o_w_in = adamw_sum(r_w_in, w_in[0].T, m_w_in[0].T, v_w_in[0].T, None, "adamw_w_in", cols=256)
    o_w_out = adamw_sum(r_w_out, w_out[0], m_w_out[0], v_w_out[0], 64, "adamw_w_out")
    row = lambda d: {n: (a.reshape(1, -1) if a.ndim == 1 else a) for n, a in d.items()}
    o_small, loss_sum = adamw_small(r_small, row(w), row(m), row(v))

    loss = loss_sum[0, 0]
    outs = [loss, dx[None]]
    for k in range(4):
        parts = {n: o_small[n][k] for n in o_small}
        parts["w_in"] = o_w_in[k].T
        parts["w_out"] = o_w_out[k]
        outs += [parts[n].reshape(shapes[n]) for n in names]
    return tuple(outs)
```

```python
import functools

import jax
import jax.numpy as jnp
import numpy as np
from jax import lax
from jax.experimental import pallas as pl
from jax.experimental.pallas import tpu as pltpu

_MM = jnp.bfloat16

D_MODEL = 1024
CHUNK = 64
CONV_K = 4
EPS = 1e-6
SSD_CONV_DIM = 1536
GDN_HEADS = 8
GDN_DK = 128
GDN_CONV_DIM = 3072
MIX_WIDTH = 2048
IN_DIM = 6688
N_DEV = 8
W_IN_SHARD = IN_DIM // N_DEV
HI = lax.Precision.HIGHEST
HIGH = lax.Precision.HIGH
VMEM_LIMIT = 56 * 1024 * 1024

ADAM_LR = 0.001
ADAM_B1 = 0.9
ADAM_B2 = 0.999
ADAM_EPS = 1e-08
ADAM_WD = 0.01
ADAM_STEP = 10


def _pc(body, **kw):
    return pl.pallas_call(body, **kw)


def _pc_comm(body, **kw):
    return pl.pallas_call(body, **kw)


def _cparams(sem):
    return pltpu.CompilerParams(dimension_semantics=sem, vmem_limit_bytes=VMEM_LIMIT)


def _sig(x):
    return 0.5 * jnp.tanh(0.5 * x) + 0.5


@jax.custom_vjp
def _sigmoid(x):
    return _sig(x)


def _sigmoid_fwd(x):
    s = _sig(x)
    return s, s


def _sigmoid_bwd(s, g):
    return (g * s * (1.0 - s),)


_sigmoid.defvjp(_sigmoid_fwd, _sigmoid_bwd)


@jax.custom_vjp
def _silu(x):
    return x * _sig(x)


def _silu_fwd(x):
    s = _sig(x)
    return x * s, (x, s)


def _silu_bwd(res, g):
    x, s = res
    return (g * (s * (1.0 + x * (1.0 - s))),)


_silu.defvjp(_silu_fwd, _silu_bwd)


def _softplus_impl(x):
    return jnp.maximum(x, 0.0) + jnp.log(1.0 + jnp.exp(-jnp.abs(x)))


@jax.custom_vjp
def _softplus(x):
    return _softplus_impl(x)


def _softplus_fwd(x):
    return _softplus_impl(x), x


def _softplus_bwd(x, g):
    return (g * _sig(x),)


_softplus.defvjp(_softplus_fwd, _softplus_bwd)


def _lane_bcast_impl(x, k):
    return jnp.broadcast_to(x[..., k:k + 1], x.shape)


@functools.partial(jax.custom_vjp, nondiff_argnums=(1,))
def _lane_bcast(x, k):
    return _lane_bcast_impl(x, k)


def _lane_bcast_fwd(x, k):
    return _lane_bcast_impl(x, k), None


def _lane_bcast_bwd(k, _, g):
    lane = lax.broadcasted_iota(jnp.int32, g.shape, g.ndim - 1)
    return (jnp.where(lane == k, jnp.sum(g, axis=-1, keepdims=True), 0.0),)


_lane_bcast.defvjp(_lane_bcast_fwd, _lane_bcast_bwd)


def _mm(a, b):
    return jnp.dot(a.astype(_MM), b.astype(_MM), preferred_element_type=jnp.float32)


def _mm_nt(a, b):
    return lax.dot_general(a.astype(_MM), b.astype(_MM), (((1,), (1,)), ((), ())),
                           preferred_element_type=jnp.float32)


def _mm_tn(a, b):
    return lax.dot_general(a.astype(_MM), b.astype(_MM), (((0,), (0,)), ((), ())),
                           preferred_element_type=jnp.float32)


def _dot_hi(a, b):
    return jnp.dot(a, b, precision=HI, preferred_element_type=jnp.float32)


def _bmm(a, b):
    return lax.dot_general(a.astype(_MM), b.astype(_MM), (((2,), (1,)), ((0,), (0,))),
                           preferred_element_type=jnp.float32)


def _bmm_nt(a, b):
    return lax.dot_general(a.astype(_MM), b.astype(_MM), (((2,), (2,)), ((0,), (0,))),
                           preferred_element_type=jnp.float32)


def _bmm_tn(a, b):
    return lax.dot_general(a.astype(_MM), b.astype(_MM), (((1,), (1,)), ((0,), (0,))),
                           preferred_element_type=jnp.float32)


def _bmm_hi(a, b):
    return lax.dot_general(a, b, (((2,), (1,)), ((0,), (0,))), precision=HIGH, preferred_element_type=jnp.float32)


def _bmm_nt_hi(a, b):
    return lax.dot_general(a, b, (((2,), (2,)), ((0,), (0,))), precision=HIGH, preferred_element_type=jnp.float32)


def _bmm_tn_hi(a, b):
    return lax.dot_general(a, b, (((1,), (1,)), ((0,), (0,))), precision=HIGH, preferred_element_type=jnp.float32)


def _consts():
    l = np.arange(CHUNK)
    tri = (l[:, None] >= l[None, :]).astype(np.float32)
    lane = np.arange(128)
    i2 =(l[:, None] == (lane[None, :] % 64)).astype(np.float32)
    mask2 = (l[:, None] >= (lane[None, :] % 64)).astype(np.float32)
    lo = (lane < 64).astype(np.float32)[None, :]
    i64 = np.eye(CHUNK, dtype=np.float32)
    strict = (l[:, None] > l[None, :]).astype(np.float32)
    return dict(tri=jnp.asarray(tri), i2=jnp.asarray(i2), mask2=jnp.asarray(mask2), lo=jnp.asarray(lo),
                i64=jnp.asarray(i64), strict=jnp.asarray(strict))


def _ssd_chunk(xs_pre, b_pre, c_pre, z, sm, ht, dtb, alog, dpar, nw, tri, i2, mask2, lo):
    lane = lax.broadcasted_iota(jnp.int32, (1, 128), 1)
    m16 = lane < 16
    dt = jnp.where(m16, _softplus(sm + dtb), 0.0)
    a_neg = -jnp.exp(alog)
    cum = _dot_hi(tri, dt * a_neg)
    row = lax.broadcasted_iota(jnp.int32, (CHUNK, 1), 0)
    is_last = row == CHUNK - 1
    hi = 1.0 - lo
    bm = [_silu(b) for b in b_pre]
    cm = [_silu(c) for c in c_pre]
    cb2 = [_mm_nt(cm[g], jnp.concatenate([bm[g], bm[g]], axis=0)) for g in range(2)]
    ht_g = [jnp.concatenate(ht[4 * g:4 * g + 4], axis=1) for g in range(2)]
    yoff_g = [_mm(cm[g], ht_g[g]) for g in range(2)]
    yg, xdec, clast = [], [], []
    for j in range(8):
        g, k4 = j // 4, j % 4
        pair = lambda v, j=j: jnp.where(lo > 0.5, _lane_bcast(v, 2 * j), _lane_bcast(v, 2 * j + 1))
        xs = _silu(xs_pre[j])
        dte = pair(dt)
        cume = pair(cum)
        cum_last = jnp.sum(jnp.where(is_last, cume, 0.0), axis=0, keepdims=True)
        xdt = xs * dte
        rowv = jnp.sum(cume * i2, axis=0, keepdims=True)
        lm = jnp.exp(jnp.where(mask2 > 0.5, cume - rowv, -jnp.inf))
        m = cb2[g] * lm
        xblk = jnp.concatenate([xdt * lo, xdt * hi], axis=0)
        y = _mm(m, xblk)
        y = y + yoff_g[g][:, 128 * k4:128 * k4 + 128] * jnp.exp(cume)
        y = y + pair(dpar) * xs
        yg.append(y * _silu(z[j]))
        xdec.append(xdt * jnp.exp(cum_last - cume))
        clast.append(cum_last)
    ht_next = []
    for g in range(2):
        st = _mm_tn(bm[g], jnp.concatenate(xdec[4 * g:4 * g + 4], axis=1))
        for k4 in range(4):
            j = 4 * g + k4
            ht_next.append(ht[j] * jnp.exp(clast[j]) + st[:, 128 * k4:128 * k4 + 128])
    outs = []
    for g in range(2):
        ss = sum(jnp.sum(yg[j] * yg[j], axis=-1, keepdims=True) for j in range(4 * g, 4 * g + 4))
        rs = lax.rsqrt(ss * (1.0 / 512.0) + EPS)
        for j in range(4 * g, 4 * g + 4):
            outs.append(yg[j] * rs * nw[j])
    return outs, ht_next


def _tri_inverse(a):
    eye = jnp.eye(CHUNK, dtype=jnp.float32)[None]
    p = eye - a
    x = _bmm_hi(a, a)
    for _ in range(4):
        both = _bmm_hi(jnp.concatenate([p, x], axis=1), x)
        p = p + both[:, :CHUNK]
        x = both[:, CHUNK:]
    return p + _bmm_hi(p, x)


def _solve_apply(t, r1, r2):
    both = _bmm_hi(t, jnp.concatenate([r1, r2], axis=-1))
    n = r1.shape[-1]
    return both[..., :n], both[..., n:]


@jax.custom_vjp
def _solve(a, r1, r2, t):
    return _solve_apply(t, r1, r2)


def _solve_fwd(a, r1, r2, t):
    u, w = _bmm_hi(t, r1), _bmm_hi(t, r2)
    return (u, w), (t, u, w)


def _solve_bwd(res, cts):
    t, u, w = res
    du, dw = cts
    dr1 = _bmm_tn_hi(t, du)
    dr2 = _bmm_tn_hi(t, dw)
    da = -(_bmm_nt_hi(dr1, u) + _bmm_nt_hi(dr2, w))
    return da, dr1, dr2, jnp.zeros_like(t)


_solve.defvjp(_solve_fwd, _solve_bwd)


def _gdn_chunk(q_pre, k_pre, v_pre, gate, sm, s, dtb, alog, nw, tri, i64, strict, t_in=None):
    lane = lax.broadcasted_iota(jnp.int32, (1, 128), 1)
    m_a = (lane >= 16) & (lane < 24)
    g_full = jnp.where(m_a, -jnp.exp(alog) * _softplus(sm + dtb), 0.0)
    gc = _dot_hi(tri, g_full)
    sig = _sigmoid(sm)
    heads = lambda f: jnp.concatenate([f(h)[None] for h in range(GDN_HEADS)], axis=0)
    gc3 = heads(lambda h: _lane_bcast(gc, 16 + h))
    beta3 = heads(lambda h: _lane_bcast(sig, 24 + h))
    q = _silu(q_pre)
    q = q * lax.rsqrt(jnp.sum(q * q, axis=-1, keepdims=True) + EPS) * (GDN_DK ** -0.5)
    k = _silu(k_pre)
    k = k * lax.rsqrt(jnp.sum(k * k, axis=-1, keepdims=True) + EPS)
    v = _silu(v_pre)
    gcl = gc3[:, :, :CHUNK]
    gc_row = jnp.sum(gcl * i64[None], axis=1, keepdims=True)
    incl = (strict + i64)[None] > 0.5
    decay = jnp.exp(jnp.where(incl, gcl - gc_row, -jnp.inf))
    kb = k * beta3
    a = jnp.where(strict[None] > 0.5, _bmm_nt(kb, k) * decay, 0.0)
    egc = jnp.exp(gc3)
    t = _tri_inverse(a) if t_in is None else t_in
    u, w = _solve(a, v * beta3, kb * egc, t)
    attn = _bmm_nt(q, k) * decay
    row = lax.broadcasted_iota(jnp.int32, (1, CHUNK, 1), 1)
    gl = jnp.sum(jnp.where(row == CHUNK - 1, gc3, 0.0), axis=1, keepdims=True)
    q_dec = q * egc
    k_dec = k * jnp.exp(gl - gc3)
    v_new = u - _bmm(w, s)
    o = _bmm(q_dec, s) + _bmm(attn, v_new)
    s_next = s * jnp.exp(gl) + _bmm_tn(k_dec, v_new)
    on = o * lax.rsqrt(jnp.mean(o * o, axis=-1, keepdims=True) + EPS) * nw
    return on * _silu(gate), s_next, t


def _conv_fwd(pbuf, w_ref, c0, c1):
    blk = pbuf[:, c0:c1]
    acc = w_ref[CONV_K - 1:CONV_K, c0:c1] * blk[8:72]
    for j in range(CONV_K - 1):
        acc = acc + w_ref[j:j + 1, c0:c1] * pltpu.roll(blk, CONV_K - 1 - j, axis=0)[8:72]
    return acc


MESH = pl.DeviceIdType.MESH
ANY = pl.BlockSpec(memory_space=pl.ANY)


def _me():
    x, y, c = lax.axis_index("x"), lax.axis_index("y"), lax.axis_index("c")
    return x, y, c, 4 * x + 2 * y + c


def _peer(r):
    x, y, c, _ = _me()
    px = 1 - x if r & 4 else x
    py = 1 - y if r & 2 else y
    pc = 1 - c if r & 1 else c
    return (px, py, pc), 4 * px + 2 * py + pc


def _exchange_ops(kind, in_ref, out_ref, send_sems, recv_sems, local_sem):
    me = _me()[3]
    local = pltpu.make_async_copy(in_ref.at[me] if kind == "scatter" else in_ref, out_ref.at[me], local_sem)
    sends, recvs = [], []
    for r in range(1, N_DEV):
        peer, pidx = _peer(r)
        src = in_ref.at[pidx] if kind == "scatter" else in_ref
        sems = dict(send_sem=send_sems.at[r - 1], recv_sem=recv_sems.at[r - 1], device_id=peer, device_id_type=MESH)
        sends.append(pltpu.make_async_remote_copy(src_ref=src, dst_ref=out_ref.at[me], **sems))
        recvs.append(pltpu.make_async_remote_copy(src_ref=src, dst_ref=out_ref.at[pidx], **sems))

    def start():
        local.start()
        for cp in sends:
            cp.start()

    def wait():
        for cp in recvs:
            cp.wait_recv()
        for cp in sends:
            cp.wait_send()
        local.wait()

    return start, wait


def _exchange_sems(n):
    return [pltpu.SemaphoreType.DMA((N_DEV - 1,)), pltpu.SemaphoreType.DMA((N_DEV - 1,)),
            pltpu.SemaphoreType.DMA(())] * n


def _exchange_out_shape(kind, a):
    return jax.ShapeDtypeStruct(a.shape if kind == "scatter" else (N_DEV,) + a.shape, a.dtype)


def _hosting(body, n_in, n_out, n_scratch, kinds, first, last):
    ne = len(kinds)

    def wrapped(*refs):
        ins, ex_in = refs[:n_in], refs[n_in:n_in + ne]
        o0 = n_in + ne
        outs, ex_out = refs[o0:o0 + n_out], refs[o0 + n_out:o0 + n_out + ne]
        s0 = o0 + n_out + ne
        scr, sems = refs[s0:s0 + n_scratch], refs[s0 + n_scratch:]
        ops = [_exchange_ops(kinds[e], ex_in[e], ex_out[e], *sems[3 * e:3 * e + 3]) for e in range(ne)]

        @pl.when(first())
        def _():
            for start, _ in ops:
                start()

        body(*ins, *outs, *scr)

        @pl.when(last())
        def _():
            for _, wait in ops:
                wait()

    return wrapped


GROUPS = (("z", 0, 1024), ("xbc", 1024, 2560), ("gate", 2560, 3584), ("qkv", 3584, 6656), ("sm", 6656, 6784))
GROUP_ROWS = dict(z=((0, 1024),), xbc=((1024, 2560),), gate=((2576, 3600),), qkv=((3600, 6672),),
                  sm=((2560, 2576), (6672, 6688)))


def _w_rows(w_ref, name, width):
    pieces = [w_ref[a:b, :] for a, b in GROUP_ROWS[name]]
    n = sum(b - a for a, b in GROUP_ROWS[name])
    if n < width:
        pieces.append(jnp.zeros((width - n, D_MODEL), w_ref.dtype))
    return pieces[0] if len(pieces) == 1 else jnp.concatenate(pieces, axis=0)


def inproj_fwd(x, norm_w, w_perm, gathered):
    t = x.shape[0]
    tm = min(512, t)
    steps = t // tm
    kinds = ["gather"] * len(gathered)

    def body(x_ref, nw_ref, w_ref, u_ref, z_ref, xbc_ref, gate_ref, qkv_ref, sm_ref):
        xf = x_ref[...]
        rstd = lax.rsqrt(jnp.mean(xf * xf, axis=-1, keepdims=True) + EPS)
        u = (xf * rstd * nw_ref[...]).astype(_MM)
        u_ref[...] = u
        for (name, c0, c1), o_ref in zip(GROUPS, (z_ref, xbc_ref, gate_ref, qkv_ref, sm_ref)):
            o_ref[...] = lax.dot_general(u, _w_rows(w_ref, name, c1 - c0), (((1,), (1,)), ((), ())),
                                         preferred_element_type=jnp.float32)

    outs = [jax.ShapeDtypeStruct((t, D_MODEL), _MM)] + [jax.ShapeDtypeStruct((t, c1 - c0), jnp.float32)
                                                        for _, c0, c1 in GROUPS]
    hosted = _hosting(body, 3, 6, 0, kinds, lambda: pl.program_id(0) == 0, lambda: pl.program_id(0) == steps - 1)
    return _pc_comm(
        hosted, name="inproj_fwd", grid=(steps,),
        in_specs=[pl.BlockSpec((tm, D_MODEL), lambda i: (i, 0)),
                  pl.BlockSpec((1, D_MODEL), lambda i: (0, 0)),
                  pl.BlockSpec((IN_DIM, D_MODEL), lambda i: (0, 0), pipeline_mode=pl.Buffered(1))] +
                 [ANY] * len(gathered),
        out_specs=[pl.BlockSpec((tm, D_MODEL), lambda i: (i, 0))] +
                  [pl.BlockSpec((tm, c1 - c0), lambda i: (i, 0)) for _, c0, c1 in GROUPS] + [ANY] * len(gathered),
        out_shape=outs + [_exchange_out_shape("gather", a) for a in gathered],
        scratch_shapes=_exchange_sems(len(gathered)), compiler_params=_cparams(("arbitrary",)),
    )(x, norm_w, w_perm, *gathered)


SUB_FWD = 4
SUB_BWD = 2


def _halo_spec(width, idx_fn):
    return pl.BlockSpec((8, width), lambda i: (jnp.maximum(idx_fn(i) * (SUB_FWD * CHUNK // 8) - 1, 0), 0))


def _when_first(shared, fn):
    if shared["first"] is not False:
        pl.when(shared["first"])(fn)


def _full(shape):
    nd = len(shape)
    return pl.BlockSpec(shape, lambda i: (0,) * nd)


def _ssd_split(pre_fn, z_ref, sm_ref):
    xs_pre = [pre_fn(128 * j, 128 * j + 128) for j in range(8)]
    b_pre = [pre_fn(1024 + 128 * g, 1152 + 128 * g) for g in range(2)]
    c_pre = [pre_fn(1280 + 128 * g, 1408 + 128 * g) for g in range(2)]
    z = [z_ref[:, 128 * j:128 * j + 128] for j in range(8)]
    return xs_pre, b_pre, c_pre, z, sm_ref[...]


def ssd_fwd(z, xbc, sm, conv_w, conv_b, dtb, alog, dpar, nw, cs):
    t = z.shape[0]
    nc = t // CHUNK

    def body(shared, z_ref, xbc_ref, halo_ref, sm_ref, cw_ref, cb_ref, dtb_ref, alog_ref, dpar_ref, nw_ref,
             tri_ref, i2_ref, mask2_ref, lo_ref, y_ref, hs_ref, pre_ref, pbuf, ht_scr):
        def init():
            ht_scr[...] = jnp.zeros_like(ht_scr)

        _when_first(shared, init)
        pbuf[0:8, :] = jnp.where(shared["first"], 0.0, halo_ref[...])
        pbuf[8:72, :] = xbc_ref[...]

        def pre_fn(c0, c1):
            pre = _conv_fwd(pbuf, cw_ref, c0, c1) + cb_ref[:, c0:c1]
            pre_ref[:, c0:c1] = pre
            return pre

        xs_pre, b_pre, c_pre, zz, smv = _ssd_split(pre_fn, z_ref, sm_ref)
        ht = [ht_scr[:, 128 * j:128 * j + 128] for j in range(8)]
        hs_ref[0] = ht_scr[...]
        nwl = [nw_ref[:, 128 * j:128 * j + 128] for j in range(8)]
        outs, ht_next = _ssd_chunk(xs_pre, b_pre, c_pre, zz, smv, ht, dtb_ref[...], alog_ref[...], dpar_ref[...],
                                   nwl, tri_ref[...], i2_ref[...], mask2_ref[...], lo_ref[...])
        for j in range(8):
            y_ref[:, 128 * j:128 * j + 128] = outs[j].astype(y_ref.dtype)
            ht_scr[:, 128 * j:128 * j + 128] = ht_next[j]

    blk = lambda w: pl.BlockSpec((SUB_FWD * CHUNK, w), lambda i: (i, 0))
    return dict(
        body=body,
        in_kinds=["rows", "rows", ("halo", 1), "rows"] + ["full"] * 10, out_kinds=["rows", "state", "rows"],
        in_specs=[blk(1024), blk(1536), _halo_spec(1536, lambda i: i), blk(128),
                  _full((CONV_K, 1536)), _full((1, 1536)), _full((1, 128)), _full((1, 128)), _full((1, 128)),
                  _full((1, 1024)), _full((64, 64)), _full((64, 128)), _full((64, 128)),
                  _full((1, 128))],
        out_specs=[blk(1024), pl.BlockSpec((SUB_FWD, 128, 1024), lambda i: (i, 0, 0)), blk(1536)],
        out_shape=[jax.ShapeDtypeStruct((t, 1024), _MM), jax.ShapeDtypeStruct((nc, 128, 1024), jnp.float32),
                   jax.ShapeDtypeStruct((t, 1536), jnp.float32)],
        scratch=[pltpu.VMEM((72, 1536), jnp.float32), pltpu.VMEM((128, 1024), jnp.float32)],
        args=[z, xbc, xbc, sm, conv_w, conv_b, dtb, alog, dpar, nw, cs["tri"], cs["i2"], cs["mask2"], cs["lo"]])


def _conv_bwd(dpre_list, col_ranges, dbuf, carry, x_ref, cw_ref, dx_ref, dcw_ref, dcb_ref, first):
    for dpre, (c0, c1) in zip(dpre_list, col_ranges):
        dbuf[0:64, c0:c1] = dpre
    dbuf[64:72, :] = jnp.where(first, 0.0, carry[...])
    carry[...] = dbuf[0:8, :]
    for (c0, c1) in col_ranges:
        xin = x_ref[:, c0:c1]
        blk = dbuf[:, c0:c1]
        acc = None
        for j in range(CONV_K):
            sh = blk[0:64] if j == CONV_K - 1 else pltpu.roll(blk, 72 - (CONV_K - 1 - j), axis=0)[0:64]
            term = cw_ref[j:j + 1, c0:c1] * sh
            acc = term if acc is None else acc + term
            dcw_ref[j:j + 1, c0:c1] += jnp.sum(xin * sh, axis=0, keepdims=True)
        dx_ref[:, c0:c1] = acc.astype(dx_ref.dtype)
        if dcb_ref is not None:
            dcb_ref[0:1, c0:c1] += jnp.sum(dbuf[0:64, c0:c1], axis=0, keepdims=True)


def ssd_bwd(z, xbc, pre, sm, hs, dy, conv_w, dtb, alog, dpar, nw, cs):
    t = z.shape[0]
    nc = t // CHUNK

    def body(shared, z_ref, xbc_ref, pre_ref, sm_ref, hs_ref, dy_ref, cw_ref, dtb_ref, alog_ref, dpar_ref, nw_ref,
             tri_ref, i2_ref, mask2_ref, lo_ref,
             dz_ref, dxbc_ref, dcw_ref, dcb_ref, ddtb_ref, dalog_ref, ddpar_ref, dnw_ref,
             dbuf, carry, dht_scr):
        def init():
            dht_scr[...] = jnp.zeros_like(dht_scr)
            dcw_ref[...] = jnp.zeros_like(dcw_ref)
            dcb_ref[...] = jnp.zeros_like(dcb_ref)
            ddtb_ref[...] = jnp.zeros_like(ddtb_ref)
            dalog_ref[...] = jnp.zeros_like(dalog_ref)
            ddpar_ref[...] = jnp.zeros_like(ddpar_ref)
            dnw_ref[...] = jnp.zeros_like(dnw_ref)

        _when_first(shared, init)
        pre_fn = lambda c0, c1: pre_ref[:, c0:c1]
        xs_pre, b_pre, c_pre, zz, smv = _ssd_split(pre_fn, z_ref, sm_ref)
        ht = [hs_ref[0, :, 128 * j:128 * j + 128] for j in range(8)]
        nwl = [nw_ref[:, 128 * j:128 * j + 128] for j in range(8)]
        consts = (tri_ref[...], i2_ref[...], mask2_ref[...], lo_ref[...])

        def f(xs_pre, b_pre, c_pre, zz, smv, ht, dtb, alog, dpar, nwl):
            return _ssd_chunk(xs_pre, b_pre, c_pre, zz, smv, ht, dtb, alog, dpar, nwl, *consts)

        _, vjp = jax.vjp(f, xs_pre, b_pre, c_pre, zz, smv, ht, dtb_ref[...], alog_ref[...], dpar_ref[...], nwl)
        dys = [dy_ref[:, 128 * j:128 * j + 128] for j in range(8)]
        dhts = [dht_scr[:, 128 * j:128 * j + 128] for j in range(8)]
        dxs, db, dc, dzz, dsm, dht, ddtb, dalog, ddpar, dnwl = vjp((dys, dhts))
        for j in range(8):
            dz_ref[:, 128 * j:128 * j + 128] = dzz[j].astype(dz_ref.dtype)
            dht_scr[:, 128 * j:128 * j + 128] = dht[j]
            dnw_ref[0:1, 128 * j:128 * j + 128] += dnwl[j]
        shared["dsm_ssd"] = dsm
        ddtb_ref[0:1, :] += ddtb
        dalog_ref[0:1, :] += dalog
        ddpar_ref[0:1, :] += ddpar
        ranges = ([(128 * j, 128 * j + 128) for j in range(8)] + [(1024 + 128 * g, 1152 + 128 * g) for g in range(2)]
                  + [(1280 + 128 * g, 1408 + 128 * g) for g in range(2)])
        _conv_bwd(dxs + db + dc, ranges, dbuf, carry, xbc_ref, cw_ref, dxbc_ref, dcw_ref, dcb_ref, shared["first"])

    ns = nc // SUB_BWD
    rblk = lambda w: pl.BlockSpec((SUB_BWD * CHUNK, w), lambda i: (ns - 1 - i, 0))
    acc = lambda w: pl.BlockSpec((8, w), lambda i: (0, 0))
    f32 = jnp.float32
    return dict(
        body=body,
        in_kinds=["rows"] * 4 + ["state", "rows"] + ["full"] * 9, out_kinds=["rows", "rows"] + ["full"] * 6,
        in_specs=[rblk(1024), rblk(1536), rblk(1536), rblk(128),
                  pl.BlockSpec((SUB_BWD, 128, 1024), lambda i: (ns - 1 - i, 0, 0)), rblk(1024),
                  _full((CONV_K, 1536)), _full((1, 128)), _full((1, 128)), _full((1, 128)),
                  _full((1, 1024)), _full((64, 64)), _full((64, 128)), _full((64, 128)),
                  _full((1, 128))],
        out_specs=[rblk(1024), rblk(1536), acc(1536), acc(1536), acc(128), acc(128), acc(128), acc(1024)],
        out_shape=[jax.ShapeDtypeStruct((t, 1024), f32), jax.ShapeDtypeStruct((t, 1536), f32),
                   jax.ShapeDtypeStruct((8, 1536), f32),
                   jax.ShapeDtypeStruct((8, 1536), f32), jax.ShapeDtypeStruct((8, 128), f32),
                   jax.ShapeDtypeStruct((8, 128), f32), jax.ShapeDtypeStruct((8, 128), f32),
                   jax.ShapeDtypeStruct((8, 1024), f32)],
        scratch=[pltpu.VMEM((72, 1536), f32), pltpu.VMEM((8, 1536), f32), pltpu.VMEM((128, 1024), f32)],
        args=[z, xbc, pre, sm, hs, dy, conv_w, dtb, alog, dpar, nw, cs["tri"], cs["i2"], cs["mask2"], cs["lo"]])


def _gdn_split(pre_fn, gate_ref):
    def heads(base):
        return jnp.stack([pre_fn(base + 128 * h, base + 128 * h + 128) for h in range(GDN_HEADS)])
    gate = jnp.stack([gate_ref[:, 128 * h:128 * h + 128] for h in range(GDN_HEADS)])
    return heads(0), heads(1024), heads(2048), gate


def gdn_fwd(gate, qkv, sm, conv_w, dtb, alog, nw, cs):
    t = gate.shape[0]
    nc = t // CHUNK

    def body(shared, gate_ref, qkv_ref, halo_ref, sm_ref, cw_ref, dtb_ref, alog_ref, nw_ref,
             tri_ref, i64_ref, strict_ref, o_ref, ss_ref, ts_ref, pre_ref, pbuf, s_scr):
        def init():
            s_scr[...] = jnp.zeros_like(s_scr)

        _when_first(shared, init)
        pbuf[0:8, :] = jnp.where(shared["first"], 0.0, halo_ref[...])
        pbuf[8:72, :] = qkv_ref[...]

        def pre_fn(c0, c1):
            pre = _conv_fwd(pbuf, cw_ref, c0, c1)
            pre_ref[:, c0:c1] = pre
            return pre

        q_pre, k_pre, v_pre, g3 = _gdn_split(pre_fn, gate_ref)
        s = s_scr[...]
        ss_ref[0] = s
        out, s_next, tinv = _gdn_chunk(q_pre, k_pre, v_pre, g3, sm_ref[...], s, dtb_ref[...], alog_ref[...],
                                       nw_ref[...], tri_ref[...], i64_ref[...], strict_ref[...])
        ts_ref[0] = tinv
        s_scr[...] = s_next
        for h in range(GDN_HEADS):
            o_ref[:, 128 * h:128 * h + 128] = out[h].astype(o_ref.dtype)

    blk = lambda w: pl.BlockSpec((SUB_FWD * CHUNK, w), lambda i: (i, 0))
    return dict(
        body=body,
        in_kinds=["rows", "rows", ("halo", 1), "rows"] + ["full"] * 7, out_kinds=["rows", "state", "state", "rows"],
        in_specs=[blk(1024), blk(3072), _halo_spec(3072, lambda i: i), blk(128),
                  _full((CONV_K, 3072)), _full((1, 128)), _full((1, 128)), _full((1, 128)),
                  _full((64, 64)), _full((64, 64)), _full((64, 64))],
        out_specs=[blk(1024), pl.BlockSpec((SUB_FWD, 8, 128, 128), lambda i: (i, 0, 0, 0)),
                   pl.BlockSpec((SUB_FWD, 8, CHUNK, CHUNK), lambda i: (i, 0, 0, 0)), blk(3072)],
        out_shape=[jax.ShapeDtypeStruct((t, 1024), _MM), jax.ShapeDtypeStruct((nc, 8, 128, 128), jnp.float32),
                   jax.ShapeDtypeStruct((nc, 8, CHUNK, CHUNK), jnp.float32),
                   jax.ShapeDtypeStruct((t, 3072), jnp.float32)],
        scratch=[pltpu.VMEM((72, 3072), jnp.float32), pltpu.VMEM((8, 128, 128), jnp.float32)],
        args=[gate, qkv, qkv, sm, conv_w, dtb, alog, nw, cs["tri"], cs["i64"], cs["strict"]])


def gdn_bwd(gate, qkv, pre, sm, ss, ts, do, conv_w, dtb, alog, nw, cs):
    t = gate.shape[0]
    nc = t // CHUNK

    def body(shared, gate_ref, qkv_ref, pre_ref, sm_ref, ss_ref, ts_ref, do_ref, cw_ref, dtb_ref, alog_ref,
             nw_ref, tri_ref, i64_ref, strict_ref,
             dgate_ref, dqkv_ref, dsm_ref, dcw_ref, ddtb_ref, dalog_ref, dnw_ref,
             dbuf, carry, ds_scr):
        def init():
            ds_scr[...] = jnp.zeros_like(ds_scr)
            dcw_ref[...] = jnp.zeros_like(dcw_ref)
            ddtb_ref[...] = jnp.zeros_like(ddtb_ref)
            dalog_ref[...] = jnp.zeros_like(dalog_ref)
            dnw_ref[...] = jnp.zeros_like(dnw_ref)

        _when_first(shared, init)

        q_pre, k_pre, v_pre, g3 = _gdn_split(lambda c0, c1: pre_ref[:, c0:c1], gate_ref)
        consts = (tri_ref[...], i64_ref[...], strict_ref[...], ts_ref[0])

        def f(q_pre, k_pre, v_pre, g3, smv, s, dtb, alog, nwv):
            return _gdn_chunk(q_pre, k_pre, v_pre, g3, smv, s, dtb, alog, nwv, *consts)[:2]

        _, vjp = jax.vjp(f, q_pre, k_pre, v_pre, g3, sm_ref[...], ss_ref[0], dtb_ref[...], alog_ref[...], nw_ref[...])
        do3 = jnp.stack([do_ref[:, 128 * h:128 * h + 128] for h in range(GDN_HEADS)])
        dq, dk, dv, dg3, dsm, ds, ddtb, dalog, dnw = vjp((do3, ds_scr[...]))
        ds_scr[...] = ds
        for h in range(GDN_HEADS):
            dgate_ref[:, 128 * h:128 * h + 128] = dg3[h].astype(dgate_ref.dtype)
        dsm_ref[...] = (dsm + shared["dsm_ssd"]).astype(dsm_ref.dtype)
        ddtb_ref[0:1, :] += ddtb
        dalog_ref[0:1, :] += dalog
        dnw_ref[0:1, :] += dnw
        ranges = [(base + 128 * h, base + 128 * h + 128) for base in (0, 1024, 2048) for h in range(GDN_HEADS)]
        dlist = [d[h] for d in (dq, dk, dv) for h in range(GDN_HEADS)]
        _conv_bwd(dlist, ranges, dbuf, carry, qkv_ref, cw_ref, dqkv_ref, dcw_ref, None, shared["first"])

    ns = nc // SUB_BWD
    rblk = lambda w: pl.BlockSpec((SUB_BWD * CHUNK, w), lambda i: (ns - 1 - i, 0))
    acc = lambda w: pl.BlockSpec((8, w), lambda i: (0, 0))
    f32 = jnp.float32
    return dict(
        body=body,
        in_kinds=["rows"] * 4 + ["state", "state", "rows"] + ["full"] * 7, out_kinds=["rows"] * 3 + ["full"] * 4,
        in_specs=[rblk(1024), rblk(3072), rblk(3072), rblk(128),
                  pl.BlockSpec((SUB_BWD, 8, 128, 128), lambda i: (ns - 1 - i, 0, 0, 0)),
                  pl.BlockSpec((SUB_BWD, 8, CHUNK, CHUNK), lambda i: (ns - 1 - i, 0, 0, 0)), rblk(1024),
                  _full((CONV_K, 3072)), _full((1, 128)), _full((1, 128)), _full((1, 128)),
                  _full((64, 64)), _full((64, 64)), _full((64, 64))],
        out_specs=[rblk(1024), rblk(3072), rblk(128), acc(3072), acc(128), acc(128), acc(128)],
        out_shape=[jax.ShapeDtypeStruct((t, 1024), f32), jax.ShapeDtypeStruct((t, 3072), f32),
                   jax.ShapeDtypeStruct((t, 128), f32), jax.ShapeDtypeStruct((8, 3072), f32),
                   jax.ShapeDtypeStruct((8, 128), f32), jax.ShapeDtypeStruct((8, 128), f32),
                   jax.ShapeDtypeStruct((8, 128), f32)],
        scratch=[pltpu.VMEM((72, 3072), f32), pltpu.VMEM((8, 3072), f32), pltpu.VMEM((8, 128, 128), f32)],
        args=[gate, qkv, pre, sm, ss, ts, do, conv_w, dtb, alog, nw, cs["tri"], cs["i64"], cs["strict"]])


def _chunk_call(parts, name, nc, reverse):
    n_in = [len(p["args"]) for p in parts]
    n_out = [len(p["out_shape"]) for p in parts]
    n_scr = [len(p["scratch"]) for p in parts]
    sub = SUB_BWD if reverse else SUB_FWD
    order = list(range(sub))[::-1] if reverse else list(range(sub))

    def view(ref, kind, s, refs):
        if kind == "rows":
            return ref.at[pl.ds(CHUNK * s, CHUNK)]
        if kind == "state":
            return ref.at[pl.ds(s, 1)]
        if kind == "full":
            return ref
        src = refs[kind[1]]
        return ref if s == 0 else src.at[pl.ds(CHUNK * s - 8, 8)]

    def body(*refs):
        ins, outs, scr = refs[:sum(n_in)], refs[sum(n_in):sum(n_in) + sum(n_out)], refs[sum(n_in) + sum(n_out):]
        for s in order:
            shared = {"first": (pl.program_id(0) == 0) if s == order[0] else False}
            for k, p in enumerate(parts):
                i0, o0, s0 = sum(n_in[:k]), sum(n_out[:k]), sum(n_scr[:k])
                p_ins = ins[i0:i0 + n_in[k]]
                p["body"](shared,
                          *[view(r, kd, s, p_ins) for r, kd in zip(p_ins, p["in_kinds"])],
                          *[view(r, kd, s, None) for r, kd in zip(outs[o0:o0 + n_out[k]], p["out_kinds"])],
                          *scr[s0:s0 + n_scr[k]])

    cat = lambda key: [v for p in parts for v in p[key]]
    return _pc(body, name=name, grid=(nc // sub,), in_specs=cat("in_specs"), out_specs=cat("out_specs"),
               out_shape=cat("out_shape"), scratch_shapes=cat("scratch"),
               compiler_params=_cparams(("arbitrary",)))(*cat("args"))


def out_fwd_bwd(x, tgt, y_ssd, y_gdn, w_out, fnw):
    t = x.shape[0]
    tm = min(512, t)
    f32 = jnp.float32

    def body(x_ref, tgt_ref, ys_ref, yg_ref, w_ref, fnw_ref,
             dout_ref, dys_ref, dyg_ref, gw_ref, gfnw_ref, loss_ref, gw_acc):
        i = pl.program_id(0)

        @pl.when(i == 0)
        def _():
            gw_acc[...] = jnp.zeros_like(gw_acc)
            gfnw_ref[...] = jnp.zeros_like(gfnw_ref)
            loss_ref[...] = jnp.zeros_like(loss_ref)

        ys = ys_ref[...]
        yg = yg_ref[...]
        out = x_ref[...] + jnp.dot(ys, w_ref[0:1024, :], preferred_element_type=f32) \
            + jnp.dot(yg, w_ref[1024:2048, :], preferred_element_type=f32)
        rstd = lax.rsqrt(jnp.mean(out * out, axis=-1, keepdims=True) + EPS)
        yhat = out * rstd
        fw = fnw_ref[...]
        e = yhat * fw - tgt_ref[...]
        loss_ref[...] += 0.5 * jnp.sum(jnp.sum(e * e, axis=-1, keepdims=True) * (1.0 / D_MODEL), axis=0, keepdims=True)
        dyf = e * (1.0 / D_MODEL)
        gfnw_ref[0:1, :] += jnp.sum(dyf * yhat, axis=0, keepdims=True)
        dyhat = dyf * fw
        dout = rstd * (dyhat - yhat * jnp.mean(dyhat * yhat, axis=-1, keepdims=True))
        dout_ref[...] = dout
        db = dout.astype(_MM)
        dys_ref[...] = lax.dot_general(db, w_ref[0:1024, :], (((1,), (1,)), ((), ())), preferred_element_type=f32)
        dyg_ref[...] = lax.dot_general(db, w_ref[1024:2048, :], (((1,), (1,)), ((), ())), preferred_element_type=f32)
        gw_acc[0:1024, :] += lax.dot_general(ys, db, (((0,), (0,)), ((), ())), preferred_element_type=f32)
        gw_acc[1024:2048, :] += lax.dot_general(yg, db, (((0,), (0,)), ((), ())), preferred_element_type=f32)

        @pl.when(i == steps - 1)
        def _():
            gw_ref[...] = gw_acc[...].astype(gw_ref.dtype)

    steps = t // tm
    blk = pl.BlockSpec((tm, D_MODEL), lambda i: (i, 0))
    return _pc(
        body, name="out_fwd_bwd", grid=(steps,),
        in_specs=[blk, blk, blk, blk, _full((MIX_WIDTH, D_MODEL)), _full((1, D_MODEL))],
        out_specs=[blk, blk, blk, _full((MIX_WIDTH, D_MODEL)), _full((8, D_MODEL)), _full((1, 128))],
        out_shape=[jax.ShapeDtypeStruct((t, D_MODEL), f32)] * 3 +
                  [jax.ShapeDtypeStruct((MIX_WIDTH, D_MODEL), _MM), jax.ShapeDtypeStruct((8, D_MODEL), f32),
                   jax.ShapeDtypeStruct((1, 128), f32)],
        scratch_shapes=[pltpu.VMEM((MIX_WIDTH, D_MODEL), f32)],
        compiler_params=_cparams(("arbitrary",)),
    )(x, tgt, y_ssd, y_gdn, w_out, fnw)


def inproj_bwd_dx(x, dout, norm_w, w_perm, dgroups, scattered):
    t = x.shape[0]
    tm = min(256, t)
    f32 = jnp.float32

    def body(x_ref, dout_ref, nw_ref, w_ref, dz_ref, dxbc_ref, dgate_ref, dqkv_ref, dsm_ref, dx_ref, gnw_ref):
        i = pl.program_id(0)

        @pl.when(i == 0)
        def _():
            gnw_ref[...] = jnp.zeros_like(gnw_ref)

        du = None
        for (name, c0, c1), d_ref in zip(GROUPS, (dz_ref, dxbc_ref, dgate_ref, dqkv_ref, dsm_ref)):
            term = jnp.dot(d_ref[...].astype(_MM), _w_rows(w_ref, name, c1 - c0), preferred_element_type=f32)
            du = term if du is None else du + term
        xf = x_ref[...]
        rstd = lax.rsqrt(jnp.mean(xf * xf, axis=-1, keepdims=True) + EPS)
        xhat = xf * rstd
        gnw_ref[0:1, :] += jnp.sum(du * xhat, axis=0, keepdims=True)
        dxh = du * nw_ref[...]
        dx_ref[...] = dout_ref[...] + rstd * (dxh - xhat * jnp.mean(dxh * xhat, axis=-1, keepdims=True))

    blk = lambda w: pl.BlockSpec((tm, w), lambda i: (i, 0))
    steps = t // tm
    kinds = ["scatter"] * len(scattered)
    hosted = _hosting(body, 9, 2, 0, kinds, lambda: pl.program_id(0) == 0, lambda: pl.program_id(0) == steps - 1)
    return _pc_comm(
        hosted, name="inproj_bwd_dx", grid=(steps,),
        in_specs=[blk(D_MODEL), blk(D_MODEL), _full((1, D_MODEL)), _full((IN_DIM, D_MODEL))] +
                 [blk(c1 - c0) for _, c0, c1 in GROUPS] + [ANY] * len(scattered),
        out_specs=[blk(D_MODEL), _full((8, D_MODEL))] + [ANY] * len(scattered),
        out_shape=[jax.ShapeDtypeStruct((t, D_MODEL), f32), jax.ShapeDtypeStruct((8, D_MODEL), f32)] +
                  [_exchange_out_shape("scatter", a) for a in scattered],
        scratch_shapes=_exchange_sems(len(scattered)), compiler_params=_cparams(("arbitrary",)),
    )(x, dout, norm_w, w_perm, *dgroups, *scattered)


def grad_w_group(u, dg, name, scattered=()):
    t, n = dg.shape
    tn = 512 if n % 512 == 0 else n
    tm = 4096 if t % 4096 == 0 else t
    nj, nk = n // tn, t // tm
    f32 = jnp.float32

    def body(u_ref, d_ref, o_ref, acc):
        k = pl.program_id(1)

        @pl.when(k == 0)
        def _():
            acc[...] = jnp.zeros_like(acc)

        acc[...] += lax.dot_general(d_ref[...].astype(_MM), u_ref[...], (((0,), (0,)), ((), ())),
                                    preferred_element_type=f32)

        @pl.when(k == nk - 1)
        def _():
            o_ref[...] = acc[...].astype(o_ref.dtype)

    ne = len(scattered)
    hosted = _hosting(body, 2, 1, 1, ["scatter"] * ne,
                      lambda: (pl.program_id(0) == 0) & (pl.program_id(1) == 0),
                      lambda: (pl.program_id(0) == nj - 1) & (pl.program_id(1) == nk - 1))
    res = (_pc_comm if ne else _pc)(
        hosted, name=name, grid=(nj, nk),
        in_specs=[pl.BlockSpec((tm, D_MODEL), lambda j, k: (k, 0)),
                  pl.BlockSpec((tm, tn), lambda j, k: (k, j))] + [ANY] * ne,
        out_specs=[pl.BlockSpec((tn, D_MODEL), lambda j, k: (j, 0))] + [ANY] * ne,
        out_shape=[jax.ShapeDtypeStruct((n, D_MODEL), _MM)] + [_exchange_out_shape("scatter", a) for a in scattered],
        scratch_shapes=[pltpu.VMEM((tn, D_MODEL), f32)] + _exchange_sems(ne),
        compiler_params=_cparams(("arbitrary", "arbitrary")),
    )(u, dg, *scattered)
    return res if ne else res[0]


def _pad_lanes(v, off):
    n = v.shape[-1]
    return jnp.pad(v.reshape(1, n).astype(jnp.float32), ((0, 0), (off, 128 - off - n)))


REF_ROWS = dict(z=(0, 1024), xbc=(1024, 2560), dt=(2560, 2576), gate=(2576, 3600), qkv=(3600, 6672), ab=(6672, 6688))


def unperm_w_in(gz, gxbc, ggate, gqkv, gsm):
    src = dict(z=gz, xbc=gxbc, dt=gsm[0:16], gate=ggate, qkv=gqkv, ab=gsm[16:32])
    slabs = []
    for k in range(N_DEV):
        a, b = k * W_IN_SHARD, (k + 1) * W_IN_SHARD
        parts = []
        for name, (s, e) in REF_ROWS.items():
            lo, hi = max(a, s), min(b, e)
            if lo < hi:
                parts.append(src[name][lo - s:hi - s])
        slabs.append(jnp.concatenate(parts, axis=0))
    return jnp.stack(slabs)


def all_gather(arrs, name):
    n = len(arrs)

    def body(*refs):
        ins, outs = refs[:n], refs[n:2 * n]
        send_sems, recv_sems, local_sems = refs[2 * n:]
        x, y, c, me = _me()
        sibling = (x, y, 1 - c)
        chips = [(1 - x, y), (x, 1 - y), (1 - x, 1 - y)]

        def idx(px, py, pc):
            return 4 * px + 2 * py + pc

        def copy(a, k, block, to, src=None):
            slot = outs[a].at[idx(*block)]
            return pltpu.make_async_remote_copy(src_ref=slot if src is None else src, dst_ref=slot,
                                                send_sem=send_sems.at[a, k], recv_sem=recv_sems.at[a, k],
                                                device_id=to, device_id_type=MESH)

        local = [pltpu.make_async_copy(ins[a], outs[a].at[me], local_sems.at[a]) for a in range(n)]
        for cp in local:
            cp.start()
        started = []
        for a in range(n):
            first = [copy(a, 0, (x, y, c), sibling, src=ins[a])]
            first += [copy(a, 1 + j, (x, y, c), (*chip, c), src=ins[a]) for j, chip in enumerate(chips)]
            for cp in first:
                cp.start()
            started += first
        for a in range(n):
            for j, chip in enumerate(chips):
                copy(a, 1 + j, (*chip, c), (x, y, c)).wait_recv()
                fwd = copy(a, 4 + j, (*chip, c), sibling)
                fwd.start()
                started.append(fwd)
        for a in range(n):
            copy(a, 0, sibling, (x, y, c)).wait_recv()
            for j, chip in enumerate(chips):
                copy(a, 4 + j, (*chip, 1 - c), (x, y, c)).wait_recv()
        for cp in started:
            cp.wait_send()
        for cp in local:
            cp.wait()

    return _pc_comm(
        body, name=name, in_specs=[ANY] * n, out_specs=[ANY] * n,
        out_shape=[jax.ShapeDtypeStruct((N_DEV,) + a.shape, a.dtype) for a in arrs],
        scratch_shapes=[pltpu.SemaphoreType.DMA((n, 7)), pltpu.SemaphoreType.DMA((n, 7)),
                        pltpu.SemaphoreType.DMA((n,))],
    )(*arrs)


def adamw_sum(recv, w, m, v, rows, name, cols=None):
    r, ccols = w.shape
    f32 = jnp.float32
    c1 = 1.0 / (1.0 - ADAM_B1 ** ADAM_STEP)
    c2 = 1.0 / (1.0 - ADAM_B2 ** ADAM_STEP)

    def body(recv_ref, w_ref, m_ref, v_ref, g_ref, d_ref, mo_ref, vo_ref):
        g = recv_ref[0].astype(f32)
        for k in range(1, N_DEV):
            g = g + recv_ref[k].astype(f32)
        mn = ADAM_B1 * m_ref[...] + (1.0 - ADAM_B1) * g
        vn = ADAM_B2 * v_ref[...] + (1.0 - ADAM_B2) * (g * g)
        g_ref[...] = g
        mo_ref[...] = mn
        vo_ref[...] = vn
        d_ref[...] = -ADAM_LR * ((mn * c1) / (jnp.sqrt(vn * c2) + ADAM_EPS) + ADAM_WD * w_ref[...])

    if cols is None:
        blk = pl.BlockSpec((rows, ccols), lambda i: (i, 0))
        rblk, steps = pl.BlockSpec((N_DEV, rows, ccols), lambda i: (0, i, 0)), r // rows
    else:
        blk = pl.BlockSpec((r, cols), lambda i: (0, i))
        rblk, steps = pl.BlockSpec((N_DEV, r, cols), lambda i: (0, 0, i)), ccols // cols
    return _pc(
        body, name=name, grid=(steps,),
        in_specs=[rblk, blk, blk, blk],
        out_specs=[blk] * 4, out_shape=[jax.ShapeDtypeStruct((r, ccols), f32)] * 4,
        compiler_params=_cparams(("arbitrary",)),
    )(recv, w, m, v)


SMALL = (("norm_w", 1, 1024, 0), ("ssd_conv_b", 1, 1536, 0), ("ssd_dt_bias", 1, 16, 0), ("ssd_a_log", 1, 16, 0),
         ("ssd_d", 1, 16, 0), ("ssd_norm_w", 1, 1024, 0), ("gdn_dt_bias", 1, 8, 16), ("gdn_a_log", 1, 8, 16),
         ("gdn_norm_w", 1, 128, 0), ("final_norm_w", 1, 1024, 0),
         ("ssd_conv_w", CONV_K, SSD_CONV_DIM // N_DEV, 0), ("gdn_conv_w", CONV_K, GDN_CONV_DIM // N_DEV, 0))


def _small_layout():
    out, off = [], 0
    for name, rows, n, lane0 in SMALL + (("loss", 1, 128, 0),):
        stride = -(-(lane0 + n) // 128) * 128
        out.append((name, rows, n, lane0, stride, off))
        off += rows * stride
    return out, off


def scatter_small(accs):
    layout, total = _small_layout()
    f32 = jnp.float32

    def body(*refs):
        acc_refs, out_ref, slabs = refs[:len(layout)], refs[len(layout)], refs[len(layout) + 1]
        sems = refs[len(layout) + 2:]
        slabs[...] = jnp.zeros_like(slabs)
        for (name, rows, n, lane0, stride, off), acc in zip(layout, acc_refs):
            for k in range(N_DEV):
                if rows == 1:
                    slabs[k, :, off:off + stride] = acc[0:1, 0:stride]
                else:
                    for j in range(rows):
                        slabs[k, :, off + stride * j:off + stride * j + n] = acc[j:j + 1, n * k:n * k + n]
        start, wait = _exchange_ops("scatter", slabs, out_ref, *sems)
        start()
        wait()

    return _pc_comm(
        body, name="scatter_small_grads", out_specs=ANY, out_shape=jax.ShapeDtypeStruct((N_DEV, 1, total), f32),
        scratch_shapes=[pltpu.VMEM((N_DEV, 1, total), f32)] + _exchange_sems(1),
    )(*accs)


def adamw_small(recv, w, m, v):
    layout, total = _small_layout()
    loss_off = layout[-1][5]
    layout = layout[:-1]
    f32 = jnp.float32
    c1 = 1.0 / (1.0 - ADAM_B1 ** ADAM_STEP)
    c2 = 1.0 / (1.0 - ADAM_B2 ** ADAM_STEP)
    np_ = len(layout)

    def body(*refs):
        recv_ref = refs[0]
        w_refs, m_refs, v_refs = refs[1:1 + np_], refs[1 + np_:1 + 2 * np_], refs[1 + 2 * np_:1 + 3 * np_]
        o_refs = refs[1 + 3 * np_:]
        g_all = recv_ref[0]
        for k in range(1, N_DEV):
            g_all = g_all + recv_ref[k]
        o_refs[4 * np_][...] = g_all[:, loss_off:loss_off + 128]

        def update(g, wv, mv, vv):
            mn = ADAM_B1 * mv + (1.0 - ADAM_B1) * g
            vn = ADAM_B2 * vv + (1.0 - ADAM_B2) * (g * g)
            return g, -ADAM_LR * ((mn * c1) / (jnp.sqrt(vn * c2) + ADAM_EPS) + ADAM_WD * wv), mn, vn

        for p, (name, rows, n, lane0, stride, off) in enumerate(layout):
            outs = o_refs[4 * p:4 * p + 4]
            if rows == 1:
                res = update(g_all[:, off + lane0:off + lane0 + n], w_refs[p][...], m_refs[p][...], v_refs[p][...])
                for o, r in zip(outs, res):
                    o[...] = r
            else:
                for j in range(rows):
                    res = update(g_all[:, off + stride * j:off + stride * j + n], w_refs[p][0, j:j + 1, :],
                                 m_refs[p][0, j:j + 1, :], v_refs[p][0, j:j + 1, :])
                    for o, r in zip(outs, res):
                        o[0, j:j + 1, :] = r

    names = [e[0] for e in layout]
    ins = [recv] + [d[nm] for d in (w, m, v) for nm in names]
    out_shape = [jax.ShapeDtypeStruct(w[nm].shape, f32) for nm in names for _ in range(4)]
    out_shape.append(jax.ShapeDtypeStruct((1, 128), f32))
    res = _pc(body, name="adamw_small", out_shape=out_shape)(*ins)
    return {nm: tuple(res[4 * p:4 * p + 4]) for p, nm in enumerate(names)}, res[4 * np_]


SHARD = (("ssd_conv_w", CONV_K * SSD_CONV_DIM // N_DEV), ("gdn_conv_w", CONV_K * GDN_CONV_DIM // N_DEV))
SHARD_ROWS = 24


def _rows_of(size):
    return -(-size // 128)


def _pack(vals, layout, total_rows):
    parts = []
    for (name, size), val in zip(layout, vals):
        flat = val.reshape(-1).astype(jnp.float32)
        parts.append(jnp.pad(flat, (0, _rows_of(size) * 128 - size)).reshape(-1, 128))
    used = sum(_rows_of(s) for _, s in layout)
    parts.append(jnp.zeros((total_rows - used, 128), jnp.float32))
    return jnp.concatenate(parts, axis=0)


def _conv_full(gathered_flat, ccols):
    return gathered_flat.reshape(N_DEV, CONV_K, ccols // N_DEV).transpose(1, 0, 2).reshape(CONV_K, ccols)


def kernel(x, norm_w, w_in, ssd_conv_w, ssd_conv_b, ssd_dt_bias, ssd_a_log, ssd_d, ssd_norm_w, gdn_conv_w, gdn_dt_bias, gdn_a_log, gdn_norm_w, w_out, final_norm_w, loss_target, m_norm_w, m_w_in, m_ssd_conv_w, m_ssd_conv_b, m_ssd_dt_bias, m_ssd_a_log, m_ssd_d, m_ssd_norm_w, m_gdn_conv_w, m_gdn_dt_bias, m_gdn_a_log, m_gdn_norm_w, m_w_out, m_final_norm_w, v_norm_w, v_w_in, v_ssd_conv_w, v_ssd_conv_b, v_ssd_dt_bias, v_ssd_a_log, v_ssd_d, v_ssd_norm_w, v_gdn_conv_w, v_gdn_dt_bias, v_gdn_a_log, v_gdn_norm_w, v_w_out, v_final_norm_w):
    f32 = jnp.float32
    w = dict(norm_w=norm_w, w_in=w_in, ssd_conv_w=ssd_conv_w, ssd_conv_b=ssd_conv_b, ssd_dt_bias=ssd_dt_bias,
             ssd_a_log=ssd_a_log, ssd_d=ssd_d, ssd_norm_w=ssd_norm_w, gdn_conv_w=gdn_conv_w, gdn_dt_bias=gdn_dt_bias,
             gdn_a_log=gdn_a_log, gdn_norm_w=gdn_norm_w, w_out=w_out, final_norm_w=final_norm_w)
    m = dict(norm_w=m_norm_w, w_in=m_w_in, ssd_conv_w=m_ssd_conv_w, ssd_conv_b=m_ssd_conv_b, ssd_dt_bias=m_ssd_dt_bias,
             ssd_a_log=m_ssd_a_log, ssd_d=m_ssd_d, ssd_norm_w=m_ssd_norm_w, gdn_conv_w=m_gdn_conv_w,
             gdn_dt_bias=m_gdn_dt_bias, gdn_a_log=m_gdn_a_log, gdn_norm_w=m_gdn_norm_w, w_out=m_w_out,
             final_norm_w=m_final_norm_w)
    v = dict(norm_w=v_norm_w, w_in=v_w_in, ssd_conv_w=v_ssd_conv_w, ssd_conv_b=v_ssd_conv_b, ssd_dt_bias=v_ssd_dt_bias,
             ssd_a_log=v_ssd_a_log, ssd_d=v_ssd_d, ssd_norm_w=v_ssd_norm_w, gdn_conv_w=v_gdn_conv_w,
             gdn_dt_bias=v_gdn_dt_bias, gdn_a_log=v_gdn_a_log, gdn_norm_w=v_gdn_norm_w, w_out=v_w_out,
             final_norm_w=v_final_norm_w)
    names = list(w)
    shapes = {n: w[n].shape for n in names}

    xl, tgt = x[0], loss_target[0]
    cs = _consts()
    dtb_s = _pad_lanes(ssd_dt_bias, 0)
    alog_s = _pad_lanes(ssd_a_log, 0)
    dpar = _pad_lanes(ssd_d, 0)
    dtb_g = _pad_lanes(gdn_dt_bias, 16)
    alog_g = _pad_lanes(gdn_a_log, 16)
    nw_g = gdn_norm_w.reshape(1, 128)
    nw_s = ssd_norm_w.reshape(1, 1024)
    cb_s = ssd_conv_b.reshape(1, 1536)
    nw1 = norm_w.reshape(1, D_MODEL)

    (g_w_in,) = all_gather([w_in[0].T.astype(_MM)], "gather_w_in")
    w_perm = g_w_in.reshape(IN_DIM, D_MODEL)
    conv_pack = _pack([w["ssd_conv_w"], w["gdn_conv_w"]], SHARD, SHARD_ROWS)
    u, z, xbc, gate, qkv, sm, g_w_out, g_conv = inproj_fwd(xl, nw1, w_perm, [w_out[0].astype(_MM), conv_pack])
    w_out_full = g_w_out.reshape(MIX_WIDTH, D_MODEL)
    ssd_cw = _conv_full(g_conv[:, 0:6].reshape(N_DEV, -1), SSD_CONV_DIM)
    gdn_cw = _conv_full(g_conv[:, 6:18].reshape(N_DEV, -1), GDN_CONV_DIM)

    nc = xl.shape[0] // CHUNK
    y_ssd, hs, pre_s, y_gdn, ss, ts, pre_g = _chunk_call(
        [ssd_fwd(z, xbc, sm, ssd_cw, cb_s, dtb_s, alog_s, dpar, nw_s, cs),
         gdn_fwd(gate, qkv, sm, gdn_cw, dtb_g, alog_g, nw_g, cs)], "scan_fwd", nc, False)
    dout, dys, dyg, g_wout, g_fnw, loss_l = out_fwd_bwd(xl, tgt, y_ssd, y_gdn, w_out_full,
                                                        final_norm_w.reshape(1, D_MODEL))
    (dz, dxbc, g_cw_s, g_cb_s, g_dtb_s, g_alog_s, g_d, g_nw_s,
     dgate, dqkv, dsm, g_cw_g, g_dtb_g, g_alog_g, g_nw_g) = _chunk_call(
        [ssd_bwd(z, xbc, pre_s, sm, hs, dys, ssd_cw, dtb_s, alog_s, dpar, nw_s, cs),
         gdn_bwd(gate, qkv, pre_g, sm, ss, ts, dyg, gdn_cw, dtb_g, alog_g, nw_g, cs)], "scan_bwd", nc, True)

    t_w_out = g_wout.reshape(N_DEV, MIX_WIDTH // N_DEV, D_MODEL)
    gws = {}
    for dg, (name, _, _) in zip((dz, dxbc, dgate, dsm), (GROUPS[0], GROUPS[1], GROUPS[2], GROUPS[4])):
        gws[name] = grad_w_group(u, dg, "grad_w_in_" + name)
    gws["qkv"], r_w_out = grad_w_group(u, dqkv, "grad_w_in_qkv", [t_w_out])
    t_w_in = unperm_w_in(gws["z"], gws["xbc"], gws["gate"], gws["qkv"], gws["sm"])
    dx, g_nw, r_w_in = inproj_bwd_dx(xl, dout, nw1, w_perm, (dz, dxbc, dgate, dqkv, dsm), [t_w_in])

    accs = dict(norm_w=g_nw, ssd_conv_b=g_cb_s, ssd_dt_bias=g_dtb_s, ssd_a_log=g_alog_s, ssd_d=g_d,
                ssd_norm_w=g_nw_s, gdn_dt_bias=g_dtb_g, gdn_a_log=g_alog_g, gdn_norm_w=g_nw_g, final_norm_w=g_fnw,
                ssd_conv_w=g_cw_s, gdn_conv_w=g_cw_g)
    r_small = scatter_small([accs[e[0]] for e in SMALL] + [loss_l])

    o_w_in = adamw_sum(r_w_in, w_in[0].T, m_w_in[0].T, v_w_in[0].T, None, "adamw_w_in", cols=256)
    o_w_out = adamw_sum(r_w_out, w_out[0], m_w_out[0], v_w_out[0], 64, "adamw_w_out")
    row = lambda d: {n: (a.reshape(1, -1) if a.ndim == 1 else a) for n, a in d.items()}
    o_small, loss_sum = adamw_small(r_small, row(w), row(m), row(v))

    loss = loss_sum[0, 0]
    outs = [loss, dx[None]]
    for k in range(4):
        parts = {n: o_small[n][k] for n in o_small}
        parts["w_in"] = o_w_in[k].T
        parts["w_out"] = o_w_out[k]
        outs += [parts[n].reshape(shapes[n]) for n in names]
    return tuple(outs)
```

```python
import functools

import jax
import jax.numpy as jnp
import numpy as np
from jax import lax
from jax.experimental import pallas as pl
from jax.experimental.pallas import tpu as pltpu

_MM = jnp.bfloat16

D_MODEL = 1024
CHUNK = 64
CONV_K = 4
EPS = 1e-6
SSD_CONV_DIM = 1536
GDN_HEADS = 8
GDN_DK = 128
GDN_CONV_DIM = 3072
MIX_WIDTH = 2048
IN_DIM = 6688
N_DEV = 8
W_IN_SHARD = IN_DIM // N_DEV
HI = lax.Precision.HIGHEST
HIGH = lax.Precision.HIGH
VMEM_LIMIT = 56 * 1024 * 1024

ADAM_LR = 0.001
ADAM_B1 = 0.9
ADAM_B2 = 0.999
ADAM_EPS = 1e-08
ADAM_WD = 0.01
ADAM_STEP = 10


def _pc(body, **kw):
    return pl.pallas_call(body, **kw)


def _pc_comm(body, **kw):
    return pl.pallas_call(body, **kw)


def _cparams(sem):
    return pltpu.CompilerParams(dimension_semantics=sem, vmem_limit_bytes=VMEM_LIMIT)


def _sig(x):
    return 0.5 * jnp.tanh(0.5 * x) + 0.5


@jax.custom_vjp
def _sigmoid(x):
    return _sig(x)


def _sigmoid_fwd(x):
    s = _sig(x)
    return s, s


def _sigmoid_bwd(s, g):
    return (g * s * (1.0 - s),)


_sigmoid.defvjp(_sigmoid_fwd, _sigmoid_bwd)


@jax.custom_vjp
def _silu(x):
    return x * _sig(x)


def _silu_fwd(x):
    s = _sig(x)
    return x * s, (x, s)


def _silu_bwd(res, g):
    x, s = res
    return (g * (s * (1.0 + x * (1.0 - s))),)


_silu.defvjp(_silu_fwd, _silu_bwd)


def _softplus_impl(x):
    return jnp.maximum(x, 0.0) + jnp.log(1.0 + jnp.exp(-jnp.abs(x)))


@jax.custom_vjp
def _softplus(x):
    return _softplus_impl(x)


def _softplus_fwd(x):
    return _softplus_impl(x), x


def _softplus_bwd(x, g):
    return (g * _sig(x),)


_softplus.defvjp(_softplus_fwd, _softplus_bwd)


def _lane_bcast_impl(x, k):
    return jnp.broadcast_to(x[..., k:k + 1], x.shape)


@functools.partial(jax.custom_vjp, nondiff_argnums=(1,))
def _lane_bcast(x, k):
    return _lane_bcast_impl(x, k)


def _lane_bcast_fwd(x, k):
    return _lane_bcast_impl(x, k), None


def _lane_bcast_bwd(k, _, g):
    lane = lax.broadcasted_iota(jnp.int32, g.shape, g.ndim - 1)
    return (jnp.where(lane == k, jnp.sum(g, axis=-1, keepdims=True), 0.0),)


_lane_bcast.defvjp(_lane_bcast_fwd, _lane_bcast_bwd)


def _mm(a, b):
    return jnp.dot(a.astype(_MM), b.astype(_MM), preferred_element_type=jnp.float32)


def _mm_nt(a, b):
    return lax.dot_general(a.astype(_MM), b.astype(_MM), (((1,), (1,)), ((), ())),
                           preferred_element_type=jnp.float32)


def _mm_tn(a, b):
    return lax.dot_general(a.astype(_MM), b.astype(_MM), (((0,), (0,)), ((), ())),
                           preferred_element_type=jnp.float32)


def _dot_hi(a, b):
    return jnp.dot(a, b, precision=HI, preferred_element_type=jnp.float32)


def _bmm(a, b):
    return lax.dot_general(a.astype(_MM), b.astype(_MM), (((2,), (1,)), ((0,), (0,))),
                           preferred_element_type=jnp.float32)


def _bmm_nt(a, b):
    return lax.dot_general(a.astype(_MM), b.astype(_MM), (((2,), (2,)), ((0,), (0,))),
                           preferred_element_type=jnp.float32)


def _bmm_tn(a, b):
    return lax.dot_general(a.astype(_MM), b.astype(_MM), (((1,), (1,)), ((0,), (0,))),
                           preferred_element_type=jnp.float32)


def _bmm_hi(a, b):
    return lax.dot_general(a, b, (((2,), (1,)), ((0,), (0,))), precision=HIGH, preferred_element_type=jnp.float32)


def _bmm_nt_hi(a, b):
    return lax.dot_general(a, b, (((2,), (2,)), ((0,), (0,))), precision=HIGH, preferred_element_type=jnp.float32)


def _bmm_tn_hi(a, b):
    return lax.dot_general(a, b, (((1,), (1,)), ((0,), (0,))), precision=HIGH, preferred_element_type=jnp.float32)


def _consts():
    l = np.arange(CHUNK)
    tri = (l[:, None] >= l[None, :]).astype(np.float32)
    lane = np.arange(128)
    i2 =(l[:, None] == (lane[None, :] % 64)).astype(np.float32)
    mask2 = (l[:, None] >= (lane[None, :] % 64)).astype(np.float32)
    lo = (lane < 64).astype(np.float32)[None, :]
    i64 = np.eye(CHUNK, dtype=np.float32)
    strict = (l[:, None] > l[None, :]).astype(np.float32)
    return dict(tri=jnp.asarray(tri), i2=jnp.asarray(i2), mask2=jnp.asarray(mask2), lo=jnp.asarray(lo),
                i64=jnp.asarray(i64), strict=jnp.asarray(strict))


def _ssd_chunk(xs_pre, b_pre, c_pre, z, sm, ht, dtb, alog, dpar, nw, tri, i2, mask2, lo):
    lane = lax.broadcasted_iota(jnp.int32, (1, 128), 1)
    m16 = lane < 16
    dt = jnp.where(m16, _softplus(sm + dtb), 0.0)
    a_neg = -jnp.exp(alog)
    cum = _dot_hi(tri, dt * a_neg)
    row = lax.broadcasted_iota(jnp.int32, (CHUNK, 1), 0)
    is_last = row == CHUNK - 1
    hi = 1.0 - lo
    bm = [_silu(b) for b in b_pre]
    cm = [_silu(c) for c in c_pre]
    cb2 = [_mm_nt(cm[g], jnp.concatenate([bm[g], bm[g]], axis=0)) for g in range(2)]
    ht_g = [jnp.concatenate(ht[4 * g:4 * g + 4], axis=1) for g in range(2)]
    yoff_g = [_mm(cm[g], ht_g[g]) for g in range(2)]
    yg, xdec, clast = [], [], []
    for j in range(8):
        g, k4 = j // 4, j % 4
        pair = lambda v, j=j: jnp.where(lo > 0.5, _lane_bcast(v, 2 * j), _lane_bcast(v, 2 * j + 1))
        xs = _silu(xs_pre[j])
        dte = pair(dt)
        cume = pair(cum)
        cum_last = jnp.sum(jnp.where(is_last, cume, 0.0), axis=0, keepdims=True)
        xdt = xs * dte
        rowv = jnp.sum(cume * i2, axis=0, keepdims=True)
        lm = jnp.exp(jnp.where(mask2 > 0.5, cume - rowv, -jnp.inf))
        m = cb2[g] * lm
        xblk = jnp.concatenate([xdt * lo, xdt * hi], axis=0)
        y = _mm(m, xblk)
        y = y + yoff_g[g][:, 128 * k4:128 * k4 + 128] * jnp.exp(cume)
        y = y + pair(dpar) * xs
        yg.append(y * _silu(z[j]))
        xdec.append(xdt * jnp.exp(cum_last - cume))
        clast.append(cum_last)
    ht_next = []
    for g in range(2):
        st = _mm_tn(bm[g], jnp.concatenate(xdec[4 * g:4 * g + 4], axis=1))
        for k4 in range(4):
            j = 4 * g + k4
            ht_next.append(ht[j] * jnp.exp(clast[j]) + st[:, 128 * k4:128 * k4 + 128])
    outs = []
    for g in range(2):
        ss = sum(jnp.sum(yg[j] * yg[j], axis=-1, keepdims=True) for j in range(4 * g, 4 * g + 4))
        rs = lax.rsqrt(ss * (1.0 / 512.0) + EPS)
        for j in range(4 * g, 4 * g + 4):
            outs.append(yg[j] * rs * nw[j])
    return outs, ht_next


def _tri_inverse(a):
    eye = jnp.eye(CHUNK, dtype=jnp.float32)[None]
    p = eye - a
    x = _bmm_hi(a, a)
    for _ in range(4):
        both = _bmm_hi(jnp.concatenate([p, x], axis=1), x)
        p = p + both[:, :CHUNK]
        x = both[:, CHUNK:]
    return p + _bmm_hi(p, x)


def _solve_apply(t, r1, r2):
    both = _bmm_hi(t, jnp.concatenate([r1, r2], axis=-1))
    n = r1.shape[-1]
    return both[..., :n], both[..., n:]


@jax.custom_vjp
def _solve(a, r1, r2, t):
    return _solve_apply(t, r1, r2)


def _solve_fwd(a, r1, r2, t):
    u, w = _bmm_hi(t, r1), _bmm_hi(t, r2)
    return (u, w), (t, u, w)


def _solve_bwd(res, cts):
    t, u, w = res
    du, dw = cts
    dr1 = _bmm_tn_hi(t, du)
    dr2 = _bmm_tn_hi(t, dw)
    da = -(_bmm_nt_hi(dr1, u) + _bmm_nt_hi(dr2, w))
    return da, dr1, dr2, jnp.zeros_like(t)


_solve.defvjp(_solve_fwd, _solve_bwd)


def _gdn_chunk(q_pre, k_pre, v_pre, gate, sm, s, dtb, alog, nw, tri, i64, strict, t_in=None):
    lane = lax.broadcasted_iota(jnp.int32, (1, 128), 1)
    m_a = (lane >= 16) & (lane < 24)
    g_full = jnp.where(m_a, -jnp.exp(alog) * _softplus(sm + dtb), 0.0)
    gc = _dot_hi(tri, g_full)
    sig = _sigmoid(sm)
    heads = lambda f: jnp.concatenate([f(h)[None] for h in range(GDN_HEADS)], axis=0)
    gc3 = heads(lambda h: _lane_bcast(gc, 16 + h))
    beta3 = heads(lambda h: _lane_bcast(sig, 24 + h))
    q = _silu(q_pre)
    q = q * lax.rsqrt(jnp.sum(q * q, axis=-1, keepdims=True) + EPS) * (GDN_DK ** -0.5)
    k = _silu(k_pre)
    k = k * lax.rsqrt(jnp.sum(k * k, axis=-1, keepdims=True) + EPS)
    v = _silu(v_pre)
    gcl = gc3[:, :, :CHUNK]
    gc_row = jnp.sum(gcl * i64[None], axis=1, keepdims=True)
    incl = (strict + i64)[None] > 0.5
    decay = jnp.exp(jnp.where(incl, gcl - gc_row, -jnp.inf))
    kb = k * beta3
    a = jnp.where(strict[None] > 0.5, _bmm_nt(kb, k) * decay, 0.0)
    egc = jnp.exp(gc3)
    t = _tri_inverse(a) if t_in is None else t_in
    u, w = _solve(a, v * beta3, kb * egc, t)
    attn = _bmm_nt(q, k) * decay
    row = lax.broadcasted_iota(jnp.int32, (1, CHUNK, 1), 1)
    gl = jnp.sum(jnp.where(row == CHUNK - 1, gc3, 0.0), axis=1, keepdims=True)
    q_dec = q * egc
    k_dec = k * jnp.exp(gl - gc3)
    ws = _bmm(jnp.concatenate([w, q_dec], axis=1), s)
    v_new = u - ws[:, :CHUNK]
    o = ws[:, CHUNK:] + _bmm(attn, v_new)
    s_next = s * jnp.exp(gl) + _bmm_tn(k_dec, v_new)
    on = o * lax.rsqrt(jnp.mean(o * o, axis=-1, keepdims=True) + EPS) * nw
    return on * _silu(gate), s_next, t


def _conv_fwd(pbuf, w_ref, c0, c1):
    blk = pbuf[:, c0:c1]
    acc = w_ref[CONV_K - 1:CONV_K, c0:c1] * blk[8:72]
    for j in range(CONV_K - 1):
        acc = acc + w_ref[j:j + 1, c0:c1] * pltpu.roll(blk, CONV_K - 1 - j, axis=0)[8:72]
    return acc


MESH = pl.DeviceIdType.MESH
ANY = pl.BlockSpec(memory_space=pl.ANY)


def _me():
    x, y, c = lax.axis_index("x"), lax.axis_index("y"), lax.axis_index("c")
    return x, y, c, 4 * x + 2 * y + c


def _peer(r):
    x, y, c, _ = _me()
    px = 1 - x if r & 4 else x
    py = 1 - y if r & 2 else y
    pc = 1 - c if r & 1 else c
    return (px, py, pc), 4 * px + 2 * py + pc


def _exchange_ops(kind, in_ref, out_ref, send_sems, recv_sems, local_sem):
    me = _me()[3]
    local = pltpu.make_async_copy(in_ref.at[me] if kind == "scatter" else in_ref, out_ref.at[me], local_sem)
    sends, recvs = [], []
    for r in range(1, N_DEV):
        peer, pidx = _peer(r)
        src = in_ref.at[pidx] if kind == "scatter" else in_ref
        sems = dict(send_sem=send_sems.at[r - 1], recv_sem=recv_sems.at[r - 1], device_id=peer, device_id_type=MESH)
        sends.append(pltpu.make_async_remote_copy(src_ref=src, dst_ref=out_ref.at[me], **sems))
        recvs.append(pltpu.make_async_remote_copy(src_ref=src, dst_ref=out_ref.at[pidx], **sems))

    def start():
        local.start()
        for cp in sends:
            cp.start()

    def wait():
        for cp in recvs:
            cp.wait_recv()
        for cp in sends:
            cp.wait_send()
        local.wait()

    return start, wait


def _exchange_sems(n):
    return [pltpu.SemaphoreType.DMA((N_DEV - 1,)), pltpu.SemaphoreType.DMA((N_DEV - 1,)),
            pltpu.SemaphoreType.DMA(())] * n


def _exchange_out_shape(kind, a):
    return jax.ShapeDtypeStruct(a.shape if kind == "scatter" else (N_DEV,) + a.shape, a.dtype)


def _hosting(body, n_in, n_out, n_scratch, kinds, first, last):
    ne = len(kinds)

    def wrapped(*refs):
        ins, ex_in = refs[:n_in], refs[n_in:n_in + ne]
        o0 = n_in + ne
        outs, ex_out = refs[o0:o0 + n_out], refs[o0 + n_out:o0 + n_out + ne]
        s0 = o0 + n_out + ne
        scr, sems = refs[s0:s0 + n_scratch], refs[s0 + n_scratch:]
        ops = [_exchange_ops(kinds[e], ex_in[e], ex_out[e], *sems[3 * e:3 * e + 3]) for e in range(ne)]

        @pl.when(first())
        def _():
            for start, _ in ops:
                start()

        body(*ins, *outs, *scr)

        @pl.when(last())
        def _():
            for _, wait in ops:
                wait()

    return wrapped


GROUPS = (("z", 0, 1024), ("xbc", 1024, 2560), ("gate", 2560, 3584), ("qkv", 3584, 6656), ("sm", 6656, 6784))
GROUP_ROWS = dict(z=((0, 1024),), xbc=((1024, 2560),), gate=((2576, 3600),), qkv=((3600, 6672),),
                  sm=((2560, 2576), (6672, 6688)))


def _w_rows(w_ref, name, width):
    pieces = [w_ref[a:b, :] for a, b in GROUP_ROWS[name]]
    n = sum(b - a for a, b in GROUP_ROWS[name])
    if n < width:
        pieces.append(jnp.zeros((width - n, D_MODEL), w_ref.dtype))
    return pieces[0] if len(pieces) == 1 else jnp.concatenate(pieces, axis=0)


def inproj_fwd(x, norm_w, w_perm, gathered):
    t = x.shape[0]
    tm = min(512, t)
    steps = t // tm
    kinds = ["gather"] * len(gathered)

    def body(x_ref, nw_ref, w_ref, u_ref, z_ref, xbc_ref, gate_ref, qkv_ref, sm_ref):
        xf = x_ref[...]
        rstd = lax.rsqrt(jnp.mean(xf * xf, axis=-1, keepdims=True) + EPS)
        u = (xf * rstd * nw_ref[...]).astype(_MM)
        u_ref[...] = u
        for (name, c0, c1), o_ref in zip(GROUPS, (z_ref, xbc_ref, gate_ref, qkv_ref, sm_ref)):
            o_ref[...] = lax.dot_general(u, _w_rows(w_ref, name, c1 - c0), (((1,), (1,)), ((), ())),
                                         preferred_element_type=jnp.float32)

    outs = [jax.ShapeDtypeStruct((t, D_MODEL), _MM)] + [jax.ShapeDtypeStruct((t, c1 - c0), jnp.float32)
                                                        for _, c0, c1 in GROUPS]
    hosted = _hosting(body, 3, 6, 0, kinds, lambda: pl.program_id(0) == 0, lambda: pl.program_id(0) == steps - 1)
    return _pc_comm(
        hosted, name="inproj_fwd", grid=(steps,),
        in_specs=[pl.BlockSpec((tm, D_MODEL), lambda i: (i, 0)),
                  pl.BlockSpec((1, D_MODEL), lambda i: (0, 0)),
                  pl.BlockSpec((IN_DIM, D_MODEL), lambda i: (0, 0), pipeline_mode=pl.Buffered(1))] +
                 [ANY] * len(gathered),
        out_specs=[pl.BlockSpec((tm, D_MODEL), lambda i: (i, 0))] +
                  [pl.BlockSpec((tm, c1 - c0), lambda i: (i, 0)) for _, c0, c1 in GROUPS] + [ANY] * len(gathered),
        out_shape=outs + [_exchange_out_shape("gather", a) for a in gathered],
        scratch_shapes=_exchange_sems(len(gathered)), compiler_params=_cparams(("arbitrary",)),
    )(x, norm_w, w_perm, *gathered)


SUB_FWD = 4
SUB_BWD = 2


def _halo_spec(width, idx_fn):
    return pl.BlockSpec((8, width), lambda i: (jnp.maximum(idx_fn(i) * (SUB_FWD * CHUNK // 8) - 1, 0), 0))


def _when_first(shared, fn):
    if shared["first"] is not False:
        pl.when(shared["first"])(fn)


def _full(shape):
    nd = len(shape)
    return pl.BlockSpec(shape, lambda i: (0,) * nd)


def _ssd_split(pre_fn, z_ref, sm_ref):
    xs_pre = [pre_fn(128 * j, 128 * j + 128) for j in range(8)]
    b_pre = [pre_fn(1024 + 128 * g, 1152 + 128 * g) for g in range(2)]
    c_pre = [pre_fn(1280 + 128 * g, 1408 + 128 * g) for g in range(2)]
    z = [z_ref[:, 128 * j:128 * j + 128] for j in range(8)]
    return xs_pre, b_pre, c_pre, z, sm_ref[...]


def ssd_fwd(z, xbc, sm, conv_w, conv_b, dtb, alog, dpar, nw, cs):
    t = z.shape[0]
    nc = t // CHUNK

    def body(shared, z_ref, xbc_ref, halo_ref, sm_ref, cw_ref, cb_ref, dtb_ref, alog_ref, dpar_ref, nw_ref,
             tri_ref, i2_ref, mask2_ref, lo_ref, y_ref, hs_ref, pre_ref, pbuf, ht_scr):
        def init():
            ht_scr[...] = jnp.zeros_like(ht_scr)

        _when_first(shared, init)
        pbuf[0:8, :] = jnp.where(shared["first"], 0.0, halo_ref[...])
        pbuf[8:72, :] = xbc_ref[...]

        def pre_fn(c0, c1):
            pre = _conv_fwd(pbuf, cw_ref, c0, c1) + cb_ref[:, c0:c1]
            pre_ref[:, c0:c1] = pre
            return pre

        xs_pre, b_pre, c_pre, zz, smv = _ssd_split(pre_fn, z_ref, sm_ref)
        ht = [ht_scr[:, 128 * j:128 * j + 128] for j in range(8)]
        hs_ref[0] = ht_scr[...]
        nwl = [nw_ref[:, 128 * j:128 * j + 128] for j in range(8)]
        outs, ht_next = _ssd_chunk(xs_pre, b_pre, c_pre, zz, smv, ht, dtb_ref[...], alog_ref[...], dpar_ref[...],
                                   nwl, tri_ref[...], i2_ref[...], mask2_ref[...], lo_ref[...])
        for j in range(8):
            y_ref[:, 128 * j:128 * j + 128] = outs[j].astype(y_ref.dtype)
            ht_scr[:, 128 * j:128 * j + 128] = ht_next[j]

    blk = lambda w: pl.BlockSpec((SUB_FWD * CHUNK, w), lambda i: (i, 0))
    return dict(
        body=body,
        in_kinds=["rows", "rows", ("halo", 1), "rows"] + ["full"] * 10, out_kinds=["rows", "state", "rows"],
        in_specs=[blk(1024), blk(1536), _halo_spec(1536, lambda i: i), blk(128),
                  _full((CONV_K, 1536)), _full((1, 1536)), _full((1, 128)), _full((1, 128)), _full((1, 128)),
                  _full((1, 1024)), _full((64, 64)), _full((64, 128)), _full((64, 128)),
                  _full((1, 128))],
        out_specs=[blk(1024), pl.BlockSpec((SUB_FWD, 128, 1024), lambda i: (i, 0, 0)), blk(1536)],
        out_shape=[jax.ShapeDtypeStruct((t, 1024), _MM), jax.ShapeDtypeStruct((nc, 128, 1024), jnp.float32),
                   jax.ShapeDtypeStruct((t, 1536), jnp.float32)],
        scratch=[pltpu.VMEM((72, 1536), jnp.float32), pltpu.VMEM((128, 1024), jnp.float32)],
        args=[z, xbc, xbc, sm, conv_w, conv_b, dtb, alog, dpar, nw, cs["tri"], cs["i2"], cs["mask2"], cs["lo"]])


def _conv_bwd(dpre_list, col_ranges, dbuf, carry, x_ref, cw_ref, dx_ref, dcw_ref, dcb_ref, first):
    for dpre, (c0, c1) in zip(dpre_list, col_ranges):
        dbuf[0:64, c0:c1] = dpre
    dbuf[64:72, :] = jnp.where(first, 0.0, carry[...])
    carry[...] = dbuf[0:8, :]
    for (c0, c1) in col_ranges:
        xin = x_ref[:, c0:c1]
        blk = dbuf[:, c0:c1]
        acc = None
        for j in range(CONV_K):
            sh = blk[0:64] if j == CONV_K - 1 else pltpu.roll(blk, 72 - (CONV_K - 1 - j), axis=0)[0:64]
            term = cw_ref[j:j + 1, c0:c1] * sh
            acc = term if acc is None else acc + term
            dcw_ref[j:j + 1, c0:c1] += jnp.sum(xin * sh, axis=0, keepdims=True)
        dx_ref[:, c0:c1] = acc.astype(dx_ref.dtype)
        if dcb_ref is not None:
            dcb_ref[0:1, c0:c1] += jnp.sum(dbuf[0:64, c0:c1], axis=0, keepdims=True)


def ssd_bwd(z, xbc, pre, sm, hs, dy, conv_w, dtb, alog, dpar, nw, cs):
    t = z.shape[0]
    nc = t // CHUNK

    def body(shared, z_ref, xbc_ref, pre_ref, sm_ref, hs_ref, dy_ref, cw_ref, dtb_ref, alog_ref, dpar_ref, nw_ref,
             tri_ref, i2_ref, mask2_ref, lo_ref,
             dz_ref, dxbc_ref, dcw_ref, dcb_ref, ddtb_ref, dalog_ref, ddpar_ref, dnw_ref,
             dbuf, carry, dht_scr):
        def init():
            dht_scr[...] = jnp.zeros_like(dht_scr)
            dcw_ref[...] = jnp.zeros_like(dcw_ref)
            dcb_ref[...] = jnp.zeros_like(dcb_ref)
            ddtb_ref[...] = jnp.zeros_like(ddtb_ref)
            dalog_ref[...] = jnp.zeros_like(dalog_ref)
            ddpar_ref[...] = jnp.zeros_like(ddpar_ref)
            dnw_ref[...] = jnp.zeros_like(dnw_ref)

        _when_first(shared, init)
        pre_fn = lambda c0, c1: pre_ref[:, c0:c1]
        xs_pre, b_pre, c_pre, zz, smv = _ssd_split(pre_fn, z_ref, sm_ref)
        ht = [hs_ref[0, :, 128 * j:128 * j + 128] for j in range(8)]
        nwl = [nw_ref[:, 128 * j:128 * j + 128] for j in range(8)]
        consts = (tri_ref[...], i2_ref[...], mask2_ref[...], lo_ref[...])

        def f(xs_pre, b_pre, c_pre, zz, smv, ht, dtb, alog, dpar, nwl):
            return _ssd_chunk(xs_pre, b_pre, c_pre, zz, smv, ht, dtb, alog, dpar, nwl, *consts)

        _, vjp = jax.vjp(f, xs_pre, b_pre, c_pre, zz, smv, ht, dtb_ref[...], alog_ref[...], dpar_ref[...], nwl)
        dys = [dy_ref[:, 128 * j:128 * j + 128] for j in range(8)]
        dhts = [dht_scr[:, 128 * j:128 * j + 128] for j in range(8)]
        dxs, db, dc, dzz, dsm, dht, ddtb, dalog, ddpar, dnwl = vjp((dys, dhts))
        for j in range(8):
            dz_ref[:, 128 * j:128 * j + 128] = dzz[j].astype(dz_ref.dtype)
            dht_scr[:, 128 * j:128 * j + 128] = dht[j]
            dnw_ref[0:1, 128 * j:128 * j + 128] += dnwl[j]
        shared["dsm_ssd"] = dsm
        ddtb_ref[0:1, :] += ddtb
        dalog_ref[0:1, :] += dalog
        ddpar_ref[0:1, :] += ddpar
        ranges = ([(128 * j, 128 * j + 128) for j in range(8)] + [(1024 + 128 * g, 1152 + 128 * g) for g in range(2)]
                  + [(1280 + 128 * g, 1408 + 128 * g) for g in range(2)])
        _conv_bwd(dxs + db + dc, ranges, dbuf, carry, xbc_ref, cw_ref, dxbc_ref, dcw_ref, dcb_ref, shared["first"])

    ns = nc // SUB_BWD
    rblk = lambda w: pl.BlockSpec((SUB_BWD * CHUNK, w), lambda i: (ns - 1 - i, 0))
    acc = lambda w: pl.BlockSpec((8, w), lambda i: (0, 0))
    f32 = jnp.float32
    return dict(
        body=body,
        in_kinds=["rows"] * 4 + ["state", "rows"] + ["full"] * 9, out_kinds=["rows", "rows"] + ["full"] * 6,
        in_specs=[rblk(1024), rblk(1536), rblk(1536), rblk(128),
                  pl.BlockSpec((SUB_BWD, 128, 1024), lambda i: (ns - 1 - i, 0, 0)), rblk(1024),
                  _full((CONV_K, 1536)), _full((1, 128)), _full((1, 128)), _full((1, 128)),
                  _full((1, 1024)), _full((64, 64)), _full((64, 128)), _full((64, 128)),
                  _full((1, 128))],
        out_specs=[rblk(1024), rblk(1536), acc(1536), acc(1536), acc(128), acc(128), acc(128), acc(1024)],
        out_shape=[jax.ShapeDtypeStruct((t, 1024), f32), jax.ShapeDtypeStruct((t, 1536), f32),
                   jax.ShapeDtypeStruct((8, 1536), f32),
                   jax.ShapeDtypeStruct((8, 1536), f32), jax.ShapeDtypeStruct((8, 128), f32),
                   jax.ShapeDtypeStruct((8, 128), f32), jax.ShapeDtypeStruct((8, 128), f32),
                   jax.ShapeDtypeStruct((8, 1024), f32)],
        scratch=[pltpu.VMEM((72, 1536), f32), pltpu.VMEM((8, 1536), f32), pltpu.VMEM((128, 1024), f32)],
        args=[z, xbc, pre, sm, hs, dy, conv_w, dtb, alog, dpar, nw, cs["tri"], cs["i2"], cs["mask2"], cs["lo"]])


def _gdn_split(pre_fn, gate_ref):
    def heads(base):
        return jnp.stack([pre_fn(base + 128 * h, base + 128 * h + 128) for h in range(GDN_HEADS)])
    gate = jnp.stack([gate_ref[:, 128 * h:128 * h + 128] for h in range(GDN_HEADS)])
    return heads(0), heads(1024), heads(2048), gate


def gdn_fwd(gate, qkv, sm, conv_w, dtb, alog, nw, cs):
    t = gate.shape[0]
    nc = t // CHUNK

    def body(shared, gate_ref, qkv_ref, halo_ref, sm_ref, cw_ref, dtb_ref, alog_ref, nw_ref,
             tri_ref, i64_ref, strict_ref, o_ref, ss_ref, ts_ref, pre_ref, pbuf, s_scr):
        def init():
            s_scr[...] = jnp.zeros_like(s_scr)

        _when_first(shared, init)
        pbuf[0:8, :] = jnp.where(shared["first"], 0.0, halo_ref[...])
        pbuf[8:72, :] = qkv_ref[...]

        def pre_fn(c0, c1):
            pre = _conv_fwd(pbuf, cw_ref, c0, c1)
            pre_ref[:, c0:c1] = pre
            return pre

        q_pre, k_pre, v_pre, g3 = _gdn_split(pre_fn, gate_ref)
        s = s_scr[...]
        ss_ref[0] = s
        out, s_next, tinv = _gdn_chunk(q_pre, k_pre, v_pre, g3, sm_ref[...], s, dtb_ref[...], alog_ref[...],
                                       nw_ref[...], tri_ref[...], i64_ref[...], strict_ref[...])
        ts_ref[0] = tinv
        s_scr[...] = s_next
        for h in range(GDN_HEADS):
            o_ref[:, 128 * h:128 * h + 128] = out[h].astype(o_ref.dtype)

    blk = lambda w: pl.BlockSpec((SUB_FWD * CHUNK, w), lambda i: (i, 0))
    return dict(
        body=body,
        in_kinds=["rows", "rows", ("halo", 1), "rows"] + ["full"] * 7, out_kinds=["rows", "state", "state", "rows"],
        in_specs=[blk(1024), blk(3072), _halo_spec(3072, lambda i: i), blk(128),
                  _full((CONV_K, 3072)), _full((1, 128)), _full((1, 128)), _full((1, 128)),
                  _full((64, 64)), _full((64, 64)), _full((64, 64))],
        out_specs=[blk(1024), pl.BlockSpec((SUB_FWD, 8, 128, 128), lambda i: (i, 0, 0, 0)),
                   pl.BlockSpec((SUB_FWD, 8, CHUNK, CHUNK), lambda i: (i, 0, 0, 0)), blk(3072)],
        out_shape=[jax.ShapeDtypeStruct((t, 1024), _MM), jax.ShapeDtypeStruct((nc, 8, 128, 128), jnp.float32),
                   jax.ShapeDtypeStruct((nc, 8, CHUNK, CHUNK), jnp.float32),
                   jax.ShapeDtypeStruct((t, 3072), jnp.float32)],
        scratch=[pltpu.VMEM((72, 3072), jnp.float32), pltpu.VMEM((8, 128, 128), jnp.float32)],
        args=[gate, qkv, qkv, sm, conv_w, dtb, alog, nw, cs["tri"], cs["i64"], cs["strict"]])


def gdn_bwd(gate, qkv, pre, sm, ss, ts, do, conv_w, dtb, alog, nw, cs):
    t = gate.shape[0]
    nc = t // CHUNK

    def body(shared, gate_ref, qkv_ref, pre_ref, sm_ref, ss_ref, ts_ref, do_ref, cw_ref, dtb_ref, alog_ref,
             nw_ref, tri_ref, i64_ref, strict_ref,
             dgate_ref, dqkv_ref, dsm_ref, dcw_ref, ddtb_ref, dalog_ref, dnw_ref,
             dbuf, carry, ds_scr):
        def init():
            ds_scr[...] = jnp.zeros_like(ds_scr)
            dcw_ref[...] = jnp.zeros_like(dcw_ref)
            ddtb_ref[...] = jnp.zeros_like(ddtb_ref)
            dalog_ref[...] = jnp.zeros_like(dalog_ref)
            dnw_ref[...] = jnp.zeros_like(dnw_ref)

        _when_first(shared, init)

        q_pre, k_pre, v_pre, g3 = _gdn_split(lambda c0, c1: pre_ref[:, c0:c1], gate_ref)
        consts = (tri_ref[...], i64_ref[...], strict_ref[...], ts_ref[0])

        def f(q_pre, k_pre, v_pre, g3, smv, s, dtb, alog, nwv):
            return _gdn_chunk(q_pre, k_pre, v_pre, g3, smv, s, dtb, alog, nwv, *consts)[:2]

        _, vjp = jax.vjp(f, q_pre, k_pre, v_pre, g3, sm_ref[...], ss_ref[0], dtb_ref[...], alog_ref[...], nw_ref[...])
        do3 = jnp.stack([do_ref[:, 128 * h:128 * h + 128] for h in range(GDN_HEADS)])
        dq, dk, dv, dg3, dsm, ds, ddtb, dalog, dnw = vjp((do3, ds_scr[...]))
        ds_scr[...] = ds
        for h in range(GDN_HEADS):
            dgate_ref[:, 128 * h:128 * h + 128] = dg3[h].astype(dgate_ref.dtype)
        dsm_ref[...] = (dsm + shared["dsm_ssd"]).astype(dsm_ref.dtype)
        ddtb_ref[0:1, :] += ddtb
        dalog_ref[0:1, :] += dalog
        dnw_ref[0:1, :] += dnw
        ranges = [(base + 128 * h, base + 128 * h + 128) for base in (0, 1024, 2048) for h in range(GDN_HEADS)]
        dlist = [d[h] for d in (dq, dk, dv) for h in range(GDN_HEADS)]
        _conv_bwd(dlist, ranges, dbuf, carry, qkv_ref, cw_ref, dqkv_ref, dcw_ref, None, shared["first"])

    ns = nc // SUB_BWD
    rblk = lambda w: pl.BlockSpec((SUB_BWD * CHUNK, w), lambda i: (ns - 1 - i, 0))
    acc = lambda w: pl.BlockSpec((8, w), lambda i: (0, 0))
    f32 = jnp.float32
    return dict(
        body=body,
        in_kinds=["rows"] * 4 + ["state", "state", "rows"] + ["full"] * 7, out_kinds=["rows"] * 3 + ["full"] * 4,
        in_specs=[rblk(1024), rblk(3072), rblk(3072), rblk(128),
                  pl.BlockSpec((SUB_BWD, 8, 128, 128), lambda i: (ns - 1 - i, 0, 0, 0)),
                  pl.BlockSpec((SUB_BWD, 8, CHUNK, CHUNK), lambda i: (ns - 1 - i, 0, 0, 0)), rblk(1024),
                  _full((CONV_K, 3072)), _full((1, 128)), _full((1, 128)), _full((1, 128)),
                  _full((64, 64)), _full((64, 64)), _full((64, 64))],
        out_specs=[rblk(1024), rblk(3072), rblk(128), acc(3072), acc(128), acc(128), acc(128)],
        out_shape=[jax.ShapeDtypeStruct((t, 1024), f32), jax.ShapeDtypeStruct((t, 3072), f32),
                   jax.ShapeDtypeStruct((t, 128), f32), jax.ShapeDtypeStruct((8, 3072), f32),
                   jax.ShapeDtypeStruct((8, 128), f32), jax.ShapeDtypeStruct((8, 128), f32),
                   jax.ShapeDtypeStruct((8, 128), f32)],
        scratch=[pltpu.VMEM((72, 3072), f32), pltpu.VMEM((8, 3072), f32), pltpu.VMEM((8, 128, 128), f32)],
        args=[gate, qkv, pre, sm, ss, ts, do, conv_w, dtb, alog, nw, cs["tri"], cs["i64"], cs["strict"]])


def _chunk_call(parts, name, nc, reverse):
    n_in = [len(p["args"]) for p in parts]
    n_out = [len(p["out_shape"]) for p in parts]
    n_scr = [len(p["scratch"]) for p in parts]
    sub = SUB_BWD if reverse else SUB_FWD
    order = list(range(sub))[::-1] if reverse else list(range(sub))

    def view(ref, kind, s, refs):
        if kind == "rows":
            return ref.at[pl.ds(CHUNK * s, CHUNK)]
        if kind == "state":
            return ref.at[pl.ds(s, 1)]
        if kind == "full":
            return ref
        src = refs[kind[1]]
        return ref if s == 0 else src.at[pl.ds(CHUNK * s - 8, 8)]

    def body(*refs):
        ins, outs, scr = refs[:sum(n_in)], refs[sum(n_in):sum(n_in) + sum(n_out)], refs[sum(n_in) + sum(n_out):]
        for s in order:
            shared = {"first": (pl.program_id(0) == 0) if s == order[0] else False}
            for k, p in enumerate(parts):
                i0, o0, s0 = sum(n_in[:k]), sum(n_out[:k]), sum(n_scr[:k])
                p_ins = ins[i0:i0 + n_in[k]]
                p["body"](shared,
                          *[view(r, kd, s, p_ins) for r, kd in zip(p_ins, p["in_kinds"])],
                          *[view(r, kd, s, None) for r, kd in zip(outs[o0:o0 + n_out[k]], p["out_kinds"])],
                          *scr[s0:s0 + n_scr[k]])

    cat = lambda key: [v for p in parts for v in p[key]]
    return _pc(body, name=name, grid=(nc // sub,), in_specs=cat("in_specs"), out_specs=cat("out_specs"),
               out_shape=cat("out_shape"), scratch_shapes=cat("scratch"),
               compiler_params=_cparams(("arbitrary",)))(*cat("args"))


def out_fwd_bwd(x, tgt, y_ssd, y_gdn, w_out, fnw):
    t = x.shape[0]
    tm = min(512, t)
    f32 = jnp.float32

    def body(x_ref, tgt_ref, ys_ref, yg_ref, w_ref, fnw_ref,
             dout_ref, dys_ref, dyg_ref, gw_ref, gfnw_ref, loss_ref, gw_acc):
        i = pl.program_id(0)

        @pl.when(i == 0)
        def _():
            gw_acc[...] = jnp.zeros_like(gw_acc)
            gfnw_ref[...] = jnp.zeros_like(gfnw_ref)
            loss_ref[...] = jnp.zeros_like(loss_ref)

        ys = ys_ref[...]
        yg = yg_ref[...]
        out = x_ref[...] + jnp.dot(ys, w_ref[0:1024, :], preferred_element_type=f32) \
            + jnp.dot(yg, w_ref[1024:2048, :], preferred_element_type=f32)
        rstd = lax.rsqrt(jnp.mean(out * out, axis=-1, keepdims=True) + EPS)
        yhat = out * rstd
        fw = fnw_ref[...]
        e = yhat * fw - tgt_ref[...]
        loss_ref[...] += 0.5 * jnp.sum(jnp.sum(e * e, axis=-1, keepdims=True) * (1.0 / D_MODEL), axis=0, keepdims=True)
        dyf = e * (1.0 / D_MODEL)
        gfnw_ref[0:1, :] += jnp.sum(dyf * yhat, axis=0, keepdims=True)
        dyhat = dyf * fw
        dout = rstd * (dyhat - yhat * jnp.mean(dyhat * yhat, axis=-1, keepdims=True))
        dout_ref[...] = dout
        db = dout.astype(_MM)
        dys_ref[...] = lax.dot_general(db, w_ref[0:1024, :], (((1,), (1,)), ((), ())), preferred_element_type=f32)
        dyg_ref[...] = lax.dot_general(db, w_ref[1024:2048, :], (((1,), (1,)), ((), ())), preferred_element_type=f32)
        gw_acc[0:1024, :] += lax.dot_general(ys, db, (((0,), (0,)), ((), ())), preferred_element_type=f32)
        gw_acc[1024:2048, :] += lax.dot_general(yg, db, (((0,), (0,)), ((), ())), preferred_element_type=f32)

        @pl.when(i == steps - 1)
        def _():
            gw_ref[...] = gw_acc[...].astype(gw_ref.dtype)

    steps = t // tm
    blk = pl.BlockSpec((tm, D_MODEL), lambda i: (i, 0))
    return _pc(
        body, name="out_fwd_bwd", grid=(steps,),
        in_specs=[blk, blk, blk, blk, _full((MIX_WIDTH, D_MODEL)), _full((1, D_MODEL))],
        out_specs=[blk, blk, blk, _full((MIX_WIDTH, D_MODEL)), _full((8, D_MODEL)), _full((1, 128))],
        out_shape=[jax.ShapeDtypeStruct((t, D_MODEL), f32)] * 3 +
                  [jax.ShapeDtypeStruct((MIX_WIDTH, D_MODEL), _MM), jax.ShapeDtypeStruct((8, D_MODEL), f32),
                   jax.ShapeDtypeStruct((1, 128), f32)],
        scratch_shapes=[pltpu.VMEM((MIX_WIDTH, D_MODEL), f32)],
        compiler_params=_cparams(("arbitrary",)),
    )(x, tgt, y_ssd, y_gdn, w_out, fnw)


def inproj_bwd_dx(x, dout, norm_w, w_perm, dgroups, scattered):
    t = x.shape[0]
    tm = min(256, t)
    f32 = jnp.float32

    def body(x_ref, dout_ref, nw_ref, w_ref, dz_ref, dxbc_ref, dgate_ref, dqkv_ref, dsm_ref, dx_ref, gnw_ref):
        i = pl.program_id(0)

        @pl.when(i == 0)
        def _():
            gnw_ref[...] = jnp.zeros_like(gnw_ref)

        du = None
        for (name, c0, c1), d_ref in zip(GROUPS, (dz_ref, dxbc_ref, dgate_ref, dqkv_ref, dsm_ref)):
            term = jnp.dot(d_ref[...].astype(_MM), _w_rows(w_ref, name, c1 - c0), preferred_element_type=f32)
            du = term if du is None else du + term
        xf = x_ref[...]
        rstd = lax.rsqrt(jnp.mean(xf * xf, axis=-1, keepdims=True) + EPS)
        xhat = xf * rstd
        gnw_ref[0:1, :] += jnp.sum(du * xhat, axis=0, keepdims=True)
        dxh = du * nw_ref[...]
        dx_ref[...] = dout_ref[...] + rstd * (dxh - xhat * jnp.mean(dxh * xhat, axis=-1, keepdims=True))

    blk = lambda w: pl.BlockSpec((tm, w), lambda i: (i, 0))
    steps = t // tm
    kinds = ["scatter"] * len(scattered)
    hosted = _hosting(body, 9, 2, 0, kinds, lambda: pl.program_id(0) == 0, lambda: pl.program_id(0) == steps - 1)
    return _pc_comm(
        hosted, name="inproj_bwd_dx", grid=(steps,),
        in_specs=[blk(D_MODEL), blk(D_MODEL), _full((1, D_MODEL)), _full((IN_DIM, D_MODEL))] +
                 [blk(c1 - c0) for _, c0, c1 in GROUPS] + [ANY] * len(scattered),
        out_specs=[blk(D_MODEL), _full((8, D_MODEL))] + [ANY] * len(scattered),
        out_shape=[jax.ShapeDtypeStruct((t, D_MODEL), f32), jax.ShapeDtypeStruct((8, D_MODEL), f32)] +
                  [_exchange_out_shape("scatter", a) for a in scattered],
        scratch_shapes=_exchange_sems(len(scattered)), compiler_params=_cparams(("arbitrary",)),
    )(x, dout, norm_w, w_perm, *dgroups, *scattered)


def grad_w_group(u, dg, name, scattered=()):
    t, n = dg.shape
    tn = 512 if n % 512 == 0 else n
    tm = 4096 if t % 4096 == 0 else t
    nj, nk = n // tn, t // tm
    f32 = jnp.float32

    def body(u_ref, d_ref, o_ref, acc):
        k = pl.program_id(1)

        @pl.when(k == 0)
        def _():
            acc[...] = jnp.zeros_like(acc)

        acc[...] += lax.dot_general(d_ref[...].astype(_MM), u_ref[...], (((0,), (0,)), ((), ())),
                                    preferred_element_type=f32)

        @pl.when(k == nk - 1)
        def _():
            o_ref[...] = acc[...].astype(o_ref.dtype)

    ne = len(scattered)
    hosted = _hosting(body, 2, 1, 1, ["scatter"] * ne,
                      lambda: (pl.program_id(0) == 0) & (pl.program_id(1) == 0),
                      lambda: (pl.program_id(0) == nj - 1) & (pl.program_id(1) == nk - 1))
    res = (_pc_comm if ne else _pc)(
        hosted, name=name, grid=(nj, nk),
        in_specs=[pl.BlockSpec((tm, D_MODEL), lambda j, k: (k, 0)),
                  pl.BlockSpec((tm, tn), lambda j, k: (k, j))] + [ANY] * ne,
        out_specs=[pl.BlockSpec((tn, D_MODEL), lambda j, k: (j, 0))] + [ANY] * ne,
        out_shape=[jax.ShapeDtypeStruct((n, D_MODEL), _MM)] + [_exchange_out_shape("scatter", a) for a in scattered],
        scratch_shapes=[pltpu.VMEM((tn, D_MODEL), f32)] + _exchange_sems(ne),
        compiler_params=_cparams(("arbitrary", "arbitrary")),
    )(u, dg, *scattered)
    return res if ne else res[0]


def _pad_lanes(v, off):
    n = v.shape[-1]
    return jnp.pad(v.reshape(1, n).astype(jnp.float32), ((0, 0), (off, 128 - off - n)))


REF_ROWS = dict(z=(0, 1024), xbc=(1024, 2560), dt=(2560, 2576), gate=(2576, 3600), qkv=(3600, 6672), ab=(6672, 6688))


def unperm_w_in(gz, gxbc, ggate, gqkv, gsm):
    src = dict(z=gz, xbc=gxbc, dt=gsm[0:16], gate=ggate, qkv=gqkv, ab=gsm[16:32])
    slabs = []
    for k in range(N_DEV):
        a, b = k * W_IN_SHARD, (k + 1) * W_IN_SHARD
        parts = []
        for name, (s, e) in REF_ROWS.items():
            lo, hi = max(a, s), min(b, e)
            if lo < hi:
                parts.append(src[name][lo - s:hi - s])
        slabs.append(jnp.concatenate(parts, axis=0))
    return jnp.stack(slabs)


def all_gather(arrs, name):
    n = len(arrs)

    def body(*refs):
        ins, outs = refs[:n], refs[n:2 * n]
        send_sems, recv_sems, local_sems = refs[2 * n:]
        x, y, c, me = _me()
        sibling = (x, y, 1 - c)
        chips = [(1 - x, y), (x, 1 - y), (1 - x, 1 - y)]

        def idx(px, py, pc):
            return 4 * px + 2 * py + pc

        def copy(a, k, block, to, src=None):
            slot = outs[a].at[idx(*block)]
            return pltpu.make_async_remote_copy(src_ref=slot if src is None else src, dst_ref=slot,
                                                send_sem=send_sems.at[a, k], recv_sem=recv_sems.at[a, k],
                                                device_id=to, device_id_type=MESH)

        local = [pltpu.make_async_copy(ins[a], outs[a].at[me], local_sems.at[a]) for a in range(n)]
        for cp in local:
            cp.start()
        started = []
        for a in range(n):
            first = [copy(a, 0, (x, y, c), sibling, src=ins[a])]
            first += [copy(a, 1 + j, (x, y, c), (*chip, c), src=ins[a]) for j, chip in enumerate(chips)]
            for cp in first:
                cp.start()
            started += first
        for a in range(n):
            for j, chip in enumerate(chips):
                copy(a, 1 + j, (*chip, c), (x, y, c)).wait_recv()
                fwd = copy(a, 4 + j, (*chip, c), sibling)
                fwd.start()
                started.append(fwd)
        for a in range(n):
            copy(a, 0, sibling, (x, y, c)).wait_recv()
            for j, chip in enumerate(chips):
                copy(a, 4 + j, (*chip, 1 - c), (x, y, c)).wait_recv()
        for cp in started:
            cp.wait_send()
        for cp in local:
            cp.wait()

    return _pc_comm(
        body, name=name, in_specs=[ANY] * n, out_specs=[ANY] * n,
        out_shape=[jax.ShapeDtypeStruct((N_DEV,) + a.shape, a.dtype) for a in arrs],
        scratch_shapes=[pltpu.SemaphoreType.DMA((n, 7)), pltpu.SemaphoreType.DMA((n, 7)),
                        pltpu.SemaphoreType.DMA((n,))],
    )(*arrs)


def adamw_sum(recv, w, m, v, rows, name, cols=None):
    r, ccols = w.shape
    f32 = jnp.float32
    c1 = 1.0 / (1.0 - ADAM_B1 ** ADAM_STEP)
    c2 = 1.0 / (1.0 - ADAM_B2 ** ADAM_STEP)

    def body(recv_ref, w_ref, m_ref, v_ref, g_ref, d_ref, mo_ref, vo_ref):
        g = recv_ref[0].astype(f32)
        for k in range(1, N_DEV):
            g = g + recv_ref[k].astype(f32)
        mn = ADAM_B1 * m_ref[...] + (1.0 - ADAM_B1) * g
        vn = ADAM_B2 * v_ref[...] + (1.0 - ADAM_B2) * (g * g)
        g_ref[...] = g
        mo_ref[...] = mn
        vo_ref[...] = vn
        d_ref[...] = -ADAM_LR * ((mn * c1) / (jnp.sqrt(vn * c2) + ADAM_EPS) + ADAM_WD * w_ref[...])

    if cols is None:
        blk = pl.BlockSpec((rows, ccols), lambda i: (i, 0))
        rblk, steps = pl.BlockSpec((N_DEV, rows, ccols), lambda i: (0, i, 0)), r // rows
    else:
        blk = pl.BlockSpec((r, cols), lambda i: (0, i))
        rblk, steps = pl.BlockSpec((N_DEV, r, cols), lambda i: (0, 0, i)), ccols // cols
    return _pc(
        body, name=name, grid=(steps,),
        in_specs=[rblk, blk, blk, blk],
        out_specs=[blk] * 4, out_shape=[jax.ShapeDtypeStruct((r, ccols), f32)] * 4,
        compiler_params=_cparams(("arbitrary",)),
    )(recv, w, m, v)


SMALL = (("norm_w", 1, 1024, 0), ("ssd_conv_b", 1, 1536, 0), ("ssd_dt_bias", 1, 16, 0), ("ssd_a_log", 1, 16, 0),
         ("ssd_d", 1, 16, 0), ("ssd_norm_w", 1, 1024, 0), ("gdn_dt_bias", 1, 8, 16), ("gdn_a_log", 1, 8, 16),
         ("gdn_norm_w", 1, 128, 0), ("final_norm_w", 1, 1024, 0),
         ("ssd_conv_w", CONV_K, SSD_CONV_DIM // N_DEV, 0), ("gdn_conv_w", CONV_K, GDN_CONV_DIM // N_DEV, 0))


def _small_layout():
    out, off = [], 0
    for name, rows, n, lane0 in SMALL + (("loss", 1, 128, 0),):
        stride = -(-(lane0 + n) // 128) * 128
        out.append((name, rows, n, lane0, stride, off))
        off += rows * stride
    return out, off


def scatter_small(accs):
    layout, total = _small_layout()
    f32 = jnp.float32

    def body(*refs):
        acc_refs, out_ref, slabs = refs[:len(layout)], refs[len(layout)], refs[len(layout) + 1]
        sems = refs[len(layout) + 2:]
        slabs[...] = jnp.zeros_like(slabs)
        for (name, rows, n, lane0, stride, off), acc in zip(layout, acc_refs):
            for k in range(N_DEV):
                if rows == 1:
                    slabs[k, :, off:off + stride] = acc[0:1, 0:stride]
                else:
                    for j in range(rows):
                        slabs[k, :, off + stride * j:off + stride * j + n] = acc[j:j + 1, n * k:n * k + n]
        start, wait = _exchange_ops("scatter", slabs, out_ref, *sems)
        start()
        wait()

    return _pc_comm(
        body, name="scatter_small_grads", out_specs=ANY, out_shape=jax.ShapeDtypeStruct((N_DEV, 1, total), f32),
        scratch_shapes=[pltpu.VMEM((N_DEV, 1, total), f32)] + _exchange_sems(1),
    )(*accs)


def adamw_small(recv, w, m, v):
    layout, total = _small_layout()
    loss_off = layout[-1][5]
    layout = layout[:-1]
    f32 = jnp.float32
    c1 = 1.0 / (1.0 - ADAM_B1 ** ADAM_STEP)
    c2 = 1.0 / (1.0 - ADAM_B2 ** ADAM_STEP)
    np_ = len(layout)

    def body(*refs):
        recv_ref = refs[0]
        w_refs, m_refs, v_refs = refs[1:1 + np_], refs[1 + np_:1 + 2 * np_], refs[1 + 2 * np_:1 + 3 * np_]
        o_refs = refs[1 + 3 * np_:]
        g_all = recv_ref[0]
        for k in range(1, N_DEV):
            g_all = g_all + recv_ref[k]
        o_refs[4 * np_][...] = g_all[:, loss_off:loss_off + 128]

        def update(g, wv, mv, vv):
            mn = ADAM_B1 * mv + (1.0 - ADAM_B1) * g
            vn = ADAM_B2 * vv + (1.0 - ADAM_B2) * (g * g)
            return g, -ADAM_LR * ((mn * c1) / (jnp.sqrt(vn * c2) + ADAM_EPS) + ADAM_WD * wv), mn, vn

        for p, (name, rows, n, lane0, stride, off) in enumerate(layout):
            outs = o_refs[4 * p:4 * p + 4]
            if rows == 1:
                res = update(g_all[:, off + lane0:off + lane0 + n], w_refs[p][...], m_refs[p][...], v_refs[p][...])
                for o, r in zip(outs, res):
                    o[...] = r
            else:
                for j in range(rows):
                    res = update(g_all[:, off + stride * j:off + stride * j + n], w_refs[p][0, j:j + 1, :],
                                 m_refs[p][0, j:j + 1, :], v_refs[p][0, j:j + 1, :])
                    for o, r in zip(outs, res):
                        o[0, j:j + 1, :] = r

    names = [e[0] for e in layout]
    ins = [recv] + [d[nm] for d in (w, m, v) for nm in names]
    out_shape = [jax.ShapeDtypeStruct(w[nm].shape, f32) for nm in names for _ in range(4)]
    out_shape.append(jax.ShapeDtypeStruct((1, 128), f32))
    res = _pc(body, name="adamw_small", out_shape=out_shape)(*ins)
    return {nm: tuple(res[4 * p:4 * p + 4]) for p, nm in enumerate(names)}, res[4 * np_]


SHARD = (("ssd_conv_w", CONV_K * SSD_CONV_DIM // N_DEV), ("gdn_conv_w", CONV_K * GDN_CONV_DIM // N_DEV))
SHARD_ROWS = 24


def _rows_of(size):
    return -(-size // 128)


def _pack(vals, layout, total_rows):
    parts = []
    for (name, size), val in zip(layout, vals):
        flat = val.reshape(-1).astype(jnp.float32)
        parts.append(jnp.pad(flat, (0, _rows_of(size) * 128 - size)).reshape(-1, 128))
    used = sum(_rows_of(s) for _, s in layout)
    parts.append(jnp.zeros((total_rows - used, 128), jnp.float32))
    return jnp.concatenate(parts, axis=0)


def _conv_full(gathered_flat, ccols):
    return gathered_flat.reshape(N_DEV, CONV_K, ccols // N_DEV).transpose(1, 0, 2).reshape(CONV_K, ccols)


def kernel(x, norm_w, w_in, ssd_conv_w, ssd_conv_b, ssd_dt_bias, ssd_a_log, ssd_d, ssd_norm_w, gdn_conv_w, gdn_dt_bias, gdn_a_log, gdn_norm_w, w_out, final_norm_w, loss_target, m_norm_w, m_w_in, m_ssd_conv_w, m_ssd_conv_b, m_ssd_dt_bias, m_ssd_a_log, m_ssd_d, m_ssd_norm_w, m_gdn_conv_w, m_gdn_dt_bias, m_gdn_a_log, m_gdn_norm_w, m_w_out, m_final_norm_w, v_norm_w, v_w_in, v_ssd_conv_w, v_ssd_conv_b, v_ssd_dt_bias, v_ssd_a_log, v_ssd_d, v_ssd_norm_w, v_gdn_conv_w, v_gdn_dt_bias, v_gdn_a_log, v_gdn_norm_w, v_w_out, v_final_norm_w):
    f32 = jnp.float32
    w = dict(norm_w=norm_w, w_in=w_in, ssd_conv_w=ssd_conv_w, ssd_conv_b=ssd_conv_b, ssd_dt_bias=ssd_dt_bias,
             ssd_a_log=ssd_a_log, ssd_d=ssd_d, ssd_norm_w=ssd_norm_w, gdn_conv_w=gdn_conv_w, gdn_dt_bias=gdn_dt_bias,
             gdn_a_log=gdn_a_log, gdn_norm_w=gdn_norm_w, w_out=w_out, final_norm_w=final_norm_w)
    m = dict(norm_w=m_norm_w, w_in=m_w_in, ssd_conv_w=m_ssd_conv_w, ssd_conv_b=m_ssd_conv_b, ssd_dt_bias=m_ssd_dt_bias,
             ssd_a_log=m_ssd_a_log, ssd_d=m_ssd_d, ssd_norm_w=m_ssd_norm_w, gdn_conv_w=m_gdn_conv_w,
             gdn_dt_bias=m_gdn_dt_bias, gdn_a_log=m_gdn_a_log, gdn_norm_w=m_gdn_norm_w, w_out=m_w_out,
             final_norm_w=m_final_norm_w)
    v = dict(norm_w=v_norm_w, w_in=v_w_in, ssd_conv_w=v_ssd_conv_w, ssd_conv_b=v_ssd_conv_b, ssd_dt_bias=v_ssd_dt_bias,
             ssd_a_log=v_ssd_a_log, ssd_d=v_ssd_d, ssd_norm_w=v_ssd_norm_w, gdn_conv_w=v_gdn_conv_w,
             gdn_dt_bias=v_gdn_dt_bias, gdn_a_log=v_gdn_a_log, gdn_norm_w=v_gdn_norm_w, w_out=v_w_out,
             final_norm_w=v_final_norm_w)
    names = list(w)
    shapes = {n: w[n].shape for n in names}

    xl, tgt = x[0], loss_target[0]
    cs = _consts()
    dtb_s = _pad_lanes(ssd_dt_bias, 0)
    alog_s = _pad_lanes(ssd_a_log, 0)
    dpar = _pad_lanes(ssd_d, 0)
    dtb_g = _pad_lanes(gdn_dt_bias, 16)
    alog_g = _pad_lanes(gdn_a_log, 16)
    nw_g = gdn_norm_w.reshape(1, 128)
    nw_s = ssd_norm_w.reshape(1, 1024)
    cb_s = ssd_conv_b.reshape(1, 1536)
    nw1 = norm_w.reshape(1, D_MODEL)

    (g_w_in,) = all_gather([w_in[0].T.astype(_MM)], "gather_w_in")
    w_perm = g_w_in.reshape(IN_DIM, D_MODEL)
    conv_pack = _pack([w["ssd_conv_w"], w["gdn_conv_w"]], SHARD, SHARD_ROWS)
    u, z, xbc, gate, qkv, sm, g_w_out, g_conv = inproj_fwd(xl, nw1, w_perm, [w_out[0].astype(_MM), conv_pack])
    w_out_full = g_w_out.reshape(MIX_WIDTH, D_MODEL)
    ssd_cw = _conv_full(g_conv[:, 0:6].reshape(N_DEV, -1), SSD_CONV_DIM)
    gdn_cw = _conv_full(g_conv[:, 6:18].reshape(N_DEV, -1), GDN_CONV_DIM)

    nc = xl.shape[0] // CHUNK
    y_ssd, hs, pre_s, y_gdn, ss, ts, pre_g = _chunk_call(
        [ssd_fwd(z, xbc, sm, ssd_cw, cb_s, dtb_s, alog_s, dpar, nw_s, cs),
         gdn_fwd(gate, qkv, sm, gdn_cw, dtb_g, alog_g, nw_g, cs)], "scan_fwd", nc, False)
    dout, dys, dyg, g_wout, g_fnw, loss_l = out_fwd_bwd(xl, tgt, y_ssd, y_gdn, w_out_full,
                                                        final_norm_w.reshape(1, D_MODEL))
    (dz, dxbc, g_cw_s, g_cb_s, g_dtb_s, g_alog_s, g_d, g_nw_s,
     dgate, dqkv, dsm, g_cw_g, g_dtb_g, g_alog_g, g_nw_g) = _chunk_call(
        [ssd_bwd(z, xbc, pre_s, sm, hs, dys, ssd_cw, dtb_s, alog_s, dpar, nw_s, cs),
         gdn_bwd(gate, qkv, pre_g, sm, ss, ts, dyg, gdn_cw, dtb_g, alog_g, nw_g, cs)], "scan_bwd", nc, True)

    t_w_out = g_wout.reshape(N_DEV, MIX_WIDTH // N_DEV, D_MODEL)
    gws = {}
    for dg, (name, _, _) in zip((dz, dxbc, dgate, dsm), (GROUPS[0], GROUPS[1], GROUPS[2], GROUPS[4])):
        gws[name] = grad_w_group(u, dg, "grad_w_in_" + name)
    gws["qkv"], r_w_out = grad_w_group(u, dqkv, "grad_w_in_qkv", [t_w_out])
    t_w_in = unperm_w_in(gws["z"], gws["xbc"], gws["gate"], gws["qkv"], gws["sm"])
    dx, g_nw, r_w_in = inproj_bwd_dx(xl, dout, nw1, w_perm, (dz, dxbc, dgate, dqkv, dsm), [t_w_in])

    accs = dict(norm_w=g_nw, ssd_conv_b=g_cb_s, ssd_dt_bias=g_dtb_s, ssd_a_log=g_alog_s, ssd_d=g_d,
                ssd_norm_w=g_nw_s, gdn_dt_bias=g_dtb_g, gdn_a_log=g_alog_g, gdn_norm_w=g_nw_g, final_norm_w=g_fnw,
                ssd_conv_w=g_cw_s, gdn_conv_w=g_cw_g)
    r_small = scatter_small([accs[e[0]] for e in SMALL] + [loss_l])

    o_w_in = adamw_sum(r_w_in, w_in[0].T, m_w_in[0].T, v_w_in[0].T, None, "adamw_w_in", cols=256)
    o_w_out = adamw_sum(r_w_out, w_out[0], m_w_out[0], v_w_out[0], 64, "adamw_w_out")
    row = lambda d: {n: (a.reshape(1, -1) if a.ndim == 1 else a) for n, a in d.items()}
    o_small, loss_sum = adamw_small(r_small, row(w), row(m), row(v))

    loss = loss_sum[0, 0]
    outs = [loss, dx[None]]
    for k in range(4):
        parts = {n: o_small[n][k] for n in o_small}
        parts["w_in"] = o_w_in[k].T
        parts["w_out"] = o_w_out[k]
        outs += [parts[n].reshape(shapes[n]) for n in names]
    return tuple(outs)
```

```python
import functools

import jax
import jax.numpy as jnp
import numpy as np
from jax import lax
from jax.experimental import pallas as pl
from jax.experimental.pallas import tpu as pltpu

_MM = jnp.bfloat16

D_MODEL = 1024
CHUNK = 64
CONV_K = 4
EPS = 1e-6
SSD_CONV_DIM = 1536
GDN_HEADS = 8
GDN_DK = 128
GDN_CONV_DIM = 3072
MIX_WIDTH = 2048
IN_DIM = 6688
N_DEV = 8
W_IN_SHARD = IN_DIM // N_DEV
HI = lax.Precision.HIGHEST
HIGH = lax.Precision.HIGH
VMEM_LIMIT = 56 * 1024 * 1024

ADAM_LR = 0.001
ADAM_B1 = 0.9
ADAM_B2 = 0.999
ADAM_EPS = 1e-08
ADAM_WD = 0.01
ADAM_STEP = 10


def _pc(body, **kw):
    return pl.pallas_call(body, **kw)


def _pc_comm(body, **kw):
    return pl.pallas_call(body, **kw)


def _cparams(sem):
    return pltpu.CompilerParams(dimension_semantics=sem, vmem_limit_bytes=VMEM_LIMIT)


def _sig(x):
    return 0.5 * jnp.tanh(0.5 * x) + 0.5


@jax.custom_vjp
def _sigmoid(x):
    return _sig(x)


def _sigmoid_fwd(x):
    s = _sig(x)
    return s, s


def _sigmoid_bwd(s, g):
    return (g * s * (1.0 - s),)


_sigmoid.defvjp(_sigmoid_fwd, _sigmoid_bwd)


@jax.custom_vjp
def _silu(x):
    return x * _sig(x)


def _silu_fwd(x):
    s = _sig(x)
    return x * s, (x, s)


def _silu_bwd(res, g):
    x, s = res
    return (g * (s * (1.0 + x * (1.0 - s))),)


_silu.defvjp(_silu_fwd, _silu_bwd)


def _softplus_impl(x):
    return jnp.maximum(x, 0.0) + jnp.log(1.0 + jnp.exp(-jnp.abs(x)))


@jax.custom_vjp
def _softplus(x):
    return _softplus_impl(x)


def _softplus_fwd(x):
    return _softplus_impl(x), x


def _softplus_bwd(x, g):
    return (g * _sig(x),)


_softplus.defvjp(_softplus_fwd, _softplus_bwd)


def _lane_bcast_impl(x, k):
    return jnp.broadcast_to(x[..., k:k + 1], x.shape)


@functools.partial(jax.custom_vjp, nondiff_argnums=(1,))
def _lane_bcast(x, k):
    return _lane_bcast_impl(x, k)


def _lane_bcast_fwd(x, k):
    return _lane_bcast_impl(x, k), None


def _lane_bcast_bwd(k, _, g):
    lane = lax.broadcasted_iota(jnp.int32, g.shape, g.ndim - 1)
    return (jnp.where(lane == k, jnp.sum(g, axis=-1, keepdims=True), 0.0),)


_lane_bcast.defvjp(_lane_bcast_fwd, _lane_bcast_bwd)


def _mm(a, b):
    return jnp.dot(a.astype(_MM), b.astype(_MM), preferred_element_type=jnp.float32)


def _mm_nt(a, b):
    return lax.dot_general(a.astype(_MM), b.astype(_MM), (((1,), (1,)), ((), ())),
                           preferred_element_type=jnp.float32)


def _mm_tn(a, b):
    return lax.dot_general(a.astype(_MM), b.astype(_MM), (((0,), (0,)), ((), ())),
                           preferred_element_type=jnp.float32)


def _dot_hi(a, b):
    return jnp.dot(a, b, precision=HI, preferred_element_type=jnp.float32)


def _bmm(a, b):
    return lax.dot_general(a.astype(_MM), b.astype(_MM), (((2,), (1,)), ((0,), (0,))),
                           preferred_element_type=jnp.float32)


def _bmm_nt(a, b):
    return lax.dot_general(a.astype(_MM), b.astype(_MM), (((2,), (2,)), ((0,), (0,))),
                           preferred_element_type=jnp.float32)


def _bmm_tn(a, b):
    return lax.dot_general(a.astype(_MM), b.astype(_MM), (((1,), (1,)), ((0,), (0,))),
                           preferred_element_type=jnp.float32)


def _bmm_hi(a, b):
    return lax.dot_general(a, b, (((2,), (1,)), ((0,), (0,))), precision=HIGH, preferred_element_type=jnp.float32)


def _bmm_nt_hi(a, b):
    return lax.dot_general(a, b, (((2,), (2,)), ((0,), (0,))), precision=HIGH, preferred_element_type=jnp.float32)


def _bmm_tn_hi(a, b):
    return lax.dot_general(a, b, (((1,), (1,)), ((0,), (0,))), precision=HIGH, preferred_element_type=jnp.float32)


def _consts():
    l = np.arange(CHUNK)
    tri = (l[:, None] >= l[None, :]).astype(np.float32)
    lane = np.arange(128)
    i2 =(l[:, None] == (lane[None, :] % 64)).astype(np.float32)
    mask2 = (l[:, None] >= (lane[None, :] % 64)).astype(np.float32)
    lo = (lane < 64).astype(np.float32)[None, :]
    i64 = np.eye(CHUNK, dtype=np.float32)
    strict = (l[:, None] > l[None, :]).astype(np.float32)
    return dict(tri=jnp.asarray(tri), i2=jnp.asarray(i2), mask2=jnp.asarray(mask2), lo=jnp.asarray(lo),
                i64=jnp.asarray(i64), strict=jnp.asarray(strict))


def _ssd_chunk(xs_pre, b_pre, c_pre, z, sm, ht, dtb, alog, dpar, nw, tri, i2, mask2, lo):
    lane = lax.broadcasted_iota(jnp.int32, (1, 128), 1)
    m16 = lane < 16
    dt = jnp.where(m16, _softplus(sm + dtb), 0.0)
    a_neg = -jnp.exp(alog)
    cum = _dot_hi(tri, dt * a_neg)
    row = lax.broadcasted_iota(jnp.int32, (CHUNK, 1), 0)
    is_last = row == CHUNK - 1
    hi = 1.0 - lo
    bm = [_silu(b) for b in b_pre]
    cm = [_silu(c) for c in c_pre]
    cb2 = [_mm_nt(cm[g], jnp.concatenate([bm[g], bm[g]], axis=0)) for g in range(2)]
    ht_g = [jnp.concatenate(ht[4 * g:4 * g + 4], axis=1) for g in range(2)]
    yoff_g = [_mm(cm[g], ht_g[g]) for g in range(2)]
    yg, xdec, clast = [], [], []
    for j in range(8):
        g, k4 = j // 4, j % 4
        pair = lambda v, j=j: jnp.where(lo > 0.5, _lane_bcast(v, 2 * j), _lane_bcast(v, 2 * j + 1))
        xs = _silu(xs_pre[j])
        dte = pair(dt)
        cume = pair(cum)
        cum_last = jnp.sum(jnp.where(is_last, cume, 0.0), axis=0, keepdims=True)
        xdt = xs * dte
        rowv = jnp.sum(cume * i2, axis=0, keepdims=True)
        lm = jnp.exp(jnp.where(mask2 > 0.5, cume - rowv, -jnp.inf))
        m = cb2[g] * lm
        xblk = jnp.concatenate([xdt * lo, xdt * hi], axis=0)
        y = _mm(m, xblk)
        y = y + yoff_g[g][:, 128 * k4:128 * k4 + 128] * jnp.exp(cume)
        y = y + pair(dpar) * xs
        yg.append(y * _silu(z[j]))
        xdec.append(xdt * jnp.exp(cum_last - cume))
        clast.append(cum_last)
    ht_next = []
    for g in range(2):
        st = _mm_tn(bm[g], jnp.concatenate(xdec[4 * g:4 * g + 4], axis=1))
        for k4 in range(4):
            j = 4 * g + k4
            ht_next.append(ht[j] * jnp.exp(clast[j]) + st[:, 128 * k4:128 * k4 + 128])
    outs = []
    for g in range(2):
        ss = sum(jnp.sum(yg[j] * yg[j], axis=-1, keepdims=True) for j in range(4 * g, 4 * g + 4))
        rs = lax.rsqrt(ss * (1.0 / 512.0) + EPS)
        for j in range(4 * g, 4 * g + 4):
            outs.append(yg[j] * rs * nw[j])
    return outs, ht_next


def _tri_inverse(a):
    eye = jnp.eye(CHUNK, dtype=jnp.float32)[None]
    p = eye - a
    x = _bmm_hi(a, a)
    for _ in range(4):
        both = _bmm_hi(jnp.concatenate([p, x], axis=1), x)
        p = p + both[:, :CHUNK]
        x = both[:, CHUNK:]
    return p + _bmm_hi(p, x)


def _solve_apply(t, r1, r2):
    both = _bmm_hi(t, jnp.concatenate([r1, r2], axis=-1))
    n = r1.shape[-1]
    return both[..., :n], both[..., n:]


@jax.custom_vjp
def _solve(a, r1, r2, t):
    return _solve_apply(t, r1, r2)


def _solve_fwd(a, r1, r2, t):
    u, w = _bmm_hi(t, r1), _bmm_hi(t, r2)
    return (u, w), (t, u, w)


def _solve_bwd(res, cts):
    t, u, w = res
    du, dw = cts
    dr1 = _bmm_tn_hi(t, du)
    dr2 = _bmm_tn_hi(t, dw)
    da = -(_bmm_nt_hi(dr1, u) + _bmm_nt_hi(dr2, w))
    return da, dr1, dr2, jnp.zeros_like(t)


_solve.defvjp(_solve_fwd, _solve_bwd)


def _gdn_chunk(q_pre, k_pre, v_pre, gate, sm, s, dtb, alog, nw, tri, i64, strict, t_in=None):
    lane = lax.broadcasted_iota(jnp.int32, (1, 128), 1)
    m_a = (lane >= 16) & (lane < 24)
    g_full = jnp.where(m_a, -jnp.exp(alog) * _softplus(sm + dtb), 0.0)
    gc = _dot_hi(tri, g_full)
    sig = _sigmoid(sm)
    heads = lambda f: jnp.concatenate([f(h)[None] for h in range(GDN_HEADS)], axis=0)
    gc3 = heads(lambda h: _lane_bcast(gc, 16 + h))
    beta3 = heads(lambda h: _lane_bcast(sig, 24 + h))
    q = _silu(q_pre)
    q = q * lax.rsqrt(jnp.sum(q * q, axis=-1, keepdims=True) + EPS) * (GDN_DK ** -0.5)
    k = _silu(k_pre)
    k = k * lax.rsqrt(jnp.sum(k * k, axis=-1, keepdims=True) + EPS)
    v = _silu(v_pre)
    gcl = gc3[:, :, :CHUNK]
    gc_row = jnp.sum(gcl * i64[None], axis=1, keepdims=True)
    incl = (strict + i64)[None] > 0.5
    decay = jnp.exp(jnp.where(incl, gcl - gc_row, -jnp.inf))
    kb = k * beta3
    a = jnp.where(strict[None] > 0.5, _bmm_nt(kb, k) * decay, 0.0)
    egc = jnp.exp(gc3)
    t = _tri_inverse(a) if t_in is None else t_in
    u, w = _solve(a, v * beta3, kb * egc, t)
    attn = _bmm_nt(q, k) * decay
    row = lax.broadcasted_iota(jnp.int32, (1, CHUNK, 1), 1)
    gl = jnp.sum(jnp.where(row == CHUNK - 1, gc3, 0.0), axis=1, keepdims=True)
    q_dec = q * egc
    k_dec = k * jnp.exp(gl - gc3)
    ws = _bmm(jnp.concatenate([w, q_dec], axis=1), s)
    v_new = u - ws[:, :CHUNK]
    o = ws[:, CHUNK:] + _bmm(attn, v_new)
    s_next = s * jnp.exp(gl) + _bmm_tn(k_dec, v_new)
    on = o * lax.rsqrt(jnp.mean(o * o, axis=-1, keepdims=True) + EPS) * nw
    return on * _silu(gate), s_next, t


def _conv_fwd(pbuf, w_ref, c0, c1):
    blk = pbuf[:, c0:c1]
    acc = w_ref[CONV_K - 1:CONV_K, c0:c1] * blk[8:72]
    for j in range(CONV_K - 1):
        acc = acc + w_ref[j:j + 1, c0:c1] * pltpu.roll(blk, CONV_K - 1 - j, axis=0)[8:72]
    return acc


MESH = pl.DeviceIdType.MESH
ANY = pl.BlockSpec(memory_space=pl.ANY)


def _me():
    x, y, c = lax.axis_index("x"), lax.axis_index("y"), lax.axis_index("c")
    return x, y, c, 4 * x + 2 * y + c


def _peer(r):
    x, y, c, _ = _me()
    px = 1 - x if r & 4 else x
    py = 1 - y if r & 2 else y
    pc = 1 - c if r & 1 else c
    return (px, py, pc), 4 * px + 2 * py + pc


def _exchange_ops(kind, in_ref, out_ref, send_sems, recv_sems, local_sem):
    me = _me()[3]
    local = pltpu.make_async_copy(in_ref.at[me] if kind == "scatter" else in_ref, out_ref.at[me], local_sem)
    sends, recvs = [], []
    for r in range(1, N_DEV):
        peer, pidx = _peer(r)
        src = in_ref.at[pidx] if kind == "scatter" else in_ref
        sems = dict(send_sem=send_sems.at[r - 1], recv_sem=recv_sems.at[r - 1], device_id=peer, device_id_type=MESH)
        sends.append(pltpu.make_async_remote_copy(src_ref=src, dst_ref=out_ref.at[me], **sems))
        recvs.append(pltpu.make_async_remote_copy(src_ref=src, dst_ref=out_ref.at[pidx], **sems))

    def start():
        local.start()
        for cp in sends:
            cp.start()

    def wait():
        for cp in recvs:
            cp.wait_recv()
        for cp in sends:
            cp.wait_send()
        local.wait()

    return start, wait


def _exchange_sems(n):
    return [pltpu.SemaphoreType.DMA((N_DEV - 1,)), pltpu.SemaphoreType.DMA((N_DEV - 1,)),
            pltpu.SemaphoreType.DMA(())] * n


def _exchange_out_shape(kind, a):
    return jax.ShapeDtypeStruct(a.shape if kind == "scatter" else (N_DEV,) + a.shape, a.dtype)


def _hosting(body, n_in, n_out, n_scratch, kinds, first, last):
    ne = len(kinds)

    def wrapped(*refs):
        ins, ex_in = refs[:n_in], refs[n_in:n_in + ne]
        o0 = n_in + ne
        outs, ex_out = refs[o0:o0 + n_out], refs[o0 + n_out:o0 + n_out + ne]
        s0 = o0 + n_out + ne
        scr, sems = refs[s0:s0 + n_scratch], refs[s0 + n_scratch:]
        ops = [_exchange_ops(kinds[e], ex_in[e], ex_out[e], *sems[3 * e:3 * e + 3]) for e in range(ne)]

        @pl.when(first())
        def _():
            for start, _ in ops:
                start()

        body(*ins, *outs, *scr)

        @pl.when(last())
        def _():
            for _, wait in ops:
                wait()

    return wrapped


GROUPS = (("z", 0, 1024), ("xbc", 1024, 2560), ("gate", 2560, 3584), ("qkv", 3584, 6656), ("sm", 6656, 6784))
GROUP_ROWS = dict(z=((0, 1024),), xbc=((1024, 2560),), gate=((2576, 3600),), qkv=((3600, 6672),),
                  sm=((2560, 2576), (6672, 6688)))


def _w_rows(w_ref, name, width):
    pieces = [w_ref[a:b, :] for a, b in GROUP_ROWS[name]]
    n = sum(b - a for a, b in GROUP_ROWS[name])
    if n < width:
        pieces.append(jnp.zeros((width - n, D_MODEL), w_ref.dtype))
    return pieces[0] if len(pieces) == 1 else jnp.concatenate(pieces, axis=0)


def inproj_fwd(x, norm_w, w_perm, gathered):
    t = x.shape[0]
    tm = min(512, t)
    steps = t // tm
    kinds = ["gather"] * len(gathered)

    def body(x_ref, nw_ref, w_ref, u_ref, z_ref, xbc_ref, gate_ref, qkv_ref, sm_ref):
        xf = x_ref[...]
        rstd = lax.rsqrt(jnp.mean(xf * xf, axis=-1, keepdims=True) + EPS)
        u = (xf * rstd * nw_ref[...]).astype(_MM)
        u_ref[...] = u
        for (name, c0, c1), o_ref in zip(GROUPS, (z_ref, xbc_ref, gate_ref, qkv_ref, sm_ref)):
            o_ref[...] = lax.dot_general(u, _w_rows(w_ref, name, c1 - c0), (((1,), (1,)), ((), ())),
                                         preferred_element_type=jnp.float32)

    outs = [jax.ShapeDtypeStruct((t, D_MODEL), _MM)] + [jax.ShapeDtypeStruct((t, c1 - c0), jnp.float32)
                                                        for _, c0, c1 in GROUPS]
    hosted = _hosting(body, 3, 6, 0, kinds, lambda: pl.program_id(0) == 0, lambda: pl.program_id(0) == steps - 1)
    return _pc_comm(
        hosted, name="inproj_fwd", grid=(steps,),
        in_specs=[pl.BlockSpec((tm, D_MODEL), lambda i: (i, 0)),
                  pl.BlockSpec((1, D_MODEL), lambda i: (0, 0)),
                  pl.BlockSpec((IN_DIM, D_MODEL), lambda i: (0, 0), pipeline_mode=pl.Buffered(1))] +
                 [ANY] * len(gathered),
        out_specs=[pl.BlockSpec((tm, D_MODEL), lambda i: (i, 0))] +
                  [pl.BlockSpec((tm, c1 - c0), lambda i: (i, 0)) for _, c0, c1 in GROUPS] + [ANY] * len(gathered),
        out_shape=outs + [_exchange_out_shape("gather", a) for a in gathered],
        scratch_shapes=_exchange_sems(len(gathered)), compiler_params=_cparams(("arbitrary",)),
    )(x, norm_w, w_perm, *gathered)


SUB_FWD = 4
SUB_BWD = 2


def _halo_spec(width, idx_fn):
    return pl.BlockSpec((8, width), lambda i: (jnp.maximum(idx_fn(i) * (SUB_FWD * CHUNK // 8) - 1, 0), 0))


def _when_first(shared, fn):
    if shared["first"] is not False:
        pl.when(shared["first"])(fn)


def _full(shape):
    nd = len(shape)
    return pl.BlockSpec(shape, lambda i: (0,) * nd)


def _ssd_split(pre_fn, z_ref, sm_ref):
    xs_pre = [pre_fn(128 * j, 128 * j + 128) for j in range(8)]
    b_pre = [pre_fn(1024 + 128 * g, 1152 + 128 * g) for g in range(2)]
    c_pre = [pre_fn(1280 + 128 * g, 1408 + 128 * g) for g in range(2)]
    z = [z_ref[:, 128 * j:128 * j + 128] for j in range(8)]
    return xs_pre, b_pre, c_pre, z, sm_ref[...]


def ssd_fwd(z, xbc, sm, conv_w, conv_b, dtb, alog, dpar, nw, cs):
    t = z.shape[0]
    nc = t // CHUNK

    def body(shared, z_ref, xbc_ref, halo_ref, sm_ref, cw_ref, cb_ref, dtb_ref, alog_ref, dpar_ref, nw_ref,
             tri_ref, i2_ref, mask2_ref, lo_ref, y_ref, hs_ref, pre_ref, pbuf, ht_scr):
        def init():
            ht_scr[...] = jnp.zeros_like(ht_scr)

        _when_first(shared, init)
        pbuf[0:8, :] = jnp.where(shared["first"], 0.0, halo_ref[...])
        pbuf[8:72, :] = xbc_ref[...]

        def pre_fn(c0, c1):
            pre = _conv_fwd(pbuf, cw_ref, c0, c1) + cb_ref[:, c0:c1]
            pre_ref[:, c0:c1] = pre
            return pre

        xs_pre, b_pre, c_pre, zz, smv = _ssd_split(pre_fn, z_ref, sm_ref)
        ht = [ht_scr[:, 128 * j:128 * j + 128] for j in range(8)]
        hs_ref[0] = ht_scr[...]
        nwl = [nw_ref[:, 128 * j:128 * j + 128] for j in range(8)]
        outs, ht_next = _ssd_chunk(xs_pre, b_pre, c_pre, zz, smv, ht, dtb_ref[...], alog_ref[...], dpar_ref[...],
                                   nwl, tri_ref[...], i2_ref[...], mask2_ref[...], lo_ref[...])
        for j in range(8):
            y_ref[:, 128 * j:128 * j + 128] = outs[j].astype(y_ref.dtype)
            ht_scr[:, 128 * j:128 * j + 128] = ht_next[j]

    blk = lambda w: pl.BlockSpec((SUB_FWD * CHUNK, w), lambda i: (i, 0))
    return dict(
        body=body,
        in_kinds=["rows", "rows", ("halo", 1), "rows"] + ["full"] * 10, out_kinds=["rows", "state", "rows"],
        in_specs=[blk(1024), blk(1536), _halo_spec(1536, lambda i: i), blk(128),
                  _full((CONV_K, 1536)), _full((1, 1536)), _full((1, 128)), _full((1, 128)), _full((1, 128)),
                  _full((1, 1024)), _full((64, 64)), _full((64, 128)), _full((64, 128)),
                  _full((1, 128))],
        out_specs=[blk(1024), pl.BlockSpec((SUB_FWD, 128, 1024), lambda i: (i, 0, 0)), blk(1536)],
        out_shape=[jax.ShapeDtypeStruct((t, 1024), _MM), jax.ShapeDtypeStruct((nc, 128, 1024), jnp.float32),
                   jax.ShapeDtypeStruct((t, 1536), jnp.float32)],
        scratch=[pltpu.VMEM((72, 1536), jnp.float32), pltpu.VMEM((128, 1024), jnp.float32)],
        args=[z, xbc, xbc, sm, conv_w, conv_b, dtb, alog, dpar, nw, cs["tri"], cs["i2"], cs["mask2"], cs["lo"]])


def _conv_bwd(dpre_list, col_ranges, dbuf, carry, x_ref, cw_ref, dx_ref, dcw_ref, dcb_ref, first):
    for dpre, (c0, c1) in zip(dpre_list, col_ranges):
        dbuf[0:64, c0:c1] = dpre
    dbuf[64:72, :] = jnp.where(first, 0.0, carry[...])
    carry[...] = dbuf[0:8, :]
    for (c0, c1) in col_ranges:
        xin = x_ref[:, c0:c1]
        blk = dbuf[:, c0:c1]
        acc = None
        for j in range(CONV_K):
            sh = blk[0:64] if j == CONV_K - 1 else pltpu.roll(blk, 72 - (CONV_K - 1 - j), axis=0)[0:64]
            term = cw_ref[j:j + 1, c0:c1] * sh
            acc = term if acc is None else acc + term
            dcw_ref[j:j + 1, c0:c1] += jnp.sum(xin * sh, axis=0, keepdims=True)
        dx_ref[:, c0:c1] = acc.astype(dx_ref.dtype)
        if dcb_ref is not None:
            dcb_ref[0:1, c0:c1] += jnp.sum(dbuf[0:64, c0:c1], axis=0, keepdims=True)


def ssd_bwd(z, xbc, pre, sm, hs, dy, conv_w, dtb, alog, dpar, nw, cs):
    t = z.shape[0]
    nc = t // CHUNK

    def body(shared, z_ref, xbc_ref, pre_ref, sm_ref, hs_ref, dy_ref, cw_ref, dtb_ref, alog_ref, dpar_ref, nw_ref,
             tri_ref, i2_ref, mask2_ref, lo_ref,
             dz_ref, dxbc_ref, dcw_ref, dcb_ref, ddtb_ref, dalog_ref, ddpar_ref, dnw_ref,
             dbuf, carry, dht_scr):
        def init():
            dht_scr[...] = jnp.zeros_like(dht_scr)
            dcw_ref[...] = jnp.zeros_like(dcw_ref)
            dcb_ref[...] = jnp.zeros_like(dcb_ref)
            ddtb_ref[...] = jnp.zeros_like(ddtb_ref)
            dalog_ref[...] = jnp.zeros_like(dalog_ref)
            ddpar_ref[...] = jnp.zeros_like(ddpar_ref)
            dnw_ref[...] = jnp.zeros_like(dnw_ref)

        _when_first(shared, init)
        pre_fn = lambda c0, c1: pre_ref[:, c0:c1]
        xs_pre, b_pre, c_pre, zz, smv = _ssd_split(pre_fn, z_ref, sm_ref)
        ht = [hs_ref[0, :, 128 * j:128 * j + 128] for j in range(8)]
        nwl = [nw_ref[:, 128 * j:128 * j + 128] for j in range(8)]
        consts = (tri_ref[...], i2_ref[...], mask2_ref[...], lo_ref[...])

        def f(xs_pre, b_pre, c_pre, zz, smv, ht, dtb, alog, dpar, nwl):
            return _ssd_chunk(xs_pre, b_pre, c_pre, zz, smv, ht, dtb, alog, dpar, nwl, *consts)

        _, vjp = jax.vjp(f, xs_pre, b_pre, c_pre, zz, smv, ht, dtb_ref[...], alog_ref[...], dpar_ref[...], nwl)
        dys = [dy_ref[:, 128 * j:128 * j + 128] for j in range(8)]
        dhts = [dht_scr[:, 128 * j:128 * j + 128] for j in range(8)]
        dxs, db, dc, dzz, dsm, dht, ddtb, dalog, ddpar, dnwl = vjp((dys, dhts))
        for j in range(8):
            dz_ref[:, 128 * j:128 * j + 128] = dzz[j].astype(dz_ref.dtype)
            dht_scr[:, 128 * j:128 * j + 128] = dht[j]
            dnw_ref[0:1, 128 * j:128 * j + 128] += dnwl[j]
        shared["dsm_ssd"] = dsm
        ddtb_ref[0:1, :] += ddtb
        dalog_ref[0:1, :] += dalog
        ddpar_ref[0:1, :] += ddpar
        ranges = ([(128 * j, 128 * j + 128) for j in range(8)] + [(1024 + 128 * g, 1152 + 128 * g) for g in range(2)]
                  + [(1280 + 128 * g, 1408 + 128 * g) for g in range(2)])
        _conv_bwd(dxs + db + dc, ranges, dbuf, carry, xbc_ref, cw_ref, dxbc_ref, dcw_ref, dcb_ref, shared["first"])

    ns = nc // SUB_BWD
    rblk = lambda w: pl.BlockSpec((SUB_BWD * CHUNK, w), lambda i: (ns - 1 - i, 0))
    acc = lambda w: pl.BlockSpec((8, w), lambda i: (0, 0))
    f32 = jnp.float32
    return dict(
        body=body,
        in_kinds=["rows"] * 4 + ["state", "rows"] + ["full"] * 9, out_kinds=["rows", "rows"] + ["full"] * 6,
        in_specs=[rblk(1024), rblk(1536), rblk(1536), rblk(128),
                  pl.BlockSpec((SUB_BWD, 128, 1024), lambda i: (ns - 1 - i, 0, 0)), rblk(1024),
                  _full((CONV_K, 1536)), _full((1, 128)), _full((1, 128)), _full((1, 128)),
                  _full((1, 1024)), _full((64, 64)), _full((64, 128)), _full((64, 128)),
                  _full((1, 128))],
        out_specs=[rblk(1024), rblk(1536), acc(1536), acc(1536), acc(128), acc(128), acc(128), acc(1024)],
        out_shape=[jax.ShapeDtypeStruct((t, 1024), f32), jax.ShapeDtypeStruct((t, 1536), f32),
                   jax.ShapeDtypeStruct((8, 1536), f32),
                   jax.ShapeDtypeStruct((8, 1536), f32), jax.ShapeDtypeStruct((8, 128), f32),
                   jax.ShapeDtypeStruct((8, 128), f32), jax.ShapeDtypeStruct((8, 128), f32),
                   jax.ShapeDtypeStruct((8, 1024), f32)],
        scratch=[pltpu.VMEM((72, 1536), f32), pltpu.VMEM((8, 1536), f32), pltpu.VMEM((128, 1024), f32)],
        args=[z, xbc, pre, sm, hs, dy, conv_w, dtb, alog, dpar, nw, cs["tri"], cs["i2"], cs["mask2"], cs["lo"]])


def _gdn_split(pre_fn, gate_ref):
    def heads(base):
        return jnp.stack([pre_fn(base + 128 * h, base + 128 * h + 128) for h in range(GDN_HEADS)])
    gate = jnp.stack([gate_ref[:, 128 * h:128 * h + 128] for h in range(GDN_HEADS)])
    return heads(0), heads(1024), heads(2048), gate


def gdn_fwd(gate, qkv, sm, conv_w, dtb, alog, nw, cs):
    t = gate.shape[0]
    nc = t // CHUNK

    def body(shared, gate_ref, qkv_ref, halo_ref, sm_ref, cw_ref, dtb_ref, alog_ref, nw_ref,
             tri_ref, i64_ref, strict_ref, o_ref, ss_ref, ts_ref, pre_ref, pbuf, s_scr):
        def init():
            s_scr[...] = jnp.zeros_like(s_scr)

        _when_first(shared, init)
        pbuf[0:8, :] = jnp.where(shared["first"], 0.0, halo_ref[...])
        pbuf[8:72, :] = qkv_ref[...]

        def pre_fn(c0, c1):
            pre = _conv_fwd(pbuf, cw_ref, c0, c1)
            pre_ref[:, c0:c1] = pre
            return pre

        q_pre, k_pre, v_pre, g3 = _gdn_split(pre_fn, gate_ref)
        s = s_scr[...]
        ss_ref[0] = s
        out, s_next, tinv = _gdn_chunk(q_pre, k_pre, v_pre, g3, sm_ref[...], s, dtb_ref[...], alog_ref[...],
                                       nw_ref[...], tri_ref[...], i64_ref[...], strict_ref[...])
        ts_ref[0] = tinv
        s_scr[...] = s_next
        for h in range(GDN_HEADS):
            o_ref[:, 128 * h:128 * h + 128] = out[h].astype(o_ref.dtype)

    blk = lambda w: pl.BlockSpec((SUB_FWD * CHUNK, w), lambda i: (i, 0))
    return dict(
        body=body,
        in_kinds=["rows", "rows", ("halo", 1), "rows"] + ["full"] * 7, out_kinds=["rows", "state", "state", "rows"],
        in_specs=[blk(1024), blk(3072), _halo_spec(3072, lambda i: i), blk(128),
                  _full((CONV_K, 3072)), _full((1, 128)), _full((1, 128)), _full((1, 128)),
                  _full((64, 64)), _full((64, 64)), _full((64, 64))],
        out_specs=[blk(1024), pl.BlockSpec((SUB_FWD, 8, 128, 128), lambda i: (i, 0, 0, 0)),
                   pl.BlockSpec((SUB_FWD, 8, CHUNK, CHUNK), lambda i: (i, 0, 0, 0)), blk(3072)],
        out_shape=[jax.ShapeDtypeStruct((t, 1024), _MM), jax.ShapeDtypeStruct((nc, 8, 128, 128), jnp.float32),
                   jax.ShapeDtypeStruct((nc, 8, CHUNK, CHUNK), jnp.float32),
                   jax.ShapeDtypeStruct((t, 3072), jnp.float32)],
        scratch=[pltpu.VMEM((72, 3072), jnp.float32), pltpu.VMEM((8, 128, 128), jnp.float32)],
        args=[gate, qkv, qkv, sm, conv_w, dtb, alog, nw, cs["tri"], cs["i64"], cs["strict"]])


def gdn_bwd(gate, qkv, pre, sm, ss, ts, do, conv_w, dtb, alog, nw, cs):
    t = gate.shape[0]
    nc = t // CHUNK

    def body(shared, gate_ref, qkv_ref, pre_ref, sm_ref, ss_ref, ts_ref, do_ref, cw_ref, dtb_ref, alog_ref,
             nw_ref, tri_ref, i64_ref, strict_ref,
             dgate_ref, dqkv_ref, dsm_ref, dcw_ref, ddtb_ref, dalog_ref, dnw_ref,
             dbuf, carry, ds_scr):
        def init():
            ds_scr[...] = jnp.zeros_like(ds_scr)
            dcw_ref[...] = jnp.zeros_like(dcw_ref)
            ddtb_ref[...] = jnp.zeros_like(ddtb_ref)
            dalog_ref[...] = jnp.zeros_like(dalog_ref)
            dnw_ref[...] = jnp.zeros_like(dnw_ref)

        _when_first(shared, init)

        q_pre, k_pre, v_pre, g3 = _gdn_split(lambda c0, c1: pre_ref[:, c0:c1], gate_ref)
        consts = (tri_ref[...], i64_ref[...], strict_ref[...], ts_ref[0])

        def f(q_pre, k_pre, v_pre, g3, smv, s, dtb, alog, nwv):
            return _gdn_chunk(q_pre, k_pre, v_pre, g3, smv, s, dtb, alog, nwv, *consts)[:2]

        _, vjp = jax.vjp(f, q_pre, k_pre, v_pre, g3, sm_ref[...], ss_ref[0], dtb_ref[...], alog_ref[...], nw_ref[...])
        do3 = jnp.stack([do_ref[:, 128 * h:128 * h + 128] for h in range(GDN_HEADS)])
        dq, dk, dv, dg3, dsm, ds, ddtb, dalog, dnw = vjp((do3, ds_scr[...]))
        ds_scr[...] = ds
        for h in range(GDN_HEADS):
            dgate_ref[:, 128 * h:128 * h + 128] = dg3[h].astype(dgate_ref.dtype)
        dsm_ref[...] = (dsm + shared["dsm_ssd"]).astype(dsm_ref.dtype)
        ddtb_ref[0:1, :] += ddtb
        dalog_ref[0:1, :] += dalog
        dnw_ref[0:1, :] += dnw
        ranges = [(base + 128 * h, base + 128 * h + 128) for base in (0, 1024, 2048) for h in range(GDN_HEADS)]
        dlist = [d[h] for d in (dq, dk, dv) for h in range(GDN_HEADS)]
        _conv_bwd(dlist, ranges, dbuf, carry, qkv_ref, cw_ref, dqkv_ref, dcw_ref, None, shared["first"])

    ns = nc // SUB_BWD
    rblk = lambda w: pl.BlockSpec((SUB_BWD * CHUNK, w), lambda i: (ns - 1 - i, 0))
    acc = lambda w: pl.BlockSpec((8, w), lambda i: (0, 0))
    f32 = jnp.float32
    return dict(
        body=body,
        in_kinds=["rows"] * 4 + ["state", "state", "rows"] + ["full"] * 7, out_kinds=["rows"] * 3 + ["full"] * 4,
        in_specs=[rblk(1024), rblk(3072), rblk(3072), rblk(128),
                  pl.BlockSpec((SUB_BWD, 8, 128, 128), lambda i: (ns - 1 - i, 0, 0, 0)),
                  pl.BlockSpec((SUB_BWD, 8, CHUNK, CHUNK), lambda i: (ns - 1 - i, 0, 0, 0)), rblk(1024),
                  _full((CONV_K, 3072)), _full((1, 128)), _full((1, 128)), _full((1, 128)),
                  _full((64, 64)), _full((64, 64)), _full((64, 64))],
        out_specs=[rblk(1024), rblk(3072), rblk(128), acc(3072), acc(128), acc(128), acc(128)],
        out_shape=[jax.ShapeDtypeStruct((t, 1024), f32), jax.ShapeDtypeStruct((t, 3072), f32),
                   jax.ShapeDtypeStruct((t, 128), f32), jax.ShapeDtypeStruct((8, 3072), f32),
                   jax.ShapeDtypeStruct((8, 128), f32), jax.ShapeDtypeStruct((8, 128), f32),
                   jax.ShapeDtypeStruct((8, 128), f32)],
        scratch=[pltpu.VMEM((72, 3072), f32), pltpu.VMEM((8, 3072), f32), pltpu.VMEM((8, 128, 128), f32)],
        args=[gate, qkv, pre, sm, ss, ts, do, conv_w, dtb, alog, nw, cs["tri"], cs["i64"], cs["strict"]])


def _chunk_call(parts, name, nc, reverse):
    n_in = [len(p["args"]) for p in parts]
    n_out = [len(p["out_shape"]) for p in parts]
    n_scr = [len(p["scratch"]) for p in parts]
    sub = SUB_BWD if reverse else SUB_FWD
    order = list(range(sub))[::-1] if reverse else list(range(sub))

    def view(ref, kind, s, refs):
        if kind == "rows":
            return ref.at[pl.ds(CHUNK * s, CHUNK)]
        if kind == "state":
            return ref.at[pl.ds(s, 1)]
        if kind == "full":
            return ref
        src = refs[kind[1]]
        return ref if s == 0 else src.at[pl.ds(CHUNK * s - 8, 8)]

    def body(*refs):
        ins, outs, scr = refs[:sum(n_in)], refs[sum(n_in):sum(n_in) + sum(n_out)], refs[sum(n_in) + sum(n_out):]
        for s in order:
            shared = {"first": (pl.program_id(0) == 0) if s == order[0] else False}
            for k, p in enumerate(parts):
                i0, o0, s0 = sum(n_in[:k]), sum(n_out[:k]), sum(n_scr[:k])
                p_ins = ins[i0:i0 + n_in[k]]
                p["body"](shared,
                          *[view(r, kd, s, p_ins) for r, kd in zip(p_ins, p["in_kinds"])],
                          *[view(r, kd, s, None) for r, kd in zip(outs[o0:o0 + n_out[k]], p["out_kinds"])],
                          *scr[s0:s0 + n_scr[k]])

    cat = lambda key: [v for p in parts for v in p[key]]
    return _pc(body, name=name, grid=(nc // sub,), in_specs=cat("in_specs"), out_specs=cat("out_specs"),
               out_shape=cat("out_shape"), scratch_shapes=cat("scratch"),
               compiler_params=_cparams(("arbitrary",)))(*cat("args"))


def out_fwd_bwd(x, tgt, y_ssd, y_gdn, w_out, fnw):
    t = x.shape[0]
    tm = min(512, t)
    f32 = jnp.float32

    def body(x_ref, tgt_ref, ys_ref, yg_ref, w_ref, fnw_ref,
             dout_ref, dys_ref, dyg_ref, gw_ref, gfnw_ref, loss_ref, gw_acc):
        i = pl.program_id(0)

        @pl.when(i == 0)
        def _():
            gw_acc[...] = jnp.zeros_like(gw_acc)
            gfnw_ref[...] = jnp.zeros_like(gfnw_ref)
            loss_ref[...] = jnp.zeros_like(loss_ref)

        ys = ys_ref[...]
        yg = yg_ref[...]
        out = x_ref[...] + jnp.dot(ys, w_ref[0:1024, :], preferred_element_type=f32) \
            + jnp.dot(yg, w_ref[1024:2048, :], preferred_element_type=f32)
        rstd = lax.rsqrt(jnp.mean(out * out, axis=-1, keepdims=True) + EPS)
        yhat = out * rstd
        fw = fnw_ref[...]
        e = yhat * fw - tgt_ref[...]
        loss_ref[...] += 0.5 * jnp.sum(jnp.sum(e * e, axis=-1, keepdims=True) * (1.0 / D_MODEL), axis=0, keepdims=True)
        dyf = e * (1.0 / D_MODEL)
        gfnw_ref[0:1, :] += jnp.sum(dyf * yhat, axis=0, keepdims=True)
        dyhat = dyf * fw
        dout = rstd * (dyhat - yhat * jnp.mean(dyhat * yhat, axis=-1, keepdims=True))
        dout_ref[...] = dout
        db = dout.astype(_MM)
        dys_ref[...] = lax.dot_general(db, w_ref[0:1024, :], (((1,), (1,)), ((), ())), preferred_element_type=f32)
        dyg_ref[...] = lax.dot_general(db, w_ref[1024:2048, :], (((1,), (1,)), ((), ())), preferred_element_type=f32)
        gw_acc[0:1024, :] += lax.dot_general(ys, db, (((0,), (0,)), ((), ())), preferred_element_type=f32)
        gw_acc[1024:2048, :] += lax.dot_general(yg, db, (((0,), (0,)), ((), ())), preferred_element_type=f32)

        @pl.when(i == steps - 1)
        def _():
            gw_ref[...] = gw_acc[...].astype(gw_ref.dtype)

    steps = t // tm
    blk = pl.BlockSpec((tm, D_MODEL), lambda i: (i, 0))
    return _pc(
        body, name="out_fwd_bwd", grid=(steps,),
        in_specs=[blk, blk, blk, blk, _full((MIX_WIDTH, D_MODEL)), _full((1, D_MODEL))],
        out_specs=[blk, blk, blk, _full((MIX_WIDTH, D_MODEL)), _full((8, D_MODEL)), _full((1, 128))],
        out_shape=[jax.ShapeDtypeStruct((t, D_MODEL), f32)] * 3 +
                  [jax.ShapeDtypeStruct((MIX_WIDTH, D_MODEL), _MM), jax.ShapeDtypeStruct((8, D_MODEL), f32),
                   jax.ShapeDtypeStruct((1, 128), f32)],
        scratch_shapes=[pltpu.VMEM((MIX_WIDTH, D_MODEL), f32)],
        compiler_params=_cparams(("arbitrary",)),
    )(x, tgt, y_ssd, y_gdn, w_out, fnw)


def inproj_bwd_dx(x, dout, norm_w, w_perm, dgroups, scattered):
    t = x.shape[0]
    tm = min(256, t)
    f32 = jnp.float32

    def body(x_ref, dout_ref, nw_ref, w_ref, dz_ref, dxbc_ref, dgate_ref, dqkv_ref, dsm_ref, dx_ref, gnw_ref):
        i = pl.program_id(0)

        @pl.when(i == 0)
        def _():
            gnw_ref[...] = jnp.zeros_like(gnw_ref)

        du = None
        for (name, c0, c1), d_ref in zip(GROUPS, (dz_ref, dxbc_ref, dgate_ref, dqkv_ref, dsm_ref)):
            term = jnp.dot(d_ref[...].astype(_MM), _w_rows(w_ref, name, c1 - c0), preferred_element_type=f32)
            du = term if du is None else du + term
        xf = x_ref[...]
        rstd = lax.rsqrt(jnp.mean(xf * xf, axis=-1, keepdims=True) + EPS)
        xhat = xf * rstd
        gnw_ref[0:1, :] += jnp.sum(du * xhat, axis=0, keepdims=True)
        dxh = du * nw_ref[...]
        dx_ref[...] = dout_ref[...] + rstd * (dxh - xhat * jnp.mean(dxh * xhat, axis=-1, keepdims=True))

    blk = lambda w: pl.BlockSpec((tm, w), lambda i: (i, 0))
    steps = t // tm
    kinds = ["scatter"] * len(scattered)
    hosted = _hosting(body, 9, 2, 0, kinds, lambda: pl.program_id(0) == 0, lambda: pl.program_id(0) == steps - 1)
    return _pc_comm(
        hosted, name="inproj_bwd_dx", grid=(steps,),
        in_specs=[blk(D_MODEL), blk(D_MODEL), _full((1, D_MODEL)), _full((IN_DIM, D_MODEL))] +
                 [blk(c1 - c0) for _, c0, c1 in GROUPS] + [ANY] * len(scattered),
        out_specs=[blk(D_MODEL), _full((8, D_MODEL))] + [ANY] * len(scattered),
        out_shape=[jax.ShapeDtypeStruct((t, D_MODEL), f32), jax.ShapeDtypeStruct((8, D_MODEL), f32)] +
                  [_exchange_out_shape("scatter", a) for a in scattered],
        scratch_shapes=_exchange_sems(len(scattered)), compiler_params=_cparams(("arbitrary",)),
    )(x, dout, norm_w, w_perm, *dgroups, *scattered)


def grad_w_group(u, dg, name, scattered=()):
    t, n = dg.shape
    tn = n if n <= 1536 else 1024
    budget = 40 * 1024 * 1024
    tm = next((c for c in (4096, 2048, 1024, 512, 256)
               if t % c == 0 and tn * D_MODEL * 4 + 2 * (c * tn * 4 + c * D_MODEL * 2 + tn * D_MODEL * 2) <= budget), t)
    nj, nk = n // tn, t // tm
    f32 = jnp.float32

    def body(u_ref, d_ref, o_ref, acc):
        k = pl.program_id(1)

        @pl.when(k == 0)
        def _():
            acc[...] = jnp.zeros_like(acc)

        acc[...] += lax.dot_general(d_ref[...].astype(_MM), u_ref[...], (((0,), (0,)), ((), ())),
                                    preferred_element_type=f32)

        @pl.when(k == nk - 1)
        def _():
            o_ref[...] = acc[...].astype(o_ref.dtype)

    ne = len(scattered)
    hosted = _hosting(body, 2, 1, 1, ["scatter"] * ne,
                      lambda: (pl.program_id(0) == 0) & (pl.program_id(1) == 0),
                      lambda: (pl.program_id(0) == nj - 1) & (pl.program_id(1) == nk - 1))
    res = (_pc_comm if ne else _pc)(
        hosted, name=name, grid=(nj, nk),
        in_specs=[pl.BlockSpec((tm, D_MODEL), lambda j, k: (k, 0)),
                  pl.BlockSpec((tm, tn), lambda j, k: (k, j))] + [ANY] * ne,
        out_specs=[pl.BlockSpec((tn, D_MODEL), lambda j, k: (j, 0))] + [ANY] * ne,
        out_shape=[jax.ShapeDtypeStruct((n, D_MODEL), _MM)] + [_exchange_out_shape("scatter", a) for a in scattered],
        scratch_shapes=[pltpu.VMEM((tn, D_MODEL), f32)] + _exchange_sems(ne),
        compiler_params=_cparams(("arbitrary", "arbitrary")),
    )(u, dg, *scattered)
    return res if ne else res[0]


def _pad_lanes(v, off):
    n = v.shape[-1]
    return jnp.pad(v.reshape(1, n).astype(jnp.float32), ((0, 0), (off, 128 - off - n)))


REF_ROWS = dict(z=(0, 1024), xbc=(1024, 2560), dt=(2560, 2576), gate=(2576, 3600), qkv=(3600, 6672), ab=(6672, 6688))


def unperm_w_in(gz, gxbc, ggate, gqkv, gsm):
    src = dict(z=gz, xbc=gxbc, dt=gsm[0:16], gate=ggate, qkv=gqkv, ab=gsm[16:32])
    slabs = []
    for k in range(N_DEV):
        a, b = k * W_IN_SHARD, (k + 1) * W_IN_SHARD
        parts = []
        for name, (s, e) in REF_ROWS.items():
            lo, hi = max(a, s), min(b, e)
            if lo < hi:
                parts.append(src[name][lo - s:hi - s])
        slabs.append(jnp.concatenate(parts, axis=0))
    return jnp.stack(slabs)


def all_gather(arrs, name):
    n = len(arrs)

    def body(*refs):
        ins, outs = refs[:n], refs[n:2 * n]
        send_sems, recv_sems, local_sems = refs[2 * n:]
        x, y, c, me = _me()
        sibling = (x, y, 1 - c)
        chips = [(1 - x, y), (x, 1 - y), (1 - x, 1 - y)]

        def idx(px, py, pc):
            return 4 * px + 2 * py + pc

        def copy(a, k, block, to, src=None):
            slot = outs[a].at[idx(*block)]
            return pltpu.make_async_remote_copy(src_ref=slot if src is None else src, dst_ref=slot,
                                                send_sem=send_sems.at[a, k], recv_sem=recv_sems.at[a, k],
                                                device_id=to, device_id_type=MESH)

        local = [pltpu.make_async_copy(ins[a], outs[a].at[me], local_sems.at[a]) for a in range(n)]
        for cp in local:
            cp.start()
        started = []
        for a in range(n):
            first = [copy(a, 0, (x, y, c), sibling, src=ins[a])]
            first += [copy(a, 1 + j, (x, y, c), (*chip, c), src=ins[a]) for j, chip in enumerate(chips)]
            for cp in first:
                cp.start()
            started += first
        for a in range(n):
            for j, chip in enumerate(chips):
                copy(a, 1 + j, (*chip, c), (x, y, c)).wait_recv()
                fwd = copy(a, 4 + j, (*chip, c), sibling)
                fwd.start()
                started.append(fwd)
        for a in range(n):
            copy(a, 0, sibling, (x, y, c)).wait_recv()
            for j, chip in enumerate(chips):
                copy(a, 4 + j, (*chip, 1 - c), (x, y, c)).wait_recv()
        for cp in started:
            cp.wait_send()
        for cp in local:
            cp.wait()

    return _pc_comm(
        body, name=name, in_specs=[ANY] * n, out_specs=[ANY] * n,
        out_shape=[jax.ShapeDtypeStruct((N_DEV,) + a.shape, a.dtype) for a in arrs],
        scratch_shapes=[pltpu.SemaphoreType.DMA((n, 7)), pltpu.SemaphoreType.DMA((n, 7)),
                        pltpu.SemaphoreType.DMA((n,))],
    )(*arrs)


def adamw_sum(recv, w, m, v, rows, name, cols=None):
    r, ccols = w.shape
    f32 = jnp.float32
    c1 = 1.0 / (1.0 - ADAM_B1 ** ADAM_STEP)
    c2 = 1.0 / (1.0 - ADAM_B2 ** ADAM_STEP)

    def body(recv_ref, w_ref, m_ref, v_ref, g_ref, d_ref, mo_ref, vo_ref):
        g = recv_ref[0].astype(f32)
        for k in range(1, N_DEV):
            g = g + recv_ref[k].astype(f32)
        mn = ADAM_B1 * m_ref[...] + (1.0 - ADAM_B1) * g
        vn = ADAM_B2 * v_ref[...] + (1.0 - ADAM_B2) * (g * g)
        g_ref[...] = g
        mo_ref[...] = mn
        vo_ref[...] = vn
        d_ref[...] = -ADAM_LR * ((mn * c1) / (jnp.sqrt(vn * c2) + ADAM_EPS) + ADAM_WD * w_ref[...])

    if cols is None:
        blk = pl.BlockSpec((rows, ccols), lambda i: (i, 0))
        rblk, steps = pl.BlockSpec((N_DEV, rows, ccols), lambda i: (0, i, 0)), r // rows
    else:
        blk = pl.BlockSpec((r, cols), lambda i: (0, i))
        rblk, steps = pl.BlockSpec((N_DEV, r, cols), lambda i: (0, 0, i)), ccols // cols
    return _pc(
        body, name=name, grid=(steps,),
        in_specs=[rblk, blk, blk, blk],
        out_specs=[blk] * 4, out_shape=[jax.ShapeDtypeStruct((r, ccols), f32)] * 4,
        compiler_params=_cparams(("arbitrary",)),
    )(recv, w, m, v)


SMALL = (("norm_w", 1, 1024, 0), ("ssd_conv_b", 1, 1536, 0), ("ssd_dt_bias", 1, 16, 0), ("ssd_a_log", 1, 16, 0),
         ("ssd_d", 1, 16, 0), ("ssd_norm_w", 1, 1024, 0), ("gdn_dt_bias", 1, 8, 16), ("gdn_a_log", 1, 8, 16),
         ("gdn_norm_w", 1, 128, 0), ("final_norm_w", 1, 1024, 0),
         ("ssd_conv_w", CONV_K, SSD_CONV_DIM // N_DEV, 0), ("gdn_conv_w", CONV_K, GDN_CONV_DIM // N_DEV, 0))


def _small_layout():
    out, off = [], 0
    for name, rows, n, lane0 in SMALL + (("loss", 1, 128, 0),):
        stride = -(-(lane0 + n) // 128) * 128
        out.append((name, rows, n, lane0, stride, off))
        off += rows * stride
    return out, off


def scatter_small(accs):
    layout, total = _small_layout()
    f32 = jnp.float32

    def body(*refs):
        acc_refs, out_ref, slabs = refs[:len(layout)], refs[len(layout)], refs[len(layout) + 1]
        sems = refs[len(layout) + 2:]
        slabs[...] = jnp.zeros_like(slabs)
        for (name, rows, n, lane0, stride, off), acc in zip(layout, acc_refs):
            for k in range(N_DEV):
                if rows == 1:
                    slabs[k, :, off:off + stride] = acc[0:1, 0:stride]
                else:
                    for j in range(rows):
                        slabs[k, :, off + stride * j:off + stride * j + n] = acc[j:j + 1, n * k:n * k + n]
        start, wait = _exchange_ops("scatter", slabs, out_ref, *sems)
        start()
        wait()

    return _pc_comm(
        body, name="scatter_small_grads", out_specs=ANY, out_shape=jax.ShapeDtypeStruct((N_DEV, 1, total), f32),
        scratch_shapes=[pltpu.VMEM((N_DEV, 1, total), f32)] + _exchange_sems(1),
    )(*accs)


def adamw_small(recv, w, m, v):
    layout, total = _small_layout()
    loss_off = layout[-1][5]
    layout = layout[:-1]
    f32 = jnp.float32
    c1 = 1.0 / (1.0 - ADAM_B1 ** ADAM_STEP)
    c2 = 1.0 / (1.0 - ADAM_B2 ** ADAM_STEP)
    np_ = len(layout)

    def body(*refs):
        recv_ref = refs[0]
        w_refs, m_refs, v_refs = refs[1:1 + np_], refs[1 + np_:1 + 2 * np_], refs[1 + 2 * np_:1 + 3 * np_]
        o_refs = refs[1 + 3 * np_:]
        g_all = recv_ref[0]
        for k in range(1, N_DEV):
            g_all = g_all + recv_ref[k]
        o_refs[4 * np_][...] = g_all[:, loss_off:loss_off + 128]

        def update(g, wv, mv, vv):
            mn = ADAM_B1 * mv + (1.0 - ADAM_B1) * g
            vn = ADAM_B2 * vv + (1.0 - ADAM_B2) * (g * g)
            return g, -ADAM_LR * ((mn * c1) / (jnp.sqrt(vn * c2) + ADAM_EPS) + ADAM_WD * wv), mn, vn

        for p, (name, rows, n, lane0, stride, off) in enumerate(layout):
            outs = o_refs[4 * p:4 * p + 4]
            if rows == 1:
                res = update(g_all[:, off + lane0:off + lane0 + n], w_refs[p][...], m_refs[p][...], v_refs[p][...])
                for o, r in zip(outs, res):
                    o[...] = r
            else:
                for j in range(rows):
                    res = update(g_all[:, off + stride * j:off + stride * j + n], w_refs[p][0, j:j + 1, :],
                                 m_refs[p][0, j:j + 1, :], v_refs[p][0, j:j + 1, :])
                    for o, r in zip(outs, res):
                        o[0, j:j + 1, :] = r

    names = [e[0] for e in layout]
    ins = [recv] + [d[nm] for d in (w, m, v) for nm in names]
    out_shape = [jax.ShapeDtypeStruct(w[nm].shape, f32) for nm in names for _ in range(4)]
    out_shape.append(jax.ShapeDtypeStruct((1, 128), f32))
    res = _pc(body, name="adamw_small", out_shape=out_shape)(*ins)
    return {nm: tuple(res[4 * p:4 * p + 4]) for p, nm in enumerate(names)}, res[4 * np_]


SHARD = (("ssd_conv_w", CONV_K * SSD_CONV_DIM // N_DEV), ("gdn_conv_w", CONV_K * GDN_CONV_DIM // N_DEV))
SHARD_ROWS = 24


def _rows_of(size):
    return -(-size // 128)


def _pack(vals, layout, total_rows):
    parts = []
    for (name, size), val in zip(layout, vals):
        flat = val.reshape(-1).astype(jnp.float32)
        parts.append(jnp.pad(flat, (0, _rows_of(size) * 128 - size)).reshape(-1, 128))
    used = sum(_rows_of(s) for _, s in layout)
    parts.append(jnp.zeros((total_rows - used, 128), jnp.float32))
    return jnp.concatenate(parts, axis=0)


def _conv_full(gathered_flat, ccols):
    return gathered_flat.reshape(N_DEV, CONV_K, ccols // N_DEV).transpose(1, 0, 2).reshape(CONV_K, ccols)


def kernel(x, norm_w, w_in, ssd_conv_w, ssd_conv_b, ssd_dt_bias, ssd_a_log, ssd_d, ssd_norm_w, gdn_conv_w, gdn_dt_bias, gdn_a_log, gdn_norm_w, w_out, final_norm_w, loss_target, m_norm_w, m_w_in, m_ssd_conv_w, m_ssd_conv_b, m_ssd_dt_bias, m_ssd_a_log, m_ssd_d, m_ssd_norm_w, m_gdn_conv_w, m_gdn_dt_bias, m_gdn_a_log, m_gdn_norm_w, m_w_out, m_final_norm_w, v_norm_w, v_w_in, v_ssd_conv_w, v_ssd_conv_b, v_ssd_dt_bias, v_ssd_a_log, v_ssd_d, v_ssd_norm_w, v_gdn_conv_w, v_gdn_dt_bias, v_gdn_a_log, v_gdn_norm_w, v_w_out, v_final_norm_w):
    f32 = jnp.float32
    w = dict(norm_w=norm_w, w_in=w_in, ssd_conv_w=ssd_conv_w, ssd_conv_b=ssd_conv_b, ssd_dt_bias=ssd_dt_bias,
             ssd_a_log=ssd_a_log, ssd_d=ssd_d, ssd_norm_w=ssd_norm_w, gdn_conv_w=gdn_conv_w, gdn_dt_bias=gdn_dt_bias,
             gdn_a_log=gdn_a_log, gdn_norm_w=gdn_norm_w, w_out=w_out, final_norm_w=final_norm_w)
    m = dict(norm_w=m_norm_w, w_in=m_w_in, ssd_conv_w=m_ssd_conv_w, ssd_conv_b=m_ssd_conv_b, ssd_dt_bias=m_ssd_dt_bias,
             ssd_a_log=m_ssd_a_log, ssd_d=m_ssd_d, ssd_norm_w=m_ssd_norm_w, gdn_conv_w=m_gdn_conv_w,
             gdn_dt_bias=m_gdn_dt_bias, gdn_a_log=m_gdn_a_log, gdn_norm_w=m_gdn_norm_w, w_out=m_w_out,
             final_norm_w=m_final_norm_w)
    v = dict(norm_w=v_norm_w, w_in=v_w_in, ssd_conv_w=v_ssd_conv_w, ssd_conv_b=v_ssd_conv_b, ssd_dt_bias=v_ssd_dt_bias,
             ssd_a_log=v_ssd_a_log, ssd_d=v_ssd_d, ssd_norm_w=v_ssd_norm_w, gdn_conv_w=v_gdn_conv_w,
             gdn_dt_bias=v_gdn_dt_bias, gdn_a_log=v_gdn_a_log, gdn_norm_w=v_gdn_norm_w, w_out=v_w_out,
             final_norm_w=v_final_norm_w)
    names = list(w)
    shapes = {n: w[n].shape for n in names}

    xl, tgt = x[0], loss_target[0]
    cs = _consts()
    dtb_s = _pad_lanes(ssd_dt_bias, 0)
    alog_s = _pad_lanes(ssd_a_log, 0)
    dpar = _pad_lanes(ssd_d, 0)
    dtb_g = _pad_lanes(gdn_dt_bias, 16)
    alog_g = _pad_lanes(gdn_a_log, 16)
    nw_g = gdn_norm_w.reshape(1, 128)
    nw_s = ssd_norm_w.reshape(1, 1024)
    cb_s = ssd_conv_b.reshape(1, 1536)
    nw1 = norm_w.reshape(1, D_MODEL)

    (g_w_in,) = all_gather([w_in[0].T.astype(_MM)], "gather_w_in")
    w_perm = g_w_in.reshape(IN_DIM, D_MODEL)
    conv_pack = _pack([w["ssd_conv_w"], w["gdn_conv_w"]], SHARD, SHARD_ROWS)
    u, z, xbc, gate, qkv, sm, g_w_out, g_conv = inproj_fwd(xl, nw1, w_perm, [w_out[0].astype(_MM), conv_pack])
    w_out_full = g_w_out.reshape(MIX_WIDTH, D_MODEL)
    ssd_cw = _conv_full(g_conv[:, 0:6].reshape(N_DEV, -1), SSD_CONV_DIM)
    gdn_cw = _conv_full(g_conv[:, 6:18].reshape(N_DEV, -1), GDN_CONV_DIM)

    nc = xl.shape[0] // CHUNK
    y_ssd, hs, pre_s, y_gdn, ss, ts, pre_g = _chunk_call(
        [ssd_fwd(z, xbc, sm, ssd_cw, cb_s, dtb_s, alog_s, dpar, nw_s, cs),
         gdn_fwd(gate, qkv, sm, gdn_cw, dtb_g, alog_g, nw_g, cs)], "scan_fwd", nc, False)
    dout, dys, dyg, g_wout, g_fnw, loss_l = out_fwd_bwd(xl, tgt, y_ssd, y_gdn, w_out_full,
                                                        final_norm_w.reshape(1, D_MODEL))
    (dz, dxbc, g_cw_s, g_cb_s, g_dtb_s, g_alog_s, g_d, g_nw_s,
     dgate, dqkv, dsm, g_cw_g, g_dtb_g, g_alog_g, g_nw_g) = _chunk_call(
        [ssd_bwd(z, xbc, pre_s, sm, hs, dys, ssd_cw, dtb_s, alog_s, dpar, nw_s, cs),
         gdn_bwd(gate, qkv, pre_g, sm, ss, ts, dyg, gdn_cw, dtb_g, alog_g, nw_g, cs)], "scan_bwd", nc, True)

    t_w_out = g_wout.reshape(N_DEV, MIX_WIDTH // N_DEV, D_MODEL)
    gws = {}
    for dg, (name, _, _) in zip((dz, dxbc, dgate, dsm), (GROUPS[0], GROUPS[1], GROUPS[2], GROUPS[4])):
        gws[name] = grad_w_group(u, dg, "grad_w_in_" + name)
    gws["qkv"], r_w_out = grad_w_group(u, dqkv, "grad_w_in_qkv", [t_w_out])
    t_w_in = unperm_w_in(gws["z"], gws["xbc"], gws["gate"], gws["qkv"], gws["sm"])
    dx, g_nw, r_w_in = inproj_bwd_dx(xl, dout, nw1, w_perm, (dz, dxbc, dgate, dqkv, dsm), [t_w_in])

    accs = dict(norm_w=g_nw, ssd_conv_b=g_cb_s, ssd_dt_bias=g_dtb_s, ssd_a_log=g_alog_s, ssd_d=g_d,
                ssd_norm_w=g_nw_s, gdn_dt_bias=g_dtb_g, gdn_a_log=g_alog_g, gdn_norm_w=g_nw_g, final_norm_w=g_fnw,
                ssd_conv_w=g_cw_s, gdn_conv_w=g_cw_g)
    r_small = scatter_small([accs[e[0]] for e in SMALL] + [loss_l])

    o_w_in = adamw_sum(r_w_in, w_in[0].T, m_w_in[0].T, v_w_in[0].T, None, "adamw_w_in", cols=256)
    o_w_out = adamw_sum(r_w_out, w_out[0], m_w_out[0], v_w_out[0], 64, "adamw_w_out")
    row = lambda d: {n: (a.reshape(1, -1) if a.ndim == 1 else a) for n, a in d.items()}
    o_small, loss_sum = adamw_small(r_small, row(w), row(m), row(v))

    loss = loss_sum[0, 0]
    outs = [loss, dx[None]]
    for k in range(4):
        parts = {n: o_small[n][k] for n in o_small}
        parts["w_in"] = o_w_in[k].T
        parts["w_out"] = o_w_out[k]
        outs += [parts[n].reshape(shapes[n]) for n in names]
    return tuple(outs)
```

```python
import functools

import jax
import jax.numpy as jnp
import numpy as np
from jax import lax
from jax.experimental import pallas as pl
from jax.experimental.pallas import tpu as pltpu

_MM = jnp.bfloat16

D_MODEL = 1024
CHUNK = 64
CONV_K = 4
EPS = 1e-6
SSD_CONV_DIM = 1536
GDN_HEADS = 8
GDN_DK = 128
GDN_CONV_DIM = 3072
MIX_WIDTH = 2048
IN_DIM = 6688
N_DEV = 8
W_IN_SHARD = IN_DIM // N_DEV
HI = lax.Precision.HIGHEST
HIGH = lax.Precision.HIGH
VMEM_LIMIT = 56 * 1024 * 1024

ADAM_LR = 0.001
ADAM_B1 = 0.9
ADAM_B2 = 0.999
ADAM_EPS = 1e-08
ADAM_WD = 0.01
ADAM_STEP = 10


def _pc(body, **kw):
    return pl.pallas_call(body, **kw)


def _pc_comm(body, **kw):
    return pl.pallas_call(body, **kw)


def _cparams(sem):
    return pltpu.CompilerParams(dimension_semantics=sem, vmem_limit_bytes=VMEM_LIMIT)


def _sig(x):
    return 0.5 * jnp.tanh(0.5 * x) + 0.5


@jax.custom_vjp
def _sigmoid(x):
    return _sig(x)


def _sigmoid_fwd(x):
    s = _sig(x)
    return s, s


def _sigmoid_bwd(s, g):
    return (g * s * (1.0 - s),)


_sigmoid.defvjp(_sigmoid_fwd, _sigmoid_bwd)


@jax.custom_vjp
def _silu(x):
    return x * _sig(x)


def _silu_fwd(x):
    s = _sig(x)
    return x * s, (x, s)


def _silu_bwd(res, g):
    x, s = res
    return (g * (s * (1.0 + x * (1.0 - s))),)


_silu.defvjp(_silu_fwd, _silu_bwd)


def _softplus_impl(x):
    return jnp.maximum(x, 0.0) + jnp.log(1.0 + jnp.exp(-jnp.abs(x)))


@jax.custom_vjp
def _softplus(x):
    return _softplus_impl(x)


def _softplus_fwd(x):
    return _softplus_impl(x), x


def _softplus_bwd(x, g):
    return (g * _sig(x),)


_softplus.defvjp(_softplus_fwd, _softplus_bwd)


def _lane_bcast_impl(x, k):
    return jnp.broadcast_to(x[..., k:k + 1], x.shape)


@functools.partial(jax.custom_vjp, nondiff_argnums=(1,))
def _lane_bcast(x, k):
    return _lane_bcast_impl(x, k)


def _lane_bcast_fwd(x, k):
    return _lane_bcast_impl(x, k), None


def _lane_bcast_bwd(k, _, g):
    lane = lax.broadcasted_iota(jnp.int32, g.shape, g.ndim - 1)
    return (jnp.where(lane == k, jnp.sum(g, axis=-1, keepdims=True), 0.0),)


_lane_bcast.defvjp(_lane_bcast_fwd, _lane_bcast_bwd)


def _mm(a, b):
    return jnp.dot(a.astype(_MM), b.astype(_MM), preferred_element_type=jnp.float32)


def _mm_nt(a, b):
    return lax.dot_general(a.astype(_MM), b.astype(_MM), (((1,), (1,)), ((), ())),
                           preferred_element_type=jnp.float32)


def _mm_tn(a, b):
    return lax.dot_general(a.astype(_MM), b.astype(_MM), (((0,), (0,)), ((), ())),
                           preferred_element_type=jnp.float32)


def _dot_hi(a, b):
    return jnp.dot(a, b, precision=HI, preferred_element_type=jnp.float32)


def _bmm(a, b):
    return lax.dot_general(a.astype(_MM), b.astype(_MM), (((2,), (1,)), ((0,), (0,))),
                           preferred_element_type=jnp.float32)


def _bmm_nt(a, b):
    return lax.dot_general(a.astype(_MM), b.astype(_MM), (((2,), (2,)), ((0,), (0,))),
                           preferred_element_type=jnp.float32)


def _bmm_tn(a, b):
    return lax.dot_general(a.astype(_MM), b.astype(_MM), (((1,), (1,)), ((0,), (0,))),
                           preferred_element_type=jnp.float32)


def _bmm_hi(a, b):
    return lax.dot_general(a, b, (((2,), (1,)), ((0,), (0,))), precision=HIGH, preferred_element_type=jnp.float32)


def _bmm_nt_hi(a, b):
    return lax.dot_general(a, b, (((2,), (2,)), ((0,), (0,))), precision=HIGH, preferred_element_type=jnp.float32)


def _bmm_tn_hi(a, b):
    return lax.dot_general(a, b, (((1,), (1,)), ((0,), (0,))), precision=HIGH, preferred_element_type=jnp.float32)


def _consts():
    l = np.arange(CHUNK)
    tri = (l[:, None] >= l[None, :]).astype(np.float32)
    lane = np.arange(128)
    i2 =(l[:, None] == (lane[None, :] % 64)).astype(np.float32)
    mask2 = (l[:, None] >= (lane[None, :] % 64)).astype(np.float32)
    lo = (lane < 64).astype(np.float32)[None, :]
    i64 = np.eye(CHUNK, dtype=np.float32)
    strict = (l[:, None] > l[None, :]).astype(np.float32)
    return dict(tri=jnp.asarray(tri), i2=jnp.asarray(i2), mask2=jnp.asarray(mask2), lo=jnp.asarray(lo),
                i64=jnp.asarray(i64), strict=jnp.asarray(strict))


def _ssd_chunk(xs_pre, b_pre, c_pre, z, sm, ht, dtb, alog, dpar, nw, tri, i2, mask2, lo):
    lane = lax.broadcasted_iota(jnp.int32, (1, 128), 1)
    m16 = lane < 16
    dt = jnp.where(m16, _softplus(sm + dtb), 0.0)
    a_neg = -jnp.exp(alog)
    cum = _dot_hi(tri, dt * a_neg)
    row = lax.broadcasted_iota(jnp.int32, (CHUNK, 1), 0)
    is_last = row == CHUNK - 1
    hi = 1.0 - lo
    bm = [_silu(b) for b in b_pre]
    cm = [_silu(c) for c in c_pre]
    cb2 = [_mm_nt(cm[g], jnp.concatenate([bm[g], bm[g]], axis=0)) for g in range(2)]
    ht_g = [jnp.concatenate(ht[4 * g:4 * g + 4], axis=1) for g in range(2)]
    yoff_g = [_mm(cm[g], ht_g[g]) for g in range(2)]
    yg, xdec, clast = [], [], []
    for j in range(8):
        g, k4 = j // 4, j % 4
        pair = lambda v, j=j: jnp.where(lo > 0.5, _lane_bcast(v, 2 * j), _lane_bcast(v, 2 * j + 1))
        xs = _silu(xs_pre[j])
        dte = pair(dt)
        cume = pair(cum)
        cum_last = jnp.sum(jnp.where(is_last, cume, 0.0), axis=0, keepdims=True)
        xdt = xs * dte
        rowv = jnp.sum(cume * i2, axis=0, keepdims=True)
        lm = jnp.exp(jnp.where(mask2 > 0.5, cume - rowv, -jnp.inf))
        m = cb2[g] * lm
        xblk = jnp.concatenate([xdt * lo, xdt * hi], axis=0)
        y = _mm(m, xblk)
        y = y + yoff_g[g][:, 128 * k4:128 * k4 + 128] * jnp.exp(cume)
        y = y + pair(dpar) * xs
        yg.append(y * _silu(z[j]))
        xdec.append(xdt * jnp.exp(cum_last - cume))
        clast.append(cum_last)
    ht_next = []
    for g in range(2):
        st = _mm_tn(bm[g], jnp.concatenate(xdec[4 * g:4 * g + 4], axis=1))
        for k4 in range(4):
            j = 4 * g + k4
            ht_next.append(ht[j] * jnp.exp(clast[j]) + st[:, 128 * k4:128 * k4 + 128])
    outs = []
    for g in range(2):
        ss = sum(jnp.sum(yg[j] * yg[j], axis=-1, keepdims=True) for j in range(4 * g, 4 * g + 4))
        rs = lax.rsqrt(ss * (1.0 / 512.0) + EPS)
        for j in range(4 * g, 4 * g + 4):
            outs.append(yg[j] * rs * nw[j])
    return outs, ht_next


def _tri_inverse(a):
    eye = jnp.eye(CHUNK, dtype=jnp.float32)[None]
    p = eye - a
    x = _bmm_hi(a, a)
    for _ in range(4):
        both = _bmm_hi(jnp.concatenate([p, x], axis=1), x)
        p = p + both[:, :CHUNK]
        x = both[:, CHUNK:]
    return p + _bmm_hi(p, x)


def _solve_apply(t, r1, r2):
    both = _bmm_hi(t, jnp.concatenate([r1, r2], axis=-1))
    n = r1.shape[-1]
    return both[..., :n], both[..., n:]


@jax.custom_vjp
def _solve(a, r1, r2, t):
    return _solve_apply(t, r1, r2)


def _solve_fwd(a, r1, r2, t):
    u, w = _bmm_hi(t, r1), _bmm_hi(t, r2)
    return (u, w), (t, u, w)


def _solve_bwd(res, cts):
    t, u, w = res
    du, dw = cts
    dr1 = _bmm_tn_hi(t, du)
    dr2 = _bmm_tn_hi(t, dw)
    da = -(_bmm_nt_hi(dr1, u) + _bmm_nt_hi(dr2, w))
    return da, dr1, dr2, jnp.zeros_like(t)


_solve.defvjp(_solve_fwd, _solve_bwd)


def _gdn_chunk(q_pre, k_pre, v_pre, gate, sm, s, dtb, alog, nw, tri, i64, strict, t_in=None):
    lane = lax.broadcasted_iota(jnp.int32, (1, 128), 1)
    m_a = (lane >= 16) & (lane < 24)
    g_full = jnp.where(m_a, -jnp.exp(alog) * _softplus(sm + dtb), 0.0)
    gc = _dot_hi(tri, g_full)
    sig = _sigmoid(sm)
    heads = lambda f: jnp.concatenate([f(h)[None] for h in range(GDN_HEADS)], axis=0)
    gc3 = heads(lambda h: _lane_bcast(gc, 16 + h))
    beta3 = heads(lambda h: _lane_bcast(sig, 24 + h))
    q = _silu(q_pre)
    q = q * lax.rsqrt(jnp.sum(q * q, axis=-1, keepdims=True) + EPS) * (GDN_DK ** -0.5)
    k = _silu(k_pre)
    k = k * lax.rsqrt(jnp.sum(k * k, axis=-1, keepdims=True) + EPS)
    v = _silu(v_pre)
    gcl = gc3[:, :, :CHUNK]
    gc_row = jnp.sum(gcl * i64[None], axis=1, keepdims=True)
    incl = (strict + i64)[None] > 0.5
    decay = jnp.exp(jnp.where(incl, gcl - gc_row, -jnp.inf))
    kb = k * beta3
    a = jnp.where(strict[None] > 0.5, _bmm_nt(kb, k) * decay, 0.0)
    egc = jnp.exp(gc3)
    t = _tri_inverse(a) if t_in is None else t_in
    u, w = _solve(a, v * beta3, kb * egc, t)
    attn = _bmm_nt(q, k) * decay
    row = lax.broadcasted_iota(jnp.int32, (1, CHUNK, 1), 1)
    gl = jnp.sum(jnp.where(row == CHUNK - 1, gc3, 0.0), axis=1, keepdims=True)
    q_dec = q * egc
    k_dec = k * jnp.exp(gl - gc3)
    ws = _bmm(jnp.concatenate([w, q_dec], axis=1), s)
    v_new = u - ws[:, :CHUNK]
    o = ws[:, CHUNK:] + _bmm(attn, v_new)
    s_next = s * jnp.exp(gl) + _bmm_tn(k_dec, v_new)
    on = o * lax.rsqrt(jnp.mean(o * o, axis=-1, keepdims=True) + EPS) * nw
    return on * _silu(gate), s_next, t


def _conv_fwd(pbuf, w_ref, c0, c1):
    blk = pbuf[:, c0:c1]
    acc = w_ref[CONV_K - 1:CONV_K, c0:c1] * blk[8:72]
    for j in range(CONV_K - 1):
        acc = acc + w_ref[j:j + 1, c0:c1] * pltpu.roll(blk, CONV_K - 1 - j, axis=0)[8:72]
    return acc


MESH = pl.DeviceIdType.MESH
ANY = pl.BlockSpec(memory_space=pl.ANY)


def _me():
    x, y, c = lax.axis_index("x"), lax.axis_index("y"), lax.axis_index("c")
    return x, y, c, 4 * x + 2 * y + c


def _peer(r):
    x, y, c, _ = _me()
    px = 1 - x if r & 4 else x
    py = 1 - y if r & 2 else y
    pc = 1 - c if r & 1 else c
    return (px, py, pc), 4 * px + 2 * py + pc


def _exchange_ops(kind, in_ref, out_ref, send_sems, recv_sems, local_sem):
    me = _me()[3]
    local = pltpu.make_async_copy(in_ref.at[me] if kind == "scatter" else in_ref, out_ref.at[me], local_sem)
    sends, recvs = [], []
    for r in range(1, N_DEV):
        peer, pidx = _peer(r)
        src = in_ref.at[pidx] if kind == "scatter" else in_ref
        sems = dict(send_sem=send_sems.at[r - 1], recv_sem=recv_sems.at[r - 1], device_id=peer, device_id_type=MESH)
        sends.append(pltpu.make_async_remote_copy(src_ref=src, dst_ref=out_ref.at[me], **sems))
        recvs.append(pltpu.make_async_remote_copy(src_ref=src, dst_ref=out_ref.at[pidx], **sems))

    def start():
        local.start()
        for cp in sends:
            cp.start()

    def wait():
        for cp in recvs:
            cp.wait_recv()
        for cp in sends:
            cp.wait_send()
        local.wait()

    return start, wait


def _exchange_sems(n):
    return [pltpu.SemaphoreType.DMA((N_DEV - 1,)), pltpu.SemaphoreType.DMA((N_DEV - 1,)),
            pltpu.SemaphoreType.DMA(())] * n


def _exchange_out_shape(kind, a):
    return jax.ShapeDtypeStruct(a.shape if kind == "scatter" else (N_DEV,) + a.shape, a.dtype)


def _hosting(body, n_in, n_out, n_scratch, kinds, first, last):
    ne = len(kinds)

    def wrapped(*refs):
        ins, ex_in = refs[:n_in], refs[n_in:n_in + ne]
        o0 = n_in + ne
        outs, ex_out = refs[o0:o0 + n_out], refs[o0 + n_out:o0 + n_out + ne]
        s0 = o0 + n_out + ne
        scr, sems = refs[s0:s0 + n_scratch], refs[s0 + n_scratch:]
        ops = [_exchange_ops(kinds[e], ex_in[e], ex_out[e], *sems[3 * e:3 * e + 3]) for e in range(ne)]

        @pl.when(first())
        def _():
            for start, _ in ops:
                start()

        body(*ins, *outs, *scr)

        @pl.when(last())
        def _():
            for _, wait in ops:
                wait()

    return wrapped


GROUPS = (("z", 0, 1024), ("xbc", 1024, 2560), ("gate", 2560, 3584), ("qkv", 3584, 6656), ("sm", 6656, 6784))
GROUP_ROWS = dict(z=((0, 1024),), xbc=((1024, 2560),), gate=((2576, 3600),), qkv=((3600, 6672),),
                  sm=((2560, 2576), (6672, 6688)))


def _w_rows(w_ref, name, width):
    pieces = [w_ref[a:b, :] for a, b in GROUP_ROWS[name]]
    n = sum(b - a for a, b in GROUP_ROWS[name])
    if n < width:
        pieces.append(jnp.zeros((width - n, D_MODEL), w_ref.dtype))
    return pieces[0] if len(pieces) == 1 else jnp.concatenate(pieces, axis=0)


def inproj_fwd(x, norm_w, w_perm, gathered):
    t = x.shape[0]
    tm = min(512, t)
    steps = t // tm
    kinds = ["gather"] * len(gathered)

    def body(x_ref, nw_ref, w_ref, u_ref, z_ref, xbc_ref, gate_ref, qkv_ref, sm_ref):
        xf = x_ref[...]
        rstd = lax.rsqrt(jnp.mean(xf * xf, axis=-1, keepdims=True) + EPS)
        u = (xf * rstd * nw_ref[...]).astype(_MM)
        u_ref[...] = u
        for (name, c0, c1), o_ref in zip(GROUPS, (z_ref, xbc_ref, gate_ref, qkv_ref, sm_ref)):
            o_ref[...] = lax.dot_general(u, _w_rows(w_ref, name, c1 - c0), (((1,), (1,)), ((), ())),
                                         preferred_element_type=jnp.float32)

    outs = [jax.ShapeDtypeStruct((t, D_MODEL), _MM)] + [jax.ShapeDtypeStruct((t, c1 - c0), jnp.float32)
                                                        for _, c0, c1 in GROUPS]
    hosted = _hosting(body, 3, 6, 0, kinds, lambda: pl.program_id(0) == 0, lambda: pl.program_id(0) == steps - 1)
    return _pc_comm(
        hosted, name="inproj_fwd", grid=(steps,),
        in_specs=[pl.BlockSpec((tm, D_MODEL), lambda i: (i, 0)),
                  pl.BlockSpec((1, D_MODEL), lambda i: (0, 0)),
                  pl.BlockSpec((IN_DIM, D_MODEL), lambda i: (0, 0), pipeline_mode=pl.Buffered(1))] +
                 [ANY] * len(gathered),
        out_specs=[pl.BlockSpec((tm, D_MODEL), lambda i: (i, 0))] +
                  [pl.BlockSpec((tm, c1 - c0), lambda i: (i, 0)) for _, c0, c1 in GROUPS] + [ANY] * len(gathered),
        out_shape=outs + [_exchange_out_shape("gather", a) for a in gathered],
        scratch_shapes=_exchange_sems(len(gathered)), compiler_params=_cparams(("arbitrary",)),
    )(x, norm_w, w_perm, *gathered)


SUB_FWD = 4
SUB_BWD = 2


def _halo_spec(width, idx_fn):
    return pl.BlockSpec((8, width), lambda i: (jnp.maximum(idx_fn(i) * (SUB_FWD * CHUNK // 8) - 1, 0), 0))


def _when_first(shared, fn):
    if shared["first"] is not False:
        pl.when(shared["first"])(fn)


def _full(shape):
    nd = len(shape)
    return pl.BlockSpec(shape, lambda i: (0,) * nd)


def _ssd_split(pre_fn, z_ref, sm_ref):
    xs_pre = [pre_fn(128 * j, 128 * j + 128) for j in range(8)]
    b_pre = [pre_fn(1024 + 128 * g, 1152 + 128 * g) for g in range(2)]
    c_pre = [pre_fn(1280 + 128 * g, 1408 + 128 * g) for g in range(2)]
    z = [z_ref[:, 128 * j:128 * j + 128] for j in range(8)]
    return xs_pre, b_pre, c_pre, z, sm_ref[...]


def ssd_fwd(z, xbc, sm, conv_w, conv_b, dtb, alog, dpar, nw, cs):
    t = z.shape[0]
    nc = t // CHUNK

    def body(shared, z_ref, xbc_ref, halo_ref, sm_ref, cw_ref, cb_ref, dtb_ref, alog_ref, dpar_ref, nw_ref,
             tri_ref, i2_ref, mask2_ref, lo_ref, y_ref, hs_ref, pre_ref, pbuf, ht_scr):
        def init():
            ht_scr[...] = jnp.zeros_like(ht_scr)

        _when_first(shared, init)
        pbuf[0:8, :] = jnp.where(shared["first"], 0.0, halo_ref[...])
        pbuf[8:72, :] = xbc_ref[...]

        def pre_fn(c0, c1):
            pre = _conv_fwd(pbuf, cw_ref, c0, c1) + cb_ref[:, c0:c1]
            pre_ref[:, c0:c1] = pre
            return pre

        xs_pre, b_pre, c_pre, zz, smv = _ssd_split(pre_fn, z_ref, sm_ref)
        ht = [ht_scr[:, 128 * j:128 * j + 128] for j in range(8)]
        hs_ref[0] = ht_scr[...]
        nwl = [nw_ref[:, 128 * j:128 * j + 128] for j in range(8)]
        outs, ht_next = _ssd_chunk(xs_pre, b_pre, c_pre, zz, smv, ht, dtb_ref[...], alog_ref[...], dpar_ref[...],
                                   nwl, tri_ref[...], i2_ref[...], mask2_ref[...], lo_ref[...])
        for j in range(8):
            y_ref[:, 128 * j:128 * j + 128] = outs[j].astype(y_ref.dtype)
            ht_scr[:, 128 * j:128 * j + 128] = ht_next[j]

    blk = lambda w: pl.BlockSpec((SUB_FWD * CHUNK, w), lambda i: (i, 0))
    return dict(
        body=body,
        in_kinds=["rows", "rows", ("halo", 1), "rows"] + ["full"] * 10, out_kinds=["rows", "state", "rows"],
        in_specs=[blk(1024), blk(1536), _halo_spec(1536, lambda i: i), blk(128),
                  _full((CONV_K, 1536)), _full((1, 1536)), _full((1, 128)), _full((1, 128)), _full((1, 128)),
                  _full((1, 1024)), _full((64, 64)), _full((64, 128)), _full((64, 128)),
                  _full((1, 128))],
        out_specs=[blk(1024), pl.BlockSpec((SUB_FWD, 128, 1024), lambda i: (i, 0, 0)), blk(1536)],
        out_shape=[jax.ShapeDtypeStruct((t, 1024), _MM), jax.ShapeDtypeStruct((nc, 128, 1024), jnp.float32),
                   jax.ShapeDtypeStruct((t, 1536), jnp.float32)],
        scratch=[pltpu.VMEM((72, 1536), jnp.float32), pltpu.VMEM((128, 1024), jnp.float32)],
        args=[z, xbc, xbc, sm, conv_w, conv_b, dtb, alog, dpar, nw, cs["tri"], cs["i2"], cs["mask2"], cs["lo"]])


def _conv_bwd(dpre_list, col_ranges, dbuf, carry, x_ref, cw_ref, dx_ref, dcw_ref, dcb_ref, first):
    for dpre, (c0, c1) in zip(dpre_list, col_ranges):
        dbuf[0:64, c0:c1] = dpre
    dbuf[64:72, :] = jnp.where(first, 0.0, carry[...])
    carry[...] = dbuf[0:8, :]
    for (c0, c1) in col_ranges:
        xin = x_ref[:, c0:c1]
        blk = dbuf[:, c0:c1]
        acc = None
        for j in range(CONV_K):
            sh = blk[0:64] if j == CONV_K - 1 else pltpu.roll(blk, 72 - (CONV_K - 1 - j), axis=0)[0:64]
            term = cw_ref[j:j + 1, c0:c1] * sh
            acc = term if acc is None else acc + term
            dcw_ref[j:j + 1, c0:c1] += jnp.sum(xin * sh, axis=0, keepdims=True)
        dx_ref[:, c0:c1] = acc.astype(dx_ref.dtype)
        if dcb_ref is not None:
            dcb_ref[0:1, c0:c1] += jnp.sum(dbuf[0:64, c0:c1], axis=0, keepdims=True)


def ssd_bwd(z, xbc, pre, sm, hs, dy, conv_w, dtb, alog, dpar, nw, cs):
    t = z.shape[0]
    nc = t // CHUNK

    def body(shared, z_ref, xbc_ref, pre_ref, sm_ref, hs_ref, dy_ref, cw_ref, dtb_ref, alog_ref, dpar_ref, nw_ref,
             tri_ref, i2_ref, mask2_ref, lo_ref,
             dz_ref, dxbc_ref, dcw_ref, dcb_ref, ddtb_ref, dalog_ref, ddpar_ref, dnw_ref,
             dbuf, carry, dht_scr):
        def init():
            dht_scr[...] = jnp.zeros_like(dht_scr)
            dcw_ref[...] = jnp.zeros_like(dcw_ref)
            dcb_ref[...] = jnp.zeros_like(dcb_ref)
            ddtb_ref[...] = jnp.zeros_like(ddtb_ref)
            dalog_ref[...] = jnp.zeros_like(dalog_ref)
            ddpar_ref[...] = jnp.zeros_like(ddpar_ref)
            dnw_ref[...] = jnp.zeros_like(dnw_ref)

        _when_first(shared, init)
        pre_fn = lambda c0, c1: pre_ref[:, c0:c1]
        xs_pre, b_pre, c_pre, zz, smv = _ssd_split(pre_fn, z_ref, sm_ref)
        ht = [hs_ref[0, :, 128 * j:128 * j + 128] for j in range(8)]
        nwl = [nw_ref[:, 128 * j:128 * j + 128] for j in range(8)]
        consts = (tri_ref[...], i2_ref[...], mask2_ref[...], lo_ref[...])

        def f(xs_pre, b_pre, c_pre, zz, smv, ht, dtb, alog, dpar, nwl):
            return _ssd_chunk(xs_pre, b_pre, c_pre, zz, smv, ht, dtb, alog, dpar, nwl, *consts)

        _, vjp = jax.vjp(f, xs_pre, b_pre, c_pre, zz, smv, ht, dtb_ref[...], alog_ref[...], dpar_ref[...], nwl)
        dys = [dy_ref[:, 128 * j:128 * j + 128] for j in range(8)]
        dhts = [dht_scr[:, 128 * j:128 * j + 128] for j in range(8)]
        dxs, db, dc, dzz, dsm, dht, ddtb, dalog, ddpar, dnwl = vjp((dys, dhts))
        for j in range(8):
            dz_ref[:, 128 * j:128 * j + 128] = dzz[j].astype(dz_ref.dtype)
            dht_scr[:, 128 * j:128 * j + 128] = dht[j]
            dnw_ref[0:1, 128 * j:128 * j + 128] += dnwl[j]
        shared["dsm_ssd"] = dsm
        ddtb_ref[0:1, :] += ddtb
        dalog_ref[0:1, :] += dalog
        ddpar_ref[0:1, :] += ddpar
        ranges = ([(128 * j, 128 * j + 128) for j in range(8)] + [(1024 + 128 * g, 1152 + 128 * g) for g in range(2)]
                  + [(1280 + 128 * g, 1408 + 128 * g) for g in range(2)])
        _conv_bwd(dxs + db + dc, ranges, dbuf, carry, xbc_ref, cw_ref, dxbc_ref, dcw_ref, dcb_ref, shared["first"])

    ns = nc // SUB_BWD
    rblk = lambda w: pl.BlockSpec((SUB_BWD * CHUNK, w), lambda i: (ns - 1 - i, 0))
    acc = lambda w: pl.BlockSpec((8, w), lambda i: (0, 0))
    f32 = jnp.float32
    return dict(
        body=body,
        in_kinds=["rows"] * 4 + ["state", "rows"] + ["full"] * 9, out_kinds=["rows", "rows"] + ["full"] * 6,
        in_specs=[rblk(1024), rblk(1536), rblk(1536), rblk(128),
                  pl.BlockSpec((SUB_BWD, 128, 1024), lambda i: (ns - 1 - i, 0, 0)), rblk(1024),
                  _full((CONV_K, 1536)), _full((1, 128)), _full((1, 128)), _full((1, 128)),
                  _full((1, 1024)), _full((64, 64)), _full((64, 128)), _full((64, 128)),
                  _full((1, 128))],
        out_specs=[rblk(1024), rblk(1536), acc(1536), acc(1536), acc(128), acc(128), acc(128), acc(1024)],
        out_shape=[jax.ShapeDtypeStruct((t, 1024), f32), jax.ShapeDtypeStruct((t, 1536), f32),
                   jax.ShapeDtypeStruct((8, 1536), f32),
                   jax.ShapeDtypeStruct((8, 1536), f32), jax.ShapeDtypeStruct((8, 128), f32),
                   jax.ShapeDtypeStruct((8, 128), f32), jax.ShapeDtypeStruct((8, 128), f32),
                   jax.ShapeDtypeStruct((8, 1024), f32)],
        scratch=[pltpu.VMEM((72, 1536), f32), pltpu.VMEM((8, 1536), f32), pltpu.VMEM((128, 1024), f32)],
        args=[z, xbc, pre, sm, hs, dy, conv_w, dtb, alog, dpar, nw, cs["tri"], cs["i2"], cs["mask2"], cs["lo"]])


def _gdn_split(pre_fn, gate_ref):
    def heads(base):
        return jnp.stack([pre_fn(base + 128 * h, base + 128 * h + 128) for h in range(GDN_HEADS)])
    gate = jnp.stack([gate_ref[:, 128 * h:128 * h + 128] for h in range(GDN_HEADS)])
    return heads(0), heads(1024), heads(2048), gate


def gdn_fwd(gate, qkv, sm, conv_w, dtb, alog, nw, cs):
    t = gate.shape[0]
    nc = t // CHUNK

    def body(shared, gate_ref, qkv_ref, halo_ref, sm_ref, cw_ref, dtb_ref, alog_ref, nw_ref,
             tri_ref, i64_ref, strict_ref, o_ref, ss_ref, ts_ref, pre_ref, pbuf, s_scr):
        def init():
            s_scr[...] = jnp.zeros_like(s_scr)

        _when_first(shared, init)
        pbuf[0:8, :] = jnp.where(shared["first"], 0.0, halo_ref[...])
        pbuf[8:72, :] = qkv_ref[...]

        def pre_fn(c0, c1):
            pre = _conv_fwd(pbuf, cw_ref, c0, c1)
            pre_ref[:, c0:c1] = pre
            return pre

        q_pre, k_pre, v_pre, g3 = _gdn_split(pre_fn, gate_ref)
        s = s_scr[...]
        ss_ref[0] = s
        out, s_next, tinv = _gdn_chunk(q_pre, k_pre, v_pre, g3, sm_ref[...], s, dtb_ref[...], alog_ref[...],
                                       nw_ref[...], tri_ref[...], i64_ref[...], strict_ref[...])
        ts_ref[0] = tinv
        s_scr[...] = s_next
        for h in range(GDN_HEADS):
            o_ref[:, 128 * h:128 * h + 128] = out[h].astype(o_ref.dtype)

    blk = lambda w: pl.BlockSpec((SUB_FWD * CHUNK, w), lambda i: (i, 0))
    return dict(
        body=body,
        in_kinds=["rows", "rows", ("halo", 1), "rows"] + ["full"] * 7, out_kinds=["rows", "state", "state", "rows"],
        in_specs=[blk(1024), blk(3072), _halo_spec(3072, lambda i: i), blk(128),
                  _full((CONV_K, 3072)), _full((1, 128)), _full((1, 128)), _full((1, 128)),
                  _full((64, 64)), _full((64, 64)), _full((64, 64))],
        out_specs=[blk(1024), pl.BlockSpec((SUB_FWD, 8, 128, 128), lambda i: (i, 0, 0, 0)),
                   pl.BlockSpec((SUB_FWD, 8, CHUNK, CHUNK), lambda i: (i, 0, 0, 0)), blk(3072)],
        out_shape=[jax.ShapeDtypeStruct((t, 1024), _MM), jax.ShapeDtypeStruct((nc, 8, 128, 128), jnp.float32),
                   jax.ShapeDtypeStruct((nc, 8, CHUNK, CHUNK), jnp.float32),
                   jax.ShapeDtypeStruct((t, 3072), jnp.float32)],
        scratch=[pltpu.VMEM((72, 3072), jnp.float32), pltpu.VMEM((8, 128, 128), jnp.float32)],
        args=[gate, qkv, qkv, sm, conv_w, dtb, alog, nw, cs["tri"], cs["i64"], cs["strict"]])


def gdn_bwd(gate, qkv, pre, sm, ss, ts, do, conv_w, dtb, alog, nw, cs):
    t = gate.shape[0]
    nc = t // CHUNK

    def body(shared, gate_ref, qkv_ref, pre_ref, sm_ref, ss_ref, ts_ref, do_ref, cw_ref, dtb_ref, alog_ref,
             nw_ref, tri_ref, i64_ref, strict_ref,
             dgate_ref, dqkv_ref, dsm_ref, dcw_ref, ddtb_ref, dalog_ref, dnw_ref,
             dbuf, carry, ds_scr):
        def init():
            ds_scr[...] = jnp.zeros_like(ds_scr)
            dcw_ref[...] = jnp.zeros_like(dcw_ref)
            ddtb_ref[...] = jnp.zeros_like(ddtb_ref)
            dalog_ref[...] = jnp.zeros_like(dalog_ref)
            dnw_ref[...] = jnp.zeros_like(dnw_ref)

        _when_first(shared, init)

        q_pre, k_pre, v_pre, g3 = _gdn_split(lambda c0, c1: pre_ref[:, c0:c1], gate_ref)
        consts = (tri_ref[...], i64_ref[...], strict_ref[...], ts_ref[0])

        def f(q_pre, k_pre, v_pre, g3, smv, s, dtb, alog, nwv):
            return _gdn_chunk(q_pre, k_pre, v_pre, g3, smv, s, dtb, alog, nwv, *consts)[:2]

        _, vjp = jax.vjp(f, q_pre, k_pre, v_pre, g3, sm_ref[...], ss_ref[0], dtb_ref[...], alog_ref[...], nw_ref[...])
        do3 = jnp.stack([do_ref[:, 128 * h:128 * h + 128] for h in range(GDN_HEADS)])
        dq, dk, dv, dg3, dsm, ds, ddtb, dalog, dnw = vjp((do3, ds_scr[...]))
        ds_scr[...] = ds
        for h in range(GDN_HEADS):
            dgate_ref[:, 128 * h:128 * h + 128] = dg3[h].astype(dgate_ref.dtype)
        dsm_ref[...] = (dsm + shared["dsm_ssd"]).astype(dsm_ref.dtype)
        ddtb_ref[0:1, :] += ddtb
        dalog_ref[0:1, :] += dalog
        dnw_ref[0:1, :] += dnw
        ranges = [(base + 128 * h, base + 128 * h + 128) for base in (0, 1024, 2048) for h in range(GDN_HEADS)]
        dlist = [d[h] for d in (dq, dk, dv) for h in range(GDN_HEADS)]
        _conv_bwd(dlist, ranges, dbuf, carry, qkv_ref, cw_ref, dqkv_ref, dcw_ref, None, shared["first"])

    ns = nc // SUB_BWD
    rblk = lambda w: pl.BlockSpec((SUB_BWD * CHUNK, w), lambda i: (ns - 1 - i, 0))
    acc = lambda w: pl.BlockSpec((8, w), lambda i: (0, 0))
    f32 = jnp.float32
    return dict(
        body=body,
        in_kinds=["rows"] * 4 + ["state", "state", "rows"] + ["full"] * 7, out_kinds=["rows"] * 3 + ["full"] * 4,
        in_specs=[rblk(1024), rblk(3072), rblk(3072), rblk(128),
                  pl.BlockSpec((SUB_BWD, 8, 128, 128), lambda i: (ns - 1 - i, 0, 0, 0)),
                  pl.BlockSpec((SUB_BWD, 8, CHUNK, CHUNK), lambda i: (ns - 1 - i, 0, 0, 0)), rblk(1024),
                  _full((CONV_K, 3072)), _full((1, 128)), _full((1, 128)), _full((1, 128)),
                  _full((64, 64)), _full((64, 64)), _full((64, 64))],
        out_specs=[rblk(1024), rblk(3072), rblk(128), acc(3072), acc(128), acc(128), acc(128)],
        out_shape=[jax.ShapeDtypeStruct((t, 1024), f32), jax.ShapeDtypeStruct((t, 3072), f32),
                   jax.ShapeDtypeStruct((t, 128), f32), jax.ShapeDtypeStruct((8, 3072), f32),
                   jax.ShapeDtypeStruct((8, 128), f32), jax.ShapeDtypeStruct((8, 128), f32),
                   jax.ShapeDtypeStruct((8, 128), f32)],
        scratch=[pltpu.VMEM((72, 3072), f32), pltpu.VMEM((8, 3072), f32), pltpu.VMEM((8, 128, 128), f32)],
        args=[gate, qkv, pre, sm, ss, ts, do, conv_w, dtb, alog, nw, cs["tri"], cs["i64"], cs["strict"]])


def _chunk_call(parts, name, nc, reverse):
    n_in = [len(p["args"]) for p in parts]
    n_out = [len(p["out_shape"]) for p in parts]
    n_scr = [len(p["scratch"]) for p in parts]
    sub = SUB_BWD if reverse else SUB_FWD
    order = list(range(sub))[::-1] if reverse else list(range(sub))

    def view(ref, kind, s, refs):
        if kind == "rows":
            return ref.at[pl.ds(CHUNK * s, CHUNK)]
        if kind == "state":
            return ref.at[pl.ds(s, 1)]
        if kind == "full":
            return ref
        src = refs[kind[1]]
        return ref if s == 0 else src.at[pl.ds(CHUNK * s - 8, 8)]

    def body(*refs):
        ins, outs, scr = refs[:sum(n_in)], refs[sum(n_in):sum(n_in) + sum(n_out)], refs[sum(n_in) + sum(n_out):]
        for s in order:
            shared = {"first": (pl.program_id(0) == 0) if s == order[0] else False}
            for k, p in enumerate(parts):
                i0, o0, s0 = sum(n_in[:k]), sum(n_out[:k]), sum(n_scr[:k])
                p_ins = ins[i0:i0 + n_in[k]]
                p["body"](shared,
                          *[view(r, kd, s, p_ins) for r, kd in zip(p_ins, p["in_kinds"])],
                          *[view(r, kd, s, None) for r, kd in zip(outs[o0:o0 + n_out[k]], p["out_kinds"])],
                          *scr[s0:s0 + n_scr[k]])

    cat = lambda key: [v for p in parts for v in p[key]]
    return _pc(body, name=name, grid=(nc // sub,), in_specs=cat("in_specs"), out_specs=cat("out_specs"),
               out_shape=cat("out_shape"), scratch_shapes=cat("scratch"),
               compiler_params=_cparams(("arbitrary",)))(*cat("args"))


def out_fwd_bwd(x, tgt, y_ssd, y_gdn, w_out, fnw):
    t = x.shape[0]
    tm = min(512, t)
    f32 = jnp.float32

    def body(x_ref, tgt_ref, ys_ref, yg_ref, w_ref, fnw_ref,
             dout_ref, dys_ref, dyg_ref, gw_ref, gfnw_ref, loss_ref, gw_acc):
        i = pl.program_id(0)

        @pl.when(i == 0)
        def _():
            gw_acc[...] = jnp.zeros_like(gw_acc)
            gfnw_ref[...] = jnp.zeros_like(gfnw_ref)
            loss_ref[...] = jnp.zeros_like(loss_ref)

        ys = ys_ref[...]
        yg = yg_ref[...]
        out = x_ref[...] + jnp.dot(ys, w_ref[0:1024, :], preferred_element_type=f32) \
            + jnp.dot(yg, w_ref[1024:2048, :], preferred_element_type=f32)
        rstd = lax.rsqrt(jnp.mean(out * out, axis=-1, keepdims=True) + EPS)
        yhat = out * rstd
        fw = fnw_ref[...]
        e = yhat * fw - tgt_ref[...]
        loss_ref[...] += 0.5 * jnp.sum(jnp.sum(e * e, axis=-1, keepdims=True) * (1.0 / D_MODEL), axis=0, keepdims=True)
        dyf = e * (1.0 / D_MODEL)
        gfnw_ref[0:1, :] += jnp.sum(dyf * yhat, axis=0, keepdims=True)
        dyhat = dyf * fw
        dout = rstd * (dyhat - yhat * jnp.mean(dyhat * yhat, axis=-1, keepdims=True))
        dout_ref[...] = dout
        db = dout.astype(_MM)
        dys_ref[...] = lax.dot_general(db, w_ref[0:1024, :], (((1,), (1,)), ((), ())), preferred_element_type=f32)
        dyg_ref[...] = lax.dot_general(db, w_ref[1024:2048, :], (((1,), (1,)), ((), ())), preferred_element_type=f32)
        gw_acc[0:1024, :] += lax.dot_general(ys, db, (((0,), (0,)), ((), ())), preferred_element_type=f32)
        gw_acc[1024:2048, :] += lax.dot_general(yg, db, (((0,), (0,)), ((), ())), preferred_element_type=f32)

        @pl.when(i == steps - 1)
        def _():
            gw_ref[...] = gw_acc[...].astype(gw_ref.dtype)

    steps = t // tm
    blk = pl.BlockSpec((tm, D_MODEL), lambda i: (i, 0))
    return _pc(
        body, name="out_fwd_bwd", grid=(steps,),
        in_specs=[blk, blk, blk, blk, _full((MIX_WIDTH, D_MODEL)), _full((1, D_MODEL))],
        out_specs=[blk, blk, blk, _full((MIX_WIDTH, D_MODEL)), _full((8, D_MODEL)), _full((1, 128))],
        out_shape=[jax.ShapeDtypeStruct((t, D_MODEL), f32)] * 3 +
                  [jax.ShapeDtypeStruct((MIX_WIDTH, D_MODEL), _MM), jax.ShapeDtypeStruct((8, D_MODEL), f32),
                   jax.ShapeDtypeStruct((1, 128), f32)],
        scratch_shapes=[pltpu.VMEM((MIX_WIDTH, D_MODEL), f32)],
        compiler_params=_cparams(("arbitrary",)),
    )(x, tgt, y_ssd, y_gdn, w_out, fnw)


def inproj_bwd_dx(x, dout, norm_w, w_perm, dgroups, scattered):
    t = x.shape[0]
    tm = min(256, t)
    f32 = jnp.float32

    def body(x_ref, dout_ref, nw_ref, w_ref, dz_ref, dxbc_ref, dgate_ref, dqkv_ref, dsm_ref, dx_ref, gnw_ref):
        i = pl.program_id(0)

        @pl.when(i == 0)
        def _():
            gnw_ref[...] = jnp.zeros_like(gnw_ref)

        du = None
        for (name, c0, c1), d_ref in zip(GROUPS, (dz_ref, dxbc_ref, dgate_ref, dqkv_ref, dsm_ref)):
            term = jnp.dot(d_ref[...].astype(_MM), _w_rows(w_ref, name, c1 - c0), preferred_element_type=f32)
            du = term if du is None else du + term
        xf = x_ref[...]
        rstd = lax.rsqrt(jnp.mean(xf * xf, axis=-1, keepdims=True) + EPS)
        xhat = xf * rstd
        gnw_ref[0:1, :] += jnp.sum(du * xhat, axis=0, keepdims=True)
        dxh = du * nw_ref[...]
        dx_ref[...] = dout_ref[...] + rstd * (dxh - xhat * jnp.mean(dxh * xhat, axis=-1, keepdims=True))

    blk = lambda w: pl.BlockSpec((tm, w), lambda i: (i, 0))
    steps = t // tm
    kinds = ["scatter"] * len(scattered)
    hosted = _hosting(body, 9, 2, 0, kinds, lambda: pl.program_id(0) == 0, lambda: pl.program_id(0) == steps - 1)
    return _pc_comm(
        hosted, name="inproj_bwd_dx", grid=(steps,),
        in_specs=[blk(D_MODEL), blk(D_MODEL), _full((1, D_MODEL)), _full((IN_DIM, D_MODEL))] +
                 [blk(c1 - c0) for _, c0, c1 in GROUPS] + [ANY] * len(scattered),
        out_specs=[blk(D_MODEL), _full((8, D_MODEL))] + [ANY] * len(scattered),
        out_shape=[jax.ShapeDtypeStruct((t, D_MODEL), f32), jax.ShapeDtypeStruct((8, D_MODEL), f32)] +
                  [_exchange_out_shape("scatter", a) for a in scattered],
        scratch_shapes=_exchange_sems(len(scattered)), compiler_params=_cparams(("arbitrary",)),
    )(x, dout, norm_w, w_perm, *dgroups, *scattered)


def grad_w_group(u, dg, name, scattered=()):
    t, n = dg.shape
    tn = n if n <= 1536 else 1024
    budget = 40 * 1024 * 1024
    tm = next((c for c in (4096, 2048, 1024, 512, 256)
               if t % c == 0 and tn * D_MODEL * 4 + 2 * (c * tn * 4 + c * D_MODEL * 2 + tn * D_MODEL * 2) <= budget), t)
    nj, nk = n // tn, t // tm
    f32 = jnp.float32

    def body(u_ref, d_ref, o_ref, acc):
        k = pl.program_id(1)

        @pl.when(k == 0)
        def _():
            acc[...] = jnp.zeros_like(acc)

        acc[...] += lax.dot_general(d_ref[...].astype(_MM), u_ref[...], (((0,), (0,)), ((), ())),
                                    preferred_element_type=f32)

        @pl.when(k == nk - 1)
        def _():
            o_ref[...] = acc[...].astype(o_ref.dtype)

    ne = len(scattered)
    hosted = _hosting(body, 2, 1, 1, ["scatter"] * ne,
                      lambda: (pl.program_id(0) == 0) & (pl.program_id(1) == 0),
                      lambda: (pl.program_id(0) == nj - 1) & (pl.program_id(1) == nk - 1))
    res = (_pc_comm if ne else _pc)(
        hosted, name=name, grid=(nj, nk),
        in_specs=[pl.BlockSpec((tm, D_MODEL), lambda j, k: (k, 0)),
                  pl.BlockSpec((tm, tn), lambda j, k: (k, j))] + [ANY] * ne,
        out_specs=[pl.BlockSpec((tn, D_MODEL), lambda j, k: (j, 0))] + [ANY] * ne,
        out_shape=[jax.ShapeDtypeStruct((n, D_MODEL), _MM)] + [_exchange_out_shape("scatter", a) for a in scattered],
        scratch_shapes=[pltpu.VMEM((tn, D_MODEL), f32)] + _exchange_sems(ne),
        compiler_params=_cparams(("arbitrary", "arbitrary")),
    )(u, dg, *scattered)
    return res if ne else res[0]


def grad_w_many(u, dgs, name):
    t = u.shape[0]
    widths = [d.shape[1] for d in dgs]
    tot, ng = sum(widths), len(dgs)
    f32 = jnp.float32
    budget = 48 * 1024 * 1024
    tm = next((c for c in (2048, 1024, 512, 256)
               if t % c == 0 and tot * D_MODEL * 4 + 2 * (c * tot * 4 + c * D_MODEL * 2 + tot * D_MODEL * 2) <= budget), t)
    nk = t // tm

    def body(*refs):
        u_ref, d_refs, o_refs, accs = refs[0], refs[1:1 + ng], refs[1 + ng:1 + 2 * ng], refs[1 + 2 * ng:]
        k = pl.program_id(0)

        @pl.when(k == 0)
        def _():
            for acc in accs:
                acc[...] = jnp.zeros_like(acc)

        uu = u_ref[...]
        for d_ref, acc in zip(d_refs, accs):
            acc[...] += lax.dot_general(d_ref[...].astype(_MM), uu, (((0,), (0,)), ((), ())),
                                        preferred_element_type=f32)

        @pl.when(k == nk - 1)
        def _():
            for o_ref, acc in zip(o_refs, accs):
                o_ref[...] = acc[...].astype(o_ref.dtype)

    return _pc(
        body, name=name, grid=(nk,),
        in_specs=[pl.BlockSpec((tm, D_MODEL), lambda k: (k, 0))] + [pl.BlockSpec((tm, n), lambda k: (k, 0)) for n in widths],
        out_specs=[pl.BlockSpec((n, D_MODEL), lambda k: (0, 0)) for n in widths],
        out_shape=[jax.ShapeDtypeStruct((n, D_MODEL), _MM) for n in widths],
        scratch_shapes=[pltpu.VMEM((n, D_MODEL), f32) for n in widths],
        compiler_params=_cparams(("arbitrary",)),
    )(u, *dgs)


def _pad_lanes(v, off):
    n = v.shape[-1]
    return jnp.pad(v.reshape(1, n).astype(jnp.float32), ((0, 0), (off, 128 - off - n)))


REF_ROWS = dict(z=(0, 1024), xbc=(1024, 2560), dt=(2560, 2576), gate=(2576, 3600), qkv=(3600, 6672), ab=(6672, 6688))


def unperm_w_in(gz, gxbc, ggate, gqkv, gsm):
    src = dict(z=gz, xbc=gxbc, dt=gsm[0:16], gate=ggate, qkv=gqkv, ab=gsm[16:32])
    slabs = []
    for k in range(N_DEV):
        a, b = k * W_IN_SHARD, (k + 1) * W_IN_SHARD
        parts = []
        for name, (s, e) in REF_ROWS.items():
            lo, hi = max(a, s), min(b, e)
            if lo < hi:
                parts.append(src[name][lo - s:hi - s])
        slabs.append(jnp.concatenate(parts, axis=0))
    return jnp.stack(slabs)


def all_gather(arrs, name):
    n = len(arrs)

    def body(*refs):
        ins, outs = refs[:n], refs[n:2 * n]
        send_sems, recv_sems, local_sems = refs[2 * n:]
        x, y, c, me = _me()
        sibling = (x, y, 1 - c)
        chips = [(1 - x, y), (x, 1 - y), (1 - x, 1 - y)]

        def idx(px, py, pc):
            return 4 * px + 2 * py + pc

        def copy(a, k, block, to, src=None):
            slot = outs[a].at[idx(*block)]
            return pltpu.make_async_remote_copy(src_ref=slot if src is None else src, dst_ref=slot,
                                                send_sem=send_sems.at[a, k], recv_sem=recv_sems.at[a, k],
                                                device_id=to, device_id_type=MESH)

        local = [pltpu.make_async_copy(ins[a], outs[a].at[me], local_sems.at[a]) for a in range(n)]
        for cp in local:
            cp.start()
        started = []
        for a in range(n):
            first = [copy(a, 0, (x, y, c), sibling, src=ins[a])]
            first += [copy(a, 1 + j, (x, y, c), (*chip, c), src=ins[a]) for j, chip in enumerate(chips)]
            for cp in first:
                cp.start()
            started += first
        for a in range(n):
            for j, chip in enumerate(chips):
                copy(a, 1 + j, (*chip, c), (x, y, c)).wait_recv()
                fwd = copy(a, 4 + j, (*chip, c), sibling)
                fwd.start()
                started.append(fwd)
        for a in range(n):
            copy(a, 0, sibling, (x, y, c)).wait_recv()
            for j, chip in enumerate(chips):
                copy(a, 4 + j, (*chip, 1 - c), (x, y, c)).wait_recv()
        for cp in started:
            cp.wait_send()
        for cp in local:
            cp.wait()

    return _pc_comm(
        body, name=name, in_specs=[ANY] * n, out_specs=[ANY] * n,
        out_shape=[jax.ShapeDtypeStruct((N_DEV,) + a.shape, a.dtype) for a in arrs],
        scratch_shapes=[pltpu.SemaphoreType.DMA((n, 7)), pltpu.SemaphoreType.DMA((n, 7)),
                        pltpu.SemaphoreType.DMA((n,))],
    )(*arrs)


def adamw_sum(recv, w, m, v, rows, name, cols=None):
    r, ccols = w.shape
    f32 = jnp.float32
    c1 = 1.0 / (1.0 - ADAM_B1 ** ADAM_STEP)
    c2 = 1.0 / (1.0 - ADAM_B2 ** ADAM_STEP)

    def body(recv_ref, w_ref, m_ref, v_ref, g_ref, d_ref, mo_ref, vo_ref):
        g = recv_ref[0].astype(f32)
        for k in range(1, N_DEV):
            g = g + recv_ref[k].astype(f32)
        mn = ADAM_B1 * m_ref[...] + (1.0 - ADAM_B1) * g
        vn = ADAM_B2 * v_ref[...] + (1.0 - ADAM_B2) * (g * g)
        g_ref[...] = g
        mo_ref[...] = mn
        vo_ref[...] = vn
        d_ref[...] = -ADAM_LR * ((mn * c1) / (jnp.sqrt(vn * c2) + ADAM_EPS) + ADAM_WD * w_ref[...])

    if cols is None:
        blk = pl.BlockSpec((rows, ccols), lambda i: (i, 0))
        rblk, steps = pl.BlockSpec((N_DEV, rows, ccols), lambda i: (0, i, 0)), r // rows
    else:
        blk = pl.BlockSpec((r, cols), lambda i: (0, i))
        rblk, steps = pl.BlockSpec((N_DEV, r, cols), lambda i: (0, 0, i)), ccols // cols
    return _pc(
        body, name=name, grid=(steps,),
        in_specs=[rblk, blk, blk, blk],
        out_specs=[blk] * 4, out_shape=[jax.ShapeDtypeStruct((r, ccols), f32)] * 4,
        compiler_params=_cparams(("arbitrary",)),
    )(recv, w, m, v)


SMALL = (("norm_w", 1, 1024, 0), ("ssd_conv_b", 1, 1536, 0), ("ssd_dt_bias", 1, 16, 0), ("ssd_a_log", 1, 16, 0),
         ("ssd_d", 1, 16, 0), ("ssd_norm_w", 1, 1024, 0), ("gdn_dt_bias", 1, 8, 16), ("gdn_a_log", 1, 8, 16),
         ("gdn_norm_w", 1, 128, 0), ("final_norm_w", 1, 1024, 0),
         ("ssd_conv_w", CONV_K, SSD_CONV_DIM // N_DEV, 0), ("gdn_conv_w", CONV_K, GDN_CONV_DIM // N_DEV, 0))


def _small_layout():
    out, off = [], 0
    for name, rows, n, lane0 in SMALL + (("loss", 1, 128, 0),):
        stride = -(-(lane0 + n) // 128) * 128
        out.append((name, rows, n, lane0, stride, off))
        off += rows * stride
    return out, off


def scatter_small(accs):
    layout, total = _small_layout()
    f32 = jnp.float32

    def body(*refs):
        acc_refs, out_ref, slabs = refs[:len(layout)], refs[len(layout)], refs[len(layout) + 1]
        sems = refs[len(layout) + 2:]
        slabs[...] = jnp.zeros_like(slabs)
        for (name, rows, n, lane0, stride, off), acc in zip(layout, acc_refs):
            for k in range(N_DEV):
                if rows == 1:
                    slabs[k, :, off:off + stride] = acc[0:1, 0:stride]
                else:
                    for j in range(rows):
                        slabs[k, :, off + stride * j:off + stride * j + n] = acc[j:j + 1, n * k:n * k + n]
        start, wait = _exchange_ops("scatter", slabs, out_ref, *sems)
        start()
        wait()

    return _pc_comm(
        body, name="scatter_small_grads", out_specs=ANY, out_shape=jax.ShapeDtypeStruct((N_DEV, 1, total), f32),
        scratch_shapes=[pltpu.VMEM((N_DEV, 1, total), f32)] + _exchange_sems(1),
    )(*accs)


def adamw_small(recv, w, m, v):
    layout, total = _small_layout()
    loss_off = layout[-1][5]
    layout = layout[:-1]
    f32 = jnp.float32
    c1 = 1.0 / (1.0 - ADAM_B1 ** ADAM_STEP)
    c2 = 1.0 / (1.0 - ADAM_B2 ** ADAM_STEP)
    np_ = len(layout)

    def body(*refs):
        recv_ref = refs[0]
        w_refs, m_refs, v_refs = refs[1:1 + np_], refs[1 + np_:1 + 2 * np_], refs[1 + 2 * np_:1 + 3 * np_]
        o_refs = refs[1 + 3 * np_:]
        g_all = recv_ref[0]
        for k in range(1, N_DEV):
            g_all = g_all + recv_ref[k]
        o_refs[4 * np_][...] = g_all[:, loss_off:loss_off + 128]

        def update(g, wv, mv, vv):
            mn = ADAM_B1 * mv + (1.0 - ADAM_B1) * g
            vn = ADAM_B2 * vv + (1.0 - ADAM_B2) * (g * g)
            return g, -ADAM_LR * ((mn * c1) / (jnp.sqrt(vn * c2) + ADAM_EPS) + ADAM_WD * wv), mn, vn

        for p, (name, rows, n, lane0, stride, off) in enumerate(layout):
            outs = o_refs[4 * p:4 * p + 4]
            if rows == 1:
                res = update(g_all[:, off + lane0:off + lane0 + n], w_refs[p][...], m_refs[p][...], v_refs[p][...])
                for o, r in zip(outs, res):
                    o[...] = r
            else:
                for j in range(rows):
                    res = update(g_all[:, off + stride * j:off + stride * j + n], w_refs[p][0, j:j + 1, :],
                                 m_refs[p][0, j:j + 1, :], v_refs[p][0, j:j + 1, :])
                    for o, r in zip(outs, res):
                        o[0, j:j + 1, :] = r

    names = [e[0] for e in layout]
    ins = [recv] + [d[nm] for d in (w, m, v) for nm in names]
    out_shape = [jax.ShapeDtypeStruct(w[nm].shape, f32) for nm in names for _ in range(4)]
    out_shape.append(jax.ShapeDtypeStruct((1, 128), f32))
    res = _pc(body, name="adamw_small", out_shape=out_shape)(*ins)
    return {nm: tuple(res[4 * p:4 * p + 4]) for p, nm in enumerate(names)}, res[4 * np_]


SHARD = (("ssd_conv_w", CONV_K * SSD_CONV_DIM // N_DEV), ("gdn_conv_w", CONV_K * GDN_CONV_DIM // N_DEV))
SHARD_ROWS = 24


def _rows_of(size):
    return -(-size // 128)


def _pack(vals, layout, total_rows):
    parts = []
    for (name, size), val in zip(layout, vals):
        flat = val.reshape(-1).astype(jnp.float32)
        parts.append(jnp.pad(flat, (0, _rows_of(size) * 128 - size)).reshape(-1, 128))
    used = sum(_rows_of(s) for _, s in layout)
    parts.append(jnp.zeros((total_rows - used, 128), jnp.float32))
    return jnp.concatenate(parts, axis=0)


def _conv_full(gathered_flat, ccols):
    return gathered_flat.reshape(N_DEV, CONV_K, ccols // N_DEV).transpose(1, 0, 2).reshape(CONV_K, ccols)


def kernel(x, norm_w, w_in, ssd_conv_w, ssd_conv_b, ssd_dt_bias, ssd_a_log, ssd_d, ssd_norm_w, gdn_conv_w, gdn_dt_bias, gdn_a_log, gdn_norm_w, w_out, final_norm_w, loss_target, m_norm_w, m_w_in, m_ssd_conv_w, m_ssd_conv_b, m_ssd_dt_bias, m_ssd_a_log, m_ssd_d, m_ssd_norm_w, m_gdn_conv_w, m_gdn_dt_bias, m_gdn_a_log, m_gdn_norm_w, m_w_out, m_final_norm_w, v_norm_w, v_w_in, v_ssd_conv_w, v_ssd_conv_b, v_ssd_dt_bias, v_ssd_a_log, v_ssd_d, v_ssd_norm_w, v_gdn_conv_w, v_gdn_dt_bias, v_gdn_a_log, v_gdn_norm_w, v_w_out, v_final_norm_w):
    f32 = jnp.float32
    w = dict(norm_w=norm_w, w_in=w_in, ssd_conv_w=ssd_conv_w, ssd_conv_b=ssd_conv_b, ssd_dt_bias=ssd_dt_bias,
             ssd_a_log=ssd_a_log, ssd_d=ssd_d, ssd_norm_w=ssd_norm_w, gdn_conv_w=gdn_conv_w, gdn_dt_bias=gdn_dt_bias,
             gdn_a_log=gdn_a_log, gdn_norm_w=gdn_norm_w, w_out=w_out, final_norm_w=final_norm_w)
    m = dict(norm_w=m_norm_w, w_in=m_w_in, ssd_conv_w=m_ssd_conv_w, ssd_conv_b=m_ssd_conv_b, ssd_dt_bias=m_ssd_dt_bias,
             ssd_a_log=m_ssd_a_log, ssd_d=m_ssd_d, ssd_norm_w=m_ssd_norm_w, gdn_conv_w=m_gdn_conv_w,
             gdn_dt_bias=m_gdn_dt_bias, gdn_a_log=m_gdn_a_log, gdn_norm_w=m_gdn_norm_w, w_out=m_w_out,
             final_norm_w=m_final_norm_w)
    v = dict(norm_w=v_norm_w, w_in=v_w_in, ssd_conv_w=v_ssd_conv_w, ssd_conv_b=v_ssd_conv_b, ssd_dt_bias=v_ssd_dt_bias,
             ssd_a_log=v_ssd_a_log, ssd_d=v_ssd_d, ssd_norm_w=v_ssd_norm_w, gdn_conv_w=v_gdn_conv_w,
             gdn_dt_bias=v_gdn_dt_bias, gdn_a_log=v_gdn_a_log, gdn_norm_w=v_gdn_norm_w, w_out=v_w_out,
             final_norm_w=v_final_norm_w)
    names = list(w)
    shapes = {n: w[n].shape for n in names}

    xl, tgt = x[0], loss_target[0]
    cs = _consts()
    dtb_s = _pad_lanes(ssd_dt_bias, 0)
    alog_s = _pad_lanes(ssd_a_log, 0)
    dpar = _pad_lanes(ssd_d, 0)
    dtb_g = _pad_lanes(gdn_dt_bias, 16)
    alog_g = _pad_lanes(gdn_a_log, 16)
    nw_g = gdn_norm_w.reshape(1, 128)
    nw_s = ssd_norm_w.reshape(1, 1024)
    cb_s = ssd_conv_b.reshape(1, 1536)
    nw1 = norm_w.reshape(1, D_MODEL)

    (g_w_in,) = all_gather([w_in[0].T.astype(_MM)], "gather_w_in")
    w_perm = g_w_in.reshape(IN_DIM, D_MODEL)
    conv_pack = _pack([w["ssd_conv_w"], w["gdn_conv_w"]], SHARD, SHARD_ROWS)
    u, z, xbc, gate, qkv, sm, g_w_out, g_conv = inproj_fwd(xl, nw1, w_perm, [w_out[0].astype(_MM), conv_pack])
    w_out_full = g_w_out.reshape(MIX_WIDTH, D_MODEL)
    ssd_cw = _conv_full(g_conv[:, 0:6].reshape(N_DEV, -1), SSD_CONV_DIM)
    gdn_cw = _conv_full(g_conv[:, 6:18].reshape(N_DEV, -1), GDN_CONV_DIM)

    nc = xl.shape[0] // CHUNK
    y_ssd, hs, pre_s, y_gdn, ss, ts, pre_g = _chunk_call(
        [ssd_fwd(z, xbc, sm, ssd_cw, cb_s, dtb_s, alog_s, dpar, nw_s, cs),
         gdn_fwd(gate, qkv, sm, gdn_cw, dtb_g, alog_g, nw_g, cs)], "scan_fwd", nc, False)
    dout, dys, dyg, g_wout, g_fnw, loss_l = out_fwd_bwd(xl, tgt, y_ssd, y_gdn, w_out_full,
                                                        final_norm_w.reshape(1, D_MODEL))
    (dz, dxbc, g_cw_s, g_cb_s, g_dtb_s, g_alog_s, g_d, g_nw_s,
     dgate, dqkv, dsm, g_cw_g, g_dtb_g, g_alog_g, g_nw_g) = _chunk_call(
        [ssd_bwd(z, xbc, pre_s, sm, hs, dys, ssd_cw, dtb_s, alog_s, dpar, nw_s, cs),
         gdn_bwd(gate, qkv, pre_g, sm, ss, ts, dyg, gdn_cw, dtb_g, alog_g, nw_g, cs)], "scan_bwd", nc, True)

    t_w_out = g_wout.reshape(N_DEV, MIX_WIDTH // N_DEV, D_MODEL)
    gws = dict(zip(("z", "sm"), grad_w_many(u, [dz, dsm], "grad_w_in_z_sm")))
    gws["xbc"] = grad_w_group(u, dxbc, "grad_w_in_xbc")
    gws["gate"] = grad_w_group(u, dgate, "grad_w_in_gate")
    gws["qkv"], r_w_out = grad_w_group(u, dqkv, "grad_w_in_qkv", [t_w_out])
    t_w_in = unperm_w_in(gws["z"], gws["xbc"], gws["gate"], gws["qkv"], gws["sm"])
    dx, g_nw, r_w_in = inproj_bwd_dx(xl, dout, nw1, w_perm, (dz, dxbc, dgate, dqkv, dsm), [t_w_in])

    accs = dict(norm_w=g_nw, ssd_conv_b=g_cb_s, ssd_dt_bias=g_dtb_s, ssd_a_log=g_alog_s, ssd_d=g_d,
                ssd_norm_w=g_nw_s, gdn_dt_bias=g_dtb_g, gdn_a_log=g_alog_g, gdn_norm_w=g_nw_g, final_norm_w=g_fnw,
                ssd_conv_w=g_cw_s, gdn_conv_w=g_cw_g)
    r_small = scatter_small([accs[e[0]] for e in SMALL] + [loss_l])

    o_w_in = adamw_sum(r_w_in, w_in[0].T, m_w_in[0].T, v_w_in[0].T, None, "adamw_w_in", cols=256)
    o_w_out = adamw_sum(r_w_out, w_out[0], m_w_out[0], v_w_out[0], 64, "adamw_w_out")
    row = lambda d: {n: (a.reshape(1, -1) if a.ndim == 1 else a) for n, a in d.items()}
    o_small, loss_sum = adamw_small(r_small, row(w), row(m), row(v))

    loss = loss_sum[0, 0]
    outs = [loss, dx[None]]
    for k in range(4):
        parts = {n: o_small[n][k] for n in o_small}
        parts["w_in"] = o_w_in[k].T
        parts["w_out"] = o_w_out[k]
        outs += [parts[n].reshape(shapes[n]) for n in names]
    return tuple(outs)
```

```python
import functools

import jax
import jax.numpy as jnp
import numpy as np
from jax import lax
from jax.experimental import pallas as pl
from jax.experimental.pallas import tpu as pltpu

_MM = jnp.bfloat16

D_MODEL = 1024
CHUNK = 64
CONV_K = 4
EPS = 1e-6
SSD_CONV_DIM = 1536
GDN_HEADS = 8
GDN_DK = 128
GDN_CONV_DIM = 3072
MIX_WIDTH = 2048
IN_DIM = 6688
N_DEV = 8
W_IN_SHARD = IN_DIM // N_DEV
HI = lax.Precision.HIGHEST
HIGH = lax.Precision.HIGH
VMEM_LIMIT = 56 * 1024 * 1024

ADAM_LR = 0.001
ADAM_B1 = 0.9
ADAM_B2 = 0.999
ADAM_EPS = 1e-08
ADAM_WD = 0.01
ADAM_STEP = 10


def _pc(body, **kw):
    return pl.pallas_call(body, **kw)


def _pc_comm(body, **kw):
    return pl.pallas_call(body, **kw)


def _cparams(sem):
    return pltpu.CompilerParams(dimension_semantics=sem, vmem_limit_bytes=VMEM_LIMIT)


def _sig(x):
    return 0.5 * jnp.tanh(0.5 * x) + 0.5


@jax.custom_vjp
def _sigmoid(x):
    return _sig(x)


def _sigmoid_fwd(x):
    s = _sig(x)
    return s, s


def _sigmoid_bwd(s, g):
    return (g * s * (1.0 - s),)


_sigmoid.defvjp(_sigmoid_fwd, _sigmoid_bwd)


@jax.custom_vjp
def _silu(x):
    return x * _sig(x)


def _silu_fwd(x):
    s = _sig(x)
    return x * s, (x, s)


def _silu_bwd(res, g):
    x, s = res
    return (g * (s * (1.0 + x * (1.0 - s))),)


_silu.defvjp(_silu_fwd, _silu_bwd)


def _softplus_impl(x):
    return jnp.maximum(x, 0.0) + jnp.log(1.0 + jnp.exp(-jnp.abs(x)))


@jax.custom_vjp
def _softplus(x):
    return _softplus_impl(x)


def _softplus_fwd(x):
    return _softplus_impl(x), x


def _softplus_bwd(x, g):
    return (g * _sig(x),)


_softplus.defvjp(_softplus_fwd, _softplus_bwd)


def _lane_bcast_impl(x, k):
    return jnp.broadcast_to(x[..., k:k + 1], x.shape)


@functools.partial(jax.custom_vjp, nondiff_argnums=(1,))
def _lane_bcast(x, k):
    return _lane_bcast_impl(x, k)


def _lane_bcast_fwd(x, k):
    return _lane_bcast_impl(x, k), None


def _lane_bcast_bwd(k, _, g):
    lane = lax.broadcasted_iota(jnp.int32, g.shape, g.ndim - 1)
    return (jnp.where(lane == k, jnp.sum(g, axis=-1, keepdims=True), 0.0),)


_lane_bcast.defvjp(_lane_bcast_fwd, _lane_bcast_bwd)


def _mm(a, b):
    return jnp.dot(a.astype(_MM), b.astype(_MM), preferred_element_type=jnp.float32)


def _mm_nt(a, b):
    return lax.dot_general(a.astype(_MM), b.astype(_MM), (((1,), (1,)), ((), ())),
                           preferred_element_type=jnp.float32)


def _mm_tn(a, b):
    return lax.dot_general(a.astype(_MM), b.astype(_MM), (((0,), (0,)), ((), ())),
                           preferred_element_type=jnp.float32)


def _dot_hi(a, b):
    return jnp.dot(a, b, precision=HI, preferred_element_type=jnp.float32)


def _bmm(a, b):
    return lax.dot_general(a.astype(_MM), b.astype(_MM), (((2,), (1,)), ((0,), (0,))),
                           preferred_element_type=jnp.float32)


def _bmm_nt(a, b):
    return lax.dot_general(a.astype(_MM), b.astype(_MM), (((2,), (2,)), ((0,), (0,))),
                           preferred_element_type=jnp.float32)


def _bmm_tn(a, b):
    return lax.dot_general(a.astype(_MM), b.astype(_MM), (((1,), (1,)), ((0,), (0,))),
                           preferred_element_type=jnp.float32)


def _bmm_hi(a, b):
    return lax.dot_general(a, b, (((2,), (1,)), ((0,), (0,))), precision=HIGH, preferred_element_type=jnp.float32)


def _bmm_nt_hi(a, b):
    return lax.dot_general(a, b, (((2,), (2,)), ((0,), (0,))), precision=HIGH, preferred_element_type=jnp.float32)


def _bmm_tn_hi(a, b):
    return lax.dot_general(a, b, (((1,), (1,)), ((0,), (0,))), precision=HIGH, preferred_element_type=jnp.float32)


def _consts():
    l = np.arange(CHUNK)
    tri = (l[:, None] >= l[None, :]).astype(np.float32)
    lane = np.arange(128)
    i2 =(l[:, None] == (lane[None, :] % 64)).astype(np.float32)
    mask2 = (l[:, None] >= (lane[None, :] % 64)).astype(np.float32)
    lo = (lane < 64).astype(np.float32)[None, :]
    i64 = np.eye(CHUNK, dtype=np.float32)
    strict = (l[:, None] > l[None, :]).astype(np.float32)
    return dict(tri=jnp.asarray(tri), i2=jnp.asarray(i2), mask2=jnp.asarray(mask2), lo=jnp.asarray(lo),
                i64=jnp.asarray(i64), strict=jnp.asarray(strict))


def _ssd_chunk(xs_pre, b_pre, c_pre, z, sm, ht, dtb, alog, dpar, nw, tri, i2, mask2, lo):
    lane = lax.broadcasted_iota(jnp.int32, (1, 128), 1)
    m16 = lane < 16
    dt = jnp.where(m16, _softplus(sm + dtb), 0.0)
    a_neg = -jnp.exp(alog)
    cum = _dot_hi(tri, dt * a_neg)
    row = lax.broadcasted_iota(jnp.int32, (CHUNK, 1), 0)
    is_last = row == CHUNK - 1
    hi = 1.0 - lo
    bm = [_silu(b) for b in b_pre]
    cm = [_silu(c) for c in c_pre]
    cb2 = [_mm_nt(cm[g], jnp.concatenate([bm[g], bm[g]], axis=0)) for g in range(2)]
    ht_g = [jnp.concatenate(ht[4 * g:4 * g + 4], axis=1) for g in range(2)]
    yoff_g = [_mm(cm[g], ht_g[g]) for g in range(2)]
    yg, xdec, clast = [], [], []
    for j in range(8):
        g, k4 = j // 4, j % 4
        pair = lambda v, j=j: jnp.where(lo > 0.5, _lane_bcast(v, 2 * j), _lane_bcast(v, 2 * j + 1))
        xs = _silu(xs_pre[j])
        dte = pair(dt)
        cume = pair(cum)
        cum_last = jnp.sum(jnp.where(is_last, cume, 0.0), axis=0, keepdims=True)
        xdt = xs * dte
        rowv = jnp.sum(cume * i2, axis=0, keepdims=True)
        lm = jnp.exp(jnp.where(mask2 > 0.5, cume - rowv, -jnp.inf))
        m = cb2[g] * lm
        xblk = jnp.concatenate([xdt * lo, xdt * hi], axis=0)
        y = _mm(m, xblk)
        y = y + yoff_g[g][:, 128 * k4:128 * k4 + 128] * jnp.exp(cume)
        y = y + pair(dpar) * xs
        yg.append(y * _silu(z[j]))
        xdec.append(xdt * jnp.exp(cum_last - cume))
        clast.append(cum_last)
    ht_next = []
    for g in range(2):
        st = _mm_tn(bm[g], jnp.concatenate(xdec[4 * g:4 * g + 4], axis=1))
        for k4 in range(4):
            j = 4 * g + k4
            ht_next.append(ht[j] * jnp.exp(clast[j]) + st[:, 128 * k4:128 * k4 + 128])
    outs = []
    for g in range(2):
        ss = sum(jnp.sum(yg[j] * yg[j], axis=-1, keepdims=True) for j in range(4 * g, 4 * g + 4))
        rs = lax.rsqrt(ss * (1.0 / 512.0) + EPS)
        for j in range(4 * g, 4 * g + 4):
            outs.append(yg[j] * rs * nw[j])
    return outs, ht_next


def _tri_inverse(a):
    eye = jnp.eye(CHUNK, dtype=jnp.float32)[None]
    p = eye - a
    x = _bmm_hi(a, a)
    for i in range(4):
        both = (_bmm_hi if i == 0 else _bmm)(jnp.concatenate([p, x], axis=1), x)
        p = p + both[:, :CHUNK]
        x = both[:, CHUNK:]
    return p + _bmm(p, x)


def _solve_apply(t, r1, r2):
    both = _bmm_hi(t, jnp.concatenate([r1, r2], axis=-1))
    n = r1.shape[-1]
    return both[..., :n], both[..., n:]


@jax.custom_vjp
def _solve(a, r1, r2, t):
    return _solve_apply(t, r1, r2)


def _solve_fwd(a, r1, r2, t):
    u, w = _bmm_hi(t, r1), _bmm_hi(t, r2)
    return (u, w), (t, u, w)


def _solve_bwd(res, cts):
    t, u, w = res
    du, dw = cts
    dr1 = _bmm_tn_hi(t, du)
    dr2 = _bmm_tn_hi(t, dw)
    da = -(_bmm_nt_hi(dr1, u) + _bmm_nt_hi(dr2, w))
    return da, dr1, dr2, jnp.zeros_like(t)


_solve.defvjp(_solve_fwd, _solve_bwd)


def _gdn_chunk(q_pre, k_pre, v_pre, gate, sm, s, dtb, alog, nw, tri, i64, strict, t_in=None):
    lane = lax.broadcasted_iota(jnp.int32, (1, 128), 1)
    m_a = (lane >= 16) & (lane < 24)
    g_full = jnp.where(m_a, -jnp.exp(alog) * _softplus(sm + dtb), 0.0)
    gc = _dot_hi(tri, g_full)
    sig = _sigmoid(sm)
    heads = lambda f: jnp.concatenate([f(h)[None] for h in range(GDN_HEADS)], axis=0)
    gc3 = heads(lambda h: _lane_bcast(gc, 16 + h))
    beta3 = heads(lambda h: _lane_bcast(sig, 24 + h))
    q = _silu(q_pre)
    q = q * lax.rsqrt(jnp.sum(q * q, axis=-1, keepdims=True) + EPS) * (GDN_DK ** -0.5)
    k = _silu(k_pre)
    k = k * lax.rsqrt(jnp.sum(k * k, axis=-1, keepdims=True) + EPS)
    v = _silu(v_pre)
    gcl = gc3[:, :, :CHUNK]
    gc_row = jnp.sum(gcl * i64[None], axis=1, keepdims=True)
    incl = (strict + i64)[None] > 0.5
    decay = jnp.exp(jnp.where(incl, gcl - gc_row, -jnp.inf))
    kb = k * beta3
    a = jnp.where(strict[None] > 0.5, _bmm_nt(kb, k) * decay, 0.0)
    egc = jnp.exp(gc3)
    t = _tri_inverse(a) if t_in is None else t_in
    u, w = _solve(a, v * beta3, kb * egc, t)
    attn = _bmm_nt(q, k) * decay
    row = lax.broadcasted_iota(jnp.int32, (1, CHUNK, 1), 1)
    gl = jnp.sum(jnp.where(row == CHUNK - 1, gc3, 0.0), axis=1, keepdims=True)
    q_dec = q * egc
    k_dec = k * jnp.exp(gl - gc3)
    ws = _bmm(jnp.concatenate([w, q_dec], axis=1), s)
    v_new = u - ws[:, :CHUNK]
    o = ws[:, CHUNK:] + _bmm(attn, v_new)
    s_next = s * jnp.exp(gl) + _bmm_tn(k_dec, v_new)
    on = o * lax.rsqrt(jnp.mean(o * o, axis=-1, keepdims=True) + EPS) * nw
    return on * _silu(gate), s_next, t


def _conv_fwd(pbuf, w_ref, c0, c1):
    blk = pbuf[:, c0:c1]
    acc = w_ref[CONV_K - 1:CONV_K, c0:c1] * blk[8:72]
    for j in range(CONV_K - 1):
        acc = acc + w_ref[j:j + 1, c0:c1] * pltpu.roll(blk, CONV_K - 1 - j, axis=0)[8:72]
    return acc


MESH = pl.DeviceIdType.MESH
ANY = pl.BlockSpec(memory_space=pl.ANY)


def _me():
    x, y, c = lax.axis_index("x"), lax.axis_index("y"), lax.axis_index("c")
    return x, y, c, 4 * x + 2 * y + c


def _peer(r):
    x, y, c, _ = _me()
    px = 1 - x if r & 4 else x
    py = 1 - y if r & 2 else y
    pc = 1 - c if r & 1 else c
    return (px, py, pc), 4 * px + 2 * py + pc


def _exchange_ops(kind, in_ref, out_ref, send_sems, recv_sems, local_sem):
    me = _me()[3]
    local = pltpu.make_async_copy(in_ref.at[me] if kind == "scatter" else in_ref, out_ref.at[me], local_sem)
    sends, recvs = [], []
    for r in range(1, N_DEV):
        peer, pidx = _peer(r)
        src = in_ref.at[pidx] if kind == "scatter" else in_ref
        sems = dict(send_sem=send_sems.at[r - 1], recv_sem=recv_sems.at[r - 1], device_id=peer, device_id_type=MESH)
        sends.append(pltpu.make_async_remote_copy(src_ref=src, dst_ref=out_ref.at[me], **sems))
        recvs.append(pltpu.make_async_remote_copy(src_ref=src, dst_ref=out_ref.at[pidx], **sems))

    def start():
        local.start()
        for cp in sends:
            cp.start()

    def wait():
        for cp in recvs:
            cp.wait_recv()
        for cp in sends:
            cp.wait_send()
        local.wait()

    return start, wait


def _exchange_sems(n):
    return [pltpu.SemaphoreType.DMA((N_DEV - 1,)), pltpu.SemaphoreType.DMA((N_DEV - 1,)),
            pltpu.SemaphoreType.DMA(())] * n


def _exchange_out_shape(kind, a):
    return jax.ShapeDtypeStruct(a.shape if kind == "scatter" else (N_DEV,) + a.shape, a.dtype)


def _hosting(body, n_in, n_out, n_scratch, kinds, first, last):
    ne = len(kinds)

    def wrapped(*refs):
        ins, ex_in = refs[:n_in], refs[n_in:n_in + ne]
        o0 = n_in + ne
        outs, ex_out = refs[o0:o0 + n_out], refs[o0 + n_out:o0 + n_out + ne]
        s0 = o0 + n_out + ne
        scr, sems = refs[s0:s0 + n_scratch], refs[s0 + n_scratch:]
        ops = [_exchange_ops(kinds[e], ex_in[e], ex_out[e], *sems[3 * e:3 * e + 3]) for e in range(ne)]

        @pl.when(first())
        def _():
            for start, _ in ops:
                start()

        body(*ins, *outs, *scr)

        @pl.when(last())
        def _():
            for _, wait in ops:
                wait()

    return wrapped


GROUPS = (("z", 0, 1024), ("xbc", 1024, 2560), ("gate", 2560, 3584), ("qkv", 3584, 6656), ("sm", 6656, 6784))
GROUP_ROWS = dict(z=((0, 1024),), xbc=((1024, 2560),), gate=((2576, 3600),), qkv=((3600, 6672),),
                  sm=((2560, 2576), (6672, 6688)))


def _w_rows(w_ref, name, width):
    pieces = [w_ref[a:b, :] for a, b in GROUP_ROWS[name]]
    n = sum(b - a for a, b in GROUP_ROWS[name])
    if n < width:
        pieces.append(jnp.zeros((width - n, D_MODEL), w_ref.dtype))
    return pieces[0] if len(pieces) == 1 else jnp.concatenate(pieces, axis=0)


def inproj_fwd(x, norm_w, w_perm, gathered):
    t = x.shape[0]
    tm = min(512, t)
    steps = t // tm
    kinds = ["gather"] * len(gathered)

    def body(x_ref, nw_ref, w_ref, u_ref, z_ref, xbc_ref, gate_ref, qkv_ref, sm_ref):
        xf = x_ref[...]
        rstd = lax.rsqrt(jnp.mean(xf * xf, axis=-1, keepdims=True) + EPS)
        u = (xf * rstd * nw_ref[...]).astype(_MM)
        u_ref[...] = u
        for (name, c0, c1), o_ref in zip(GROUPS, (z_ref, xbc_ref, gate_ref, qkv_ref, sm_ref)):
            o_ref[...] = lax.dot_general(u, _w_rows(w_ref, name, c1 - c0), (((1,), (1,)), ((), ())),
                                         preferred_element_type=jnp.float32)

    outs = [jax.ShapeDtypeStruct((t, D_MODEL), _MM)] + [jax.ShapeDtypeStruct((t, c1 - c0), jnp.float32)
                                                        for _, c0, c1 in GROUPS]
    hosted = _hosting(body, 3, 6, 0, kinds, lambda: pl.program_id(0) == 0, lambda: pl.program_id(0) == steps - 1)
    return _pc_comm(
        hosted, name="inproj_fwd", grid=(steps,),
        in_specs=[pl.BlockSpec((tm, D_MODEL), lambda i: (i, 0)),
                  pl.BlockSpec((1, D_MODEL), lambda i: (0, 0)),
                  pl.BlockSpec((IN_DIM, D_MODEL), lambda i: (0, 0), pipeline_mode=pl.Buffered(1))] +
                 [ANY] * len(gathered),
        out_specs=[pl.BlockSpec((tm, D_MODEL), lambda i: (i, 0))] +
                  [pl.BlockSpec((tm, c1 - c0), lambda i: (i, 0)) for _, c0, c1 in GROUPS] + [ANY] * len(gathered),
        out_shape=outs + [_exchange_out_shape("gather", a) for a in gathered],
        scratch_shapes=_exchange_sems(len(gathered)), compiler_params=_cparams(("arbitrary",)),
    )(x, norm_w, w_perm, *gathered)


SUB_FWD = 4
SUB_BWD = 2


def _halo_spec(width, idx_fn):
    return pl.BlockSpec((8, width), lambda i: (jnp.maximum(idx_fn(i) * (SUB_FWD * CHUNK // 8) - 1, 0), 0))


def _when_first(shared, fn):
    if shared["first"] is not False:
        pl.when(shared["first"])(fn)


def _full(shape):
    nd = len(shape)
    return pl.BlockSpec(shape, lambda i: (0,) * nd)


def _ssd_split(pre_fn, z_ref, sm_ref):
    xs_pre = [pre_fn(128 * j, 128 * j + 128) for j in range(8)]
    b_pre = [pre_fn(1024 + 128 * g, 1152 + 128 * g) for g in range(2)]
    c_pre = [pre_fn(1280 + 128 * g, 1408 + 128 * g) for g in range(2)]
    z = [z_ref[:, 128 * j:128 * j + 128] for j in range(8)]
    return xs_pre, b_pre, c_pre, z, sm_ref[...]


def ssd_fwd(z, xbc, sm, conv_w, conv_b, dtb, alog, dpar, nw, cs):
    t = z.shape[0]
    nc = t // CHUNK

    def body(shared, z_ref, xbc_ref, halo_ref, sm_ref, cw_ref, cb_ref, dtb_ref, alog_ref, dpar_ref, nw_ref,
             tri_ref, i2_ref, mask2_ref, lo_ref, y_ref, hs_ref, pre_ref, pbuf, ht_scr):
        def init():
            ht_scr[...] = jnp.zeros_like(ht_scr)

        _when_first(shared, init)
        pbuf[0:8, :] = jnp.where(shared["first"], 0.0, halo_ref[...])
        pbuf[8:72, :] = xbc_ref[...]

        def pre_fn(c0, c1):
            pre = _conv_fwd(pbuf, cw_ref, c0, c1) + cb_ref[:, c0:c1]
            pre_ref[:, c0:c1] = pre
            return pre

        xs_pre, b_pre, c_pre, zz, smv = _ssd_split(pre_fn, z_ref, sm_ref)
        ht = [ht_scr[:, 128 * j:128 * j + 128] for j in range(8)]
        hs_ref[0] = ht_scr[...]
        nwl = [nw_ref[:, 128 * j:128 * j + 128] for j in range(8)]
        outs, ht_next = _ssd_chunk(xs_pre, b_pre, c_pre, zz, smv, ht, dtb_ref[...], alog_ref[...], dpar_ref[...],
                                   nwl, tri_ref[...], i2_ref[...], mask2_ref[...], lo_ref[...])
        for j in range(8):
            y_ref[:, 128 * j:128 * j + 128] = outs[j].astype(y_ref.dtype)
            ht_scr[:, 128 * j:128 * j + 128] = ht_next[j]

    blk = lambda w: pl.BlockSpec((SUB_FWD * CHUNK, w), lambda i: (i, 0))
    return dict(
        body=body,
        in_kinds=["rows", "rows", ("halo", 1), "rows"] + ["full"] * 10, out_kinds=["rows", "state", "rows"],
        in_specs=[blk(1024), blk(1536), _halo_spec(1536, lambda i: i), blk(128),
                  _full((CONV_K, 1536)), _full((1, 1536)), _full((1, 128)), _full((1, 128)), _full((1, 128)),
                  _full((1, 1024)), _full((64, 64)), _full((64, 128)), _full((64, 128)),
                  _full((1, 128))],
        out_specs=[blk(1024), pl.BlockSpec((SUB_FWD, 128, 1024), lambda i: (i, 0, 0)), blk(1536)],
        out_shape=[jax.ShapeDtypeStruct((t, 1024), _MM), jax.ShapeDtypeStruct((nc, 128, 1024), jnp.float32),
                   jax.ShapeDtypeStruct((t, 1536), jnp.float32)],
        scratch=[pltpu.VMEM((72, 1536), jnp.float32), pltpu.VMEM((128, 1024), jnp.float32)],
        args=[z, xbc, xbc, sm, conv_w, conv_b, dtb, alog, dpar, nw, cs["tri"], cs["i2"], cs["mask2"], cs["lo"]])


def _conv_bwd(dpre_list, col_ranges, dbuf, carry, x_ref, cw_ref, dx_ref, dcw_ref, dcb_ref, first):
    for dpre, (c0, c1) in zip(dpre_list, col_ranges):
        dbuf[0:64, c0:c1] = dpre
    dbuf[64:72, :] = jnp.where(first, 0.0, carry[...])
    carry[...] = dbuf[0:8, :]
    for (c0, c1) in col_ranges:
        xin = x_ref[:, c0:c1]
        blk = dbuf[:, c0:c1]
        acc = None
        for j in range(CONV_K):
            sh = blk[0:64] if j == CONV_K - 1 else pltpu.roll(blk, 72 - (CONV_K - 1 - j), axis=0)[0:64]
            term = cw_ref[j:j + 1, c0:c1] * sh
            acc = term if acc is None else acc + term
            dcw_ref[j:j + 1, c0:c1] += jnp.sum(xin * sh, axis=0, keepdims=True)
        dx_ref[:, c0:c1] = acc.astype(dx_ref.dtype)
        if dcb_ref is not None:
            dcb_ref[0:1, c0:c1] += jnp.sum(dbuf[0:64, c0:c1], axis=0, keepdims=True)


def ssd_bwd(z, xbc, pre, sm, hs, dy, conv_w, dtb, alog, dpar, nw, cs):
    t = z.shape[0]
    nc = t // CHUNK

    def body(shared, z_ref, xbc_ref, pre_ref, sm_ref, hs_ref, dy_ref, cw_ref, dtb_ref, alog_ref, dpar_ref, nw_ref,
             tri_ref, i2_ref, mask2_ref, lo_ref,
             dz_ref, dxbc_ref, dcw_ref, dcb_ref, ddtb_ref, dalog_ref, ddpar_ref, dnw_ref,
             dbuf, carry, dht_scr):
        def init():
            dht_scr[...] = jnp.zeros_like(dht_scr)
            dcw_ref[...] = jnp.zeros_like(dcw_ref)
            dcb_ref[...] = jnp.zeros_like(dcb_ref)
            ddtb_ref[...] = jnp.zeros_like(ddtb_ref)
            dalog_ref[...] = jnp.zeros_like(dalog_ref)
            ddpar_ref[...] = jnp.zeros_like(ddpar_ref)
            dnw_ref[...] = jnp.zeros_like(dnw_ref)

        _when_first(shared, init)
        pre_fn = lambda c0, c1: pre_ref[:, c0:c1]
        xs_pre, b_pre, c_pre, zz, smv = _ssd_split(pre_fn, z_ref, sm_ref)
        ht = [hs_ref[0, :, 128 * j:128 * j + 128] for j in range(8)]
        nwl = [nw_ref[:, 128 * j:128 * j + 128] for j in range(8)]
        consts = (tri_ref[...], i2_ref[...], mask2_ref[...], lo_ref[...])

        def f(xs_pre, b_pre, c_pre, zz, smv, ht, dtb, alog, dpar, nwl):
            return _ssd_chunk(xs_pre, b_pre, c_pre, zz, smv, ht, dtb, alog, dpar, nwl, *consts)

        _, vjp = jax.vjp(f, xs_pre, b_pre, c_pre, zz, smv, ht, dtb_ref[...], alog_ref[...], dpar_ref[...], nwl)
        dys = [dy_ref[:, 128 * j:128 * j + 128] for j in range(8)]
        dhts = [dht_scr[:, 128 * j:128 * j + 128] for j in range(8)]
        dxs, db, dc, dzz, dsm, dht, ddtb, dalog, ddpar, dnwl = vjp((dys, dhts))
        for j in range(8):
            dz_ref[:, 128 * j:128 * j + 128] = dzz[j].astype(dz_ref.dtype)
            dht_scr[:, 128 * j:128 * j + 128] = dht[j]
            dnw_ref[0:1, 128 * j:128 * j + 128] += dnwl[j]
        shared["dsm_ssd"] = dsm
        ddtb_ref[0:1, :] += ddtb
        dalog_ref[0:1, :] += dalog
        ddpar_ref[0:1, :] += ddpar
        ranges = ([(128 * j, 128 * j + 128) for j in range(8)] + [(1024 + 128 * g, 1152 + 128 * g) for g in range(2)]
                  + [(1280 + 128 * g, 1408 + 128 * g) for g in range(2)])
        _conv_bwd(dxs + db + dc, ranges, dbuf, carry, xbc_ref, cw_ref, dxbc_ref, dcw_ref, dcb_ref, shared["first"])

    ns = nc // SUB_BWD
    rblk = lambda w: pl.BlockSpec((SUB_BWD * CHUNK, w), lambda i: (ns - 1 - i, 0))
    acc = lambda w: pl.BlockSpec((8, w), lambda i: (0, 0))
    f32 = jnp.float32
    return dict(
        body=body,
        in_kinds=["rows"] * 4 + ["state", "rows"] + ["full"] * 9, out_kinds=["rows", "rows"] + ["full"] * 6,
        in_specs=[rblk(1024), rblk(1536), rblk(1536), rblk(128),
                  pl.BlockSpec((SUB_BWD, 128, 1024), lambda i: (ns - 1 - i, 0, 0)), rblk(1024),
                  _full((CONV_K, 1536)), _full((1, 128)), _full((1, 128)), _full((1, 128)),
                  _full((1, 1024)), _full((64, 64)), _full((64, 128)), _full((64, 128)),
                  _full((1, 128))],
        out_specs=[rblk(1024), rblk(1536), acc(1536), acc(1536), acc(128), acc(128), acc(128), acc(1024)],
        out_shape=[jax.ShapeDtypeStruct((t, 1024), f32), jax.ShapeDtypeStruct((t, 1536), f32),
                   jax.ShapeDtypeStruct((8, 1536), f32),
                   jax.ShapeDtypeStruct((8, 1536), f32), jax.ShapeDtypeStruct((8, 128), f32),
                   jax.ShapeDtypeStruct((8, 128), f32), jax.ShapeDtypeStruct((8, 128), f32),
                   jax.ShapeDtypeStruct((8, 1024), f32)],
        scratch=[pltpu.VMEM((72, 1536), f32), pltpu.VMEM((8, 1536), f32), pltpu.VMEM((128, 1024), f32)],
        args=[z, xbc, pre, sm, hs, dy, conv_w, dtb, alog, dpar, nw, cs["tri"], cs["i2"], cs["mask2"], cs["lo"]])


def _gdn_split(pre_fn, gate_ref):
    def heads(base):
        return jnp.stack([pre_fn(base + 128 * h, base + 128 * h + 128) for h in range(GDN_HEADS)])
    gate = jnp.stack([gate_ref[:, 128 * h:128 * h + 128] for h in range(GDN_HEADS)])
    return heads(0), heads(1024), heads(2048), gate


def gdn_fwd(gate, qkv, sm, conv_w, dtb, alog, nw, cs):
    t = gate.shape[0]
    nc = t // CHUNK

    def body(shared, gate_ref, qkv_ref, halo_ref, sm_ref, cw_ref, dtb_ref, alog_ref, nw_ref,
             tri_ref, i64_ref, strict_ref, o_ref, ss_ref, ts_ref, pre_ref, pbuf, s_scr):
        def init():
            s_scr[...] = jnp.zeros_like(s_scr)

        _when_first(shared, init)
        pbuf[0:8, :] = jnp.where(shared["first"], 0.0, halo_ref[...])
        pbuf[8:72, :] = qkv_ref[...]

        def pre_fn(c0, c1):
            pre = _conv_fwd(pbuf, cw_ref, c0, c1)
            pre_ref[:, c0:c1] = pre
            return pre

        q_pre, k_pre, v_pre, g3 = _gdn_split(pre_fn, gate_ref)
        s = s_scr[...]
        ss_ref[0] = s
        out, s_next, tinv = _gdn_chunk(q_pre, k_pre, v_pre, g3, sm_ref[...], s, dtb_ref[...], alog_ref[...],
                                       nw_ref[...], tri_ref[...], i64_ref[...], strict_ref[...])
        ts_ref[0] = tinv
        s_scr[...] = s_next
        for h in range(GDN_HEADS):
            o_ref[:, 128 * h:128 * h + 128] = out[h].astype(o_ref.dtype)

    blk = lambda w: pl.BlockSpec((SUB_FWD * CHUNK, w), lambda i: (i, 0))
    return dict(
        body=body,
        in_kinds=["rows", "rows", ("halo", 1), "rows"] + ["full"] * 7, out_kinds=["rows", "state", "state", "rows"],
        in_specs=[blk(1024), blk(3072), _halo_spec(3072, lambda i: i), blk(128),
                  _full((CONV_K, 3072)), _full((1, 128)), _full((1, 128)), _full((1, 128)),
                  _full((64, 64)), _full((64, 64)), _full((64, 64))],
        out_specs=[blk(1024), pl.BlockSpec((SUB_FWD, 8, 128, 128), lambda i: (i, 0, 0, 0)),
                   pl.BlockSpec((SUB_FWD, 8, CHUNK, CHUNK), lambda i: (i, 0, 0, 0)), blk(3072)],
        out_shape=[jax.ShapeDtypeStruct((t, 1024), _MM), jax.ShapeDtypeStruct((nc, 8, 128, 128), jnp.float32),
                   jax.ShapeDtypeStruct((nc, 8, CHUNK, CHUNK), jnp.float32),
                   jax.ShapeDtypeStruct((t, 3072), jnp.float32)],
        scratch=[pltpu.VMEM((72, 3072), jnp.float32), pltpu.VMEM((8, 128, 128), jnp.float32)],
        args=[gate, qkv, qkv, sm, conv_w, dtb, alog, nw, cs["tri"], cs["i64"], cs["strict"]])


def gdn_bwd(gate, qkv, pre, sm, ss, ts, do, conv_w, dtb, alog, nw, cs):
    t = gate.shape[0]
    nc = t // CHUNK

    def body(shared, gate_ref, qkv_ref, pre_ref, sm_ref, ss_ref, ts_ref, do_ref, cw_ref, dtb_ref, alog_ref,
             nw_ref, tri_ref, i64_ref, strict_ref,
             dgate_ref, dqkv_ref, dsm_ref, dcw_ref, ddtb_ref, dalog_ref, dnw_ref,
             dbuf, carry, ds_scr):
        def init():
            ds_scr[...] = jnp.zeros_like(ds_scr)
            dcw_ref[...] = jnp.zeros_like(dcw_ref)
            ddtb_ref[...] = jnp.zeros_like(ddtb_ref)
            dalog_ref[...] = jnp.zeros_like(dalog_ref)
            dnw_ref[...] = jnp.zeros_like(dnw_ref)

        _when_first(shared, init)

        q_pre, k_pre, v_pre, g3 = _gdn_split(lambda c0, c1: pre_ref[:, c0:c1], gate_ref)
        consts = (tri_ref[...], i64_ref[...], strict_ref[...], ts_ref[0])

        def f(q_pre, k_pre, v_pre, g3, smv, s, dtb, alog, nwv):
            return _gdn_chunk(q_pre, k_pre, v_pre, g3, smv, s, dtb, alog, nwv, *consts)[:2]

        _, vjp = jax.vjp(f, q_pre, k_pre, v_pre, g3, sm_ref[...], ss_ref[0], dtb_ref[...], alog_ref[...], nw_ref[...])
        do3 = jnp.stack([do_ref[:, 128 * h:128 * h + 128] for h in range(GDN_HEADS)])
        dq, dk, dv, dg3, dsm, ds, ddtb, dalog, dnw = vjp((do3, ds_scr[...]))
        ds_scr[...] = ds
        for h in range(GDN_HEADS):
            dgate_ref[:, 128 * h:128 * h + 128] = dg3[h].astype(dgate_ref.dtype)
        dsm_ref[...] = (dsm + shared["dsm_ssd"]).astype(dsm_ref.dtype)
        ddtb_ref[0:1, :] += ddtb
        dalog_ref[0:1, :] += dalog
        dnw_ref[0:1, :] += dnw
        ranges = [(base + 128 * h, base + 128 * h + 128) for base in (0, 1024, 2048) for h in range(GDN_HEADS)]
        dlist = [d[h] for d in (dq, dk, dv) for h in range(GDN_HEADS)]
        _conv_bwd(dlist, ranges, dbuf, carry, qkv_ref, cw_ref, dqkv_ref, dcw_ref, None, shared["first"])

    ns = nc // SUB_BWD
    rblk = lambda w: pl.BlockSpec((SUB_BWD * CHUNK, w), lambda i: (ns - 1 - i, 0))
    acc = lambda w: pl.BlockSpec((8, w), lambda i: (0, 0))
    f32 = jnp.float32
    return dict(
        body=body,
        in_kinds=["rows"] * 4 + ["state", "state", "rows"] + ["full"] * 7, out_kinds=["rows"] * 3 + ["full"] * 4,
        in_specs=[rblk(1024), rblk(3072), rblk(3072), rblk(128),
                  pl.BlockSpec((SUB_BWD, 8, 128, 128), lambda i: (ns - 1 - i, 0, 0, 0)),
                  pl.BlockSpec((SUB_BWD, 8, CHUNK, CHUNK), lambda i: (ns - 1 - i, 0, 0, 0)), rblk(1024),
                  _full((CONV_K, 3072)), _full((1, 128)), _full((1, 128)), _full((1, 128)),
                  _full((64, 64)), _full((64, 64)), _full((64, 64))],
        out_specs=[rblk(1024), rblk(3072), rblk(128), acc(3072), acc(128), acc(128), acc(128)],
        out_shape=[jax.ShapeDtypeStruct((t, 1024), f32), jax.ShapeDtypeStruct((t, 3072), f32),
                   jax.ShapeDtypeStruct((t, 128), f32), jax.ShapeDtypeStruct((8, 3072), f32),
                   jax.ShapeDtypeStruct((8, 128), f32), jax.ShapeDtypeStruct((8, 128), f32),
                   jax.ShapeDtypeStruct((8, 128), f32)],
        scratch=[pltpu.VMEM((72, 3072), f32), pltpu.VMEM((8, 3072), f32), pltpu.VMEM((8, 128, 128), f32)],
        args=[gate, qkv, pre, sm, ss, ts, do, conv_w, dtb, alog, nw, cs["tri"], cs["i64"], cs["strict"]])


def _chunk_call(parts, name, nc, reverse):
    n_in = [len(p["args"]) for p in parts]
    n_out = [len(p["out_shape"]) for p in parts]
    n_scr = [len(p["scratch"]) for p in parts]
    sub = SUB_BWD if reverse else SUB_FWD
    order = list(range(sub))[::-1] if reverse else list(range(sub))

    def view(ref, kind, s, refs):
        if kind == "rows":
            return ref.at[pl.ds(CHUNK * s, CHUNK)]
        if kind == "state":
            return ref.at[pl.ds(s, 1)]
        if kind == "full":
            return ref
        src = refs[kind[1]]
        return ref if s == 0 else src.at[pl.ds(CHUNK * s - 8, 8)]

    def body(*refs):
        ins, outs, scr = refs[:sum(n_in)], refs[sum(n_in):sum(n_in) + sum(n_out)], refs[sum(n_in) + sum(n_out):]
        for s in order:
            shared = {"first": (pl.program_id(0) == 0) if s == order[0] else False}
            for k, p in enumerate(parts):
                i0, o0, s0 = sum(n_in[:k]), sum(n_out[:k]), sum(n_scr[:k])
                p_ins = ins[i0:i0 + n_in[k]]
                p["body"](shared,
                          *[view(r, kd, s, p_ins) for r, kd in zip(p_ins, p["in_kinds"])],
                          *[view(r, kd, s, None) for r, kd in zip(outs[o0:o0 + n_out[k]], p["out_kinds"])],
                          *scr[s0:s0 + n_scr[k]])

    cat = lambda key: [v for p in parts for v in p[key]]
    return _pc(body, name=name, grid=(nc // sub,), in_specs=cat("in_specs"), out_specs=cat("out_specs"),
               out_shape=cat("out_shape"), scratch_shapes=cat("scratch"),
               compiler_params=_cparams(("arbitrary",)))(*cat("args"))


def out_fwd_bwd(x, tgt, y_ssd, y_gdn, w_out, fnw):
    t = x.shape[0]
    tm = min(512, t)
    f32 = jnp.float32

    def body(x_ref, tgt_ref, ys_ref, yg_ref, w_ref, fnw_ref,
             dout_ref, dys_ref, dyg_ref, gw_ref, gfnw_ref, loss_ref, gw_acc):
        i = pl.program_id(0)

        @pl.when(i == 0)
        def _():
            gw_acc[...] = jnp.zeros_like(gw_acc)
            gfnw_ref[...] = jnp.zeros_like(gfnw_ref)
            loss_ref[...] = jnp.zeros_like(loss_ref)

        ys = ys_ref[...]
        yg = yg_ref[...]
        out = x_ref[...] + jnp.dot(ys, w_ref[0:1024, :], preferred_element_type=f32) \
            + jnp.dot(yg, w_ref[1024:2048, :], preferred_element_type=f32)
        rstd = lax.rsqrt(jnp.mean(out * out, axis=-1, keepdims=True) + EPS)
        yhat = out * rstd
        fw = fnw_ref[...]
        e = yhat * fw - tgt_ref[...]
        loss_ref[...] += 0.5 * jnp.sum(jnp.sum(e * e, axis=-1, keepdims=True) * (1.0 / D_MODEL), axis=0, keepdims=True)
        dyf = e * (1.0 / D_MODEL)
        gfnw_ref[0:1, :] += jnp.sum(dyf * yhat, axis=0, keepdims=True)
        dyhat = dyf * fw
        dout = rstd * (dyhat - yhat * jnp.mean(dyhat * yhat, axis=-1, keepdims=True))
        dout_ref[...] = dout
        db = dout.astype(_MM)
        dys_ref[...] = lax.dot_general(db, w_ref[0:1024, :], (((1,), (1,)), ((), ())), preferred_element_type=f32)
        dyg_ref[...] = lax.dot_general(db, w_ref[1024:2048, :], (((1,), (1,)), ((), ())), preferred_element_type=f32)
        gw_acc[0:1024, :] += lax.dot_general(ys, db, (((0,), (0,)), ((), ())), preferred_element_type=f32)
        gw_acc[1024:2048, :] += lax.dot_general(yg, db, (((0,), (0,)), ((), ())), preferred_element_type=f32)

        @pl.when(i == steps - 1)
        def _():
            gw_ref[...] = gw_acc[...].astype(gw_ref.dtype)

    steps = t // tm
    blk = pl.BlockSpec((tm, D_MODEL), lambda i: (i, 0))
    return _pc(
        body, name="out_fwd_bwd", grid=(steps,),
        in_specs=[blk, blk, blk, blk, _full((MIX_WIDTH, D_MODEL)), _full((1, D_MODEL))],
        out_specs=[blk, blk, blk, _full((MIX_WIDTH, D_MODEL)), _full((8, D_MODEL)), _full((1, 128))],
        out_shape=[jax.ShapeDtypeStruct((t, D_MODEL), f32)] * 3 +
                  [jax.ShapeDtypeStruct((MIX_WIDTH, D_MODEL), _MM), jax.ShapeDtypeStruct((8, D_MODEL), f32),
                   jax.ShapeDtypeStruct((1, 128), f32)],
        scratch_shapes=[pltpu.VMEM((MIX_WIDTH, D_MODEL), f32)],
        compiler_params=_cparams(("arbitrary",)),
    )(x, tgt, y_ssd, y_gdn, w_out, fnw)


def inproj_bwd_dx(x, dout, norm_w, w_perm, dgroups, scattered):
    t = x.shape[0]
    tm = min(256, t)
    f32 = jnp.float32

    def body(x_ref, dout_ref, nw_ref, w_ref, dz_ref, dxbc_ref, dgate_ref, dqkv_ref, dsm_ref, dx_ref, gnw_ref):
        i = pl.program_id(0)

        @pl.when(i == 0)
        def _():
            gnw_ref[...] = jnp.zeros_like(gnw_ref)

        du = None
        for (name, c0, c1), d_ref in zip(GROUPS, (dz_ref, dxbc_ref, dgate_ref, dqkv_ref, dsm_ref)):
            term = jnp.dot(d_ref[...].astype(_MM), _w_rows(w_ref, name, c1 - c0), preferred_element_type=f32)
            du = term if du is None else du + term
        xf = x_ref[...]
        rstd = lax.rsqrt(jnp.mean(xf * xf, axis=-1, keepdims=True) + EPS)
        xhat = xf * rstd
        gnw_ref[0:1, :] += jnp.sum(du * xhat, axis=0, keepdims=True)
        dxh = du * nw_ref[...]
        dx_ref[...] = dout_ref[...] + rstd * (dxh - xhat * jnp.mean(dxh * xhat, axis=-1, keepdims=True))

    blk = lambda w: pl.BlockSpec((tm, w), lambda i: (i, 0))
    steps = t // tm
    kinds = ["scatter"] * len(scattered)
    hosted = _hosting(body, 9, 2, 0, kinds, lambda: pl.program_id(0) == 0, lambda: pl.program_id(0) == steps - 1)
    return _pc_comm(
        hosted, name="inproj_bwd_dx", grid=(steps,),
        in_specs=[blk(D_MODEL), blk(D_MODEL), _full((1, D_MODEL)), _full((IN_DIM, D_MODEL))] +
                 [blk(c1 - c0) for _, c0, c1 in GROUPS] + [ANY] * len(scattered),
        out_specs=[blk(D_MODEL), _full((8, D_MODEL))] + [ANY] * len(scattered),
        out_shape=[jax.ShapeDtypeStruct((t, D_MODEL), f32), jax.ShapeDtypeStruct((8, D_MODEL), f32)] +
                  [_exchange_out_shape("scatter", a) for a in scattered],
        scratch_shapes=_exchange_sems(len(scattered)), compiler_params=_cparams(("arbitrary",)),
    )(x, dout, norm_w, w_perm, *dgroups, *scattered)


def grad_w_group(u, dg, name, scattered=()):
    t, n = dg.shape
    tn = n if n <= 1536 else 1024
    budget = 40 * 1024 * 1024
    tm = next((c for c in (4096, 2048, 1024, 512, 256)
               if t % c == 0 and tn * D_MODEL * 4 + 2 * (c * tn * 4 + c * D_MODEL * 2 + tn * D_MODEL * 2) <= budget), t)
    nj, nk = n // tn, t // tm
    f32 = jnp.float32

    def body(u_ref, d_ref, o_ref, acc):
        k = pl.program_id(1)

        @pl.when(k == 0)
        def _():
            acc[...] = jnp.zeros_like(acc)

        acc[...] += lax.dot_general(d_ref[...].astype(_MM), u_ref[...], (((0,), (0,)), ((), ())),
                                    preferred_element_type=f32)

        @pl.when(k == nk - 1)
        def _():
            o_ref[...] = acc[...].astype(o_ref.dtype)

    ne = len(scattered)
    hosted = _hosting(body, 2, 1, 1, ["scatter"] * ne,
                      lambda: (pl.program_id(0) == 0) & (pl.program_id(1) == 0),
                      lambda: (pl.program_id(0) == nj - 1) & (pl.program_id(1) == nk - 1))
    res = (_pc_comm if ne else _pc)(
        hosted, name=name, grid=(nj, nk),
        in_specs=[pl.BlockSpec((tm, D_MODEL), lambda j, k: (k, 0)),
                  pl.BlockSpec((tm, tn), lambda j, k: (k, j))] + [ANY] * ne,
        out_specs=[pl.BlockSpec((tn, D_MODEL), lambda j, k: (j, 0))] + [ANY] * ne,
        out_shape=[jax.ShapeDtypeStruct((n, D_MODEL), _MM)] + [_exchange_out_shape("scatter", a) for a in scattered],
        scratch_shapes=[pltpu.VMEM((tn, D_MODEL), f32)] + _exchange_sems(ne),
        compiler_params=_cparams(("arbitrary", "arbitrary")),
    )(u, dg, *scattered)
    return res if ne else res[0]


def grad_w_many(u, dgs, name):
    t = u.shape[0]
    widths = [d.shape[1] for d in dgs]
    tot, ng = sum(widths), len(dgs)
    f32 = jnp.float32
    budget = 48 * 1024 * 1024
    tm = next((c for c in (2048, 1024, 512, 256)
               if t % c == 0 and tot * D_MODEL * 4 + 2 * (c * tot * 4 + c * D_MODEL * 2 + tot * D_MODEL * 2) <= budget), t)
    nk = t // tm

    def body(*refs):
        u_ref, d_refs, o_refs, accs = refs[0], refs[1:1 + ng], refs[1 + ng:1 + 2 * ng], refs[1 + 2 * ng:]
        k = pl.program_id(0)

        @pl.when(k == 0)
        def _():
            for acc in accs:
                acc[...] = jnp.zeros_like(acc)

        uu = u_ref[...]
        for d_ref, acc in zip(d_refs, accs):
            acc[...] += lax.dot_general(d_ref[...].astype(_MM), uu, (((0,), (0,)), ((), ())),
                                        preferred_element_type=f32)

        @pl.when(k == nk - 1)
        def _():
            for o_ref, acc in zip(o_refs, accs):
                o_ref[...] = acc[...].astype(o_ref.dtype)

    return _pc(
        body, name=name, grid=(nk,),
        in_specs=[pl.BlockSpec((tm, D_MODEL), lambda k: (k, 0))] + [pl.BlockSpec((tm, n), lambda k: (k, 0)) for n in widths],
        out_specs=[pl.BlockSpec((n, D_MODEL), lambda k: (0, 0)) for n in widths],
        out_shape=[jax.ShapeDtypeStruct((n, D_MODEL), _MM) for n in widths],
        scratch_shapes=[pltpu.VMEM((n, D_MODEL), f32) for n in widths],
        compiler_params=_cparams(("arbitrary",)),
    )(u, *dgs)


def _pad_lanes(v, off):
    n = v.shape[-1]
    return jnp.pad(v.reshape(1, n).astype(jnp.float32), ((0, 0), (off, 128 - off - n)))


REF_ROWS = dict(z=(0, 1024), xbc=(1024, 2560), dt=(2560, 2576), gate=(2576, 3600), qkv=(3600, 6672), ab=(6672, 6688))


def unperm_w_in(gz, gxbc, ggate, gqkv, gsm):
    src = dict(z=gz, xbc=gxbc, dt=gsm[0:16], gate=ggate, qkv=gqkv, ab=gsm[16:32])
    slabs = []
    for k in range(N_DEV):
        a, b = k * W_IN_SHARD, (k + 1) * W_IN_SHARD
        parts = []
        for name, (s, e) in REF_ROWS.items():
            lo, hi = max(a, s), min(b, e)
            if lo < hi:
                parts.append(src[name][lo - s:hi - s])
        slabs.append(jnp.concatenate(parts, axis=0))
    return jnp.stack(slabs)


def all_gather(arrs, name):
    n = len(arrs)

    def body(*refs):
        ins, outs = refs[:n], refs[n:2 * n]
        send_sems, recv_sems, local_sems = refs[2 * n:]
        x, y, c, me = _me()
        sibling = (x, y, 1 - c)
        chips = [(1 - x, y), (x, 1 - y), (1 - x, 1 - y)]

        def idx(px, py, pc):
            return 4 * px + 2 * py + pc

        def copy(a, k, block, to, src=None):
            slot = outs[a].at[idx(*block)]
            return pltpu.make_async_remote_copy(src_ref=slot if src is None else src, dst_ref=slot,
                                                send_sem=send_sems.at[a, k], recv_sem=recv_sems.at[a, k],
                                                device_id=to, device_id_type=MESH)

        local = [pltpu.make_async_copy(ins[a], outs[a].at[me], local_sems.at[a]) for a in range(n)]
        for cp in local:
            cp.start()
        started = []
        for a in range(n):
            first = [copy(a, 0, (x, y, c), sibling, src=ins[a])]
            first += [copy(a, 1 + j, (x, y, c), (*chip, c), src=ins[a]) for j, chip in enumerate(chips)]
            for cp in first:
                cp.start()
            started += first
        for a in range(n):
            for j, chip in enumerate(chips):
                copy(a, 1 + j, (*chip, c), (x, y, c)).wait_recv()
                fwd = copy(a, 4 + j, (*chip, c), sibling)
                fwd.start()
                started.append(fwd)
        for a in range(n):
            copy(a, 0, sibling, (x, y, c)).wait_recv()
            for j, chip in enumerate(chips):
                copy(a, 4 + j, (*chip, 1 - c), (x, y, c)).wait_recv()
        for cp in started:
            cp.wait_send()
        for cp in local:
            cp.wait()

    return _pc_comm(
        body, name=name, in_specs=[ANY] * n, out_specs=[ANY] * n,
        out_shape=[jax.ShapeDtypeStruct((N_DEV,) + a.shape, a.dtype) for a in arrs],
        scratch_shapes=[pltpu.SemaphoreType.DMA((n, 7)), pltpu.SemaphoreType.DMA((n, 7)),
                        pltpu.SemaphoreType.DMA((n,))],
    )(*arrs)


def adamw_sum(recv, w, m, v, rows, name, cols=None):
    r, ccols = w.shape
    f32 = jnp.float32
    c1 = 1.0 / (1.0 - ADAM_B1 ** ADAM_STEP)
    c2 = 1.0 / (1.0 - ADAM_B2 ** ADAM_STEP)

    def body(recv_ref, w_ref, m_ref, v_ref, g_ref, d_ref, mo_ref, vo_ref):
        g = recv_ref[0].astype(f32)
        for k in range(1, N_DEV):
            g = g + recv_ref[k].astype(f32)
        mn = ADAM_B1 * m_ref[...] + (1.0 - ADAM_B1) * g
        vn = ADAM_B2 * v_ref[...] + (1.0 - ADAM_B2) * (g * g)
        g_ref[...] = g
        mo_ref[...] = mn
        vo_ref[...] = vn
        d_ref[...] = -ADAM_LR * ((mn * c1) / (jnp.sqrt(vn * c2) + ADAM_EPS) + ADAM_WD * w_ref[...])

    if cols is None:
        blk = pl.BlockSpec((rows, ccols), lambda i: (i, 0))
        rblk, steps = pl.BlockSpec((N_DEV, rows, ccols), lambda i: (0, i, 0)), r // rows
    else:
        blk = pl.BlockSpec((r, cols), lambda i: (0, i))
        rblk, steps = pl.BlockSpec((N_DEV, r, cols), lambda i: (0, 0, i)), ccols // cols
    return _pc(
        body, name=name, grid=(steps,),
        in_specs=[rblk, blk, blk, blk],
        out_specs=[blk] * 4, out_shape=[jax.ShapeDtypeStruct((r, ccols), f32)] * 4,
        compiler_params=_cparams(("arbitrary",)),
    )(recv, w, m, v)


SMALL = (("norm_w", 1, 1024, 0), ("ssd_conv_b", 1, 1536, 0), ("ssd_dt_bias", 1, 16, 0), ("ssd_a_log", 1, 16, 0),
         ("ssd_d", 1, 16, 0), ("ssd_norm_w", 1, 1024, 0), ("gdn_dt_bias", 1, 8, 16), ("gdn_a_log", 1, 8, 16),
         ("gdn_norm_w", 1, 128, 0), ("final_norm_w", 1, 1024, 0),
         ("ssd_conv_w", CONV_K, SSD_CONV_DIM // N_DEV, 0), ("gdn_conv_w", CONV_K, GDN_CONV_DIM // N_DEV, 0))


def _small_layout():
    out, off = [], 0
    for name, rows, n, lane0 in SMALL + (("loss", 1, 128, 0),):
        stride = -(-(lane0 + n) // 128) * 128
        out.append((name, rows, n, lane0, stride, off))
        off += rows * stride
    return out, off


def scatter_small(accs):
    layout, total = _small_layout()
    f32 = jnp.float32

    def body(*refs):
        acc_refs, out_ref, slabs = refs[:len(layout)], refs[len(layout)], refs[len(layout) + 1]
        sems = refs[len(layout) + 2:]
        slabs[...] = jnp.zeros_like(slabs)
        for (name, rows, n, lane0, stride, off), acc in zip(layout, acc_refs):
            for k in range(N_DEV):
                if rows == 1:
                    slabs[k, :, off:off + stride] = acc[0:1, 0:stride]
                else:
                    for j in range(rows):
                        slabs[k, :, off + stride * j:off + stride * j + n] = acc[j:j + 1, n * k:n * k + n]
        start, wait = _exchange_ops("scatter", slabs, out_ref, *sems)
        start()
        wait()

    return _pc_comm(
        body, name="scatter_small_grads", out_specs=ANY, out_shape=jax.ShapeDtypeStruct((N_DEV, 1, total), f32),
        scratch_shapes=[pltpu.VMEM((N_DEV, 1, total), f32)] + _exchange_sems(1),
    )(*accs)


def adamw_small(recv, w, m, v):
    layout, total = _small_layout()
    loss_off = layout[-1][5]
    layout = layout[:-1]
    f32 = jnp.float32
    c1 = 1.0 / (1.0 - ADAM_B1 ** ADAM_STEP)
    c2 = 1.0 / (1.0 - ADAM_B2 ** ADAM_STEP)
    np_ = len(layout)

    def body(*refs):
        recv_ref = refs[0]
        w_refs, m_refs, v_refs = refs[1:1 + np_], refs[1 + np_:1 + 2 * np_], refs[1 + 2 * np_:1 + 3 * np_]
        o_refs = refs[1 + 3 * np_:]
        g_all = recv_ref[0]
        for k in range(1, N_DEV):
            g_all = g_all + recv_ref[k]
        o_refs[4 * np_][...] = g_all[:, loss_off:loss_off + 128]

        def update(g, wv, mv, vv):
            mn = ADAM_B1 * mv + (1.0 - ADAM_B1) * g
            vn = ADAM_B2 * vv + (1.0 - ADAM_B2) * (g * g)
            return g, -ADAM_LR * ((mn * c1) / (jnp.sqrt(vn * c2) + ADAM_EPS) + ADAM_WD * wv), mn, vn

        for p, (name, rows, n, lane0, stride, off) in enumerate(layout):
            outs = o_refs[4 * p:4 * p + 4]
            if rows == 1:
                res = update(g_all[:, off + lane0:off + lane0 + n], w_refs[p][...], m_refs[p][...], v_refs[p][...])
                for o, r in zip(outs, res):
                    o[...] = r
            else:
                for j in range(rows):
                    res = update(g_all[:, off + stride * j:off + stride * j + n], w_refs[p][0, j:j + 1, :],
                                 m_refs[p][0, j:j + 1, :], v_refs[p][0, j:j + 1, :])
                    for o, r in zip(outs, res):
                        o[0, j:j + 1, :] = r

    names = [e[0] for e in layout]
    ins = [recv] + [d[nm] for d in (w, m, v) for nm in names]
    out_shape = [jax.ShapeDtypeStruct(w[nm].shape, f32) for nm in names for _ in range(4)]
    out_shape.append(jax.ShapeDtypeStruct((1, 128), f32))
    res = _pc(body, name="adamw_small", out_shape=out_shape)(*ins)
    return {nm: tuple(res[4 * p:4 * p + 4]) for p, nm in enumerate(names)}, res[4 * np_]


SHARD = (("ssd_conv_w", CONV_K * SSD_CONV_DIM // N_DEV), ("gdn_conv_w", CONV_K * GDN_CONV_DIM // N_DEV))
SHARD_ROWS = 24


def _rows_of(size):
    return -(-size // 128)


def _pack(vals, layout, total_rows):
    parts = []
    for (name, size), val in zip(layout, vals):
        flat = val.reshape(-1).astype(jnp.float32)
        parts.append(jnp.pad(flat, (0, _rows_of(size) * 128 - size)).reshape(-1, 128))
    used = sum(_rows_of(s) for _, s in layout)
    parts.append(jnp.zeros((total_rows - used, 128), jnp.float32))
    return jnp.concatenate(parts, axis=0)


def _conv_full(gathered_flat, ccols):
    return gathered_flat.reshape(N_DEV, CONV_K, ccols // N_DEV).transpose(1, 0, 2).reshape(CONV_K, ccols)


def kernel(x, norm_w, w_in, ssd_conv_w, ssd_conv_b, ssd_dt_bias, ssd_a_log, ssd_d, ssd_norm_w, gdn_conv_w, gdn_dt_bias, gdn_a_log, gdn_norm_w, w_out, final_norm_w, loss_target, m_norm_w, m_w_in, m_ssd_conv_w, m_ssd_conv_b, m_ssd_dt_bias, m_ssd_a_log, m_ssd_d, m_ssd_norm_w, m_gdn_conv_w, m_gdn_dt_bias, m_gdn_a_log, m_gdn_norm_w, m_w_out, m_final_norm_w, v_norm_w, v_w_in, v_ssd_conv_w, v_ssd_conv_b, v_ssd_dt_bias, v_ssd_a_log, v_ssd_d, v_ssd_norm_w, v_gdn_conv_w, v_gdn_dt_bias, v_gdn_a_log, v_gdn_norm_w, v_w_out, v_final_norm_w):
    f32 = jnp.float32
    w = dict(norm_w=norm_w, w_in=w_in, ssd_conv_w=ssd_conv_w, ssd_conv_b=ssd_conv_b, ssd_dt_bias=ssd_dt_bias,
             ssd_a_log=ssd_a_log, ssd_d=ssd_d, ssd_norm_w=ssd_norm_w, gdn_conv_w=gdn_conv_w, gdn_dt_bias=gdn_dt_bias,
             gdn_a_log=gdn_a_log, gdn_norm_w=gdn_norm_w, w_out=w_out, final_norm_w=final_norm_w)
    m = dict(norm_w=m_norm_w, w_in=m_w_in, ssd_conv_w=m_ssd_conv_w, ssd_conv_b=m_ssd_conv_b, ssd_dt_bias=m_ssd_dt_bias,
             ssd_a_log=m_ssd_a_log, ssd_d=m_ssd_d, ssd_norm_w=m_ssd_norm_w, gdn_conv_w=m_gdn_conv_w,
             gdn_dt_bias=m_gdn_dt_bias, gdn_a_log=m_gdn_a_log, gdn_norm_w=m_gdn_norm_w, w_out=m_w_out,
             final_norm_w=m_final_norm_w)
    v = dict(norm_w=v_norm_w, w_in=v_w_in, ssd_conv_w=v_ssd_conv_w, ssd_conv_b=v_ssd_conv_b, ssd_dt_bias=v_ssd_dt_bias,
             ssd_a_log=v_ssd_a_log, ssd_d=v_ssd_d, ssd_norm_w=v_ssd_norm_w, gdn_conv_w=v_gdn_conv_w,
             gdn_dt_bias=v_gdn_dt_bias, gdn_a_log=v_gdn_a_log, gdn_norm_w=v_gdn_norm_w, w_out=v_w_out,
             final_norm_w=v_final_norm_w)
    names = list(w)
    shapes = {n: w[n].shape for n in names}

    xl, tgt = x[0], loss_target[0]
    cs = _consts()
    dtb_s = _pad_lanes(ssd_dt_bias, 0)
    alog_s = _pad_lanes(ssd_a_log, 0)
    dpar = _pad_lanes(ssd_d, 0)
    dtb_g = _pad_lanes(gdn_dt_bias, 16)
    alog_g = _pad_lanes(gdn_a_log, 16)
    nw_g = gdn_norm_w.reshape(1, 128)
    nw_s = ssd_norm_w.reshape(1, 1024)
    cb_s = ssd_conv_b.reshape(1, 1536)
    nw1 = norm_w.reshape(1, D_MODEL)

    (g_w_in,) = all_gather([w_in[0].T.astype(_MM)], "gather_w_in")
    w_perm = g_w_in.reshape(IN_DIM, D_MODEL)
    conv_pack = _pack([w["ssd_conv_w"], w["gdn_conv_w"]], SHARD, SHARD_ROWS)
    u, z, xbc, gate, qkv, sm, g_w_out, g_conv = inproj_fwd(xl, nw1, w_perm, [w_out[0].astype(_MM), conv_pack])
    w_out_full = g_w_out.reshape(MIX_WIDTH, D_MODEL)
    ssd_cw = _conv_full(g_conv[:, 0:6].reshape(N_DEV, -1), SSD_CONV_DIM)
    gdn_cw = _conv_full(g_conv[:, 6:18].reshape(N_DEV, -1), GDN_CONV_DIM)

    nc = xl.shape[0] // CHUNK
    y_ssd, hs, pre_s, y_gdn, ss, ts, pre_g = _chunk_call(
        [ssd_fwd(z, xbc, sm, ssd_cw, cb_s, dtb_s, alog_s, dpar, nw_s, cs),
         gdn_fwd(gate, qkv, sm, gdn_cw, dtb_g, alog_g, nw_g, cs)], "scan_fwd", nc, False)
    dout, dys, dyg, g_wout, g_fnw, loss_l = out_fwd_bwd(xl, tgt, y_ssd, y_gdn, w_out_full,
                                                        final_norm_w.reshape(1, D_MODEL))
    (dz, dxbc, g_cw_s, g_cb_s, g_dtb_s, g_alog_s, g_d, g_nw_s,
     dgate, dqkv, dsm, g_cw_g, g_dtb_g, g_alog_g, g_nw_g) = _chunk_call(
        [ssd_bwd(z, xbc, pre_s, sm, hs, dys, ssd_cw, dtb_s, alog_s, dpar, nw_s, cs),
         gdn_bwd(gate, qkv, pre_g, sm, ss, ts, dyg, gdn_cw, dtb_g, alog_g, nw_g, cs)], "scan_bwd", nc, True)

    t_w_out = g_wout.reshape(N_DEV, MIX_WIDTH // N_DEV, D_MODEL)
    gws = dict(zip(("z", "sm"), grad_w_many(u, [dz, dsm], "grad_w_in_z_sm")))
    gws["xbc"] = grad_w_group(u, dxbc, "grad_w_in_xbc")
    gws["gate"] = grad_w_group(u, dgate, "grad_w_in_gate")
    gws["qkv"], r_w_out = grad_w_group(u, dqkv, "grad_w_in_qkv", [t_w_out])
    t_w_in = unperm_w_in(gws["z"], gws["xbc"], gws["gate"], gws["qkv"], gws["sm"])
    dx, g_nw, r_w_in = inproj_bwd_dx(xl, dout, nw1, w_perm, (dz, dxbc, dgate, dqkv, dsm), [t_w_in])

    accs = dict(norm_w=g_nw, ssd_conv_b=g_cb_s, ssd_dt_bias=g_dtb_s, ssd_a_log=g_alog_s, ssd_d=g_d,
                ssd_norm_w=g_nw_s, gdn_dt_bias=g_dtb_g, gdn_a_log=g_alog_g, gdn_norm_w=g_nw_g, final_norm_w=g_fnw,
                ssd_conv_w=g_cw_s, gdn_conv_w=g_cw_g)
    r_small = scatter_small([accs[e[0]] for e in SMALL] + [loss_l])

    o_w_in = adamw_sum(r_w_in, w_in[0].T, m_w_in[0].T, v_w_in[0].T, None, "adamw_w_in", cols=256)
    o_w_out = adamw_sum(r_w_out, w_out[0], m_w_out[0], v_w_out[0], 64, "adamw_w_out")
    row = lambda d: {n: (a.reshape(1, -1) if a.ndim == 1 else a) for n, a in d.items()}
    o_small, loss_sum = adamw_small(r_small, row(w), row(m), row(v))

    loss = loss_sum[0, 0]
    outs = [loss, dx[None]]
    for k in range(4):
        parts = {n: o_small[n][k] for n in o_small}
        parts["w_in"] = o_w_in[k].T
        parts["w_out"] = o_w_out[k]
        outs += [parts[n].reshape(shapes[n]) for n in names]
    return tuple(outs)
```

```python
import functools

import jax
import jax.numpy as jnp
import numpy as np
from jax import lax
from jax.experimental import pallas as pl
from jax.experimental.pallas import tpu as pltpu

_MM = jnp.bfloat16

D_MODEL = 1024
CHUNK = 64
CONV_K = 4
EPS = 1e-6
SSD_CONV_DIM = 1536
GDN_HEADS = 8
GDN_DK = 128
GDN_CONV_DIM = 3072
MIX_WIDTH = 2048
IN_DIM = 6688
N_DEV = 8
W_IN_SHARD = IN_DIM // N_DEV
HI = lax.Precision.HIGHEST
HIGH = lax.Precision.HIGH
VMEM_LIMIT = 56 * 1024 * 1024

ADAM_LR = 0.001
ADAM_B1 = 0.9
ADAM_B2 = 0.999
ADAM_EPS = 1e-08
ADAM_WD = 0.01
ADAM_STEP = 10


def _pc(body, **kw):
    return pl.pallas_call(body, **kw)


def _pc_comm(body, **kw):
    return pl.pallas_call(body, **kw)


def _cparams(sem):
    return pltpu.CompilerParams(dimension_semantics=sem, vmem_limit_bytes=VMEM_LIMIT)


def _sig(x):
    return 0.5 * jnp.tanh(0.5 * x) + 0.5


@jax.custom_vjp
def _sigmoid(x):
    return _sig(x)


def _sigmoid_fwd(x):
    s = _sig(x)
    return s, s


def _sigmoid_bwd(s, g):
    return (g * s * (1.0 - s),)


_sigmoid.defvjp(_sigmoid_fwd, _sigmoid_bwd)


@jax.custom_vjp
def _silu(x):
    return x * _sig(x)


def _silu_fwd(x):
    s = _sig(x)
    return x * s, (x, s)


def _silu_bwd(res, g):
    x, s = res
    return (g * (s * (1.0 + x * (1.0 - s))),)


_silu.defvjp(_silu_fwd, _silu_bwd)


def _softplus_impl(x):
    return jnp.maximum(x, 0.0) + jnp.log(1.0 + jnp.exp(-jnp.abs(x)))


@jax.custom_vjp
def _softplus(x):
    return _softplus_impl(x)


def _softplus_fwd(x):
    return _softplus_impl(x), x


def _softplus_bwd(x, g):
    return (g * _sig(x),)


_softplus.defvjp(_softplus_fwd, _softplus_bwd)


def _lane_bcast_impl(x, k):
    return jnp.broadcast_to(x[..., k:k + 1], x.shape)


@functools.partial(jax.custom_vjp, nondiff_argnums=(1,))
def _lane_bcast(x, k):
    return _lane_bcast_impl(x, k)


def _lane_bcast_fwd(x, k):
    return _lane_bcast_impl(x, k), None


def _lane_bcast_bwd(k, _, g):
    lane = lax.broadcasted_iota(jnp.int32, g.shape, g.ndim - 1)
    return (jnp.where(lane == k, jnp.sum(g, axis=-1, keepdims=True), 0.0),)


_lane_bcast.defvjp(_lane_bcast_fwd, _lane_bcast_bwd)


def _mm(a, b):
    return jnp.dot(a.astype(_MM), b.astype(_MM), preferred_element_type=jnp.float32)


def _mm_nt(a, b):
    return lax.dot_general(a.astype(_MM), b.astype(_MM), (((1,), (1,)), ((), ())),
                           preferred_element_type=jnp.float32)


def _mm_tn(a, b):
    return lax.dot_general(a.astype(_MM), b.astype(_MM), (((0,), (0,)), ((), ())),
                           preferred_element_type=jnp.float32)


def _dot_hi(a, b):
    return jnp.dot(a, b, precision=HI, preferred_element_type=jnp.float32)


def _bmm(a, b):
    return lax.dot_general(a.astype(_MM), b.astype(_MM), (((2,), (1,)), ((0,), (0,))),
                           preferred_element_type=jnp.float32)


def _bmm_nt(a, b):
    return lax.dot_general(a.astype(_MM), b.astype(_MM), (((2,), (2,)), ((0,), (0,))),
                           preferred_element_type=jnp.float32)


def _bmm_tn(a, b):
    return lax.dot_general(a.astype(_MM), b.astype(_MM), (((1,), (1,)), ((0,), (0,))),
                           preferred_element_type=jnp.float32)


def _bmm_hi(a, b):
    return lax.dot_general(a, b, (((2,), (1,)), ((0,), (0,))), precision=HIGH, preferred_element_type=jnp.float32)


def _bmm_nt_hi(a, b):
    return lax.dot_general(a, b, (((2,), (2,)), ((0,), (0,))), precision=HIGH, preferred_element_type=jnp.float32)


def _bmm_tn_hi(a, b):
    return lax.dot_general(a, b, (((1,), (1,)), ((0,), (0,))), precision=HIGH, preferred_element_type=jnp.float32)


def _consts():
    l = np.arange(CHUNK)
    tri = (l[:, None] >= l[None, :]).astype(np.float32)
    lane = np.arange(128)
    i2 =(l[:, None] == (lane[None, :] % 64)).astype(np.float32)
    mask2 = (l[:, None] >= (lane[None, :] % 64)).astype(np.float32)
    lo = (lane < 64).astype(np.float32)[None, :]
    i64 = np.eye(CHUNK, dtype=np.float32)
    strict = (l[:, None] > l[None, :]).astype(np.float32)
    return dict(tri=jnp.asarray(tri), i2=jnp.asarray(i2), mask2=jnp.asarray(mask2), lo=jnp.asarray(lo),
                i64=jnp.asarray(i64), strict=jnp.asarray(strict))


def _ssd_chunk(xs_pre, b_pre, c_pre, z, sm, ht, dtb, alog, dpar, nw, tri, i2, mask2, lo):
    lane = lax.broadcasted_iota(jnp.int32, (1, 128), 1)
    m16 = lane < 16
    dt = jnp.where(m16, _softplus(sm + dtb), 0.0)
    a_neg = -jnp.exp(alog)
    cum = _dot_hi(tri, dt * a_neg)
    row = lax.broadcasted_iota(jnp.int32, (CHUNK, 1), 0)
    is_last = row == CHUNK - 1
    hi = 1.0 - lo
    bm = [_silu(b) for b in b_pre]
    cm = [_silu(c) for c in c_pre]
    cb2 = [_mm_nt(cm[g], jnp.concatenate([bm[g], bm[g]], axis=0)) for g in range(2)]
    ht_g = [jnp.concatenate(ht[4 * g:4 * g + 4], axis=1) for g in range(2)]
    yoff_g = [_mm(cm[g], ht_g[g]) for g in range(2)]
    yg, xdec, clast = [], [], []
    for j in range(8):
        g, k4 = j // 4, j % 4
        pair = lambda v, j=j: jnp.where(lo > 0.5, _lane_bcast(v, 2 * j), _lane_bcast(v, 2 * j + 1))
        xs = _silu(xs_pre[j])
        dte = pair(dt)
        cume = pair(cum)
        cum_last = jnp.sum(jnp.where(is_last, cume, 0.0), axis=0, keepdims=True)
        xdt = xs * dte
        rowv = jnp.sum(cume * i2, axis=0, keepdims=True)
        lm = jnp.exp(jnp.where(mask2 > 0.5, cume - rowv, -jnp.inf))
        m = cb2[g] * lm
        xblk = jnp.concatenate([xdt * lo, xdt * hi], axis=0)
        y = _mm(m, xblk)
        y = y + yoff_g[g][:, 128 * k4:128 * k4 + 128] * jnp.exp(cume)
        y = y + pair(dpar) * xs
        yg.append(y * _silu(z[j]))
        xdec.append(xdt * jnp.exp(cum_last - cume))
        clast.append(cum_last)
    ht_next = []
    for g in range(2):
        st = _mm_tn(bm[g], jnp.concatenate(xdec[4 * g:4 * g + 4], axis=1))
        for k4 in range(4):
            j = 4 * g + k4
            ht_next.append(ht[j] * jnp.exp(clast[j]) + st[:, 128 * k4:128 * k4 + 128])
    outs = []
    for g in range(2):
        ss = sum(jnp.sum(yg[j] * yg[j], axis=-1, keepdims=True) for j in range(4 * g, 4 * g + 4))
        rs = lax.rsqrt(ss * (1.0 / 512.0) + EPS)
        for j in range(4 * g, 4 * g + 4):
            outs.append(yg[j] * rs * nw[j])
    return outs, ht_next


def _tri_inverse(a):
    eye = jnp.eye(CHUNK, dtype=jnp.float32)[None]
    p = eye - a
    x = _bmm_hi(a, a)
    for i in range(4):
        both = (_bmm_hi if i == 0 else _bmm)(jnp.concatenate([p, x], axis=1), x)
        p = p + both[:, :CHUNK]
        x = both[:, CHUNK:]
    return p + _bmm(p, x)


def _solve_apply(t, r1, r2):
    both = _bmm_hi(t, jnp.concatenate([r1, r2], axis=-1))
    n = r1.shape[-1]
    return both[..., :n], both[..., n:]


@jax.custom_vjp
def _solve(a, r1, r2, t):
    return _solve_apply(t, r1, r2)


def _solve_fwd(a, r1, r2, t):
    u, w = _bmm_hi(t, r1), _bmm_hi(t, r2)
    return (u, w), (t, u, w)


def _solve_bwd(res, cts):
    t, u, w = res
    du, dw = cts
    dr1 = _bmm_tn_hi(t, du)
    dr2 = _bmm_tn_hi(t, dw)
    da = -(_bmm_nt_hi(dr1, u) + _bmm_nt_hi(dr2, w))
    return da, dr1, dr2, jnp.zeros_like(t)


_solve.defvjp(_solve_fwd, _solve_bwd)


def _gdn_chunk(q_pre, k_pre, v_pre, gate, sm, s, dtb, alog, nw, tri, i64, strict, t_in=None):
    lane = lax.broadcasted_iota(jnp.int32, (1, 128), 1)
    m_a = (lane >= 16) & (lane < 24)
    g_full = jnp.where(m_a, -jnp.exp(alog) * _softplus(sm + dtb), 0.0)
    gc = _dot_hi(tri, g_full)
    sig = _sigmoid(sm)
    heads = lambda f: jnp.concatenate([f(h)[None] for h in range(GDN_HEADS)], axis=0)
    gc3 = heads(lambda h: _lane_bcast(gc, 16 + h))
    beta3 = heads(lambda h: _lane_bcast(sig, 24 + h))
    q = _silu(q_pre)
    q = q * lax.rsqrt(jnp.sum(q * q, axis=-1, keepdims=True) + EPS) * (GDN_DK ** -0.5)
    k = _silu(k_pre)
    k = k * lax.rsqrt(jnp.sum(k * k, axis=-1, keepdims=True) + EPS)
    v = _silu(v_pre)
    gcl = gc3[:, :, :CHUNK]
    gc_row = jnp.sum(gcl * i64[None], axis=1, keepdims=True)
    incl = (strict + i64)[None] > 0.5
    decay = jnp.exp(jnp.where(incl, gcl - gc_row, -jnp.inf))
    kb = k * beta3
    a = jnp.where(strict[None] > 0.5, _bmm_nt(kb, k) * decay, 0.0)
    egc = jnp.exp(gc3)
    t = _tri_inverse(a) if t_in is None else t_in
    u, w = _solve(a, v * beta3, kb * egc, t)
    attn = _bmm_nt(q, k) * decay
    row = lax.broadcasted_iota(jnp.int32, (1, CHUNK, 1), 1)
    gl = jnp.sum(jnp.where(row == CHUNK - 1, gc3, 0.0), axis=1, keepdims=True)
    q_dec = q * egc
    k_dec = k * jnp.exp(gl - gc3)
    ws = _bmm(jnp.concatenate([w, q_dec], axis=1), s)
    v_new = u - ws[:, :CHUNK]
    o = ws[:, CHUNK:] + _bmm(attn, v_new)
    s_next = s * jnp.exp(gl) + _bmm_tn(k_dec, v_new)
    on = o * lax.rsqrt(jnp.mean(o * o, axis=-1, keepdims=True) + EPS) * nw
    return on * _silu(gate), s_next, t


def _conv_fwd(pbuf, w_ref, c0, c1):
    blk = pbuf[:, c0:c1]
    acc = w_ref[CONV_K - 1:CONV_K, c0:c1] * blk[8:72]
    for j in range(CONV_K - 1):
        acc = acc + w_ref[j:j + 1, c0:c1] * pltpu.roll(blk, CONV_K - 1 - j, axis=0)[8:72]
    return acc


MESH = pl.DeviceIdType.MESH
ANY = pl.BlockSpec(memory_space=pl.ANY)


def _me():
    x, y, c = lax.axis_index("x"), lax.axis_index("y"), lax.axis_index("c")
    return x, y, c, 4 * x + 2 * y + c


def _peer(r):
    x, y, c, _ = _me()
    px = 1 - x if r & 4 else x
    py = 1 - y if r & 2 else y
    pc = 1 - c if r & 1 else c
    return (px, py, pc), 4 * px + 2 * py + pc


def _exchange_ops(kind, in_ref, out_ref, send_sems, recv_sems, local_sem):
    me = _me()[3]
    local = pltpu.make_async_copy(in_ref.at[me] if kind == "scatter" else in_ref, out_ref.at[me], local_sem)
    sends, recvs = [], []
    for r in range(1, N_DEV):
        peer, pidx = _peer(r)
        src = in_ref.at[pidx] if kind == "scatter" else in_ref
        sems = dict(send_sem=send_sems.at[r - 1], recv_sem=recv_sems.at[r - 1], device_id=peer, device_id_type=MESH)
        sends.append(pltpu.make_async_remote_copy(src_ref=src, dst_ref=out_ref.at[me], **sems))
        recvs.append(pltpu.make_async_remote_copy(src_ref=src, dst_ref=out_ref.at[pidx], **sems))

    def start():
        local.start()
        for cp in sends:
            cp.start()

    def wait():
        for cp in recvs:
            cp.wait_recv()
        for cp in sends:
            cp.wait_send()
        local.wait()

    return start, wait


def _exchange_sems(n):
    return [pltpu.SemaphoreType.DMA((N_DEV - 1,)), pltpu.SemaphoreType.DMA((N_DEV - 1,)),
            pltpu.SemaphoreType.DMA(())] * n


def _exchange_out_shape(kind, a):
    return jax.ShapeDtypeStruct(a.shape if kind == "scatter" else (N_DEV,) + a.shape, a.dtype)


def _hosting(body, n_in, n_out, n_scratch, kinds, first, last):
    ne = len(kinds)

    def wrapped(*refs):
        ins, ex_in = refs[:n_in], refs[n_in:n_in + ne]
        o0 = n_in + ne
        outs, ex_out = refs[o0:o0 + n_out], refs[o0 + n_out:o0 + n_out + ne]
        s0 = o0 + n_out + ne
        scr, sems = refs[s0:s0 + n_scratch], refs[s0 + n_scratch:]
        ops = [_exchange_ops(kinds[e], ex_in[e], ex_out[e], *sems[3 * e:3 * e + 3]) for e in range(ne)]

        @pl.when(first())
        def _():
            for start, _ in ops:
                start()

        body(*ins, *outs, *scr)

        @pl.when(last())
        def _():
            for _, wait in ops:
                wait()

    return wrapped


GROUPS = (("z", 0, 1024), ("xbc", 1024, 2560), ("gate", 2560, 3584), ("qkv", 3584, 6656), ("sm", 6656, 6784))
GROUP_ROWS = dict(z=((0, 1024),), xbc=((1024, 2560),), gate=((2576, 3600),), qkv=((3600, 6672),),
                  sm=((2560, 2576), (6672, 6688)))


W_SLOT = 848
W_GATHERED = N_DEV * W_SLOT


def _slot_pos(s, c):
    return W_SLOT * s + (c - W_IN_SHARD * s) + (W_IN_SHARD * s) % 16


def _w_pieces(w_ref, a, b):
    out, cur = [], a
    while cur < b:
        s = cur // W_IN_SHARD
        end = W_IN_SHARD * (s + 1)
        if end >= b:
            out.append(w_ref[_slot_pos(s, cur):_slot_pos(s, b), :])
            break
        hi = end // 16 * 16
        if hi > cur:
            out.append(w_ref[_slot_pos(s, cur):_slot_pos(s, hi), :])
        if end % 16:
            p, q = _slot_pos(s, hi), W_SLOT * (s + 1)
            out.append(w_ref[p:p + 16, :] + w_ref[q:q + 16, :])
            cur = hi + 16
        else:
            cur = hi
    return out


def _w_rows(w_ref, name, width):
    pieces = [p for a, b in GROUP_ROWS[name] for p in _w_pieces(w_ref, a, b)]
    n = sum(b - a for a, b in GROUP_ROWS[name])
    if n < width:
        pieces.append(jnp.zeros((width - n, D_MODEL), w_ref.dtype))
    return pieces[0] if len(pieces) == 1 else jnp.concatenate(pieces, axis=0)


def inproj_fwd(x, norm_w, w_perm, gathered):
    t = x.shape[0]
    tm = min(512, t)
    steps = t // tm
    kinds = ["gather"] * len(gathered)

    def body(x_ref, nw_ref, w_ref, u_ref, z_ref, xbc_ref, gate_ref, qkv_ref, sm_ref):
        xf = x_ref[...]
        rstd = lax.rsqrt(jnp.mean(xf * xf, axis=-1, keepdims=True) + EPS)
        u = (xf * rstd * nw_ref[...]).astype(_MM)
        u_ref[...] = u
        for (name, c0, c1), o_ref in zip(GROUPS, (z_ref, xbc_ref, gate_ref, qkv_ref, sm_ref)):
            o_ref[...] = lax.dot_general(u, _w_rows(w_ref, name, c1 - c0), (((1,), (1,)), ((), ())),
                                         preferred_element_type=jnp.float32)

    outs = [jax.ShapeDtypeStruct((t, D_MODEL), _MM)] + [jax.ShapeDtypeStruct((t, c1 - c0), jnp.float32)
                                                        for _, c0, c1 in GROUPS]
    hosted = _hosting(body, 3, 6, 0, kinds, lambda: pl.program_id(0) == 0, lambda: pl.program_id(0) == steps - 1)
    return _pc_comm(
        hosted, name="inproj_fwd", grid=(steps,),
        in_specs=[pl.BlockSpec((tm, D_MODEL), lambda i: (i, 0)),
                  pl.BlockSpec((1, D_MODEL), lambda i: (0, 0)),
                  pl.BlockSpec((W_GATHERED, D_MODEL), lambda i: (0, 0), pipeline_mode=pl.Buffered(1))] +
                 [ANY] * len(gathered),
        out_specs=[pl.BlockSpec((tm, D_MODEL), lambda i: (i, 0))] +
                  [pl.BlockSpec((tm, c1 - c0), lambda i: (i, 0)) for _, c0, c1 in GROUPS] + [ANY] * len(gathered),
        out_shape=outs + [_exchange_out_shape("gather", a) for a in gathered],
        scratch_shapes=_exchange_sems(len(gathered)), compiler_params=_cparams(("arbitrary",)),
    )(x, norm_w, w_perm, *gathered)


SUB_FWD = 4
SUB_BWD = 2


def _halo_spec(width, idx_fn):
    return pl.BlockSpec((8, width), lambda i: (jnp.maximum(idx_fn(i) * (SUB_FWD * CHUNK // 8) - 1, 0), 0))


def _when_first(shared, fn):
    if shared["first"] is not False:
        pl.when(shared["first"])(fn)


def _full(shape):
    nd = len(shape)
    return pl.BlockSpec(shape, lambda i: (0,) * nd)


def _ssd_split(pre_fn, z_ref, sm_ref):
    xs_pre = [pre_fn(128 * j, 128 * j + 128) for j in range(8)]
    b_pre = [pre_fn(1024 + 128 * g, 1152 + 128 * g) for g in range(2)]
    c_pre = [pre_fn(1280 + 128 * g, 1408 + 128 * g) for g in range(2)]
    z = [z_ref[:, 128 * j:128 * j + 128] for j in range(8)]
    return xs_pre, b_pre, c_pre, z, sm_ref[...]


def ssd_fwd(z, xbc, sm, conv_w, conv_b, dtb, alog, dpar, nw, cs):
    t = z.shape[0]
    nc = t // CHUNK

    def body(shared, z_ref, xbc_ref, halo_ref, sm_ref, cw_ref, cb_ref, dtb_ref, alog_ref, dpar_ref, nw_ref,
             tri_ref, i2_ref, mask2_ref, lo_ref, y_ref, hs_ref, pre_ref, pbuf, ht_scr):
        def init():
            ht_scr[...] = jnp.zeros_like(ht_scr)

        _when_first(shared, init)
        pbuf[0:8, :] = jnp.where(shared["first"], 0.0, halo_ref[...])
        pbuf[8:72, :] = xbc_ref[...]

        def pre_fn(c0, c1):
            pre = _conv_fwd(pbuf, cw_ref, c0, c1) + cb_ref[:, c0:c1]
            pre_ref[:, c0:c1] = pre
            return pre

        xs_pre, b_pre, c_pre, zz, smv = _ssd_split(pre_fn, z_ref, sm_ref)
        ht = [ht_scr[:, 128 * j:128 * j + 128] for j in range(8)]
        hs_ref[0] = ht_scr[...]
        nwl = [nw_ref[:, 128 * j:128 * j + 128] for j in range(8)]
        outs, ht_next = _ssd_chunk(xs_pre, b_pre, c_pre, zz, smv, ht, dtb_ref[...], alog_ref[...], dpar_ref[...],
                                   nwl, tri_ref[...], i2_ref[...], mask2_ref[...], lo_ref[...])
        for j in range(8):
            y_ref[:, 128 * j:128 * j + 128] = outs[j].astype(y_ref.dtype)
            ht_scr[:, 128 * j:128 * j + 128] = ht_next[j]

    blk = lambda w: pl.BlockSpec((SUB_FWD * CHUNK, w), lambda i: (i, 0))
    return dict(
        body=body,
        in_kinds=["rows", "rows", ("halo", 1), "rows"] + ["full"] * 10, out_kinds=["rows", "state", "rows"],
        in_specs=[blk(1024), blk(1536), _halo_spec(1536, lambda i: i), blk(128),
                  _full((CONV_K, 1536)), _full((1, 1536)), _full((1, 128)), _full((1, 128)), _full((1, 128)),
                  _full((1, 1024)), _full((64, 64)), _full((64, 128)), _full((64, 128)),
                  _full((1, 128))],
        out_specs=[blk(1024), pl.BlockSpec((SUB_FWD, 128, 1024), lambda i: (i, 0, 0)), blk(1536)],
        out_shape=[jax.ShapeDtypeStruct((t, 1024), _MM), jax.ShapeDtypeStruct((nc, 128, 1024), jnp.float32),
                   jax.ShapeDtypeStruct((t, 1536), jnp.float32)],
        scratch=[pltpu.VMEM((72, 1536), jnp.float32), pltpu.VMEM((128, 1024), jnp.float32)],
        args=[z, xbc, xbc, sm, conv_w, conv_b, dtb, alog, dpar, nw, cs["tri"], cs["i2"], cs["mask2"], cs["lo"]])


def _conv_bwd(dpre_list, col_ranges, dbuf, carry, x_ref, cw_ref, dx_ref, dcw_ref, dcb_ref, first):
    for dpre, (c0, c1) in zip(dpre_list, col_ranges):
        dbuf[0:64, c0:c1] = dpre
    dbuf[64:72, :] = jnp.where(first, 0.0, carry[...])
    carry[...] = dbuf[0:8, :]
    for (c0, c1) in col_ranges:
        xin = x_ref[:, c0:c1]
        blk = dbuf[:, c0:c1]
        acc = None
        for j in range(CONV_K):
            sh = blk[0:64] if j == CONV_K - 1 else pltpu.roll(blk, 72 - (CONV_K - 1 - j), axis=0)[0:64]
            term = cw_ref[j:j + 1, c0:c1] * sh
            acc = term if acc is None else acc + term
            dcw_ref[j:j + 1, c0:c1] += jnp.sum(xin * sh, axis=0, keepdims=True)
        dx_ref[:, c0:c1] = acc.astype(dx_ref.dtype)
        if dcb_ref is not None:
            dcb_ref[0:1, c0:c1] += jnp.sum(dbuf[0:64, c0:c1], axis=0, keepdims=True)


def ssd_bwd(z, xbc, pre, sm, hs, dy, conv_w, dtb, alog, dpar, nw, cs):
    t = z.shape[0]
    nc = t // CHUNK

    def body(shared, z_ref, xbc_ref, pre_ref, sm_ref, hs_ref, dy_ref, cw_ref, dtb_ref, alog_ref, dpar_ref, nw_ref,
             tri_ref, i2_ref, mask2_ref, lo_ref,
             dz_ref, dxbc_ref, dcw_ref, dcb_ref, ddtb_ref, dalog_ref, ddpar_ref, dnw_ref,
             dbuf, carry, dht_scr):
        def init():
            dht_scr[...] = jnp.zeros_like(dht_scr)
            dcw_ref[...] = jnp.zeros_like(dcw_ref)
            dcb_ref[...] = jnp.zeros_like(dcb_ref)
            ddtb_ref[...] = jnp.zeros_like(ddtb_ref)
            dalog_ref[...] = jnp.zeros_like(dalog_ref)
            ddpar_ref[...] = jnp.zeros_like(ddpar_ref)
            dnw_ref[...] = jnp.zeros_like(dnw_ref)

        _when_first(shared, init)
        pre_fn = lambda c0, c1: pre_ref[:, c0:c1]
        xs_pre, b_pre, c_pre, zz, smv = _ssd_split(pre_fn, z_ref, sm_ref)
        ht = [hs_ref[0, :, 128 * j:128 * j + 128] for j in range(8)]
        nwl = [nw_ref[:, 128 * j:128 * j + 128] for j in range(8)]
        consts = (tri_ref[...], i2_ref[...], mask2_ref[...], lo_ref[...])

        def f(xs_pre, b_pre, c_pre, zz, smv, ht, dtb, alog, dpar, nwl):
            return _ssd_chunk(xs_pre, b_pre, c_pre, zz, smv, ht, dtb, alog, dpar, nwl, *consts)

        _, vjp = jax.vjp(f, xs_pre, b_pre, c_pre, zz, smv, ht, dtb_ref[...], alog_ref[...], dpar_ref[...], nwl)
        dys = [dy_ref[:, 128 * j:128 * j + 128] for j in range(8)]
        dhts = [dht_scr[:, 128 * j:128 * j + 128] for j in range(8)]
        dxs, db, dc, dzz, dsm, dht, ddtb, dalog, ddpar, dnwl = vjp((dys, dhts))
        for j in range(8):
            dz_ref[:, 128 * j:128 * j + 128] = dzz[j].astype(dz_ref.dtype)
            dht_scr[:, 128 * j:128 * j + 128] = dht[j]
            dnw_ref[0:1, 128 * j:128 * j + 128] += dnwl[j]
        shared["dsm_ssd"] = dsm
        ddtb_ref[0:1, :] += ddtb
        dalog_ref[0:1, :] += dalog
        ddpar_ref[0:1, :] += ddpar
        ranges = ([(128 * j, 128 * j + 128) for j in range(8)] + [(1024 + 128 * g, 1152 + 128 * g) for g in range(2)]
                  + [(1280 + 128 * g, 1408 + 128 * g) for g in range(2)])
        _conv_bwd(dxs + db + dc, ranges, dbuf, carry, xbc_ref, cw_ref, dxbc_ref, dcw_ref, dcb_ref, shared["first"])

    ns = nc // SUB_BWD
    rblk = lambda w: pl.BlockSpec((SUB_BWD * CHUNK, w), lambda i: (ns - 1 - i, 0))
    acc = lambda w: pl.BlockSpec((8, w), lambda i: (0, 0))
    f32 = jnp.float32
    return dict(
        body=body,
        in_kinds=["rows"] * 4 + ["state", "rows"] + ["full"] * 9, out_kinds=["rows", "rows"] + ["full"] * 6,
        in_specs=[rblk(1024), rblk(1536), rblk(1536), rblk(128),
                  pl.BlockSpec((SUB_BWD, 128, 1024), lambda i: (ns - 1 - i, 0, 0)), rblk(1024),
                  _full((CONV_K, 1536)), _full((1, 128)), _full((1, 128)), _full((1, 128)),
                  _full((1, 1024)), _full((64, 64)), _full((64, 128)), _full((64, 128)),
                  _full((1, 128))],
        out_specs=[rblk(1024), rblk(1536), acc(1536), acc(1536), acc(128), acc(128), acc(128), acc(1024)],
        out_shape=[jax.ShapeDtypeStruct((t, 1024), f32), jax.ShapeDtypeStruct((t, 1536), f32),
                   jax.ShapeDtypeStruct((8, 1536), f32),
                   jax.ShapeDtypeStruct((8, 1536), f32), jax.ShapeDtypeStruct((8, 128), f32),
                   jax.ShapeDtypeStruct((8, 128), f32), jax.ShapeDtypeStruct((8, 128), f32),
                   jax.ShapeDtypeStruct((8, 1024), f32)],
        scratch=[pltpu.VMEM((72, 1536), f32), pltpu.VMEM((8, 1536), f32), pltpu.VMEM((128, 1024), f32)],
        args=[z, xbc, pre, sm, hs, dy, conv_w, dtb, alog, dpar, nw, cs["tri"], cs["i2"], cs["mask2"], cs["lo"]])


def _gdn_split(pre_fn, gate_ref):
    def heads(base):
        return jnp.stack([pre_fn(base + 128 * h, base + 128 * h + 128) for h in range(GDN_HEADS)])
    gate = jnp.stack([gate_ref[:, 128 * h:128 * h + 128] for h in range(GDN_HEADS)])
    return heads(0), heads(1024), heads(2048), gate


def gdn_fwd(gate, qkv, sm, conv_w, dtb, alog, nw, cs):
    t = gate.shape[0]
    nc = t // CHUNK

    def body(shared, gate_ref, qkv_ref, halo_ref, sm_ref, cw_ref, dtb_ref, alog_ref, nw_ref,
             tri_ref, i64_ref, strict_ref, o_ref, ss_ref, ts_ref, pre_ref, pbuf, s_scr):
        def init():
            s_scr[...] = jnp.zeros_like(s_scr)

        _when_first(shared, init)
        pbuf[0:8, :] = jnp.where(shared["first"], 0.0, halo_ref[...])
        pbuf[8:72, :] = qkv_ref[...]

        def pre_fn(c0, c1):
            pre = _conv_fwd(pbuf, cw_ref, c0, c1)
            pre_ref[:, c0:c1] = pre
            return pre

        q_pre, k_pre, v_pre, g3 = _gdn_split(pre_fn, gate_ref)
        s = s_scr[...]
        ss_ref[0] = s
        out, s_next, tinv = _gdn_chunk(q_pre, k_pre, v_pre, g3, sm_ref[...], s, dtb_ref[...], alog_ref[...],
                                       nw_ref[...], tri_ref[...], i64_ref[...], strict_ref[...])
        ts_ref[0] = tinv
        s_scr[...] = s_next
        for h in range(GDN_HEADS):
            o_ref[:, 128 * h:128 * h + 128] = out[h].astype(o_ref.dtype)

    blk = lambda w: pl.BlockSpec((SUB_FWD * CHUNK, w), lambda i: (i, 0))
    return dict(
        body=body,
        in_kinds=["rows", "rows", ("halo", 1), "rows"] + ["full"] * 7, out_kinds=["rows", "state", "state", "rows"],
        in_specs=[blk(1024), blk(3072), _halo_spec(3072, lambda i: i), blk(128),
                  _full((CONV_K, 3072)), _full((1, 128)), _full((1, 128)), _full((1, 128)),
                  _full((64, 64)), _full((64, 64)), _full((64, 64))],
        out_specs=[blk(1024), pl.BlockSpec((SUB_FWD, 8, 128, 128), lambda i: (i, 0, 0, 0)),
                   pl.BlockSpec((SUB_FWD, 8, CHUNK, CHUNK), lambda i: (i, 0, 0, 0)), blk(3072)],
        out_shape=[jax.ShapeDtypeStruct((t, 1024), _MM), jax.ShapeDtypeStruct((nc, 8, 128, 128), jnp.float32),
                   jax.ShapeDtypeStruct((nc, 8, CHUNK, CHUNK), jnp.float32),
                   jax.ShapeDtypeStruct((t, 3072), jnp.float32)],
        scratch=[pltpu.VMEM((72, 3072), jnp.float32), pltpu.VMEM((8, 128, 128), jnp.float32)],
        args=[gate, qkv, qkv, sm, conv_w, dtb, alog, nw, cs["tri"], cs["i64"], cs["strict"]])


def gdn_bwd(gate, qkv, pre, sm, ss, ts, do, conv_w, dtb, alog, nw, cs):
    t = gate.shape[0]
    nc = t // CHUNK

    def body(shared, gate_ref, qkv_ref, pre_ref, sm_ref, ss_ref, ts_ref, do_ref, cw_ref, dtb_ref, alog_ref,
             nw_ref, tri_ref, i64_ref, strict_ref,
             dgate_ref, dqkv_ref, dsm_ref, dcw_ref, ddtb_ref, dalog_ref, dnw_ref,
             dbuf, carry, ds_scr):
        def init():
            ds_scr[...] = jnp.zeros_like(ds_scr)
            dcw_ref[...] = jnp.zeros_like(dcw_ref)
            ddtb_ref[...] = jnp.zeros_like(ddtb_ref)
            dalog_ref[...] = jnp.zeros_like(dalog_ref)
            dnw_ref[...] = jnp.zeros_like(dnw_ref)

        _when_first(shared, init)

        q_pre, k_pre, v_pre, g3 = _gdn_split(lambda c0, c1: pre_ref[:, c0:c1], gate_ref)
        consts = (tri_ref[...], i64_ref[...], strict_ref[...], ts_ref[0])

        def f(q_pre, k_pre, v_pre, g3, smv, s, dtb, alog, nwv):
            return _gdn_chunk(q_pre, k_pre, v_pre, g3, smv, s, dtb, alog, nwv, *consts)[:2]

        _, vjp = jax.vjp(f, q_pre, k_pre, v_pre, g3, sm_ref[...], ss_ref[0], dtb_ref[...], alog_ref[...], nw_ref[...])
        do3 = jnp.stack([do_ref[:, 128 * h:128 * h + 128] for h in range(GDN_HEADS)])
        dq, dk, dv, dg3, dsm, ds, ddtb, dalog, dnw = vjp((do3, ds_scr[...]))
        ds_scr[...] = ds
        for h in range(GDN_HEADS):
            dgate_ref[:, 128 * h:128 * h + 128] = dg3[h].astype(dgate_ref.dtype)
        dsm_ref[...] = (dsm + shared["dsm_ssd"]).astype(dsm_ref.dtype)
        ddtb_ref[0:1, :] += ddtb
        dalog_ref[0:1, :] += dalog
        dnw_ref[0:1, :] += dnw
        ranges = [(base + 128 * h, base + 128 * h + 128) for base in (0, 1024, 2048) for h in range(GDN_HEADS)]
        dlist = [d[h] for d in (dq, dk, dv) for h in range(GDN_HEADS)]
        _conv_bwd(dlist, ranges, dbuf, carry, qkv_ref, cw_ref, dqkv_ref, dcw_ref, None, shared["first"])

    ns = nc // SUB_BWD
    rblk = lambda w: pl.BlockSpec((SUB_BWD * CHUNK, w), lambda i: (ns - 1 - i, 0))
    acc = lambda w: pl.BlockSpec((8, w), lambda i: (0, 0))
    f32 = jnp.float32
    return dict(
        body=body,
        in_kinds=["rows"] * 4 + ["state", "state", "rows"] + ["full"] * 7, out_kinds=["rows"] * 3 + ["full"] * 4,
        in_specs=[rblk(1024), rblk(3072), rblk(3072), rblk(128),
                  pl.BlockSpec((SUB_BWD, 8, 128, 128), lambda i: (ns - 1 - i, 0, 0, 0)),
                  pl.BlockSpec((SUB_BWD, 8, CHUNK, CHUNK), lambda i: (ns - 1 - i, 0, 0, 0)), rblk(1024),
                  _full((CONV_K, 3072)), _full((1, 128)), _full((1, 128)), _full((1, 128)),
                  _full((64, 64)), _full((64, 64)), _full((64, 64))],
        out_specs=[rblk(1024), rblk(3072), rblk(128), acc(3072), acc(128), acc(128), acc(128)],
        out_shape=[jax.ShapeDtypeStruct((t, 1024), f32), jax.ShapeDtypeStruct((t, 3072), f32),
                   jax.ShapeDtypeStruct((t, 128), f32), jax.ShapeDtypeStruct((8, 3072), f32),
                   jax.ShapeDtypeStruct((8, 128), f32), jax.ShapeDtypeStruct((8, 128), f32),
                   jax.ShapeDtypeStruct((8, 128), f32)],
        scratch=[pltpu.VMEM((72, 3072), f32), pltpu.VMEM((8, 3072), f32), pltpu.VMEM((8, 128, 128), f32)],
        args=[gate, qkv, pre, sm, ss, ts, do, conv_w, dtb, alog, nw, cs["tri"], cs["i64"], cs["strict"]])


def _chunk_call(parts, name, nc, reverse):
    n_in = [len(p["args"]) for p in parts]
    n_out = [len(p["out_shape"]) for p in parts]
    n_scr = [len(p["scratch"]) for p in parts]
    sub = SUB_BWD if reverse else SUB_FWD
    order = list(range(sub))[::-1] if reverse else list(range(sub))

    def view(ref, kind, s, refs):
        if kind == "rows":
            return ref.at[pl.ds(CHUNK * s, CHUNK)]
        if kind == "state":
            return ref.at[pl.ds(s, 1)]
        if kind == "full":
            return ref
        src = refs[kind[1]]
        return ref if s == 0 else src.at[pl.ds(CHUNK * s - 8, 8)]

    def body(*refs):
        ins, outs, scr = refs[:sum(n_in)], refs[sum(n_in):sum(n_in) + sum(n_out)], refs[sum(n_in) + sum(n_out):]
        for s in order:
            shared = {"first": (pl.program_id(0) == 0) if s == order[0] else False}
            for k, p in enumerate(parts):
                i0, o0, s0 = sum(n_in[:k]), sum(n_out[:k]), sum(n_scr[:k])
                p_ins = ins[i0:i0 + n_in[k]]
                p["body"](shared,
                          *[view(r, kd, s, p_ins) for r, kd in zip(p_ins, p["in_kinds"])],
                          *[view(r, kd, s, None) for r, kd in zip(outs[o0:o0 + n_out[k]], p["out_kinds"])],
                          *scr[s0:s0 + n_scr[k]])

    cat = lambda key: [v for p in parts for v in p[key]]
    return _pc(body, name=name, grid=(nc // sub,), in_specs=cat("in_specs"), out_specs=cat("out_specs"),
               out_shape=cat("out_shape"), scratch_shapes=cat("scratch"),
               compiler_params=_cparams(("arbitrary",)))(*cat("args"))


def out_fwd_bwd(x, tgt, y_ssd, y_gdn, w_out, fnw):
    t = x.shape[0]
    tm = min(512, t)
    f32 = jnp.float32

    def body(x_ref, tgt_ref, ys_ref, yg_ref, w_ref, fnw_ref,
             dout_ref, dys_ref, dyg_ref, gw_ref, gfnw_ref, loss_ref, gw_acc):
        i = pl.program_id(0)

        @pl.when(i == 0)
        def _():
            gw_acc[...] = jnp.zeros_like(gw_acc)
            gfnw_ref[...] = jnp.zeros_like(gfnw_ref)
            loss_ref[...] = jnp.zeros_like(loss_ref)

        ys = ys_ref[...]
        yg = yg_ref[...]
        out = x_ref[...] + jnp.dot(ys, w_ref[0:1024, :], preferred_element_type=f32) \
            + jnp.dot(yg, w_ref[1024:2048, :], preferred_element_type=f32)
        rstd = lax.rsqrt(jnp.mean(out * out, axis=-1, keepdims=True) + EPS)
        yhat = out * rstd
        fw = fnw_ref[...]
        e = yhat * fw - tgt_ref[...]
        loss_ref[...] += 0.5 * jnp.sum(jnp.sum(e * e, axis=-1, keepdims=True) * (1.0 / D_MODEL), axis=0, keepdims=True)
        dyf = e * (1.0 / D_MODEL)
        gfnw_ref[0:1, :] += jnp.sum(dyf * yhat, axis=0, keepdims=True)
        dyhat = dyf * fw
        dout = rstd * (dyhat - yhat * jnp.mean(dyhat * yhat, axis=-1, keepdims=True))
        dout_ref[...] = dout
        db = dout.astype(_MM)
        dys_ref[...] = lax.dot_general(db, w_ref[0:1024, :], (((1,), (1,)), ((), ())), preferred_element_type=f32)
        dyg_ref[...] = lax.dot_general(db, w_ref[1024:2048, :], (((1,), (1,)), ((), ())), preferred_element_type=f32)
        gw_acc[0:1024, :] += lax.dot_general(ys, db, (((0,), (0,)), ((), ())), preferred_element_type=f32)
        gw_acc[1024:2048, :] += lax.dot_general(yg, db, (((0,), (0,)), ((), ())), preferred_element_type=f32)

        @pl.when(i == steps - 1)
        def _():
            gw_ref[...] = gw_acc[...].astype(gw_ref.dtype)

    steps = t // tm
    blk = pl.BlockSpec((tm, D_MODEL), lambda i: (i, 0))
    return _pc(
        body, name="out_fwd_bwd", grid=(steps,),
        in_specs=[blk, blk, blk, blk, _full((MIX_WIDTH, D_MODEL)), _full((1, D_MODEL))],
        out_specs=[blk, blk, blk, _full((MIX_WIDTH, D_MODEL)), _full((8, D_MODEL)), _full((1, 128))],
        out_shape=[jax.ShapeDtypeStruct((t, D_MODEL), f32)] * 3 +
                  [jax.ShapeDtypeStruct((MIX_WIDTH, D_MODEL), _MM), jax.ShapeDtypeStruct((8, D_MODEL), f32),
                   jax.ShapeDtypeStruct((1, 128), f32)],
        scratch_shapes=[pltpu.VMEM((MIX_WIDTH, D_MODEL), f32)],
        compiler_params=_cparams(("arbitrary",)),
    )(x, tgt, y_ssd, y_gdn, w_out, fnw)


def inproj_bwd_dx(x, dout, norm_w, w_perm, dgroups, scattered):
    t = x.shape[0]
    tm = min(256, t)
    f32 = jnp.float32

    def body(x_ref, dout_ref, nw_ref, w_ref, dz_ref, dxbc_ref, dgate_ref, dqkv_ref, dsm_ref, dx_ref, gnw_ref):
        i = pl.program_id(0)

        @pl.when(i == 0)
        def _():
            gnw_ref[...] = jnp.zeros_like(gnw_ref)

        du = None
        for (name, c0, c1), d_ref in zip(GROUPS, (dz_ref, dxbc_ref, dgate_ref, dqkv_ref, dsm_ref)):
            term = jnp.dot(d_ref[...].astype(_MM), _w_rows(w_ref, name, c1 - c0), preferred_element_type=f32)
            du = term if du is None else du + term
        xf = x_ref[...]
        rstd = lax.rsqrt(jnp.mean(xf * xf, axis=-1, keepdims=True) + EPS)
        xhat = xf * rstd
        gnw_ref[0:1, :] += jnp.sum(du * xhat, axis=0, keepdims=True)
        dxh = du * nw_ref[...]
        dx_ref[...] = dout_ref[...] + rstd * (dxh - xhat * jnp.mean(dxh * xhat, axis=-1, keepdims=True))

    blk = lambda w: pl.BlockSpec((tm, w), lambda i: (i, 0))
    steps = t // tm
    kinds = ["scatter"] * len(scattered)
    hosted = _hosting(body, 9, 2, 0, kinds, lambda: pl.program_id(0) == 0, lambda: pl.program_id(0) == steps - 1)
    return _pc_comm(
        hosted, name="inproj_bwd_dx", grid=(steps,),
        in_specs=[blk(D_MODEL), blk(D_MODEL), _full((1, D_MODEL)), _full((W_GATHERED, D_MODEL))] +
                 [blk(c1 - c0) for _, c0, c1 in GROUPS] + [ANY] * len(scattered),
        out_specs=[blk(D_MODEL), _full((8, D_MODEL))] + [ANY] * len(scattered),
        out_shape=[jax.ShapeDtypeStruct((t, D_MODEL), f32), jax.ShapeDtypeStruct((8, D_MODEL), f32)] +
                  [_exchange_out_shape("scatter", a) for a in scattered],
        scratch_shapes=_exchange_sems(len(scattered)), compiler_params=_cparams(("arbitrary",)),
    )(x, dout, norm_w, w_perm, *dgroups, *scattered)


def grad_w_group(u, dg, name, scattered=()):
    t, n = dg.shape
    tn = n if n <= 1536 else 1024
    budget = 40 * 1024 * 1024
    tm = next((c for c in (4096, 2048, 1024, 512, 256)
               if t % c == 0 and tn * D_MODEL * 4 + 2 * (c * tn * 4 + c * D_MODEL * 2 + tn * D_MODEL * 2) <= budget), t)
    nj, nk = n // tn, t // tm
    f32 = jnp.float32

    def body(u_ref, d_ref, o_ref, acc):
        k = pl.program_id(1)

        @pl.when(k == 0)
        def _():
            acc[...] = jnp.zeros_like(acc)

        acc[...] += lax.dot_general(d_ref[...].astype(_MM), u_ref[...], (((0,), (0,)), ((), ())),
                                    preferred_element_type=f32)

        @pl.when(k == nk - 1)
        def _():
            o_ref[...] = acc[...].astype(o_ref.dtype)

    ne = len(scattered)
    hosted = _hosting(body, 2, 1, 1, ["scatter"] * ne,
                      lambda: (pl.program_id(0) == 0) & (pl.program_id(1) == 0),
                      lambda: (pl.program_id(0) == nj - 1) & (pl.program_id(1) == nk - 1))
    res = (_pc_comm if ne else _pc)(
        hosted, name=name, grid=(nj, nk),
        in_specs=[pl.BlockSpec((tm, D_MODEL), lambda j, k: (k, 0)),
                  pl.BlockSpec((tm, tn), lambda j, k: (k, j))] + [ANY] * ne,
        out_specs=[pl.BlockSpec((tn, D_MODEL), lambda j, k: (j, 0))] + [ANY] * ne,
        out_shape=[jax.ShapeDtypeStruct((n, D_MODEL), _MM)] + [_exchange_out_shape("scatter", a) for a in scattered],
        scratch_shapes=[pltpu.VMEM((tn, D_MODEL), f32)] + _exchange_sems(ne),
        compiler_params=_cparams(("arbitrary", "arbitrary")),
    )(u, dg, *scattered)
    return res if ne else res[0]


def grad_w_many(u, dgs, name):
    t = u.shape[0]
    widths = [d.shape[1] for d in dgs]
    tot, ng = sum(widths), len(dgs)
    f32 = jnp.float32
    budget = 48 * 1024 * 1024
    tm = next((c for c in (2048, 1024, 512, 256)
               if t % c == 0 and tot * D_MODEL * 4 + 2 * (c * tot * 4 + c * D_MODEL * 2 + tot * D_MODEL * 2) <= budget), t)
    nk = t // tm

    def body(*refs):
        u_ref, d_refs, o_refs, accs = refs[0], refs[1:1 + ng], refs[1 + ng:1 + 2 * ng], refs[1 + 2 * ng:]
        k = pl.program_id(0)

        @pl.when(k == 0)
        def _():
            for acc in accs:
                acc[...] = jnp.zeros_like(acc)

        uu = u_ref[...]
        for d_ref, acc in zip(d_refs, accs):
            acc[...] += lax.dot_general(d_ref[...].astype(_MM), uu, (((0,), (0,)), ((), ())),
                                        preferred_element_type=f32)

        @pl.when(k == nk - 1)
        def _():
            for o_ref, acc in zip(o_refs, accs):
                o_ref[...] = acc[...].astype(o_ref.dtype)

    return _pc(
        body, name=name, grid=(nk,),
        in_specs=[pl.BlockSpec((tm, D_MODEL), lambda k: (k, 0))] + [pl.BlockSpec((tm, n), lambda k: (k, 0)) for n in widths],
        out_specs=[pl.BlockSpec((n, D_MODEL), lambda k: (0, 0)) for n in widths],
        out_shape=[jax.ShapeDtypeStruct((n, D_MODEL), _MM) for n in widths],
        scratch_shapes=[pltpu.VMEM((n, D_MODEL), f32) for n in widths],
        compiler_params=_cparams(("arbitrary",)),
    )(u, *dgs)


def _pad_lanes(v, off):
    n = v.shape[-1]
    return jnp.pad(v.reshape(1, n).astype(jnp.float32), ((0, 0), (off, 128 - off - n)))


REF_ROWS = dict(z=(0, 1024), xbc=(1024, 2560), dt=(2560, 2576), gate=(2576, 3600), qkv=(3600, 6672), ab=(6672, 6688))


def unperm_w_in(gz, gxbc, ggate, gqkv, gsm):
    src = dict(z=gz, xbc=gxbc, dt=gsm[0:16], gate=ggate, qkv=gqkv, ab=gsm[16:32])
    slabs = []
    for k in range(N_DEV):
        a, b = k * W_IN_SHARD, (k + 1) * W_IN_SHARD
        parts = []
        for name, (s, e) in REF_ROWS.items():
            lo, hi = max(a, s), min(b, e)
            if lo < hi:
                parts.append(src[name][lo - s:hi - s])
        slabs.append(jnp.concatenate(parts, axis=0))
    return jnp.stack(slabs)


def all_gather(arrs, name):
    n = len(arrs)

    def body(*refs):
        ins, outs = refs[:n], refs[n:2 * n]
        send_sems, recv_sems, local_sems = refs[2 * n:]
        x, y, c, me = _me()
        sibling = (x, y, 1 - c)
        chips = [(1 - x, y), (x, 1 - y), (1 - x, 1 - y)]

        def idx(px, py, pc):
            return 4 * px + 2 * py + pc

        def copy(a, k, block, to, src=None):
            slot = outs[a].at[idx(*block)]
            return pltpu.make_async_remote_copy(src_ref=slot if src is None else src, dst_ref=slot,
                                                send_sem=send_sems.at[a, k], recv_sem=recv_sems.at[a, k],
                                                device_id=to, device_id_type=MESH)

        local = [pltpu.make_async_copy(ins[a], outs[a].at[me], local_sems.at[a]) for a in range(n)]
        for cp in local:
            cp.start()
        started = []
        for a in range(n):
            first = [copy(a, 0, (x, y, c), sibling, src=ins[a])]
            first += [copy(a, 1 + j, (x, y, c), (*chip, c), src=ins[a]) for j, chip in enumerate(chips)]
            for cp in first:
                cp.start()
            started += first
        for a in range(n):
            for j, chip in enumerate(chips):
                copy(a, 1 + j, (*chip, c), (x, y, c)).wait_recv()
                fwd = copy(a, 4 + j, (*chip, c), sibling)
                fwd.start()
                started.append(fwd)
        for a in range(n):
            copy(a, 0, sibling, (x, y, c)).wait_recv()
            for j, chip in enumerate(chips):
                copy(a, 4 + j, (*chip, 1 - c), (x, y, c)).wait_recv()
        for cp in started:
            cp.wait_send()
        for cp in local:
            cp.wait()

    return _pc_comm(
        body, name=name, in_specs=[ANY] * n, out_specs=[ANY] * n,
        out_shape=[jax.ShapeDtypeStruct((N_DEV,) + a.shape, a.dtype) for a in arrs],
        scratch_shapes=[pltpu.SemaphoreType.DMA((n, 7)), pltpu.SemaphoreType.DMA((n, 7)),
                        pltpu.SemaphoreType.DMA((n,))],
    )(*arrs)


def adamw_sum(recv, w, m, v, rows, name, cols=None):
    r, ccols = w.shape
    f32 = jnp.float32
    c1 = 1.0 / (1.0 - ADAM_B1 ** ADAM_STEP)
    c2 = 1.0 / (1.0 - ADAM_B2 ** ADAM_STEP)

    def body(recv_ref, w_ref, m_ref, v_ref, g_ref, d_ref, mo_ref, vo_ref):
        g = recv_ref[0].astype(f32)
        for k in range(1, N_DEV):
            g = g + recv_ref[k].astype(f32)
        mn = ADAM_B1 * m_ref[...] + (1.0 - ADAM_B1) * g
        vn = ADAM_B2 * v_ref[...] + (1.0 - ADAM_B2) * (g * g)
        g_ref[...] = g
        mo_ref[...] = mn
        vo_ref[...] = vn
        d_ref[...] = -ADAM_LR * ((mn * c1) / (jnp.sqrt(vn * c2) + ADAM_EPS) + ADAM_WD * w_ref[...])

    if cols is None:
        blk = pl.BlockSpec((rows, ccols), lambda i: (i, 0))
        rblk, steps = pl.BlockSpec((N_DEV, rows, ccols), lambda i: (0, i, 0)), r // rows
    else:
        blk = pl.BlockSpec((r, cols), lambda i: (0, i))
        rblk, steps = pl.BlockSpec((N_DEV, r, cols), lambda i: (0, 0, i)), ccols // cols
    return _pc(
        body, name=name, grid=(steps,),
        in_specs=[rblk, blk, blk, blk],
        out_specs=[blk] * 4, out_shape=[jax.ShapeDtypeStruct((r, ccols), f32)] * 4,
        compiler_params=_cparams(("arbitrary",)),
    )(recv, w, m, v)


SMALL = (("norm_w", 1, 1024, 0), ("ssd_conv_b", 1, 1536, 0), ("ssd_dt_bias", 1, 16, 0), ("ssd_a_log", 1, 16, 0),
         ("ssd_d", 1, 16, 0), ("ssd_norm_w", 1, 1024, 0), ("gdn_dt_bias", 1, 8, 16), ("gdn_a_log", 1, 8, 16),
         ("gdn_norm_w", 1, 128, 0), ("final_norm_w", 1, 1024, 0),
         ("ssd_conv_w", CONV_K, SSD_CONV_DIM // N_DEV, 0), ("gdn_conv_w", CONV_K, GDN_CONV_DIM // N_DEV, 0))


def _small_layout():
    out, off = [], 0
    for name, rows, n, lane0 in SMALL + (("loss", 1, 128, 0),):
        stride = -(-(lane0 + n) // 128) * 128
        out.append((name, rows, n, lane0, stride, off))
        off += rows * stride
    return out, off


def scatter_small(accs):
    layout, total = _small_layout()
    f32 = jnp.float32

    def body(*refs):
        acc_refs, out_ref, slabs = refs[:len(layout)], refs[len(layout)], refs[len(layout) + 1]
        sems = refs[len(layout) + 2:]
        slabs[...] = jnp.zeros_like(slabs)
        for (name, rows, n, lane0, stride, off), acc in zip(layout, acc_refs):
            for k in range(N_DEV):
                if rows == 1:
                    slabs[k, :, off:off + stride] = acc[0:1, 0:stride]
                else:
                    for j in range(rows):
                        slabs[k, :, off + stride * j:off + stride * j + n] = acc[j:j + 1, n * k:n * k + n]
        start, wait = _exchange_ops("scatter", slabs, out_ref, *sems)
        start()
        wait()

    return _pc_comm(
        body, name="scatter_small_grads", out_specs=ANY, out_shape=jax.ShapeDtypeStruct((N_DEV, 1, total), f32),
        scratch_shapes=[pltpu.VMEM((N_DEV, 1, total), f32)] + _exchange_sems(1),
    )(*accs)


def adamw_small(recv, w, m, v):
    layout, total = _small_layout()
    loss_off = layout[-1][5]
    layout = layout[:-1]
    f32 = jnp.float32
    c1 = 1.0 / (1.0 - ADAM_B1 ** ADAM_STEP)
    c2 = 1.0 / (1.0 - ADAM_B2 ** ADAM_STEP)
    np_ = len(layout)

    def body(*refs):
        recv_ref = refs[0]
        w_refs, m_refs, v_refs = refs[1:1 + np_], refs[1 + np_:1 + 2 * np_], refs[1 + 2 * np_:1 + 3 * np_]
        o_refs = refs[1 + 3 * np_:]
        g_all = recv_ref[0]
        for k in range(1, N_DEV):
            g_all = g_all + recv_ref[k]
        o_refs[4 * np_][...] = g_all[:, loss_off:loss_off + 128]

        def update(g, wv, mv, vv):
            mn = ADAM_B1 * mv + (1.0 - ADAM_B1) * g
            vn = ADAM_B2 * vv + (1.0 - ADAM_B2) * (g * g)
            return g, -ADAM_LR * ((mn * c1) / (jnp.sqrt(vn * c2) + ADAM_EPS) + ADAM_WD * wv), mn, vn

        for p, (name, rows, n, lane0, stride, off) in enumerate(layout):
            outs = o_refs[4 * p:4 * p + 4]
            if rows == 1:
                res = update(g_all[:, off + lane0:off + lane0 + n], w_refs[p][...], m_refs[p][...], v_refs[p][...])
                for o, r in zip(outs, res):
                    o[...] = r
            else:
                for j in range(rows):
                    res = update(g_all[:, off + stride * j:off + stride * j + n], w_refs[p][0, j:j + 1, :],
                                 m_refs[p][0, j:j + 1, :], v_refs[p][0, j:j + 1, :])
                    for o, r in zip(outs, res):
                        o[0, j:j + 1, :] = r

    names = [e[0] for e in layout]
    ins = [recv] + [d[nm] for d in (w, m, v) for nm in names]
    out_shape = [jax.ShapeDtypeStruct(w[nm].shape, f32) for nm in names for _ in range(4)]
    out_shape.append(jax.ShapeDtypeStruct((1, 128), f32))
    res = _pc(body, name="adamw_small", out_shape=out_shape)(*ins)
    return {nm: tuple(res[4 * p:4 * p + 4]) for p, nm in enumerate(names)}, res[4 * np_]


SHARD = (("ssd_conv_w", CONV_K * SSD_CONV_DIM // N_DEV), ("gdn_conv_w", CONV_K * GDN_CONV_DIM // N_DEV))
SHARD_ROWS = 24


def _rows_of(size):
    return -(-size // 128)


def _pack(vals, layout, total_rows):
    parts = []
    for (name, size), val in zip(layout, vals):
        flat = val.reshape(-1).astype(jnp.float32)
        parts.append(jnp.pad(flat, (0, _rows_of(size) * 128 - size)).reshape(-1, 128))
    used = sum(_rows_of(s) for _, s in layout)
    parts.append(jnp.zeros((total_rows - used, 128), jnp.float32))
    return jnp.concatenate(parts, axis=0)


def _conv_full(gathered_flat, ccols):
    return gathered_flat.reshape(N_DEV, CONV_K, ccols // N_DEV).transpose(1, 0, 2).reshape(CONV_K, ccols)


def kernel(x, norm_w, w_in, ssd_conv_w, ssd_conv_b, ssd_dt_bias, ssd_a_log, ssd_d, ssd_norm_w, gdn_conv_w, gdn_dt_bias, gdn_a_log, gdn_norm_w, w_out, final_norm_w, loss_target, m_norm_w, m_w_in, m_ssd_conv_w, m_ssd_conv_b, m_ssd_dt_bias, m_ssd_a_log, m_ssd_d, m_ssd_norm_w, m_gdn_conv_w, m_gdn_dt_bias, m_gdn_a_log, m_gdn_norm_w, m_w_out, m_final_norm_w, v_norm_w, v_w_in, v_ssd_conv_w, v_ssd_conv_b, v_ssd_dt_bias, v_ssd_a_log, v_ssd_d, v_ssd_norm_w, v_gdn_conv_w, v_gdn_dt_bias, v_gdn_a_log, v_gdn_norm_w, v_w_out, v_final_norm_w):
    f32 = jnp.float32
    w = dict(norm_w=norm_w, w_in=w_in, ssd_conv_w=ssd_conv_w, ssd_conv_b=ssd_conv_b, ssd_dt_bias=ssd_dt_bias,
             ssd_a_log=ssd_a_log, ssd_d=ssd_d, ssd_norm_w=ssd_norm_w, gdn_conv_w=gdn_conv_w, gdn_dt_bias=gdn_dt_bias,
             gdn_a_log=gdn_a_log, gdn_norm_w=gdn_norm_w, w_out=w_out, final_norm_w=final_norm_w)
    m = dict(norm_w=m_norm_w, w_in=m_w_in, ssd_conv_w=m_ssd_conv_w, ssd_conv_b=m_ssd_conv_b, ssd_dt_bias=m_ssd_dt_bias,
             ssd_a_log=m_ssd_a_log, ssd_d=m_ssd_d, ssd_norm_w=m_ssd_norm_w, gdn_conv_w=m_gdn_conv_w,
             gdn_dt_bias=m_gdn_dt_bias, gdn_a_log=m_gdn_a_log, gdn_norm_w=m_gdn_norm_w, w_out=m_w_out,
             final_norm_w=m_final_norm_w)
    v = dict(norm_w=v_norm_w, w_in=v_w_in, ssd_conv_w=v_ssd_conv_w, ssd_conv_b=v_ssd_conv_b, ssd_dt_bias=v_ssd_dt_bias,
             ssd_a_log=v_ssd_a_log, ssd_d=v_ssd_d, ssd_norm_w=v_ssd_norm_w, gdn_conv_w=v_gdn_conv_w,
             gdn_dt_bias=v_gdn_dt_bias, gdn_a_log=v_gdn_a_log, gdn_norm_w=v_gdn_norm_w, w_out=v_w_out,
             final_norm_w=v_final_norm_w)
    names = list(w)
    shapes = {n: w[n].shape for n in names}

    xl, tgt = x[0], loss_target[0]
    cs = _consts()
    dtb_s = _pad_lanes(ssd_dt_bias, 0)
    alog_s = _pad_lanes(ssd_a_log, 0)
    dpar = _pad_lanes(ssd_d, 0)
    dtb_g = _pad_lanes(gdn_dt_bias, 16)
    alog_g = _pad_lanes(gdn_a_log, 16)
    nw_g = gdn_norm_w.reshape(1, 128)
    nw_s = ssd_norm_w.reshape(1, 1024)
    cb_s = ssd_conv_b.reshape(1, 1536)
    nw1 = norm_w.reshape(1, D_MODEL)

    w_slot = lax.dynamic_update_slice(jnp.zeros((W_SLOT, D_MODEL), _MM), w_in[0].T.astype(_MM),
                                      ((W_IN_SHARD * _me()[3]) % 16, 0))
    (g_w_in,) = all_gather([w_slot], "gather_w_in")
    w_perm = g_w_in.reshape(W_GATHERED, D_MODEL)
    conv_pack = _pack([w["ssd_conv_w"], w["gdn_conv_w"]], SHARD, SHARD_ROWS)
    u, z, xbc, gate, qkv, sm, g_w_out, g_conv = inproj_fwd(xl, nw1, w_perm, [w_out[0].astype(_MM), conv_pack])
    w_out_full = g_w_out.reshape(MIX_WIDTH, D_MODEL)
    ssd_cw = _conv_full(g_conv[:, 0:6].reshape(N_DEV, -1), SSD_CONV_DIM)
    gdn_cw = _conv_full(g_conv[:, 6:18].reshape(N_DEV, -1), GDN_CONV_DIM)

    nc = xl.shape[0] // CHUNK
    y_ssd, hs, pre_s, y_gdn, ss, ts, pre_g = _chunk_call(
        [ssd_fwd(z, xbc, sm, ssd_cw, cb_s, dtb_s, alog_s, dpar, nw_s, cs),
         gdn_fwd(gate, qkv, sm, gdn_cw, dtb_g, alog_g, nw_g, cs)], "scan_fwd", nc, False)
    dout, dys, dyg, g_wout, g_fnw, loss_l = out_fwd_bwd(xl, tgt, y_ssd, y_gdn, w_out_full,
                                                        final_norm_w.reshape(1, D_MODEL))
    (dz, dxbc, g_cw_s, g_cb_s, g_dtb_s, g_alog_s, g_d, g_nw_s,
     dgate, dqkv, dsm, g_cw_g, g_dtb_g, g_alog_g, g_nw_g) = _chunk_call(
        [ssd_bwd(z, xbc, pre_s, sm, hs, dys, ssd_cw, dtb_s, alog_s, dpar, nw_s, cs),
         gdn_bwd(gate, qkv, pre_g, sm, ss, ts, dyg, gdn_cw, dtb_g, alog_g, nw_g, cs)], "scan_bwd", nc, True)

    t_w_out = g_wout.reshape(N_DEV, MIX_WIDTH // N_DEV, D_MODEL)
    gws = dict(zip(("z", "sm"), grad_w_many(u, [dz, dsm], "grad_w_in_z_sm")))
    gws["xbc"] = grad_w_group(u, dxbc, "grad_w_in_xbc")
    gws["gate"] = grad_w_group(u, dgate, "grad_w_in_gate")
    gws["qkv"], r_w_out = grad_w_group(u, dqkv, "grad_w_in_qkv", [t_w_out])
    t_w_in = unperm_w_in(gws["z"], gws["xbc"], gws["gate"], gws["qkv"], gws["sm"])
    dx, g_nw, r_w_in = inproj_bwd_dx(xl, dout, nw1, w_perm, (dz, dxbc, dgate, dqkv, dsm), [t_w_in])

    accs = dict(norm_w=g_nw, ssd_conv_b=g_cb_s, ssd_dt_bias=g_dtb_s, ssd_a_log=g_alog_s, ssd_d=g_d,
                ssd_norm_w=g_nw_s, gdn_dt_bias=g_dtb_g, gdn_a_log=g_alog_g, gdn_norm_w=g_nw_g, final_norm_w=g_fnw,
                ssd_conv_w=g_cw_s, gdn_conv_w=g_cw_g)
    r_small = scatter_small([accs[e[0]] for e in SMALL] + [loss_l])

    o_w_in = adamw_sum(r_w_in, w_in[0].T, m_w_in[0].T, v_w_in[0].T, None, "adamw_w_in", cols=256)
    o_w_out = adamw_sum(r_w_out, w_out[0], m_w_out[0], v_w_out[0], 64, "adamw_w_out")
    row = lambda d: {n: (a.reshape(1, -1) if a.ndim == 1 else a) for n, a in d.items()}
    o_small, loss_sum = adamw_small(r_small, row(w), row(m), row(v))

    loss = loss_sum[0, 0]
    outs = [loss, dx[None]]
    for k in range(4):
        parts = {n: o_small[n][k] for n in o_small}
        parts["w_in"] = o_w_in[k].T
        parts["w_out"] = o_w_out[k]
        outs += [parts[n].reshape(shapes[n]) for n in names]
    return tuple(outs)
```

```python
import functools

import jax
import jax.numpy as jnp
import numpy as np
from jax import lax
from jax.experimental import pallas as pl
from jax.experimental.pallas import tpu as pltpu

_MM = jnp.bfloat16

D_MODEL = 1024
CHUNK = 64
CONV_K = 4
EPS = 1e-6
SSD_CONV_DIM = 1536
GDN_HEADS = 8
GDN_DK = 128
GDN_CONV_DIM = 3072
MIX_WIDTH = 2048
IN_DIM = 6688
N_DEV = 8
W_IN_SHARD = IN_DIM // N_DEV
HI = lax.Precision.HIGHEST
HIGH = lax.Precision.HIGH
VMEM_LIMIT = 56 * 1024 * 1024

ADAM_LR = 0.001
ADAM_B1 = 0.9
ADAM_B2 = 0.999
ADAM_EPS = 1e-08
ADAM_WD = 0.01
ADAM_STEP = 10


def _pc(body, **kw):
    return pl.pallas_call(body, **kw)


def _pc_comm(body, **kw):
    return pl.pallas_call(body, **kw)


def _cparams(sem):
    return pltpu.CompilerParams(dimension_semantics=sem, vmem_limit_bytes=VMEM_LIMIT)


def _sig(x):
    return 0.5 * jnp.tanh(0.5 * x) + 0.5


@jax.custom_vjp
def _sigmoid(x):
    return _sig(x)


def _sigmoid_fwd(x):
    s = _sig(x)
    return s, s


def _sigmoid_bwd(s, g):
    return (g * s * (1.0 - s),)


_sigmoid.defvjp(_sigmoid_fwd, _sigmoid_bwd)


@jax.custom_vjp
def _silu(x):
    return x * _sig(x)


def _silu_fwd(x):
    s = _sig(x)
    return x * s, (x, s)


def _silu_bwd(res, g):
    x, s = res
    return (g * (s * (1.0 + x * (1.0 - s))),)


_silu.defvjp(_silu_fwd, _silu_bwd)


def _softplus_impl(x):
    return jnp.maximum(x, 0.0) + jnp.log(1.0 + jnp.exp(-jnp.abs(x)))


@jax.custom_vjp
def _softplus(x):
    return _softplus_impl(x)


def _softplus_fwd(x):
    return _softplus_impl(x), x


def _softplus_bwd(x, g):
    return (g * _sig(x),)


_softplus.defvjp(_softplus_fwd, _softplus_bwd)


def _lane_bcast_impl(x, k):
    return jnp.broadcast_to(x[..., k:k + 1], x.shape)


@functools.partial(jax.custom_vjp, nondiff_argnums=(1,))
def _lane_bcast(x, k):
    return _lane_bcast_impl(x, k)


def _lane_bcast_fwd(x, k):
    return _lane_bcast_impl(x, k), None


def _lane_bcast_bwd(k, _, g):
    lane = lax.broadcasted_iota(jnp.int32, g.shape, g.ndim - 1)
    return (jnp.where(lane == k, jnp.sum(g, axis=-1, keepdims=True), 0.0),)


_lane_bcast.defvjp(_lane_bcast_fwd, _lane_bcast_bwd)


def _mm(a, b):
    return jnp.dot(a.astype(_MM), b.astype(_MM), preferred_element_type=jnp.float32)


def _mm_nt(a, b):
    return lax.dot_general(a.astype(_MM), b.astype(_MM), (((1,), (1,)), ((), ())),
                           preferred_element_type=jnp.float32)


def _mm_tn(a, b):
    return lax.dot_general(a.astype(_MM), b.astype(_MM), (((0,), (0,)), ((), ())),
                           preferred_element_type=jnp.float32)


def _dot_hi(a, b):
    return jnp.dot(a, b, precision=HI, preferred_element_type=jnp.float32)


def _bmm(a, b):
    return lax.dot_general(a.astype(_MM), b.astype(_MM), (((2,), (1,)), ((0,), (0,))),
                           preferred_element_type=jnp.float32)


def _bmm_nt(a, b):
    return lax.dot_general(a.astype(_MM), b.astype(_MM), (((2,), (2,)), ((0,), (0,))),
                           preferred_element_type=jnp.float32)


def _bmm_tn(a, b):
    return lax.dot_general(a.astype(_MM), b.astype(_MM), (((1,), (1,)), ((0,), (0,))),
                           preferred_element_type=jnp.float32)


def _bmm_hi(a, b):
    return lax.dot_general(a, b, (((2,), (1,)), ((0,), (0,))), precision=HIGH, preferred_element_type=jnp.float32)


def _bmm_nt_hi(a, b):
    return lax.dot_general(a, b, (((2,), (2,)), ((0,), (0,))), precision=HIGH, preferred_element_type=jnp.float32)


def _bmm_tn_hi(a, b):
    return lax.dot_general(a, b, (((1,), (1,)), ((0,), (0,))), precision=HIGH, preferred_element_type=jnp.float32)


def _consts():
    l = np.arange(CHUNK)
    tri = (l[:, None] >= l[None, :]).astype(np.float32)
    lane = np.arange(128)
    i2 =(l[:, None] == (lane[None, :] % 64)).astype(np.float32)
    mask2 = (l[:, None] >= (lane[None, :] % 64)).astype(np.float32)
    lo = (lane < 64).astype(np.float32)[None, :]
    i64 = np.eye(CHUNK, dtype=np.float32)
    strict = (l[:, None] > l[None, :]).astype(np.float32)
    return dict(tri=jnp.asarray(tri), i2=jnp.asarray(i2), mask2=jnp.asarray(mask2), lo=jnp.asarray(lo),
                i64=jnp.asarray(i64), strict=jnp.asarray(strict))


def _ssd_chunk(xs_pre, b_pre, c_pre, z, sm, ht, dtb, alog, dpar, nw, tri, i2, mask2, lo):
    lane = lax.broadcasted_iota(jnp.int32, (1, 128), 1)
    m16 = lane < 16
    dt = jnp.where(m16, _softplus(sm + dtb), 0.0)
    a_neg = -jnp.exp(alog)
    cum = _dot_hi(tri, dt * a_neg)
    row = lax.broadcasted_iota(jnp.int32, (CHUNK, 1), 0)
    is_last = row == CHUNK - 1
    hi = 1.0 - lo
    bm = [_silu(b) for b in b_pre]
    cm = [_silu(c) for c in c_pre]
    cb2 = [_mm_nt(cm[g], jnp.concatenate([bm[g], bm[g]], axis=0)) for g in range(2)]
    ht_g = [jnp.concatenate(ht[4 * g:4 * g + 4], axis=1) for g in range(2)]
    yoff_g = [_mm(cm[g], ht_g[g]) for g in range(2)]
    yg, xdec, clast = [], [], []
    for j in range(8):
        g, k4 = j // 4, j % 4
        pair = lambda v, j=j: jnp.where(lo > 0.5, _lane_bcast(v, 2 * j), _lane_bcast(v, 2 * j + 1))
        xs = _silu(xs_pre[j])
        dte = pair(dt)
        cume = pair(cum)
        cum_last = jnp.sum(jnp.where(is_last, cume, 0.0), axis=0, keepdims=True)
        xdt = xs * dte
        rowv = jnp.sum(cume * i2, axis=0, keepdims=True)
        lm = jnp.exp(jnp.where(mask2 > 0.5, cume - rowv, -jnp.inf))
        m = cb2[g] * lm
        xblk = jnp.concatenate([xdt * lo, xdt * hi], axis=0)
        y = _mm(m, xblk)
        y = y + yoff_g[g][:, 128 * k4:128 * k4 + 128] * jnp.exp(cume)
        y = y + pair(dpar) * xs
        yg.append(y * _silu(z[j]))
        xdec.append(xdt * jnp.exp(cum_last - cume))
        clast.append(cum_last)
    ht_next = []
    for g in range(2):
        st = _mm_tn(bm[g], jnp.concatenate(xdec[4 * g:4 * g + 4], axis=1))
        for k4 in range(4):
            j = 4 * g + k4
            ht_next.append(ht[j] * jnp.exp(clast[j]) + st[:, 128 * k4:128 * k4 + 128])
    outs = []
    for g in range(2):
        ss = sum(jnp.sum(yg[j] * yg[j], axis=-1, keepdims=True) for j in range(4 * g, 4 * g + 4))
        rs = lax.rsqrt(ss * (1.0 / 512.0) + EPS)
        for j in range(4 * g, 4 * g + 4):
            outs.append(yg[j] * rs * nw[j])
    return outs, ht_next


def _tri_inverse(a):
    eye = jnp.eye(CHUNK, dtype=jnp.float32)[None]
    p = eye - a
    x = _bmm_hi(a, a)
    for i in range(4):
        both = (_bmm_hi if i == 0 else _bmm)(jnp.concatenate([p, x], axis=1), x)
        p = p + both[:, :CHUNK]
        x = both[:, CHUNK:]
    return p + _bmm(p, x)


def _solve_apply(t, r1, r2):
    both = _bmm_hi(t, jnp.concatenate([r1, r2], axis=-1))
    n = r1.shape[-1]
    return both[..., :n], both[..., n:]


@jax.custom_vjp
def _solve(a, r1, r2, t):
    return _solve_apply(t, r1, r2)


def _solve_fwd(a, r1, r2, t):
    u, w = _bmm_hi(t, r1), _bmm_hi(t, r2)
    return (u, w), (t, u, w)


def _solve_bwd(res, cts):
    t, u, w = res
    du, dw = cts
    dr1 = _bmm_tn_hi(t, du)
    dr2 = _bmm_tn_hi(t, dw)
    da = -(_bmm_nt_hi(dr1, u) + _bmm_nt_hi(dr2, w))
    return da, dr1, dr2, jnp.zeros_like(t)


_solve.defvjp(_solve_fwd, _solve_bwd)


def _gdn_chunk(q_pre, k_pre, v_pre, gate, sm, s, dtb, alog, nw, tri, i64, strict, t_in=None):
    lane = lax.broadcasted_iota(jnp.int32, (1, 128), 1)
    m_a = (lane >= 16) & (lane < 24)
    g_full = jnp.where(m_a, -jnp.exp(alog) * _softplus(sm + dtb), 0.0)
    gc = _dot_hi(tri, g_full)
    sig = _sigmoid(sm)
    heads = lambda f: jnp.concatenate([f(h)[None] for h in range(GDN_HEADS)], axis=0)
    gc3 = heads(lambda h: _lane_bcast(gc, 16 + h))
    beta3 = heads(lambda h: _lane_bcast(sig, 24 + h))
    q = _silu(q_pre)
    q = q * lax.rsqrt(jnp.sum(q * q, axis=-1, keepdims=True) + EPS) * (GDN_DK ** -0.5)
    k = _silu(k_pre)
    k = k * lax.rsqrt(jnp.sum(k * k, axis=-1, keepdims=True) + EPS)
    v = _silu(v_pre)
    gcl = gc3[:, :, :CHUNK]
    gc_row = jnp.sum(gcl * i64[None], axis=1, keepdims=True)
    incl = (strict + i64)[None] > 0.5
    decay = jnp.exp(jnp.where(incl, gcl - gc_row, -jnp.inf))
    kb = k * beta3
    a = jnp.where(strict[None] > 0.5, _bmm_nt(kb, k) * decay, 0.0)
    egc = jnp.exp(gc3)
    t = _tri_inverse(a) if t_in is None else t_in
    u, w = _solve(a, v * beta3, kb * egc, t)
    attn = _bmm_nt(q, k) * decay
    row = lax.broadcasted_iota(jnp.int32, (1, CHUNK, 1), 1)
    gl = jnp.sum(jnp.where(row == CHUNK - 1, gc3, 0.0), axis=1, keepdims=True)
    q_dec = q * egc
    k_dec = k * jnp.exp(gl - gc3)
    ws = _bmm(jnp.concatenate([w, q_dec], axis=1), s)
    v_new = u - ws[:, :CHUNK]
    o = ws[:, CHUNK:] + _bmm(attn, v_new)
    s_next = s * jnp.exp(gl) + _bmm_tn(k_dec, v_new)
    on = o * lax.rsqrt(jnp.mean(o * o, axis=-1, keepdims=True) + EPS) * nw
    return on * _silu(gate), s_next, t


def _conv_fwd(pbuf, w_ref, c0, c1):
    blk = pbuf[:, c0:c1]
    acc = w_ref[CONV_K - 1:CONV_K, c0:c1] * blk[8:72]
    for j in range(CONV_K - 1):
        acc = acc + w_ref[j:j + 1, c0:c1] * pltpu.roll(blk, CONV_K - 1 - j, axis=0)[8:72]
    return acc


MESH = pl.DeviceIdType.MESH
ANY = pl.BlockSpec(memory_space=pl.ANY)


def _me():
    x, y, c = lax.axis_index("x"), lax.axis_index("y"), lax.axis_index("c")
    return x, y, c, 4 * x + 2 * y + c


def _peer(r):
    x, y, c, _ = _me()
    px = 1 - x if r & 4 else x
    py = 1 - y if r & 2 else y
    pc = 1 - c if r & 1 else c
    return (px, py, pc), 4 * px + 2 * py + pc


def _exchange_ops(kind, in_ref, out_ref, send_sems, recv_sems, local_sem):
    me = _me()[3]
    local = pltpu.make_async_copy(in_ref.at[me] if kind == "scatter" else in_ref, out_ref.at[me], local_sem)
    sends, recvs = [], []
    for r in range(1, N_DEV):
        peer, pidx = _peer(r)
        src = in_ref.at[pidx] if kind == "scatter" else in_ref
        sems = dict(send_sem=send_sems.at[r - 1], recv_sem=recv_sems.at[r - 1], device_id=peer, device_id_type=MESH)
        sends.append(pltpu.make_async_remote_copy(src_ref=src, dst_ref=out_ref.at[me], **sems))
        recvs.append(pltpu.make_async_remote_copy(src_ref=src, dst_ref=out_ref.at[pidx], **sems))

    def start():
        local.start()
        for cp in sends:
            cp.start()

    def wait():
        for cp in recvs:
            cp.wait_recv()
        for cp in sends:
            cp.wait_send()
        local.wait()

    return start, wait


def _exchange_sems(n):
    return [pltpu.SemaphoreType.DMA((N_DEV - 1,)), pltpu.SemaphoreType.DMA((N_DEV - 1,)),
            pltpu.SemaphoreType.DMA(())] * n


def _exchange_out_shape(kind, a):
    return jax.ShapeDtypeStruct(a.shape if kind == "scatter" else (N_DEV,) + a.shape, a.dtype)


def _hosting(body, n_in, n_out, n_scratch, kinds, first, last):
    ne = len(kinds)

    def wrapped(*refs):
        ins, ex_in = refs[:n_in], refs[n_in:n_in + ne]
        o0 = n_in + ne
        outs, ex_out = refs[o0:o0 + n_out], refs[o0 + n_out:o0 + n_out + ne]
        s0 = o0 + n_out + ne
        scr, sems = refs[s0:s0 + n_scratch], refs[s0 + n_scratch:]
        ops = [_exchange_ops(kinds[e], ex_in[e], ex_out[e], *sems[3 * e:3 * e + 3]) for e in range(ne)]

        @pl.when(first())
        def _():
            for start, _ in ops:
                start()

        body(*ins, *outs, *scr)

        @pl.when(last())
        def _():
            for _, wait in ops:
                wait()

    return wrapped


GROUPS = (("z", 0, 1024), ("xbc", 1024, 2560), ("gate", 2560, 3584), ("qkv", 3584, 6656), ("sm", 6656, 6784))
GROUP_ROWS = dict(z=((0, 1024),), xbc=((1024, 2560),), gate=((2576, 3600),), qkv=((3600, 6672),),
                  sm=((2560, 2576), (6672, 6688)))


W_SLOT = 848
W_GATHERED = N_DEV * W_SLOT


def _slot_pos(s, c):
    return W_SLOT * s + (c - W_IN_SHARD * s) + (W_IN_SHARD * s) % 16


def _w_pieces(w_ref, a, b):
    out, cur = [], a
    while cur < b:
        s = cur // W_IN_SHARD
        end = W_IN_SHARD * (s + 1)
        if end >= b:
            out.append(w_ref[_slot_pos(s, cur):_slot_pos(s, b), :])
            break
        hi = end // 16 * 16
        if hi > cur:
            out.append(w_ref[_slot_pos(s, cur):_slot_pos(s, hi), :])
        if end % 16:
            p, q = _slot_pos(s, hi), W_SLOT * (s + 1)
            out.append(w_ref[p:p + 16, :] + w_ref[q:q + 16, :])
            cur = hi + 16
        else:
            cur = hi
    return out


def _w_rows(w_ref, name, width):
    pieces = [p for a, b in GROUP_ROWS[name] for p in _w_pieces(w_ref, a, b)]
    n = sum(b - a for a, b in GROUP_ROWS[name])
    if n < width:
        pieces.append(jnp.zeros((width - n, D_MODEL), w_ref.dtype))
    return pieces[0] if len(pieces) == 1 else jnp.concatenate(pieces, axis=0)


def inproj_fwd(x, norm_w, w_perm, gathered):
    t = x.shape[0]
    tm = min(512, t)
    steps = t // tm
    kinds = ["gather"] * len(gathered)

    def body(x_ref, nw_ref, w_ref, u_ref, z_ref, xbc_ref, gate_ref, qkv_ref, sm_ref):
        xf = x_ref[...]
        rstd = lax.rsqrt(jnp.mean(xf * xf, axis=-1, keepdims=True) + EPS)
        u = (xf * rstd * nw_ref[...]).astype(_MM)
        u_ref[...] = u
        for (name, c0, c1), o_ref in zip(GROUPS, (z_ref, xbc_ref, gate_ref, qkv_ref, sm_ref)):
            o_ref[...] = lax.dot_general(u, _w_rows(w_ref, name, c1 - c0), (((1,), (1,)), ((), ())),
                                         preferred_element_type=jnp.float32)

    outs = [jax.ShapeDtypeStruct((t, D_MODEL), _MM)] + [jax.ShapeDtypeStruct((t, c1 - c0), jnp.float32)
                                                        for _, c0, c1 in GROUPS]
    hosted = _hosting(body, 3, 6, 0, kinds, lambda: pl.program_id(0) == 0, lambda: pl.program_id(0) == steps - 1)
    return _pc_comm(
        hosted, name="inproj_fwd", grid=(steps,),
        in_specs=[pl.BlockSpec((tm, D_MODEL), lambda i: (i, 0)),
                  pl.BlockSpec((1, D_MODEL), lambda i: (0, 0)),
                  pl.BlockSpec((W_GATHERED, D_MODEL), lambda i: (0, 0), pipeline_mode=pl.Buffered(1))] +
                 [ANY] * len(gathered),
        out_specs=[pl.BlockSpec((tm, D_MODEL), lambda i: (i, 0))] +
                  [pl.BlockSpec((tm, c1 - c0), lambda i: (i, 0)) for _, c0, c1 in GROUPS] + [ANY] * len(gathered),
        out_shape=outs + [_exchange_out_shape("gather", a) for a in gathered],
        scratch_shapes=_exchange_sems(len(gathered)), compiler_params=_cparams(("arbitrary",)),
    )(x, norm_w, w_perm, *gathered)


SUB_FWD = 4
SUB_BWD = 2


def _halo_spec(width, idx_fn):
    return pl.BlockSpec((8, width), lambda i: (jnp.maximum(idx_fn(i) * (SUB_FWD * CHUNK // 8) - 1, 0), 0))


def _when_first(shared, fn):
    if shared["first"] is not False:
        pl.when(shared["first"])(fn)


def _full(shape):
    nd = len(shape)
    return pl.BlockSpec(shape, lambda i: (0,) * nd)


def _ssd_split(pre_fn, z_ref, sm_ref):
    xs_pre = [pre_fn(128 * j, 128 * j + 128) for j in range(8)]
    b_pre = [pre_fn(1024 + 128 * g, 1152 + 128 * g) for g in range(2)]
    c_pre = [pre_fn(1280 + 128 * g, 1408 + 128 * g) for g in range(2)]
    z = [z_ref[:, 128 * j:128 * j + 128] for j in range(8)]
    return xs_pre, b_pre, c_pre, z, sm_ref[...]


def ssd_fwd(z, xbc, sm, conv_w, conv_b, dtb, alog, dpar, nw, cs):
    t = z.shape[0]
    nc = t // CHUNK

    def body(shared, z_ref, xbc_ref, halo_ref, sm_ref, cw_ref, cb_ref, dtb_ref, alog_ref, dpar_ref, nw_ref,
             tri_ref, i2_ref, mask2_ref, lo_ref, y_ref, hs_ref, pre_ref, pbuf, ht_scr):
        def init():
            ht_scr[...] = jnp.zeros_like(ht_scr)

        _when_first(shared, init)
        pbuf[0:8, :] = jnp.where(shared["first"], 0.0, halo_ref[...])
        pbuf[8:72, :] = xbc_ref[...]

        def pre_fn(c0, c1):
            pre = _conv_fwd(pbuf, cw_ref, c0, c1) + cb_ref[:, c0:c1]
            pre_ref[:, c0:c1] = pre
            return pre

        xs_pre, b_pre, c_pre, zz, smv = _ssd_split(pre_fn, z_ref, sm_ref)
        ht = [ht_scr[:, 128 * j:128 * j + 128] for j in range(8)]
        hs_ref[0] = ht_scr[...]
        nwl = [nw_ref[:, 128 * j:128 * j + 128] for j in range(8)]
        outs, ht_next = _ssd_chunk(xs_pre, b_pre, c_pre, zz, smv, ht, dtb_ref[...], alog_ref[...], dpar_ref[...],
                                   nwl, tri_ref[...], i2_ref[...], mask2_ref[...], lo_ref[...])
        for j in range(8):
            y_ref[:, 128 * j:128 * j + 128] = outs[j].astype(y_ref.dtype)
            ht_scr[:, 128 * j:128 * j + 128] = ht_next[j]

    blk = lambda w: pl.BlockSpec((SUB_FWD * CHUNK, w), lambda i: (i, 0))
    return dict(
        body=body,
        in_kinds=["rows", "rows", ("halo", 1), "rows"] + ["full"] * 10, out_kinds=["rows", "state", "rows"],
        in_specs=[blk(1024), blk(1536), _halo_spec(1536, lambda i: i), blk(128),
                  _full((CONV_K, 1536)), _full((1, 1536)), _full((1, 128)), _full((1, 128)), _full((1, 128)),
                  _full((1, 1024)), _full((64, 64)), _full((64, 128)), _full((64, 128)),
                  _full((1, 128))],
        out_specs=[blk(1024), pl.BlockSpec((SUB_FWD, 128, 1024), lambda i: (i, 0, 0)), blk(1536)],
        out_shape=[jax.ShapeDtypeStruct((t, 1024), _MM), jax.ShapeDtypeStruct((nc, 128, 1024), jnp.float32),
                   jax.ShapeDtypeStruct((t, 1536), jnp.float32)],
        scratch=[pltpu.VMEM((72, 1536), jnp.float32), pltpu.VMEM((128, 1024), jnp.float32)],
        args=[z, xbc, xbc, sm, conv_w, conv_b, dtb, alog, dpar, nw, cs["tri"], cs["i2"], cs["mask2"], cs["lo"]])


def _conv_bwd(dpre_list, col_ranges, dbuf, carry, x_ref, cw_ref, dx_ref, dcw_ref, dcb_ref, first):
    for dpre, (c0, c1) in zip(dpre_list, col_ranges):
        dbuf[0:64, c0:c1] = dpre
    dbuf[64:72, :] = jnp.where(first, 0.0, carry[...])
    carry[...] = dbuf[0:8, :]
    for (c0, c1) in col_ranges:
        xin = x_ref[:, c0:c1]
        blk = dbuf[:, c0:c1]
        acc = None
        for j in range(CONV_K):
            sh = blk[0:64] if j == CONV_K - 1 else pltpu.roll(blk, 72 - (CONV_K - 1 - j), axis=0)[0:64]
            term = cw_ref[j:j + 1, c0:c1] * sh
            acc = term if acc is None else acc + term
            dcw_ref[j:j + 1, c0:c1] += jnp.sum(xin * sh, axis=0, keepdims=True)
        dx_ref[:, c0:c1] = acc.astype(dx_ref.dtype)
        if dcb_ref is not None:
            dcb_ref[0:1, c0:c1] += jnp.sum(dbuf[0:64, c0:c1], axis=0, keepdims=True)


def ssd_bwd(z, xbc, pre, sm, hs, dy, conv_w, dtb, alog, dpar, nw, cs):
    t = z.shape[0]
    nc = t // CHUNK

    def body(shared, z_ref, xbc_ref, pre_ref, sm_ref, hs_ref, dy_ref, cw_ref, dtb_ref, alog_ref, dpar_ref, nw_ref,
             tri_ref, i2_ref, mask2_ref, lo_ref,
             dz_ref, dxbc_ref, dcw_ref, dcb_ref, ddtb_ref, dalog_ref, ddpar_ref, dnw_ref,
             dbuf, carry, dht_scr):
        def init():
            dht_scr[...] = jnp.zeros_like(dht_scr)
            dcw_ref[...] = jnp.zeros_like(dcw_ref)
            dcb_ref[...] = jnp.zeros_like(dcb_ref)
            ddtb_ref[...] = jnp.zeros_like(ddtb_ref)
            dalog_ref[...] = jnp.zeros_like(dalog_ref)
            ddpar_ref[...] = jnp.zeros_like(ddpar_ref)
            dnw_ref[...] = jnp.zeros_like(dnw_ref)

        _when_first(shared, init)
        pre_fn = lambda c0, c1: pre_ref[:, c0:c1]
        xs_pre, b_pre, c_pre, zz, smv = _ssd_split(pre_fn, z_ref, sm_ref)
        ht = [hs_ref[0, :, 128 * j:128 * j + 128] for j in range(8)]
        nwl = [nw_ref[:, 128 * j:128 * j + 128] for j in range(8)]
        consts = (tri_ref[...], i2_ref[...], mask2_ref[...], lo_ref[...])

        def f(xs_pre, b_pre, c_pre, zz, smv, ht, dtb, alog, dpar, nwl):
            return _ssd_chunk(xs_pre, b_pre, c_pre, zz, smv, ht, dtb, alog, dpar, nwl, *consts)

        _, vjp = jax.vjp(f, xs_pre, b_pre, c_pre, zz, smv, ht, dtb_ref[...], alog_ref[...], dpar_ref[...], nwl)
        dys = [dy_ref[:, 128 * j:128 * j + 128] for j in range(8)]
        dhts = [dht_scr[:, 128 * j:128 * j + 128] for j in range(8)]
        dxs, db, dc, dzz, dsm, dht, ddtb, dalog, ddpar, dnwl = vjp((dys, dhts))
        for j in range(8):
            dz_ref[:, 128 * j:128 * j + 128] = dzz[j].astype(dz_ref.dtype)
            dht_scr[:, 128 * j:128 * j + 128] = dht[j]
            dnw_ref[0:1, 128 * j:128 * j + 128] += dnwl[j]
        shared["dsm_ssd"] = dsm
        ddtb_ref[0:1, :] += ddtb
        dalog_ref[0:1, :] += dalog
        ddpar_ref[0:1, :] += ddpar
        ranges = ([(128 * j, 128 * j + 128) for j in range(8)] + [(1024 + 128 * g, 1152 + 128 * g) for g in range(2)]
                  + [(1280 + 128 * g, 1408 + 128 * g) for g in range(2)])
        _conv_bwd(dxs + db + dc, ranges, dbuf, carry, xbc_ref, cw_ref, dxbc_ref, dcw_ref, dcb_ref, shared["first"])

    ns = nc // SUB_BWD
    rblk = lambda w: pl.BlockSpec((SUB_BWD * CHUNK, w), lambda i: (ns - 1 - i, 0))
    acc = lambda w: pl.BlockSpec((8, w), lambda i: (0, 0))
    f32 = jnp.float32
    return dict(
        body=body,
        in_kinds=["rows"] * 4 + ["state", "rows"] + ["full"] * 9, out_kinds=["rows", "rows"] + ["full"] * 6,
        in_specs=[rblk(1024), rblk(1536), rblk(1536), rblk(128),
                  pl.BlockSpec((SUB_BWD, 128, 1024), lambda i: (ns - 1 - i, 0, 0)), rblk(1024),
                  _full((CONV_K, 1536)), _full((1, 128)), _full((1, 128)), _full((1, 128)),
                  _full((1, 1024)), _full((64, 64)), _full((64, 128)), _full((64, 128)),
                  _full((1, 128))],
        out_specs=[rblk(1024), rblk(1536), acc(1536), acc(1536), acc(128), acc(128), acc(128), acc(1024)],
        out_shape=[jax.ShapeDtypeStruct((t, 1024), f32), jax.ShapeDtypeStruct((t, 1536), f32),
                   jax.ShapeDtypeStruct((8, 1536), f32),
                   jax.ShapeDtypeStruct((8, 1536), f32), jax.ShapeDtypeStruct((8, 128), f32),
                   jax.ShapeDtypeStruct((8, 128), f32), jax.ShapeDtypeStruct((8, 128), f32),
                   jax.ShapeDtypeStruct((8, 1024), f32)],
        scratch=[pltpu.VMEM((72, 1536), f32), pltpu.VMEM((8, 1536), f32), pltpu.VMEM((128, 1024), f32)],
        args=[z, xbc, pre, sm, hs, dy, conv_w, dtb, alog, dpar, nw, cs["tri"], cs["i2"], cs["mask2"], cs["lo"]])


def _gdn_split(pre_fn, gate_ref):
    def heads(base):
        return jnp.stack([pre_fn(base + 128 * h, base + 128 * h + 128) for h in range(GDN_HEADS)])
    gate = jnp.stack([gate_ref[:, 128 * h:128 * h + 128] for h in range(GDN_HEADS)])
    return heads(0), heads(1024), heads(2048), gate


def gdn_fwd(gate, qkv, sm, conv_w, dtb, alog, nw, cs):
    t = gate.shape[0]
    nc = t // CHUNK

    def body(shared, gate_ref, qkv_ref, halo_ref, sm_ref, cw_ref, dtb_ref, alog_ref, nw_ref,
             tri_ref, i64_ref, strict_ref, o_ref, ss_ref, ts_ref, pre_ref, pbuf, s_scr):
        def init():
            s_scr[...] = jnp.zeros_like(s_scr)

        _when_first(shared, init)
        pbuf[0:8, :] = jnp.where(shared["first"], 0.0, halo_ref[...])
        pbuf[8:72, :] = qkv_ref[...]

        def pre_fn(c0, c1):
            pre = _conv_fwd(pbuf, cw_ref, c0, c1)
            pre_ref[:, c0:c1] = pre
            return pre

        q_pre, k_pre, v_pre, g3 = _gdn_split(pre_fn, gate_ref)
        s = s_scr[...]
        ss_ref[0] = s
        out, s_next, tinv = _gdn_chunk(q_pre, k_pre, v_pre, g3, sm_ref[...], s, dtb_ref[...], alog_ref[...],
                                       nw_ref[...], tri_ref[...], i64_ref[...], strict_ref[...])
        ts_ref[0] = tinv
        s_scr[...] = s_next
        for h in range(GDN_HEADS):
            o_ref[:, 128 * h:128 * h + 128] = out[h].astype(o_ref.dtype)

    blk = lambda w: pl.BlockSpec((SUB_FWD * CHUNK, w), lambda i: (i, 0))
    return dict(
        body=body,
        in_kinds=["rows", "rows", ("halo", 1), "rows"] + ["full"] * 7, out_kinds=["rows", "state", "state", "rows"],
        in_specs=[blk(1024), blk(3072), _halo_spec(3072, lambda i: i), blk(128),
                  _full((CONV_K, 3072)), _full((1, 128)), _full((1, 128)), _full((1, 128)),
                  _full((64, 64)), _full((64, 64)), _full((64, 64))],
        out_specs=[blk(1024), pl.BlockSpec((SUB_FWD, 8, 128, 128), lambda i: (i, 0, 0, 0)),
                   pl.BlockSpec((SUB_FWD, 8, CHUNK, CHUNK), lambda i: (i, 0, 0, 0)), blk(3072)],
        out_shape=[jax.ShapeDtypeStruct((t, 1024), _MM), jax.ShapeDtypeStruct((nc, 8, 128, 128), jnp.float32),
                   jax.ShapeDtypeStruct((nc, 8, CHUNK, CHUNK), jnp.float32),
                   jax.ShapeDtypeStruct((t, 3072), jnp.float32)],
        scratch=[pltpu.VMEM((72, 3072), jnp.float32), pltpu.VMEM((8, 128, 128), jnp.float32)],
        args=[gate, qkv, qkv, sm, conv_w, dtb, alog, nw, cs["tri"], cs["i64"], cs["strict"]])


def gdn_bwd(gate, qkv, pre, sm, ss, ts, do, conv_w, dtb, alog, nw, cs):
    t = gate.shape[0]
    nc = t // CHUNK

    def body(shared, gate_ref, qkv_ref, pre_ref, sm_ref, ss_ref, ts_ref, do_ref, cw_ref, dtb_ref, alog_ref,
             nw_ref, tri_ref, i64_ref, strict_ref,
             dgate_ref, dqkv_ref, dsm_ref, dcw_ref, ddtb_ref, dalog_ref, dnw_ref,
             dbuf, carry, ds_scr):
        def init():
            ds_scr[...] = jnp.zeros_like(ds_scr)
            dcw_ref[...] = jnp.zeros_like(dcw_ref)
            ddtb_ref[...] = jnp.zeros_like(ddtb_ref)
            dalog_ref[...] = jnp.zeros_like(dalog_ref)
            dnw_ref[...] = jnp.zeros_like(dnw_ref)

        _when_first(shared, init)

        q_pre, k_pre, v_pre, g3 = _gdn_split(lambda c0, c1: pre_ref[:, c0:c1], gate_ref)
        consts = (tri_ref[...], i64_ref[...], strict_ref[...], ts_ref[0])

        def f(q_pre, k_pre, v_pre, g3, smv, s, dtb, alog, nwv):
            return _gdn_chunk(q_pre, k_pre, v_pre, g3, smv, s, dtb, alog, nwv, *consts)[:2]

        _, vjp = jax.vjp(f, q_pre, k_pre, v_pre, g3, sm_ref[...], ss_ref[0], dtb_ref[...], alog_ref[...], nw_ref[...])
        do3 = jnp.stack([do_ref[:, 128 * h:128 * h + 128] for h in range(GDN_HEADS)])
        dq, dk, dv, dg3, dsm, ds, ddtb, dalog, dnw = vjp((do3, ds_scr[...]))
        ds_scr[...] = ds
        for h in range(GDN_HEADS):
            dgate_ref[:, 128 * h:128 * h + 128] = dg3[h].astype(dgate_ref.dtype)
        dsm_ref[...] = (dsm + shared["dsm_ssd"]).astype(dsm_ref.dtype)
        ddtb_ref[0:1, :] += ddtb
        dalog_ref[0:1, :] += dalog
        dnw_ref[0:1, :] += dnw
        ranges = [(base + 128 * h, base + 128 * h + 128) for base in (0, 1024, 2048) for h in range(GDN_HEADS)]
        dlist = [d[h] for d in (dq, dk, dv) for h in range(GDN_HEADS)]
        _conv_bwd(dlist, ranges, dbuf, carry, qkv_ref, cw_ref, dqkv_ref, dcw_ref, None, shared["first"])

    ns = nc // SUB_BWD
    rblk = lambda w: pl.BlockSpec((SUB_BWD * CHUNK, w), lambda i: (ns - 1 - i, 0))
    acc = lambda w: pl.BlockSpec((8, w), lambda i: (0, 0))
    f32 = jnp.float32
    return dict(
        body=body,
        in_kinds=["rows"] * 4 + ["state", "state", "rows"] + ["full"] * 7, out_kinds=["rows"] * 3 + ["full"] * 4,
        in_specs=[rblk(1024), rblk(3072), rblk(3072), rblk(128),
                  pl.BlockSpec((SUB_BWD, 8, 128, 128), lambda i: (ns - 1 - i, 0, 0, 0)),
                  pl.BlockSpec((SUB_BWD, 8, CHUNK, CHUNK), lambda i: (ns - 1 - i, 0, 0, 0)), rblk(1024),
                  _full((CONV_K, 3072)), _full((1, 128)), _full((1, 128)), _full((1, 128)),
                  _full((64, 64)), _full((64, 64)), _full((64, 64))],
        out_specs=[rblk(1024), rblk(3072), rblk(128), acc(3072), acc(128), acc(128), acc(128)],
        out_shape=[jax.ShapeDtypeStruct((t, 1024), f32), jax.ShapeDtypeStruct((t, 3072), f32),
                   jax.ShapeDtypeStruct((t, 128), f32), jax.ShapeDtypeStruct((8, 3072), f32),
                   jax.ShapeDtypeStruct((8, 128), f32), jax.ShapeDtypeStruct((8, 128), f32),
                   jax.ShapeDtypeStruct((8, 128), f32)],
        scratch=[pltpu.VMEM((72, 3072), f32), pltpu.VMEM((8, 3072), f32), pltpu.VMEM((8, 128, 128), f32)],
        args=[gate, qkv, pre, sm, ss, ts, do, conv_w, dtb, alog, nw, cs["tri"], cs["i64"], cs["strict"]])


def _chunk_call(parts, name, nc, reverse):
    n_in = [len(p["args"]) for p in parts]
    n_out = [len(p["out_shape"]) for p in parts]
    n_scr = [len(p["scratch"]) for p in parts]
    sub = SUB_BWD if reverse else SUB_FWD
    order = list(range(sub))[::-1] if reverse else list(range(sub))

    def view(ref, kind, s, refs):
        if kind == "rows":
            return ref.at[pl.ds(CHUNK * s, CHUNK)]
        if kind == "state":
            return ref.at[pl.ds(s, 1)]
        if kind == "full":
            return ref
        src = refs[kind[1]]
        return ref if s == 0 else src.at[pl.ds(CHUNK * s - 8, 8)]

    def body(*refs):
        ins, outs, scr = refs[:sum(n_in)], refs[sum(n_in):sum(n_in) + sum(n_out)], refs[sum(n_in) + sum(n_out):]
        for s in order:
            shared = {"first": (pl.program_id(0) == 0) if s == order[0] else False}
            for k, p in enumerate(parts):
                i0, o0, s0 = sum(n_in[:k]), sum(n_out[:k]), sum(n_scr[:k])
                p_ins = ins[i0:i0 + n_in[k]]
                p["body"](shared,
                          *[view(r, kd, s, p_ins) for r, kd in zip(p_ins, p["in_kinds"])],
                          *[view(r, kd, s, None) for r, kd in zip(outs[o0:o0 + n_out[k]], p["out_kinds"])],
                          *scr[s0:s0 + n_scr[k]])

    cat = lambda key: [v for p in parts for v in p[key]]
    return _pc(body, name=name, grid=(nc // sub,), in_specs=cat("in_specs"), out_specs=cat("out_specs"),
               out_shape=cat("out_shape"), scratch_shapes=cat("scratch"),
               compiler_params=_cparams(("arbitrary",)))(*cat("args"))


def out_fwd_bwd(x, tgt, y_ssd, y_gdn, w_out, fnw):
    t = x.shape[0]
    tm = min(512, t)
    f32 = jnp.float32

    def body(x_ref, tgt_ref, ys_ref, yg_ref, w_ref, fnw_ref,
             dout_ref, dys_ref, dyg_ref, gw_ref, gfnw_ref, loss_ref, gw_acc):
        i = pl.program_id(0)

        @pl.when(i == 0)
        def _():
            gw_acc[...] = jnp.zeros_like(gw_acc)
            gfnw_ref[...] = jnp.zeros_like(gfnw_ref)
            loss_ref[...] = jnp.zeros_like(loss_ref)

        ys = ys_ref[...]
        yg = yg_ref[...]
        out = x_ref[...] + jnp.dot(ys, w_ref[0:1024, :], preferred_element_type=f32) \
            + jnp.dot(yg, w_ref[1024:2048, :], preferred_element_type=f32)
        rstd = lax.rsqrt(jnp.mean(out * out, axis=-1, keepdims=True) + EPS)
        yhat = out * rstd
        fw = fnw_ref[...]
        e = yhat * fw - tgt_ref[...]
        loss_ref[...] += 0.5 * jnp.sum(jnp.sum(e * e, axis=-1, keepdims=True) * (1.0 / D_MODEL), axis=0, keepdims=True)
        dyf = e * (1.0 / D_MODEL)
        gfnw_ref[0:1, :] += jnp.sum(dyf * yhat, axis=0, keepdims=True)
        dyhat = dyf * fw
        dout = rstd * (dyhat - yhat * jnp.mean(dyhat * yhat, axis=-1, keepdims=True))
        dout_ref[...] = dout
        db = dout.astype(_MM)
        dys_ref[...] = lax.dot_general(db, w_ref[0:1024, :], (((1,), (1,)), ((), ())), preferred_element_type=f32)
        dyg_ref[...] = lax.dot_general(db, w_ref[1024:2048, :], (((1,), (1,)), ((), ())), preferred_element_type=f32)
        gw_acc[0:1024, :] += lax.dot_general(ys, db, (((0,), (0,)), ((), ())), preferred_element_type=f32)
        gw_acc[1024:2048, :] += lax.dot_general(yg, db, (((0,), (0,)), ((), ())), preferred_element_type=f32)

        @pl.when(i == steps - 1)
        def _():
            gw_ref[...] = gw_acc[...].astype(gw_ref.dtype)

    steps = t // tm
    blk = pl.BlockSpec((tm, D_MODEL), lambda i: (i, 0))
    return _pc(
        body, name="out_fwd_bwd", grid=(steps,),
        in_specs=[blk, blk, blk, blk, _full((MIX_WIDTH, D_MODEL)), _full((1, D_MODEL))],
        out_specs=[blk, blk, blk, _full((MIX_WIDTH, D_MODEL)), _full((8, D_MODEL)), _full((1, 128))],
        out_shape=[jax.ShapeDtypeStruct((t, D_MODEL), f32)] * 3 +
                  [jax.ShapeDtypeStruct((MIX_WIDTH, D_MODEL), _MM), jax.ShapeDtypeStruct((8, D_MODEL), f32),
                   jax.ShapeDtypeStruct((1, 128), f32)],
        scratch_shapes=[pltpu.VMEM((MIX_WIDTH, D_MODEL), f32)],
        compiler_params=_cparams(("arbitrary",)),
    )(x, tgt, y_ssd, y_gdn, w_out, fnw)


def inproj_bwd_dx(x, dout, norm_w, w_perm, dgroups, scattered):
    t = x.shape[0]
    tm = min(256, t)
    f32 = jnp.float32

    def body(x_ref, dout_ref, nw_ref, w_ref, dz_ref, dxbc_ref, dgate_ref, dqkv_ref, dsm_ref, dx_ref, gnw_ref):
        i = pl.program_id(0)

        @pl.when(i == 0)
        def _():
            gnw_ref[...] = jnp.zeros_like(gnw_ref)

        du = None
        for (name, c0, c1), d_ref in zip(GROUPS, (dz_ref, dxbc_ref, dgate_ref, dqkv_ref, dsm_ref)):
            term = jnp.dot(d_ref[...].astype(_MM), _w_rows(w_ref, name, c1 - c0), preferred_element_type=f32)
            du = term if du is None else du + term
        xf = x_ref[...]
        rstd = lax.rsqrt(jnp.mean(xf * xf, axis=-1, keepdims=True) + EPS)
        xhat = xf * rstd
        gnw_ref[0:1, :] += jnp.sum(du * xhat, axis=0, keepdims=True)
        dxh = du * nw_ref[...]
        dx_ref[...] = dout_ref[...] + rstd * (dxh - xhat * jnp.mean(dxh * xhat, axis=-1, keepdims=True))

    blk = lambda w: pl.BlockSpec((tm, w), lambda i: (i, 0))
    steps = t // tm
    kinds = ["scatter"] * len(scattered)
    hosted = _hosting(body, 9, 2, 0, kinds, lambda: pl.program_id(0) == 0, lambda: pl.program_id(0) == steps - 1)
    return _pc_comm(
        hosted, name="inproj_bwd_dx", grid=(steps,),
        in_specs=[blk(D_MODEL), blk(D_MODEL), _full((1, D_MODEL)), _full((W_GATHERED, D_MODEL))] +
                 [blk(c1 - c0) for _, c0, c1 in GROUPS] + [ANY] * len(scattered),
        out_specs=[blk(D_MODEL), _full((8, D_MODEL))] + [ANY] * len(scattered),
        out_shape=[jax.ShapeDtypeStruct((t, D_MODEL), f32), jax.ShapeDtypeStruct((8, D_MODEL), f32)] +
                  [_exchange_out_shape("scatter", a) for a in scattered],
        scratch_shapes=_exchange_sems(len(scattered)), compiler_params=_cparams(("arbitrary",)),
    )(x, dout, norm_w, w_perm, *dgroups, *scattered)


def grad_w_group(u, dg, name, scattered=()):
    t, n = dg.shape
    tn = n if n <= 1536 else 1024
    budget = 40 * 1024 * 1024
    tm = next((c for c in (4096, 2048, 1024, 512, 256)
               if t % c == 0 and tn * D_MODEL * 4 + 2 * (c * tn * 4 + c * D_MODEL * 2 + tn * D_MODEL * 2) <= budget), t)
    nj, nk = n // tn, t // tm
    f32 = jnp.float32

    def body(u_ref, d_ref, o_ref, acc):
        k = pl.program_id(1)

        @pl.when(k == 0)
        def _():
            acc[...] = jnp.zeros_like(acc)

        acc[...] += lax.dot_general(d_ref[...].astype(_MM), u_ref[...], (((0,), (0,)), ((), ())),
                                    preferred_element_type=f32)

        @pl.when(k == nk - 1)
        def _():
            o_ref[...] = acc[...].astype(o_ref.dtype)

    ne = len(scattered)
    hosted = _hosting(body, 2, 1, 1, ["scatter"] * ne,
                      lambda: (pl.program_id(0) == 0) & (pl.program_id(1) == 0),
                      lambda: (pl.program_id(0) == nj - 1) & (pl.program_id(1) == nk - 1))
    res = (_pc_comm if ne else _pc)(
        hosted, name=name, grid=(nj, nk),
        in_specs=[pl.BlockSpec((tm, D_MODEL), lambda j, k: (k, 0)),
                  pl.BlockSpec((tm, tn), lambda j, k: (k, j))] + [ANY] * ne,
        out_specs=[pl.BlockSpec((tn, D_MODEL), lambda j, k: (j, 0))] + [ANY] * ne,
        out_shape=[jax.ShapeDtypeStruct((n, D_MODEL), _MM)] + [_exchange_out_shape("scatter", a) for a in scattered],
        scratch_shapes=[pltpu.VMEM((tn, D_MODEL), f32)] + _exchange_sems(ne),
        compiler_params=_cparams(("arbitrary", "arbitrary")),
    )(u, dg, *scattered)
    return res if ne else res[0]


def grad_w_many(u, dgs, name):
    t = u.shape[0]
    widths = [d.shape[1] for d in dgs]
    tot, ng = sum(widths), len(dgs)
    f32 = jnp.float32
    budget = 48 * 1024 * 1024
    tm = next((c for c in (2048, 1024, 512, 256)
               if t % c == 0 and tot * D_MODEL * 4 + 2 * (c * tot * 4 + c * D_MODEL * 2 + tot * D_MODEL * 2) <= budget), t)
    nk = t // tm

    def body(*refs):
        u_ref, d_refs, o_refs, accs = refs[0], refs[1:1 + ng], refs[1 + ng:1 + 2 * ng], refs[1 + 2 * ng:]
        k = pl.program_id(0)

        @pl.when(k == 0)
        def _():
            for acc in accs:
                acc[...] = jnp.zeros_like(acc)

        uu = u_ref[...]
        for d_ref, acc in zip(d_refs, accs):
            acc[...] += lax.dot_general(d_ref[...].astype(_MM), uu, (((0,), (0,)), ((), ())),
                                        preferred_element_type=f32)

        @pl.when(k == nk - 1)
        def _():
            for o_ref, acc in zip(o_refs, accs):
                o_ref[...] = acc[...].astype(o_ref.dtype)

    return _pc(
        body, name=name, grid=(nk,),
        in_specs=[pl.BlockSpec((tm, D_MODEL), lambda k: (k, 0))] + [pl.BlockSpec((tm, n), lambda k: (k, 0)) for n in widths],
        out_specs=[pl.BlockSpec((n, D_MODEL), lambda k: (0, 0)) for n in widths],
        out_shape=[jax.ShapeDtypeStruct((n, D_MODEL), _MM) for n in widths],
        scratch_shapes=[pltpu.VMEM((n, D_MODEL), f32) for n in widths],
        compiler_params=_cparams(("arbitrary",)),
    )(u, *dgs)


def _pad_lanes(v, off):
    n = v.shape[-1]
    return jnp.pad(v.reshape(1, n).astype(jnp.float32), ((0, 0), (off, 128 - off - n)))


REF_ROWS = dict(z=(0, 1024), xbc=(1024, 2560), dt=(2560, 2576), gate=(2576, 3600), qkv=(3600, 6672), ab=(6672, 6688))


def unperm_w_in(gz, gxbc, ggate, gqkv, gsm):
    src = dict(z=gz, xbc=gxbc, dt=gsm[0:16], gate=ggate, qkv=gqkv, ab=gsm[16:32])
    slabs = []
    for k in range(N_DEV):
        a, b = k * W_IN_SHARD, (k + 1) * W_IN_SHARD
        parts = []
        for name, (s, e) in REF_ROWS.items():
            lo, hi = max(a, s), min(b, e)
            if lo < hi:
                parts.append(src[name][lo - s:hi - s])
        slabs.append(jnp.concatenate(parts, axis=0))
    return jnp.stack(slabs)


def all_gather(arrs, name):
    n = len(arrs)
    halves = []
    for arr in arrs:
        h = arr.shape[0] // 2 // 16 * 16
        halves.append(((0, h), (h, arr.shape[0] - h)))

    def body(*refs):
        ins, outs = refs[:n], refs[n:2 * n]
        send_sems, recv_sems, local_sems = refs[2 * n:]
        x, y, c, me = _me()
        sibling = (x, y, 1 - c)
        xn, yn, dg = (1 - x, y), (x, 1 - y), (1 - x, 1 - y)

        def idx(px, py, pc):
            return 4 * px + 2 * py + pc

        def copy(a, k, block, to, src=None, half=None):
            slot = outs[a].at[idx(*block)]
            if half is not None:
                slot = slot.at[pl.ds(*halves[a][half])]
            return pltpu.make_async_remote_copy(src_ref=slot if src is None else src, dst_ref=slot,
                                                send_sem=send_sems.at[a, k], recv_sem=recv_sems.at[a, k],
                                                device_id=to, device_id_type=MESH)

        local = [pltpu.make_async_copy(ins[a], outs[a].at[me], local_sems.at[a]) for a in range(n)]
        for cp in local:
            cp.start()
        started = []

        def start(cps):
            for cp in cps:
                cp.start()
            started.extend(cps)

        for a in range(n):
            start([copy(a, 1, (x, y, c), (*xn, c), src=ins[a]), copy(a, 2, (x, y, c), (*yn, c), src=ins[a]),
                   copy(a, 0, (x, y, c), sibling, src=ins[a])])
        for a in range(n):
            copy(a, 1, (*xn, c), (x, y, c)).wait_recv()
            start([copy(a, 3, (*xn, c), (*yn, c), half=0), copy(a, 5, (*xn, c), sibling)])
            copy(a, 2, (*yn, c), (x, y, c)).wait_recv()
            start([copy(a, 4, (*yn, c), (*xn, c), half=1), copy(a, 6, (*yn, c), sibling)])
        for a in range(n):
            copy(a, 3, (*dg, c), (x, y, c), half=0).wait_recv()
            copy(a, 4, (*dg, c), (x, y, c), half=1).wait_recv()
            start([copy(a, 7, (*dg, c), sibling)])
        for a in range(n):
            copy(a, 0, sibling, (x, y, c)).wait_recv()
            for j, chip in enumerate((xn, yn, dg)):
                copy(a, 5 + j, (*chip, 1 - c), (x, y, c)).wait_recv()
        for cp in started:
            cp.wait_send()
        for cp in local:
            cp.wait()

    return _pc_comm(
        body, name=name, in_specs=[ANY] * n, out_specs=[ANY] * n,
        out_shape=[jax.ShapeDtypeStruct((N_DEV,) + a.shape, a.dtype) for a in arrs],
        scratch_shapes=[pltpu.SemaphoreType.DMA((n, 8)), pltpu.SemaphoreType.DMA((n, 8)),
                        pltpu.SemaphoreType.DMA((n,))],
    )(*arrs)


def adamw_sum(recv, w, m, v, rows, name, cols=None):
    r, ccols = w.shape
    f32 = jnp.float32
    c1 = 1.0 / (1.0 - ADAM_B1 ** ADAM_STEP)
    c2 = 1.0 / (1.0 - ADAM_B2 ** ADAM_STEP)

    def body(recv_ref, w_ref, m_ref, v_ref, g_ref, d_ref, mo_ref, vo_ref):
        g = recv_ref[0].astype(f32)
        for k in range(1, N_DEV):
            g = g + recv_ref[k].astype(f32)
        mn = ADAM_B1 * m_ref[...] + (1.0 - ADAM_B1) * g
        vn = ADAM_B2 * v_ref[...] + (1.0 - ADAM_B2) * (g * g)
        g_ref[...] = g
        mo_ref[...] = mn
        vo_ref[...] = vn
        d_ref[...] = -ADAM_LR * ((mn * c1) / (jnp.sqrt(vn * c2) + ADAM_EPS) + ADAM_WD * w_ref[...])

    if cols is None:
        blk = pl.BlockSpec((rows, ccols), lambda i: (i, 0))
        rblk, steps = pl.BlockSpec((N_DEV, rows, ccols), lambda i: (0, i, 0)), r // rows
    else:
        blk = pl.BlockSpec((r, cols), lambda i: (0, i))
        rblk, steps = pl.BlockSpec((N_DEV, r, cols), lambda i: (0, 0, i)), ccols // cols
    return _pc(
        body, name=name, grid=(steps,),
        in_specs=[rblk, blk, blk, blk],
        out_specs=[blk] * 4, out_shape=[jax.ShapeDtypeStruct((r, ccols), f32)] * 4,
        compiler_params=_cparams(("arbitrary",)),
    )(recv, w, m, v)


SMALL = (("norm_w", 1, 1024, 0), ("ssd_conv_b", 1, 1536, 0), ("ssd_dt_bias", 1, 16, 0), ("ssd_a_log", 1, 16, 0),
         ("ssd_d", 1, 16, 0), ("ssd_norm_w", 1, 1024, 0), ("gdn_dt_bias", 1, 8, 16), ("gdn_a_log", 1, 8, 16),
         ("gdn_norm_w", 1, 128, 0), ("final_norm_w", 1, 1024, 0),
         ("ssd_conv_w", CONV_K, SSD_CONV_DIM // N_DEV, 0), ("gdn_conv_w", CONV_K, GDN_CONV_DIM // N_DEV, 0))


def _small_layout():
    out, off = [], 0
    for name, rows, n, lane0 in SMALL + (("loss", 1, 128, 0),):
        stride = -(-(lane0 + n) // 128) * 128
        out.append((name, rows, n, lane0, stride, off))
        off += rows * stride
    return out, off


def scatter_small(accs):
    layout, total = _small_layout()
    f32 = jnp.float32

    def body(*refs):
        acc_refs, out_ref, slabs = refs[:len(layout)], refs[len(layout)], refs[len(layout) + 1]
        sems = refs[len(layout) + 2:]
        slabs[...] = jnp.zeros_like(slabs)
        for (name, rows, n, lane0, stride, off), acc in zip(layout, acc_refs):
            for k in range(N_DEV):
                if rows == 1:
                    slabs[k, :, off:off + stride] = acc[0:1, 0:stride]
                else:
                    for j in range(rows):
                        slabs[k, :, off + stride * j:off + stride * j + n] = acc[j:j + 1, n * k:n * k + n]
        start, wait = _exchange_ops("scatter", slabs, out_ref, *sems)
        start()
        wait()

    return _pc_comm(
        body, name="scatter_small_grads", out_specs=ANY, out_shape=jax.ShapeDtypeStruct((N_DEV, 1, total), f32),
        scratch_shapes=[pltpu.VMEM((N_DEV, 1, total), f32)] + _exchange_sems(1),
    )(*accs)


def adamw_small(recv, w, m, v):
    layout, total = _small_layout()
    loss_off = layout[-1][5]
    layout = layout[:-1]
    f32 = jnp.float32
    c1 = 1.0 / (1.0 - ADAM_B1 ** ADAM_STEP)
    c2 = 1.0 / (1.0 - ADAM_B2 ** ADAM_STEP)
    np_ = len(layout)

    def body(*refs):
        recv_ref = refs[0]
        w_refs, m_refs, v_refs = refs[1:1 + np_], refs[1 + np_:1 + 2 * np_], refs[1 + 2 * np_:1 + 3 * np_]
        o_refs = refs[1 + 3 * np_:]
        g_all = recv_ref[0]
        for k in range(1, N_DEV):
            g_all = g_all + recv_ref[k]
        o_refs[4 * np_][...] = g_all[:, loss_off:loss_off + 128]

        def update(g, wv, mv, vv):
            mn = ADAM_B1 * mv + (1.0 - ADAM_B1) * g
            vn = ADAM_B2 * vv + (1.0 - ADAM_B2) * (g * g)
            return g, -ADAM_LR * ((mn * c1) / (jnp.sqrt(vn * c2) + ADAM_EPS) + ADAM_WD * wv), mn, vn

        for p, (name, rows, n, lane0, stride, off) in enumerate(layout):
            outs = o_refs[4 * p:4 * p + 4]
            if rows == 1:
                res = update(g_all[:, off + lane0:off + lane0 + n], w_refs[p][...], m_refs[p][...], v_refs[p][...])
                for o, r in zip(outs, res):
                    o[...] = r
            else:
                for j in range(rows):
                    res = update(g_all[:, off + stride * j:off + stride * j + n], w_refs[p][0, j:j + 1, :],
                                 m_refs[p][0, j:j + 1, :], v_refs[p][0, j:j + 1, :])
                    for o, r in zip(outs, res):
                        o[0, j:j + 1, :] = r

    names = [e[0] for e in layout]
    ins = [recv] + [d[nm] for d in (w, m, v) for nm in names]
    out_shape = [jax.ShapeDtypeStruct(w[nm].shape, f32) for nm in names for _ in range(4)]
    out_shape.append(jax.ShapeDtypeStruct((1, 128), f32))
    res = _pc(body, name="adamw_small", out_shape=out_shape)(*ins)
    return {nm: tuple(res[4 * p:4 * p + 4]) for p, nm in enumerate(names)}, res[4 * np_]


SHARD = (("ssd_conv_w", CONV_K * SSD_CONV_DIM // N_DEV), ("gdn_conv_w", CONV_K * GDN_CONV_DIM // N_DEV))
SHARD_ROWS = 24


def _rows_of(size):
    return -(-size // 128)


def _pack(vals, layout, total_rows):
    parts = []
    for (name, size), val in zip(layout, vals):
        flat = val.reshape(-1).astype(jnp.float32)
        parts.append(jnp.pad(flat, (0, _rows_of(size) * 128 - size)).reshape(-1, 128))
    used = sum(_rows_of(s) for _, s in layout)
    parts.append(jnp.zeros((total_rows - used, 128), jnp.float32))
    return jnp.concatenate(parts, axis=0)


def _conv_full(gathered_flat, ccols):
    return gathered_flat.reshape(N_DEV, CONV_K, ccols // N_DEV).transpose(1, 0, 2).reshape(CONV_K, ccols)


def kernel(x, norm_w, w_in, ssd_conv_w, ssd_conv_b, ssd_dt_bias, ssd_a_log, ssd_d, ssd_norm_w, gdn_conv_w, gdn_dt_bias, gdn_a_log, gdn_norm_w, w_out, final_norm_w, loss_target, m_norm_w, m_w_in, m_ssd_conv_w, m_ssd_conv_b, m_ssd_dt_bias, m_ssd_a_log, m_ssd_d, m_ssd_norm_w, m_gdn_conv_w, m_gdn_dt_bias, m_gdn_a_log, m_gdn_norm_w, m_w_out, m_final_norm_w, v_norm_w, v_w_in, v_ssd_conv_w, v_ssd_conv_b, v_ssd_dt_bias, v_ssd_a_log, v_ssd_d, v_ssd_norm_w, v_gdn_conv_w, v_gdn_dt_bias, v_gdn_a_log, v_gdn_norm_w, v_w_out, v_final_norm_w):
    f32 = jnp.float32
    w = dict(norm_w=norm_w, w_in=w_in, ssd_conv_w=ssd_conv_w, ssd_conv_b=ssd_conv_b, ssd_dt_bias=ssd_dt_bias,
             ssd_a_log=ssd_a_log, ssd_d=ssd_d, ssd_norm_w=ssd_norm_w, gdn_conv_w=gdn_conv_w, gdn_dt_bias=gdn_dt_bias,
             gdn_a_log=gdn_a_log, gdn_norm_w=gdn_norm_w, w_out=w_out, final_norm_w=final_norm_w)
    m = dict(norm_w=m_norm_w, w_in=m_w_in, ssd_conv_w=m_ssd_conv_w, ssd_conv_b=m_ssd_conv_b, ssd_dt_bias=m_ssd_dt_bias,
             ssd_a_log=m_ssd_a_log, ssd_d=m_ssd_d, ssd_norm_w=m_ssd_norm_w, gdn_conv_w=m_gdn_conv_w,
             gdn_dt_bias=m_gdn_dt_bias, gdn_a_log=m_gdn_a_log, gdn_norm_w=m_gdn_norm_w, w_out=m_w_out,
             final_norm_w=m_final_norm_w)
    v = dict(norm_w=v_norm_w, w_in=v_w_in, ssd_conv_w=v_ssd_conv_w, ssd_conv_b=v_ssd_conv_b, ssd_dt_bias=v_ssd_dt_bias,
             ssd_a_log=v_ssd_a_log, ssd_d=v_ssd_d, ssd_norm_w=v_ssd_norm_w, gdn_conv_w=v_gdn_conv_w,
             gdn_dt_bias=v_gdn_dt_bias, gdn_a_log=v_gdn_a_log, gdn_norm_w=v_gdn_norm_w, w_out=v_w_out,
             final_norm_w=v_final_norm_w)
    names = list(w)
    shapes = {n: w[n].shape for n in names}

    xl, tgt = x[0], loss_target[0]
    cs = _consts()
    dtb_s = _pad_lanes(ssd_dt_bias, 0)
    alog_s = _pad_lanes(ssd_a_log, 0)
    dpar = _pad_lanes(ssd_d, 0)
    dtb_g = _pad_lanes(gdn_dt_bias, 16)
    alog_g = _pad_lanes(gdn_a_log, 16)
    nw_g = gdn_norm_w.reshape(1, 128)
    nw_s = ssd_norm_w.reshape(1, 1024)
    cb_s = ssd_conv_b.reshape(1, 1536)
    nw1 = norm_w.reshape(1, D_MODEL)

    w_slot = lax.dynamic_slice(jnp.pad(w_in[0].T.astype(_MM), ((12, 12), (0, 0))),
                               (12 - (W_IN_SHARD * _me()[3]) % 16, 0), (W_SLOT, D_MODEL))
    (g_w_in,) = all_gather([w_slot], "gather_w_in")
    w_perm = g_w_in.reshape(W_GATHERED, D_MODEL)
    conv_pack = _pack([w["ssd_conv_w"], w["gdn_conv_w"]], SHARD, SHARD_ROWS)
    u, z, xbc, gate, qkv, sm, g_w_out, g_conv = inproj_fwd(xl, nw1, w_perm, [w_out[0].astype(_MM), conv_pack])
    w_out_full = g_w_out.reshape(MIX_WIDTH, D_MODEL)
    ssd_cw = _conv_full(g_conv[:, 0:6].reshape(N_DEV, -1), SSD_CONV_DIM)
    gdn_cw = _conv_full(g_conv[:, 6:18].reshape(N_DEV, -1), GDN_CONV_DIM)

    nc = xl.shape[0] // CHUNK
    y_ssd, hs, pre_s, y_gdn, ss, ts, pre_g = _chunk_call(
        [ssd_fwd(z, xbc, sm, ssd_cw, cb_s, dtb_s, alog_s, dpar, nw_s, cs),
         gdn_fwd(gate, qkv, sm, gdn_cw, dtb_g, alog_g, nw_g, cs)], "scan_fwd", nc, False)
    dout, dys, dyg, g_wout, g_fnw, loss_l = out_fwd_bwd(xl, tgt, y_ssd, y_gdn, w_out_full,
                                                        final_norm_w.reshape(1, D_MODEL))
    (dz, dxbc, g_cw_s, g_cb_s, g_dtb_s, g_alog_s, g_d, g_nw_s,
     dgate, dqkv, dsm, g_cw_g, g_dtb_g, g_alog_g, g_nw_g) = _chunk_call(
        [ssd_bwd(z, xbc, pre_s, sm, hs, dys, ssd_cw, dtb_s, alog_s, dpar, nw_s, cs),
         gdn_bwd(gate, qkv, pre_g, sm, ss, ts, dyg, gdn_cw, dtb_g, alog_g, nw_g, cs)], "scan_bwd", nc, True)

    t_w_out = g_wout.reshape(N_DEV, MIX_WIDTH // N_DEV, D_MODEL)
    gws = dict(zip(("z", "sm"), grad_w_many(u, [dz, dsm], "grad_w_in_z_sm")))
    gws["xbc"] = grad_w_group(u, dxbc, "grad_w_in_xbc")
    gws["gate"] = grad_w_group(u, dgate, "grad_w_in_gate")
    gws["qkv"], r_w_out = grad_w_group(u, dqkv, "grad_w_in_qkv", [t_w_out])
    t_w_in = unperm_w_in(gws["z"], gws["xbc"], gws["gate"], gws["qkv"], gws["sm"])
    dx, g_nw, r_w_in = inproj_bwd_dx(xl, dout, nw1, w_perm, (dz, dxbc, dgate, dqkv, dsm), [t_w_in])

    accs = dict(norm_w=g_nw, ssd_conv_b=g_cb_s, ssd_dt_bias=g_dtb_s, ssd_a_log=g_alog_s, ssd_d=g_d,
                ssd_norm_w=g_nw_s, gdn_dt_bias=g_dtb_g, gdn_a_log=g_alog_g, gdn_norm_w=g_nw_g, final_norm_w=g_fnw,
                ssd_conv_w=g_cw_s, gdn_conv_w=g_cw_g)
    r_small = scatter_small([accs[e[0]] for e in SMALL] + [loss_l])

    o_w_in = adamw_sum(r_w_in, w_in[0].T, m_w_in[0].T, v_w_in[0].T, None, "adamw_w_in", cols=256)
    o_w_out = adamw_sum(r_w_out, w_out[0], m_w_out[0], v_w_out[0], 64, "adamw_w_out")
    row = lambda d: {n: (a.reshape(1, -1) if a.ndim == 1 else a) for n, a in d.items()}
    o_small, loss_sum = adamw_small(r_small, row(w), row(m), row(v))

    loss = loss_sum[0, 0]
    outs = [loss, dx[None]]
    for k in range(4):
        parts = {n: o_small[n][k] for n in o_small}
        parts["w_in"] = o_w_in[k].T
        parts["w_out"] = o_w_out[k]
        outs += [parts[n].reshape(shapes[n]) for n in names]
    return tuple(outs)
```

```python
import functools

import jax
import jax.numpy as jnp
import numpy as np
from jax import lax
from jax.experimental import pallas as pl
from jax.experimental.pallas import tpu as pltpu

_MM = jnp.bfloat16

D_MODEL = 1024
CHUNK = 64
CONV_K = 4
EPS = 1e-6
SSD_CONV_DIM = 1536
GDN_HEADS = 8
GDN_DK = 128
GDN_CONV_DIM = 3072
MIX_WIDTH = 2048
IN_DIM = 6688
N_DEV = 8
W_IN_SHARD = IN_DIM // N_DEV
HI = lax.Precision.HIGHEST
HIGH = lax.Precision.HIGH
VMEM_LIMIT = 56 * 1024 * 1024

ADAM_LR = 0.001
ADAM_B1 = 0.9
ADAM_B2 = 0.999
ADAM_EPS = 1e-08
ADAM_WD = 0.01
ADAM_STEP = 10


def _pc(body, **kw):
    return pl.pallas_call(body, **kw)


def _pc_comm(body, **kw):
    return pl.pallas_call(body, **kw)


def _cparams(sem):
    return pltpu.CompilerParams(dimension_semantics=sem, vmem_limit_bytes=VMEM_LIMIT)


def _sig(x):
    return 0.5 * jnp.tanh(0.5 * x) + 0.5


@jax.custom_vjp
def _sigmoid(x):
    return _sig(x)


def _sigmoid_fwd(x):
    s = _sig(x)
    return s, s


def _sigmoid_bwd(s, g):
    return (g * s * (1.0 - s),)


_sigmoid.defvjp(_sigmoid_fwd, _sigmoid_bwd)


@jax.custom_vjp
def _silu(x):
    return x * _sig(x)


def _silu_fwd(x):
    s = _sig(x)
    return x * s, (x, s)


def _silu_bwd(res, g):
    x, s = res
    return (g * (s * (1.0 + x * (1.0 - s))),)


_silu.defvjp(_silu_fwd, _silu_bwd)


def _softplus_impl(x):
    return jnp.maximum(x, 0.0) + jnp.log(1.0 + jnp.exp(-jnp.abs(x)))


@jax.custom_vjp
def _softplus(x):
    return _softplus_impl(x)


def _softplus_fwd(x):
    return _softplus_impl(x), x


def _softplus_bwd(x, g):
    return (g * _sig(x),)


_softplus.defvjp(_softplus_fwd, _softplus_bwd)


def _lane_bcast_impl(x, k):
    return jnp.broadcast_to(x[..., k:k + 1], x.shape)


@functools.partial(jax.custom_vjp, nondiff_argnums=(1,))
def _lane_bcast(x, k):
    return _lane_bcast_impl(x, k)


def _lane_bcast_fwd(x, k):
    return _lane_bcast_impl(x, k), None


def _lane_bcast_bwd(k, _, g):
    lane = lax.broadcasted_iota(jnp.int32, g.shape, g.ndim - 1)
    return (jnp.where(lane == k, jnp.sum(g, axis=-1, keepdims=True), 0.0),)


_lane_bcast.defvjp(_lane_bcast_fwd, _lane_bcast_bwd)


def _mm(a, b):
    return jnp.dot(a.astype(_MM), b.astype(_MM), preferred_element_type=jnp.float32)


def _mm_nt(a, b):
    return lax.dot_general(a.astype(_MM), b.astype(_MM), (((1,), (1,)), ((), ())),
                           preferred_element_type=jnp.float32)


def _mm_tn(a, b):
    return lax.dot_general(a.astype(_MM), b.astype(_MM), (((0,), (0,)), ((), ())),
                           preferred_element_type=jnp.float32)


def _dot_hi(a, b):
    return jnp.dot(a, b, precision=HI, preferred_element_type=jnp.float32)


def _bmm(a, b):
    return lax.dot_general(a.astype(_MM), b.astype(_MM), (((2,), (1,)), ((0,), (0,))),
                           preferred_element_type=jnp.float32)


def _bmm_nt(a, b):
    return lax.dot_general(a.astype(_MM), b.astype(_MM), (((2,), (2,)), ((0,), (0,))),
                           preferred_element_type=jnp.float32)


def _bmm_tn(a, b):
    return lax.dot_general(a.astype(_MM), b.astype(_MM), (((1,), (1,)), ((0,), (0,))),
                           preferred_element_type=jnp.float32)


def _bmm_hi(a, b):
    return lax.dot_general(a, b, (((2,), (1,)), ((0,), (0,))), precision=HIGH, preferred_element_type=jnp.float32)


def _bmm_nt_hi(a, b):
    return lax.dot_general(a, b, (((2,), (2,)), ((0,), (0,))), precision=HIGH, preferred_element_type=jnp.float32)


def _bmm_tn_hi(a, b):
    return lax.dot_general(a, b, (((1,), (1,)), ((0,), (0,))), precision=HIGH, preferred_element_type=jnp.float32)


def _consts():
    l = np.arange(CHUNK)
    tri = (l[:, None] >= l[None, :]).astype(np.float32)
    lane = np.arange(128)
    i2 =(l[:, None] == (lane[None, :] % 64)).astype(np.float32)
    mask2 = (l[:, None] >= (lane[None, :] % 64)).astype(np.float32)
    lo = (lane < 64).astype(np.float32)[None, :]
    i64 = np.eye(CHUNK, dtype=np.float32)
    strict = (l[:, None] > l[None, :]).astype(np.float32)
    return dict(tri=jnp.asarray(tri), i2=jnp.asarray(i2), mask2=jnp.asarray(mask2), lo=jnp.asarray(lo),
                i64=jnp.asarray(i64), strict=jnp.asarray(strict))


def _ssd_chunk(xs_pre, b_pre, c_pre, z, sm, ht, dtb, alog, dpar, nw, tri, i2, mask2, lo):
    lane = lax.broadcasted_iota(jnp.int32, (1, 128), 1)
    m16 = lane < 16
    dt = jnp.where(m16, _softplus(sm + dtb), 0.0)
    a_neg = -jnp.exp(alog)
    cum = _dot_hi(tri, dt * a_neg)
    row = lax.broadcasted_iota(jnp.int32, (CHUNK, 1), 0)
    is_last = row == CHUNK - 1
    hi = 1.0 - lo
    bm = [_silu(b) for b in b_pre]
    cm = [_silu(c) for c in c_pre]
    cb2 = [_mm_nt(cm[g], jnp.concatenate([bm[g], bm[g]], axis=0)) for g in range(2)]
    ht_g = [jnp.concatenate(ht[4 * g:4 * g + 4], axis=1) for g in range(2)]
    yoff_g = [_mm(cm[g], ht_g[g]) for g in range(2)]
    yg, xdec, clast = [], [], []
    for j in range(8):
        g, k4 = j // 4, j % 4
        pair = lambda v, j=j: jnp.where(lo > 0.5, _lane_bcast(v, 2 * j), _lane_bcast(v, 2 * j + 1))
        xs = _silu(xs_pre[j])
        dte = pair(dt)
        cume = pair(cum)
        cum_last = jnp.sum(jnp.where(is_last, cume, 0.0), axis=0, keepdims=True)
        xdt = xs * dte
        rowv = jnp.sum(cume * i2, axis=0, keepdims=True)
        lm = jnp.exp(jnp.where(mask2 > 0.5, cume - rowv, -jnp.inf))
        m = cb2[g] * lm
        xblk = jnp.concatenate([xdt * lo, xdt * hi], axis=0)
        y = _mm(m, xblk)
        y = y + yoff_g[g][:, 128 * k4:128 * k4 + 128] * jnp.exp(cume)
        y = y + pair(dpar) * xs
        yg.append(y * _silu(z[j]))
        xdec.append(xdt * jnp.exp(cum_last - cume))
        clast.append(cum_last)
    ht_next = []
    for g in range(2):
        st = _mm_tn(bm[g], jnp.concatenate(xdec[4 * g:4 * g + 4], axis=1))
        for k4 in range(4):
            j = 4 * g + k4
            ht_next.append(ht[j] * jnp.exp(clast[j]) + st[:, 128 * k4:128 * k4 + 128])
    outs = []
    for g in range(2):
        ss = sum(jnp.sum(yg[j] * yg[j], axis=-1, keepdims=True) for j in range(4 * g, 4 * g + 4))
        rs = lax.rsqrt(ss * (1.0 / 512.0) + EPS)
        for j in range(4 * g, 4 * g + 4):
            outs.append(yg[j] * rs * nw[j])
    return outs, ht_next


def _tri_inverse(a):
    eye = jnp.eye(CHUNK, dtype=jnp.float32)[None]
    p = eye - a
    x = _bmm_hi(a, a)
    for i in range(4):
        both = (_bmm_hi if i == 0 else _bmm)(jnp.concatenate([p, x], axis=1), x)
        p = p + both[:, :CHUNK]
        x = both[:, CHUNK:]
    return p + _bmm(p, x)


def _solve_apply(t, r1, r2):
    both = _bmm_hi(t, jnp.concatenate([r1, r2], axis=-1))
    n = r1.shape[-1]
    return both[..., :n], both[..., n:]


@jax.custom_vjp
def _solve(a, r1, r2, t):
    return _solve_apply(t, r1, r2)


def _solve_fwd(a, r1, r2, t):
    u, w = _bmm_hi(t, r1), _bmm_hi(t, r2)
    return (u, w), (t, u, w)


def _solve_bwd(res, cts):
    t, u, w = res
    du, dw = cts
    dr1 = _bmm_tn_hi(t, du)
    dr2 = _bmm_tn_hi(t, dw)
    da = -(_bmm_nt_hi(dr1, u) + _bmm_nt_hi(dr2, w))
    return da, dr1, dr2, jnp.zeros_like(t)


_solve.defvjp(_solve_fwd, _solve_bwd)


def _gdn_chunk(q_pre, k_pre, v_pre, gate, sm, s, dtb, alog, nw, tri, i64, strict, t_in=None):
    lane = lax.broadcasted_iota(jnp.int32, (1, 128), 1)
    m_a = (lane >= 16) & (lane < 24)
    g_full = jnp.where(m_a, -jnp.exp(alog) * _softplus(sm + dtb), 0.0)
    gc = _dot_hi(tri, g_full)
    sig = _sigmoid(sm)
    heads = lambda f: jnp.concatenate([f(h)[None] for h in range(GDN_HEADS)], axis=0)
    gc3 = heads(lambda h: _lane_bcast(gc, 16 + h))
    beta3 = heads(lambda h: _lane_bcast(sig, 24 + h))
    q = _silu(q_pre)
    q = q * lax.rsqrt(jnp.sum(q * q, axis=-1, keepdims=True) + EPS) * (GDN_DK ** -0.5)
    k = _silu(k_pre)
    k = k * lax.rsqrt(jnp.sum(k * k, axis=-1, keepdims=True) + EPS)
    v = _silu(v_pre)
    gcl = gc3[:, :, :CHUNK]
    gc_row = jnp.sum(gcl * i64[None], axis=1, keepdims=True)
    incl = (strict + i64)[None] > 0.5
    decay = jnp.exp(jnp.where(incl, gcl - gc_row, -jnp.inf))
    kb = k * beta3
    a = jnp.where(strict[None] > 0.5, _bmm_nt(kb, k) * decay, 0.0)
    egc = jnp.exp(gc3)
    t = _tri_inverse(a) if t_in is None else t_in
    u, w = _solve(a, v * beta3, kb * egc, t)
    attn = _bmm_nt(q, k) * decay
    row = lax.broadcasted_iota(jnp.int32, (1, CHUNK, 1), 1)
    gl = jnp.sum(jnp.where(row == CHUNK - 1, gc3, 0.0), axis=1, keepdims=True)
    q_dec = q * egc
    k_dec = k * jnp.exp(gl - gc3)
    ws = _bmm(jnp.concatenate([w, q_dec], axis=1), s)
    v_new = u - ws[:, :CHUNK]
    o = ws[:, CHUNK:] + _bmm(attn, v_new)
    s_next = s * jnp.exp(gl) + _bmm_tn(k_dec, v_new)
    on = o * lax.rsqrt(jnp.mean(o * o, axis=-1, keepdims=True) + EPS) * nw
    return on * _silu(gate), s_next, t


def _conv_fwd(pbuf, w_ref, c0, c1):
    blk = pbuf[:, c0:c1]
    acc = w_ref[CONV_K - 1:CONV_K, c0:c1] * blk[8:72]
    for j in range(CONV_K - 1):
        acc = acc + w_ref[j:j + 1, c0:c1] * pltpu.roll(blk, CONV_K - 1 - j, axis=0)[8:72]
    return acc


MESH = pl.DeviceIdType.MESH
ANY = pl.BlockSpec(memory_space=pl.ANY)


def _me():
    x, y, c = lax.axis_index("x"), lax.axis_index("y"), lax.axis_index("c")
    return x, y, c, 4 * x + 2 * y + c


def _peer(r):
    x, y, c, _ = _me()
    px = 1 - x if r & 4 else x
    py = 1 - y if r & 2 else y
    pc = 1 - c if r & 1 else c
    return (px, py, pc), 4 * px + 2 * py + pc


def _exchange_ops(kind, in_ref, out_ref, send_sems, recv_sems, local_sem):
    me = _me()[3]
    local = pltpu.make_async_copy(in_ref.at[me] if kind == "scatter" else in_ref, out_ref.at[me], local_sem)
    sends, recvs = [], []
    for r in range(1, N_DEV):
        peer, pidx = _peer(r)
        src = in_ref.at[pidx] if kind == "scatter" else in_ref
        sems = dict(send_sem=send_sems.at[r - 1], recv_sem=recv_sems.at[r - 1], device_id=peer, device_id_type=MESH)
        sends.append(pltpu.make_async_remote_copy(src_ref=src, dst_ref=out_ref.at[me], **sems))
        recvs.append(pltpu.make_async_remote_copy(src_ref=src, dst_ref=out_ref.at[pidx], **sems))

    def start():
        local.start()
        for cp in sends:
            cp.start()

    def wait():
        for cp in recvs:
            cp.wait_recv()
        for cp in sends:
            cp.wait_send()
        local.wait()

    return start, wait


def _exchange_sems(n):
    return [pltpu.SemaphoreType.DMA((N_DEV - 1,)), pltpu.SemaphoreType.DMA((N_DEV - 1,)),
            pltpu.SemaphoreType.DMA(())] * n


def _exchange_out_shape(kind, a):
    return jax.ShapeDtypeStruct(a.shape if kind == "scatter" else (N_DEV,) + a.shape, a.dtype)


def _hosting(body, n_in, n_out, n_scratch, kinds, first, last):
    ne = len(kinds)

    def wrapped(*refs):
        ins, ex_in = refs[:n_in], refs[n_in:n_in + ne]
        o0 = n_in + ne
        outs, ex_out = refs[o0:o0 + n_out], refs[o0 + n_out:o0 + n_out + ne]
        s0 = o0 + n_out + ne
        scr, sems = refs[s0:s0 + n_scratch], refs[s0 + n_scratch:]
        ops = [_exchange_ops(kinds[e], ex_in[e], ex_out[e], *sems[3 * e:3 * e + 3]) for e in range(ne)]

        @pl.when(first())
        def _():
            for start, _ in ops:
                start()

        body(*ins, *outs, *scr)

        @pl.when(last())
        def _():
            for _, wait in ops:
                wait()

    return wrapped


GROUPS = (("z", 0, 1024), ("xbc", 1024, 2560), ("gate", 2560, 3584), ("qkv", 3584, 6656), ("sm", 6656, 6784))
GROUP_ROWS = dict(z=((0, 1024),), xbc=((1024, 2560),), gate=((2576, 3600),), qkv=((3600, 6672),),
                  sm=((2560, 2576), (6672, 6688)))


W_SLOT = 848
W_GATHERED = N_DEV * W_SLOT


def _slot_pos(s, c):
    return W_SLOT * s + (c - W_IN_SHARD * s) + (W_IN_SHARD * s) % 16


def _w_pieces(w_ref, a, b):
    out, cur = [], a
    while cur < b:
        s = cur // W_IN_SHARD
        end = W_IN_SHARD * (s + 1)
        if end >= b:
            out.append(w_ref[_slot_pos(s, cur):_slot_pos(s, b), :])
            break
        hi = end // 16 * 16
        if hi > cur:
            out.append(w_ref[_slot_pos(s, cur):_slot_pos(s, hi), :])
        if end % 16:
            p, q = _slot_pos(s, hi), W_SLOT * (s + 1)
            out.append(w_ref[p:p + 16, :] + w_ref[q:q + 16, :])
            cur = hi + 16
        else:
            cur = hi
    return out


def _w_rows(w_ref, name, width):
    pieces = [p for a, b in GROUP_ROWS[name] for p in _w_pieces(w_ref, a, b)]
    n = sum(b - a for a, b in GROUP_ROWS[name])
    if n < width:
        pieces.append(jnp.zeros((width - n, D_MODEL), w_ref.dtype))
    return pieces[0] if len(pieces) == 1 else jnp.concatenate(pieces, axis=0)


def inproj_fwd(x, norm_w, w_perm, gathered):
    t = x.shape[0]
    tm = min(512, t)
    steps = t // tm
    kinds = ["gather"] * len(gathered)

    def body(x_ref, nw_ref, w_ref, u_ref, z_ref, xbc_ref, gate_ref, qkv_ref, sm_ref):
        xf = x_ref[...]
        rstd = lax.rsqrt(jnp.mean(xf * xf, axis=-1, keepdims=True) + EPS)
        u = (xf * rstd * nw_ref[...]).astype(_MM)
        u_ref[...] = u
        for (name, c0, c1), o_ref in zip(GROUPS, (z_ref, xbc_ref, gate_ref, qkv_ref, sm_ref)):
            o_ref[...] = lax.dot_general(u, _w_rows(w_ref, name, c1 - c0), (((1,), (1,)), ((), ())),
                                         preferred_element_type=jnp.float32)

    outs = [jax.ShapeDtypeStruct((t, D_MODEL), _MM)] + [jax.ShapeDtypeStruct((t, c1 - c0), jnp.float32)
                                                        for _, c0, c1 in GROUPS]
    hosted = _hosting(body, 3, 6, 0, kinds, lambda: pl.program_id(0) == 0, lambda: pl.program_id(0) == steps - 1)
    return _pc_comm(
        hosted, name="inproj_fwd", grid=(steps,),
        in_specs=[pl.BlockSpec((tm, D_MODEL), lambda i: (i, 0)),
                  pl.BlockSpec((1, D_MODEL), lambda i: (0, 0)),
                  pl.BlockSpec((W_GATHERED, D_MODEL), lambda i: (0, 0), pipeline_mode=pl.Buffered(1))] +
                 [ANY] * len(gathered),
        out_specs=[pl.BlockSpec((tm, D_MODEL), lambda i: (i, 0))] +
                  [pl.BlockSpec((tm, c1 - c0), lambda i: (i, 0)) for _, c0, c1 in GROUPS] + [ANY] * len(gathered),
        out_shape=outs + [_exchange_out_shape("gather", a) for a in gathered],
        scratch_shapes=_exchange_sems(len(gathered)), compiler_params=_cparams(("arbitrary",)),
    )(x, norm_w, w_perm, *gathered)


SUB_FWD = 4
SUB_BWD = 2


def _halo_spec(width, idx_fn):
    return pl.BlockSpec((8, width), lambda i: (jnp.maximum(idx_fn(i) * (SUB_FWD * CHUNK // 8) - 1, 0), 0))


def _when_first(shared, fn):
    if shared["first"] is not False:
        pl.when(shared["first"])(fn)


def _full(shape):
    nd = len(shape)
    return pl.BlockSpec(shape, lambda i: (0,) * nd)


def _ssd_split(pre_fn, z_ref, sm_ref):
    xs_pre = [pre_fn(128 * j, 128 * j + 128) for j in range(8)]
    b_pre = [pre_fn(1024 + 128 * g, 1152 + 128 * g) for g in range(2)]
    c_pre = [pre_fn(1280 + 128 * g, 1408 + 128 * g) for g in range(2)]
    z = [z_ref[:, 128 * j:128 * j + 128] for j in range(8)]
    return xs_pre, b_pre, c_pre, z, sm_ref[...]


def ssd_fwd(z, xbc, sm, conv_w, conv_b, dtb, alog, dpar, nw, cs):
    t = z.shape[0]
    nc = t // CHUNK

    def body(shared, z_ref, xbc_ref, halo_ref, sm_ref, cw_ref, cb_ref, dtb_ref, alog_ref, dpar_ref, nw_ref,
             tri_ref, i2_ref, mask2_ref, lo_ref, y_ref, hs_ref, pre_ref, pbuf, ht_scr):
        def init():
            ht_scr[...] = jnp.zeros_like(ht_scr)

        _when_first(shared, init)
        pbuf[0:8, :] = jnp.where(shared["first"], 0.0, halo_ref[...])
        pbuf[8:72, :] = xbc_ref[...]

        def pre_fn(c0, c1):
            pre = _conv_fwd(pbuf, cw_ref, c0, c1) + cb_ref[:, c0:c1]
            pre_ref[:, c0:c1] = pre
            return pre

        xs_pre, b_pre, c_pre, zz, smv = _ssd_split(pre_fn, z_ref, sm_ref)
        ht = [ht_scr[:, 128 * j:128 * j + 128] for j in range(8)]
        hs_ref[0] = ht_scr[...]
        nwl = [nw_ref[:, 128 * j:128 * j + 128] for j in range(8)]
        outs, ht_next = _ssd_chunk(xs_pre, b_pre, c_pre, zz, smv, ht, dtb_ref[...], alog_ref[...], dpar_ref[...],
                                   nwl, tri_ref[...], i2_ref[...], mask2_ref[...], lo_ref[...])
        for j in range(8):
            y_ref[:, 128 * j:128 * j + 128] = outs[j].astype(y_ref.dtype)
            ht_scr[:, 128 * j:128 * j + 128] = ht_next[j]

    blk = lambda w: pl.BlockSpec((SUB_FWD * CHUNK, w), lambda i: (i, 0))
    return dict(
        body=body,
        in_kinds=["rows", "rows", ("halo", 1), "rows"] + ["full"] * 10, out_kinds=["rows", "state", "rows"],
        in_specs=[blk(1024), blk(1536), _halo_spec(1536, lambda i: i), blk(128),
                  _full((CONV_K, 1536)), _full((1, 1536)), _full((1, 128)), _full((1, 128)), _full((1, 128)),
                  _full((1, 1024)), _full((64, 64)), _full((64, 128)), _full((64, 128)),
                  _full((1, 128))],
        out_specs=[blk(1024), pl.BlockSpec((SUB_FWD, 128, 1024), lambda i: (i, 0, 0)), blk(1536)],
        out_shape=[jax.ShapeDtypeStruct((t, 1024), _MM), jax.ShapeDtypeStruct((nc, 128, 1024), jnp.float32),
                   jax.ShapeDtypeStruct((t, 1536), jnp.float32)],
        scratch=[pltpu.VMEM((72, 1536), jnp.float32), pltpu.VMEM((128, 1024), jnp.float32)],
        args=[z, xbc, xbc, sm, conv_w, conv_b, dtb, alog, dpar, nw, cs["tri"], cs["i2"], cs["mask2"], cs["lo"]])


def _conv_bwd(dpre_list, col_ranges, dbuf, carry, x_ref, cw_ref, dx_ref, dcw_ref, dcb_ref, first):
    for dpre, (c0, c1) in zip(dpre_list, col_ranges):
        dbuf[0:64, c0:c1] = dpre
    dbuf[64:72, :] = jnp.where(first, 0.0, carry[...])
    carry[...] = dbuf[0:8, :]
    for (c0, c1) in col_ranges:
        xin = x_ref[:, c0:c1]
        blk = dbuf[:, c0:c1]
        acc = None
        for j in range(CONV_K):
            sh = blk[0:64] if j == CONV_K - 1 else pltpu.roll(blk, 72 - (CONV_K - 1 - j), axis=0)[0:64]
            term = cw_ref[j:j + 1, c0:c1] * sh
            acc = term if acc is None else acc + term
            dcw_ref[j:j + 1, c0:c1] += jnp.sum(xin * sh, axis=0, keepdims=True)
        dx_ref[:, c0:c1] = acc.astype(dx_ref.dtype)
        if dcb_ref is not None:
            dcb_ref[0:1, c0:c1] += jnp.sum(dbuf[0:64, c0:c1], axis=0, keepdims=True)


def ssd_bwd(z, xbc, pre, sm, hs, dy, conv_w, dtb, alog, dpar, nw, cs):
    t = z.shape[0]
    nc = t // CHUNK

    def body(shared, z_ref, xbc_ref, pre_ref, sm_ref, hs_ref, dy_ref, cw_ref, dtb_ref, alog_ref, dpar_ref, nw_ref,
             tri_ref, i2_ref, mask2_ref, lo_ref,
             dz_ref, dxbc_ref, dcw_ref, dcb_ref, ddtb_ref, dalog_ref, ddpar_ref, dnw_ref,
             dbuf, carry, dht_scr):
        def init():
            dht_scr[...] = jnp.zeros_like(dht_scr)
            dcw_ref[...] = jnp.zeros_like(dcw_ref)
            dcb_ref[...] = jnp.zeros_like(dcb_ref)
            ddtb_ref[...] = jnp.zeros_like(ddtb_ref)
            dalog_ref[...] = jnp.zeros_like(dalog_ref)
            ddpar_ref[...] = jnp.zeros_like(ddpar_ref)
            dnw_ref[...] = jnp.zeros_like(dnw_ref)

        _when_first(shared, init)
        pre_fn = lambda c0, c1: pre_ref[:, c0:c1]
        xs_pre, b_pre, c_pre, zz, smv = _ssd_split(pre_fn, z_ref, sm_ref)
        ht = [hs_ref[0, :, 128 * j:128 * j + 128] for j in range(8)]
        nwl = [nw_ref[:, 128 * j:128 * j + 128] for j in range(8)]
        consts = (tri_ref[...], i2_ref[...], mask2_ref[...], lo_ref[...])

        def f(xs_pre, b_pre, c_pre, zz, smv, ht, dtb, alog, dpar, nwl):
            return _ssd_chunk(xs_pre, b_pre, c_pre, zz, smv, ht, dtb, alog, dpar, nwl, *consts)

        _, vjp = jax.vjp(f, xs_pre, b_pre, c_pre, zz, smv, ht, dtb_ref[...], alog_ref[...], dpar_ref[...], nwl)
        dys = [dy_ref[:, 128 * j:128 * j + 128] for j in range(8)]
        dhts = [dht_scr[:, 128 * j:128 * j + 128] for j in range(8)]
        dxs, db, dc, dzz, dsm, dht, ddtb, dalog, ddpar, dnwl = vjp((dys, dhts))
        for j in range(8):
            dz_ref[:, 128 * j:128 * j + 128] = dzz[j].astype(dz_ref.dtype)
            dht_scr[:, 128 * j:128 * j + 128] = dht[j]
            dnw_ref[0:1, 128 * j:128 * j + 128] += dnwl[j]
        shared["dsm_ssd"] = dsm
        ddtb_ref[0:1, :] += ddtb
        dalog_ref[0:1, :] += dalog
        ddpar_ref[0:1, :] += ddpar
        ranges = ([(128 * j, 128 * j + 128) for j in range(8)] + [(1024 + 128 * g, 1152 + 128 * g) for g in range(2)]
                  + [(1280 + 128 * g, 1408 + 128 * g) for g in range(2)])
        _conv_bwd(dxs + db + dc, ranges, dbuf, carry, xbc_ref, cw_ref, dxbc_ref, dcw_ref, dcb_ref, shared["first"])

    ns = nc // SUB_BWD
    rblk = lambda w: pl.BlockSpec((SUB_BWD * CHUNK, w), lambda i: (ns - 1 - i, 0))
    acc = lambda w: pl.BlockSpec((8, w), lambda i: (0, 0))
    f32 = jnp.float32
    return dict(
        body=body,
        in_kinds=["rows"] * 4 + ["state", "rows"] + ["full"] * 9, out_kinds=["rows", "rows"] + ["full"] * 6,
        in_specs=[rblk(1024), rblk(1536), rblk(1536), rblk(128),
                  pl.BlockSpec((SUB_BWD, 128, 1024), lambda i: (ns - 1 - i, 0, 0)), rblk(1024),
                  _full((CONV_K, 1536)), _full((1, 128)), _full((1, 128)), _full((1, 128)),
                  _full((1, 1024)), _full((64, 64)), _full((64, 128)), _full((64, 128)),
                  _full((1, 128))],
        out_specs=[rblk(1024), rblk(1536), acc(1536), acc(1536), acc(128), acc(128), acc(128), acc(1024)],
        out_shape=[jax.ShapeDtypeStruct((t, 1024), f32), jax.ShapeDtypeStruct((t, 1536), f32),
                   jax.ShapeDtypeStruct((8, 1536), f32),
                   jax.ShapeDtypeStruct((8, 1536), f32), jax.ShapeDtypeStruct((8, 128), f32),
                   jax.ShapeDtypeStruct((8, 128), f32), jax.ShapeDtypeStruct((8, 128), f32),
                   jax.ShapeDtypeStruct((8, 1024), f32)],
        scratch=[pltpu.VMEM((72, 1536), f32), pltpu.VMEM((8, 1536), f32), pltpu.VMEM((128, 1024), f32)],
        args=[z, xbc, pre, sm, hs, dy, conv_w, dtb, alog, dpar, nw, cs["tri"], cs["i2"], cs["mask2"], cs["lo"]])


def _gdn_split(pre_fn, gate_ref):
    def heads(base):
        return jnp.stack([pre_fn(base + 128 * h, base + 128 * h + 128) for h in range(GDN_HEADS)])
    gate = jnp.stack([gate_ref[:, 128 * h:128 * h + 128] for h in range(GDN_HEADS)])
    return heads(0), heads(1024), heads(2048), gate


def gdn_fwd(gate, qkv, sm, conv_w, dtb, alog, nw, cs):
    t = gate.shape[0]
    nc = t // CHUNK

    def body(shared, gate_ref, qkv_ref, halo_ref, sm_ref, cw_ref, dtb_ref, alog_ref, nw_ref,
             tri_ref, i64_ref, strict_ref, o_ref, ss_ref, ts_ref, pre_ref, pbuf, s_scr):
        def init():
            s_scr[...] = jnp.zeros_like(s_scr)

        _when_first(shared, init)
        pbuf[0:8, :] = jnp.where(shared["first"], 0.0, halo_ref[...])
        pbuf[8:72, :] = qkv_ref[...]

        def pre_fn(c0, c1):
            pre = _conv_fwd(pbuf, cw_ref, c0, c1)
            pre_ref[:, c0:c1] = pre
            return pre

        q_pre, k_pre, v_pre, g3 = _gdn_split(pre_fn, gate_ref)
        s = s_scr[...]
        ss_ref[0] = s
        out, s_next, tinv = _gdn_chunk(q_pre, k_pre, v_pre, g3, sm_ref[...], s, dtb_ref[...], alog_ref[...],
                                       nw_ref[...], tri_ref[...], i64_ref[...], strict_ref[...])
        ts_ref[0] = tinv
        s_scr[...] = s_next
        for h in range(GDN_HEADS):
            o_ref[:, 128 * h:128 * h + 128] = out[h].astype(o_ref.dtype)

    blk = lambda w: pl.BlockSpec((SUB_FWD * CHUNK, w), lambda i: (i, 0))
    return dict(
        body=body,
        in_kinds=["rows", "rows", ("halo", 1), "rows"] + ["full"] * 7, out_kinds=["rows", "state", "state", "rows"],
        in_specs=[blk(1024), blk(3072), _halo_spec(3072, lambda i: i), blk(128),
                  _full((CONV_K, 3072)), _full((1, 128)), _full((1, 128)), _full((1, 128)),
                  _full((64, 64)), _full((64, 64)), _full((64, 64))],
        out_specs=[blk(1024), pl.BlockSpec((SUB_FWD, 8, 128, 128), lambda i: (i, 0, 0, 0)),
                   pl.BlockSpec((SUB_FWD, 8, CHUNK, CHUNK), lambda i: (i, 0, 0, 0)), blk(3072)],
        out_shape=[jax.ShapeDtypeStruct((t, 1024), _MM), jax.ShapeDtypeStruct((nc, 8, 128, 128), jnp.float32),
                   jax.ShapeDtypeStruct((nc, 8, CHUNK, CHUNK), jnp.float32),
                   jax.ShapeDtypeStruct((t, 3072), jnp.float32)],
        scratch=[pltpu.VMEM((72, 3072), jnp.float32), pltpu.VMEM((8, 128, 128), jnp.float32)],
        args=[gate, qkv, qkv, sm, conv_w, dtb, alog, nw, cs["tri"], cs["i64"], cs["strict"]])


def gdn_bwd(gate, qkv, pre, sm, ss, ts, do, conv_w, dtb, alog, nw, cs):
    t = gate.shape[0]
    nc = t // CHUNK

    def body(shared, gate_ref, qkv_ref, pre_ref, sm_ref, ss_ref, ts_ref, do_ref, cw_ref, dtb_ref, alog_ref,
             nw_ref, tri_ref, i64_ref, strict_ref,
             dgate_ref, dqkv_ref, dsm_ref, dcw_ref, ddtb_ref, dalog_ref, dnw_ref,
             dbuf, carry, ds_scr):
        def init():
            ds_scr[...] = jnp.zeros_like(ds_scr)
            dcw_ref[...] = jnp.zeros_like(dcw_ref)
            ddtb_ref[...] = jnp.zeros_like(ddtb_ref)
            dalog_ref[...] = jnp.zeros_like(dalog_ref)
            dnw_ref[...] = jnp.zeros_like(dnw_ref)

        _when_first(shared, init)

        q_pre, k_pre, v_pre, g3 = _gdn_split(lambda c0, c1: pre_ref[:, c0:c1], gate_ref)
        consts = (tri_ref[...], i64_ref[...], strict_ref[...], ts_ref[0])

        def f(q_pre, k_pre, v_pre, g3, smv, s, dtb, alog, nwv):
            return _gdn_chunk(q_pre, k_pre, v_pre, g3, smv, s, dtb, alog, nwv, *consts)[:2]

        _, vjp = jax.vjp(f, q_pre, k_pre, v_pre, g3, sm_ref[...], ss_ref[0], dtb_ref[...], alog_ref[...], nw_ref[...])
        do3 = jnp.stack([do_ref[:, 128 * h:128 * h + 128] for h in range(GDN_HEADS)])
        dq, dk, dv, dg3, dsm, ds, ddtb, dalog, dnw = vjp((do3, ds_scr[...]))
        ds_scr[...] = ds
        for h in range(GDN_HEADS):
            dgate_ref[:, 128 * h:128 * h + 128] = dg3[h].astype(dgate_ref.dtype)
        dsm_ref[...] = (dsm + shared["dsm_ssd"]).astype(dsm_ref.dtype)
        ddtb_ref[0:1, :] += ddtb
        dalog_ref[0:1, :] += dalog
        dnw_ref[0:1, :] += dnw
        ranges = [(base + 128 * h, base + 128 * h + 128) for base in (0, 1024, 2048) for h in range(GDN_HEADS)]
        dlist = [d[h] for d in (dq, dk, dv) for h in range(GDN_HEADS)]
        _conv_bwd(dlist, ranges, dbuf, carry, qkv_ref, cw_ref, dqkv_ref, dcw_ref, None, shared["first"])

    ns = nc // SUB_BWD
    rblk = lambda w: pl.BlockSpec((SUB_BWD * CHUNK, w), lambda i: (ns - 1 - i, 0))
    acc = lambda w: pl.BlockSpec((8, w), lambda i: (0, 0))
    f32 = jnp.float32
    return dict(
        body=body,
        in_kinds=["rows"] * 4 + ["state", "state", "rows"] + ["full"] * 7, out_kinds=["rows"] * 3 + ["full"] * 4,
        in_specs=[rblk(1024), rblk(3072), rblk(3072), rblk(128),
                  pl.BlockSpec((SUB_BWD, 8, 128, 128), lambda i: (ns - 1 - i, 0, 0, 0)),
                  pl.BlockSpec((SUB_BWD, 8, CHUNK, CHUNK), lambda i: (ns - 1 - i, 0, 0, 0)), rblk(1024),
                  _full((CONV_K, 3072)), _full((1, 128)), _full((1, 128)), _full((1, 128)),
                  _full((64, 64)), _full((64, 64)), _full((64, 64))],
        out_specs=[rblk(1024), rblk(3072), rblk(128), acc(3072), acc(128), acc(128), acc(128)],
        out_shape=[jax.ShapeDtypeStruct((t, 1024), f32), jax.ShapeDtypeStruct((t, 3072), f32),
                   jax.ShapeDtypeStruct((t, 128), f32), jax.ShapeDtypeStruct((8, 3072), f32),
                   jax.ShapeDtypeStruct((8, 128), f32), jax.ShapeDtypeStruct((8, 128), f32),
                   jax.ShapeDtypeStruct((8, 128), f32)],
        scratch=[pltpu.VMEM((72, 3072), f32), pltpu.VMEM((8, 3072), f32), pltpu.VMEM((8, 128, 128), f32)],
        args=[gate, qkv, pre, sm, ss, ts, do, conv_w, dtb, alog, nw, cs["tri"], cs["i64"], cs["strict"]])


def _chunk_call(parts, name, nc, reverse):
    n_in = [len(p["args"]) for p in parts]
    n_out = [len(p["out_shape"]) for p in parts]
    n_scr = [len(p["scratch"]) for p in parts]
    sub = SUB_BWD if reverse else SUB_FWD
    order = list(range(sub))[::-1] if reverse else list(range(sub))

    def view(ref, kind, s, refs):
        if kind == "rows":
            return ref.at[pl.ds(CHUNK * s, CHUNK)]
        if kind == "state":
            return ref.at[pl.ds(s, 1)]
        if kind == "full":
            return ref
        src = refs[kind[1]]
        return ref if s == 0 else src.at[pl.ds(CHUNK * s - 8, 8)]

    def body(*refs):
        ins, outs, scr = refs[:sum(n_in)], refs[sum(n_in):sum(n_in) + sum(n_out)], refs[sum(n_in) + sum(n_out):]
        for s in order:
            shared = {"first": (pl.program_id(0) == 0) if s == order[0] else False}
            for k, p in enumerate(parts):
                i0, o0, s0 = sum(n_in[:k]), sum(n_out[:k]), sum(n_scr[:k])
                p_ins = ins[i0:i0 + n_in[k]]
                p["body"](shared,
                          *[view(r, kd, s, p_ins) for r, kd in zip(p_ins, p["in_kinds"])],
                          *[view(r, kd, s, None) for r, kd in zip(outs[o0:o0 + n_out[k]], p["out_kinds"])],
                          *scr[s0:s0 + n_scr[k]])

    cat = lambda key: [v for p in parts for v in p[key]]
    return _pc(body, name=name, grid=(nc // sub,), in_specs=cat("in_specs"), out_specs=cat("out_specs"),
               out_shape=cat("out_shape"), scratch_shapes=cat("scratch"),
               compiler_params=_cparams(("arbitrary",)))(*cat("args"))


def out_fwd_bwd(x, tgt, y_ssd, y_gdn, w_out, fnw):
    t = x.shape[0]
    tm = min(512, t)
    f32 = jnp.float32

    def body(x_ref, tgt_ref, ys_ref, yg_ref, w_ref, fnw_ref,
             dout_ref, dys_ref, dyg_ref, gw_ref, gfnw_ref, loss_ref, gw_acc):
        i = pl.program_id(0)

        @pl.when(i == 0)
        def _():
            gw_acc[...] = jnp.zeros_like(gw_acc)
            gfnw_ref[...] = jnp.zeros_like(gfnw_ref)
            loss_ref[...] = jnp.zeros_like(loss_ref)

        ys = ys_ref[...]
        yg = yg_ref[...]
        out = x_ref[...] + jnp.dot(ys, w_ref[0:1024, :], preferred_element_type=f32) \
            + jnp.dot(yg, w_ref[1024:2048, :], preferred_element_type=f32)
        rstd = lax.rsqrt(jnp.mean(out * out, axis=-1, keepdims=True) + EPS)
        yhat = out * rstd
        fw = fnw_ref[...]
        e = yhat * fw - tgt_ref[...]
        loss_ref[...] += 0.5 * jnp.sum(jnp.sum(e * e, axis=-1, keepdims=True) * (1.0 / D_MODEL), axis=0, keepdims=True)
        dyf = e * (1.0 / D_MODEL)
        gfnw_ref[0:1, :] += jnp.sum(dyf * yhat, axis=0, keepdims=True)
        dyhat = dyf * fw
        dout = rstd * (dyhat - yhat * jnp.mean(dyhat * yhat, axis=-1, keepdims=True))
        dout_ref[...] = dout
        db = dout.astype(_MM)
        dys_ref[...] = lax.dot_general(db, w_ref[0:1024, :], (((1,), (1,)), ((), ())), preferred_element_type=f32)
        dyg_ref[...] = lax.dot_general(db, w_ref[1024:2048, :], (((1,), (1,)), ((), ())), preferred_element_type=f32)
        gw_acc[0:1024, :] += lax.dot_general(ys, db, (((0,), (0,)), ((), ())), preferred_element_type=f32)
        gw_acc[1024:2048, :] += lax.dot_general(yg, db, (((0,), (0,)), ((), ())), preferred_element_type=f32)

        @pl.when(i == steps - 1)
        def _():
            gw_ref[...] = gw_acc[...].astype(gw_ref.dtype)

    steps = t // tm
    blk = pl.BlockSpec((tm, D_MODEL), lambda i: (i, 0))
    return _pc(
        body, name="out_fwd_bwd", grid=(steps,),
        in_specs=[blk, blk, blk, blk, _full((MIX_WIDTH, D_MODEL)), _full((1, D_MODEL))],
        out_specs=[blk, blk, blk, _full((MIX_WIDTH, D_MODEL)), _full((8, D_MODEL)), _full((1, 128))],
        out_shape=[jax.ShapeDtypeStruct((t, D_MODEL), f32)] * 3 +
                  [jax.ShapeDtypeStruct((MIX_WIDTH, D_MODEL), _MM), jax.ShapeDtypeStruct((8, D_MODEL), f32),
                   jax.ShapeDtypeStruct((1, 128), f32)],
        scratch_shapes=[pltpu.VMEM((MIX_WIDTH, D_MODEL), f32)],
        compiler_params=_cparams(("arbitrary",)),
    )(x, tgt, y_ssd, y_gdn, w_out, fnw)


def inproj_bwd_dx(x, dout, norm_w, w_perm, dgroups, scattered):
    t = x.shape[0]
    tm = min(256, t)
    f32 = jnp.float32

    def body(x_ref, dout_ref, nw_ref, w_ref, dz_ref, dxbc_ref, dgate_ref, dqkv_ref, dsm_ref, dx_ref, gnw_ref):
        i = pl.program_id(0)

        @pl.when(i == 0)
        def _():
            gnw_ref[...] = jnp.zeros_like(gnw_ref)

        du = None
        for (name, c0, c1), d_ref in zip(GROUPS, (dz_ref, dxbc_ref, dgate_ref, dqkv_ref, dsm_ref)):
            term = jnp.dot(d_ref[...].astype(_MM), _w_rows(w_ref, name, c1 - c0), preferred_element_type=f32)
            du = term if du is None else du + term
        xf = x_ref[...]
        rstd = lax.rsqrt(jnp.mean(xf * xf, axis=-1, keepdims=True) + EPS)
        xhat = xf * rstd
        gnw_ref[0:1, :] += jnp.sum(du * xhat, axis=0, keepdims=True)
        dxh = du * nw_ref[...]
        dx_ref[...] = dout_ref[...] + rstd * (dxh - xhat * jnp.mean(dxh * xhat, axis=-1, keepdims=True))

    blk = lambda w: pl.BlockSpec((tm, w), lambda i: (i, 0))
    steps = t // tm
    kinds = ["scatter"] * len(scattered)
    hosted = _hosting(body, 9, 2, 0, kinds, lambda: pl.program_id(0) == 0, lambda: pl.program_id(0) == steps - 1)
    return _pc_comm(
        hosted, name="inproj_bwd_dx", grid=(steps,),
        in_specs=[blk(D_MODEL), blk(D_MODEL), _full((1, D_MODEL)), _full((W_GATHERED, D_MODEL))] +
                 [blk(c1 - c0) for _, c0, c1 in GROUPS] + [ANY] * len(scattered),
        out_specs=[blk(D_MODEL), _full((8, D_MODEL))] + [ANY] * len(scattered),
        out_shape=[jax.ShapeDtypeStruct((t, D_MODEL), f32), jax.ShapeDtypeStruct((8, D_MODEL), f32)] +
                  [_exchange_out_shape("scatter", a) for a in scattered],
        scratch_shapes=_exchange_sems(len(scattered)), compiler_params=_cparams(("arbitrary",)),
    )(x, dout, norm_w, w_perm, *dgroups, *scattered)


def grad_w_group(u, dg, name, scattered=()):
    t, n = dg.shape
    tn = n if n <= 1536 else 1024
    budget = 40 * 1024 * 1024
    tm = next((c for c in (4096, 2048, 1024, 512, 256)
               if t % c == 0 and tn * D_MODEL * 4 + 2 * (c * tn * 4 + c * D_MODEL * 2 + tn * D_MODEL * 2) <= budget), t)
    nj, nk = n // tn, t // tm
    f32 = jnp.float32

    def body(u_ref, d_ref, o_ref, acc):
        k = pl.program_id(1)

        @pl.when(k == 0)
        def _():
            acc[...] = jnp.zeros_like(acc)

        acc[...] += lax.dot_general(d_ref[...].astype(_MM), u_ref[...], (((0,), (0,)), ((), ())),
                                    preferred_element_type=f32)

        @pl.when(k == nk - 1)
        def _():
            o_ref[...] = acc[...].astype(o_ref.dtype)

    ne = len(scattered)
    hosted = _hosting(body, 2, 1, 1, ["scatter"] * ne,
                      lambda: (pl.program_id(0) == 0) & (pl.program_id(1) == 0),
                      lambda: (pl.program_id(0) == nj - 1) & (pl.program_id(1) == nk - 1))
    res = (_pc_comm if ne else _pc)(
        hosted, name=name, grid=(nj, nk),
        in_specs=[pl.BlockSpec((tm, D_MODEL), lambda j, k: (k, 0)),
                  pl.BlockSpec((tm, tn), lambda j, k: (k, j))] + [ANY] * ne,
        out_specs=[pl.BlockSpec((tn, D_MODEL), lambda j, k: (j, 0))] + [ANY] * ne,
        out_shape=[jax.ShapeDtypeStruct((n, D_MODEL), _MM)] + [_exchange_out_shape("scatter", a) for a in scattered],
        scratch_shapes=[pltpu.VMEM((tn, D_MODEL), f32)] + _exchange_sems(ne),
        compiler_params=_cparams(("arbitrary", "arbitrary")),
    )(u, dg, *scattered)
    return res if ne else res[0]


def grad_w_many(u, dgs, name):
    t = u.shape[0]
    widths = [d.shape[1] for d in dgs]
    tot, ng = sum(widths), len(dgs)
    f32 = jnp.float32
    budget = 48 * 1024 * 1024
    tm = next((c for c in (2048, 1024, 512, 256)
               if t % c == 0 and tot * D_MODEL * 4 + 2 * (c * tot * 4 + c * D_MODEL * 2 + tot * D_MODEL * 2) <= budget), t)
    nk = t // tm

    def body(*refs):
        u_ref, d_refs, o_refs, accs = refs[0], refs[1:1 + ng], refs[1 + ng:1 + 2 * ng], refs[1 + 2 * ng:]
        k = pl.program_id(0)

        @pl.when(k == 0)
        def _():
            for acc in accs:
                acc[...] = jnp.zeros_like(acc)

        uu = u_ref[...]
        for d_ref, acc in zip(d_refs, accs):
            acc[...] += lax.dot_general(d_ref[...].astype(_MM), uu, (((0,), (0,)), ((), ())),
                                        preferred_element_type=f32)

        @pl.when(k == nk - 1)
        def _():
            for o_ref, acc in zip(o_refs, accs):
                o_ref[...] = acc[...].astype(o_ref.dtype)

    return _pc(
        body, name=name, grid=(nk,),
        in_specs=[pl.BlockSpec((tm, D_MODEL), lambda k: (k, 0))] + [pl.BlockSpec((tm, n), lambda k: (k, 0)) for n in widths],
        out_specs=[pl.BlockSpec((n, D_MODEL), lambda k: (0, 0)) for n in widths],
        out_shape=[jax.ShapeDtypeStruct((n, D_MODEL), _MM) for n in widths],
        scratch_shapes=[pltpu.VMEM((n, D_MODEL), f32) for n in widths],
        compiler_params=_cparams(("arbitrary",)),
    )(u, *dgs)


def _pad_lanes(v, off):
    n = v.shape[-1]
    return jnp.pad(v.reshape(1, n).astype(jnp.float32), ((0, 0), (off, 128 - off - n)))


REF_ROWS = dict(z=(0, 1024), xbc=(1024, 2560), dt=(2560, 2576), gate=(2576, 3600), qkv=(3600, 6672), ab=(6672, 6688))


def unperm_w_in(gz, gxbc, ggate, gqkv, gsm):
    src = dict(z=gz, xbc=gxbc, dt=gsm[0:16], gate=ggate, qkv=gqkv, ab=gsm[16:32])
    slabs = []
    for k in range(N_DEV):
        a, b = k * W_IN_SHARD, (k + 1) * W_IN_SHARD
        parts = []
        for name, (s, e) in REF_ROWS.items():
            lo, hi = max(a, s), min(b, e)
            if lo < hi:
                parts.append(src[name][lo - s:hi - s])
        slabs.append(jnp.concatenate(parts, axis=0))
    return jnp.stack(slabs)


def all_gather(arrs, name):
    n = len(arrs)
    halves = []
    for arr in arrs:
        h = arr.shape[0] // 2 // 16 * 16
        halves.append(((0, h), (h, arr.shape[0] - h)))

    def body(*refs):
        ins, outs = refs[:n], refs[n:2 * n]
        send_sems, recv_sems, local_sems = refs[2 * n:]
        x, y, c, me = _me()
        sibling = (x, y, 1 - c)
        xn, yn, dg = (1 - x, y), (x, 1 - y), (1 - x, 1 - y)

        def idx(px, py, pc):
            return 4 * px + 2 * py + pc

        def copy(a, k, block, to, src=None, half=None):
            slot = outs[a].at[idx(*block)]
            if half is not None:
                slot = slot.at[pl.ds(*halves[a][half])]
            return pltpu.make_async_remote_copy(src_ref=slot if src is None else src, dst_ref=slot,
                                                send_sem=send_sems.at[a, k], recv_sem=recv_sems.at[a, k],
                                                device_id=to, device_id_type=MESH)

        local = [pltpu.make_async_copy(ins[a], outs[a].at[me], local_sems.at[a]) for a in range(n)]
        for cp in local:
            cp.start()
        started = []

        def start(cps):
            for cp in cps:
                cp.start()
            started.extend(cps)

        for a in range(n):
            start([copy(a, 1, (x, y, c), (*xn, c), src=ins[a]), copy(a, 2, (x, y, c), (*yn, c), src=ins[a]),
                   copy(a, 0, (x, y, c), sibling, src=ins[a])])
        for a in range(n):
            copy(a, 1, (*xn, c), (x, y, c)).wait_recv()
            start([copy(a, 3, (*xn, c), (*yn, c), half=0), copy(a, 5, (*xn, c), sibling)])
            copy(a, 2, (*yn, c), (x, y, c)).wait_recv()
            start([copy(a, 4, (*yn, c), (*xn, c), half=1), copy(a, 6, (*yn, c), sibling)])
        for a in range(n):
            copy(a, 3, (*dg, c), (x, y, c), half=0).wait_recv()
            copy(a, 4, (*dg, c), (x, y, c), half=1).wait_recv()
            start([copy(a, 7, (*dg, c), sibling)])
        for a in range(n):
            copy(a, 0, sibling, (x, y, c)).wait_recv()
            for j, chip in enumerate((xn, yn, dg)):
                copy(a, 5 + j, (*chip, 1 - c), (x, y, c)).wait_recv()
        for cp in started:
            cp.wait_send()
        for cp in local:
            cp.wait()

    return _pc_comm(
        body, name=name, in_specs=[ANY] * n, out_specs=[ANY] * n,
        out_shape=[jax.ShapeDtypeStruct((N_DEV,) + a.shape, a.dtype) for a in arrs],
        scratch_shapes=[pltpu.SemaphoreType.DMA((n, 8)), pltpu.SemaphoreType.DMA((n, 8)),
                        pltpu.SemaphoreType.DMA((n,))],
    )(*arrs)


def adamw_sum(recv, w, m, v, rows, name, cols=None, small=()):
    r, ccols = w.shape
    f32 = jnp.float32
    c1 = 1.0 / (1.0 - ADAM_B1 ** ADAM_STEP)
    c2 = 1.0 / (1.0 - ADAM_B2 ** ADAM_STEP)
    ns = len(small)
    steps = r // rows if cols is None else ccols // cols

    def body(*refs):
        recv_ref, w_ref, m_ref, v_ref = refs[:4]
        g_ref, d_ref, mo_ref, vo_ref = refs[4 + ns:8 + ns]
        if ns:
            small_out, slabs = refs[8 + ns], refs[9 + ns]
            start, wait = _exchange_ops("scatter", slabs, small_out, *refs[10 + ns:])

            @pl.when(pl.program_id(0) == 0)
            def _():
                _fill_small_slabs(slabs, refs[4:4 + ns])
                start()

        g = recv_ref[0].astype(f32)
        for k in range(1, N_DEV):
            g = g + recv_ref[k].astype(f32)
        mn = ADAM_B1 * m_ref[...] + (1.0 - ADAM_B1) * g
        vn = ADAM_B2 * v_ref[...] + (1.0 - ADAM_B2) * (g * g)
        g_ref[...] = g
        mo_ref[...] = mn
        vo_ref[...] = vn
        d_ref[...] = -ADAM_LR * ((mn * c1) / (jnp.sqrt(vn * c2) + ADAM_EPS) + ADAM_WD * w_ref[...])

        if ns:
            pl.when(pl.program_id(0) == steps - 1)(wait)

    if cols is None:
        blk = pl.BlockSpec((rows, ccols), lambda i: (i, 0))
        rblk = pl.BlockSpec((N_DEV, rows, ccols), lambda i: (0, i, 0))
    else:
        blk = pl.BlockSpec((r, cols), lambda i: (0, i))
        rblk = pl.BlockSpec((N_DEV, r, cols), lambda i: (0, 0, i))
    total = _small_layout()[1]
    return (_pc_comm if ns else _pc)(
        body, name=name, grid=(steps,),
        in_specs=[rblk, blk, blk, blk] + [_full(a.shape) for a in small],
        out_specs=[blk] * 4 + [ANY] * bool(ns),
        out_shape=[jax.ShapeDtypeStruct((r, ccols), f32)] * 4 +
                  [jax.ShapeDtypeStruct((N_DEV, 1, total), f32)] * bool(ns),
        scratch_shapes=([pltpu.VMEM((N_DEV, 1, total), f32)] + _exchange_sems(1)) * bool(ns),
        compiler_params=_cparams(("arbitrary",)),
    )(recv, w, m, v, *small)


SMALL = (("norm_w", 1, 1024, 0), ("ssd_conv_b", 1, 1536, 0), ("ssd_dt_bias", 1, 16, 0), ("ssd_a_log", 1, 16, 0),
         ("ssd_d", 1, 16, 0), ("ssd_norm_w", 1, 1024, 0), ("gdn_dt_bias", 1, 8, 16), ("gdn_a_log", 1, 8, 16),
         ("gdn_norm_w", 1, 128, 0), ("final_norm_w", 1, 1024, 0),
         ("ssd_conv_w", CONV_K, SSD_CONV_DIM // N_DEV, 0), ("gdn_conv_w", CONV_K, GDN_CONV_DIM // N_DEV, 0))


def _small_layout():
    out, off = [], 0
    for name, rows, n, lane0 in SMALL + (("loss", 1, 128, 0),):
        stride = -(-(lane0 + n) // 128) * 128
        out.append((name, rows, n, lane0, stride, off))
        off += rows * stride
    return out, off


def _fill_small_slabs(slabs, acc_refs):
    layout, _ = _small_layout()
    slabs[...] = jnp.zeros_like(slabs)
    for (name, rows, n, lane0, stride, off), acc in zip(layout, acc_refs):
        for k in range(N_DEV):
            if rows == 1:
                slabs[k, :, off:off + stride] = acc[0:1, 0:stride]
            else:
                for j in range(rows):
                    slabs[k, :, off + stride * j:off + stride * j + n] = acc[j:j + 1, n * k:n * k + n]


def adamw_small(recv, w, m, v):
    layout, total = _small_layout()
    loss_off = layout[-1][5]
    layout = layout[:-1]
    f32 = jnp.float32
    c1 = 1.0 / (1.0 - ADAM_B1 ** ADAM_STEP)
    c2 = 1.0 / (1.0 - ADAM_B2 ** ADAM_STEP)
    np_ = len(layout)

    def body(*refs):
        recv_ref = refs[0]
        w_refs, m_refs, v_refs = refs[1:1 + np_], refs[1 + np_:1 + 2 * np_], refs[1 + 2 * np_:1 + 3 * np_]
        o_refs = refs[1 + 3 * np_:]
        g_all = recv_ref[0]
        for k in range(1, N_DEV):
            g_all = g_all + recv_ref[k]
        o_refs[4 * np_][...] = g_all[:, loss_off:loss_off + 128]

        def update(g, wv, mv, vv):
            mn = ADAM_B1 * mv + (1.0 - ADAM_B1) * g
            vn = ADAM_B2 * vv + (1.0 - ADAM_B2) * (g * g)
            return g, -ADAM_LR * ((mn * c1) / (jnp.sqrt(vn * c2) + ADAM_EPS) + ADAM_WD * wv), mn, vn

        for p, (name, rows, n, lane0, stride, off) in enumerate(layout):
            outs = o_refs[4 * p:4 * p + 4]
            if rows == 1:
                res = update(g_all[:, off + lane0:off + lane0 + n], w_refs[p][...], m_refs[p][...], v_refs[p][...])
                for o, r in zip(outs, res):
                    o[...] = r
            else:
                for j in range(rows):
                    res = update(g_all[:, off + stride * j:off + stride * j + n], w_refs[p][0, j:j + 1, :],
                                 m_refs[p][0, j:j + 1, :], v_refs[p][0, j:j + 1, :])
                    for o, r in zip(outs, res):
                        o[0, j:j + 1, :] = r

    names = [e[0] for e in layout]
    ins = [recv] + [d[nm] for d in (w, m, v) for nm in names]
    out_shape = [jax.ShapeDtypeStruct(w[nm].shape, f32) for nm in names for _ in range(4)]
    out_shape.append(jax.ShapeDtypeStruct((1, 128), f32))
    res = _pc(body, name="adamw_small", out_shape=out_shape)(*ins)
    return {nm: tuple(res[4 * p:4 * p + 4]) for p, nm in enumerate(names)}, res[4 * np_]


SHARD = (("ssd_conv_w", CONV_K * SSD_CONV_DIM // N_DEV), ("gdn_conv_w", CONV_K * GDN_CONV_DIM // N_DEV))
SHARD_ROWS = 24


def _rows_of(size):
    return -(-size // 128)


def _pack(vals, layout, total_rows):
    parts = []
    for (name, size), val in zip(layout, vals):
        flat = val.reshape(-1).astype(jnp.float32)
        parts.append(jnp.pad(flat, (0, _rows_of(size) * 128 - size)).reshape(-1, 128))
    used = sum(_rows_of(s) for _, s in layout)
    parts.append(jnp.zeros((total_rows - used, 128), jnp.float32))
    return jnp.concatenate(parts, axis=0)


def _conv_full(gathered_flat, ccols):
    return gathered_flat.reshape(N_DEV, CONV_K, ccols // N_DEV).transpose(1, 0, 2).reshape(CONV_K, ccols)


def kernel(x, norm_w, w_in, ssd_conv_w, ssd_conv_b, ssd_dt_bias, ssd_a_log, ssd_d, ssd_norm_w, gdn_conv_w, gdn_dt_bias, gdn_a_log, gdn_norm_w, w_out, final_norm_w, loss_target, m_norm_w, m_w_in, m_ssd_conv_w, m_ssd_conv_b, m_ssd_dt_bias, m_ssd_a_log, m_ssd_d, m_ssd_norm_w, m_gdn_conv_w, m_gdn_dt_bias, m_gdn_a_log, m_gdn_norm_w, m_w_out, m_final_norm_w, v_norm_w, v_w_in, v_ssd_conv_w, v_ssd_conv_b, v_ssd_dt_bias, v_ssd_a_log, v_ssd_d, v_ssd_norm_w, v_gdn_conv_w, v_gdn_dt_bias, v_gdn_a_log, v_gdn_norm_w, v_w_out, v_final_norm_w):
    f32 = jnp.float32
    w = dict(norm_w=norm_w, w_in=w_in, ssd_conv_w=ssd_conv_w, ssd_conv_b=ssd_conv_b, ssd_dt_bias=ssd_dt_bias,
             ssd_a_log=ssd_a_log, ssd_d=ssd_d, ssd_norm_w=ssd_norm_w, gdn_conv_w=gdn_conv_w, gdn_dt_bias=gdn_dt_bias,
             gdn_a_log=gdn_a_log, gdn_norm_w=gdn_norm_w, w_out=w_out, final_norm_w=final_norm_w)
    m = dict(norm_w=m_norm_w, w_in=m_w_in, ssd_conv_w=m_ssd_conv_w, ssd_conv_b=m_ssd_conv_b, ssd_dt_bias=m_ssd_dt_bias,
             ssd_a_log=m_ssd_a_log, ssd_d=m_ssd_d, ssd_norm_w=m_ssd_norm_w, gdn_conv_w=m_gdn_conv_w,
             gdn_dt_bias=m_gdn_dt_bias, gdn_a_log=m_gdn_a_log, gdn_norm_w=m_gdn_norm_w, w_out=m_w_out,
             final_norm_w=m_final_norm_w)
    v = dict(norm_w=v_norm_w, w_in=v_w_in, ssd_conv_w=v_ssd_conv_w, ssd_conv_b=v_ssd_conv_b, ssd_dt_bias=v_ssd_dt_bias,
             ssd_a_log=v_ssd_a_log, ssd_d=v_ssd_d, ssd_norm_w=v_ssd_norm_w, gdn_conv_w=v_gdn_conv_w,
             gdn_dt_bias=v_gdn_dt_bias, gdn_a_log=v_gdn_a_log, gdn_norm_w=v_gdn_norm_w, w_out=v_w_out,
             final_norm_w=v_final_norm_w)
    names = list(w)
    shapes = {n: w[n].shape for n in names}

    xl, tgt = x[0], loss_target[0]
    cs = _consts()
    dtb_s = _pad_lanes(ssd_dt_bias, 0)
    alog_s = _pad_lanes(ssd_a_log, 0)
    dpar = _pad_lanes(ssd_d, 0)
    dtb_g = _pad_lanes(gdn_dt_bias, 16)
    alog_g = _pad_lanes(gdn_a_log, 16)
    nw_g = gdn_norm_w.reshape(1, 128)
    nw_s = ssd_norm_w.reshape(1, 1024)
    cb_s = ssd_conv_b.reshape(1, 1536)
    nw1 = norm_w.reshape(1, D_MODEL)

    w_slot = lax.dynamic_slice(jnp.pad(w_in[0].T.astype(_MM), ((12, 12), (0, 0))),
                               (12 - (W_IN_SHARD * _me()[3]) % 16, 0), (W_SLOT, D_MODEL))
    (g_w_in,) = all_gather([w_slot], "gather_w_in")
    w_perm = g_w_in.reshape(W_GATHERED, D_MODEL)
    conv_pack = _pack([w["ssd_conv_w"], w["gdn_conv_w"]], SHARD, SHARD_ROWS)
    u, z, xbc, gate, qkv, sm, g_w_out, g_conv = inproj_fwd(xl, nw1, w_perm, [w_out[0].astype(_MM), conv_pack])
    w_out_full = g_w_out.reshape(MIX_WIDTH, D_MODEL)
    ssd_cw = _conv_full(g_conv[:, 0:6].reshape(N_DEV, -1), SSD_CONV_DIM)
    gdn_cw = _conv_full(g_conv[:, 6:18].reshape(N_DEV, -1), GDN_CONV_DIM)

    nc = xl.shape[0] // CHUNK
    y_ssd, hs, pre_s, y_gdn, ss, ts, pre_g = _chunk_call(
        [ssd_fwd(z, xbc, sm, ssd_cw, cb_s, dtb_s, alog_s, dpar, nw_s, cs),
         gdn_fwd(gate, qkv, sm, gdn_cw, dtb_g, alog_g, nw_g, cs)], "scan_fwd", nc, False)
    dout, dys, dyg, g_wout, g_fnw, loss_l = out_fwd_bwd(xl, tgt, y_ssd, y_gdn, w_out_full,
                                                        final_norm_w.reshape(1, D_MODEL))
    (dz, dxbc, g_cw_s, g_cb_s, g_dtb_s, g_alog_s, g_d, g_nw_s,
     dgate, dqkv, dsm, g_cw_g, g_dtb_g, g_alog_g, g_nw_g) = _chunk_call(
        [ssd_bwd(z, xbc, pre_s, sm, hs, dys, ssd_cw, dtb_s, alog_s, dpar, nw_s, cs),
         gdn_bwd(gate, qkv, pre_g, sm, ss, ts, dyg, gdn_cw, dtb_g, alog_g, nw_g, cs)], "scan_bwd", nc, True)

    t_w_out = g_wout.reshape(N_DEV, MIX_WIDTH // N_DEV, D_MODEL)
    gws = dict(zip(("z", "sm"), grad_w_many(u, [dz, dsm], "grad_w_in_z_sm")))
    gws["xbc"] = grad_w_group(u, dxbc, "grad_w_in_xbc")
    gws["gate"] = grad_w_group(u, dgate, "grad_w_in_gate")
    gws["qkv"], r_w_out = grad_w_group(u, dqkv, "grad_w_in_qkv", [t_w_out])
    t_w_in = unperm_w_in(gws["z"], gws["xbc"], gws["gate"], gws["qkv"], gws["sm"])
    dx, g_nw, r_w_in = inproj_bwd_dx(xl, dout, nw1, w_perm, (dz, dxbc, dgate, dqkv, dsm), [t_w_in])

    accs = dict(norm_w=g_nw, ssd_conv_b=g_cb_s, ssd_dt_bias=g_dtb_s, ssd_a_log=g_alog_s, ssd_d=g_d,
                ssd_norm_w=g_nw_s, gdn_dt_bias=g_dtb_g, gdn_a_log=g_alog_g, gdn_norm_w=g_nw_g, final_norm_w=g_fnw,
                ssd_conv_w=g_cw_s, gdn_conv_w=g_cw_g)
    *o_w_in, r_small = adamw_sum(r_w_in, w_in[0].T, m_w_in[0].T, v_w_in[0].T, None, "adamw_w_in", cols=256,
                                 small=[accs[e[0]] for e in SMALL] + [loss_l])
    o_w_out = adamw_sum(r_w_out, w_out[0], m_w_out[0], v_w_out[0], 64, "adamw_w_out")
    row = lambda d: {n: (a.reshape(1, -1) if a.ndim == 1 else a) for n, a in d.items()}
    o_small, loss_sum = adamw_small(r_small, row(w), row(m), row(v))

    loss = loss_sum[0, 0]
    outs = [loss, dx[None]]
    for k in range(4):
        parts = {n: o_small[n][k] for n in o_small}
        parts["w_in"] = o_w_in[k].T
        parts["w_out"] = o_w_out[k]
        outs += [parts[n].reshape(shapes[n]) for n in names]
    return tuple(outs)
```

```python
import functools

import jax
import jax.numpy as jnp
import numpy as np
from jax import lax
from jax.experimental import pallas as pl
from jax.experimental.pallas import tpu as pltpu

_MM = jnp.bfloat16

D_MODEL = 1024
CHUNK = 64
CONV_K = 4
EPS = 1e-6
SSD_CONV_DIM = 1536
GDN_HEADS = 8
GDN_DK = 128
GDN_CONV_DIM = 3072
MIX_WIDTH = 2048
IN_DIM = 6688
N_DEV = 8
W_IN_SHARD = IN_DIM // N_DEV
HI = lax.Precision.HIGHEST
HIGH = lax.Precision.HIGH
VMEM_LIMIT = 56 * 1024 * 1024

ADAM_LR = 0.001
ADAM_B1 = 0.9
ADAM_B2 = 0.999
ADAM_EPS = 1e-08
ADAM_WD = 0.01
ADAM_STEP = 10


def _pc(body, **kw):
    return pl.pallas_call(body, **kw)


def _pc_comm(body, **kw):
    return pl.pallas_call(body, **kw)


def _cparams(sem):
    return pltpu.CompilerParams(dimension_semantics=sem, vmem_limit_bytes=VMEM_LIMIT)


def _sig(x):
    return 0.5 * jnp.tanh(0.5 * x) + 0.5


@jax.custom_vjp
def _sigmoid(x):
    return _sig(x)


def _sigmoid_fwd(x):
    s = _sig(x)
    return s, s


def _sigmoid_bwd(s, g):
    return (g * s * (1.0 - s),)


_sigmoid.defvjp(_sigmoid_fwd, _sigmoid_bwd)


@jax.custom_vjp
def _silu(x):
    return x * _sig(x)


def _silu_fwd(x):
    s = _sig(x)
    return x * s, (x, s)


def _silu_bwd(res, g):
    x, s = res
    return (g * (s * (1.0 + x * (1.0 - s))),)


_silu.defvjp(_silu_fwd, _silu_bwd)


def _softplus_impl(x):
    return jnp.maximum(x, 0.0) + jnp.log(1.0 + jnp.exp(-jnp.abs(x)))


@jax.custom_vjp
def _softplus(x):
    return _softplus_impl(x)


def _softplus_fwd(x):
    return _softplus_impl(x), x


def _softplus_bwd(x, g):
    return (g * _sig(x),)


_softplus.defvjp(_softplus_fwd, _softplus_bwd)


def _lane_bcast_impl(x, k):
    return jnp.broadcast_to(x[..., k:k + 1], x.shape)


@functools.partial(jax.custom_vjp, nondiff_argnums=(1,))
def _lane_bcast(x, k):
    return _lane_bcast_impl(x, k)


def _lane_bcast_fwd(x, k):
    return _lane_bcast_impl(x, k), None


def _lane_bcast_bwd(k, _, g):
    lane = lax.broadcasted_iota(jnp.int32, g.shape, g.ndim - 1)
    return (jnp.where(lane == k, jnp.sum(g, axis=-1, keepdims=True), 0.0),)


_lane_bcast.defvjp(_lane_bcast_fwd, _lane_bcast_bwd)


def _mm(a, b):
    return jnp.dot(a.astype(_MM), b.astype(_MM), preferred_element_type=jnp.float32)


def _mm_nt(a, b):
    return lax.dot_general(a.astype(_MM), b.astype(_MM), (((1,), (1,)), ((), ())),
                           preferred_element_type=jnp.float32)


def _mm_tn(a, b):
    return lax.dot_general(a.astype(_MM), b.astype(_MM), (((0,), (0,)), ((), ())),
                           preferred_element_type=jnp.float32)


def _dot_hi(a, b):
    return jnp.dot(a, b, precision=HI, preferred_element_type=jnp.float32)


def _bmm(a, b):
    return lax.dot_general(a.astype(_MM), b.astype(_MM), (((2,), (1,)), ((0,), (0,))),
                           preferred_element_type=jnp.float32)


def _bmm_nt(a, b):
    return lax.dot_general(a.astype(_MM), b.astype(_MM), (((2,), (2,)), ((0,), (0,))),
                           preferred_element_type=jnp.float32)


def _bmm_tn(a, b):
    return lax.dot_general(a.astype(_MM), b.astype(_MM), (((1,), (1,)), ((0,), (0,))),
                           preferred_element_type=jnp.float32)


def _bmm_hi(a, b):
    return lax.dot_general(a, b, (((2,), (1,)), ((0,), (0,))), precision=HIGH, preferred_element_type=jnp.float32)


def _bmm_nt_hi(a, b):
    return lax.dot_general(a, b, (((2,), (2,)), ((0,), (0,))), precision=HIGH, preferred_element_type=jnp.float32)


def _bmm_tn_hi(a, b):
    return lax.dot_general(a, b, (((1,), (1,)), ((0,), (0,))), precision=HIGH, preferred_element_type=jnp.float32)


def _consts():
    l = np.arange(CHUNK)
    tri = (l[:, None] >= l[None, :]).astype(np.float32)
    lane = np.arange(128)
    i2 =(l[:, None] == (lane[None, :] % 64)).astype(np.float32)
    mask2 = (l[:, None] >= (lane[None, :] % 64)).astype(np.float32)
    lo = (lane < 64).astype(np.float32)[None, :]
    i64 = np.eye(CHUNK, dtype=np.float32)
    strict = (l[:, None] > l[None, :]).astype(np.float32)
    return dict(tri=jnp.asarray(tri), i2=jnp.asarray(i2), mask2=jnp.asarray(mask2), lo=jnp.asarray(lo),
                i64=jnp.asarray(i64), strict=jnp.asarray(strict))


def _ssd_chunk(xs_pre, b_pre, c_pre, z, sm, ht, dtb, alog, dpar, nw, tri, i2, mask2, lo):
    lane = lax.broadcasted_iota(jnp.int32, (1, 128), 1)
    m16 = lane < 16
    dt = jnp.where(m16, _softplus(sm + dtb), 0.0)
    a_neg = -jnp.exp(alog)
    cum = _dot_hi(tri, dt * a_neg)
    row = lax.broadcasted_iota(jnp.int32, (CHUNK, 1), 0)
    is_last = row == CHUNK - 1
    hi = 1.0 - lo
    bm = [_silu(b) for b in b_pre]
    cm = [_silu(c) for c in c_pre]
    cb2 = [_mm_nt(cm[g], jnp.concatenate([bm[g], bm[g]], axis=0)) for g in range(2)]
    ht_g = [jnp.concatenate(ht[4 * g:4 * g + 4], axis=1) for g in range(2)]
    yoff_g = [_mm(cm[g], ht_g[g]) for g in range(2)]
    yg, xdec, clast = [], [], []
    for j in range(8):
        g, k4 = j // 4, j % 4
        pair = lambda v, j=j: jnp.where(lo > 0.5, _lane_bcast(v, 2 * j), _lane_bcast(v, 2 * j + 1))
        xs = _silu(xs_pre[j])
        dte = pair(dt)
        cume = pair(cum)
        cum_last = jnp.sum(jnp.where(is_last, cume, 0.0), axis=0, keepdims=True)
        xdt = xs * dte
        rowv = jnp.sum(cume * i2, axis=0, keepdims=True)
        lm = jnp.exp(jnp.where(mask2 > 0.5, cume - rowv, -jnp.inf))
        m = cb2[g] * lm
        xblk = jnp.concatenate([xdt * lo, xdt * hi], axis=0)
        y = _mm(m, xblk)
        y = y + yoff_g[g][:, 128 * k4:128 * k4 + 128] * jnp.exp(cume)
        y = y + pair(dpar) * xs
        yg.append(y * _silu(z[j]))
        xdec.append(xdt * jnp.exp(cum_last - cume))
        clast.append(cum_last)
    ht_next = []
    for g in range(2):
        st = _mm_tn(bm[g], jnp.concatenate(xdec[4 * g:4 * g + 4], axis=1))
        for k4 in range(4):
            j = 4 * g + k4
            ht_next.append(ht[j] * jnp.exp(clast[j]) + st[:, 128 * k4:128 * k4 + 128])
    outs = []
    for g in range(2):
        ss = sum(jnp.sum(yg[j] * yg[j], axis=-1, keepdims=True) for j in range(4 * g, 4 * g + 4))
        rs = lax.rsqrt(ss * (1.0 / 512.0) + EPS)
        for j in range(4 * g, 4 * g + 4):
            outs.append(yg[j] * rs * nw[j])
    return outs, ht_next


def _tri_inverse(a):
    eye = jnp.eye(CHUNK, dtype=jnp.float32)[None]
    p = eye - a
    x = _bmm_hi(a, a)
    for i in range(4):
        both = (_bmm_hi if i == 0 else _bmm)(jnp.concatenate([p, x], axis=1), x)
        p = p + both[:, :CHUNK]
        x = both[:, CHUNK:]
    return p + _bmm(p, x)


def _solve_apply(t, r1, r2):
    both = _bmm_hi(t, jnp.concatenate([r1, r2], axis=-1))
    n = r1.shape[-1]
    return both[..., :n], both[..., n:]


@jax.custom_vjp
def _solve(a, r1, r2, t):
    return _solve_apply(t, r1, r2)


def _solve_fwd(a, r1, r2, t):
    u, w = _bmm_hi(t, r1), _bmm_hi(t, r2)
    return (u, w), (t, u, w)


def _solve_bwd(res, cts):
    t, u, w = res
    du, dw = cts
    dr1 = _bmm_tn_hi(t, du)
    dr2 = _bmm_tn_hi(t, dw)
    da = -(_bmm_nt_hi(dr1, u) + _bmm_nt_hi(dr2, w))
    return da, dr1, dr2, jnp.zeros_like(t)


_solve.defvjp(_solve_fwd, _solve_bwd)


def _gdn_chunk(q_pre, k_pre, v_pre, gate, sm, s, dtb, alog, nw, tri, i64, strict, t_in=None):
    lane = lax.broadcasted_iota(jnp.int32, (1, 128), 1)
    m_a = (lane >= 16) & (lane < 24)
    g_full = jnp.where(m_a, -jnp.exp(alog) * _softplus(sm + dtb), 0.0)
    gc = _dot_hi(tri, g_full)
    sig = _sigmoid(sm)
    heads = lambda f: jnp.concatenate([f(h)[None] for h in range(GDN_HEADS)], axis=0)
    gc3 = heads(lambda h: _lane_bcast(gc, 16 + h))
    beta3 = heads(lambda h: _lane_bcast(sig, 24 + h))
    q = _silu(q_pre)
    q = q * lax.rsqrt(jnp.sum(q * q, axis=-1, keepdims=True) + EPS) * (GDN_DK ** -0.5)
    k = _silu(k_pre)
    k = k * lax.rsqrt(jnp.sum(k * k, axis=-1, keepdims=True) + EPS)
    v = _silu(v_pre)
    gcl = gc3[:, :, :CHUNK]
    gc_row = jnp.sum(gcl * i64[None], axis=1, keepdims=True)
    incl = (strict + i64)[None] > 0.5
    decay = jnp.exp(jnp.where(incl, gcl - gc_row, -jnp.inf))
    kb = k * beta3
    a = jnp.where(strict[None] > 0.5, _bmm_nt(kb, k) * decay, 0.0)
    egc = jnp.exp(gc3)
    t = _tri_inverse(a) if t_in is None else t_in
    u, w = _solve(a, v * beta3, kb * egc, t)
    attn = _bmm_nt(q, k) * decay
    row = lax.broadcasted_iota(jnp.int32, (1, CHUNK, 1), 1)
    gl = jnp.sum(jnp.where(row == CHUNK - 1, gc3, 0.0), axis=1, keepdims=True)
    q_dec = q * egc
    k_dec = k * jnp.exp(gl - gc3)
    ws = _bmm(jnp.concatenate([w, q_dec], axis=1), s)
    v_new = u - ws[:, :CHUNK]
    o = ws[:, CHUNK:] + _bmm(attn, v_new)
    s_next = s * jnp.exp(gl) + _bmm_tn(k_dec, v_new)
    on = o * lax.rsqrt(jnp.mean(o * o, axis=-1, keepdims=True) + EPS) * nw
    return on * _silu(gate), s_next, t


def _conv_fwd(pbuf, w_ref, c0, c1):
    blk = pbuf[:, c0:c1]
    acc = w_ref[CONV_K - 1:CONV_K, c0:c1] * blk[8:72]
    for j in range(CONV_K - 1):
        acc = acc + w_ref[j:j + 1, c0:c1] * pltpu.roll(blk, CONV_K - 1 - j, axis=0)[8:72]
    return acc


MESH = pl.DeviceIdType.MESH
ANY = pl.BlockSpec(memory_space=pl.ANY)


def _me():
    x, y, c = lax.axis_index("x"), lax.axis_index("y"), lax.axis_index("c")
    return x, y, c, 4 * x + 2 * y + c


def _peer(r):
    x, y, c, _ = _me()
    px = 1 - x if r & 4 else x
    py = 1 - y if r & 2 else y
    pc = 1 - c if r & 1 else c
    return (px, py, pc), 4 * px + 2 * py + pc


def _exchange_ops(kind, in_ref, out_ref, send_sems, recv_sems, local_sem):
    me = _me()[3]
    local = pltpu.make_async_copy(in_ref.at[me] if kind == "scatter" else in_ref, out_ref.at[me], local_sem)
    sends, recvs = [], []
    for r in range(1, N_DEV):
        peer, pidx = _peer(r)
        src = in_ref.at[pidx] if kind == "scatter" else in_ref
        sems = dict(send_sem=send_sems.at[r - 1], recv_sem=recv_sems.at[r - 1], device_id=peer, device_id_type=MESH)
        sends.append(pltpu.make_async_remote_copy(src_ref=src, dst_ref=out_ref.at[me], **sems))
        recvs.append(pltpu.make_async_remote_copy(src_ref=src, dst_ref=out_ref.at[pidx], **sems))

    def start():
        local.start()
        for cp in sends:
            cp.start()

    def wait():
        for cp in recvs:
            cp.wait_recv()
        for cp in sends:
            cp.wait_send()
        local.wait()

    return start, wait


def _exchange_sems(n):
    return [pltpu.SemaphoreType.DMA((N_DEV - 1,)), pltpu.SemaphoreType.DMA((N_DEV - 1,)),
            pltpu.SemaphoreType.DMA(())] * n


def _exchange_out_shape(kind, a):
    return jax.ShapeDtypeStruct(a.shape if kind == "scatter" else (N_DEV,) + a.shape, a.dtype)


def _hosting(body, n_in, n_out, n_scratch, kinds, first, last):
    ne = len(kinds)

    def wrapped(*refs):
        ins, ex_in = refs[:n_in], refs[n_in:n_in + ne]
        o0 = n_in + ne
        outs, ex_out = refs[o0:o0 + n_out], refs[o0 + n_out:o0 + n_out + ne]
        s0 = o0 + n_out + ne
        scr, sems = refs[s0:s0 + n_scratch], refs[s0 + n_scratch:]
        ops = [_exchange_ops(kinds[e], ex_in[e], ex_out[e], *sems[3 * e:3 * e + 3]) for e in range(ne)]

        @pl.when(first())
        def _():
            for start, _ in ops:
                start()

        body(*ins, *outs, *scr)

        @pl.when(last())
        def _():
            for _, wait in ops:
                wait()

    return wrapped


GROUPS = (("z", 0, 1024), ("xbc", 1024, 2560), ("gate", 2560, 3584), ("qkv", 3584, 6656), ("sm", 6656, 6784))
GROUP_ROWS = dict(z=((0, 1024),), xbc=((1024, 2560),), gate=((2576, 3600),), qkv=((3600, 6672),),
                  sm=((2560, 2576), (6672, 6688)))


W_SLOT = 848
W_GATHERED = N_DEV * W_SLOT


def _slot_pos(s, c):
    return W_SLOT * s + (c - W_IN_SHARD * s) + (W_IN_SHARD * s) % 16


def _w_pieces(w_ref, a, b):
    out, cur = [], a
    while cur < b:
        s = cur // W_IN_SHARD
        end = W_IN_SHARD * (s + 1)
        if end >= b:
            out.append(w_ref[_slot_pos(s, cur):_slot_pos(s, b), :])
            break
        hi = end // 16 * 16
        if hi > cur:
            out.append(w_ref[_slot_pos(s, cur):_slot_pos(s, hi), :])
        if end % 16:
            p, q = _slot_pos(s, hi), W_SLOT * (s + 1)
            out.append(w_ref[p:p + 16, :] + w_ref[q:q + 16, :])
            cur = hi + 16
        else:
            cur = hi
    return out


def _w_rows(w_ref, name, width):
    pieces = [p for a, b in GROUP_ROWS[name] for p in _w_pieces(w_ref, a, b)]
    n = sum(b - a for a, b in GROUP_ROWS[name])
    if n < width:
        pieces.append(jnp.zeros((width - n, D_MODEL), w_ref.dtype))
    return pieces[0] if len(pieces) == 1 else jnp.concatenate(pieces, axis=0)


def inproj_fwd(x, norm_w, w_perm, gathered):
    t = x.shape[0]
    tm = min(512, t)
    steps = t // tm
    kinds = ["gather"] * len(gathered)

    def body(x_ref, nw_ref, w_ref, u_ref, z_ref, xbc_ref, gate_ref, qkv_ref, sm_ref):
        xf = x_ref[...]
        rstd = lax.rsqrt(jnp.mean(xf * xf, axis=-1, keepdims=True) + EPS)
        u = (xf * rstd * nw_ref[...]).astype(_MM)
        u_ref[...] = u
        for (name, c0, c1), o_ref in zip(GROUPS, (z_ref, xbc_ref, gate_ref, qkv_ref, sm_ref)):
            o_ref[...] = lax.dot_general(u, _w_rows(w_ref, name, c1 - c0), (((1,), (1,)), ((), ())),
                                         preferred_element_type=jnp.float32)

    outs = [jax.ShapeDtypeStruct((t, D_MODEL), _MM)] + [jax.ShapeDtypeStruct((t, c1 - c0), jnp.float32)
                                                        for _, c0, c1 in GROUPS]
    hosted = _hosting(body, 3, 6, 0, kinds, lambda: pl.program_id(0) == 0, lambda: pl.program_id(0) == steps - 1)
    return _pc_comm(
        hosted, name="inproj_fwd", grid=(steps,),
        in_specs=[pl.BlockSpec((tm, D_MODEL), lambda i: (i, 0)),
                  pl.BlockSpec((1, D_MODEL), lambda i: (0, 0)),
                  pl.BlockSpec((W_GATHERED, D_MODEL), lambda i: (0, 0), pipeline_mode=pl.Buffered(1))] +
                 [ANY] * len(gathered),
        out_specs=[pl.BlockSpec((tm, D_MODEL), lambda i: (i, 0))] +
                  [pl.BlockSpec((tm, c1 - c0), lambda i: (i, 0)) for _, c0, c1 in GROUPS] + [ANY] * len(gathered),
        out_shape=outs + [_exchange_out_shape("gather", a) for a in gathered],
        scratch_shapes=_exchange_sems(len(gathered)), compiler_params=_cparams(("arbitrary",)),
    )(x, norm_w, w_perm, *gathered)


SUB_FWD = 4
SUB_BWD = 2


def _halo_spec(width, idx_fn):
    return pl.BlockSpec((8, width), lambda i: (jnp.maximum(idx_fn(i) * (SUB_FWD * CHUNK // 8) - 1, 0), 0))


def _when_first(shared, fn):
    if shared["first"] is not False:
        pl.when(shared["first"])(fn)


def _full(shape):
    nd = len(shape)
    return pl.BlockSpec(shape, lambda i: (0,) * nd)


def _ssd_split(pre_fn, z_ref, sm_ref):
    xs_pre = [pre_fn(128 * j, 128 * j + 128) for j in range(8)]
    b_pre = [pre_fn(1024 + 128 * g, 1152 + 128 * g) for g in range(2)]
    c_pre = [pre_fn(1280 + 128 * g, 1408 + 128 * g) for g in range(2)]
    z = [z_ref[:, 128 * j:128 * j + 128] for j in range(8)]
    return xs_pre, b_pre, c_pre, z, sm_ref[...]


def ssd_fwd(z, xbc, sm, conv_w, conv_b, dtb, alog, dpar, nw, cs):
    t = z.shape[0]
    nc = t // CHUNK

    def body(shared, z_ref, xbc_ref, halo_ref, sm_ref, cw_ref, cb_ref, dtb_ref, alog_ref, dpar_ref, nw_ref,
             tri_ref, i2_ref, mask2_ref, lo_ref, y_ref, hs_ref, pre_ref, pbuf, ht_scr):
        def init():
            ht_scr[...] = jnp.zeros_like(ht_scr)

        _when_first(shared, init)
        pbuf[0:8, :] = jnp.where(shared["first"], 0.0, halo_ref[...])
        pbuf[8:72, :] = xbc_ref[...]

        def pre_fn(c0, c1):
            pre = _conv_fwd(pbuf, cw_ref, c0, c1) + cb_ref[:, c0:c1]
            pre_ref[:, c0:c1] = pre
            return pre

        xs_pre, b_pre, c_pre, zz, smv = _ssd_split(pre_fn, z_ref, sm_ref)
        ht = [ht_scr[:, 128 * j:128 * j + 128] for j in range(8)]
        hs_ref[0] = ht_scr[...]
        nwl = [nw_ref[:, 128 * j:128 * j + 128] for j in range(8)]
        outs, ht_next = _ssd_chunk(xs_pre, b_pre, c_pre, zz, smv, ht, dtb_ref[...], alog_ref[...], dpar_ref[...],
                                   nwl, tri_ref[...], i2_ref[...], mask2_ref[...], lo_ref[...])
        for j in range(8):
            y_ref[:, 128 * j:128 * j + 128] = outs[j].astype(y_ref.dtype)
            ht_scr[:, 128 * j:128 * j + 128] = ht_next[j]

    blk = lambda w: pl.BlockSpec((SUB_FWD * CHUNK, w), lambda i: (i, 0))
    return dict(
        body=body,
        in_kinds=["rows", "rows", ("halo", 1), "rows"] + ["full"] * 10, out_kinds=["rows", "state", "rows"],
        in_specs=[blk(1024), blk(1536), _halo_spec(1536, lambda i: i), blk(128),
                  _full((CONV_K, 1536)), _full((1, 1536)), _full((1, 128)), _full((1, 128)), _full((1, 128)),
                  _full((1, 1024)), _full((64, 64)), _full((64, 128)), _full((64, 128)),
                  _full((1, 128))],
        out_specs=[blk(1024), pl.BlockSpec((SUB_FWD, 128, 1024), lambda i: (i, 0, 0)), blk(1536)],
        out_shape=[jax.ShapeDtypeStruct((t, 1024), _MM), jax.ShapeDtypeStruct((nc, 128, 1024), jnp.float32),
                   jax.ShapeDtypeStruct((t, 1536), jnp.float32)],
        scratch=[pltpu.VMEM((72, 1536), jnp.float32), pltpu.VMEM((128, 1024), jnp.float32)],
        args=[z, xbc, xbc, sm, conv_w, conv_b, dtb, alog, dpar, nw, cs["tri"], cs["i2"], cs["mask2"], cs["lo"]])


def _conv_bwd(dpre_list, col_ranges, dbuf, carry, x_ref, cw_ref, dx_ref, dcw_ref, dcb_ref, first):
    for dpre, (c0, c1) in zip(dpre_list, col_ranges):
        dbuf[0:64, c0:c1] = dpre
    dbuf[64:72, :] = jnp.where(first, 0.0, carry[...])
    carry[...] = dbuf[0:8, :]
    for (c0, c1) in col_ranges:
        xin = x_ref[:, c0:c1]
        blk = dbuf[:, c0:c1]
        acc = None
        for j in range(CONV_K):
            sh = blk[0:64] if j == CONV_K - 1 else pltpu.roll(blk, 72 - (CONV_K - 1 - j), axis=0)[0:64]
            term = cw_ref[j:j + 1, c0:c1] * sh
            acc = term if acc is None else acc + term
            dcw_ref[j:j + 1, c0:c1] += jnp.sum(xin * sh, axis=0, keepdims=True)
        dx_ref[:, c0:c1] = acc.astype(dx_ref.dtype)
        if dcb_ref is not None:
            dcb_ref[0:1, c0:c1] += jnp.sum(dbuf[0:64, c0:c1], axis=0, keepdims=True)


def ssd_bwd(z, xbc, pre, sm, hs, dy, conv_w, dtb, alog, dpar, nw, cs):
    t = z.shape[0]
    nc = t // CHUNK

    def body(shared, z_ref, xbc_ref, pre_ref, sm_ref, hs_ref, dy_ref, cw_ref, dtb_ref, alog_ref, dpar_ref, nw_ref,
             tri_ref, i2_ref, mask2_ref, lo_ref,
             dz_ref, dxbc_ref, dcw_ref, dcb_ref, ddtb_ref, dalog_ref, ddpar_ref, dnw_ref,
             dbuf, carry, dht_scr):
        def init():
            dht_scr[...] = jnp.zeros_like(dht_scr)
            dcw_ref[...] = jnp.zeros_like(dcw_ref)
            dcb_ref[...] = jnp.zeros_like(dcb_ref)
            ddtb_ref[...] = jnp.zeros_like(ddtb_ref)
            dalog_ref[...] = jnp.zeros_like(dalog_ref)
            ddpar_ref[...] = jnp.zeros_like(ddpar_ref)
            dnw_ref[...] = jnp.zeros_like(dnw_ref)

        _when_first(shared, init)
        pre_fn = lambda c0, c1: pre_ref[:, c0:c1]
        xs_pre, b_pre, c_pre, zz, smv = _ssd_split(pre_fn, z_ref, sm_ref)
        ht = [hs_ref[0, :, 128 * j:128 * j + 128] for j in range(8)]
        nwl = [nw_ref[:, 128 * j:128 * j + 128] for j in range(8)]
        consts = (tri_ref[...], i2_ref[...], mask2_ref[...], lo_ref[...])

        def f(xs_pre, b_pre, c_pre, zz, smv, ht, dtb, alog, dpar, nwl):
            return _ssd_chunk(xs_pre, b_pre, c_pre, zz, smv, ht, dtb, alog, dpar, nwl, *consts)

        _, vjp = jax.vjp(f, xs_pre, b_pre, c_pre, zz, smv, ht, dtb_ref[...], alog_ref[...], dpar_ref[...], nwl)
        dys = [dy_ref[:, 128 * j:128 * j + 128] for j in range(8)]
        dhts = [dht_scr[:, 128 * j:128 * j + 128] for j in range(8)]
        dxs, db, dc, dzz, dsm, dht, ddtb, dalog, ddpar, dnwl = vjp((dys, dhts))
        for j in range(8):
            dz_ref[:, 128 * j:128 * j + 128] = dzz[j].astype(dz_ref.dtype)
            dht_scr[:, 128 * j:128 * j + 128] = dht[j]
            dnw_ref[0:1, 128 * j:128 * j + 128] += dnwl[j]
        shared["dsm_ssd"] = dsm
        ddtb_ref[0:1, :] += ddtb
        dalog_ref[0:1, :] += dalog
        ddpar_ref[0:1, :] += ddpar
        ranges = ([(128 * j, 128 * j + 128) for j in range(8)] + [(1024 + 128 * g, 1152 + 128 * g) for g in range(2)]
                  + [(1280 + 128 * g, 1408 + 128 * g) for g in range(2)])
        _conv_bwd(dxs + db + dc, ranges, dbuf, carry, xbc_ref, cw_ref, dxbc_ref, dcw_ref, dcb_ref, shared["first"])

    ns = nc // SUB_BWD
    rblk = lambda w: pl.BlockSpec((SUB_BWD * CHUNK, w), lambda i: (ns - 1 - i, 0))
    acc = lambda w: pl.BlockSpec((8, w), lambda i: (0, 0))
    f32 = jnp.float32
    return dict(
        body=body,
        in_kinds=["rows"] * 4 + ["state", "rows"] + ["full"] * 9, out_kinds=["rows", "rows"] + ["full"] * 6,
        in_specs=[rblk(1024), rblk(1536), rblk(1536), rblk(128),
                  pl.BlockSpec((SUB_BWD, 128, 1024), lambda i: (ns - 1 - i, 0, 0)), rblk(1024),
                  _full((CONV_K, 1536)), _full((1, 128)), _full((1, 128)), _full((1, 128)),
                  _full((1, 1024)), _full((64, 64)), _full((64, 128)), _full((64, 128)),
                  _full((1, 128))],
        out_specs=[rblk(1024), rblk(1536), acc(1536), acc(1536), acc(128), acc(128), acc(128), acc(1024)],
        out_shape=[jax.ShapeDtypeStruct((t, 1024), f32), jax.ShapeDtypeStruct((t, 1536), f32),
                   jax.ShapeDtypeStruct((8, 1536), f32),
                   jax.ShapeDtypeStruct((8, 1536), f32), jax.ShapeDtypeStruct((8, 128), f32),
                   jax.ShapeDtypeStruct((8, 128), f32), jax.ShapeDtypeStruct((8, 128), f32),
                   jax.ShapeDtypeStruct((8, 1024), f32)],
        scratch=[pltpu.VMEM((72, 1536), f32), pltpu.VMEM((8, 1536), f32), pltpu.VMEM((128, 1024), f32)],
        args=[z, xbc, pre, sm, hs, dy, conv_w, dtb, alog, dpar, nw, cs["tri"], cs["i2"], cs["mask2"], cs["lo"]])


def _gdn_split(pre_fn, gate_ref):
    def heads(base):
        return jnp.stack([pre_fn(base + 128 * h, base + 128 * h + 128) for h in range(GDN_HEADS)])
    gate = jnp.stack([gate_ref[:, 128 * h:128 * h + 128] for h in range(GDN_HEADS)])
    return heads(0), heads(1024), heads(2048), gate


def gdn_fwd(gate, qkv, sm, conv_w, dtb, alog, nw, cs):
    t = gate.shape[0]
    nc = t // CHUNK

    def body(shared, gate_ref, qkv_ref, halo_ref, sm_ref, cw_ref, dtb_ref, alog_ref, nw_ref,
             tri_ref, i64_ref, strict_ref, o_ref, ss_ref, ts_ref, pre_ref, pbuf, s_scr):
        def init():
            s_scr[...] = jnp.zeros_like(s_scr)

        _when_first(shared, init)
        pbuf[0:8, :] = jnp.where(shared["first"], 0.0, halo_ref[...])
        pbuf[8:72, :] = qkv_ref[...]

        def pre_fn(c0, c1):
            pre = _conv_fwd(pbuf, cw_ref, c0, c1)
            pre_ref[:, c0:c1] = pre
            return pre

        q_pre, k_pre, v_pre, g3 = _gdn_split(pre_fn, gate_ref)
        s = s_scr[...]
        ss_ref[0] = s
        out, s_next, tinv = _gdn_chunk(q_pre, k_pre, v_pre, g3, sm_ref[...], s, dtb_ref[...], alog_ref[...],
                                       nw_ref[...], tri_ref[...], i64_ref[...], strict_ref[...])
        ts_ref[0] = tinv
        s_scr[...] = s_next
        for h in range(GDN_HEADS):
            o_ref[:, 128 * h:128 * h + 128] = out[h].astype(o_ref.dtype)

    blk = lambda w: pl.BlockSpec((SUB_FWD * CHUNK, w), lambda i: (i, 0))
    return dict(
        body=body,
        in_kinds=["rows", "rows", ("halo", 1), "rows"] + ["full"] * 7, out_kinds=["rows", "state", "state", "rows"],
        in_specs=[blk(1024), blk(3072), _halo_spec(3072, lambda i: i), blk(128),
                  _full((CONV_K, 3072)), _full((1, 128)), _full((1, 128)), _full((1, 128)),
                  _full((64, 64)), _full((64, 64)), _full((64, 64))],
        out_specs=[blk(1024), pl.BlockSpec((SUB_FWD, 8, 128, 128), lambda i: (i, 0, 0, 0)),
                   pl.BlockSpec((SUB_FWD, 8, CHUNK, CHUNK), lambda i: (i, 0, 0, 0)), blk(3072)],
        out_shape=[jax.ShapeDtypeStruct((t, 1024), _MM), jax.ShapeDtypeStruct((nc, 8, 128, 128), jnp.float32),
                   jax.ShapeDtypeStruct((nc, 8, CHUNK, CHUNK), jnp.float32),
                   jax.ShapeDtypeStruct((t, 3072), jnp.float32)],
        scratch=[pltpu.VMEM((72, 3072), jnp.float32), pltpu.VMEM((8, 128, 128), jnp.float32)],
        args=[gate, qkv, qkv, sm, conv_w, dtb, alog, nw, cs["tri"], cs["i64"], cs["strict"]])


def gdn_bwd(gate, qkv, pre, sm, ss, ts, do, conv_w, dtb, alog, nw, cs):
    t = gate.shape[0]
    nc = t // CHUNK

    def body(shared, gate_ref, qkv_ref, pre_ref, sm_ref, ss_ref, ts_ref, do_ref, cw_ref, dtb_ref, alog_ref,
             nw_ref, tri_ref, i64_ref, strict_ref,
             dgate_ref, dqkv_ref, dsm_ref, dcw_ref, ddtb_ref, dalog_ref, dnw_ref,
             dbuf, carry, ds_scr):
        def init():
            ds_scr[...] = jnp.zeros_like(ds_scr)
            dcw_ref[...] = jnp.zeros_like(dcw_ref)
            ddtb_ref[...] = jnp.zeros_like(ddtb_ref)
            dalog_ref[...] = jnp.zeros_like(dalog_ref)
            dnw_ref[...] = jnp.zeros_like(dnw_ref)

        _when_first(shared, init)

        q_pre, k_pre, v_pre, g3 = _gdn_split(lambda c0, c1: pre_ref[:, c0:c1], gate_ref)
        consts = (tri_ref[...], i64_ref[...], strict_ref[...], ts_ref[0])

        def f(q_pre, k_pre, v_pre, g3, smv, s, dtb, alog, nwv):
            return _gdn_chunk(q_pre, k_pre, v_pre, g3, smv, s, dtb, alog, nwv, *consts)[:2]

        _, vjp = jax.vjp(f, q_pre, k_pre, v_pre, g3, sm_ref[...], ss_ref[0], dtb_ref[...], alog_ref[...], nw_ref[...])
        do3 = jnp.stack([do_ref[:, 128 * h:128 * h + 128] for h in range(GDN_HEADS)])
        dq, dk, dv, dg3, dsm, ds, ddtb, dalog, dnw = vjp((do3, ds_scr[...]))
        ds_scr[...] = ds
        for h in range(GDN_HEADS):
            dgate_ref[:, 128 * h:128 * h + 128] = dg3[h].astype(dgate_ref.dtype)
        dsm_ref[...] = (dsm + shared["dsm_ssd"]).astype(dsm_ref.dtype)
        ddtb_ref[0:1, :] += ddtb
        dalog_ref[0:1, :] += dalog
        dnw_ref[0:1, :] += dnw
        ranges = [(base + 128 * h, base + 128 * h + 128) for base in (0, 1024, 2048) for h in range(GDN_HEADS)]
        dlist = [d[h] for d in (dq, dk, dv) for h in range(GDN_HEADS)]
        _conv_bwd(dlist, ranges, dbuf, carry, qkv_ref, cw_ref, dqkv_ref, dcw_ref, None, shared["first"])

    ns = nc // SUB_BWD
    rblk = lambda w: pl.BlockSpec((SUB_BWD * CHUNK, w), lambda i: (ns - 1 - i, 0))
    acc = lambda w: pl.BlockSpec((8, w), lambda i: (0, 0))
    f32 = jnp.float32
    return dict(
        body=body,
        in_kinds=["rows"] * 4 + ["state", "state", "rows"] + ["full"] * 7, out_kinds=["rows"] * 3 + ["full"] * 4,
        in_specs=[rblk(1024), rblk(3072), rblk(3072), rblk(128),
                  pl.BlockSpec((SUB_BWD, 8, 128, 128), lambda i: (ns - 1 - i, 0, 0, 0)),
                  pl.BlockSpec((SUB_BWD, 8, CHUNK, CHUNK), lambda i: (ns - 1 - i, 0, 0, 0)), rblk(1024),
                  _full((CONV_K, 3072)), _full((1, 128)), _full((1, 128)), _full((1, 128)),
                  _full((64, 64)), _full((64, 64)), _full((64, 64))],
        out_specs=[rblk(1024), rblk(3072), rblk(128), acc(3072), acc(128), acc(128), acc(128)],
        out_shape=[jax.ShapeDtypeStruct((t, 1024), f32), jax.ShapeDtypeStruct((t, 3072), f32),
                   jax.ShapeDtypeStruct((t, 128), f32), jax.ShapeDtypeStruct((8, 3072), f32),
                   jax.ShapeDtypeStruct((8, 128), f32), jax.ShapeDtypeStruct((8, 128), f32),
                   jax.ShapeDtypeStruct((8, 128), f32)],
        scratch=[pltpu.VMEM((72, 3072), f32), pltpu.VMEM((8, 3072), f32), pltpu.VMEM((8, 128, 128), f32)],
        args=[gate, qkv, pre, sm, ss, ts, do, conv_w, dtb, alog, nw, cs["tri"], cs["i64"], cs["strict"]])


def _chunk_call(parts, name, nc, reverse):
    n_in = [len(p["args"]) for p in parts]
    n_out = [len(p["out_shape"]) for p in parts]
    n_scr = [len(p["scratch"]) for p in parts]
    sub = SUB_BWD if reverse else SUB_FWD
    order = list(range(sub))[::-1] if reverse else list(range(sub))

    def view(ref, kind, s, refs):
        if kind == "rows":
            return ref.at[pl.ds(CHUNK * s, CHUNK)]
        if kind == "state":
            return ref.at[pl.ds(s, 1)]
        if kind == "full":
            return ref
        src = refs[kind[1]]
        return ref if s == 0 else src.at[pl.ds(CHUNK * s - 8, 8)]

    def body(*refs):
        ins, outs, scr = refs[:sum(n_in)], refs[sum(n_in):sum(n_in) + sum(n_out)], refs[sum(n_in) + sum(n_out):]
        for s in order:
            shared = {"first": (pl.program_id(0) == 0) if s == order[0] else False}
            for k, p in enumerate(parts):
                i0, o0, s0 = sum(n_in[:k]), sum(n_out[:k]), sum(n_scr[:k])
                p_ins = ins[i0:i0 + n_in[k]]
                p["body"](shared,
                          *[view(r, kd, s, p_ins) for r, kd in zip(p_ins, p["in_kinds"])],
                          *[view(r, kd, s, None) for r, kd in zip(outs[o0:o0 + n_out[k]], p["out_kinds"])],
                          *scr[s0:s0 + n_scr[k]])

    cat = lambda key: [v for p in parts for v in p[key]]
    return _pc(body, name=name, grid=(nc // sub,), in_specs=cat("in_specs"), out_specs=cat("out_specs"),
               out_shape=cat("out_shape"), scratch_shapes=cat("scratch"),
               compiler_params=_cparams(("arbitrary",)))(*cat("args"))


def out_fwd_bwd(x, tgt, y_ssd, y_gdn, w_out, fnw):
    t = x.shape[0]
    tm = min(512, t)
    f32 = jnp.float32

    def body(x_ref, tgt_ref, ys_ref, yg_ref, w_ref, fnw_ref,
             dout_ref, dys_ref, dyg_ref, gw_ref, gfnw_ref, loss_ref, gw_acc):
        i = pl.program_id(0)

        @pl.when(i == 0)
        def _():
            gw_acc[...] = jnp.zeros_like(gw_acc)
            gfnw_ref[...] = jnp.zeros_like(gfnw_ref)
            loss_ref[...] = jnp.zeros_like(loss_ref)

        ys = ys_ref[...]
        yg = yg_ref[...]
        out = x_ref[...] + jnp.dot(ys, w_ref[0:1024, :], preferred_element_type=f32) \
            + jnp.dot(yg, w_ref[1024:2048, :], preferred_element_type=f32)
        rstd = lax.rsqrt(jnp.mean(out * out, axis=-1, keepdims=True) + EPS)
        yhat = out * rstd
        fw = fnw_ref[...]
        e = yhat * fw - tgt_ref[...]
        loss_ref[...] += 0.5 * jnp.sum(jnp.sum(e * e, axis=-1, keepdims=True) * (1.0 / D_MODEL), axis=0, keepdims=True)
        dyf = e * (1.0 / D_MODEL)
        gfnw_ref[0:1, :] += jnp.sum(dyf * yhat, axis=0, keepdims=True)
        dyhat = dyf * fw
        dout = rstd * (dyhat - yhat * jnp.mean(dyhat * yhat, axis=-1, keepdims=True))
        dout_ref[...] = dout
        db = dout.astype(_MM)
        dys_ref[...] = lax.dot_general(db, w_ref[0:1024, :], (((1,), (1,)), ((), ())), preferred_element_type=f32)
        dyg_ref[...] = lax.dot_general(db, w_ref[1024:2048, :], (((1,), (1,)), ((), ())), preferred_element_type=f32)
        gw_acc[0:1024, :] += lax.dot_general(ys, db, (((0,), (0,)), ((), ())), preferred_element_type=f32)
        gw_acc[1024:2048, :] += lax.dot_general(yg, db, (((0,), (0,)), ((), ())), preferred_element_type=f32)

        @pl.when(i == steps - 1)
        def _():
            gw_ref[...] = gw_acc[...].astype(gw_ref.dtype)

    steps = t // tm
    blk = pl.BlockSpec((tm, D_MODEL), lambda i: (i, 0))
    return _pc(
        body, name="out_fwd_bwd", grid=(steps,),
        in_specs=[blk, blk, blk, blk, _full((MIX_WIDTH, D_MODEL)), _full((1, D_MODEL))],
        out_specs=[blk, blk, blk, _full((MIX_WIDTH, D_MODEL)), _full((8, D_MODEL)), _full((1, 128))],
        out_shape=[jax.ShapeDtypeStruct((t, D_MODEL), f32)] * 3 +
                  [jax.ShapeDtypeStruct((MIX_WIDTH, D_MODEL), _MM), jax.ShapeDtypeStruct((8, D_MODEL), f32),
                   jax.ShapeDtypeStruct((1, 128), f32)],
        scratch_shapes=[pltpu.VMEM((MIX_WIDTH, D_MODEL), f32)],
        compiler_params=_cparams(("arbitrary",)),
    )(x, tgt, y_ssd, y_gdn, w_out, fnw)


def inproj_bwd_dx(x, dout, norm_w, w_perm, dgroups, scattered):
    t = x.shape[0]
    tm = min(256, t)
    f32 = jnp.float32

    def body(x_ref, dout_ref, nw_ref, w_ref, dz_ref, dxbc_ref, dgate_ref, dqkv_ref, dsm_ref, dx_ref, gnw_ref):
        i = pl.program_id(0)

        @pl.when(i == 0)
        def _():
            gnw_ref[...] = jnp.zeros_like(gnw_ref)

        du = None
        for (name, c0, c1), d_ref in zip(GROUPS, (dz_ref, dxbc_ref, dgate_ref, dqkv_ref, dsm_ref)):
            term = jnp.dot(d_ref[...].astype(_MM), _w_rows(w_ref, name, c1 - c0), preferred_element_type=f32)
            du = term if du is None else du + term
        xf = x_ref[...]
        rstd = lax.rsqrt(jnp.mean(xf * xf, axis=-1, keepdims=True) + EPS)
        xhat = xf * rstd
        gnw_ref[0:1, :] += jnp.sum(du * xhat, axis=0, keepdims=True)
        dxh = du * nw_ref[...]
        dx_ref[...] = dout_ref[...] + rstd * (dxh - xhat * jnp.mean(dxh * xhat, axis=-1, keepdims=True))

    blk = lambda w: pl.BlockSpec((tm, w), lambda i: (i, 0))
    steps = t // tm
    kinds = ["scatter"] * len(scattered)
    hosted = _hosting(body, 9, 2, 0, kinds, lambda: pl.program_id(0) == 0, lambda: pl.program_id(0) == steps - 1)
    return _pc_comm(
        hosted, name="inproj_bwd_dx", grid=(steps,),
        in_specs=[blk(D_MODEL), blk(D_MODEL), _full((1, D_MODEL)), _full((W_GATHERED, D_MODEL))] +
                 [blk(c1 - c0) for _, c0, c1 in GROUPS] + [ANY] * len(scattered),
        out_specs=[blk(D_MODEL), _full((8, D_MODEL))] + [ANY] * len(scattered),
        out_shape=[jax.ShapeDtypeStruct((t, D_MODEL), f32), jax.ShapeDtypeStruct((8, D_MODEL), f32)] +
                  [_exchange_out_shape("scatter", a) for a in scattered],
        scratch_shapes=_exchange_sems(len(scattered)), compiler_params=_cparams(("arbitrary",)),
    )(x, dout, norm_w, w_perm, *dgroups, *scattered)


def grad_w_group(u, dg, name, scattered=()):
    t, n = dg.shape
    tn = n if n <= 1536 else 1024
    budget = 40 * 1024 * 1024
    tm = next((c for c in (4096, 2048, 1024, 512, 256)
               if t % c == 0 and tn * D_MODEL * 4 + 2 * (c * tn * 4 + c * D_MODEL * 2 + tn * D_MODEL * 2) <= budget), t)
    nj, nk = n // tn, t // tm
    f32 = jnp.float32

    def body(u_ref, d_ref, o_ref, acc):
        k = pl.program_id(1)

        @pl.when(k == 0)
        def _():
            acc[...] = jnp.zeros_like(acc)

        acc[...] += lax.dot_general(d_ref[...].astype(_MM), u_ref[...], (((0,), (0,)), ((), ())),
                                    preferred_element_type=f32)

        @pl.when(k == nk - 1)
        def _():
            o_ref[...] = acc[...].astype(o_ref.dtype)

    ne = len(scattered)
    hosted = _hosting(body, 2, 1, 1, ["scatter"] * ne,
                      lambda: (pl.program_id(0) == 0) & (pl.program_id(1) == 0),
                      lambda: (pl.program_id(0) == nj - 1) & (pl.program_id(1) == nk - 1))
    res = (_pc_comm if ne else _pc)(
        hosted, name=name, grid=(nj, nk),
        in_specs=[pl.BlockSpec((tm, D_MODEL), lambda j, k: (k, 0)),
                  pl.BlockSpec((tm, tn), lambda j, k: (k, j))] + [ANY] * ne,
        out_specs=[pl.BlockSpec((tn, D_MODEL), lambda j, k: (j, 0))] + [ANY] * ne,
        out_shape=[jax.ShapeDtypeStruct((n, D_MODEL), _MM)] + [_exchange_out_shape("scatter", a) for a in scattered],
        scratch_shapes=[pltpu.VMEM((tn, D_MODEL), f32)] + _exchange_sems(ne),
        compiler_params=_cparams(("arbitrary", "arbitrary")),
    )(u, dg, *scattered)
    return res if ne else res[0]


def grad_w_many(u, dgs, name):
    t = u.shape[0]
    widths = [d.shape[1] for d in dgs]
    tot, ng = sum(widths), len(dgs)
    f32 = jnp.float32
    budget = 48 * 1024 * 1024
    tm = next((c for c in (2048, 1024, 512, 256)
               if t % c == 0 and tot * D_MODEL * 4 + 2 * (c * tot * 4 + c * D_MODEL * 2 + tot * D_MODEL * 2) <= budget), t)
    nk = t // tm

    def body(*refs):
        u_ref, d_refs, o_refs, accs = refs[0], refs[1:1 + ng], refs[1 + ng:1 + 2 * ng], refs[1 + 2 * ng:]
        k = pl.program_id(0)

        @pl.when(k == 0)
        def _():
            for acc in accs:
                acc[...] = jnp.zeros_like(acc)

        uu = u_ref[...]
        for d_ref, acc in zip(d_refs, accs):
            acc[...] += lax.dot_general(d_ref[...].astype(_MM), uu, (((0,), (0,)), ((), ())),
                                        preferred_element_type=f32)

        @pl.when(k == nk - 1)
        def _():
            for o_ref, acc in zip(o_refs, accs):
                o_ref[...] = acc[...].astype(o_ref.dtype)

    return _pc(
        body, name=name, grid=(nk,),
        in_specs=[pl.BlockSpec((tm, D_MODEL), lambda k: (k, 0))] + [pl.BlockSpec((tm, n), lambda k: (k, 0)) for n in widths],
        out_specs=[pl.BlockSpec((n, D_MODEL), lambda k: (0, 0)) for n in widths],
        out_shape=[jax.ShapeDtypeStruct((n, D_MODEL), _MM) for n in widths],
        scratch_shapes=[pltpu.VMEM((n, D_MODEL), f32) for n in widths],
        compiler_params=_cparams(("arbitrary",)),
    )(u, *dgs)


def _pad_lanes(v, off):
    n = v.shape[-1]
    return jnp.pad(v.reshape(1, n).astype(jnp.float32), ((0, 0), (off, 128 - off - n)))


REF_ROWS = dict(z=(0, 1024), xbc=(1024, 2560), dt=(2560, 2576), gate=(2576, 3600), qkv=(3600, 6672), ab=(6672, 6688))


def unperm_w_in(gz, gxbc, ggate, gqkv, gsm):
    src = dict(z=gz, xbc=gxbc, dt=gsm[0:16], gate=ggate, qkv=gqkv, ab=gsm[16:32])
    slabs = []
    for k in range(N_DEV):
        a, b = k * W_IN_SHARD, (k + 1) * W_IN_SHARD
        parts = []
        for name, (s, e) in REF_ROWS.items():
            lo, hi = max(a, s), min(b, e)
            if lo < hi:
                parts.append(src[name][lo - s:hi - s])
        slabs.append(jnp.concatenate(parts, axis=0))
    return jnp.stack(slabs)


def all_gather(arrs, name):
    n = len(arrs)
    halves = []
    for arr in arrs:
        h = arr.shape[0] // 2 // 16 * 16
        halves.append(((0, h), (h, arr.shape[0] - h)))

    def body(*refs):
        ins, outs = refs[:n], refs[n:2 * n]
        send_sems, recv_sems, local_sems = refs[2 * n:]
        x, y, c, me = _me()
        sibling = (x, y, 1 - c)
        xn, yn, dg = (1 - x, y), (x, 1 - y), (1 - x, 1 - y)

        def idx(px, py, pc):
            return 4 * px + 2 * py + pc

        def copy(a, k, block, to, src=None, half=None):
            slot = outs[a].at[idx(*block)]
            if half is not None:
                slot = slot.at[pl.ds(*halves[a][half])]
                src = src if src is None else src.at[pl.ds(*halves[a][half])]
            return pltpu.make_async_remote_copy(src_ref=slot if src is None else src, dst_ref=slot,
                                                send_sem=send_sems.at[a, k], recv_sem=recv_sems.at[a, k],
                                                device_id=to, device_id_type=MESH)

        local = [pltpu.make_async_copy(ins[a], outs[a].at[me], local_sems.at[a]) for a in range(n)]
        for cp in local:
            cp.start()
        started = []

        def start(cps):
            for cp in cps:
                cp.start()
            started.extend(cps)

        for a in range(n):
            start([copy(a, 1, (x, y, c), (*xn, c), src=ins[a], half=0),
                   copy(a, 2, (x, y, c), (*yn, c), src=ins[a], half=1),
                   copy(a, 8, (x, y, c), (*xn, c), src=ins[a], half=1),
                   copy(a, 9, (x, y, c), (*yn, c), src=ins[a], half=0),
                   copy(a, 0, (x, y, c), sibling, src=ins[a])])
        for a in range(n):
            copy(a, 1, (*xn, c), (x, y, c), half=0).wait_recv()
            start([copy(a, 3, (*xn, c), (*yn, c), half=0)])
            copy(a, 2, (*yn, c), (x, y, c), half=1).wait_recv()
            start([copy(a, 4, (*yn, c), (*xn, c), half=1)])
        for a in range(n):
            copy(a, 8, (*xn, c), (x, y, c), half=1).wait_recv()
            start([copy(a, 5, (*xn, c), sibling)])
            copy(a, 9, (*yn, c), (x, y, c), half=0).wait_recv()
            start([copy(a, 6, (*yn, c), sibling)])
        for a in range(n):
            copy(a, 3, (*dg, c), (x, y, c), half=0).wait_recv()
            copy(a, 4, (*dg, c), (x, y, c), half=1).wait_recv()
            start([copy(a, 7, (*dg, c), sibling)])
        for a in range(n):
            copy(a, 0, sibling, (x, y, c)).wait_recv()
            for j, chip in enumerate((xn, yn, dg)):
                copy(a, 5 + j, (*chip, 1 - c), (x, y, c)).wait_recv()
        for cp in started:
            cp.wait_send()
        for cp in local:
            cp.wait()

    return _pc_comm(
        body, name=name, in_specs=[ANY] * n, out_specs=[ANY] * n,
        out_shape=[jax.ShapeDtypeStruct((N_DEV,) + a.shape, a.dtype) for a in arrs],
        scratch_shapes=[pltpu.SemaphoreType.DMA((n, 10)), pltpu.SemaphoreType.DMA((n, 10)),
                        pltpu.SemaphoreType.DMA((n,))],
    )(*arrs)


def adamw_sum(recv, w, m, v, rows, name, cols=None, small=()):
    r, ccols = w.shape
    f32 = jnp.float32
    c1 = 1.0 / (1.0 - ADAM_B1 ** ADAM_STEP)
    c2 = 1.0 / (1.0 - ADAM_B2 ** ADAM_STEP)
    ns = len(small)
    steps = r // rows if cols is None else ccols // cols

    def body(*refs):
        recv_ref, w_ref, m_ref, v_ref = refs[:4]
        g_ref, d_ref, mo_ref, vo_ref = refs[4 + ns:8 + ns]
        if ns:
            small_out, slabs = refs[8 + ns], refs[9 + ns]
            start, wait = _exchange_ops("scatter", slabs, small_out, *refs[10 + ns:])

            @pl.when(pl.program_id(0) == 0)
            def _():
                _fill_small_slabs(slabs, refs[4:4 + ns])
                start()

        g = recv_ref[0].astype(f32)
        for k in range(1, N_DEV):
            g = g + recv_ref[k].astype(f32)
        mn = ADAM_B1 * m_ref[...] + (1.0 - ADAM_B1) * g
        vn = ADAM_B2 * v_ref[...] + (1.0 - ADAM_B2) * (g * g)
        g_ref[...] = g
        mo_ref[...] = mn
        vo_ref[...] = vn
        d_ref[...] = -ADAM_LR * ((mn * c1) / (jnp.sqrt(vn * c2) + ADAM_EPS) + ADAM_WD * w_ref[...])

        if ns:
            pl.when(pl.program_id(0) == steps - 1)(wait)

    if cols is None:
        blk = pl.BlockSpec((rows, ccols), lambda i: (i, 0))
        rblk = pl.BlockSpec((N_DEV, rows, ccols), lambda i: (0, i, 0))
    else:
        blk = pl.BlockSpec((r, cols), lambda i: (0, i))
        rblk = pl.BlockSpec((N_DEV, r, cols), lambda i: (0, 0, i))
    total = _small_layout()[1]
    return (_pc_comm if ns else _pc)(
        body, name=name, grid=(steps,),
        in_specs=[rblk, blk, blk, blk] + [_full(a.shape) for a in small],
        out_specs=[blk] * 4 + [ANY] * bool(ns),
        out_shape=[jax.ShapeDtypeStruct((r, ccols), f32)] * 4 +
                  [jax.ShapeDtypeStruct((N_DEV, 1, total), f32)] * bool(ns),
        scratch_shapes=([pltpu.VMEM((N_DEV, 1, total), f32)] + _exchange_sems(1)) * bool(ns),
        compiler_params=_cparams(("arbitrary",)),
    )(recv, w, m, v, *small)


SMALL = (("norm_w", 1, 1024, 0), ("ssd_conv_b", 1, 1536, 0), ("ssd_dt_bias", 1, 16, 0), ("ssd_a_log", 1, 16, 0),
         ("ssd_d", 1, 16, 0), ("ssd_norm_w", 1, 1024, 0), ("gdn_dt_bias", 1, 8, 16), ("gdn_a_log", 1, 8, 16),
         ("gdn_norm_w", 1, 128, 0), ("final_norm_w", 1, 1024, 0),
         ("ssd_conv_w", CONV_K, SSD_CONV_DIM // N_DEV, 0), ("gdn_conv_w", CONV_K, GDN_CONV_DIM // N_DEV, 0))


def _small_layout():
    out, off = [], 0
    for name, rows, n, lane0 in SMALL + (("loss", 1, 128, 0),):
        stride = -(-(lane0 + n) // 128) * 128
        out.append((name, rows, n, lane0, stride, off))
        off += rows * stride
    return out, off


def _fill_small_slabs(slabs, acc_refs):
    layout, _ = _small_layout()
    slabs[...] = jnp.zeros_like(slabs)
    for (name, rows, n, lane0, stride, off), acc in zip(layout, acc_refs):
        for k in range(N_DEV):
            if rows == 1:
                slabs[k, :, off:off + stride] = acc[0:1, 0:stride]
            else:
                for j in range(rows):
                    slabs[k, :, off + stride * j:off + stride * j + n] = acc[j:j + 1, n * k:n * k + n]


def adamw_small(recv, w, m, v):
    layout, total = _small_layout()
    loss_off = layout[-1][5]
    layout = layout[:-1]
    f32 = jnp.float32
    c1 = 1.0 / (1.0 - ADAM_B1 ** ADAM_STEP)
    c2 = 1.0 / (1.0 - ADAM_B2 ** ADAM_STEP)
    np_ = len(layout)

    def body(*refs):
        recv_ref = refs[0]
        w_refs, m_refs, v_refs = refs[1:1 + np_], refs[1 + np_:1 + 2 * np_], refs[1 + 2 * np_:1 + 3 * np_]
        o_refs = refs[1 + 3 * np_:]
        g_all = recv_ref[0]
        for k in range(1, N_DEV):
            g_all = g_all + recv_ref[k]
        o_refs[4 * np_][...] = g_all[:, loss_off:loss_off + 128]

        def update(g, wv, mv, vv):
            mn = ADAM_B1 * mv + (1.0 - ADAM_B1) * g
            vn = ADAM_B2 * vv + (1.0 - ADAM_B2) * (g * g)
            return g, -ADAM_LR * ((mn * c1) / (jnp.sqrt(vn * c2) + ADAM_EPS) + ADAM_WD * wv), mn, vn

        for p, (name, rows, n, lane0, stride, off) in enumerate(layout):
            outs = o_refs[4 * p:4 * p + 4]
            if rows == 1:
                res = update(g_all[:, off + lane0:off + lane0 + n], w_refs[p][...], m_refs[p][...], v_refs[p][...])
                for o, r in zip(outs, res):
                    o[...] = r
            else:
                for j in range(rows):
                    res = update(g_all[:, off + stride * j:off + stride * j + n], w_refs[p][0, j:j + 1, :],
                                 m_refs[p][0, j:j + 1, :], v_refs[p][0, j:j + 1, :])
                    for o, r in zip(outs, res):
                        o[0, j:j + 1, :] = r

    names = [e[0] for e in layout]
    ins = [recv] + [d[nm] for d in (w, m, v) for nm in names]
    out_shape = [jax.ShapeDtypeStruct(w[nm].shape, f32) for nm in names for _ in range(4)]
    out_shape.append(jax.ShapeDtypeStruct((1, 128), f32))
    res = _pc(body, name="adamw_small", out_shape=out_shape)(*ins)
    return {nm: tuple(res[4 * p:4 * p + 4]) for p, nm in enumerate(names)}, res[4 * np_]


SHARD = (("ssd_conv_w", CONV_K * SSD_CONV_DIM // N_DEV), ("gdn_conv_w", CONV_K * GDN_CONV_DIM // N_DEV))
SHARD_ROWS = 24


def _rows_of(size):
    return -(-size // 128)


def _pack(vals, layout, total_rows):
    parts = []
    for (name, size), val in zip(layout, vals):
        flat = val.reshape(-1).astype(jnp.float32)
        parts.append(jnp.pad(flat, (0, _rows_of(size) * 128 - size)).reshape(-1, 128))
    used = sum(_rows_of(s) for _, s in layout)
    parts.append(jnp.zeros((total_rows - used, 128), jnp.float32))
    return jnp.concatenate(parts, axis=0)


def _conv_full(gathered_flat, ccols):
    return gathered_flat.reshape(N_DEV, CONV_K, ccols // N_DEV).transpose(1, 0, 2).reshape(CONV_K, ccols)


def kernel(x, norm_w, w_in, ssd_conv_w, ssd_conv_b, ssd_dt_bias, ssd_a_log, ssd_d, ssd_norm_w, gdn_conv_w, gdn_dt_bias, gdn_a_log, gdn_norm_w, w_out, final_norm_w, loss_target, m_norm_w, m_w_in, m_ssd_conv_w, m_ssd_conv_b, m_ssd_dt_bias, m_ssd_a_log, m_ssd_d, m_ssd_norm_w, m_gdn_conv_w, m_gdn_dt_bias, m_gdn_a_log, m_gdn_norm_w, m_w_out, m_final_norm_w, v_norm_w, v_w_in, v_ssd_conv_w, v_ssd_conv_b, v_ssd_dt_bias, v_ssd_a_log, v_ssd_d, v_ssd_norm_w, v_gdn_conv_w, v_gdn_dt_bias, v_gdn_a_log, v_gdn_norm_w, v_w_out, v_final_norm_w):
    f32 = jnp.float32
    w = dict(norm_w=norm_w, w_in=w_in, ssd_conv_w=ssd_conv_w, ssd_conv_b=ssd_conv_b, ssd_dt_bias=ssd_dt_bias,
             ssd_a_log=ssd_a_log, ssd_d=ssd_d, ssd_norm_w=ssd_norm_w, gdn_conv_w=gdn_conv_w, gdn_dt_bias=gdn_dt_bias,
             gdn_a_log=gdn_a_log, gdn_norm_w=gdn_norm_w, w_out=w_out, final_norm_w=final_norm_w)
    m = dict(norm_w=m_norm_w, w_in=m_w_in, ssd_conv_w=m_ssd_conv_w, ssd_conv_b=m_ssd_conv_b, ssd_dt_bias=m_ssd_dt_bias,
             ssd_a_log=m_ssd_a_log, ssd_d=m_ssd_d, ssd_norm_w=m_ssd_norm_w, gdn_conv_w=m_gdn_conv_w,
             gdn_dt_bias=m_gdn_dt_bias, gdn_a_log=m_gdn_a_log, gdn_norm_w=m_gdn_norm_w, w_out=m_w_out,
             final_norm_w=m_final_norm_w)
    v = dict(norm_w=v_norm_w, w_in=v_w_in, ssd_conv_w=v_ssd_conv_w, ssd_conv_b=v_ssd_conv_b, ssd_dt_bias=v_ssd_dt_bias,
             ssd_a_log=v_ssd_a_log, ssd_d=v_ssd_d, ssd_norm_w=v_ssd_norm_w, gdn_conv_w=v_gdn_conv_w,
             gdn_dt_bias=v_gdn_dt_bias, gdn_a_log=v_gdn_a_log, gdn_norm_w=v_gdn_norm_w, w_out=v_w_out,
             final_norm_w=v_final_norm_w)
    names = list(w)
    shapes = {n: w[n].shape for n in names}

    xl, tgt = x[0], loss_target[0]
    cs = _consts()
    dtb_s = _pad_lanes(ssd_dt_bias, 0)
    alog_s = _pad_lanes(ssd_a_log, 0)
    dpar = _pad_lanes(ssd_d, 0)
    dtb_g = _pad_lanes(gdn_dt_bias, 16)
    alog_g = _pad_lanes(gdn_a_log, 16)
    nw_g = gdn_norm_w.reshape(1, 128)
    nw_s = ssd_norm_w.reshape(1, 1024)
    cb_s = ssd_conv_b.reshape(1, 1536)
    nw1 = norm_w.reshape(1, D_MODEL)

    w_slot = lax.dynamic_slice(jnp.pad(w_in[0].T.astype(_MM), ((12, 12), (0, 0))),
                               (12 - (W_IN_SHARD * _me()[3]) % 16, 0), (W_SLOT, D_MODEL))
    (g_w_in,) = all_gather([w_slot], "gather_w_in")
    w_perm = g_w_in.reshape(W_GATHERED, D_MODEL)
    conv_pack = _pack([w["ssd_conv_w"], w["gdn_conv_w"]], SHARD, SHARD_ROWS)
    u, z, xbc, gate, qkv, sm, g_w_out, g_conv = inproj_fwd(xl, nw1, w_perm, [w_out[0].astype(_MM), conv_pack])
    w_out_full = g_w_out.reshape(MIX_WIDTH, D_MODEL)
    ssd_cw = _conv_full(g_conv[:, 0:6].reshape(N_DEV, -1), SSD_CONV_DIM)
    gdn_cw = _conv_full(g_conv[:, 6:18].reshape(N_DEV, -1), GDN_CONV_DIM)

    nc = xl.shape[0] // CHUNK
    y_ssd, hs, pre_s, y_gdn, ss, ts, pre_g = _chunk_call(
        [ssd_fwd(z, xbc, sm, ssd_cw, cb_s, dtb_s, alog_s, dpar, nw_s, cs),
         gdn_fwd(gate, qkv, sm, gdn_cw, dtb_g, alog_g, nw_g, cs)], "scan_fwd", nc, False)
    dout, dys, dyg, g_wout, g_fnw, loss_l = out_fwd_bwd(xl, tgt, y_ssd, y_gdn, w_out_full,
                                                        final_norm_w.reshape(1, D_MODEL))
    (dz, dxbc, g_cw_s, g_cb_s, g_dtb_s, g_alog_s, g_d, g_nw_s,
     dgate, dqkv, dsm, g_cw_g, g_dtb_g, g_alog_g, g_nw_g) = _chunk_call(
        [ssd_bwd(z, xbc, pre_s, sm, hs, dys, ssd_cw, dtb_s, alog_s, dpar, nw_s, cs),
         gdn_bwd(gate, qkv, pre_g, sm, ss, ts, dyg, gdn_cw, dtb_g, alog_g, nw_g, cs)], "scan_bwd", nc, True)

    t_w_out = g_wout.reshape(N_DEV, MIX_WIDTH // N_DEV, D_MODEL)
    gws = dict(zip(("z", "sm"), grad_w_many(u, [dz, dsm], "grad_w_in_z_sm")))
    gws["xbc"] = grad_w_group(u, dxbc, "grad_w_in_xbc")
    gws["gate"] = grad_w_group(u, dgate, "grad_w_in_gate")
    gws["qkv"], r_w_out = grad_w_group(u, dqkv, "grad_w_in_qkv", [t_w_out])
    t_w_in = unperm_w_in(gws["z"], gws["xbc"], gws["gate"], gws["qkv"], gws["sm"])
    dx, g_nw, r_w_in = inproj_bwd_dx(xl, dout, nw1, w_perm, (dz, dxbc, dgate, dqkv, dsm), [t_w_in])

    accs = dict(norm_w=g_nw, ssd_conv_b=g_cb_s, ssd_dt_bias=g_dtb_s, ssd_a_log=g_alog_s, ssd_d=g_d,
                ssd_norm_w=g_nw_s, gdn_dt_bias=g_dtb_g, gdn_a_log=g_alog_g, gdn_norm_w=g_nw_g, final_norm_w=g_fnw,
                ssd_conv_w=g_cw_s, gdn_conv_w=g_cw_g)
    *o_w_in, r_small = adamw_sum(r_w_in, w_in[0].T, m_w_in[0].T, v_w_in[0].T, None, "adamw_w_in", cols=256,
                                 small=[accs[e[0]] for e in SMALL] + [loss_l])
    o_w_out = adamw_sum(r_w_out, w_out[0], m_w_out[0], v_w_out[0], 64, "adamw_w_out")
    row = lambda d: {n: (a.reshape(1, -1) if a.ndim == 1 else a) for n, a in d.items()}
    o_small, loss_sum = adamw_small(r_small, row(w), row(m), row(v))

    loss = loss_sum[0, 0]
    outs = [loss, dx[None]]
    for k in range(4):
        parts = {n: o_small[n][k] for n in o_small}
        parts["w_in"] = o_w_in[k].T
        parts["w_out"] = o_w_out[k]
        outs += [parts[n].reshape(shapes[n]) for n in names]
    return tuple(outs)
```

```python
import functools

import jax
import jax.numpy as jnp
import numpy as np
from jax import lax
from jax.experimental import pallas as pl
from jax.experimental.pallas import tpu as pltpu

_MM = jnp.bfloat16

D_MODEL = 1024
CHUNK = 64
CONV_K = 4
EPS = 1e-6
SSD_CONV_DIM = 1536
GDN_HEADS = 8
GDN_DK = 128
GDN_CONV_DIM = 3072
MIX_WIDTH = 2048
IN_DIM = 6688
N_DEV = 8
W_IN_SHARD = IN_DIM // N_DEV
HI = lax.Precision.HIGHEST
HIGH = lax.Precision.HIGH
VMEM_LIMIT = 56 * 1024 * 1024

ADAM_LR = 0.001
ADAM_B1 = 0.9
ADAM_B2 = 0.999
ADAM_EPS = 1e-08
ADAM_WD = 0.01
ADAM_STEP = 10


def _pc(body, **kw):
    return pl.pallas_call(body, **kw)


def _pc_comm(body, **kw):
    return pl.pallas_call(body, **kw)


def _cparams(sem):
    return pltpu.CompilerParams(dimension_semantics=sem, vmem_limit_bytes=VMEM_LIMIT)


def _sig(x):
    return 0.5 * jnp.tanh(0.5 * x) + 0.5


@jax.custom_vjp
def _sigmoid(x):
    return _sig(x)


def _sigmoid_fwd(x):
    s = _sig(x)
    return s, s


def _sigmoid_bwd(s, g):
    return (g * s * (1.0 - s),)


_sigmoid.defvjp(_sigmoid_fwd, _sigmoid_bwd)


@jax.custom_vjp
def _silu(x):
    return x * _sig(x)


def _silu_fwd(x):
    s = _sig(x)
    return x * s, (x, s)


def _silu_bwd(res, g):
    x, s = res
    return (g * (s * (1.0 + x * (1.0 - s))),)


_silu.defvjp(_silu_fwd, _silu_bwd)


def _softplus_impl(x):
    return jnp.maximum(x, 0.0) + jnp.log(1.0 + jnp.exp(-jnp.abs(x)))


@jax.custom_vjp
def _softplus(x):
    return _softplus_impl(x)


def _softplus_fwd(x):
    return _softplus_impl(x), x


def _softplus_bwd(x, g):
    return (g * _sig(x),)


_softplus.defvjp(_softplus_fwd, _softplus_bwd)


def _lane_bcast_impl(x, k):
    return jnp.broadcast_to(x[..., k:k + 1], x.shape)


@functools.partial(jax.custom_vjp, nondiff_argnums=(1,))
def _lane_bcast(x, k):
    return _lane_bcast_impl(x, k)


def _lane_bcast_fwd(x, k):
    return _lane_bcast_impl(x, k), None


def _lane_bcast_bwd(k, _, g):
    lane = lax.broadcasted_iota(jnp.int32, g.shape, g.ndim - 1)
    return (jnp.where(lane == k, jnp.sum(g, axis=-1, keepdims=True), 0.0),)


_lane_bcast.defvjp(_lane_bcast_fwd, _lane_bcast_bwd)


def _mm(a, b):
    return jnp.dot(a.astype(_MM), b.astype(_MM), preferred_element_type=jnp.float32)


def _mm_nt(a, b):
    return lax.dot_general(a.astype(_MM), b.astype(_MM), (((1,), (1,)), ((), ())),
                           preferred_element_type=jnp.float32)


def _mm_tn(a, b):
    return lax.dot_general(a.astype(_MM), b.astype(_MM), (((0,), (0,)), ((), ())),
                           preferred_element_type=jnp.float32)


def _dot_hi(a, b):
    return jnp.dot(a, b, precision=HI, preferred_element_type=jnp.float32)


def _bmm(a, b):
    return lax.dot_general(a.astype(_MM), b.astype(_MM), (((2,), (1,)), ((0,), (0,))),
                           preferred_element_type=jnp.float32)


def _bmm_nt(a, b):
    return lax.dot_general(a.astype(_MM), b.astype(_MM), (((2,), (2,)), ((0,), (0,))),
                           preferred_element_type=jnp.float32)


def _bmm_tn(a, b):
    return lax.dot_general(a.astype(_MM), b.astype(_MM), (((1,), (1,)), ((0,), (0,))),
                           preferred_element_type=jnp.float32)


def _bmm_hi(a, b):
    return lax.dot_general(a, b, (((2,), (1,)), ((0,), (0,))), precision=HIGH, preferred_element_type=jnp.float32)


def _bmm_nt_hi(a, b):
    return lax.dot_general(a, b, (((2,), (2,)), ((0,), (0,))), precision=HIGH, preferred_element_type=jnp.float32)


def _bmm_tn_hi(a, b):
    return lax.dot_general(a, b, (((1,), (1,)), ((0,), (0,))), precision=HIGH, preferred_element_type=jnp.float32)


def _consts():
    l = np.arange(CHUNK)
    tri = (l[:, None] >= l[None, :]).astype(np.float32)
    lane = np.arange(128)
    i2 =(l[:, None] == (lane[None, :] % 64)).astype(np.float32)
    mask2 = (l[:, None] >= (lane[None, :] % 64)).astype(np.float32)
    lo = (lane < 64).astype(np.float32)[None, :]
    i64 = np.eye(CHUNK, dtype=np.float32)
    strict = (l[:, None] > l[None, :]).astype(np.float32)
    return dict(tri=jnp.asarray(tri), i2=jnp.asarray(i2), mask2=jnp.asarray(mask2), lo=jnp.asarray(lo),
                i64=jnp.asarray(i64), strict=jnp.asarray(strict))


def _ssd_chunk(xs_pre, b_pre, c_pre, z, sm, ht, dtb, alog, dpar, nw, tri, i2, mask2, lo):
    lane = lax.broadcasted_iota(jnp.int32, (1, 128), 1)
    m16 = lane < 16
    dt = jnp.where(m16, _softplus(sm + dtb), 0.0)
    a_neg = -jnp.exp(alog)
    cum = _dot_hi(tri, dt * a_neg)
    row = lax.broadcasted_iota(jnp.int32, (CHUNK, 1), 0)
    is_last = row == CHUNK - 1
    hi = 1.0 - lo
    bm = [_silu(b) for b in b_pre]
    cm = [_silu(c) for c in c_pre]
    cb2 = [_mm_nt(cm[g], jnp.concatenate([bm[g], bm[g]], axis=0)) for g in range(2)]
    ht_g = [jnp.concatenate(ht[4 * g:4 * g + 4], axis=1) for g in range(2)]
    yoff_g = [_mm(cm[g], ht_g[g]) for g in range(2)]
    yg, xdec, clast = [], [], []
    for j in range(8):
        g, k4 = j // 4, j % 4
        pair = lambda v, j=j: jnp.where(lo > 0.5, _lane_bcast(v, 2 * j), _lane_bcast(v, 2 * j + 1))
        xs = _silu(xs_pre[j])
        dte = pair(dt)
        cume = pair(cum)
        cum_last = jnp.sum(jnp.where(is_last, cume, 0.0), axis=0, keepdims=True)
        xdt = xs * dte
        rowv = jnp.sum(cume * i2, axis=0, keepdims=True)
        lm = jnp.exp(jnp.where(mask2 > 0.5, cume - rowv, -jnp.inf))
        m = cb2[g] * lm
        xblk = jnp.concatenate([xdt * lo, xdt * hi], axis=0)
        y = _mm(m, xblk)
        y = y + yoff_g[g][:, 128 * k4:128 * k4 + 128] * jnp.exp(cume)
        y = y + pair(dpar) * xs
        yg.append(y * _silu(z[j]))
        xdec.append(xdt * jnp.exp(cum_last - cume))
        clast.append(cum_last)
    ht_next = []
    for g in range(2):
        st = _mm_tn(bm[g], jnp.concatenate(xdec[4 * g:4 * g + 4], axis=1))
        for k4 in range(4):
            j = 4 * g + k4
            ht_next.append(ht[j] * jnp.exp(clast[j]) + st[:, 128 * k4:128 * k4 + 128])
    outs = []
    for g in range(2):
        ss = sum(jnp.sum(yg[j] * yg[j], axis=-1, keepdims=True) for j in range(4 * g, 4 * g + 4))
        rs = lax.rsqrt(ss * (1.0 / 512.0) + EPS)
        for j in range(4 * g, 4 * g + 4):
            outs.append(yg[j] * rs * nw[j])
    return outs, ht_next


def _tri_inverse(a):
    eye = jnp.eye(CHUNK, dtype=jnp.float32)[None]
    p = eye - a
    x = _bmm_hi(a, a)
    for i in range(4):
        both = (_bmm_hi if i == 0 else _bmm)(jnp.concatenate([p, x], axis=1), x)
        p = p + both[:, :CHUNK]
        x = both[:, CHUNK:]
    return p + _bmm(p, x)


def _solve_apply(t, r1, r2):
    both = _bmm_hi(t, jnp.concatenate([r1, r2], axis=-1))
    n = r1.shape[-1]
    return both[..., :n], both[..., n:]


@jax.custom_vjp
def _solve(a, r1, r2, t):
    return _solve_apply(t, r1, r2)


def _solve_fwd(a, r1, r2, t):
    u, w = _bmm_hi(t, r1), _bmm_hi(t, r2)
    return (u, w), (t, u, w)


def _solve_bwd(res, cts):
    t, u, w = res
    du, dw = cts
    dr1 = _bmm_tn_hi(t, du)
    dr2 = _bmm_tn_hi(t, dw)
    da = -(_bmm_nt_hi(dr1, u) + _bmm_nt_hi(dr2, w))
    return da, dr1, dr2, jnp.zeros_like(t)


_solve.defvjp(_solve_fwd, _solve_bwd)


def _gdn_chunk(q_pre, k_pre, v_pre, gate, sm, s, dtb, alog, nw, tri, i64, strict, t_in=None):
    lane = lax.broadcasted_iota(jnp.int32, (1, 128), 1)
    m_a = (lane >= 16) & (lane < 24)
    g_full = jnp.where(m_a, -jnp.exp(alog) * _softplus(sm + dtb), 0.0)
    gc = _dot_hi(tri, g_full)
    sig = _sigmoid(sm)
    heads = lambda f: jnp.concatenate([f(h)[None] for h in range(GDN_HEADS)], axis=0)
    gc3 = heads(lambda h: _lane_bcast(gc, 16 + h))
    beta3 = heads(lambda h: _lane_bcast(sig, 24 + h))
    q = _silu(q_pre)
    q = q * lax.rsqrt(jnp.sum(q * q, axis=-1, keepdims=True) + EPS) * (GDN_DK ** -0.5)
    k = _silu(k_pre)
    k = k * lax.rsqrt(jnp.sum(k * k, axis=-1, keepdims=True) + EPS)
    v = _silu(v_pre)
    gcl = gc3[:, :, :CHUNK]
    gc_row = jnp.sum(gcl * i64[None], axis=1, keepdims=True)
    incl = (strict + i64)[None] > 0.5
    decay = jnp.exp(jnp.where(incl, gcl - gc_row, -jnp.inf))
    kb = k * beta3
    a = jnp.where(strict[None] > 0.5, _bmm_nt(kb, k) * decay, 0.0)
    egc = jnp.exp(gc3)
    t = _tri_inverse(a) if t_in is None else t_in
    u, w = _solve(a, v * beta3, kb * egc, t)
    attn = _bmm_nt(q, k) * decay
    row = lax.broadcasted_iota(jnp.int32, (1, CHUNK, 1), 1)
    gl = jnp.sum(jnp.where(row == CHUNK - 1, gc3, 0.0), axis=1, keepdims=True)
    q_dec = q * egc
    k_dec = k * jnp.exp(gl - gc3)
    ws = _bmm(jnp.concatenate([w, q_dec], axis=1), s)
    v_new = u - ws[:, :CHUNK]
    o = ws[:, CHUNK:] + _bmm(attn, v_new)
    s_next = s * jnp.exp(gl) + _bmm_tn(k_dec, v_new)
    on = o * lax.rsqrt(jnp.mean(o * o, axis=-1, keepdims=True) + EPS) * nw
    return on * _silu(gate), s_next, t


def _conv_fwd(pbuf, w_ref, c0, c1):
    blk = pbuf[:, c0:c1]
    acc = w_ref[CONV_K - 1:CONV_K, c0:c1] * blk[8:72]
    for j in range(CONV_K - 1):
        acc = acc + w_ref[j:j + 1, c0:c1] * pltpu.roll(blk, CONV_K - 1 - j, axis=0)[8:72]
    return acc


MESH = pl.DeviceIdType.MESH
ANY = pl.BlockSpec(memory_space=pl.ANY)


def _me():
    x, y, c = lax.axis_index("x"), lax.axis_index("y"), lax.axis_index("c")
    return x, y, c, 4 * x + 2 * y + c


def _peer(r):
    x, y, c, _ = _me()
    px = 1 - x if r & 4 else x
    py = 1 - y if r & 2 else y
    pc = 1 - c if r & 1 else c
    return (px, py, pc), 4 * px + 2 * py + pc


def _exchange_ops(kind, in_ref, out_ref, send_sems, recv_sems, local_sem):
    me = _me()[3]
    local = pltpu.make_async_copy(in_ref.at[me] if kind == "scatter" else in_ref, out_ref.at[me], local_sem)
    sends, recvs = [], []
    for r in range(1, N_DEV):
        peer, pidx = _peer(r)
        src = in_ref.at[pidx] if kind == "scatter" else in_ref
        sems = dict(send_sem=send_sems.at[r - 1], recv_sem=recv_sems.at[r - 1], device_id=peer, device_id_type=MESH)
        sends.append(pltpu.make_async_remote_copy(src_ref=src, dst_ref=out_ref.at[me], **sems))
        recvs.append(pltpu.make_async_remote_copy(src_ref=src, dst_ref=out_ref.at[pidx], **sems))

    def start():
        local.start()
        for cp in sends:
            cp.start()

    def wait():
        for cp in recvs:
            cp.wait_recv()
        for cp in sends:
            cp.wait_send()
        local.wait()

    return start, wait


def _exchange_sems(n):
    return [pltpu.SemaphoreType.DMA((N_DEV - 1,)), pltpu.SemaphoreType.DMA((N_DEV - 1,)),
            pltpu.SemaphoreType.DMA(())] * n


def _exchange_out_shape(kind, a):
    return jax.ShapeDtypeStruct(a.shape if kind == "scatter" else (N_DEV,) + a.shape, a.dtype)


def _hosting(body, n_in, n_out, n_scratch, kinds, first, last):
    ne = len(kinds)

    def wrapped(*refs):
        ins, ex_in = refs[:n_in], refs[n_in:n_in + ne]
        o0 = n_in + ne
        outs, ex_out = refs[o0:o0 + n_out], refs[o0 + n_out:o0 + n_out + ne]
        s0 = o0 + n_out + ne
        scr, sems = refs[s0:s0 + n_scratch], refs[s0 + n_scratch:]
        ops = [_exchange_ops(kinds[e], ex_in[e], ex_out[e], *sems[3 * e:3 * e + 3]) for e in range(ne)]

        @pl.when(first())
        def _():
            for start, _ in ops:
                start()

        body(*ins, *outs, *scr)

        @pl.when(last())
        def _():
            for _, wait in ops:
                wait()

    return wrapped


GROUPS = (("z", 0, 1024), ("xbc", 1024, 2560), ("gate", 2560, 3584), ("qkv", 3584, 6656), ("sm", 6656, 6784))
GROUP_ROWS = dict(z=((0, 1024),), xbc=((1024, 2560),), gate=((2576, 3600),), qkv=((3600, 6672),),
                  sm=((2560, 2576), (6672, 6688)))


W_SLOT = 848
W_GATHERED = N_DEV * W_SLOT


def _slot_pos(s, c):
    return W_SLOT * s + (c - W_IN_SHARD * s) + (W_IN_SHARD * s) % 16


def _w_pieces(w_ref, a, b):
    out, cur = [], a
    while cur < b:
        s = cur // W_IN_SHARD
        end = W_IN_SHARD * (s + 1)
        if end >= b:
            out.append(w_ref[_slot_pos(s, cur):_slot_pos(s, b), :])
            break
        hi = end // 16 * 16
        if hi > cur:
            out.append(w_ref[_slot_pos(s, cur):_slot_pos(s, hi), :])
        if end % 16:
            p, q = _slot_pos(s, hi), W_SLOT * (s + 1)
            out.append(w_ref[p:p + 16, :] + w_ref[q:q + 16, :])
            cur = hi + 16
        else:
            cur = hi
    return out


def _w_rows(w_ref, name, width):
    pieces = [p for a, b in GROUP_ROWS[name] for p in _w_pieces(w_ref, a, b)]
    n = sum(b - a for a, b in GROUP_ROWS[name])
    if n < width:
        pieces.append(jnp.zeros((width - n, D_MODEL), w_ref.dtype))
    return pieces[0] if len(pieces) == 1 else jnp.concatenate(pieces, axis=0)


def inproj_fwd(x, norm_w, w_perm, gathered):
    t = x.shape[0]
    tm = min(512, t)
    steps = t // tm
    kinds = ["gather"] * len(gathered)

    def body(x_ref, nw_ref, w_ref, u_ref, z_ref, xbc_ref, gate_ref, qkv_ref, sm_ref):
        xf = x_ref[...]
        rstd = lax.rsqrt(jnp.mean(xf * xf, axis=-1, keepdims=True) + EPS)
        u = (xf * rstd * nw_ref[...]).astype(_MM)
        u_ref[...] = u
        for (name, c0, c1), o_ref in zip(GROUPS, (z_ref, xbc_ref, gate_ref, qkv_ref, sm_ref)):
            o_ref[...] = lax.dot_general(u, _w_rows(w_ref, name, c1 - c0), (((1,), (1,)), ((), ())),
                                         preferred_element_type=jnp.float32)

    outs = [jax.ShapeDtypeStruct((t, D_MODEL), _MM)] + [jax.ShapeDtypeStruct((t, c1 - c0), jnp.float32)
                                                        for _, c0, c1 in GROUPS]
    hosted = _hosting(body, 3, 6, 0, kinds, lambda: pl.program_id(0) == 0, lambda: pl.program_id(0) == steps - 1)
    return _pc_comm(
        hosted, name="inproj_fwd", grid=(steps,),
        in_specs=[pl.BlockSpec((tm, D_MODEL), lambda i: (i, 0)),
                  pl.BlockSpec((1, D_MODEL), lambda i: (0, 0)),
                  pl.BlockSpec((W_GATHERED, D_MODEL), lambda i: (0, 0), pipeline_mode=pl.Buffered(1))] +
                 [ANY] * len(gathered),
        out_specs=[pl.BlockSpec((tm, D_MODEL), lambda i: (i, 0))] +
                  [pl.BlockSpec((tm, c1 - c0), lambda i: (i, 0)) for _, c0, c1 in GROUPS] + [ANY] * len(gathered),
        out_shape=outs + [_exchange_out_shape("gather", a) for a in gathered],
        scratch_shapes=_exchange_sems(len(gathered)), compiler_params=_cparams(("arbitrary",)),
    )(x, norm_w, w_perm, *gathered)


SUB_FWD = 4
SUB_BWD = 2


def _halo_spec(width, idx_fn):
    return pl.BlockSpec((8, width), lambda i: (jnp.maximum(idx_fn(i) * (SUB_FWD * CHUNK // 8) - 1, 0), 0))


def _when_first(shared, fn):
    if shared["first"] is not False:
        pl.when(shared["first"])(fn)


def _full(shape):
    nd = len(shape)
    return pl.BlockSpec(shape, lambda i: (0,) * nd)


def _ssd_split(pre_fn, z_ref, sm_ref):
    xs_pre = [pre_fn(128 * j, 128 * j + 128) for j in range(8)]
    b_pre = [pre_fn(1024 + 128 * g, 1152 + 128 * g) for g in range(2)]
    c_pre = [pre_fn(1280 + 128 * g, 1408 + 128 * g) for g in range(2)]
    z = [z_ref[:, 128 * j:128 * j + 128] for j in range(8)]
    return xs_pre, b_pre, c_pre, z, sm_ref[...]


def ssd_fwd(z, xbc, sm, conv_w, conv_b, dtb, alog, dpar, nw, cs):
    t = z.shape[0]
    nc = t // CHUNK

    def body(shared, z_ref, xbc_ref, halo_ref, sm_ref, cw_ref, cb_ref, dtb_ref, alog_ref, dpar_ref, nw_ref,
             tri_ref, i2_ref, mask2_ref, lo_ref, y_ref, hs_ref, pre_ref, pbuf, ht_scr):
        def init():
            ht_scr[...] = jnp.zeros_like(ht_scr)

        _when_first(shared, init)
        pbuf[0:8, :] = jnp.where(shared["first"], 0.0, halo_ref[...])
        pbuf[8:72, :] = xbc_ref[...]

        def pre_fn(c0, c1):
            pre = _conv_fwd(pbuf, cw_ref, c0, c1) + cb_ref[:, c0:c1]
            pre_ref[:, c0:c1] = pre
            return pre

        xs_pre, b_pre, c_pre, zz, smv = _ssd_split(pre_fn, z_ref, sm_ref)
        ht = [ht_scr[:, 128 * j:128 * j + 128] for j in range(8)]
        hs_ref[0] = ht_scr[...]
        nwl = [nw_ref[:, 128 * j:128 * j + 128] for j in range(8)]
        outs, ht_next = _ssd_chunk(xs_pre, b_pre, c_pre, zz, smv, ht, dtb_ref[...], alog_ref[...], dpar_ref[...],
                                   nwl, tri_ref[...], i2_ref[...], mask2_ref[...], lo_ref[...])
        for j in range(8):
            y_ref[:, 128 * j:128 * j + 128] = outs[j].astype(y_ref.dtype)
            ht_scr[:, 128 * j:128 * j + 128] = ht_next[j]

    blk = lambda w: pl.BlockSpec((SUB_FWD * CHUNK, w), lambda i: (i, 0))
    return dict(
        body=body,
        in_kinds=["rows", "rows", ("halo", 1), "rows"] + ["full"] * 10, out_kinds=["rows", "state", "rows"],
        in_specs=[blk(1024), blk(1536), _halo_spec(1536, lambda i: i), blk(128),
                  _full((CONV_K, 1536)), _full((1, 1536)), _full((1, 128)), _full((1, 128)), _full((1, 128)),
                  _full((1, 1024)), _full((64, 64)), _full((64, 128)), _full((64, 128)),
                  _full((1, 128))],
        out_specs=[blk(1024), pl.BlockSpec((SUB_FWD, 128, 1024), lambda i: (i, 0, 0)), blk(1536)],
        out_shape=[jax.ShapeDtypeStruct((t, 1024), _MM), jax.ShapeDtypeStruct((nc, 128, 1024), jnp.float32),
                   jax.ShapeDtypeStruct((t, 1536), jnp.float32)],
        scratch=[pltpu.VMEM((72, 1536), jnp.float32), pltpu.VMEM((128, 1024), jnp.float32)],
        args=[z, xbc, xbc, sm, conv_w, conv_b, dtb, alog, dpar, nw, cs["tri"], cs["i2"], cs["mask2"], cs["lo"]])


def _conv_bwd(dpre_list, col_ranges, dbuf, carry, x_ref, cw_ref, dx_ref, dcw_ref, dcb_ref, first):
    for dpre, (c0, c1) in zip(dpre_list, col_ranges):
        dbuf[0:64, c0:c1] = dpre
    dbuf[64:72, :] = jnp.where(first, 0.0, carry[...])
    carry[...] = dbuf[0:8, :]
    for (c0, c1) in col_ranges:
        xin = x_ref[:, c0:c1]
        blk = dbuf[:, c0:c1]
        acc = None
        for j in range(CONV_K):
            sh = blk[0:64] if j == CONV_K - 1 else pltpu.roll(blk, 72 - (CONV_K - 1 - j), axis=0)[0:64]
            term = cw_ref[j:j + 1, c0:c1] * sh
            acc = term if acc is None else acc + term
            dcw_ref[j:j + 1, c0:c1] += jnp.sum(xin * sh, axis=0, keepdims=True)
        dx_ref[:, c0:c1] = acc.astype(dx_ref.dtype)
        if dcb_ref is not None:
            dcb_ref[0:1, c0:c1] += jnp.sum(dbuf[0:64, c0:c1], axis=0, keepdims=True)


def ssd_bwd(z, xbc, pre, sm, hs, dy, conv_w, dtb, alog, dpar, nw, cs):
    t = z.shape[0]
    nc = t // CHUNK

    def body(shared, z_ref, xbc_ref, pre_ref, sm_ref, hs_ref, dy_ref, cw_ref, dtb_ref, alog_ref, dpar_ref, nw_ref,
             tri_ref, i2_ref, mask2_ref, lo_ref,
             dz_ref, dxbc_ref, dcw_ref, dcb_ref, ddtb_ref, dalog_ref, ddpar_ref, dnw_ref,
             dbuf, carry, dht_scr):
        def init():
            dht_scr[...] = jnp.zeros_like(dht_scr)
            dcw_ref[...] = jnp.zeros_like(dcw_ref)
            dcb_ref[...] = jnp.zeros_like(dcb_ref)
            ddtb_ref[...] = jnp.zeros_like(ddtb_ref)
            dalog_ref[...] = jnp.zeros_like(dalog_ref)
            ddpar_ref[...] = jnp.zeros_like(ddpar_ref)
            dnw_ref[...] = jnp.zeros_like(dnw_ref)

        _when_first(shared, init)
        pre_fn = lambda c0, c1: pre_ref[:, c0:c1]
        xs_pre, b_pre, c_pre, zz, smv = _ssd_split(pre_fn, z_ref, sm_ref)
        ht = [hs_ref[0, :, 128 * j:128 * j + 128] for j in range(8)]
        nwl = [nw_ref[:, 128 * j:128 * j + 128] for j in range(8)]
        consts = (tri_ref[...], i2_ref[...], mask2_ref[...], lo_ref[...])

        def f(xs_pre, b_pre, c_pre, zz, smv, ht, dtb, alog, dpar, nwl):
            return _ssd_chunk(xs_pre, b_pre, c_pre, zz, smv, ht, dtb, alog, dpar, nwl, *consts)

        _, vjp = jax.vjp(f, xs_pre, b_pre, c_pre, zz, smv, ht, dtb_ref[...], alog_ref[...], dpar_ref[...], nwl)
        dys = [dy_ref[:, 128 * j:128 * j + 128] for j in range(8)]
        dhts = [dht_scr[:, 128 * j:128 * j + 128] for j in range(8)]
        dxs, db, dc, dzz, dsm, dht, ddtb, dalog, ddpar, dnwl = vjp((dys, dhts))
        for j in range(8):
            dz_ref[:, 128 * j:128 * j + 128] = dzz[j].astype(dz_ref.dtype)
            dht_scr[:, 128 * j:128 * j + 128] = dht[j]
            dnw_ref[0:1, 128 * j:128 * j + 128] += dnwl[j]
        shared["dsm_ssd"] = dsm
        ddtb_ref[0:1, :] += ddtb
        dalog_ref[0:1, :] += dalog
        ddpar_ref[0:1, :] += ddpar
        ranges = ([(128 * j, 128 * j + 128) for j in range(8)] + [(1024 + 128 * g, 1152 + 128 * g) for g in range(2)]
                  + [(1280 + 128 * g, 1408 + 128 * g) for g in range(2)])
        _conv_bwd(dxs + db + dc, ranges, dbuf, carry, xbc_ref, cw_ref, dxbc_ref, dcw_ref, dcb_ref, shared["first"])

    ns = nc // SUB_BWD
    rblk = lambda w: pl.BlockSpec((SUB_BWD * CHUNK, w), lambda i: (ns - 1 - i, 0))
    acc = lambda w: pl.BlockSpec((8, w), lambda i: (0, 0))
    f32 = jnp.float32
    return dict(
        body=body,
        in_kinds=["rows"] * 4 + ["state", "rows"] + ["full"] * 9, out_kinds=["rows", "rows"] + ["full"] * 6,
        in_specs=[rblk(1024), rblk(1536), rblk(1536), rblk(128),
                  pl.BlockSpec((SUB_BWD, 128, 1024), lambda i: (ns - 1 - i, 0, 0)), rblk(1024),
                  _full((CONV_K, 1536)), _full((1, 128)), _full((1, 128)), _full((1, 128)),
                  _full((1, 1024)), _full((64, 64)), _full((64, 128)), _full((64, 128)),
                  _full((1, 128))],
        out_specs=[rblk(1024), rblk(1536), acc(1536), acc(1536), acc(128), acc(128), acc(128), acc(1024)],
        out_shape=[jax.ShapeDtypeStruct((t, 1024), f32), jax.ShapeDtypeStruct((t, 1536), f32),
                   jax.ShapeDtypeStruct((8, 1536), f32),
                   jax.ShapeDtypeStruct((8, 1536), f32), jax.ShapeDtypeStruct((8, 128), f32),
                   jax.ShapeDtypeStruct((8, 128), f32), jax.ShapeDtypeStruct((8, 128), f32),
                   jax.ShapeDtypeStruct((8, 1024), f32)],
        scratch=[pltpu.VMEM((72, 1536), f32), pltpu.VMEM((8, 1536), f32), pltpu.VMEM((128, 1024), f32)],
        args=[z, xbc, pre, sm, hs, dy, conv_w, dtb, alog, dpar, nw, cs["tri"], cs["i2"], cs["mask2"], cs["lo"]])


def _gdn_split(pre_fn, gate_ref):
    def heads(base):
        return jnp.stack([pre_fn(base + 128 * h, base + 128 * h + 128) for h in range(GDN_HEADS)])
    gate = jnp.stack([gate_ref[:, 128 * h:128 * h + 128] for h in range(GDN_HEADS)])
    return heads(0), heads(1024), heads(2048), gate


def gdn_fwd(gate, qkv, sm, conv_w, dtb, alog, nw, cs):
    t = gate.shape[0]
    nc = t // CHUNK

    def body(shared, gate_ref, qkv_ref, halo_ref, sm_ref, cw_ref, dtb_ref, alog_ref, nw_ref,
             tri_ref, i64_ref, strict_ref, o_ref, ss_ref, ts_ref, pre_ref, pbuf, s_scr):
        def init():
            s_scr[...] = jnp.zeros_like(s_scr)

        _when_first(shared, init)
        pbuf[0:8, :] = jnp.where(shared["first"], 0.0, halo_ref[...])
        pbuf[8:72, :] = qkv_ref[...]

        def pre_fn(c0, c1):
            pre = _conv_fwd(pbuf, cw_ref, c0, c1)
            pre_ref[:, c0:c1] = pre
            return pre

        q_pre, k_pre, v_pre, g3 = _gdn_split(pre_fn, gate_ref)
        s = s_scr[...]
        ss_ref[0] = s
        out, s_next, tinv = _gdn_chunk(q_pre, k_pre, v_pre, g3, sm_ref[...], s, dtb_ref[...], alog_ref[...],
                                       nw_ref[...], tri_ref[...], i64_ref[...], strict_ref[...])
        ts_ref[0] = tinv
        s_scr[...] = s_next
        for h in range(GDN_HEADS):
            o_ref[:, 128 * h:128 * h + 128] = out[h].astype(o_ref.dtype)

    blk = lambda w: pl.BlockSpec((SUB_FWD * CHUNK, w), lambda i: (i, 0))
    return dict(
        body=body,
        in_kinds=["rows", "rows", ("halo", 1), "rows"] + ["full"] * 7, out_kinds=["rows", "state", "state", "rows"],
        in_specs=[blk(1024), blk(3072), _halo_spec(3072, lambda i: i), blk(128),
                  _full((CONV_K, 3072)), _full((1, 128)), _full((1, 128)), _full((1, 128)),
                  _full((64, 64)), _full((64, 64)), _full((64, 64))],
        out_specs=[blk(1024), pl.BlockSpec((SUB_FWD, 8, 128, 128), lambda i: (i, 0, 0, 0)),
                   pl.BlockSpec((SUB_FWD, 8, CHUNK, CHUNK), lambda i: (i, 0, 0, 0)), blk(3072)],
        out_shape=[jax.ShapeDtypeStruct((t, 1024), _MM), jax.ShapeDtypeStruct((nc, 8, 128, 128), jnp.float32),
                   jax.ShapeDtypeStruct((nc, 8, CHUNK, CHUNK), jnp.float32),
                   jax.ShapeDtypeStruct((t, 3072), jnp.float32)],
        scratch=[pltpu.VMEM((72, 3072), jnp.float32), pltpu.VMEM((8, 128, 128), jnp.float32)],
        args=[gate, qkv, qkv, sm, conv_w, dtb, alog, nw, cs["tri"], cs["i64"], cs["strict"]])


def gdn_bwd(gate, qkv, pre, sm, ss, ts, do, conv_w, dtb, alog, nw, cs):
    t = gate.shape[0]
    nc = t // CHUNK

    def body(shared, gate_ref, qkv_ref, pre_ref, sm_ref, ss_ref, ts_ref, do_ref, cw_ref, dtb_ref, alog_ref,
             nw_ref, tri_ref, i64_ref, strict_ref,
             dgate_ref, dqkv_ref, dsm_ref, dcw_ref, ddtb_ref, dalog_ref, dnw_ref,
             dbuf, carry, ds_scr):
        def init():
            ds_scr[...] = jnp.zeros_like(ds_scr)
            dcw_ref[...] = jnp.zeros_like(dcw_ref)
            ddtb_ref[...] = jnp.zeros_like(ddtb_ref)
            dalog_ref[...] = jnp.zeros_like(dalog_ref)
            dnw_ref[...] = jnp.zeros_like(dnw_ref)

        _when_first(shared, init)

        q_pre, k_pre, v_pre, g3 = _gdn_split(lambda c0, c1: pre_ref[:, c0:c1], gate_ref)
        consts = (tri_ref[...], i64_ref[...], strict_ref[...], ts_ref[0])

        def f(q_pre, k_pre, v_pre, g3, smv, s, dtb, alog, nwv):
            return _gdn_chunk(q_pre, k_pre, v_pre, g3, smv, s, dtb, alog, nwv, *consts)[:2]

        _, vjp = jax.vjp(f, q_pre, k_pre, v_pre, g3, sm_ref[...], ss_ref[0], dtb_ref[...], alog_ref[...], nw_ref[...])
        do3 = jnp.stack([do_ref[:, 128 * h:128 * h + 128] for h in range(GDN_HEADS)])
        dq, dk, dv, dg3, dsm, ds, ddtb, dalog, dnw = vjp((do3, ds_scr[...]))
        ds_scr[...] = ds
        for h in range(GDN_HEADS):
            dgate_ref[:, 128 * h:128 * h + 128] = dg3[h].astype(dgate_ref.dtype)
        dsm_ref[...] = (dsm + shared["dsm_ssd"]).astype(dsm_ref.dtype)
        ddtb_ref[0:1, :] += ddtb
        dalog_ref[0:1, :] += dalog
        dnw_ref[0:1, :] += dnw
        ranges = [(base + 128 * h, base + 128 * h + 128) for base in (0, 1024, 2048) for h in range(GDN_HEADS)]
        dlist = [d[h] for d in (dq, dk, dv) for h in range(GDN_HEADS)]
        _conv_bwd(dlist, ranges, dbuf, carry, qkv_ref, cw_ref, dqkv_ref, dcw_ref, None, shared["first"])

    ns = nc // SUB_BWD
    rblk = lambda w: pl.BlockSpec((SUB_BWD * CHUNK, w), lambda i: (ns - 1 - i, 0))
    acc = lambda w: pl.BlockSpec((8, w), lambda i: (0, 0))
    f32 = jnp.float32
    return dict(
        body=body,
        in_kinds=["rows"] * 4 + ["state", "state", "rows"] + ["full"] * 7, out_kinds=["rows"] * 3 + ["full"] * 4,
        in_specs=[rblk(1024), rblk(3072), rblk(3072), rblk(128),
                  pl.BlockSpec((SUB_BWD, 8, 128, 128), lambda i: (ns - 1 - i, 0, 0, 0)),
                  pl.BlockSpec((SUB_BWD, 8, CHUNK, CHUNK), lambda i: (ns - 1 - i, 0, 0, 0)), rblk(1024),
                  _full((CONV_K, 3072)), _full((1, 128)), _full((1, 128)), _full((1, 128)),
                  _full((64, 64)), _full((64, 64)), _full((64, 64))],
        out_specs=[rblk(1024), rblk(3072), rblk(128), acc(3072), acc(128), acc(128), acc(128)],
        out_shape=[jax.ShapeDtypeStruct((t, 1024), f32), jax.ShapeDtypeStruct((t, 3072), f32),
                   jax.ShapeDtypeStruct((t, 128), f32), jax.ShapeDtypeStruct((8, 3072), f32),
                   jax.ShapeDtypeStruct((8, 128), f32), jax.ShapeDtypeStruct((8, 128), f32),
                   jax.ShapeDtypeStruct((8, 128), f32)],
        scratch=[pltpu.VMEM((72, 3072), f32), pltpu.VMEM((8, 3072), f32), pltpu.VMEM((8, 128, 128), f32)],
        args=[gate, qkv, pre, sm, ss, ts, do, conv_w, dtb, alog, nw, cs["tri"], cs["i64"], cs["strict"]])


def _chunk_call(parts, name, nc, reverse):
    n_in = [len(p["args"]) for p in parts]
    n_out = [len(p["out_shape"]) for p in parts]
    n_scr = [len(p["scratch"]) for p in parts]
    sub = SUB_BWD if reverse else SUB_FWD
    order = list(range(sub))[::-1] if reverse else list(range(sub))

    def view(ref, kind, s, refs):
        if kind == "rows":
            return ref.at[pl.ds(CHUNK * s, CHUNK)]
        if kind == "state":
            return ref.at[pl.ds(s, 1)]
        if kind == "full":
            return ref
        src = refs[kind[1]]
        return ref if s == 0 else src.at[pl.ds(CHUNK * s - 8, 8)]

    def body(*refs):
        ins, outs, scr = refs[:sum(n_in)], refs[sum(n_in):sum(n_in) + sum(n_out)], refs[sum(n_in) + sum(n_out):]
        for s in order:
            shared = {"first": (pl.program_id(0) == 0) if s == order[0] else False}
            for k, p in enumerate(parts):
                i0, o0, s0 = sum(n_in[:k]), sum(n_out[:k]), sum(n_scr[:k])
                p_ins = ins[i0:i0 + n_in[k]]
                p["body"](shared,
                          *[view(r, kd, s, p_ins) for r, kd in zip(p_ins, p["in_kinds"])],
                          *[view(r, kd, s, None) for r, kd in zip(outs[o0:o0 + n_out[k]], p["out_kinds"])],
                          *scr[s0:s0 + n_scr[k]])

    cat = lambda key: [v for p in parts for v in p[key]]
    return _pc(body, name=name, grid=(nc // sub,), in_specs=cat("in_specs"), out_specs=cat("out_specs"),
               out_shape=cat("out_shape"), scratch_shapes=cat("scratch"),
               compiler_params=_cparams(("arbitrary",)))(*cat("args"))


def out_fwd_bwd(x, tgt, y_ssd, y_gdn, w_out, fnw):
    t = x.shape[0]
    tm = min(512, t)
    f32 = jnp.float32

    def body(x_ref, tgt_ref, ys_ref, yg_ref, w_ref, fnw_ref,
             dout_ref, dys_ref, dyg_ref, gw_ref, gfnw_ref, loss_ref, gw_acc):
        i = pl.program_id(0)

        @pl.when(i == 0)
        def _():
            gw_acc[...] = jnp.zeros_like(gw_acc)
            gfnw_ref[...] = jnp.zeros_like(gfnw_ref)
            loss_ref[...] = jnp.zeros_like(loss_ref)

        ys = ys_ref[...]
        yg = yg_ref[...]
        out = x_ref[...] + jnp.dot(ys, w_ref[0:1024, :], preferred_element_type=f32) \
            + jnp.dot(yg, w_ref[1024:2048, :], preferred_element_type=f32)
        rstd = lax.rsqrt(jnp.mean(out * out, axis=-1, keepdims=True) + EPS)
        yhat = out * rstd
        fw = fnw_ref[...]
        e = yhat * fw - tgt_ref[...]
        loss_ref[...] += 0.5 * jnp.sum(jnp.sum(e * e, axis=-1, keepdims=True) * (1.0 / D_MODEL), axis=0, keepdims=True)
        dyf = e * (1.0 / D_MODEL)
        gfnw_ref[0:1, :] += jnp.sum(dyf * yhat, axis=0, keepdims=True)
        dyhat = dyf * fw
        dout = rstd * (dyhat - yhat * jnp.mean(dyhat * yhat, axis=-1, keepdims=True))
        dout_ref[...] = dout
        db = dout.astype(_MM)
        dys_ref[...] = lax.dot_general(db, w_ref[0:1024, :], (((1,), (1,)), ((), ())), preferred_element_type=f32)
        dyg_ref[...] = lax.dot_general(db, w_ref[1024:2048, :], (((1,), (1,)), ((), ())), preferred_element_type=f32)
        gw_acc[0:1024, :] += lax.dot_general(ys, db, (((0,), (0,)), ((), ())), preferred_element_type=f32)
        gw_acc[1024:2048, :] += lax.dot_general(yg, db, (((0,), (0,)), ((), ())), preferred_element_type=f32)

        @pl.when(i == steps - 1)
        def _():
            gw_ref[...] = gw_acc[...].astype(gw_ref.dtype)

    steps = t // tm
    blk = pl.BlockSpec((tm, D_MODEL), lambda i: (i, 0))
    return _pc(
        body, name="out_fwd_bwd", grid=(steps,),
        in_specs=[blk, blk, blk, blk, _full((MIX_WIDTH, D_MODEL)), _full((1, D_MODEL))],
        out_specs=[blk, blk, blk, _full((MIX_WIDTH, D_MODEL)), _full((8, D_MODEL)), _full((1, 128))],
        out_shape=[jax.ShapeDtypeStruct((t, D_MODEL), f32)] * 3 +
                  [jax.ShapeDtypeStruct((MIX_WIDTH, D_MODEL), _MM), jax.ShapeDtypeStruct((8, D_MODEL), f32),
                   jax.ShapeDtypeStruct((1, 128), f32)],
        scratch_shapes=[pltpu.VMEM((MIX_WIDTH, D_MODEL), f32)],
        compiler_params=_cparams(("arbitrary",)),
    )(x, tgt, y_ssd, y_gdn, w_out, fnw)


def inproj_bwd_dx(x, dout, norm_w, w_perm, dgroups, scattered):
    t = x.shape[0]
    tm = min(256, t)
    f32 = jnp.float32

    def body(x_ref, dout_ref, nw_ref, w_ref, dz_ref, dxbc_ref, dgate_ref, dqkv_ref, dsm_ref, dx_ref, gnw_ref):
        i = pl.program_id(0)

        @pl.when(i == 0)
        def _():
            gnw_ref[...] = jnp.zeros_like(gnw_ref)

        du = None
        for (name, c0, c1), d_ref in zip(GROUPS, (dz_ref, dxbc_ref, dgate_ref, dqkv_ref, dsm_ref)):
            term = jnp.dot(d_ref[...].astype(_MM), _w_rows(w_ref, name, c1 - c0), preferred_element_type=f32)
            du = term if du is None else du + term
        xf = x_ref[...]
        rstd = lax.rsqrt(jnp.mean(xf * xf, axis=-1, keepdims=True) + EPS)
        xhat = xf * rstd
        gnw_ref[0:1, :] += jnp.sum(du * xhat, axis=0, keepdims=True)
        dxh = du * nw_ref[...]
        dx_ref[...] = dout_ref[...] + rstd * (dxh - xhat * jnp.mean(dxh * xhat, axis=-1, keepdims=True))

    blk = lambda w: pl.BlockSpec((tm, w), lambda i: (i, 0))
    steps = t // tm
    kinds = ["scatter"] * len(scattered)
    hosted = _hosting(body, 9, 2, 0, kinds, lambda: pl.program_id(0) == 0, lambda: pl.program_id(0) == steps - 1)
    return _pc_comm(
        hosted, name="inproj_bwd_dx", grid=(steps,),
        in_specs=[blk(D_MODEL), blk(D_MODEL), _full((1, D_MODEL)), _full((W_GATHERED, D_MODEL))] +
                 [blk(c1 - c0) for _, c0, c1 in GROUPS] + [ANY] * len(scattered),
        out_specs=[blk(D_MODEL), _full((8, D_MODEL))] + [ANY] * len(scattered),
        out_shape=[jax.ShapeDtypeStruct((t, D_MODEL), f32), jax.ShapeDtypeStruct((8, D_MODEL), f32)] +
                  [_exchange_out_shape("scatter", a) for a in scattered],
        scratch_shapes=_exchange_sems(len(scattered)), compiler_params=_cparams(("arbitrary",)),
    )(x, dout, norm_w, w_perm, *dgroups, *scattered)


def grad_w_group(u, dg, name, scattered=()):
    t, n = dg.shape
    tn = n if n <= 1536 else 1024
    budget = 40 * 1024 * 1024
    tm = next((c for c in (4096, 2048, 1024, 512, 256)
               if t % c == 0 and tn * D_MODEL * 4 + 2 * (c * tn * 4 + c * D_MODEL * 2 + tn * D_MODEL * 2) <= budget), t)
    nj, nk = n // tn, t // tm
    f32 = jnp.float32

    def body(u_ref, d_ref, o_ref, acc):
        k = pl.program_id(1)

        @pl.when(k == 0)
        def _():
            acc[...] = jnp.zeros_like(acc)

        acc[...] += lax.dot_general(d_ref[...].astype(_MM), u_ref[...], (((0,), (0,)), ((), ())),
                                    preferred_element_type=f32)

        @pl.when(k == nk - 1)
        def _():
            o_ref[...] = acc[...].astype(o_ref.dtype)

    ne = len(scattered)
    hosted = _hosting(body, 2, 1, 1, ["scatter"] * ne,
                      lambda: (pl.program_id(0) == 0) & (pl.program_id(1) == 0),
                      lambda: (pl.program_id(0) == nj - 1) & (pl.program_id(1) == nk - 1))
    res = (_pc_comm if ne else _pc)(
        hosted, name=name, grid=(nj, nk),
        in_specs=[pl.BlockSpec((tm, D_MODEL), lambda j, k: (k, 0)),
                  pl.BlockSpec((tm, tn), lambda j, k: (k, j))] + [ANY] * ne,
        out_specs=[pl.BlockSpec((tn, D_MODEL), lambda j, k: (j, 0))] + [ANY] * ne,
        out_shape=[jax.ShapeDtypeStruct((n, D_MODEL), _MM)] + [_exchange_out_shape("scatter", a) for a in scattered],
        scratch_shapes=[pltpu.VMEM((tn, D_MODEL), f32)] + _exchange_sems(ne),
        compiler_params=_cparams(("arbitrary", "arbitrary")),
    )(u, dg, *scattered)
    return res if ne else res[0]


def grad_w_many(u, dgs, name):
    t = u.shape[0]
    widths = [d.shape[1] for d in dgs]
    tot, ng = sum(widths), len(dgs)
    f32 = jnp.float32
    budget = 48 * 1024 * 1024
    tm = next((c for c in (2048, 1024, 512, 256)
               if t % c == 0 and tot * D_MODEL * 4 + 2 * (c * tot * 4 + c * D_MODEL * 2 + tot * D_MODEL * 2) <= budget), t)
    nk = t // tm

    def body(*refs):
        u_ref, d_refs, o_refs, accs = refs[0], refs[1:1 + ng], refs[1 + ng:1 + 2 * ng], refs[1 + 2 * ng:]
        k = pl.program_id(0)

        @pl.when(k == 0)
        def _():
            for acc in accs:
                acc[...] = jnp.zeros_like(acc)

        uu = u_ref[...]
        for d_ref, acc in zip(d_refs, accs):
            acc[...] += lax.dot_general(d_ref[...].astype(_MM), uu, (((0,), (0,)), ((), ())),
                                        preferred_element_type=f32)

        @pl.when(k == nk - 1)
        def _():
            for o_ref, acc in zip(o_refs, accs):
                o_ref[...] = acc[...].astype(o_ref.dtype)

    return _pc(
        body, name=name, grid=(nk,),
        in_specs=[pl.BlockSpec((tm, D_MODEL), lambda k: (k, 0))] + [pl.BlockSpec((tm, n), lambda k: (k, 0)) for n in widths],
        out_specs=[pl.BlockSpec((n, D_MODEL), lambda k: (0, 0)) for n in widths],
        out_shape=[jax.ShapeDtypeStruct((n, D_MODEL), _MM) for n in widths],
        scratch_shapes=[pltpu.VMEM((n, D_MODEL), f32) for n in widths],
        compiler_params=_cparams(("arbitrary",)),
    )(u, *dgs)


def _pad_lanes(v, off):
    n = v.shape[-1]
    return jnp.pad(v.reshape(1, n).astype(jnp.float32), ((0, 0), (off, 128 - off - n)))


REF_ROWS = dict(z=(0, 1024), xbc=(1024, 2560), dt=(2560, 2576), gate=(2576, 3600), qkv=(3600, 6672), ab=(6672, 6688))


def unperm_w_in(gz, gxbc, ggate, gqkv, gsm):
    src = dict(z=gz, xbc=gxbc, dt=gsm[0:16], gate=ggate, qkv=gqkv, ab=gsm[16:32])
    slabs = []
    for k in range(N_DEV):
        a, b = k * W_IN_SHARD, (k + 1) * W_IN_SHARD
        parts = []
        for name, (s, e) in REF_ROWS.items():
            lo, hi = max(a, s), min(b, e)
            if lo < hi:
                parts.append(src[name][lo - s:hi - s])
        slabs.append(jnp.concatenate(parts, axis=0))
    return jnp.stack(slabs)


def all_gather(arrs, name):
    n = len(arrs)
    halves = []
    for arr in arrs:
        h = arr.shape[0] // 2 // 16 * 16
        halves.append(((0, h), (h, arr.shape[0] - h)))

    def body(*refs):
        ins, outs = refs[:n], refs[n:2 * n]
        send_sems, recv_sems, local_sems = refs[2 * n:]
        x, y, c, me = _me()
        sibling = (x, y, 1 - c)
        xn, yn, dg = (1 - x, y), (x, 1 - y), (1 - x, 1 - y)

        def idx(px, py, pc):
            return 4 * px + 2 * py + pc

        def copy(a, k, block, to, src=None, half=None):
            slot = outs[a].at[idx(*block)]
            if half is not None:
                slot = slot.at[pl.ds(*halves[a][half])]
                src = src if src is None else src.at[pl.ds(*halves[a][half])]
            return pltpu.make_async_remote_copy(src_ref=slot if src is None else src, dst_ref=slot,
                                                send_sem=send_sems.at[a, k], recv_sem=recv_sems.at[a, k],
                                                device_id=to, device_id_type=MESH)

        local = [pltpu.make_async_copy(ins[a], outs[a].at[me], local_sems.at[a]) for a in range(n)]
        for cp in local:
            cp.start()
        started = []

        def start(cps):
            for cp in cps:
                cp.start()
            started.extend(cps)

        for a in range(n):
            start([copy(a, 1, (x, y, c), (*xn, c), src=ins[a], half=0),
                   copy(a, 2, (x, y, c), (*yn, c), src=ins[a], half=1),
                   copy(a, 8, (x, y, c), (*xn, c), src=ins[a], half=1),
                   copy(a, 9, (x, y, c), (*yn, c), src=ins[a], half=0),
                   copy(a, 0, (x, y, c), sibling, src=ins[a])])
        for a in range(n):
            copy(a, 1, (*xn, c), (x, y, c), half=0).wait_recv()
            start([copy(a, 3, (*xn, c), (*yn, c), half=0)])
            copy(a, 2, (*yn, c), (x, y, c), half=1).wait_recv()
            start([copy(a, 4, (*yn, c), (*xn, c), half=1)])
        for a in range(n):
            copy(a, 8, (*xn, c), (x, y, c), half=1).wait_recv()
            start([copy(a, 5, (*xn, c), sibling)])
            copy(a, 9, (*yn, c), (x, y, c), half=0).wait_recv()
            start([copy(a, 6, (*yn, c), sibling)])
        for a in range(n):
            copy(a, 3, (*dg, c), (x, y, c), half=0).wait_recv()
            copy(a, 4, (*dg, c), (x, y, c), half=1).wait_recv()
            start([copy(a, 7, (*dg, c), sibling)])
        for a in range(n):
            copy(a, 0, sibling, (x, y, c)).wait_recv()
            for j, chip in enumerate((xn, yn, dg)):
                copy(a, 5 + j, (*chip, 1 - c), (x, y, c)).wait_recv()
        for cp in started:
            cp.wait_send()
        for cp in local:
            cp.wait()

    return _pc_comm(
        body, name=name, in_specs=[ANY] * n, out_specs=[ANY] * n,
        out_shape=[jax.ShapeDtypeStruct((N_DEV,) + a.shape, a.dtype) for a in arrs],
        scratch_shapes=[pltpu.SemaphoreType.DMA((n, 10)), pltpu.SemaphoreType.DMA((n, 10)),
                        pltpu.SemaphoreType.DMA((n,))],
    )(*arrs)


def adamw_sum(recv, w, m, v, rows, name, cols=None, small=()):
    r, ccols = w.shape
    f32 = jnp.float32
    c1 = 1.0 / (1.0 - ADAM_B1 ** ADAM_STEP)
    c2 = 1.0 / (1.0 - ADAM_B2 ** ADAM_STEP)
    ns = len(small)
    steps = r // rows if cols is None else ccols // cols

    def body(*refs):
        recv_ref, w_ref, m_ref, v_ref = refs[:4]
        g_ref, d_ref, mo_ref, vo_ref = refs[4 + ns:8 + ns]
        if ns:
            small_out, slabs = refs[8 + ns], refs[9 + ns]
            start, wait = _exchange_ops("scatter", slabs, small_out, *refs[10 + ns:])

            @pl.when(pl.program_id(0) == 0)
            def _():
                _fill_small_slabs(slabs, refs[4:4 + ns])
                start()

        g = recv_ref[0].astype(f32)
        for k in range(1, N_DEV):
            g = g + recv_ref[k].astype(f32)
        mn = ADAM_B1 * m_ref[...] + (1.0 - ADAM_B1) * g
        vn = ADAM_B2 * v_ref[...] + (1.0 - ADAM_B2) * (g * g)
        g_ref[...] = g
        mo_ref[...] = mn
        vo_ref[...] = vn
        d_ref[...] = -ADAM_LR * ((mn * c1) / (jnp.sqrt(vn * c2) + ADAM_EPS) + ADAM_WD * w_ref[...])

        if ns:
            pl.when(pl.program_id(0) == steps - 1)(wait)

    if cols is None:
        blk = pl.BlockSpec((rows, ccols), lambda i: (i, 0))
        rblk = pl.BlockSpec((N_DEV, rows, ccols), lambda i: (0, i, 0))
    else:
        blk = pl.BlockSpec((r, cols), lambda i: (0, i))
        rblk = pl.BlockSpec((N_DEV, r, cols), lambda i: (0, 0, i))
    total = _small_layout()[1]
    return (_pc_comm if ns else _pc)(
        body, name=name, grid=(steps,),
        in_specs=[rblk, blk, blk, blk] + [_full(a.shape) for a in small],
        out_specs=[blk] * 4 + [ANY] * bool(ns),
        out_shape=[jax.ShapeDtypeStruct((r, ccols), f32)] * 4 +
                  [jax.ShapeDtypeStruct((N_DEV, 1, total), f32)] * bool(ns),
        scratch_shapes=([pltpu.VMEM((N_DEV, 1, total), f32)] + _exchange_sems(1)) * bool(ns),
        compiler_params=_cparams(("arbitrary",)),
    )(recv, w, m, v, *small)


SMALL = (("norm_w", 1, 1024, 0), ("ssd_conv_b", 1, 1536, 0), ("ssd_dt_bias", 1, 16, 0), ("ssd_a_log", 1, 16, 0),
         ("ssd_d", 1, 16, 0), ("ssd_norm_w", 1, 1024, 0), ("gdn_dt_bias", 1, 8, 16), ("gdn_a_log", 1, 8, 16),
         ("gdn_norm_w", 1, 128, 0), ("final_norm_w", 1, 1024, 0),
         ("ssd_conv_w", CONV_K, SSD_CONV_DIM // N_DEV, 0), ("gdn_conv_w", CONV_K, GDN_CONV_DIM // N_DEV, 0))


def _small_layout():
    out, off = [], 0
    for name, rows, n, lane0 in SMALL + (("loss", 1, 128, 0),):
        stride = -(-(lane0 + n) // 128) * 128
        out.append((name, rows, n, lane0, stride, off))
        off += rows * stride
    return out, off


def _fill_small_slabs(slabs, acc_refs):
    layout, _ = _small_layout()
    slabs[...] = jnp.zeros_like(slabs)
    for (name, rows, n, lane0, stride, off), acc in zip(layout, acc_refs):
        for k in range(N_DEV):
            if rows == 1:
                slabs[k, :, off:off + stride] = acc[0:1, 0:stride]
            else:
                for j in range(rows):
                    slabs[k, :, off + stride * j:off + stride * j + n] = acc[j:j + 1, n * k:n * k + n]


def adamw_small(recv, w, m, v):
    layout, total = _small_layout()
    loss_off = layout[-1][5]
    layout = layout[:-1]
    f32 = jnp.float32
    c1 = 1.0 / (1.0 - ADAM_B1 ** ADAM_STEP)
    c2 = 1.0 / (1.0 - ADAM_B2 ** ADAM_STEP)
    np_ = len(layout)

    def body(*refs):
        recv_ref = refs[0]
        w_refs, m_refs, v_refs = refs[1:1 + np_], refs[1 + np_:1 + 2 * np_], refs[1 + 2 * np_:1 + 3 * np_]
        o_refs = refs[1 + 3 * np_:]
        g_all = recv_ref[0]
        for k in range(1, N_DEV):
            g_all = g_all + recv_ref[k]
        o_refs[4 * np_][...] = g_all[:, loss_off:loss_off + 128]

        def update(g, wv, mv, vv):
            mn = ADAM_B1 * mv + (1.0 - ADAM_B1) * g
            vn = ADAM_B2 * vv + (1.0 - ADAM_B2) * (g * g)
            return g, -ADAM_LR * ((mn * c1) / (jnp.sqrt(vn * c2) + ADAM_EPS) + ADAM_WD * wv), mn, vn

        for p, (name, rows, n, lane0, stride, off) in enumerate(layout):
            outs = o_refs[4 * p:4 * p + 4]
            if rows == 1:
                res = update(g_all[:, off + lane0:off + lane0 + n], w_refs[p][...], m_refs[p][...], v_refs[p][...])
                for o, r in zip(outs, res):
                    o[...] = r
            else:
                for j in range(rows):
                    res = update(g_all[:, off + stride * j:off + stride * j + n], w_refs[p][0, j:j + 1, :],
                                 m_refs[p][0, j:j + 1, :], v_refs[p][0, j:j + 1, :])
                    for o, r in zip(outs, res):
                        o[0, j:j + 1, :] = r

    names = [e[0] for e in layout]
    ins = [recv] + [d[nm] for d in (w, m, v) for nm in names]
    out_shape = [jax.ShapeDtypeStruct(w[nm].shape, f32) for nm in names for _ in range(4)]
    out_shape.append(jax.ShapeDtypeStruct((1, 128), f32))
    res = _pc(body, name="adamw_small", out_shape=out_shape)(*ins)
    return {nm: tuple(res[4 * p:4 * p + 4]) for p, nm in enumerate(names)}, res[4 * np_]


SHARD = (("ssd_conv_w", CONV_K * SSD_CONV_DIM // N_DEV), ("gdn_conv_w", CONV_K * GDN_CONV_DIM // N_DEV))
SHARD_ROWS = 24


def _rows_of(size):
    return -(-size // 128)


def _pack(vals, layout, total_rows):
    parts = []
    for (name, size), val in zip(layout, vals):
        flat = val.reshape(-1).astype(jnp.float32)
        parts.append(jnp.pad(flat, (0, _rows_of(size) * 128 - size)).reshape(-1, 128))
    used = sum(_rows_of(s) for _, s in layout)
    parts.append(jnp.zeros((total_rows - used, 128), jnp.float32))
    return jnp.concatenate(parts, axis=0)


def _conv_full(gathered_flat, ccols):
    return gathered_flat.reshape(N_DEV, CONV_K, ccols // N_DEV).transpose(1, 0, 2).reshape(CONV_K, ccols)


def kernel(x, norm_w, w_in, ssd_conv_w, ssd_conv_b, ssd_dt_bias, ssd_a_log, ssd_d, ssd_norm_w, gdn_conv_w, gdn_dt_bias, gdn_a_log, gdn_norm_w, w_out, final_norm_w, loss_target, m_norm_w, m_w_in, m_ssd_conv_w, m_ssd_conv_b, m_ssd_dt_bias, m_ssd_a_log, m_ssd_d, m_ssd_norm_w, m_gdn_conv_w, m_gdn_dt_bias, m_gdn_a_log, m_gdn_norm_w, m_w_out, m_final_norm_w, v_norm_w, v_w_in, v_ssd_conv_w, v_ssd_conv_b, v_ssd_dt_bias, v_ssd_a_log, v_ssd_d, v_ssd_norm_w, v_gdn_conv_w, v_gdn_dt_bias, v_gdn_a_log, v_gdn_norm_w, v_w_out, v_final_norm_w):
    f32 = jnp.float32
    w = dict(norm_w=norm_w, w_in=w_in, ssd_conv_w=ssd_conv_w, ssd_conv_b=ssd_conv_b, ssd_dt_bias=ssd_dt_bias,
             ssd_a_log=ssd_a_log, ssd_d=ssd_d, ssd_norm_w=ssd_norm_w, gdn_conv_w=gdn_conv_w, gdn_dt_bias=gdn_dt_bias,
             gdn_a_log=gdn_a_log, gdn_norm_w=gdn_norm_w, w_out=w_out, final_norm_w=final_norm_w)
    m = dict(norm_w=m_norm_w, w_in=m_w_in, ssd_conv_w=m_ssd_conv_w, ssd_conv_b=m_ssd_conv_b, ssd_dt_bias=m_ssd_dt_bias,
             ssd_a_log=m_ssd_a_log, ssd_d=m_ssd_d, ssd_norm_w=m_ssd_norm_w, gdn_conv_w=m_gdn_conv_w,
             gdn_dt_bias=m_gdn_dt_bias, gdn_a_log=m_gdn_a_log, gdn_norm_w=m_gdn_norm_w, w_out=m_w_out,
             final_norm_w=m_final_norm_w)
    v = dict(norm_w=v_norm_w, w_in=v_w_in, ssd_conv_w=v_ssd_conv_w, ssd_conv_b=v_ssd_conv_b, ssd_dt_bias=v_ssd_dt_bias,
             ssd_a_log=v_ssd_a_log, ssd_d=v_ssd_d, ssd_norm_w=v_ssd_norm_w, gdn_conv_w=v_gdn_conv_w,
             gdn_dt_bias=v_gdn_dt_bias, gdn_a_log=v_gdn_a_log, gdn_norm_w=v_gdn_norm_w, w_out=v_w_out,
             final_norm_w=v_final_norm_w)
    names = list(w)
    shapes = {n: w[n].shape for n in names}

    xl, tgt = x[0], loss_target[0]
    cs = _consts()
    dtb_s = _pad_lanes(ssd_dt_bias, 0)
    alog_s = _pad_lanes(ssd_a_log, 0)
    dpar = _pad_lanes(ssd_d, 0)
    dtb_g = _pad_lanes(gdn_dt_bias, 16)
    alog_g = _pad_lanes(gdn_a_log, 16)
    nw_g = gdn_norm_w.reshape(1, 128)
    nw_s = ssd_norm_w.reshape(1, 1024)
    cb_s = ssd_conv_b.reshape(1, 1536)
    nw1 = norm_w.reshape(1, D_MODEL)

    w_slot = lax.dynamic_slice(jnp.pad(w_in[0].T.astype(_MM), ((12, 12), (0, 0))),
                               (12 - (W_IN_SHARD * _me()[3]) % 16, 0), (W_SLOT, D_MODEL))
    (g_w_in,) = all_gather([w_slot], "gather_w_in")
    w_perm = g_w_in.reshape(W_GATHERED, D_MODEL)
    conv_pack = _pack([w["ssd_conv_w"], w["gdn_conv_w"]], SHARD, SHARD_ROWS)
    u, z, xbc, gate, qkv, sm, g_w_out, g_conv = inproj_fwd(xl, nw1, w_perm, [w_out[0].astype(_MM), conv_pack])
    w_out_full = g_w_out.reshape(MIX_WIDTH, D_MODEL)
    ssd_cw = _conv_full(g_conv[:, 0:6].reshape(N_DEV, -1), SSD_CONV_DIM)
    gdn_cw = _conv_full(g_conv[:, 6:18].reshape(N_DEV, -1), GDN_CONV_DIM)

    nc = xl.shape[0] // CHUNK
    y_ssd, hs, pre_s, y_gdn, ss, ts, pre_g = _chunk_call(
        [ssd_fwd(z, xbc, sm, ssd_cw, cb_s, dtb_s, alog_s, dpar, nw_s, cs),
         gdn_fwd(gate, qkv, sm, gdn_cw, dtb_g, alog_g, nw_g, cs)], "scan_fwd", nc, False)
    dout, dys, dyg, g_wout, g_fnw, loss_l = out_fwd_bwd(xl, tgt, y_ssd, y_gdn, w_out_full,
                                                        final_norm_w.reshape(1, D_MODEL))
    (dz, dxbc, g_cw_s, g_cb_s, g_dtb_s, g_alog_s, g_d, g_nw_s,
     dgate, dqkv, dsm, g_cw_g, g_dtb_g, g_alog_g, g_nw_g) = _chunk_call(
        [ssd_bwd(z, xbc, pre_s, sm, hs, dys, ssd_cw, dtb_s, alog_s, dpar, nw_s, cs),
         gdn_bwd(gate, qkv, pre_g, sm, ss, ts, dyg, gdn_cw, dtb_g, alog_g, nw_g, cs)], "scan_bwd", nc, True)

    t_w_out = g_wout.reshape(N_DEV, MIX_WIDTH // N_DEV, D_MODEL)
    gws = dict(zip(("z", "sm"), grad_w_many(u, [dz, dsm], "grad_w_in_z_sm")))
    gws["xbc"] = grad_w_group(u, dxbc, "grad_w_in_xbc")
    gws["gate"] = grad_w_group(u, dgate, "grad_w_in_gate")
    gws["qkv"] = grad_w_group(u, dqkv, "grad_w_in_qkv")
    t_w_in = unperm_w_in(gws["z"], gws["xbc"], gws["gate"], gws["qkv"], gws["sm"])
    dx, g_nw, r_w_in, r_w_out = inproj_bwd_dx(xl, dout, nw1, w_perm, (dz, dxbc, dgate, dqkv, dsm),
                                               [t_w_in, t_w_out])

    accs = dict(norm_w=g_nw, ssd_conv_b=g_cb_s, ssd_dt_bias=g_dtb_s, ssd_a_log=g_alog_s, ssd_d=g_d,
                ssd_norm_w=g_nw_s, gdn_dt_bias=g_dtb_g, gdn_a_log=g_alog_g, gdn_norm_w=g_nw_g, final_norm_w=g_fnw,
                ssd_conv_w=g_cw_s, gdn_conv_w=g_cw_g)
    *o_w_in, r_small = adamw_sum(r_w_in, w_in[0].T, m_w_in[0].T, v_w_in[0].T, None, "adamw_w_in", cols=256,
                                 small=[accs[e[0]] for e in SMALL] + [loss_l])
    o_w_out = adamw_sum(r_w_out, w_out[0], m_w_out[0], v_w_out[0], 64, "adamw_w_out")
    row = lambda d: {n: (a.reshape(1, -1) if a.ndim == 1 else a) for n, a in d.items()}
    o_small, loss_sum = adamw_small(r_small, row(w), row(m), row(v))

    loss = loss_sum[0, 0]
    outs = [loss, dx[None]]
    for k in range(4):
        parts = {n: o_small[n][k] for n in o_small}
        parts["w_in"] = o_w_in[k].T
        parts["w_out"] = o_w_out[k]
        outs += [parts[n].reshape(shapes[n]) for n in names]
    return tuple(outs)
```
